```python
import jax, jax.numpy as jnp
from jax import lax
import numpy as np

D_MODEL = 2048
BATCH = 8
SEQ = 8192
DEPTH = 1

HEAD_DIM = 128
N_KV_HEADS = 4
DILATED_PATTERNS = ((128, 1), (512, 4), (2048, 16))
N_PATTERNS = len(DILATED_PATTERNS)
N_Q_HEADS = N_KV_HEADS * N_PATTERNS
Q_WIDTH = N_Q_HEADS * HEAD_DIM
KV_WIDTH = N_KV_HEADS * HEAD_DIM
ATTN_OUT_WIDTH = N_KV_HEADS * HEAD_DIM
ROT_DIMS = HEAD_DIM // 4
ROPE_THETA = 500000.0
LRU_WIDTH = D_MODEL - ATTN_OUT_WIDTH
LRU_BLOCK_WIDTH = 128
LRU_BLOCKS = LRU_WIDTH // LRU_BLOCK_WIDTH
CONV_WIDTH = 4
LRU_C = 8.0
IN_PROJ_WIDTH = Q_WIDTH + 2 * KV_WIDTH + 2 * LRU_WIDTH
D_FF = 256 * ((8 * D_MODEL // 3 + 255) // 256)
D_PLE = 256
LN_EPS = 1e-5
DEEPNORM_ALPHA = (2.0 * DEPTH) ** 0.25
DEEPNORM_BETA = (8.0 * DEPTH) ** -0.25

kernel_name = 'hybrid_rglru_dilated_attn_macaron_deepnorm'


def layer_norm(x, g, b):
    xf = x.astype(jnp.float32)
    mu = xf.mean(-1, keepdims=True)
    var = jnp.square(xf - mu).mean(-1, keepdims=True)
    y = (xf - mu) * lax.rsqrt(var + LN_EPS)
    return (y * g.astype(jnp.float32) + b.astype(jnp.float32)).astype(x.dtype)


def swiglu(x, w_gate, w_up, w_down):
    return (jax.nn.silu(x @ w_gate) * (x @ w_up)) @ w_down


def partial_rotary(t, positions):
    half = ROT_DIMS // 2
    inv_freq = jnp.power(jnp.float32(ROPE_THETA), -jnp.arange(half, dtype=jnp.float32) * (2.0 / ROT_DIMS))
    ang = positions.astype(jnp.float32)[:, None, :, None] * inv_freq
    cos = jnp.cos(ang).astype(t.dtype)
    sin = jnp.sin(ang).astype(t.dtype)
    t1 = t[..., :half]
    t2 = t[..., half:ROT_DIMS]
    return jnp.concatenate([t1 * cos - t2 * sin, t2 * cos + t1 * sin, t[..., ROT_DIMS:]], axis=-1)


def dilated_window_attention(q, k, v, window, dilation):
    b, h, s, dh = q.shape
    span = window // dilation
    blk = span
    sub_len = s // dilation
    pad = (-sub_len) % blk
    nb = (sub_len + pad) // blk

    def to_blocks(t):
        t = t.reshape(b, h, sub_len, dilation, dh).swapaxes(2, 3)
        t = jnp.pad(t, ((0, 0), (0, 0), (0, 0), (0, pad), (0, 0)))
        return t.reshape(b, h, dilation, nb, blk, dh)

    def with_prev(t):
        prev = jnp.pad(t, ((0, 0), (0, 0), (0, 0), (1, 0), (0, 0), (0, 0)))[:, :, :, :-1]
        return jnp.concatenate([prev, t], axis=4)

    qb = to_blocks(q)
    kw = with_prev(to_blocks(k))
    vw = with_prev(to_blocks(v))
    scores = jnp.einsum('bhrnqe,bhrnke->bhrnqk', qb, kw,
                        preferred_element_type=jnp.float32) * (dh ** -0.5)
    qi = jnp.arange(blk)[:, None]
    ki = jnp.arange(2 * blk)[None, :]
    dist = qi + blk - ki
    band = (dist >= 0) & (dist <= span)
    not_first = (jnp.arange(nb) > 0)[:, None, None]
    mask = band[None] & (not_first | (ki >= blk)[None])
    scores = jnp.where(mask, scores, -jnp.inf)
    m = scores.max(-1, keepdims=True)
    e = jnp.exp(scores - m)
    den = e.sum(-1, keepdims=True)
    out = jnp.einsum('bhrnqk,bhrnke->bhrnqe', e, vw.astype(jnp.float32)) / den
    lse = (m + jnp.log(den))[..., 0]
    out = out.reshape(b, h, dilation, nb * blk, dh)[:, :, :, :sub_len].swapaxes(2, 3).reshape(b, h, s, dh)
    lse = lse.reshape(b, h, dilation, nb * blk)[..., :sub_len].swapaxes(2, 3).reshape(b, h, s)
    return out, lse


def rg_lru_branch(xb, yb, conv_w, conv_b, w_rgate, b_rgate, w_igate, b_igate, lam):
    b, s, c = xb.shape
    xc = lax.conv_general_dilated(xb, conv_w[:, None, :], window_strides=(1,),
                                  padding=((CONV_WIDTH - 1, 0),),
                                  dimension_numbers=('NWC', 'WIO', 'NWC'),
                                  feature_group_count=c) + conv_b
    xh = xc.reshape(b, s, LRU_BLOCKS, LRU_BLOCK_WIDTH)
    r = jax.nn.sigmoid(jnp.einsum('bsgi,gij->bsgj', xh, w_rgate).reshape(b, s, c) + b_rgate)
    i = jax.nn.sigmoid(jnp.einsum('bsgi,gij->bsgj', xh, w_igate).reshape(b, s, c) + b_igate)
    log_a = -LRU_C * jax.nn.softplus(-lam.astype(jnp.float32)) * r.astype(jnp.float32)
    a = jnp.exp(log_a)
    u = jnp.sqrt(-jnp.expm1(2.0 * log_a)) * (i * xc).astype(jnp.float32)

    def combine(left, right):
        a1, b1 = left
        a2, b2 = right
        return a1 * a2, a2 * b1 + b2

    _, hseq = lax.associative_scan(combine, (a, u), axis=1)
    return hseq.astype(xb.dtype) * jax.nn.gelu(yb)


def hybrid_mixer(h, positions, w_in, conv_w, conv_b, w_rgate, b_rgate, w_igate, b_igate, lam, w_out):
    b, s, _ = h.shape
    proj = h @ w_in
    q, k, v, xb, yb = jnp.split(proj, [Q_WIDTH, Q_WIDTH + KV_WIDTH, Q_WIDTH + 2 * KV_WIDTH,
                                       Q_WIDTH + 2 * KV_WIDTH + LRU_WIDTH], axis=-1)
    q = q.reshape(b, s, N_PATTERNS, N_KV_HEADS, HEAD_DIM).transpose(2, 0, 3, 1, 4)
    k = k.reshape(b, s, N_KV_HEADS, HEAD_DIM).transpose(0, 2, 1, 3)
    v = v.reshape(b, s, N_KV_HEADS, HEAD_DIM).transpose(0, 2, 1, 3)
    q = partial_rotary(q, positions)
    k = partial_rotary(k, positions)
    outs = []
    lses = []
    for g, (window, dilation) in enumerate(DILATED_PATTERNS):
        o, l = dilated_window_attention(q[g], k, v, window, dilation)
        outs.append(o)
        lses.append(l)
    weights = jax.nn.softmax(jnp.stack(lses), axis=0)
    attn = jnp.einsum('gbhs,gbhse->bshe', weights, jnp.stack(outs))
    attn = attn.reshape(b, s, ATTN_OUT_WIDTH).astype(h.dtype)
    rec = rg_lru_branch(xb, yb, conv_w, conv_b, w_rgate, b_rgate, w_igate, b_igate, lam)
    return jnp.concatenate([attn, rec], axis=-1) @ w_out


def _fwd_setup_inputs(seed: int = 0) -> dict:
    key = jax.random.key(seed)
    ks = jax.random.split(key, 32)
    f32 = jnp.float32

    def nrm(k, shape, scale):
        return jax.random.normal(k, shape, f32) * scale

    x = jax.random.normal(ks[0], (BATCH, SEQ, D_MODEL), f32)
    p = jax.random.normal(ks[1], (DEPTH, BATCH, SEQ, D_PLE), f32)
    offset = jax.random.randint(ks[2], (BATCH, 1), 0, 1024, dtype=jnp.int32)
    positions = (jnp.arange(SEQ, dtype=jnp.int32)[None, :] + offset).astype(jnp.int32)
    a_pow = jax.random.uniform(ks[3], (DEPTH, LRU_WIDTH), f32, minval=0.9, maxval=0.999)
    a_base = a_pow ** (1.0 / LRU_C)
    lru_lambda = jnp.log(a_base) - jnp.log1p(-a_base)
    return {
        'x': x,
        'p': p,
        'positions': positions,
        'ffn1_w_gate': nrm(ks[4], (DEPTH, D_MODEL, D_FF), D_MODEL ** -0.5),
        'ffn1_w_up': nrm(ks[5], (DEPTH, D_MODEL, D_FF), D_MODEL ** -0.5),
        'ffn1_w_down': nrm(ks[6], (DEPTH, D_FF, D_MODEL), D_FF ** -0.5 * DEEPNORM_BETA),
        'ln1_g': 1.0 + nrm(ks[7], (DEPTH, D_MODEL), 0.02),
        'ln1_b': nrm(ks[8], (DEPTH, D_MODEL), 0.02),
        'w_in': nrm(ks[9], (DEPTH, D_MODEL, IN_PROJ_WIDTH), D_MODEL ** -0.5),
        'conv_w': nrm(ks[10], (DEPTH, CONV_WIDTH, LRU_WIDTH), CONV_WIDTH ** -0.5),
        'conv_b': nrm(ks[11], (DEPTH, LRU_WIDTH), 0.02),
        'w_rgate': nrm(ks[12], (DEPTH, LRU_BLOCKS, LRU_BLOCK_WIDTH, LRU_BLOCK_WIDTH), LRU_BLOCK_WIDTH ** -0.5),
        'b_rgate': nrm(ks[13], (DEPTH, LRU_WIDTH), 0.02),
        'w_igate': nrm(ks[14], (DEPTH, LRU_BLOCKS, LRU_BLOCK_WIDTH, LRU_BLOCK_WIDTH), LRU_BLOCK_WIDTH ** -0.5),
        'b_igate': nrm(ks[15], (DEPTH, LRU_WIDTH), 0.02),
        'lru_lambda': lru_lambda,
        'w_out': nrm(ks[16], (DEPTH, D_MODEL, D_MODEL), D_MODEL ** -0.5 * DEEPNORM_BETA),
        'ln2_g': 1.0 + nrm(ks[17], (DEPTH, D_MODEL), 0.02),
        'ln2_b': nrm(ks[18], (DEPTH, D_MODEL), 0.02),
        'ffn2_w_gate': nrm(ks[19], (DEPTH, D_MODEL, D_FF), D_MODEL ** -0.5),
        'ffn2_w_up': nrm(ks[20], (DEPTH, D_MODEL, D_FF), D_MODEL ** -0.5),
        'ffn2_w_down': nrm(ks[21], (DEPTH, D_FF, D_MODEL), D_FF ** -0.5 * DEEPNORM_BETA),
        'ln3_g': 1.0 + nrm(ks[22], (DEPTH, D_MODEL), 0.02),
        'ln3_b': nrm(ks[23], (DEPTH, D_MODEL), 0.02),
        'w_ple_proj': nrm(ks[24], (DEPTH, D_PLE, D_MODEL), D_PLE ** -0.5),
        'w_ple_gate': nrm(ks[25], (DEPTH, D_MODEL, D_MODEL), D_MODEL ** -0.5),
    }


def _fwd_reference(x, p, positions, ffn1_w_gate, ffn1_w_up, ffn1_w_down, ln1_g, ln1_b,
              w_in, conv_w, conv_b, w_rgate, b_rgate, w_igate, b_igate, lru_lambda, w_out,
              ln2_g, ln2_b, ffn2_w_gate, ffn2_w_up, ffn2_w_down, ln3_g, ln3_b,
              w_ple_proj, w_ple_gate):
    for i in range(DEPTH):
        x = layer_norm(DEEPNORM_ALPHA * x + 0.5 * swiglu(x, ffn1_w_gate[i], ffn1_w_up[i], ffn1_w_down[i]),
                       ln1_g[i], ln1_b[i])
        mix = hybrid_mixer(x, positions, w_in[i], conv_w[i], conv_b[i], w_rgate[i], b_rgate[i],
                           w_igate[i], b_igate[i], lru_lambda[i], w_out[i])
        x = layer_norm(DEEPNORM_ALPHA * x + mix, ln2_g[i], ln2_b[i])
        x = layer_norm(DEEPNORM_ALPHA * x + 0.5 * swiglu(x, ffn2_w_gate[i], ffn2_w_up[i], ffn2_w_down[i]),
                       ln3_g[i], ln3_b[i])
        x = x + jax.nn.sigmoid(x @ w_ple_gate[i]) * (p[i] @ w_ple_proj[i])
    return x


import jax as _jax
import jax.numpy as _jnp

TWIN_FORMAT = 'train_step'
FWD_PARAMS = ['x', 'p', 'positions', 'ffn1_w_gate', 'ffn1_w_up', 'ffn1_w_down', 'ln1_g', 'ln1_b', 'w_in', 'conv_w', 'conv_b', 'w_rgate', 'b_rgate', 'w_igate', 'b_igate', 'lru_lambda', 'w_out', 'ln2_g', 'ln2_b', 'ffn2_w_gate', 'ffn2_w_up', 'ffn2_w_down', 'ln3_g', 'ln3_b', 'w_ple_proj', 'w_ple_gate']
TWIN_WEIGHTS = ['ffn1_w_gate', 'ffn1_w_up', 'ffn1_w_down', 'ln1_g', 'ln1_b', 'w_in', 'conv_w', 'conv_b', 'w_rgate', 'b_rgate', 'w_igate', 'b_igate', 'lru_lambda', 'w_out', 'ln2_g', 'ln2_b', 'ffn2_w_gate', 'ffn2_w_up', 'ffn2_w_down', 'ln3_g', 'ln3_b', 'w_ple_proj', 'w_ple_gate']
TWIN_DIFF_INPUT = 'x'
TWIN_INPUTS = ['x', 'p', 'positions', 'ffn1_w_gate', 'ffn1_w_up', 'ffn1_w_down', 'ln1_g', 'ln1_b', 'w_in', 'conv_w', 'conv_b', 'w_rgate', 'b_rgate', 'w_igate', 'b_igate', 'lru_lambda', 'w_out', 'ln2_g', 'ln2_b', 'ffn2_w_gate', 'ffn2_w_up', 'ffn2_w_down', 'ln3_g', 'ln3_b', 'w_ple_proj', 'w_ple_gate', 'loss_target', 'm_ffn1_w_gate', 'm_ffn1_w_up', 'm_ffn1_w_down', 'm_ln1_g', 'm_ln1_b', 'm_w_in', 'm_conv_w', 'm_conv_b', 'm_w_rgate', 'm_b_rgate', 'm_w_igate', 'm_b_igate', 'm_lru_lambda', 'm_w_out', 'm_ln2_g', 'm_ln2_b', 'm_ffn2_w_gate', 'm_ffn2_w_up', 'm_ffn2_w_down', 'm_ln3_g', 'm_ln3_b', 'm_w_ple_proj', 'm_w_ple_gate', 'v_ffn1_w_gate', 'v_ffn1_w_up', 'v_ffn1_w_down', 'v_ln1_g', 'v_ln1_b', 'v_w_in', 'v_conv_w', 'v_conv_b', 'v_w_rgate', 'v_b_rgate', 'v_w_igate', 'v_b_igate', 'v_lru_lambda', 'v_w_out', 'v_ln2_g', 'v_ln2_b', 'v_ffn2_w_gate', 'v_ffn2_w_up', 'v_ffn2_w_down', 'v_ln3_g', 'v_ln3_b', 'v_w_ple_proj', 'v_w_ple_gate']
TWIN_OUTPUTS = ['loss', 'grad_x', 'grad_ffn1_w_gate', 'grad_ffn1_w_up', 'grad_ffn1_w_down', 'grad_ln1_g', 'grad_ln1_b', 'grad_w_in', 'grad_conv_w', 'grad_conv_b', 'grad_w_rgate', 'grad_b_rgate', 'grad_w_igate', 'grad_b_igate', 'grad_lru_lambda', 'grad_w_out', 'grad_ln2_g', 'grad_ln2_b', 'grad_ffn2_w_gate', 'grad_ffn2_w_up', 'grad_ffn2_w_down', 'grad_ln3_g', 'grad_ln3_b', 'grad_w_ple_proj', 'grad_w_ple_gate', 'delta_ffn1_w_gate', 'delta_ffn1_w_up', 'delta_ffn1_w_down', 'delta_ln1_g', 'delta_ln1_b', 'delta_w_in', 'delta_conv_w', 'delta_conv_b', 'delta_w_rgate', 'delta_b_rgate', 'delta_w_igate', 'delta_b_igate', 'delta_lru_lambda', 'delta_w_out', 'delta_ln2_g', 'delta_ln2_b', 'delta_ffn2_w_gate', 'delta_ffn2_w_up', 'delta_ffn2_w_down', 'delta_ln3_g', 'delta_ln3_b', 'delta_w_ple_proj', 'delta_w_ple_gate', 'new_m_ffn1_w_gate', 'new_m_ffn1_w_up', 'new_m_ffn1_w_down', 'new_m_ln1_g', 'new_m_ln1_b', 'new_m_w_in', 'new_m_conv_w', 'new_m_conv_b', 'new_m_w_rgate', 'new_m_b_rgate', 'new_m_w_igate', 'new_m_b_igate', 'new_m_lru_lambda', 'new_m_w_out', 'new_m_ln2_g', 'new_m_ln2_b', 'new_m_ffn2_w_gate', 'new_m_ffn2_w_up', 'new_m_ffn2_w_down', 'new_m_ln3_g', 'new_m_ln3_b', 'new_m_w_ple_proj', 'new_m_w_ple_gate', 'new_v_ffn1_w_gate', 'new_v_ffn1_w_up', 'new_v_ffn1_w_down', 'new_v_ln1_g', 'new_v_ln1_b', 'new_v_w_in', 'new_v_conv_w', 'new_v_conv_b', 'new_v_w_rgate', 'new_v_b_rgate', 'new_v_w_igate', 'new_v_b_igate', 'new_v_lru_lambda', 'new_v_w_out', 'new_v_ln2_g', 'new_v_ln2_b', 'new_v_ffn2_w_gate', 'new_v_ffn2_w_up', 'new_v_ffn2_w_down', 'new_v_ln3_g', 'new_v_ln3_b', 'new_v_w_ple_proj', 'new_v_w_ple_gate']
TWIN_LEAF_KINDS = {'loss': 'loss', 'grad_x': 'grad_x', 'grad_ffn1_w_gate': 'grad_w', 'grad_ffn1_w_up': 'grad_w', 'grad_ffn1_w_down': 'grad_w', 'grad_ln1_g': 'grad_w', 'grad_ln1_b': 'grad_w', 'grad_w_in': 'grad_w', 'grad_conv_w': 'grad_w', 'grad_conv_b': 'grad_w', 'grad_w_rgate': 'grad_w', 'grad_b_rgate': 'grad_w', 'grad_w_igate': 'grad_w', 'grad_b_igate': 'grad_w', 'grad_lru_lambda': 'grad_w', 'grad_w_out': 'grad_w', 'grad_ln2_g': 'grad_w', 'grad_ln2_b': 'grad_w', 'grad_ffn2_w_gate': 'grad_w', 'grad_ffn2_w_up': 'grad_w', 'grad_ffn2_w_down': 'grad_w', 'grad_ln3_g': 'grad_w', 'grad_ln3_b': 'grad_w', 'grad_w_ple_proj': 'grad_w', 'grad_w_ple_gate': 'grad_w', 'delta_ffn1_w_gate': 'delta_w', 'delta_ffn1_w_up': 'delta_w', 'delta_ffn1_w_down': 'delta_w', 'delta_ln1_g': 'delta_w', 'delta_ln1_b': 'delta_w', 'delta_w_in': 'delta_w', 'delta_conv_w': 'delta_w', 'delta_conv_b': 'delta_w', 'delta_w_rgate': 'delta_w', 'delta_b_rgate': 'delta_w', 'delta_w_igate': 'delta_w', 'delta_b_igate': 'delta_w', 'delta_lru_lambda': 'delta_w', 'delta_w_out': 'delta_w', 'delta_ln2_g': 'delta_w', 'delta_ln2_b': 'delta_w', 'delta_ffn2_w_gate': 'delta_w', 'delta_ffn2_w_up': 'delta_w', 'delta_ffn2_w_down': 'delta_w', 'delta_ln3_g': 'delta_w', 'delta_ln3_b': 'delta_w', 'delta_w_ple_proj': 'delta_w', 'delta_w_ple_gate': 'delta_w', 'new_m_ffn1_w_gate': 'new_m', 'new_m_ffn1_w_up': 'new_m', 'new_m_ffn1_w_down': 'new_m', 'new_m_ln1_g': 'new_m', 'new_m_ln1_b': 'new_m', 'new_m_w_in': 'new_m', 'new_m_conv_w': 'new_m', 'new_m_conv_b': 'new_m', 'new_m_w_rgate': 'new_m', 'new_m_b_rgate': 'new_m', 'new_m_w_igate': 'new_m', 'new_m_b_igate': 'new_m', 'new_m_lru_lambda': 'new_m', 'new_m_w_out': 'new_m', 'new_m_ln2_g': 'new_m', 'new_m_ln2_b': 'new_m', 'new_m_ffn2_w_gate': 'new_m', 'new_m_ffn2_w_up': 'new_m', 'new_m_ffn2_w_down': 'new_m', 'new_m_ln3_g': 'new_m', 'new_m_ln3_b': 'new_m', 'new_m_w_ple_proj': 'new_m', 'new_m_w_ple_gate': 'new_m', 'new_v_ffn1_w_gate': 'new_v', 'new_v_ffn1_w_up': 'new_v', 'new_v_ffn1_w_down': 'new_v', 'new_v_ln1_g': 'new_v', 'new_v_ln1_b': 'new_v', 'new_v_w_in': 'new_v', 'new_v_conv_w': 'new_v', 'new_v_conv_b': 'new_v', 'new_v_w_rgate': 'new_v', 'new_v_b_rgate': 'new_v', 'new_v_w_igate': 'new_v', 'new_v_b_igate': 'new_v', 'new_v_lru_lambda': 'new_v', 'new_v_w_out': 'new_v', 'new_v_ln2_g': 'new_v', 'new_v_ln2_b': 'new_v', 'new_v_ffn2_w_gate': 'new_v', 'new_v_ffn2_w_up': 'new_v', 'new_v_ffn2_w_down': 'new_v', 'new_v_ln3_g': 'new_v', 'new_v_ln3_b': 'new_v', 'new_v_w_ple_proj': 'new_v', 'new_v_w_ple_gate': 'new_v'}


def _forward(args):
    return _fwd_reference(*[args[k] for k in FWD_PARAMS])


def _output_shape():
    def fwd():
        inp = _fwd_setup_inputs(0)
        return _fwd_reference(*[inp[k] for k in FWD_PARAMS])
    out = _jax.eval_shape(fwd)
    return out.shape, out.dtype

N_MICROBATCH = 1
ADAM_LR = 0.001
ADAM_B1 = 0.9
ADAM_B2 = 0.999
ADAM_EPS = 1e-08
ADAM_WD = 0.01
ADAM_STEP = 10
PER_EXAMPLE_BATCH_AXIS = {'x': 0, 'p': 1, 'positions': 0, 'loss_target': 0}
SHARED_INPUTS = []
_WEIGHT_DTYPES = {'ffn1_w_gate': _jnp.float32, 'ffn1_w_up': _jnp.float32, 'ffn1_w_down': _jnp.float32, 'ln1_g': _jnp.float32, 'ln1_b': _jnp.float32, 'w_in': _jnp.float32, 'conv_w': _jnp.float32, 'conv_b': _jnp.float32, 'w_rgate': _jnp.float32, 'b_rgate': _jnp.float32, 'w_igate': _jnp.float32, 'b_igate': _jnp.float32, 'lru_lambda': _jnp.float32, 'w_out': _jnp.float32, 'ln2_g': _jnp.float32, 'ln2_b': _jnp.float32, 'ffn2_w_gate': _jnp.float32, 'ffn2_w_up': _jnp.float32, 'ffn2_w_down': _jnp.float32, 'ln3_g': _jnp.float32, 'ln3_b': _jnp.float32, 'w_ple_proj': _jnp.float32, 'w_ple_gate': _jnp.float32}
MOMENT_SCALE = {'ffn1_w_gate': 1.455696e-02, 'ffn1_w_up': 1.411776e-02, 'ffn1_w_down': 3.947084e-02, 'ln1_g': 1.051236e+00, 'ln1_b': 3.970593e+00, 'w_in': 3.065633e-02, 'conv_w': 9.256050e-02, 'conv_b': 2.064084e+00, 'w_rgate': 5.921458e-02, 'b_rgate': 3.785962e-02, 'w_igate': 1.083219e-01, 'b_igate': 3.676120e-02, 'lru_lambda': 5.613110e-02, 'w_out': 1.737878e-01, 'ln2_g': 1.102569e+00, 'ln2_b': 3.339589e+00, 'ffn2_w_gate': 1.403936e-02, 'ffn2_w_up': 1.383656e-02, 'ffn2_w_down': 3.861223e-02, 'ln3_g': 3.302970e+01, 'ln3_b': 3.715797e+00, 'w_ple_proj': 4.612065e-01, 'w_ple_gate': 1.165959e-01}


def _to_microbatches(a, axis):
    t = _jnp.moveaxis(a, axis, 0)
    t = t.reshape((N_MICROBATCH, t.shape[0] // N_MICROBATCH) + t.shape[1:])
    return _jnp.moveaxis(t, 1, axis + 1)


def setup_inputs(seed: int = 0) -> dict:
    inp = _fwd_setup_inputs(seed)
    key = _jax.random.fold_in(_jax.random.key(seed), 7919)
    shape, _ = _output_shape()
    out = dict(inp)
    out["loss_target"] = _jax.random.normal(_jax.random.fold_in(key, 0), shape, _jnp.float32)
    for i, name in enumerate(TWIN_WEIGHTS):
        w = inp[name].astype(_jnp.float32)
        if MOMENT_SCALE is None:
            s = _jnp.sqrt(_jnp.mean(_jnp.square(w)) + 1e-30)
        else:
            s = MOMENT_SCALE[name]
        km, kv = _jax.random.split(_jax.random.fold_in(key, i + 1))
        out[name] = w
        out["m_" + name] = s * _jax.random.normal(km, w.shape, _jnp.float32)
        out["v_" + name] = (s * s) * _jax.random.uniform(kv, w.shape, _jnp.float32, 0.5, 1.5)
    if N_MICROBATCH > 1:
        for name, axis in PER_EXAMPLE_BATCH_AXIS.items():
            out[name] = _to_microbatches(out[name], axis)
    return {'x': out['x'], 'p': out['p'], 'positions': out['positions'], 'ffn1_w_gate': out['ffn1_w_gate'], 'ffn1_w_up': out['ffn1_w_up'], 'ffn1_w_down': out['ffn1_w_down'], 'ln1_g': out['ln1_g'], 'ln1_b': out['ln1_b'], 'w_in': out['w_in'], 'conv_w': out['conv_w'], 'conv_b': out['conv_b'], 'w_rgate': out['w_rgate'], 'b_rgate': out['b_rgate'], 'w_igate': out['w_igate'], 'b_igate': out['b_igate'], 'lru_lambda': out['lru_lambda'], 'w_out': out['w_out'], 'ln2_g': out['ln2_g'], 'ln2_b': out['ln2_b'], 'ffn2_w_gate': out['ffn2_w_gate'], 'ffn2_w_up': out['ffn2_w_up'], 'ffn2_w_down': out['ffn2_w_down'], 'ln3_g': out['ln3_g'], 'ln3_b': out['ln3_b'], 'w_ple_proj': out['w_ple_proj'], 'w_ple_gate': out['w_ple_gate'], 'loss_target': out['loss_target'], 'm_ffn1_w_gate': out['m_ffn1_w_gate'], 'm_ffn1_w_up': out['m_ffn1_w_up'], 'm_ffn1_w_down': out['m_ffn1_w_down'], 'm_ln1_g': out['m_ln1_g'], 'm_ln1_b': out['m_ln1_b'], 'm_w_in': out['m_w_in'], 'm_conv_w': out['m_conv_w'], 'm_conv_b': out['m_conv_b'], 'm_w_rgate': out['m_w_rgate'], 'm_b_rgate': out['m_b_rgate'], 'm_w_igate': out['m_w_igate'], 'm_b_igate': out['m_b_igate'], 'm_lru_lambda': out['m_lru_lambda'], 'm_w_out': out['m_w_out'], 'm_ln2_g': out['m_ln2_g'], 'm_ln2_b': out['m_ln2_b'], 'm_ffn2_w_gate': out['m_ffn2_w_gate'], 'm_ffn2_w_up': out['m_ffn2_w_up'], 'm_ffn2_w_down': out['m_ffn2_w_down'], 'm_ln3_g': out['m_ln3_g'], 'm_ln3_b': out['m_ln3_b'], 'm_w_ple_proj': out['m_w_ple_proj'], 'm_w_ple_gate': out['m_w_ple_gate'], 'v_ffn1_w_gate': out['v_ffn1_w_gate'], 'v_ffn1_w_up': out['v_ffn1_w_up'], 'v_ffn1_w_down': out['v_ffn1_w_down'], 'v_ln1_g': out['v_ln1_g'], 'v_ln1_b': out['v_ln1_b'], 'v_w_in': out['v_w_in'], 'v_conv_w': out['v_conv_w'], 'v_conv_b': out['v_conv_b'], 'v_w_rgate': out['v_w_rgate'], 'v_b_rgate': out['v_b_rgate'], 'v_w_igate': out['v_w_igate'], 'v_b_igate': out['v_b_igate'], 'v_lru_lambda': out['v_lru_lambda'], 'v_w_out': out['v_w_out'], 'v_ln2_g': out['v_ln2_g'], 'v_ln2_b': out['v_ln2_b'], 'v_ffn2_w_gate': out['v_ffn2_w_gate'], 'v_ffn2_w_up': out['v_ffn2_w_up'], 'v_ffn2_w_down': out['v_ffn2_w_down'], 'v_ln3_g': out['v_ln3_g'], 'v_ln3_b': out['v_ln3_b'], 'v_w_ple_proj': out['v_w_ple_proj'], 'v_w_ple_gate': out['v_w_ple_gate']}


def _loss(weights, diff, rest, loss_target):
    with _jax.named_scope("forward"):
        args = {**rest, TWIN_DIFF_INPUT: diff, **{k: w.astype(_WEIGHT_DTYPES[k]) for k, w in weights.items()}}
        y = _forward(args)
    with _jax.named_scope("loss_head"):
        err = _jnp.square(y.astype(_jnp.float32) - loss_target)
        return 0.5 * _jnp.sum(_jnp.mean(err, axis=-1)) if err.ndim else 0.5 * err


def _adamw(w, g, m, v):
    m = ADAM_B1 * m + (1.0 - ADAM_B1) * g
    v = ADAM_B2 * v + (1.0 - ADAM_B2) * _jnp.square(g)
    m_hat = m / (1.0 - ADAM_B1 ** ADAM_STEP)
    v_hat = v / (1.0 - ADAM_B2 ** ADAM_STEP)
    delta = -ADAM_LR * (m_hat / (_jnp.sqrt(v_hat) + ADAM_EPS) + ADAM_WD * w)
    return delta, m, v


def reference(x, p, positions, ffn1_w_gate, ffn1_w_up, ffn1_w_down, ln1_g, ln1_b, w_in, conv_w, conv_b, w_rgate, b_rgate, w_igate, b_igate, lru_lambda, w_out, ln2_g, ln2_b, ffn2_w_gate, ffn2_w_up, ffn2_w_down, ln3_g, ln3_b, w_ple_proj, w_ple_gate, loss_target, m_ffn1_w_gate, m_ffn1_w_up, m_ffn1_w_down, m_ln1_g, m_ln1_b, m_w_in, m_conv_w, m_conv_b, m_w_rgate, m_b_rgate, m_w_igate, m_b_igate, m_lru_lambda, m_w_out, m_ln2_g, m_ln2_b, m_ffn2_w_gate, m_ffn2_w_up, m_ffn2_w_down, m_ln3_g, m_ln3_b, m_w_ple_proj, m_w_ple_gate, v_ffn1_w_gate, v_ffn1_w_up, v_ffn1_w_down, v_ln1_g, v_ln1_b, v_w_in, v_conv_w, v_conv_b, v_w_rgate, v_b_rgate, v_w_igate, v_b_igate, v_lru_lambda, v_w_out, v_ln2_g, v_ln2_b, v_ffn2_w_gate, v_ffn2_w_up, v_ffn2_w_down, v_ln3_g, v_ln3_b, v_w_ple_proj, v_w_ple_gate):
    given = dict(x=x, p=p, positions=positions, ffn1_w_gate=ffn1_w_gate, ffn1_w_up=ffn1_w_up, ffn1_w_down=ffn1_w_down, ln1_g=ln1_g, ln1_b=ln1_b, w_in=w_in, conv_w=conv_w, conv_b=conv_b, w_rgate=w_rgate, b_rgate=b_rgate, w_igate=w_igate, b_igate=b_igate, lru_lambda=lru_lambda, w_out=w_out, ln2_g=ln2_g, ln2_b=ln2_b, ffn2_w_gate=ffn2_w_gate, ffn2_w_up=ffn2_w_up, ffn2_w_down=ffn2_w_down, ln3_g=ln3_g, ln3_b=ln3_b, w_ple_proj=w_ple_proj, w_ple_gate=w_ple_gate, loss_target=loss_target, m_ffn1_w_gate=m_ffn1_w_gate, m_ffn1_w_up=m_ffn1_w_up, m_ffn1_w_down=m_ffn1_w_down, m_ln1_g=m_ln1_g, m_ln1_b=m_ln1_b, m_w_in=m_w_in, m_conv_w=m_conv_w, m_conv_b=m_conv_b, m_w_rgate=m_w_rgate, m_b_rgate=m_b_rgate, m_w_igate=m_w_igate, m_b_igate=m_b_igate, m_lru_lambda=m_lru_lambda, m_w_out=m_w_out, m_ln2_g=m_ln2_g, m_ln2_b=m_ln2_b, m_ffn2_w_gate=m_ffn2_w_gate, m_ffn2_w_up=m_ffn2_w_up, m_ffn2_w_down=m_ffn2_w_down, m_ln3_g=m_ln3_g, m_ln3_b=m_ln3_b, m_w_ple_proj=m_w_ple_proj, m_w_ple_gate=m_w_ple_gate, v_ffn1_w_gate=v_ffn1_w_gate, v_ffn1_w_up=v_ffn1_w_up, v_ffn1_w_down=v_ffn1_w_down, v_ln1_g=v_ln1_g, v_ln1_b=v_ln1_b, v_w_in=v_w_in, v_conv_w=v_conv_w, v_conv_b=v_conv_b, v_w_rgate=v_w_rgate, v_b_rgate=v_b_rgate, v_w_igate=v_w_igate, v_b_igate=v_b_igate, v_lru_lambda=v_lru_lambda, v_w_out=v_w_out, v_ln2_g=v_ln2_g, v_ln2_b=v_ln2_b, v_ffn2_w_gate=v_ffn2_w_gate, v_ffn2_w_up=v_ffn2_w_up, v_ffn2_w_down=v_ffn2_w_down, v_ln3_g=v_ln3_g, v_ln3_b=v_ln3_b, v_w_ple_proj=v_w_ple_proj, v_w_ple_gate=v_w_ple_gate)
    weights = {n: given[n] for n in TWIN_WEIGHTS}
    shared = {n: given[n] for n in SHARED_INPUTS}
    per_example = {n: given[n] for n in ['x', 'p', 'positions']}
    grad_fn = _jax.value_and_grad(_loss, argnums=(0, 1))

    def one_microbatch(ex, loss_target):
        ex = dict(ex)
        diff = ex.pop(TWIN_DIFF_INPUT)
        return grad_fn(weights, diff, {**shared, **ex}, loss_target)

    if N_MICROBATCH == 1:
        loss, (grad_w, grad_x) = one_microbatch(per_example, given["loss_target"])
    else:
        def body(carry, xs):
            loss_sum, grad_sum = carry
            l_k, (gw_k, gx_k) = one_microbatch(xs[0], xs[1])
            with _jax.named_scope("update"):
                return (loss_sum + l_k, _jax.tree.map(_jnp.add, grad_sum, gw_k)), gx_k

        init = (_jnp.zeros((), _jnp.float32), _jax.tree.map(_jnp.zeros_like, weights))
        (loss, grad_w), grad_x = _jax.lax.scan(body, init, (per_example, given["loss_target"]))
    with _jax.named_scope("update"):
        delta_w, new_m, new_v = {}, {}, {}
        for n in TWIN_WEIGHTS:
            delta_w[n], new_m[n], new_v[n] = _adamw(weights[n], grad_w[n], given["m_" + n], given["v_" + n])
    return (loss, grad_x, *[grad_w[n] for n in TWIN_WEIGHTS], *[delta_w[n] for n in TWIN_WEIGHTS],
            *[new_m[n] for n in TWIN_WEIGHTS], *[new_v[n] for n in TWIN_WEIGHTS])
```

```python
import functools

import jax
import jax.numpy as jnp
from jax import lax
from jax.experimental import pallas as pl
from jax.experimental.pallas import tpu as pltpu

F32 = jnp.float32
BF16 = jnp.bfloat16

N_DEV = 8
LANES = 128
MIB = 1 << 20

HEAD_DIM = 128
N_KV_HEADS = 4
DILATIONS = (1, 4, 16)
N_PATTERNS = 3
SPAN = 128
ROT_DIMS = 32
ROPE_THETA = 500000.0
LRU_C = 8.0
CONV_WIDTH = 4
LN_EPS = 1e-5
DEEPNORM_ALPHA = 2.0 ** 0.25
ATTN_TILE = SPAN * DILATIONS[-1]

ADAM_LR = 0.001
ADAM_B1 = 0.9
ADAM_B2 = 0.999
ADAM_EPS = 1e-08
ADAM_WD = 0.01
ADAM_STEP = 10

MESH = pl.DeviceIdType.MESH
EPILOGUE_ROWS = 64


def _cp(semantics, vmem_mib):
    return pltpu.CompilerParams(dimension_semantics=semantics, vmem_limit_bytes=vmem_mib * MIB)


def _pick(n, candidates):
    for c in candidates:
        if n % c == 0:
            return c
    return n


def _mm(a, b, *, ta=False, tb=False, out_dtype=F32, scale=1.0, bm, bn, bk, name):
    m, k = (a.shape[1], a.shape[0]) if ta else a.shape
    n = b.shape[0] if tb else b.shape[1]
    bm, bn, bk = min(bm, m), min(bn, n), min(bk, k)
    assert m % bm == 0 and n % bn == 0 and k % bk == 0, (name, m, n, k, bm, bn, bk)
    nk = k // bk
    a_spec = pl.BlockSpec((bk, bm), lambda i, j, kk: (kk, i)) if ta else pl.BlockSpec((bm, bk), lambda i, j, kk: (i, kk))
    b_spec = pl.BlockSpec((bn, bk), lambda i, j, kk: (j, kk)) if tb else pl.BlockSpec((bk, bn), lambda i, j, kk: (kk, j))
    dn = (((0 if ta else 1,), (1 if tb else 0,)), ((), ()))

    def body(a_ref, b_ref, o_ref, *acc):
        part = lax.dot_general(a_ref[...].astype(BF16), b_ref[...].astype(BF16), dn, preferred_element_type=F32)
        if nk == 1:
            o_ref[...] = (part * scale).astype(out_dtype)
            return
        acc_ref, = acc
        kk = pl.program_id(2)

        @pl.when(kk == 0)
        def _():
            acc_ref[...] = part

        @pl.when(kk > 0)
        def _():
            acc_ref[...] += part

        @pl.when(kk == nk - 1)
        def _():
            o_ref[...] = (acc_ref[...] * scale).astype(out_dtype)

    return pl.pallas_call(
        body, name=name,
        out_shape=jax.ShapeDtypeStruct((m, n), out_dtype),
        grid=(m // bm, n // bn, nk),
        in_specs=[a_spec, b_spec],
        out_specs=pl.BlockSpec((bm, bn), lambda i, j, kk: (i, j)),
        scratch_shapes=[pltpu.VMEM((bm, bn), F32)] if nk > 1 else [],
        compiler_params=_cp(("parallel", "parallel", "arbitrary"), 56),
    )(a, b)


def _ffn_up(xb, wg, wu, *, bm, bn, name):
    s, d = xb.shape
    f = wg.shape[1]
    bm, bn = min(bm, s), min(bn, f)
    assert s % bm == 0 and f % bn == 0

    def body(x_ref, wg_ref, wu_ref, g_ref, u_ref, h_ref):
        x = x_ref[...]
        g = jnp.dot(x, wg_ref[...], preferred_element_type=F32)
        u = jnp.dot(x, wu_ref[...], preferred_element_type=F32)
        g_ref[...] = g.astype(BF16)
        u_ref[...] = u.astype(BF16)
        h_ref[...] = (g * jax.nn.sigmoid(g) * u).astype(BF16)

    out = jax.ShapeDtypeStruct((s, f), BF16)
    blk = pl.BlockSpec((bm, bn), lambda i, j: (i, j))
    return pl.pallas_call(
        body, name=name, out_shape=(out, out, out),
        grid=(s // bm, f // bn),
        in_specs=[pl.BlockSpec((bm, d), lambda i, j: (i, 0)),
                  pl.BlockSpec((d, bn), lambda i, j: (0, j)),
                  pl.BlockSpec((d, bn), lambda i, j: (0, j))],
        out_specs=(blk, blk, blk),
        compiler_params=_cp(("parallel", "arbitrary"), 56),
    )(xb, wg, wu)


def _ffn_bwd_dh(dzb, wd, g, u, *, scale, bm, bn, name):
    s, d = dzb.shape
    f = wd.shape[0]
    bm, bn = min(bm, s), min(bn, f)
    assert s % bm == 0 and f % bn == 0

    def body(dz_ref, wd_ref, g_ref, u_ref, dg_ref, du_ref):
        dh = lax.dot_general(dz_ref[...], wd_ref[...], (((1,), (1,)), ((), ())), preferred_element_type=F32) * scale
        gg = g_ref[...].astype(F32)
        sig = jax.nn.sigmoid(gg)
        silu = gg * sig
        dsilu = sig * (1.0 + gg * (1.0 - sig))
        dg_ref[...] = (dh * u_ref[...].astype(F32) * dsilu).astype(BF16)
        du_ref[...] = (dh * silu).astype(BF16)

    out = jax.ShapeDtypeStruct((s, f), BF16)
    blk = pl.BlockSpec((bm, bn), lambda i, j: (i, j))
    return pl.pallas_call(
        body, name=name, out_shape=(out, out),
        grid=(s // bm, f // bn),
        in_specs=[pl.BlockSpec((bm, d), lambda i, j: (i, 0)),
                  pl.BlockSpec((bn, d), lambda i, j: (j, 0)), blk, blk],
        out_specs=(blk, blk),
        compiler_params=_cp(("parallel", "arbitrary"), 56),
    )(dzb, wd, g, u)


def _mm_ln(a, b, res, gamma, beta, *, res_scale, mm_scale, bm, bk, name):
    s, k = a.shape
    d = b.shape[1]
    bm, bk = min(bm, s), min(bk, k)
    assert s % bm == 0 and k % bk == 0
    nk = k // bk
    ch = min(EPILOGUE_ROWS, bm)

    def body(a_ref, b_ref, r_ref, g_ref, be_ref, y_ref, yb_ref, xh_ref, rs_ref, acc_ref):
        kk = pl.program_id(1)
        part = jnp.dot(a_ref[...], b_ref[...], preferred_element_type=F32)

        @pl.when(kk == 0)
        def _():
            acc_ref[...] = part

        @pl.when(kk > 0)
        def _():
            acc_ref[...] += part

        @pl.when(kk == nk - 1)
        def _():
            def chunk(ci, carry):
                rows = pl.ds(pl.multiple_of(ci * ch, ch), ch)
                z = res_scale * r_ref[rows, :] + mm_scale * acc_ref[rows, :]
                mu = jnp.mean(z, axis=-1, keepdims=True)
                zc = z - mu
                var = jnp.mean(zc * zc, axis=-1, keepdims=True)
                rstd = lax.rsqrt(var + LN_EPS)
                xh = zc * rstd
                y = xh * g_ref[...] + be_ref[...]
                y_ref[rows, :] = y
                yb_ref[rows, :] = y.astype(BF16)
                xh_ref[rows, :] = xh
                rs_ref[rows, :] = rstd
                return carry

            lax.fori_loop(0, bm // ch, chunk, 0)

    row = pl.BlockSpec((bm, d), lambda i, kk: (i, 0))
    vec = pl.BlockSpec((1, d), lambda i, kk: (0, 0))
    return pl.pallas_call(
        body, name=name,
        out_shape=(jax.ShapeDtypeStruct((s, d), F32), jax.ShapeDtypeStruct((s, d), BF16),
                   jax.ShapeDtypeStruct((s, d), F32), jax.ShapeDtypeStruct((s, 1), F32)),
        grid=(s // bm, nk),
        in_specs=[pl.BlockSpec((bm, bk), lambda i, kk: (i, kk)),
                  pl.BlockSpec((bk, d), lambda i, kk: (kk, 0)), row, vec, vec],
        out_specs=(row, row, row, pl.BlockSpec((bm, 1), lambda i, kk: (i, 0))),
        scratch_shapes=[pltpu.VMEM((bm, d), F32)],
        compiler_params=_cp(("parallel", "arbitrary"), 56),
    )(a, b, res, gamma, beta)


def _mm_dx(pairs, extra, *, extra_scale, ln, bm, bk, name):
    s, k = pairs[0][0].shape
    d = pairs[0][1].shape[0]
    bm, bk = min(bm, s), min(bk, k)
    assert s % bm == 0 and k % bk == 0
    nk = k // bk
    npair = len(pairs)
    ch = min(EPILOGUE_ROWS, bm)

    def body(*refs):
        ab = refs[:2 * npair]
        e_ref = refs[2 * npair]
        pos = 2 * npair + 1
        if ln is not None:
            xh_ref, rs_ref, g_ref = refs[pos:pos + 3]
            dz_ref, dzb_ref, dg_ref, db_ref, acc_ref = refs[pos + 3:]
        else:
            dx_ref, acc_ref = refs[pos:]
        i = pl.program_id(0)
        kk = pl.program_id(1)
        part = None
        for p in range(npair):
            t = lax.dot_general(ab[2 * p][...], ab[2 * p + 1][...], (((1,), (1,)), ((), ())),
                                preferred_element_type=F32)
            part = t if part is None else part + t

        @pl.when(kk == 0)
        def _():
            acc_ref[...] = part

        @pl.when(kk > 0)
        def _():
            acc_ref[...] += part

        @pl.when(kk == nk - 1)
        def _():
            if ln is None:
                dx_ref[...] = extra_scale * e_ref[...] + acc_ref[...]
                return

            def chunk(ci, carry):
                dgp, dbp = carry
                rows = pl.ds(pl.multiple_of(ci * ch, ch), ch)
                dx = extra_scale * e_ref[rows, :] + acc_ref[rows, :]
                xh = xh_ref[rows, :]
                dxh = dx * g_ref[...]
                m1 = jnp.mean(dxh, axis=-1, keepdims=True)
                m2 = jnp.mean(dxh * xh, axis=-1, keepdims=True)
                dz = rs_ref[rows, :] * (dxh - m1 - xh * m2)
                dz_ref[rows, :] = dz
                dzb_ref[rows, :] = dz.astype(BF16)
                return dgp + jnp.sum(dx * xh, axis=0, keepdims=True), dbp + jnp.sum(dx, axis=0, keepdims=True)

            zero = jnp.zeros((1, d), F32)
            dgp, dbp = lax.fori_loop(0, bm // ch, chunk, (zero, zero))

            @pl.when(i == 0)
            def _():
                dg_ref[...] = dgp
                db_ref[...] = dbp

            @pl.when(i > 0)
            def _():
                dg_ref[...] += dgp
                db_ref[...] += dbp

    row = pl.BlockSpec((bm, d), lambda i, kk: (i, 0))
    vec = pl.BlockSpec((1, d), lambda i, kk: (0, 0))
    in_specs, args = [], []
    for a, w in pairs:
        in_specs += [pl.BlockSpec((bm, bk), lambda i, kk: (i, kk)), pl.BlockSpec((d, bk), lambda i, kk: (0, kk))]
        args += [a, w]
    in_specs.append(row)
    args.append(extra)
    if ln is not None:
        in_specs += [row, pl.BlockSpec((bm, 1), lambda i, kk: (i, 0)), vec]
        args += list(ln)
        out_shape = (jax.ShapeDtypeStruct((s, d), F32), jax.ShapeDtypeStruct((s, d), BF16),
                     jax.ShapeDtypeStruct((1, d), F32), jax.ShapeDtypeStruct((1, d), F32))
        out_specs = (row, row, vec, vec)
    else:
        out_shape = jax.ShapeDtypeStruct((s, d), F32)
        out_specs = row
    return pl.pallas_call(
        body, name=name, out_shape=out_shape,
        grid=(s // bm, nk), in_specs=in_specs, out_specs=out_specs,
        scratch_shapes=[pltpu.VMEM((bm, d), F32)],
        compiler_params=_cp(("arbitrary", "arbitrary"), 56),
    )(*args)


def _ple_loss(x3, x3b, p, wpg, wpp, target, *, bm, bn, name):
    s, d = x3.shape
    dp = p.shape[1]
    bm, bn = min(bm, s), min(bn, d)
    assert s % bm == 0 and d % bn == 0
    inv_d = 1.0 / d

    def body(x_ref, xb_ref, p_ref, wg_ref, wp_ref, t_ref, l_ref, dy_ref, dg_ref, dp_ref):
        first = (pl.program_id(0) == 0) & (pl.program_id(1) == 0)
        gp = jnp.dot(xb_ref[...], wg_ref[...], preferred_element_type=F32)
        pp = jnp.dot(p_ref[...].astype(BF16), wp_ref[...], preferred_element_type=F32)
        sig = jax.nn.sigmoid(gp)
        err = x_ref[...] + sig * pp - t_ref[...]
        part = jnp.sum(err * err)

        @pl.when(first)
        def _():
            l_ref[...] = jnp.zeros_like(l_ref)

        l_ref[...] += part
        dy = err * inv_d
        dy_ref[...] = dy
        dg_ref[...] = (dy * pp * sig * (1.0 - sig)).astype(BF16)
        dp_ref[...] = (dy * sig).astype(BF16)

    blk = pl.BlockSpec((bm, bn), lambda i, j: (i, j))
    return pl.pallas_call(
        body, name=name,
        out_shape=(jax.ShapeDtypeStruct((8, LANES), F32), jax.ShapeDtypeStruct((s, d), F32),
                   jax.ShapeDtypeStruct((s, d), BF16), jax.ShapeDtypeStruct((s, d), BF16)),
        grid=(s // bm, d // bn),
        in_specs=[blk, pl.BlockSpec((bm, d), lambda i, j: (i, 0)), pl.BlockSpec((bm, dp), lambda i, j: (i, 0)),
                  pl.BlockSpec((d, bn), lambda i, j: (0, j)), pl.BlockSpec((dp, bn), lambda i, j: (0, j)), blk],
        out_specs=(pl.BlockSpec((8, LANES), lambda i, j: (0, 0)), blk, blk, blk),
        compiler_params=_cp(("arbitrary", "arbitrary"), 56),
    )(x3, x3b, p, wpg, wpp, target)


def _rope_tables(positions):
    half = ROT_DIMS // 2
    inv_freq = jnp.power(jnp.float32(ROPE_THETA), -jnp.arange(half, dtype=F32) * (2.0 / ROT_DIMS))
    ang = positions.astype(F32)[:, None] * inv_freq
    cos, sin = jnp.cos(ang), jnp.sin(ang)
    s = positions.shape[0]
    zeros = jnp.zeros((s, half), F32)
    rest0 = jnp.zeros((s, HEAD_DIM - ROT_DIMS), F32)
    cf = jnp.concatenate([cos, cos, jnp.ones((s, HEAD_DIM - ROT_DIMS), F32)], axis=1)
    sa = jnp.concatenate([-sin, zeros, rest0], axis=1)
    sb = jnp.concatenate([zeros, sin, rest0], axis=1)
    return cf, sa, sb


def _rotary(t, tabs, *, n_cols, inverse, out_dtype, bs, name):
    s = t.shape[0]
    bs = min(bs, s)
    half = ROT_DIMS // 2

    def body(t_ref, cf_ref, sa_ref, sb_ref, o_ref):
        v = t_ref[...]
        if inverse:
            o = (v * cf_ref[...] + pltpu.roll(v * sa_ref[...], half, 1)
                 + pltpu.roll(v * sb_ref[...], HEAD_DIM - half, 1))
        else:
            o = (v * cf_ref[...] + pltpu.roll(v, HEAD_DIM - half, 1) * sa_ref[...]
                 + pltpu.roll(v, half, 1) * sb_ref[...])
        o_ref[...] = o.astype(out_dtype)

    blk = pl.BlockSpec((bs, HEAD_DIM), lambda i, j: (i, j))
    tab = pl.BlockSpec((bs, HEAD_DIM), lambda i, j: (i, 0))
    return pl.pallas_call(
        body, name=name, out_shape=jax.ShapeDtypeStruct((s, n_cols * HEAD_DIM), out_dtype),
        grid=(s // bs, n_cols), in_specs=[blk, tab, tab, tab], out_specs=blk,
        compiler_params=_cp(("parallel", "arbitrary"), 32),
    )(t, *tabs)


def _attn_blocks():
    out = []
    for g, dil in enumerate(DILATIONS):
        sup = SPAN * dil
        for j in range(ATTN_TILE // sup):
            for r in range(dil):
                out.append((g, j * sup + r, dil, (j - 1) * sup + r if j > 0 else None, ATTN_TILE - sup + r))
    return out


def _rows(ref, start, dil, lead=None):
    idx = pl.ds(start, SPAN, stride=dil) if dil > 1 else pl.ds(start, SPAN)
    return ref[idx, :] if lead is None else ref[lead, idx, :]


def _band_masks(n):
    qi = lax.broadcasted_iota(jnp.int32, (SPAN, 2 * SPAN), 0)
    ki = lax.broadcasted_iota(jnp.int32, (SPAN, 2 * SPAN), 1)
    band = (ki >= qi) & (ki <= qi + SPAN)
    return band, band & ((ki >= SPAN) | (n > 0))


def _attn_fwd(qkr, proj, *, name):
    s = qkr.shape[0]
    t = ATTN_TILE
    assert s % t == 0
    nt = s // t
    scale = HEAD_DIM ** -0.5
    kcol, vcol = N_PATTERNS * N_KV_HEADS, (N_PATTERNS + 1) * N_KV_HEADS
    blocks = _attn_blocks()

    def body(q0, q1, q2, kc_ref, kp_ref, vc_ref, vp_ref, o_ref, l_ref, og, lg):
        n = pl.program_id(1)
        band, band_first = _band_masks(n)
        q_refs = (q0, q1, q2)
        for g, start, dil, prev_in_tile, prev_start in blocks:
            q = _rows(q_refs[g], start, dil).astype(BF16)
            if prev_in_tile is not None:
                kp, vp, mask = _rows(kc_ref, prev_in_tile, dil), _rows(vc_ref, prev_in_tile, dil), band
            else:
                kp, vp, mask = _rows(kp_ref, prev_start, dil), _rows(vp_ref, prev_start, dil), band_first
            kk = jnp.concatenate([kp, _rows(kc_ref, start, dil)], axis=0).astype(BF16)
            vv = jnp.concatenate([vp, _rows(vc_ref, start, dil)], axis=0).astype(BF16)
            sc = lax.dot_general(q, kk, (((1,), (1,)), ((), ())), preferred_element_type=F32) * scale
            sc = jnp.where(mask, sc, -1e30)
            m = jnp.max(sc, axis=-1, keepdims=True)
            e = jnp.exp(sc - m)
            den = jnp.sum(e, axis=-1, keepdims=True)
            o = jnp.dot(e.astype(BF16), vv, preferred_element_type=F32) / den
            idx = pl.ds(start, SPAN, stride=dil) if dil > 1 else pl.ds(start, SPAN)
            og[g, idx, :] = o
            lg[g, idx, :] = jnp.broadcast_to(m + jnp.log(den), (SPAN, HEAD_DIM))
        l0, l1, l2 = lg[0], lg[1], lg[2]
        m = jnp.maximum(jnp.maximum(l0, l1), l2)
        w0, w1, w2 = jnp.exp(l0 - m), jnp.exp(l1 - m), jnp.exp(l2 - m)
        den = w0 + w1 + w2
        o_ref[...] = (w0 * og[0] + w1 * og[1] + w2 * og[2]) / den
        l_ref[...] = m + jnp.log(den)

    def col(c, prev=False):
        if prev:
            return pl.BlockSpec((t, HEAD_DIM), lambda h, n: (jnp.maximum(n - 1, 0), c + h))
        return pl.BlockSpec((t, HEAD_DIM), lambda h, n: (n, c + h))

    out = jax.ShapeDtypeStruct((s, N_KV_HEADS * HEAD_DIM), F32)
    return pl.pallas_call(
        body, name=name, out_shape=(out, out),
        grid=(N_KV_HEADS, nt),
        in_specs=[col(0), col(N_KV_HEADS), col(2 * N_KV_HEADS), col(kcol), col(kcol, True), col(vcol), col(vcol, True)],
        out_specs=(col(0), col(0)),
        scratch_shapes=[pltpu.VMEM((N_PATTERNS, t, HEAD_DIM), F32), pltpu.VMEM((N_PATTERNS, t, HEAD_DIM), F32)],
        compiler_params=_cp(("parallel", "arbitrary"), 48),
    )(qkr, qkr, qkr, qkr, qkr, proj, proj)


def _attn_bwd(qkr, proj, attn, lse, dcat, *, name):
    s = qkr.shape[0]
    t = ATTN_TILE
    nt = s // t
    scale = HEAD_DIM ** -0.5
    kcol, vcol = N_PATTERNS * N_KV_HEADS, (N_PATTERNS + 1) * N_KV_HEADS
    blocks = _attn_blocks()

    def body(q0, q1, q2, kc_ref, kp_ref, vc_ref, vp_ref, o_ref, l_ref, do_ref,
             dq0, dq1, dq2, dk_ref, dv_ref, ck, cv, tkc, tvc, tkp, tvp):
        n = pl.program_id(1)
        for ref in (tkc, tvc, tkp, tvp):
            ref[...] = jnp.zeros_like(ref)

        @pl.when(n < nt)
        def _():
            band, band_first = _band_masks(n)
            q_refs, dq_refs = (q0, q1, q2), (dq0, dq1, dq2)
            for g, start, dil, prev_in_tile, prev_start in blocks:
                idx = pl.ds(start, SPAN, stride=dil) if dil > 1 else pl.ds(start, SPAN)
                q = q_refs[g][idx, :].astype(BF16)
                if prev_in_tile is not None:
                    kp, vp, mask = _rows(kc_ref, prev_in_tile, dil), _rows(vc_ref, prev_in_tile, dil), band
                else:
                    kp, vp, mask = _rows(kp_ref, prev_start, dil), _rows(vp_ref, prev_start, dil), band_first
                kk = jnp.concatenate([kp, kc_ref[idx, :]], axis=0).astype(BF16)
                vv = jnp.concatenate([vp, vc_ref[idx, :]], axis=0).astype(BF16)
                do = do_ref[idx, :]
                dsum = jnp.sum(do * o_ref[idx, :], axis=-1, keepdims=True)
                lrow = l_ref[idx, :][:, :1]
                dob = do.astype(BF16)
                sc = lax.dot_general(q, kk, (((1,), (1,)), ((), ())), preferred_element_type=F32) * scale
                p = jnp.where(mask, jnp.exp(sc - lrow), 0.0)
                dp = lax.dot_general(dob, vv, (((1,), (1,)), ((), ())), preferred_element_type=F32)
                ds = (p * (dp - dsum) * scale).astype(BF16)
                pb = p.astype(BF16)
                dq_refs[g][idx, :] = jnp.dot(ds, kk, preferred_element_type=F32)
                dkk = lax.dot_general(ds, q, (((0,), (0,)), ((), ())), preferred_element_type=F32)
                dvv = lax.dot_general(pb, dob, (((0,), (0,)), ((), ())), preferred_element_type=F32)
                tkc[idx, :] += dkk[SPAN:]
                tvc[idx, :] += dvv[SPAN:]
                if prev_in_tile is not None:
                    pidx = pl.ds(prev_in_tile, SPAN, stride=dil) if dil > 1 else pl.ds(prev_in_tile, SPAN)
                    tkc[pidx, :] += dkk[:SPAN]
                    tvc[pidx, :] += dvv[:SPAN]
                else:
                    pidx = pl.ds(prev_start, SPAN, stride=dil) if dil > 1 else pl.ds(prev_start, SPAN)
                    tkp[pidx, :] += dkk[:SPAN]
                    tvp[pidx, :] += dvv[:SPAN]

        @pl.when(n > 0)
        def _():
            dk_ref[...] = ck[...] + tkp[...]
            dv_ref[...] = (cv[...] + tvp[...]).astype(BF16)

        ck[...] = tkc[...]
        cv[...] = tvc[...]

    def col(c, prev=False):
        if prev:
            return pl.BlockSpec((t, HEAD_DIM), lambda h, n: (jnp.maximum(jnp.minimum(n, nt - 1) - 1, 0), c + h))
        return pl.BlockSpec((t, HEAD_DIM), lambda h, n: (jnp.minimum(n, nt - 1), c + h))

    kv_out = pl.BlockSpec((t, HEAD_DIM), lambda h, n: (jnp.maximum(n - 1, 0), h))
    tile = pltpu.VMEM((t, HEAD_DIM), F32)
    per_head = jax.ShapeDtypeStruct((s, N_KV_HEADS * HEAD_DIM), F32)
    return pl.pallas_call(
        body, name=name,
        out_shape=(per_head, per_head, per_head, per_head, jax.ShapeDtypeStruct((s, N_KV_HEADS * HEAD_DIM), BF16)),
        grid=(N_KV_HEADS, nt + 1),
        in_specs=[col(0), col(N_KV_HEADS), col(2 * N_KV_HEADS), col(kcol), col(kcol, True), col(vcol), col(vcol, True),
                  col(0), col(0), col(0)],
        out_specs=(col(0), col(0), col(0), kv_out, kv_out),
        scratch_shapes=[tile] * 6,
        compiler_params=_cp(("parallel", "arbitrary"), 48),
    )(qkr, qkr, qkr, qkr, qkr, proj, proj, attn, lse, dcat)


GELU_C0 = 0.7978845608028654
GELU_C1 = 0.044715


def _softplus_neg(lam):
    y = jnp.exp(-jnp.abs(lam))
    w = 1.0 + y
    log1p = jnp.where(w == 1.0, y, jnp.log(w) * (y / jnp.where(w == 1.0, 1.0, w - 1.0)))
    return jnp.maximum(-lam, 0.0) + log1p


def _down(cur, prev, k, row):
    if k == 0:
        return cur
    return jnp.where(row < k, pltpu.roll(prev, k, 0), pltpu.roll(cur, k, 0))


def _up(cur, nxt, k, row, tt):
    if k == 0:
        return cur
    return jnp.where(row >= tt - k, pltpu.roll(nxt, tt - k, 0), pltpu.roll(cur, tt - k, 0))


def _lru_gates(x, xp, cw, cb, wr, br, wi, bi, lam, row):
    shifts = [_down(x, xp, k, row) for k in range(CONV_WIDTH)]
    xc = cb
    for j in range(CONV_WIDTH):
        xc = xc + cw[j:j + 1, :] * shifts[CONV_WIDTH - 1 - j]
    xcb = xc.astype(BF16)
    r = jax.nn.sigmoid(jnp.dot(xcb, wr, preferred_element_type=F32) + br)
    i = jax.nn.sigmoid(jnp.dot(xcb, wi, preferred_element_type=F32) + bi)
    c = -LRU_C * _softplus_neg(lam)
    la = c * r
    a = jnp.exp(la)
    mult = jnp.sqrt(jnp.tanh(-la) * (a * a + 1.0))
    return shifts, xc, xcb, r, i, c, a, mult


def _lru_fwd(proj, cw, cb, wr, br, wi, bi, lam, *, tt, name):
    s = proj.shape[0]
    nblk = wr.shape[0]
    c = nblk * LANES
    tt = min(tt, s)
    xcol0 = (N_PATTERNS + 2) * N_KV_HEADS
    ycol0 = xcol0 + nblk

    def body(x_ref, y_ref, cw_ref, cb_ref, wr_ref, br_ref, wi_ref, bi_ref, lam_ref, rec_ref, h_ref, xprev, hc):
        n = pl.program_id(1)

        @pl.when(n == 0)
        def _():
            xprev[...] = jnp.zeros_like(xprev)
            hc[...] = jnp.zeros_like(hc)

        row = lax.broadcasted_iota(jnp.int32, (tt, LANES), 0)
        x = x_ref[...]
        _, xc, _, _, i, _, a, mult = _lru_gates(
            x, xprev[...], cw_ref[...], cb_ref[...], wr_ref[0].astype(BF16), br_ref[...],
            wi_ref[0].astype(BF16), bi_ref[...], lam_ref[...], row)
        av, bv = a, mult * (i * xc)
        k = 1
        while k < tt:
            bs = jnp.where(row < k, 0.0, pltpu.roll(bv, k, 0))
            as_ = jnp.where(row < k, 1.0, pltpu.roll(av, k, 0))
            bv = bv + av * bs
            av = av * as_
            k *= 2
        h = bv + av * hc[0:1, :]
        hc[...] = jnp.broadcast_to(h[tt - 1:tt, :], hc.shape)
        h_ref[...] = h
        y = y_ref[...]
        gel = 0.5 * y * (1.0 + jnp.tanh(GELU_C0 * (y + GELU_C1 * y * y * y)))
        rec_ref[...] = (h * gel).astype(BF16)
        xprev[...] = x

    vec = pl.BlockSpec((1, LANES), lambda b, n: (0, b))
    wblk = pl.BlockSpec((1, LANES, LANES), lambda b, n: (b, 0, 0))
    out = pl.BlockSpec((tt, LANES), lambda b, n: (n, b))
    return pl.pallas_call(
        body, name=name,
        out_shape=(jax.ShapeDtypeStruct((s, c), BF16), jax.ShapeDtypeStruct((s, c), F32)),
        grid=(nblk, s // tt),
        in_specs=[pl.BlockSpec((tt, LANES), lambda b, n: (n, xcol0 + b)),
                  pl.BlockSpec((tt, LANES), lambda b, n: (n, ycol0 + b)),
                  pl.BlockSpec((CONV_WIDTH, LANES), lambda b, n: (0, b)), vec, wblk, vec, wblk, vec, vec],
        out_specs=(out, out),
        scratch_shapes=[pltpu.VMEM((tt, LANES), F32), pltpu.VMEM((8, LANES), F32)],
        compiler_params=_cp(("parallel", "arbitrary"), 32),
    )(proj, proj, cw, cb, wr, br, wi, bi, lam)


def _lru_bwd(proj, hseq, dcat, cw, cb, wr, br, wi, bi, lam, *, tt, name):
    s = proj.shape[0]
    nblk = wr.shape[0]
    c = nblk * LANES
    tt = min(tt, s)
    nt = s // tt
    xcol0 = (N_PATTERNS + 2) * N_KV_HEADS
    ycol0 = xcol0 + nblk
    rcol0 = N_KV_HEADS

    def body(x_ref, xp_ref, y_ref, h_ref, hp_ref, dr_ref, cw_ref, cb_ref, wr_ref, br_ref, wi_ref, bi_ref, lam_ref,
             dx_ref, dy_ref, dcw_ref, dcb_ref, dwr_ref, dbr_ref, dwi_ref, dbi_ref, dlam_ref, dxc_next, gcar, acar):
        n = pl.program_id(1)
        rt = nt - 1 - n

        @pl.when(n == 0)
        def _():
            for ref in (dxc_next, gcar, acar, dcw_ref, dcb_ref, dwr_ref, dbr_ref, dwi_ref, dbi_ref, dlam_ref):
                ref[...] = jnp.zeros_like(ref)

        row = lax.broadcasted_iota(jnp.int32, (tt, LANES), 0)
        x = x_ref[...]
        xp = jnp.where(rt > 0, xp_ref[...], 0.0)
        cwv = cw_ref[...]
        wrb, wib = wr_ref[0].astype(BF16), wi_ref[0].astype(BF16)
        lam_v = lam_ref[...]
        shifts, xc, xcb, r, i, cc, a, mult = _lru_gates(x, xp, cwv, cb_ref[...], wrb, br_ref[...], wib, bi_ref[...],
                                                        lam_v, row)
        h = h_ref[...]
        hp_last = jnp.where(rt > 0, hp_ref[7:8, :], 0.0)
        hprev = jnp.where(row < 1, hp_last, pltpu.roll(h, 1, 0))
        y = y_ref[...]
        y2 = y * y
        th = jnp.tanh(GELU_C0 * (y + GELU_C1 * y2 * y))
        gel = 0.5 * y * (1.0 + th)
        dgel = 0.5 * (1.0 + th) + 0.5 * y * (1.0 - th * th) * GELU_C0 * (1.0 + 3.0 * GELU_C1 * y2)
        drec = dr_ref[...]
        dy_ref[...] = (drec * h * dgel).astype(BF16)
        av = jnp.where(row >= tt - 1, acar[0:1, :], pltpu.roll(a, tt - 1, 0))
        bv = drec * gel
        k = 1
        while k < tt:
            bs = jnp.where(row >= tt - k, 0.0, pltpu.roll(bv, tt - k, 0))
            as_ = jnp.where(row >= tt - k, 1.0, pltpu.roll(av, tt - k, 0))
            bv = bv + av * bs
            av = av * as_
            k *= 2
        g = bv + av * gcar[0:1, :]
        gcar[...] = jnp.broadcast_to(g[0:1, :], gcar.shape)
        acar[...] = jnp.broadcast_to(a[0:1, :], acar.shape)
        da = g * hprev
        d_ixc = g * mult
        dmult = g * (i * xc)
        di = d_ixc * xc
        dxc = d_ixc * i
        a2 = a * a
        dla = da * a - dmult * (a2 / mult)
        dr = dla * cc
        dsp = jnp.sum(dla * r, axis=0, keepdims=True) * (-LRU_C)
        dlam_ref[...] += dsp * (-jax.nn.sigmoid(-lam_v))
        dzr = dr * r * (1.0 - r)
        dzi = di * i * (1.0 - i)
        dbr_ref[...] += jnp.sum(dzr, axis=0, keepdims=True)
        dbi_ref[...] += jnp.sum(dzi, axis=0, keepdims=True)
        dzrb, dzib = dzr.astype(BF16), dzi.astype(BF16)
        tn = (((0,), (0,)), ((), ()))
        ntd = (((1,), (1,)), ((), ()))
        dwr_ref[0] += lax.dot_general(xcb, dzrb, tn, preferred_element_type=F32)
        dwi_ref[0] += lax.dot_general(xcb, dzib, tn, preferred_element_type=F32)
        dxc = (dxc + lax.dot_general(dzrb, wrb, ntd, preferred_element_type=F32)
               + lax.dot_general(dzib, wib, ntd, preferred_element_type=F32))
        dcb_ref[...] += jnp.sum(dxc, axis=0, keepdims=True)
        dcw_ref[...] += jnp.concatenate(
            [jnp.sum(dxc * shifts[CONV_WIDTH - 1 - j], axis=0, keepdims=True) for j in range(CONV_WIDTH)], axis=0)
        nxt = dxc_next[...]
        dx = cwv[0:1, :] * _up(dxc, nxt, CONV_WIDTH - 1, row, tt)
        for j in range(1, CONV_WIDTH):
            dx = dx + cwv[j:j + 1, :] * _up(dxc, nxt, CONV_WIDTH - 1 - j, row, tt)
        dx_ref[...] = dx.astype(BF16)
        dxc_next[...] = dxc

    def tile(col0, prev=False):
        if prev:
            return pl.BlockSpec((tt, LANES), lambda b, n: (jnp.maximum(nt - 2 - n, 0), col0 + b))
        return pl.BlockSpec((tt, LANES), lambda b, n: (nt - 1 - n, col0 + b))

    vec = pl.BlockSpec((1, LANES), lambda b, n: (0, b))
    wblk = pl.BlockSpec((1, LANES, LANES), lambda b, n: (b, 0, 0))
    cwblk = pl.BlockSpec((CONV_WIDTH, LANES), lambda b, n: (0, b))
    hp8 = pl.BlockSpec((8, LANES), lambda b, n: (jnp.maximum((nt - 1 - n) * (tt // 8) - 1, 0), b))
    vshape = jax.ShapeDtypeStruct((1, c), F32)
    wshape = jax.ShapeDtypeStruct((nblk, LANES, LANES), F32)
    return pl.pallas_call(
        body, name=name,
        out_shape=(jax.ShapeDtypeStruct((s, c), BF16), jax.ShapeDtypeStruct((s, c), BF16),
                   jax.ShapeDtypeStruct((CONV_WIDTH, c), F32), vshape, wshape, vshape, wshape, vshape, vshape),
        grid=(nblk, nt),
        in_specs=[tile(xcol0), tile(xcol0, True), tile(ycol0), tile(0), hp8, tile(rcol0),
                  cwblk, vec, wblk, vec, wblk, vec, vec],
        out_specs=(tile(0), tile(0), cwblk, vec, wblk, vec, wblk, vec, vec),
        scratch_shapes=[pltpu.VMEM((tt, LANES), F32), pltpu.VMEM((8, LANES), F32), pltpu.VMEM((8, LANES), F32)],
        compiler_params=_cp(("parallel", "arbitrary"), 32),
    )(proj, proj, proj, hseq, hseq, dcat, cw, cb, wr, br, wi, bi, lam)


ROW_BLOCKS = (256, 128, 64, 32, 16, 8)


def _adamw(w, m, v, gparts, *, name):
    r, c = w.shape
    npart = gparts.shape[0]
    br = _pick(r, ROW_BLOCKS)
    c1 = 1.0 - ADAM_B1 ** ADAM_STEP
    c2 = 1.0 - ADAM_B2 ** ADAM_STEP

    def body(w_ref, m_ref, v_ref, g_ref, go_ref, d_ref, mo_ref, vo_ref):
        g = g_ref[0]
        for q in range(1, npart):
            g = g + g_ref[q]
        mn = ADAM_B1 * m_ref[...] + (1.0 - ADAM_B1) * g
        vn = ADAM_B2 * v_ref[...] + (1.0 - ADAM_B2) * (g * g)
        go_ref[...] = g
        mo_ref[...] = mn
        vo_ref[...] = vn
        d_ref[...] = -ADAM_LR * ((mn / c1) / (jnp.sqrt(vn / c2) + ADAM_EPS) + ADAM_WD * w_ref[...])

    blk = pl.BlockSpec((br, c), lambda i: (i, 0))
    out = jax.ShapeDtypeStruct((r, c), F32)
    return pl.pallas_call(
        body, name=name, out_shape=(out, out, out, out), grid=(r // br,),
        in_specs=[blk, blk, blk, pl.BlockSpec((npart, br, c), lambda i: (0, i, 0))],
        out_specs=(blk, blk, blk, blk),
        compiler_params=_cp(("parallel",), 48),
    )(w, m, v, gparts)


def _sum_parts(parts, *, name):
    npart, r, c = parts.shape
    br = _pick(r, ROW_BLOCKS)

    def body(p_ref, o_ref):
        acc = p_ref[0]
        for q in range(1, npart):
            acc = acc + p_ref[q]
        o_ref[...] = acc

    return pl.pallas_call(
        body, name=name, out_shape=jax.ShapeDtypeStruct((r, c), F32), grid=(r // br,),
        in_specs=[pl.BlockSpec((npart, br, c), lambda i: (0, i, 0))],
        out_specs=pl.BlockSpec((br, c), lambda i: (i, 0)),
        compiler_params=_cp(("parallel",), 48),
    )(parts)


def _pair_add(g8, buf, core, *, name):
    _, r, c = g8.shape
    br = _pick(r, ROW_BLOCKS)

    def body(core_ref, g_ref, b_ref, o_ref):
        o_ref[...] = g_ref[...] + b_ref[...]

    return pl.pallas_call(
        body, name=name, out_shape=jax.ShapeDtypeStruct((4, r, c), F32),
        grid_spec=pltpu.PrefetchScalarGridSpec(
            num_scalar_prefetch=1, grid=(4, r // br),
            in_specs=[pl.BlockSpec((1, br, c), lambda q, i, core_ref: (2 * q + core_ref[0], i, 0)),
                      pl.BlockSpec((1, br, c), lambda q, i, core_ref: (q, i, 0))],
            out_specs=pl.BlockSpec((1, br, c), lambda q, i, core_ref: (q, i, 0))),
        compiler_params=_cp(("parallel", "parallel"), 48),
    )(core, g8, buf)


HBM = pl.BlockSpec(memory_space=pltpu.HBM)


def _mesh_pos():
    return lax.axis_index("x"), lax.axis_index("y"), lax.axis_index("c")


def _all_gather(shards, *, name):
    na = len(shards)

    def body(*refs):
        x_refs, out_refs = refs[:na], refs[na:2 * na]
        send_sems, recv_sems, local_sems = refs[2 * na:]
        x, y, c = _mesh_pos()
        me, sibling = (x, y, c), (x, y, 1 - c)
        chips = [(1 - x, y), (x, 1 - y), (1 - x, 1 - y)]

        def copy(a, k, block, to, src=None):
            px, py, pc = block
            dst = out_refs[a].at[4 * px + 2 * py + pc]
            return pltpu.make_async_remote_copy(
                src_ref=dst if src is None else src, dst_ref=dst,
                send_sem=send_sems.at[a, k], recv_sem=recv_sems.at[a, k],
                device_id=to, device_id_type=MESH)

        mine, first, passed = [], [], []
        for a in range(na):
            cp = pltpu.make_async_copy(x_refs[a], out_refs[a].at[4 * x + 2 * y + c], local_sems.at[a])
            cp.start()
            mine.append(cp)
            sends = [copy(a, 0, me, sibling, src=x_refs[a])]
            sends += [copy(a, 1 + j, me, (*chip, c), src=x_refs[a]) for j, chip in enumerate(chips)]
            for cp in sends:
                cp.start()
            first += sends
        for j, chip in enumerate(chips):
            for a in range(na):
                copy(a, 1 + j, (*chip, c), me).wait_recv()
                cp = copy(a, 4 + j, (*chip, c), sibling)
                cp.start()
                passed.append(cp)
        for a in range(na):
            copy(a, 0, sibling, me).wait_recv()
            for j, chip in enumerate(chips):
                copy(a, 4 + j, (*chip, 1 - c), me).wait_recv()
        for cp in first + passed:
            cp.wait_send()
        for cp in mine:
            cp.wait()

    return pl.pallas_call(
        body, name=name,
        out_shape=tuple(jax.ShapeDtypeStruct((N_DEV,) + a.shape, a.dtype) for a in shards),
        in_specs=[HBM] * na, out_specs=tuple([HBM] * na),
        scratch_shapes=[pltpu.SemaphoreType.DMA((na, 7)), pltpu.SemaphoreType.DMA((na, 7)),
                        pltpu.SemaphoreType.DMA((na,))],
    )(*shards)


def _sibling_exchange(g8s, *, name):
    na = len(g8s)

    def body(*refs):
        g_refs, buf_refs = refs[:na], refs[na:2 * na]
        send_sems, recv_sems = refs[2 * na:]
        x, y, c = _mesh_pos()
        copies = []
        for a in range(na):
            for q in range(4):
                cp = pltpu.make_async_remote_copy(
                    src_ref=g_refs[a].at[2 * q + (1 - c)], dst_ref=buf_refs[a].at[q],
                    send_sem=send_sems.at[a, q], recv_sem=recv_sems.at[a, q],
                    device_id=(x, y, 1 - c), device_id_type=MESH)
                cp.start()
                copies.append(cp)
        for cp in copies:
            cp.wait()

    return pl.pallas_call(
        body, name=name,
        out_shape=tuple(jax.ShapeDtypeStruct((4,) + g.shape[1:], g.dtype) for g in g8s),
        in_specs=[HBM] * na, out_specs=tuple([HBM] * na),
        scratch_shapes=[pltpu.SemaphoreType.DMA((na, 4)), pltpu.SemaphoreType.DMA((na, 4))],
    )(*g8s)


def _chip_exchange(parts, *, name):
    na = len(parts)

    def body(*refs):
        p_refs, buf_refs = refs[:na], refs[na:2 * na]
        send_sems, recv_sems, local_sems = refs[2 * na:]
        x, y, c = _mesh_pos()
        chips = [(1 - x, y), (x, 1 - y), (1 - x, 1 - y)]
        my_chip = 2 * x + y

        def copy(a, j, slot):
            px, py = chips[j]
            return pltpu.make_async_remote_copy(
                src_ref=p_refs[a].at[2 * px + py], dst_ref=buf_refs[a].at[slot],
                send_sem=send_sems.at[a, j], recv_sem=recv_sems.at[a, j],
                device_id=(px, py, c), device_id_type=MESH)

        sends, local = [], []
        for a in range(na):
            cp = pltpu.make_async_copy(p_refs[a].at[my_chip], buf_refs[a].at[my_chip], local_sems.at[a])
            cp.start()
            local.append(cp)
            for j in range(3):
                cp = copy(a, j, my_chip)
                cp.start()
                sends.append(cp)
        for a in range(na):
            for j, (px, py) in enumerate(chips):
                copy(a, j, 2 * px + py).wait_recv()
        for cp in sends:
            cp.wait_send()
        for cp in local:
            cp.wait()

    return pl.pallas_call(
        body, name=name,
        out_shape=tuple(jax.ShapeDtypeStruct(p.shape, p.dtype) for p in parts),
        in_specs=[HBM] * na, out_specs=tuple([HBM] * na),
        scratch_shapes=[pltpu.SemaphoreType.DMA((na, 3)), pltpu.SemaphoreType.DMA((na, 3)),
                        pltpu.SemaphoreType.DMA((na,))],
    )(*parts)


BIG_WEIGHTS = ("ffn1_w_gate", "ffn1_w_up", "ffn1_w_down", "w_in", "w_out",
               "ffn2_w_gate", "ffn2_w_up", "ffn2_w_down", "w_ple_proj", "w_ple_gate")
COLUMN_SHARDED = ("ffn1_w_gate", "ffn1_w_up", "w_in", "ffn2_w_gate", "ffn2_w_up", "w_ple_proj", "conv_w")
SMALL_WEIGHTS = ("ln1_g", "ln1_b", "conv_b", "w_rgate", "b_rgate", "w_igate", "b_igate", "lru_lambda",
                 "ln2_g", "ln2_b", "ln3_g", "ln3_b")

BM_CANDIDATES = (1024, 512, 256, 128)
BN_CANDIDATES = (1408, 1024, 896, 512, 256, 128)


def _dw(a, b, *, scale=1.0, name):
    m, n = a.shape[1], b.shape[1]
    return _mm(a, b, ta=True, scale=scale, bm=_pick(m, BN_CANDIDATES), bn=_pick(n, BN_CANDIDATES), bk=1024, name=name)


def _ffn_fwd(xb, x, wg, wu, wd, gamma, beta, tag):
    f = wg.shape[1]
    g, u, h = _ffn_up(xb, wg, wu, bm=1024, bn=_pick(f, (512, 256, 128)), name=f"{tag}_up")
    y, yb, xh, rs = _mm_ln(h, wd, x, gamma, beta, res_scale=DEEPNORM_ALPHA, mm_scale=0.5,
                           bm=512, bk=_pick(f, (1408, 512, 256, 128)), name=f"{tag}_down_ln")
    return (g, u, h, xh, rs), y, yb


def _ffn_bwd(saved, xb_in, dz, dzb, wg, wu, wd, ln_in, tag):
    g, u, h, _, _ = saved
    f = wg.shape[1]
    dg, du = _ffn_bwd_dh(dzb, wd, g, u, scale=0.5, bm=1024, bn=_pick(f, (512, 256, 128)), name=f"{tag}_dh")
    dwd = _dw(h, dzb, scale=0.5, name=f"{tag}_dwd")
    dwg = _dw(xb_in, dg, name=f"{tag}_dwg")
    dwu = _dw(xb_in, du, name=f"{tag}_dwu")
    out = _mm_dx([(dg, wg), (du, wu)], dz, extra_scale=DEEPNORM_ALPHA, ln=ln_in,
                 bm=512, bk=_pick(f, (512, 256, 128)), name=f"{tag}_dx")
    return (dwg, dwu, dwd), out


def _local_step(x, p, target, positions, w):
    s, d = x.shape
    tabs = _rope_tables(positions)
    xb = x.astype(BF16)
    sv1, x1, x1b = _ffn_fwd(xb, x, w["ffn1_w_gate"], w["ffn1_w_up"], w["ffn1_w_down"], w["ln1_g"], w["ln1_b"], "ffn1")
    pw = w["w_in"].shape[1]
    proj = _mm(x1b, w["w_in"], bm=1024, bn=_pick(pw, BN_CANDIDATES), bk=d, name="in_proj")
    nqk = (N_PATTERNS + 1) * N_KV_HEADS
    qkr = _rotary(proj, tabs, n_cols=nqk, inverse=False, out_dtype=F32, bs=1024, name="rotary")
    attn, lse = _attn_fwd(qkr, proj, name="attn_fwd")
    lru_w = (w["conv_w"], w["conv_b"], w["w_rgate"], w["b_rgate"], w["w_igate"], w["b_igate"], w["lru_lambda"])
    rec, hseq = _lru_fwd(proj, *lru_w, tt=512, name="lru_fwd")
    cat = jnp.concatenate([attn.astype(BF16), rec], axis=1)
    x2, x2b, xh2, rs2 = _mm_ln(cat, w["w_out"], x1, w["ln2_g"], w["ln2_b"], res_scale=DEEPNORM_ALPHA, mm_scale=1.0,
                               bm=512, bk=1024, name="out_proj_ln")
    sv3, x3, x3b = _ffn_fwd(x2b, x2, w["ffn2_w_gate"], w["ffn2_w_up"], w["ffn2_w_down"], w["ln3_g"], w["ln3_b"], "ffn2")
    lsum, dy, dgate, dple = _ple_loss(x3, x3b, p, w["w_ple_gate"], w["w_ple_proj"], target,
                                      bm=1024, bn=_pick(d, (512, 256, 128)), name="ple_loss")
    grads = {}
    grads["w_ple_gate"] = _dw(x3b, dgate, name="dw_ple_gate")
    grads["w_ple_proj"] = _dw(p, dple, name="dw_ple_proj")
    dz3, dz3b, grads["ln3_g"], grads["ln3_b"] = _mm_dx(
        [(dgate, w["w_ple_gate"])], dy, extra_scale=1.0, ln=(sv3[3], sv3[4], w["ln3_g"]), bm=512, bk=1024, name="ple_dx")
    (grads["ffn2_w_gate"], grads["ffn2_w_up"], grads["ffn2_w_down"]), (dz2, dz2b, grads["ln2_g"], grads["ln2_b"]) = _ffn_bwd(
        sv3, x2b, dz3, dz3b, w["ffn2_w_gate"], w["ffn2_w_up"], w["ffn2_w_down"], (xh2, rs2, w["ln2_g"]), "ffn2")
    grads["w_out"] = _dw(cat, dz2b, name="dw_out")
    dcat = _mm(dz2b, w["w_out"], tb=True, bm=1024, bn=1024, bk=d, name="out_proj_dx")
    dq0, dq1, dq2, dk, dvb = _attn_bwd(qkr, proj, attn, lse, dcat, name="attn_bwd")
    nh = N_KV_HEADS
    dqkv = [_rotary(t, tabs, n_cols=nh, inverse=True, out_dtype=BF16, bs=1024, name=f"rotary_bwd{i}")
            for i, t in enumerate((dq0, dq1, dq2, dk))]
    (dxb, dyb, grads["conv_w"], grads["conv_b"], grads["w_rgate"], grads["b_rgate"], grads["w_igate"],
     grads["b_igate"], grads["lru_lambda"]) = _lru_bwd(proj, hseq, dcat, *lru_w, tt=512, name="lru_bwd")
    dproj = jnp.concatenate(dqkv + [dvb, dxb, dyb], axis=1)
    grads["w_in"] = _dw(x1b, dproj, name="dw_in")
    dz1, dz1b, grads["ln1_g"], grads["ln1_b"] = _mm_dx(
        [(dproj, w["w_in"])], dz2, extra_scale=DEEPNORM_ALPHA, ln=(sv1[3], sv1[4], w["ln1_g"]),
        bm=512, bk=_pick(pw, (1408, 1024, 896, 512, 256, 128)), name="in_proj_dx")
    (grads["ffn1_w_gate"], grads["ffn1_w_up"], grads["ffn1_w_down"]), grad_x = _ffn_bwd(
        sv1, xb, dz1, dz1b, w["ffn1_w_gate"], w["ffn1_w_up"], w["ffn1_w_down"], None, "ffn1")
    return lsum, grad_x, grads


def _to_full(name, gathered):
    if name in COLUMN_SHARDED:
        _, r, c = gathered.shape
        return jnp.transpose(gathered, (1, 0, 2)).reshape(r, N_DEV * c)
    return gathered.reshape((N_DEV * gathered.shape[1],) + gathered.shape[2:])


def _to_owner_blocks(name, full):
    if name in COLUMN_SHARDED:
        r, c = full.shape
        return jnp.transpose(full.reshape(r, N_DEV, c // N_DEV), (1, 0, 2))
    return full.reshape((N_DEV, full.shape[0] // N_DEV) + full.shape[1:])


def _rows128(a):
    flat = a.reshape(-1, LANES)
    pad = (-flat.shape[0]) % 8
    return jnp.pad(flat, ((0, pad), (0, 0))) if pad else flat


def kernel(x, p, positions, ffn1_w_gate, ffn1_w_up, ffn1_w_down, ln1_g, ln1_b, w_in, conv_w, conv_b, w_rgate, b_rgate, w_igate, b_igate, lru_lambda, w_out, ln2_g, ln2_b, ffn2_w_gate, ffn2_w_up, ffn2_w_down, ln3_g, ln3_b, w_ple_proj, w_ple_gate, loss_target, m_ffn1_w_gate, m_ffn1_w_up, m_ffn1_w_down, m_ln1_g, m_ln1_b, m_w_in, m_conv_w, m_conv_b, m_w_rgate, m_b_rgate, m_w_igate, m_b_igate, m_lru_lambda, m_w_out, m_ln2_g, m_ln2_b, m_ffn2_w_gate, m_ffn2_w_up, m_ffn2_w_down, m_ln3_g, m_ln3_b, m_w_ple_proj, m_w_ple_gate, v_ffn1_w_gate, v_ffn1_w_up, v_ffn1_w_down, v_ln1_g, v_ln1_b, v_w_in, v_conv_w, v_conv_b, v_w_rgate, v_b_rgate, v_w_igate, v_b_igate, v_lru_lambda, v_w_out, v_ln2_g, v_ln2_b, v_ffn2_w_gate, v_ffn2_w_up, v_ffn2_w_down, v_ln3_g, v_ln3_b, v_w_ple_proj, v_w_ple_gate):
    names = ("ffn1_w_gate", "ffn1_w_up", "ffn1_w_down", "ln1_g", "ln1_b", "w_in", "conv_w", "conv_b", "w_rgate",
             "b_rgate", "w_igate", "b_igate", "lru_lambda", "w_out", "ln2_g", "ln2_b", "ffn2_w_gate", "ffn2_w_up",
             "ffn2_w_down", "ln3_g", "ln3_b", "w_ple_proj", "w_ple_gate")
    ws = (ffn1_w_gate, ffn1_w_up, ffn1_w_down, ln1_g, ln1_b, w_in, conv_w, conv_b, w_rgate, b_rgate, w_igate, b_igate,
          lru_lambda, w_out, ln2_g, ln2_b, ffn2_w_gate, ffn2_w_up, ffn2_w_down, ln3_g, ln3_b, w_ple_proj, w_ple_gate)
    ms = (m_ffn1_w_gate, m_ffn1_w_up, m_ffn1_w_down, m_ln1_g, m_ln1_b, m_w_in, m_conv_w, m_conv_b, m_w_rgate, m_b_rgate,
          m_w_igate, m_b_igate, m_lru_lambda, m_w_out, m_ln2_g, m_ln2_b, m_ffn2_w_gate, m_ffn2_w_up, m_ffn2_w_down,
          m_ln3_g, m_ln3_b, m_w_ple_proj, m_w_ple_gate)
    vs = (v_ffn1_w_gate, v_ffn1_w_up, v_ffn1_w_down, v_ln1_g, v_ln1_b, v_w_in, v_conv_w, v_conv_b, v_w_rgate, v_b_rgate,
          v_w_igate, v_b_igate, v_lru_lambda, v_w_out, v_ln2_g, v_ln2_b, v_ffn2_w_gate, v_ffn2_w_up, v_ffn2_w_down,
          v_ln3_g, v_ln3_b, v_w_ple_proj, v_w_ple_gate)
    def local(a):
        return a[0] if a.ndim >= 3 else a

    w_loc = {n: local(a) for n, a in zip(names, ws)}
    m_loc = {n: local(a) for n, a in zip(names, ms)}
    v_loc = {n: local(a) for n, a in zip(names, vs)}
    out_shapes = {n: a.shape for n, a in zip(names, ws)}

    shards = [w_loc[n].astype(BF16) for n in BIG_WEIGHTS] + [w_loc["conv_w"]]
    gathered = _all_gather(shards, name="gather_weights")
    full = {n: _to_full(n, g) for n, g in zip(BIG_WEIGHTS + ("conv_w",), gathered)}
    for n in SMALL_WEIGHTS:
        full[n] = w_loc[n]

    lsum, grad_x, grads = _local_step(x[0], p[0, 0], loss_target[0], positions[0], full)
    d_model = x.shape[-1]
    loss = lax.psum(lsum[0, 0] * (0.5 / d_model), ("x", "y", "c"))

    core = lax.axis_index("c").astype(jnp.int32).reshape(1)
    g8 = [_to_owner_blocks(n, grads[n]) for n in BIG_WEIGHTS]
    sib = _sibling_exchange(g8, name="rs_sibling")
    chip_parts = [_pair_add(a, b, core, name=f"rs_add_{n}") for n, a, b in zip(BIG_WEIGHTS, g8, sib)]
    reduced = dict(zip(BIG_WEIGHTS, _chip_exchange(chip_parts, name="rs_chips")))

    small = SMALL_WEIGHTS + ("conv_w",)
    packed = jnp.concatenate([_rows128(grads[n]) for n in small], axis=0)
    all_packed, = _all_gather([packed], name="gather_small_grads")
    summed = _sum_parts(all_packed, name="sum_small_grads")
    small_grads, row = {}, 0
    for n in small:
        rows = grads[n].size // LANES
        small_grads[n] = summed[row:row + rows].reshape(grads[n].shape)
        row += rows + (-rows) % 8
    me = 4 * lax.axis_index("x") + 2 * lax.axis_index("y") + lax.axis_index("c")
    cw_cols = w_loc["conv_w"].shape[1]
    small_grads["conv_w"] = lax.dynamic_slice_in_dim(small_grads["conv_w"], me * cw_cols, cw_cols, axis=1)

    out_g, out_d, out_m, out_v = {}, {}, {}, {}
    for n in names:
        wl, ml, vl = w_loc[n], m_loc[n], v_loc[n]
        shape = wl.shape
        if n in BIG_WEIGHTS:
            gparts = reduced[n]
        else:
            gparts = small_grads[n].reshape((1,) + shape)
        if wl.ndim == 3:
            wl, ml, vl = (t.reshape(-1, shape[-1]) for t in (wl, ml, vl))
            gparts = gparts.reshape(gparts.shape[0], -1, shape[-1])
        res = _adamw(wl, ml, vl, gparts, name=f"adamw_{n}")
        out_g[n], out_d[n], out_m[n], out_v[n] = (t.reshape(out_shapes[n]) for t in res)

    return (loss, grad_x[None], *[out_g[n] for n in names], *[out_d[n] for n in names],
            *[out_m[n] for n in names], *[out_v[n] for n in names])
```

```python
import jax
import jax.numpy as jnp
from jax import lax
from jax.experimental import pallas as pl
from jax.experimental.pallas import tpu as pltpu

F32 = jnp.float32
BF16 = jnp.bfloat16

N_DEV = 8
LANES = 128
MIB = 1 << 20

HEAD_DIM = 128
N_KV_HEADS = 4
DILATIONS = (1, 4, 16)
N_PATTERNS = 3
SPAN = 128
ROT_DIMS = 32
ROPE_THETA = 500000.0
LRU_C = 8.0
CONV_WIDTH = 4
LN_EPS = 1e-5
DEEPNORM_ALPHA = 2.0 ** 0.25
ATTN_TILE = SPAN * DILATIONS[-1]

ADAM_LR = 0.001
ADAM_B1 = 0.9
ADAM_B2 = 0.999
ADAM_EPS = 1e-08
ADAM_WD = 0.01
ADAM_STEP = 10

MESH = pl.DeviceIdType.MESH
EPILOGUE_ROWS = 64


def _cp(semantics, vmem_mib):
    return pltpu.CompilerParams(dimension_semantics=semantics, vmem_limit_bytes=vmem_mib * MIB)


def _pick(n, candidates):
    for c in candidates:
        if n % c == 0:
            return c
    return n


def _mm(a, b, *, ta=False, tb=False, out_dtype=F32, scale=1.0, bm, bn, bk, name):
    m, k = (a.shape[1], a.shape[0]) if ta else a.shape
    n = b.shape[0] if tb else b.shape[1]
    bm, bn, bk = min(bm, m), min(bn, n), min(bk, k)
    assert m % bm == 0 and n % bn == 0 and k % bk == 0, (name, m, n, k, bm, bn, bk)
    nk = k // bk
    a_spec = pl.BlockSpec((bk, bm), lambda i, j, kk: (kk, i)) if ta else pl.BlockSpec((bm, bk), lambda i, j, kk: (i, kk))
    b_spec = pl.BlockSpec((bn, bk), lambda i, j, kk: (j, kk)) if tb else pl.BlockSpec((bk, bn), lambda i, j, kk: (kk, j))
    dn = (((0 if ta else 1,), (1 if tb else 0,)), ((), ()))

    def body(a_ref, b_ref, o_ref, *acc):
        part = lax.dot_general(a_ref[...].astype(BF16), b_ref[...].astype(BF16), dn, preferred_element_type=F32)
        if nk == 1:
            o_ref[...] = (part * scale).astype(out_dtype)
            return
        acc_ref, = acc
        kk = pl.program_id(2)

        @pl.when(kk == 0)
        def _():
            acc_ref[...] = part

        @pl.when(kk > 0)
        def _():
            acc_ref[...] += part

        @pl.when(kk == nk - 1)
        def _():
            o_ref[...] = (acc_ref[...] * scale).astype(out_dtype)

    return pl.pallas_call(
        body, name=name,
        out_shape=jax.ShapeDtypeStruct((m, n), out_dtype),
        grid=(m // bm, n // bn, nk),
        in_specs=[a_spec, b_spec],
        out_specs=pl.BlockSpec((bm, bn), lambda i, j, kk: (i, j)),
        scratch_shapes=[pltpu.VMEM((bm, bn), F32)] if nk > 1 else [],
        compiler_params=_cp(("parallel", "parallel", "arbitrary"), 56),
    )(a, b)


def _ffn_up(xb, wg, wu, *, bm, bn, name):
    s, d = xb.shape
    f = wg.shape[1]
    bm, bn = min(bm, s), min(bn, f)
    assert s % bm == 0 and f % bn == 0

    def body(x_ref, wg_ref, wu_ref, g_ref, u_ref, h_ref):
        x = x_ref[...]
        g = jnp.dot(x, wg_ref[...], preferred_element_type=F32)
        u = jnp.dot(x, wu_ref[...], preferred_element_type=F32)
        g_ref[...] = g.astype(BF16)
        u_ref[...] = u.astype(BF16)
        h_ref[...] = (g * jax.nn.sigmoid(g) * u).astype(BF16)

    out = jax.ShapeDtypeStruct((s, f), BF16)
    blk = pl.BlockSpec((bm, bn), lambda i, j: (i, j))
    return pl.pallas_call(
        body, name=name, out_shape=(out, out, out),
        grid=(s // bm, f // bn),
        in_specs=[pl.BlockSpec((bm, d), lambda i, j: (i, 0)),
                  pl.BlockSpec((d, bn), lambda i, j: (0, j)),
                  pl.BlockSpec((d, bn), lambda i, j: (0, j))],
        out_specs=(blk, blk, blk),
        compiler_params=_cp(("parallel", "arbitrary"), 56),
    )(xb, wg, wu)


def _ffn_bwd_dh(dzb, wd, g, u, *, scale, bm, bn, name):
    s, d = dzb.shape
    f = wd.shape[0]
    bm, bn = min(bm, s), min(bn, f)
    assert s % bm == 0 and f % bn == 0

    def body(dz_ref, wd_ref, g_ref, u_ref, dg_ref, du_ref):
        dh = lax.dot_general(dz_ref[...], wd_ref[...], (((1,), (1,)), ((), ())), preferred_element_type=F32) * scale
        gg = g_ref[...].astype(F32)
        sig = jax.nn.sigmoid(gg)
        silu = gg * sig
        dsilu = sig * (1.0 + gg * (1.0 - sig))
        dg_ref[...] = (dh * u_ref[...].astype(F32) * dsilu).astype(BF16)
        du_ref[...] = (dh * silu).astype(BF16)

    out = jax.ShapeDtypeStruct((s, f), BF16)
    blk = pl.BlockSpec((bm, bn), lambda i, j: (i, j))
    return pl.pallas_call(
        body, name=name, out_shape=(out, out),
        grid=(s // bm, f // bn),
        in_specs=[pl.BlockSpec((bm, d), lambda i, j: (i, 0)),
                  pl.BlockSpec((bn, d), lambda i, j: (j, 0)), blk, blk],
        out_specs=(blk, blk),
        compiler_params=_cp(("parallel", "arbitrary"), 56),
    )(dzb, wd, g, u)


def _mm_ln(a, b, res, gamma, beta, *, res_scale, mm_scale, bm, bk, name):
    s, k = a.shape
    d = b.shape[1]
    bm, bk = min(bm, s), min(bk, k)
    assert s % bm == 0 and k % bk == 0
    nk = k // bk
    ch = min(EPILOGUE_ROWS, bm)

    def body(a_ref, b_ref, r_ref, g_ref, be_ref, y_ref, yb_ref, xh_ref, rs_ref, acc_ref):
        kk = pl.program_id(1)
        part = jnp.dot(a_ref[...], b_ref[...], preferred_element_type=F32)

        @pl.when(kk == 0)
        def _():
            acc_ref[...] = part

        @pl.when(kk > 0)
        def _():
            acc_ref[...] += part

        @pl.when(kk == nk - 1)
        def _():
            def chunk(ci, carry):
                rows = pl.ds(pl.multiple_of(ci * ch, ch), ch)
                z = res_scale * r_ref[rows, :] + mm_scale * acc_ref[rows, :]
                mu = jnp.mean(z, axis=-1, keepdims=True)
                zc = z - mu
                var = jnp.mean(zc * zc, axis=-1, keepdims=True)
                rstd = lax.rsqrt(var + LN_EPS)
                xh = zc * rstd
                y = xh * g_ref[...] + be_ref[...]
                y_ref[rows, :] = y
                yb_ref[rows, :] = y.astype(BF16)
                xh_ref[rows, :] = xh
                rs_ref[rows, :] = rstd
                return carry

            lax.fori_loop(0, bm // ch, chunk, 0)

    row = pl.BlockSpec((bm, d), lambda i, kk: (i, 0))
    vec = pl.BlockSpec((1, d), lambda i, kk: (0, 0))
    return pl.pallas_call(
        body, name=name,
        out_shape=(jax.ShapeDtypeStruct((s, d), F32), jax.ShapeDtypeStruct((s, d), BF16),
                   jax.ShapeDtypeStruct((s, d), F32), jax.ShapeDtypeStruct((s, 1), F32)),
        grid=(s // bm, nk),
        in_specs=[pl.BlockSpec((bm, bk), lambda i, kk: (i, kk)),
                  pl.BlockSpec((bk, d), lambda i, kk: (kk, 0)), row, vec, vec],
        out_specs=(row, row, row, pl.BlockSpec((bm, 1), lambda i, kk: (i, 0))),
        scratch_shapes=[pltpu.VMEM((bm, d), F32)],
        compiler_params=_cp(("parallel", "arbitrary"), 56),
    )(a, b, res, gamma, beta)


def _mm_dx(pairs, extra, *, extra_scale, ln, bm, bk, name):
    s, k = pairs[0][0].shape
    d = pairs[0][1].shape[0]
    bm, bk = min(bm, s), min(bk, k)
    assert s % bm == 0 and k % bk == 0
    nk = k // bk
    npair = len(pairs)
    ch = min(EPILOGUE_ROWS, bm)

    def body(*refs):
        ab = refs[:2 * npair]
        e_ref = refs[2 * npair]
        pos = 2 * npair + 1
        if ln is not None:
            xh_ref, rs_ref, g_ref = refs[pos:pos + 3]
            dz_ref, dzb_ref, dg_ref, db_ref, acc_ref = refs[pos + 3:]
        else:
            dx_ref, acc_ref = refs[pos:]
        i = pl.program_id(0)
        kk = pl.program_id(1)
        part = None
        for p in range(npair):
            t = lax.dot_general(ab[2 * p][...], ab[2 * p + 1][...], (((1,), (1,)), ((), ())),
                                preferred_element_type=F32)
            part = t if part is None else part + t

        @pl.when(kk == 0)
        def _():
            acc_ref[...] = part

        @pl.when(kk > 0)
        def _():
            acc_ref[...] += part

        @pl.when(kk == nk - 1)
        def _():
            if ln is None:
                dx_ref[...] = extra_scale * e_ref[...] + acc_ref[...]
                return

            def chunk(ci, carry):
                dgp, dbp = carry
                rows = pl.ds(pl.multiple_of(ci * ch, ch), ch)
                dx = extra_scale * e_ref[rows, :] + acc_ref[rows, :]
                xh = xh_ref[rows, :]
                dxh = dx * g_ref[...]
                m1 = jnp.mean(dxh, axis=-1, keepdims=True)
                m2 = jnp.mean(dxh * xh, axis=-1, keepdims=True)
                dz = rs_ref[rows, :] * (dxh - m1 - xh * m2)
                dz_ref[rows, :] = dz
                dzb_ref[rows, :] = dz.astype(BF16)
                return dgp + jnp.sum(dx * xh, axis=0, keepdims=True), dbp + jnp.sum(dx, axis=0, keepdims=True)

            zero = jnp.zeros((1, d), F32)
            dgp, dbp = lax.fori_loop(0, bm // ch, chunk, (zero, zero))

            @pl.when(i == 0)
            def _():
                dg_ref[...] = dgp
                db_ref[...] = dbp

            @pl.when(i > 0)
            def _():
                dg_ref[...] += dgp
                db_ref[...] += dbp

    row = pl.BlockSpec((bm, d), lambda i, kk: (i, 0))
    vec = pl.BlockSpec((1, d), lambda i, kk: (0, 0))
    in_specs, args = [], []
    for a, w in pairs:
        in_specs += [pl.BlockSpec((bm, bk), lambda i, kk: (i, kk)), pl.BlockSpec((d, bk), lambda i, kk: (0, kk))]
        args += [a, w]
    in_specs.append(row)
    args.append(extra)
    if ln is not None:
        in_specs += [row, pl.BlockSpec((bm, 1), lambda i, kk: (i, 0)), vec]
        args += list(ln)
        out_shape = (jax.ShapeDtypeStruct((s, d), F32), jax.ShapeDtypeStruct((s, d), BF16),
                     jax.ShapeDtypeStruct((1, d), F32), jax.ShapeDtypeStruct((1, d), F32))
        out_specs = (row, row, vec, vec)
    else:
        out_shape = jax.ShapeDtypeStruct((s, d), F32)
        out_specs = row
    return pl.pallas_call(
        body, name=name, out_shape=out_shape,
        grid=(s // bm, nk), in_specs=in_specs, out_specs=out_specs,
        scratch_shapes=[pltpu.VMEM((bm, d), F32)],
        compiler_params=_cp(("arbitrary", "arbitrary"), 56),
    )(*args)


def _ple_loss(x3, x3b, p, wpg, wpp, target, *, bm, bn, name):
    s, d = x3.shape
    dp = p.shape[1]
    bm, bn = min(bm, s), min(bn, d)
    assert s % bm == 0 and d % bn == 0
    inv_d = 1.0 / d

    def body(x_ref, xb_ref, p_ref, wg_ref, wp_ref, t_ref, l_ref, dy_ref, dg_ref, dp_ref):
        first = (pl.program_id(0) == 0) & (pl.program_id(1) == 0)
        gp = jnp.dot(xb_ref[...], wg_ref[...], preferred_element_type=F32)
        pp = jnp.dot(p_ref[...].astype(BF16), wp_ref[...], preferred_element_type=F32)
        sig = jax.nn.sigmoid(gp)
        err = x_ref[...] + sig * pp - t_ref[...]
        part = jnp.sum(err * err)

        @pl.when(first)
        def _():
            l_ref[...] = jnp.zeros_like(l_ref)

        l_ref[...] += part
        dy = err * inv_d
        dy_ref[...] = dy
        dg_ref[...] = (dy * pp * sig * (1.0 - sig)).astype(BF16)
        dp_ref[...] = (dy * sig).astype(BF16)

    blk = pl.BlockSpec((bm, bn), lambda i, j: (i, j))
    return pl.pallas_call(
        body, name=name,
        out_shape=(jax.ShapeDtypeStruct((8, LANES), F32), jax.ShapeDtypeStruct((s, d), F32),
                   jax.ShapeDtypeStruct((s, d), BF16), jax.ShapeDtypeStruct((s, d), BF16)),
        grid=(s // bm, d // bn),
        in_specs=[blk, pl.BlockSpec((bm, d), lambda i, j: (i, 0)), pl.BlockSpec((bm, dp), lambda i, j: (i, 0)),
                  pl.BlockSpec((d, bn), lambda i, j: (0, j)), pl.BlockSpec((dp, bn), lambda i, j: (0, j)), blk],
        out_specs=(pl.BlockSpec((8, LANES), lambda i, j: (0, 0)), blk, blk, blk),
        compiler_params=_cp(("arbitrary", "arbitrary"), 56),
    )(x3, x3b, p, wpg, wpp, target)


def _rope_tables(positions):
    half = ROT_DIMS // 2
    inv_freq = jnp.power(jnp.float32(ROPE_THETA), -jnp.arange(half, dtype=F32) * (2.0 / ROT_DIMS))
    ang = positions.astype(F32)[:, None] * inv_freq
    cos, sin = jnp.cos(ang), jnp.sin(ang)
    s = positions.shape[0]
    zeros = jnp.zeros((s, half), F32)
    rest0 = jnp.zeros((s, HEAD_DIM - ROT_DIMS), F32)
    cf = jnp.concatenate([cos, cos, jnp.ones((s, HEAD_DIM - ROT_DIMS), F32)], axis=1)
    sa = jnp.concatenate([-sin, zeros, rest0], axis=1)
    sb = jnp.concatenate([zeros, sin, rest0], axis=1)
    return cf, sa, sb


def _rotary(t, tabs, *, n_cols, inverse, out_dtype, bs, name):
    s = t.shape[0]
    bs = min(bs, s)
    half = ROT_DIMS // 2

    def body(t_ref, cf_ref, sa_ref, sb_ref, o_ref):
        v = t_ref[...]
        if inverse:
            o = (v * cf_ref[...] + pltpu.roll(v * sa_ref[...], half, 1)
                 + pltpu.roll(v * sb_ref[...], HEAD_DIM - half, 1))
        else:
            o = (v * cf_ref[...] + pltpu.roll(v, HEAD_DIM - half, 1) * sa_ref[...]
                 + pltpu.roll(v, half, 1) * sb_ref[...])
        o_ref[...] = o.astype(out_dtype)

    blk = pl.BlockSpec((bs, HEAD_DIM), lambda i, j: (i, j))
    tab = pl.BlockSpec((bs, HEAD_DIM), lambda i, j: (i, 0))
    return pl.pallas_call(
        body, name=name, out_shape=jax.ShapeDtypeStruct((s, n_cols * HEAD_DIM), out_dtype),
        grid=(s // bs, n_cols), in_specs=[blk, tab, tab, tab], out_specs=blk,
        compiler_params=_cp(("parallel", "arbitrary"), 32),
    )(t, *tabs)


def _attn_blocks():
    out = []
    for g, dil in enumerate(DILATIONS):
        sup = SPAN * dil
        for j in range(ATTN_TILE // sup):
            for r in range(dil):
                out.append((g, j * sup + r, dil, (j - 1) * sup + r if j > 0 else None, ATTN_TILE - sup + r))
    return out


def _rows(ref, start, dil, lead=None):
    idx = pl.ds(start, SPAN, stride=dil) if dil > 1 else pl.ds(start, SPAN)
    return ref[idx, :] if lead is None else ref[lead, idx, :]


def _band_masks(n):
    qi = lax.broadcasted_iota(jnp.int32, (SPAN, 2 * SPAN), 0)
    ki = lax.broadcasted_iota(jnp.int32, (SPAN, 2 * SPAN), 1)
    band = (ki >= qi) & (ki <= qi + SPAN)
    return band, band & ((ki >= SPAN) | (n > 0))


def _attn_fwd(qkr, proj, *, name):
    s = qkr.shape[0]
    t = ATTN_TILE
    assert s % t == 0
    nt = s // t
    scale = HEAD_DIM ** -0.5
    kcol, vcol = N_PATTERNS * N_KV_HEADS, (N_PATTERNS + 1) * N_KV_HEADS
    blocks = _attn_blocks()

    def body(q0, q1, q2, kc_ref, kp_ref, vc_ref, vp_ref, o_ref, l_ref, og, lg):
        n = pl.program_id(1)
        band, band_first = _band_masks(n)
        q_refs = (q0, q1, q2)
        for g, start, dil, prev_in_tile, prev_start in blocks:
            q = _rows(q_refs[g], start, dil).astype(BF16)
            if prev_in_tile is not None:
                kp, vp, mask = _rows(kc_ref, prev_in_tile, dil), _rows(vc_ref, prev_in_tile, dil), band
            else:
                kp, vp, mask = _rows(kp_ref, prev_start, dil), _rows(vp_ref, prev_start, dil), band_first
            kk = jnp.concatenate([kp, _rows(kc_ref, start, dil)], axis=0).astype(BF16)
            vv = jnp.concatenate([vp, _rows(vc_ref, start, dil)], axis=0).astype(BF16)
            sc = lax.dot_general(q, kk, (((1,), (1,)), ((), ())), preferred_element_type=F32) * scale
            sc = jnp.where(mask, sc, -1e30)
            m = jnp.max(sc, axis=-1, keepdims=True)
            e = jnp.exp(sc - m)
            den = jnp.sum(e, axis=-1, keepdims=True)
            o = jnp.dot(e.astype(BF16), vv, preferred_element_type=F32) / den
            idx = pl.ds(start, SPAN, stride=dil) if dil > 1 else pl.ds(start, SPAN)
            og[g, idx, :] = o
            lg[g, idx, :] = jnp.broadcast_to(m + jnp.log(den), (SPAN, HEAD_DIM))
        l0, l1, l2 = lg[0], lg[1], lg[2]
        m = jnp.maximum(jnp.maximum(l0, l1), l2)
        w0, w1, w2 = jnp.exp(l0 - m), jnp.exp(l1 - m), jnp.exp(l2 - m)
        den = w0 + w1 + w2
        o_ref[...] = (w0 * og[0] + w1 * og[1] + w2 * og[2]) / den
        l_ref[...] = m + jnp.log(den)

    def col(c, prev=False):
        if prev:
            return pl.BlockSpec((t, HEAD_DIM), lambda h, n: (jnp.maximum(n - 1, 0), c + h))
        return pl.BlockSpec((t, HEAD_DIM), lambda h, n: (n, c + h))

    out = jax.ShapeDtypeStruct((s, N_KV_HEADS * HEAD_DIM), F32)
    return pl.pallas_call(
        body, name=name, out_shape=(out, out),
        grid=(N_KV_HEADS, nt),
        in_specs=[col(0), col(N_KV_HEADS), col(2 * N_KV_HEADS), col(kcol), col(kcol, True), col(vcol), col(vcol, True)],
        out_specs=(col(0), col(0)),
        scratch_shapes=[pltpu.VMEM((N_PATTERNS, t, HEAD_DIM), F32), pltpu.VMEM((N_PATTERNS, t, HEAD_DIM), F32)],
        compiler_params=_cp(("parallel", "arbitrary"), 48),
    )(qkr, qkr, qkr, qkr, qkr, proj, proj)


def _attn_bwd(qkr, proj, attn, lse, dcat, *, name):
    s = qkr.shape[0]
    t = ATTN_TILE
    nt = s // t
    scale = HEAD_DIM ** -0.5
    kcol, vcol = N_PATTERNS * N_KV_HEADS, (N_PATTERNS + 1) * N_KV_HEADS
    blocks = _attn_blocks()

    def body(q0, q1, q2, kc_ref, kp_ref, vc_ref, vp_ref, o_ref, l_ref, do_ref,
             dq0, dq1, dq2, dk_ref, dv_ref, ck, cv, tkc, tvc, tkp, tvp):
        n = pl.program_id(1)
        for ref in (tkc, tvc, tkp, tvp):
            ref[...] = jnp.zeros_like(ref)

        @pl.when(n < nt)
        def _():
            band, band_first = _band_masks(n)
            q_refs, dq_refs = (q0, q1, q2), (dq0, dq1, dq2)
            for g, start, dil, prev_in_tile, prev_start in blocks:
                idx = pl.ds(start, SPAN, stride=dil) if dil > 1 else pl.ds(start, SPAN)
                q = q_refs[g][idx, :].astype(BF16)
                if prev_in_tile is not None:
                    kp, vp, mask = _rows(kc_ref, prev_in_tile, dil), _rows(vc_ref, prev_in_tile, dil), band
                else:
                    kp, vp, mask = _rows(kp_ref, prev_start, dil), _rows(vp_ref, prev_start, dil), band_first
                kk = jnp.concatenate([kp, kc_ref[idx, :]], axis=0).astype(BF16)
                vv = jnp.concatenate([vp, vc_ref[idx, :]], axis=0).astype(BF16)
                do = do_ref[idx, :]
                dsum = jnp.sum(do * o_ref[idx, :], axis=-1, keepdims=True)
                lrow = l_ref[idx, :][:, :1]
                dob = do.astype(BF16)
                sc = lax.dot_general(q, kk, (((1,), (1,)), ((), ())), preferred_element_type=F32) * scale
                p = jnp.where(mask, jnp.exp(sc - lrow), 0.0)
                dp = lax.dot_general(dob, vv, (((1,), (1,)), ((), ())), preferred_element_type=F32)
                ds = (p * (dp - dsum) * scale).astype(BF16)
                pb = p.astype(BF16)
                dq_refs[g][idx, :] = jnp.dot(ds, kk, preferred_element_type=F32)
                dkk = lax.dot_general(ds, q, (((0,), (0,)), ((), ())), preferred_element_type=F32)
                dvv = lax.dot_general(pb, dob, (((0,), (0,)), ((), ())), preferred_element_type=F32)
                tkc[idx, :] += dkk[SPAN:]
                tvc[idx, :] += dvv[SPAN:]
                if prev_in_tile is not None:
                    pidx = pl.ds(prev_in_tile, SPAN, stride=dil) if dil > 1 else pl.ds(prev_in_tile, SPAN)
                    tkc[pidx, :] += dkk[:SPAN]
                    tvc[pidx, :] += dvv[:SPAN]
                else:
                    pidx = pl.ds(prev_start, SPAN, stride=dil) if dil > 1 else pl.ds(prev_start, SPAN)
                    tkp[pidx, :] += dkk[:SPAN]
                    tvp[pidx, :] += dvv[:SPAN]

        @pl.when(n > 0)
        def _():
            dk_ref[...] = ck[...] + tkp[...]
            dv_ref[...] = (cv[...] + tvp[...]).astype(BF16)

        ck[...] = tkc[...]
        cv[...] = tvc[...]

    def col(c, prev=False):
        if prev:
            return pl.BlockSpec((t, HEAD_DIM), lambda h, n: (jnp.maximum(jnp.minimum(n, nt - 1) - 1, 0), c + h))
        return pl.BlockSpec((t, HEAD_DIM), lambda h, n: (jnp.minimum(n, nt - 1), c + h))

    kv_out = pl.BlockSpec((t, HEAD_DIM), lambda h, n: (jnp.maximum(n - 1, 0), h))
    tile = pltpu.VMEM((t, HEAD_DIM), F32)
    per_head = jax.ShapeDtypeStruct((s, N_KV_HEADS * HEAD_DIM), F32)
    return pl.pallas_call(
        body, name=name,
        out_shape=(per_head, per_head, per_head, per_head, jax.ShapeDtypeStruct((s, N_KV_HEADS * HEAD_DIM), BF16)),
        grid=(N_KV_HEADS, nt + 1),
        in_specs=[col(0), col(N_KV_HEADS), col(2 * N_KV_HEADS), col(kcol), col(kcol, True), col(vcol), col(vcol, True),
                  col(0), col(0), col(0)],
        out_specs=(col(0), col(0), col(0), kv_out, kv_out),
        scratch_shapes=[tile] * 6,
        compiler_params=_cp(("parallel", "arbitrary"), 48),
    )(qkr, qkr, qkr, qkr, qkr, proj, proj, attn, lse, dcat)


GELU_C0 = 0.7978845608028654
GELU_C1 = 0.044715


def _softplus_neg(lam):
    y = jnp.exp(-jnp.abs(lam))
    w = 1.0 + y
    log1p = jnp.where(w == 1.0, y, jnp.log(w) * (y / jnp.where(w == 1.0, 1.0, w - 1.0)))
    return jnp.maximum(-lam, 0.0) + log1p


def _down(cur, prev, k, row):
    if k == 0:
        return cur
    return jnp.where(row < k, pltpu.roll(prev, k, 0), pltpu.roll(cur, k, 0))


def _up(cur, nxt, k, row, tt):
    if k == 0:
        return cur
    return jnp.where(row >= tt - k, pltpu.roll(nxt, tt - k, 0), pltpu.roll(cur, tt - k, 0))


def _lru_gates(x, xp, cw, cb, wr, br, wi, bi, lam, row):
    shifts = [_down(x, xp, k, row) for k in range(CONV_WIDTH)]
    xc = cb
    for j in range(CONV_WIDTH):
        xc = xc + cw[j:j + 1, :] * shifts[CONV_WIDTH - 1 - j]
    xcb = xc.astype(BF16)
    r = jax.nn.sigmoid(jnp.dot(xcb, wr, preferred_element_type=F32) + br)
    i = jax.nn.sigmoid(jnp.dot(xcb, wi, preferred_element_type=F32) + bi)
    c = -LRU_C * _softplus_neg(lam)
    la = c * r
    a = jnp.exp(la)
    mult = jnp.sqrt(jnp.tanh(-la) * (a * a + 1.0))
    return shifts, xc, xcb, r, i, c, a, mult


def _lru_fwd(proj, cw, cb, wr, br, wi, bi, lam, *, tt, name):
    s = proj.shape[0]
    nblk = wr.shape[0]
    c = nblk * LANES
    tt = min(tt, s)
    xcol0 = (N_PATTERNS + 2) * N_KV_HEADS
    ycol0 = xcol0 + nblk

    def body(x_ref, y_ref, cw_ref, cb_ref, wr_ref, br_ref, wi_ref, bi_ref, lam_ref, rec_ref, h_ref, xprev, hc):
        n = pl.program_id(1)

        @pl.when(n == 0)
        def _():
            xprev[...] = jnp.zeros_like(xprev)
            hc[...] = jnp.zeros_like(hc)

        row = lax.broadcasted_iota(jnp.int32, (tt, LANES), 0)
        x = x_ref[...]
        _, xc, _, _, i, _, a, mult = _lru_gates(
            x, xprev[...], cw_ref[...], cb_ref[...], wr_ref[0].astype(BF16), br_ref[...],
            wi_ref[0].astype(BF16), bi_ref[...], lam_ref[...], row)
        av, bv = a, mult * (i * xc)
        k = 1
        while k < tt:
            bs = jnp.where(row < k, 0.0, pltpu.roll(bv, k, 0))
            as_ = jnp.where(row < k, 1.0, pltpu.roll(av, k, 0))
            bv = bv + av * bs
            av = av * as_
            k *= 2
        h = bv + av * hc[0:1, :]
        hc[...] = jnp.broadcast_to(h[tt - 1:tt, :], hc.shape)
        h_ref[...] = h
        y = y_ref[...]
        gel = 0.5 * y * (1.0 + jnp.tanh(GELU_C0 * (y + GELU_C1 * y * y * y)))
        rec_ref[...] = (h * gel).astype(BF16)
        xprev[...] = x

    vec = pl.BlockSpec((1, LANES), lambda b, n: (0, b))
    wblk = pl.BlockSpec((1, LANES, LANES), lambda b, n: (b, 0, 0))
    out = pl.BlockSpec((tt, LANES), lambda b, n: (n, b))
    return pl.pallas_call(
        body, name=name,
        out_shape=(jax.ShapeDtypeStruct((s, c), BF16), jax.ShapeDtypeStruct((s, c), F32)),
        grid=(nblk, s // tt),
        in_specs=[pl.BlockSpec((tt, LANES), lambda b, n: (n, xcol0 + b)),
                  pl.BlockSpec((tt, LANES), lambda b, n: (n, ycol0 + b)),
                  pl.BlockSpec((CONV_WIDTH, LANES), lambda b, n: (0, b)), vec, wblk, vec, wblk, vec, vec],
        out_specs=(out, out),
        scratch_shapes=[pltpu.VMEM((tt, LANES), F32), pltpu.VMEM((8, LANES), F32)],
        compiler_params=_cp(("parallel", "arbitrary"), 32),
    )(proj, proj, cw, cb, wr, br, wi, bi, lam)


def _lru_bwd(proj, hseq, dcat, cw, cb, wr, br, wi, bi, lam, *, tt, name):
    s = proj.shape[0]
    nblk = wr.shape[0]
    c = nblk * LANES
    tt = min(tt, s)
    nt = s // tt
    xcol0 = (N_PATTERNS + 2) * N_KV_HEADS
    ycol0 = xcol0 + nblk
    rcol0 = N_KV_HEADS

    def body(x_ref, xp_ref, y_ref, h_ref, hp_ref, dr_ref, cw_ref, cb_ref, wr_ref, br_ref, wi_ref, bi_ref, lam_ref,
             dx_ref, dy_ref, dcw_ref, dcb_ref, dwr_ref, dbr_ref, dwi_ref, dbi_ref, dlam_ref, dxc_next, gcar, acar):
        n = pl.program_id(1)
        rt = nt - 1 - n

        @pl.when(n == 0)
        def _():
            for ref in (dxc_next, gcar, acar, dcw_ref, dcb_ref, dwr_ref, dbr_ref, dwi_ref, dbi_ref, dlam_ref):
                ref[...] = jnp.zeros_like(ref)

        row = lax.broadcasted_iota(jnp.int32, (tt, LANES), 0)
        x = x_ref[...]
        xp = jnp.where(rt > 0, xp_ref[...], 0.0)
        cwv = cw_ref[...]
        wrb, wib = wr_ref[0].astype(BF16), wi_ref[0].astype(BF16)
        lam_v = lam_ref[...]
        shifts, xc, xcb, r, i, cc, a, mult = _lru_gates(x, xp, cwv, cb_ref[...], wrb, br_ref[...], wib, bi_ref[...],
                                                        lam_v, row)
        h = h_ref[...]
        hp_last = jnp.where(rt > 0, hp_ref[7:8, :], 0.0)
        hprev = jnp.where(row < 1, hp_last, pltpu.roll(h, 1, 0))
        y = y_ref[...]
        y2 = y * y
        th = jnp.tanh(GELU_C0 * (y + GELU_C1 * y2 * y))
        gel = 0.5 * y * (1.0 + th)
        dgel = 0.5 * (1.0 + th) + 0.5 * y * (1.0 - th * th) * GELU_C0 * (1.0 + 3.0 * GELU_C1 * y2)
        drec = dr_ref[...]
        dy_ref[...] = (drec * h * dgel).astype(BF16)
        av = jnp.where(row >= tt - 1, acar[0:1, :], pltpu.roll(a, tt - 1, 0))
        bv = drec * gel
        k = 1
        while k < tt:
            bs = jnp.where(row >= tt - k, 0.0, pltpu.roll(bv, tt - k, 0))
            as_ = jnp.where(row >= tt - k, 1.0, pltpu.roll(av, tt - k, 0))
            bv = bv + av * bs
            av = av * as_
            k *= 2
        g = bv + av * gcar[0:1, :]
        gcar[...] = jnp.broadcast_to(g[0:1, :], gcar.shape)
        acar[...] = jnp.broadcast_to(a[0:1, :], acar.shape)
        da = g * hprev
        d_ixc = g * mult
        dmult = g * (i * xc)
        di = d_ixc * xc
        dxc = d_ixc * i
        a2 = a * a
        dla = da * a - dmult * (a2 / mult)
        dr = dla * cc
        dsp = jnp.sum(dla * r, axis=0, keepdims=True) * (-LRU_C)
        dlam_ref[...] += dsp * (-jax.nn.sigmoid(-lam_v))
        dzr = dr * r * (1.0 - r)
        dzi = di * i * (1.0 - i)
        dbr_ref[...] += jnp.sum(dzr, axis=0, keepdims=True)
        dbi_ref[...] += jnp.sum(dzi, axis=0, keepdims=True)
        dzrb, dzib = dzr.astype(BF16), dzi.astype(BF16)
        tn = (((0,), (0,)), ((), ()))
        ntd = (((1,), (1,)), ((), ()))
        dwr_ref[0] += lax.dot_general(xcb, dzrb, tn, preferred_element_type=F32)
        dwi_ref[0] += lax.dot_general(xcb, dzib, tn, preferred_element_type=F32)
        dxc = (dxc + lax.dot_general(dzrb, wrb, ntd, preferred_element_type=F32)
               + lax.dot_general(dzib, wib, ntd, preferred_element_type=F32))
        dcb_ref[...] += jnp.sum(dxc, axis=0, keepdims=True)
        dcw_ref[...] += jnp.concatenate(
            [jnp.sum(dxc * shifts[CONV_WIDTH - 1 - j], axis=0, keepdims=True) for j in range(CONV_WIDTH)], axis=0)
        nxt = dxc_next[...]
        dx = cwv[0:1, :] * _up(dxc, nxt, CONV_WIDTH - 1, row, tt)
        for j in range(1, CONV_WIDTH):
            dx = dx + cwv[j:j + 1, :] * _up(dxc, nxt, CONV_WIDTH - 1 - j, row, tt)
        dx_ref[...] = dx.astype(BF16)
        dxc_next[...] = dxc

    def tile(col0, prev=False):
        if prev:
            return pl.BlockSpec((tt, LANES), lambda b, n: (jnp.maximum(nt - 2 - n, 0), col0 + b))
        return pl.BlockSpec((tt, LANES), lambda b, n: (nt - 1 - n, col0 + b))

    vec = pl.BlockSpec((1, LANES), lambda b, n: (0, b))
    wblk = pl.BlockSpec((1, LANES, LANES), lambda b, n: (b, 0, 0))
    cwblk = pl.BlockSpec((CONV_WIDTH, LANES), lambda b, n: (0, b))
    hp8 = pl.BlockSpec((8, LANES), lambda b, n: (jnp.maximum((nt - 1 - n) * (tt // 8) - 1, 0), b))
    vshape = jax.ShapeDtypeStruct((1, c), F32)
    wshape = jax.ShapeDtypeStruct((nblk, LANES, LANES), F32)
    return pl.pallas_call(
        body, name=name,
        out_shape=(jax.ShapeDtypeStruct((s, c), BF16), jax.ShapeDtypeStruct((s, c), BF16),
                   jax.ShapeDtypeStruct((CONV_WIDTH, c), F32), vshape, wshape, vshape, wshape, vshape, vshape),
        grid=(nblk, nt),
        in_specs=[tile(xcol0), tile(xcol0, True), tile(ycol0), tile(0), hp8, tile(rcol0),
                  cwblk, vec, wblk, vec, wblk, vec, vec],
        out_specs=(tile(0), tile(0), cwblk, vec, wblk, vec, wblk, vec, vec),
        scratch_shapes=[pltpu.VMEM((tt, LANES), F32), pltpu.VMEM((8, LANES), F32), pltpu.VMEM((8, LANES), F32)],
        compiler_params=_cp(("parallel", "arbitrary"), 32),
    )(proj, proj, proj, hseq, hseq, dcat, cw, cb, wr, br, wi, bi, lam)


ROW_BLOCKS = (256, 128, 64, 32, 16, 8)


def _adamw(w, m, v, gparts, *, name):
    r, c = w.shape
    npart = gparts.shape[0]
    br = _pick(r, ROW_BLOCKS)
    c1 = 1.0 - ADAM_B1 ** ADAM_STEP
    c2 = 1.0 - ADAM_B2 ** ADAM_STEP

    def body(w_ref, m_ref, v_ref, g_ref, go_ref, d_ref, mo_ref, vo_ref):
        g = g_ref[0].astype(F32)
        for q in range(1, npart):
            g = g + g_ref[q].astype(F32)
        mn = ADAM_B1 * m_ref[...] + (1.0 - ADAM_B1) * g
        vn = ADAM_B2 * v_ref[...] + (1.0 - ADAM_B2) * (g * g)
        go_ref[...] = g
        mo_ref[...] = mn
        vo_ref[...] = vn
        d_ref[...] = -ADAM_LR * ((mn / c1) / (jnp.sqrt(vn / c2) + ADAM_EPS) + ADAM_WD * w_ref[...])

    blk = pl.BlockSpec((br, c), lambda i: (i, 0))
    out = jax.ShapeDtypeStruct((r, c), F32)
    return pl.pallas_call(
        body, name=name, out_shape=(out, out, out, out), grid=(r // br,),
        in_specs=[blk, blk, blk, pl.BlockSpec((npart, br, c), lambda i: (0, i, 0))],
        out_specs=(blk, blk, blk, blk),
        compiler_params=_cp(("parallel",), 48),
    )(w, m, v, gparts)


def _sum_parts(parts, *, name):
    npart, r, c = parts.shape
    br = _pick(r, ROW_BLOCKS)

    def body(p_ref, o_ref):
        acc = p_ref[0]
        for q in range(1, npart):
            acc = acc + p_ref[q]
        o_ref[...] = acc

    return pl.pallas_call(
        body, name=name, out_shape=jax.ShapeDtypeStruct((r, c), F32), grid=(r // br,),
        in_specs=[pl.BlockSpec((npart, br, c), lambda i: (0, i, 0))],
        out_specs=pl.BlockSpec((br, c), lambda i: (i, 0)),
        compiler_params=_cp(("parallel",), 48),
    )(parts)


def _pair_add(g8, buf, core, *, name):
    _, r, c = g8.shape
    br = _pick(r, ROW_BLOCKS[:-1])

    def body(core_ref, g_ref, b_ref, o_ref):
        o_ref[...] = (g_ref[...].astype(F32) + b_ref[...].astype(F32)).astype(g8.dtype)

    return pl.pallas_call(
        body, name=name, out_shape=jax.ShapeDtypeStruct((4, r, c), g8.dtype),
        grid_spec=pltpu.PrefetchScalarGridSpec(
            num_scalar_prefetch=1, grid=(4, r // br),
            in_specs=[pl.BlockSpec((1, br, c), lambda q, i, core_ref: (2 * q + core_ref[0], i, 0)),
                      pl.BlockSpec((1, br, c), lambda q, i, core_ref: (q, i, 0))],
            out_specs=pl.BlockSpec((1, br, c), lambda q, i, core_ref: (q, i, 0))),
        compiler_params=_cp(("parallel", "parallel"), 48),
    )(core, g8, buf)


HBM = pl.BlockSpec(memory_space=pltpu.HBM)


def _mesh_pos():
    return lax.axis_index("x"), lax.axis_index("y"), lax.axis_index("c")


def _all_gather(shards, *, name):
    na = len(shards)

    def body(*refs):
        x_refs, out_refs = refs[:na], refs[na:2 * na]
        send_sems, recv_sems, local_sems = refs[2 * na:]
        x, y, c = _mesh_pos()
        me, sibling = (x, y, c), (x, y, 1 - c)
        chips = [(1 - x, y), (x, 1 - y), (1 - x, 1 - y)]

        def copy(a, k, block, to, src=None):
            px, py, pc = block
            dst = out_refs[a].at[4 * px + 2 * py + pc]
            return pltpu.make_async_remote_copy(
                src_ref=dst if src is None else src, dst_ref=dst,
                send_sem=send_sems.at[a, k], recv_sem=recv_sems.at[a, k],
                device_id=to, device_id_type=MESH)

        mine, first, passed = [], [], []
        for a in range(na):
            cp = pltpu.make_async_copy(x_refs[a], out_refs[a].at[4 * x + 2 * y + c], local_sems.at[a])
            cp.start()
            mine.append(cp)
            sends = [copy(a, 0, me, sibling, src=x_refs[a])]
            sends += [copy(a, 1 + j, me, (*chip, c), src=x_refs[a]) for j, chip in enumerate(chips)]
            for cp in sends:
                cp.start()
            first += sends
        for j, chip in enumerate(chips):
            for a in range(na):
                copy(a, 1 + j, (*chip, c), me).wait_recv()
                cp = copy(a, 4 + j, (*chip, c), sibling)
                cp.start()
                passed.append(cp)
        for a in range(na):
            copy(a, 0, sibling, me).wait_recv()
            for j, chip in enumerate(chips):
                copy(a, 4 + j, (*chip, 1 - c), me).wait_recv()
        for cp in first + passed:
            cp.wait_send()
        for cp in mine:
            cp.wait()

    return pl.pallas_call(
        body, name=name,
        out_shape=tuple(jax.ShapeDtypeStruct((N_DEV,) + a.shape, a.dtype) for a in shards),
        in_specs=[HBM] * na, out_specs=tuple([HBM] * na),
        scratch_shapes=[pltpu.SemaphoreType.DMA((na, 7)), pltpu.SemaphoreType.DMA((na, 7)),
                        pltpu.SemaphoreType.DMA((na,))],
    )(*shards)


def _sibling_exchange(g8s, *, name):
    na = len(g8s)

    def body(*refs):
        g_refs, buf_refs = refs[:na], refs[na:2 * na]
        send_sems, recv_sems = refs[2 * na:]
        x, y, c = _mesh_pos()
        copies = []
        for a in range(na):
            for q in range(4):
                cp = pltpu.make_async_remote_copy(
                    src_ref=g_refs[a].at[2 * q + (1 - c)], dst_ref=buf_refs[a].at[q],
                    send_sem=send_sems.at[a, q], recv_sem=recv_sems.at[a, q],
                    device_id=(x, y, 1 - c), device_id_type=MESH)
                cp.start()
                copies.append(cp)
        for cp in copies:
            cp.wait()

    return pl.pallas_call(
        body, name=name,
        out_shape=tuple(jax.ShapeDtypeStruct((4,) + g.shape[1:], g.dtype) for g in g8s),
        in_specs=[HBM] * na, out_specs=tuple([HBM] * na),
        scratch_shapes=[pltpu.SemaphoreType.DMA((na, 4)), pltpu.SemaphoreType.DMA((na, 4))],
    )(*g8s)


def _chip_exchange(parts, *, name):
    na = len(parts)

    def body(*refs):
        p_refs, buf_refs = refs[:na], refs[na:2 * na]
        send_sems, recv_sems, local_sems = refs[2 * na:]
        x, y, c = _mesh_pos()
        chips = [(1 - x, y), (x, 1 - y), (1 - x, 1 - y)]
        my_chip = 2 * x + y

        def copy(a, j, slot):
            px, py = chips[j]
            return pltpu.make_async_remote_copy(
                src_ref=p_refs[a].at[2 * px + py], dst_ref=buf_refs[a].at[slot],
                send_sem=send_sems.at[a, j], recv_sem=recv_sems.at[a, j],
                device_id=(px, py, c), device_id_type=MESH)

        sends, local = [], []
        for a in range(na):
            cp = pltpu.make_async_copy(p_refs[a].at[my_chip], buf_refs[a].at[my_chip], local_sems.at[a])
            cp.start()
            local.append(cp)
            for j in range(3):
                cp = copy(a, j, my_chip)
                cp.start()
                sends.append(cp)
        for a in range(na):
            for j, (px, py) in enumerate(chips):
                copy(a, j, 2 * px + py).wait_recv()
        for cp in sends:
            cp.wait_send()
        for cp in local:
            cp.wait()

    return pl.pallas_call(
        body, name=name,
        out_shape=tuple(jax.ShapeDtypeStruct(p.shape, p.dtype) for p in parts),
        in_specs=[HBM] * na, out_specs=tuple([HBM] * na),
        scratch_shapes=[pltpu.SemaphoreType.DMA((na, 3)), pltpu.SemaphoreType.DMA((na, 3)),
                        pltpu.SemaphoreType.DMA((na,))],
    )(*parts)


BIG_WEIGHTS = ("ffn1_w_gate", "ffn1_w_up", "ffn1_w_down", "w_in", "w_out",
               "ffn2_w_gate", "ffn2_w_up", "ffn2_w_down", "w_ple_proj", "w_ple_gate")
COLUMN_SHARDED = ("ffn1_w_gate", "ffn1_w_up", "w_in", "ffn2_w_gate", "ffn2_w_up", "w_ple_proj", "conv_w")
SMALL_WEIGHTS = ("ln1_g", "ln1_b", "conv_b", "w_rgate", "b_rgate", "w_igate", "b_igate", "lru_lambda",
                 "ln2_g", "ln2_b", "ln3_g", "ln3_b")

BM_CANDIDATES = (1024, 512, 256, 128)
BN_CANDIDATES = (1408, 1024, 896, 512, 256, 128)


def _dw(a, b, *, scale=1.0, name):
    m, n = a.shape[1], b.shape[1]
    return _mm(a, b, ta=True, scale=scale, out_dtype=BF16, bm=_pick(m, BN_CANDIDATES), bn=_pick(n, BN_CANDIDATES),
               bk=1024, name=name)


def _ffn_fwd(xb, x, wg, wu, wd, gamma, beta, tag):
    f = wg.shape[1]
    g, u, h = _ffn_up(xb, wg, wu, bm=1024, bn=_pick(f, (512, 256, 128)), name=f"{tag}_up")
    y, yb, xh, rs = _mm_ln(h, wd, x, gamma, beta, res_scale=DEEPNORM_ALPHA, mm_scale=0.5,
                           bm=512, bk=_pick(f, (1408, 512, 256, 128)), name=f"{tag}_down_ln")
    return (g, u, h, xh, rs), y, yb


def _ffn_bwd(saved, xb_in, dz, dzb, wg, wu, wd, ln_in, tag):
    g, u, h, _, _ = saved
    f = wg.shape[1]
    dg, du = _ffn_bwd_dh(dzb, wd, g, u, scale=0.5, bm=1024, bn=_pick(f, (512, 256, 128)), name=f"{tag}_dh")
    dwd = _dw(h, dzb, scale=0.5, name=f"{tag}_dwd")
    dwg = _dw(xb_in, dg, name=f"{tag}_dwg")
    dwu = _dw(xb_in, du, name=f"{tag}_dwu")
    out = _mm_dx([(dg, wg), (du, wu)], dz, extra_scale=DEEPNORM_ALPHA, ln=ln_in,
                 bm=512, bk=_pick(f, (512, 256, 128)), name=f"{tag}_dx")
    return (dwg, dwu, dwd), out


def _local_step(x, p, target, positions, w):
    s, d = x.shape
    tabs = _rope_tables(positions)
    xb = x.astype(BF16)
    sv1, x1, x1b = _ffn_fwd(xb, x, w["ffn1_w_gate"], w["ffn1_w_up"], w["ffn1_w_down"], w["ln1_g"], w["ln1_b"], "ffn1")
    pw = w["w_in"].shape[1]
    proj = _mm(x1b, w["w_in"], bm=1024, bn=_pick(pw, BN_CANDIDATES), bk=d, name="in_proj")
    nqk = (N_PATTERNS + 1) * N_KV_HEADS
    qkr = _rotary(proj, tabs, n_cols=nqk, inverse=False, out_dtype=F32, bs=1024, name="rotary")
    attn, lse = _attn_fwd(qkr, proj, name="attn_fwd")
    lru_w = (w["conv_w"], w["conv_b"], w["w_rgate"], w["b_rgate"], w["w_igate"], w["b_igate"], w["lru_lambda"])
    rec, hseq = _lru_fwd(proj, *lru_w, tt=512, name="lru_fwd")
    cat = jnp.concatenate([attn.astype(BF16), rec], axis=1)
    x2, x2b, xh2, rs2 = _mm_ln(cat, w["w_out"], x1, w["ln2_g"], w["ln2_b"], res_scale=DEEPNORM_ALPHA, mm_scale=1.0,
                               bm=512, bk=1024, name="out_proj_ln")
    sv3, x3, x3b = _ffn_fwd(x2b, x2, w["ffn2_w_gate"], w["ffn2_w_up"], w["ffn2_w_down"], w["ln3_g"], w["ln3_b"], "ffn2")
    lsum, dy, dgate, dple = _ple_loss(x3, x3b, p, w["w_ple_gate"], w["w_ple_proj"], target,
                                      bm=1024, bn=_pick(d, (512, 256, 128)), name="ple_loss")
    grads = {}
    grads["w_ple_gate"] = _dw(x3b, dgate, name="dw_ple_gate")
    grads["w_ple_proj"] = _dw(p, dple, name="dw_ple_proj")
    dz3, dz3b, grads["ln3_g"], grads["ln3_b"] = _mm_dx(
        [(dgate, w["w_ple_gate"])], dy, extra_scale=1.0, ln=(sv3[3], sv3[4], w["ln3_g"]), bm=512, bk=1024, name="ple_dx")
    (grads["ffn2_w_gate"], grads["ffn2_w_up"], grads["ffn2_w_down"]), (dz2, dz2b, grads["ln2_g"], grads["ln2_b"]) = _ffn_bwd(
        sv3, x2b, dz3, dz3b, w["ffn2_w_gate"], w["ffn2_w_up"], w["ffn2_w_down"], (xh2, rs2, w["ln2_g"]), "ffn2")
    grads["w_out"] = _dw(cat, dz2b, name="dw_out")
    dcat = _mm(dz2b, w["w_out"], tb=True, bm=1024, bn=1024, bk=d, name="out_proj_dx")
    dq0, dq1, dq2, dk, dvb = _attn_bwd(qkr, proj, attn, lse, dcat, name="attn_bwd")
    nh = N_KV_HEADS
    dqkv = [_rotary(t, tabs, n_cols=nh, inverse=True, out_dtype=BF16, bs=1024, name=f"rotary_bwd{i}")
            for i, t in enumerate((dq0, dq1, dq2, dk))]
    (dxb, dyb, grads["conv_w"], grads["conv_b"], grads["w_rgate"], grads["b_rgate"], grads["w_igate"],
     grads["b_igate"], grads["lru_lambda"]) = _lru_bwd(proj, hseq, dcat, *lru_w, tt=512, name="lru_bwd")
    dproj = jnp.concatenate(dqkv + [dvb, dxb, dyb], axis=1)
    grads["w_in"] = _dw(x1b, dproj, name="dw_in")
    dz1, dz1b, grads["ln1_g"], grads["ln1_b"] = _mm_dx(
        [(dproj, w["w_in"])], dz2, extra_scale=DEEPNORM_ALPHA, ln=(sv1[3], sv1[4], w["ln1_g"]),
        bm=512, bk=_pick(pw, (1408, 1024, 896, 512, 256, 128)), name="in_proj_dx")
    (grads["ffn1_w_gate"], grads["ffn1_w_up"], grads["ffn1_w_down"]), grad_x = _ffn_bwd(
        sv1, xb, dz1, dz1b, w["ffn1_w_gate"], w["ffn1_w_up"], w["ffn1_w_down"], None, "ffn1")
    return lsum, grad_x, grads


def _to_full(name, gathered):
    if name in COLUMN_SHARDED:
        _, r, c = gathered.shape
        return jnp.transpose(gathered, (1, 0, 2)).reshape(r, N_DEV * c)
    return gathered.reshape((N_DEV * gathered.shape[1],) + gathered.shape[2:])


def _to_owner_blocks(name, full):
    if name in COLUMN_SHARDED:
        r, c = full.shape
        return jnp.transpose(full.reshape(r, N_DEV, c // N_DEV), (1, 0, 2))
    return full.reshape((N_DEV, full.shape[0] // N_DEV) + full.shape[1:])


def _rows128(a):
    flat = a.reshape(-1, LANES)
    pad = (-flat.shape[0]) % 8
    return jnp.pad(flat, ((0, pad), (0, 0))) if pad else flat


def kernel(x, p, positions, ffn1_w_gate, ffn1_w_up, ffn1_w_down, ln1_g, ln1_b, w_in, conv_w, conv_b, w_rgate, b_rgate, w_igate, b_igate, lru_lambda, w_out, ln2_g, ln2_b, ffn2_w_gate, ffn2_w_up, ffn2_w_down, ln3_g, ln3_b, w_ple_proj, w_ple_gate, loss_target, m_ffn1_w_gate, m_ffn1_w_up, m_ffn1_w_down, m_ln1_g, m_ln1_b, m_w_in, m_conv_w, m_conv_b, m_w_rgate, m_b_rgate, m_w_igate, m_b_igate, m_lru_lambda, m_w_out, m_ln2_g, m_ln2_b, m_ffn2_w_gate, m_ffn2_w_up, m_ffn2_w_down, m_ln3_g, m_ln3_b, m_w_ple_proj, m_w_ple_gate, v_ffn1_w_gate, v_ffn1_w_up, v_ffn1_w_down, v_ln1_g, v_ln1_b, v_w_in, v_conv_w, v_conv_b, v_w_rgate, v_b_rgate, v_w_igate, v_b_igate, v_lru_lambda, v_w_out, v_ln2_g, v_ln2_b, v_ffn2_w_gate, v_ffn2_w_up, v_ffn2_w_down, v_ln3_g, v_ln3_b, v_w_ple_proj, v_w_ple_gate):
    names = ("ffn1_w_gate", "ffn1_w_up", "ffn1_w_down", "ln1_g", "ln1_b", "w_in", "conv_w", "conv_b", "w_rgate",
             "b_rgate", "w_igate", "b_igate", "lru_lambda", "w_out", "ln2_g", "ln2_b", "ffn2_w_gate", "ffn2_w_up",
             "ffn2_w_down", "ln3_g", "ln3_b", "w_ple_proj", "w_ple_gate")
    ws = (ffn1_w_gate, ffn1_w_up, ffn1_w_down, ln1_g, ln1_b, w_in, conv_w, conv_b, w_rgate, b_rgate, w_igate, b_igate,
          lru_lambda, w_out, ln2_g, ln2_b, ffn2_w_gate, ffn2_w_up, ffn2_w_down, ln3_g, ln3_b, w_ple_proj, w_ple_gate)
    ms = (m_ffn1_w_gate, m_ffn1_w_up, m_ffn1_w_down, m_ln1_g, m_ln1_b, m_w_in, m_conv_w, m_conv_b, m_w_rgate, m_b_rgate,
          m_w_igate, m_b_igate, m_lru_lambda, m_w_out, m_ln2_g, m_ln2_b, m_ffn2_w_gate, m_ffn2_w_up, m_ffn2_w_down,
          m_ln3_g, m_ln3_b, m_w_ple_proj, m_w_ple_gate)
    vs = (v_ffn1_w_gate, v_ffn1_w_up, v_ffn1_w_down, v_ln1_g, v_ln1_b, v_w_in, v_conv_w, v_conv_b, v_w_rgate, v_b_rgate,
          v_w_igate, v_b_igate, v_lru_lambda, v_w_out, v_ln2_g, v_ln2_b, v_ffn2_w_gate, v_ffn2_w_up, v_ffn2_w_down,
          v_ln3_g, v_ln3_b, v_w_ple_proj, v_w_ple_gate)
    def local(a):
        return a[0] if a.ndim >= 3 else a

    w_loc = {n: local(a) for n, a in zip(names, ws)}
    m_loc = {n: local(a) for n, a in zip(names, ms)}
    v_loc = {n: local(a) for n, a in zip(names, vs)}
    out_shapes = {n: a.shape for n, a in zip(names, ws)}

    shards = [w_loc[n].astype(BF16) for n in BIG_WEIGHTS] + [w_loc["conv_w"]]
    gathered = _all_gather(shards, name="gather_weights")
    full = {n: _to_full(n, g) for n, g in zip(BIG_WEIGHTS + ("conv_w",), gathered)}
    for n in SMALL_WEIGHTS:
        full[n] = w_loc[n]

    lsum, grad_x, grads = _local_step(x[0], p[0, 0], loss_target[0], positions[0], full)
    d_model = x.shape[-1]
    loss = lax.psum(lsum[0, 0] * (0.5 / d_model), ("x", "y", "c"))

    core = lax.axis_index("c").astype(jnp.int32).reshape(1)
    g8 = [_to_owner_blocks(n, grads[n]) for n in BIG_WEIGHTS]
    sib = _sibling_exchange(g8, name="rs_sibling")
    chip_parts = [_pair_add(a, b, core, name=f"rs_add_{n}") for n, a, b in zip(BIG_WEIGHTS, g8, sib)]
    reduced = dict(zip(BIG_WEIGHTS, _chip_exchange(chip_parts, name="rs_chips")))

    small = SMALL_WEIGHTS + ("conv_w",)
    packed = jnp.concatenate([_rows128(grads[n]) for n in small], axis=0)
    all_packed, = _all_gather([packed], name="gather_small_grads")
    summed = _sum_parts(all_packed, name="sum_small_grads")
    small_grads, row = {}, 0
    for n in small:
        rows = grads[n].size // LANES
        small_grads[n] = summed[row:row + rows].reshape(grads[n].shape)
        row += rows + (-rows) % 8
    me = 4 * lax.axis_index("x") + 2 * lax.axis_index("y") + lax.axis_index("c")
    cw_cols = w_loc["conv_w"].shape[1]
    small_grads["conv_w"] = lax.dynamic_slice_in_dim(small_grads["conv_w"], me * cw_cols, cw_cols, axis=1)

    out_g, out_d, out_m, out_v = {}, {}, {}, {}
    for n in names:
        wl, ml, vl = w_loc[n], m_loc[n], v_loc[n]
        shape = wl.shape
        if n in BIG_WEIGHTS:
            gparts = reduced[n]
        else:
            gparts = small_grads[n].reshape((1,) + shape)
        if wl.ndim == 3:
            wl, ml, vl = (t.reshape(-1, shape[-1]) for t in (wl, ml, vl))
            gparts = gparts.reshape(gparts.shape[0], -1, shape[-1])
        res = _adamw(wl, ml, vl, gparts, name=f"adamw_{n}")
        out_g[n], out_d[n], out_m[n], out_v[n] = (t.reshape(out_shapes[n]) for t in res)

    return (loss, grad_x[None], *[out_g[n] for n in names], *[out_d[n] for n in names],
            *[out_m[n] for n in names], *[out_v[n] for n in names])
```

```python
import jax
import jax.numpy as jnp
from jax import lax
from jax.experimental import pallas as pl
from jax.experimental.pallas import tpu as pltpu

F32 = jnp.float32
BF16 = jnp.bfloat16

N_DEV = 8
LANES = 128
MIB = 1 << 20

HEAD_DIM = 128
N_KV_HEADS = 4
DILATIONS = (1, 4, 16)
N_PATTERNS = 3
SPAN = 128
ROT_DIMS = 32
ROPE_THETA = 500000.0
LRU_C = 8.0
CONV_WIDTH = 4
LN_EPS = 1e-5
DEEPNORM_ALPHA = 2.0 ** 0.25
ATTN_TILE = SPAN * DILATIONS[-1]

ADAM_LR = 0.001
ADAM_B1 = 0.9
ADAM_B2 = 0.999
ADAM_EPS = 1e-08
ADAM_WD = 0.01
ADAM_STEP = 10

MESH = pl.DeviceIdType.MESH
EPILOGUE_ROWS = 64


def _cp(semantics, vmem_mib):
    return pltpu.CompilerParams(dimension_semantics=semantics, vmem_limit_bytes=vmem_mib * MIB)


def _pick(n, candidates):
    for c in candidates:
        if n % c == 0:
            return c
    return n


class _Comm:
    def __init__(self, arrays, out_shapes, scratch, start, end, mid=None):
        self.arrays, self.out_shapes, self.scratch = list(arrays), list(out_shapes), list(scratch)
        self.start, self.mid, self.end = start, mid, end


def _call(body, *, name, grid, in_specs, out_specs, out_shape, args, scratch_shapes=(), vmem_mib, comm=None):
    single = not isinstance(out_shape, (tuple, list))
    out_shape_t = (out_shape,) if single else tuple(out_shape)
    out_specs_t = (out_specs,) if single else tuple(out_specs)
    params = _cp(("arbitrary",) * len(grid), vmem_mib)
    if comm is None:
        res = pl.pallas_call(body, name=name, grid=grid, in_specs=list(in_specs), out_specs=out_specs_t,
                             out_shape=out_shape_t, scratch_shapes=list(scratch_shapes), compiler_params=params)(*args)
        return res[0] if single else res
    n_in, n_out, n_scr = len(args), len(out_shape_t), len(scratch_shapes)
    nci, nco = len(comm.arrays), len(comm.out_shapes)
    total = 1
    for g in grid:
        total *= g

    def wrapped(*refs):
        ins, refs = refs[:n_in], refs[n_in:]
        cin, refs = refs[:nci], refs[nci:]
        outs, refs = refs[:n_out], refs[n_out:]
        cout, refs = refs[:nco], refs[nco:]
        scr, csem = refs[:n_scr], refs[n_scr:]
        step = pl.program_id(0)
        for ax in range(1, len(grid)):
            step = step * grid[ax] + pl.program_id(ax)

        @pl.when(step == 0)
        def _():
            comm.start(cin, cout, csem)

        body(*ins, *outs, *scr)
        if comm.mid is not None:
            @pl.when(step == total // 2)
            def _():
                comm.mid(cin, cout, csem)

        @pl.when(step == total - 1)
        def _():
            comm.end(cin, cout, csem)

    hbm = pl.BlockSpec(memory_space=pltpu.HBM)
    res = pl.pallas_call(
        wrapped, name=name, grid=grid,
        in_specs=list(in_specs) + [hbm] * nci,
        out_specs=out_specs_t + (hbm,) * nco,
        out_shape=out_shape_t + tuple(comm.out_shapes),
        scratch_shapes=list(scratch_shapes) + comm.scratch,
        compiler_params=params)(*args, *comm.arrays)
    own, extra = res[:n_out], res[n_out:]
    return (own[0] if single else own), extra


def _mm(a, b, *, ta=False, tb=False, out_dtype=F32, scale=1.0, bm, bn, bk, name, comm=None):
    m, k = (a.shape[1], a.shape[0]) if ta else a.shape
    n = b.shape[0] if tb else b.shape[1]
    bm, bn, bk = min(bm, m), min(bn, n), min(bk, k)
    assert m % bm == 0 and n % bn == 0 and k % bk == 0, (name, m, n, k, bm, bn, bk)
    nk = k // bk
    a_spec = pl.BlockSpec((bk, bm), lambda i, j, kk: (kk, i)) if ta else pl.BlockSpec((bm, bk), lambda i, j, kk: (i, kk))
    b_spec = pl.BlockSpec((bn, bk), lambda i, j, kk: (j, kk)) if tb else pl.BlockSpec((bk, bn), lambda i, j, kk: (kk, j))
    dn = (((0 if ta else 1,), (1 if tb else 0,)), ((), ()))

    def body(a_ref, b_ref, o_ref, *acc):
        part = lax.dot_general(a_ref[...].astype(BF16), b_ref[...].astype(BF16), dn, preferred_element_type=F32)
        if nk == 1:
            o_ref[...] = (part * scale).astype(out_dtype)
            return
        acc_ref, = acc
        kk = pl.program_id(2)

        @pl.when(kk == 0)
        def _():
            acc_ref[...] = part

        @pl.when(kk > 0)
        def _():
            acc_ref[...] += part

        @pl.when(kk == nk - 1)
        def _():
            o_ref[...] = (acc_ref[...] * scale).astype(out_dtype)

    return _call(
        body, name=name,
        out_shape=jax.ShapeDtypeStruct((m, n), out_dtype),
        grid=(m // bm, n // bn, nk),
        in_specs=[a_spec, b_spec],
        out_specs=pl.BlockSpec((bm, bn), lambda i, j, kk: (i, j)),
        scratch_shapes=[pltpu.VMEM((bm, bn), F32)] if nk > 1 else [],
        args=(a, b), vmem_mib=56, comm=comm)


def _ffn_up(xb, wg, wu, *, bm, bn, name, comm=None):
    s, d = xb.shape
    f = wg.shape[1]
    bm, bn = min(bm, s), min(bn, f)
    assert s % bm == 0 and f % bn == 0

    def body(x_ref, wg_ref, wu_ref, g_ref, u_ref, h_ref):
        x = x_ref[...]
        g = jnp.dot(x, wg_ref[...], preferred_element_type=F32)
        u = jnp.dot(x, wu_ref[...], preferred_element_type=F32)
        g_ref[...] = g.astype(BF16)
        u_ref[...] = u.astype(BF16)
        h_ref[...] = (g * jax.nn.sigmoid(g) * u).astype(BF16)

    out = jax.ShapeDtypeStruct((s, f), BF16)
    blk = pl.BlockSpec((bm, bn), lambda i, j: (i, j))
    return _call(
        body, name=name, out_shape=(out, out, out),
        grid=(s // bm, f // bn),
        in_specs=[pl.BlockSpec((bm, d), lambda i, j: (i, 0)),
                  pl.BlockSpec((d, bn), lambda i, j: (0, j)),
                  pl.BlockSpec((d, bn), lambda i, j: (0, j))],
        out_specs=(blk, blk, blk),
        args=(xb, wg, wu), vmem_mib=56, comm=comm)


def _ffn_bwd_dh(dzb, wd, g, u, *, scale, bm, bn, name, comm=None):
    s, d = dzb.shape
    f = wd.shape[0]
    bm, bn = min(bm, s), min(bn, f)
    assert s % bm == 0 and f % bn == 0

    def body(dz_ref, wd_ref, g_ref, u_ref, dg_ref, du_ref):
        dh = lax.dot_general(dz_ref[...], wd_ref[...], (((1,), (1,)), ((), ())), preferred_element_type=F32) * scale
        gg = g_ref[...].astype(F32)
        sig = jax.nn.sigmoid(gg)
        silu = gg * sig
        dsilu = sig * (1.0 + gg * (1.0 - sig))
        dg_ref[...] = (dh * u_ref[...].astype(F32) * dsilu).astype(BF16)
        du_ref[...] = (dh * silu).astype(BF16)

    out = jax.ShapeDtypeStruct((s, f), BF16)
    blk = pl.BlockSpec((bm, bn), lambda i, j: (i, j))
    return _call(
        body, name=name, out_shape=(out, out),
        grid=(s // bm, f // bn),
        in_specs=[pl.BlockSpec((bm, d), lambda i, j: (i, 0)),
                  pl.BlockSpec((bn, d), lambda i, j: (j, 0)), blk, blk],
        out_specs=(blk, blk),
        args=(dzb, wd, g, u), vmem_mib=56, comm=comm)


def _mm_ln(a, b, res, gamma, beta, *, res_scale, mm_scale, bm, bk, name, comm=None):
    s, k = a.shape
    d = b.shape[1]
    bm, bk = min(bm, s), min(bk, k)
    assert s % bm == 0 and k % bk == 0
    nk = k // bk
    ch = min(EPILOGUE_ROWS, bm)

    def body(a_ref, b_ref, r_ref, g_ref, be_ref, y_ref, yb_ref, xh_ref, rs_ref, acc_ref):
        kk = pl.program_id(1)
        part = jnp.dot(a_ref[...], b_ref[...], preferred_element_type=F32)

        @pl.when(kk == 0)
        def _():
            acc_ref[...] = part

        @pl.when(kk > 0)
        def _():
            acc_ref[...] += part

        @pl.when(kk == nk - 1)
        def _():
            def chunk(ci, carry):
                rows = pl.ds(pl.multiple_of(ci * ch, ch), ch)
                z = res_scale * r_ref[rows, :] + mm_scale * acc_ref[rows, :]
                mu = jnp.mean(z, axis=-1, keepdims=True)
                zc = z - mu
                var = jnp.mean(zc * zc, axis=-1, keepdims=True)
                rstd = lax.rsqrt(var + LN_EPS)
                xh = zc * rstd
                y = xh * g_ref[...] + be_ref[...]
                y_ref[rows, :] = y
                yb_ref[rows, :] = y.astype(BF16)
                xh_ref[rows, :] = xh
                rs_ref[rows, :] = rstd
                return carry

            lax.fori_loop(0, bm // ch, chunk, 0)

    row = pl.BlockSpec((bm, d), lambda i, kk: (i, 0))
    vec = pl.BlockSpec((1, d), lambda i, kk: (0, 0))
    return _call(
        body, name=name,
        out_shape=(jax.ShapeDtypeStruct((s, d), F32), jax.ShapeDtypeStruct((s, d), BF16),
                   jax.ShapeDtypeStruct((s, d), F32), jax.ShapeDtypeStruct((s, 1), F32)),
        grid=(s // bm, nk),
        in_specs=[pl.BlockSpec((bm, bk), lambda i, kk: (i, kk)),
                  pl.BlockSpec((bk, d), lambda i, kk: (kk, 0)), row, vec, vec],
        out_specs=(row, row, row, pl.BlockSpec((bm, 1), lambda i, kk: (i, 0))),
        scratch_shapes=[pltpu.VMEM((bm, d), F32)],
        args=(a, b, res, gamma, beta), vmem_mib=56, comm=comm)


def _mm_dx(pairs, extra, *, extra_scale, ln, bm, bk, name, comm=None):
    s, k = pairs[0][0].shape
    d = pairs[0][1].shape[0]
    bm, bk = min(bm, s), min(bk, k)
    assert s % bm == 0 and k % bk == 0
    nk = k // bk
    npair = len(pairs)
    ch = min(EPILOGUE_ROWS, bm)

    def body(*refs):
        ab = refs[:2 * npair]
        e_ref = refs[2 * npair]
        pos = 2 * npair + 1
        if ln is not None:
            xh_ref, rs_ref, g_ref = refs[pos:pos + 3]
            dz_ref, dzb_ref, dg_ref, db_ref, acc_ref = refs[pos + 3:]
        else:
            dx_ref, acc_ref = refs[pos:]
        i = pl.program_id(0)
        kk = pl.program_id(1)
        part = None
        for p in range(npair):
            t = lax.dot_general(ab[2 * p][...], ab[2 * p + 1][...], (((1,), (1,)), ((), ())),
                                preferred_element_type=F32)
            part = t if part is None else part + t

        @pl.when(kk == 0)
        def _():
            acc_ref[...] = part

        @pl.when(kk > 0)
        def _():
            acc_ref[...] += part

        @pl.when(kk == nk - 1)
        def _():
            if ln is None:
                dx_ref[...] = extra_scale * e_ref[...] + acc_ref[...]
                return

            def chunk(ci, carry):
                dgp, dbp = carry
                rows = pl.ds(pl.multiple_of(ci * ch, ch), ch)
                dx = extra_scale * e_ref[rows, :] + acc_ref[rows, :]
                xh = xh_ref[rows, :]
                dxh = dx * g_ref[...]
                m1 = jnp.mean(dxh, axis=-1, keepdims=True)
                m2 = jnp.mean(dxh * xh, axis=-1, keepdims=True)
                dz = rs_ref[rows, :] * (dxh - m1 - xh * m2)
                dz_ref[rows, :] = dz
                dzb_ref[rows, :] = dz.astype(BF16)
                return dgp + jnp.sum(dx * xh, axis=0, keepdims=True), dbp + jnp.sum(dx, axis=0, keepdims=True)

            zero = jnp.zeros((1, d), F32)
            dgp, dbp = lax.fori_loop(0, bm // ch, chunk, (zero, zero))

            @pl.when(i == 0)
            def _():
                dg_ref[...] = dgp
                db_ref[...] = dbp

            @pl.when(i > 0)
            def _():
                dg_ref[...] += dgp
                db_ref[...] += dbp

    row = pl.BlockSpec((bm, d), lambda i, kk: (i, 0))
    vec = pl.BlockSpec((1, d), lambda i, kk: (0, 0))
    in_specs, args = [], []
    for a, w in pairs:
        in_specs += [pl.BlockSpec((bm, bk), lambda i, kk: (i, kk)), pl.BlockSpec((d, bk), lambda i, kk: (0, kk))]
        args += [a, w]
    in_specs.append(row)
    args.append(extra)
    if ln is not None:
        in_specs += [row, pl.BlockSpec((bm, 1), lambda i, kk: (i, 0)), vec]
        args += list(ln)
        out_shape = (jax.ShapeDtypeStruct((s, d), F32), jax.ShapeDtypeStruct((s, d), BF16),
                     jax.ShapeDtypeStruct((1, d), F32), jax.ShapeDtypeStruct((1, d), F32))
        out_specs = (row, row, vec, vec)
    else:
        out_shape = jax.ShapeDtypeStruct((s, d), F32)
        out_specs = row
    return _call(
        body, name=name, out_shape=out_shape,
        grid=(s // bm, nk), in_specs=in_specs, out_specs=out_specs,
        scratch_shapes=[pltpu.VMEM((bm, d), F32)],
        args=args, vmem_mib=56, comm=comm)


def _ple_loss(x3, x3b, p, wpg, wpp, target, *, bm, bn, name):
    s, d = x3.shape
    dp = p.shape[1]
    bm, bn = min(bm, s), min(bn, d)
    assert s % bm == 0 and d % bn == 0
    inv_d = 1.0 / d

    def body(x_ref, xb_ref, p_ref, wg_ref, wp_ref, t_ref, l_ref, dy_ref, dg_ref, dp_ref):
        first = (pl.program_id(0) == 0) & (pl.program_id(1) == 0)
        gp = jnp.dot(xb_ref[...], wg_ref[...], preferred_element_type=F32)
        pp = jnp.dot(p_ref[...].astype(BF16), wp_ref[...], preferred_element_type=F32)
        sig = jax.nn.sigmoid(gp)
        err = x_ref[...] + sig * pp - t_ref[...]
        part = jnp.sum(err * err)

        @pl.when(first)
        def _():
            l_ref[...] = jnp.zeros_like(l_ref)

        l_ref[...] += part
        dy = err * inv_d
        dy_ref[...] = dy
        dg_ref[...] = (dy * pp * sig * (1.0 - sig)).astype(BF16)
        dp_ref[...] = (dy * sig).astype(BF16)

    blk = pl.BlockSpec((bm, bn), lambda i, j: (i, j))
    return pl.pallas_call(
        body, name=name,
        out_shape=(jax.ShapeDtypeStruct((8, LANES), F32), jax.ShapeDtypeStruct((s, d), F32),
                   jax.ShapeDtypeStruct((s, d), BF16), jax.ShapeDtypeStruct((s, d), BF16)),
        grid=(s // bm, d // bn),
        in_specs=[blk, pl.BlockSpec((bm, d), lambda i, j: (i, 0)), pl.BlockSpec((bm, dp), lambda i, j: (i, 0)),
                  pl.BlockSpec((d, bn), lambda i, j: (0, j)), pl.BlockSpec((dp, bn), lambda i, j: (0, j)), blk],
        out_specs=(pl.BlockSpec((8, LANES), lambda i, j: (0, 0)), blk, blk, blk),
        compiler_params=_cp(("arbitrary", "arbitrary"), 56),
    )(x3, x3b, p, wpg, wpp, target)


def _rope_tables(positions):
    half = ROT_DIMS // 2
    inv_freq = jnp.power(jnp.float32(ROPE_THETA), -jnp.arange(half, dtype=F32) * (2.0 / ROT_DIMS))
    ang = positions.astype(F32)[:, None] * inv_freq
    cos, sin = jnp.cos(ang), jnp.sin(ang)
    s = positions.shape[0]
    zeros = jnp.zeros((s, half), F32)
    rest0 = jnp.zeros((s, HEAD_DIM - ROT_DIMS), F32)
    cf = jnp.concatenate([cos, cos, jnp.ones((s, HEAD_DIM - ROT_DIMS), F32)], axis=1)
    sa = jnp.concatenate([-sin, zeros, rest0], axis=1)
    sb = jnp.concatenate([zeros, sin, rest0], axis=1)
    return cf, sa, sb


def _rotary(t, tabs, *, n_cols, inverse, out_dtype, bs, name):
    s = t.shape[0]
    bs = min(bs, s)
    half = ROT_DIMS // 2

    def body(t_ref, cf_ref, sa_ref, sb_ref, o_ref):
        v = t_ref[...]
        if inverse:
            o = (v * cf_ref[...] + pltpu.roll(v * sa_ref[...], half, 1)
                 + pltpu.roll(v * sb_ref[...], HEAD_DIM - half, 1))
        else:
            o = (v * cf_ref[...] + pltpu.roll(v, HEAD_DIM - half, 1) * sa_ref[...]
                 + pltpu.roll(v, half, 1) * sb_ref[...])
        o_ref[...] = o.astype(out_dtype)

    blk = pl.BlockSpec((bs, HEAD_DIM), lambda i, j: (i, j))
    tab = pl.BlockSpec((bs, HEAD_DIM), lambda i, j: (i, 0))
    return pl.pallas_call(
        body, name=name, out_shape=jax.ShapeDtypeStruct((s, n_cols * HEAD_DIM), out_dtype),
        grid=(s // bs, n_cols), in_specs=[blk, tab, tab, tab], out_specs=blk,
        compiler_params=_cp(("parallel", "arbitrary"), 32),
    )(t, *tabs)


def _attn_blocks():
    out = []
    for g, dil in enumerate(DILATIONS):
        sup = SPAN * dil
        for j in range(ATTN_TILE // sup):
            for r in range(dil):
                out.append((g, j * sup + r, dil, (j - 1) * sup + r if j > 0 else None, ATTN_TILE - sup + r))
    return out


def _rows(ref, start, dil, lead=None):
    idx = pl.ds(start, SPAN, stride=dil) if dil > 1 else pl.ds(start, SPAN)
    return ref[idx, :] if lead is None else ref[lead, idx, :]


def _band_masks(n):
    qi = lax.broadcasted_iota(jnp.int32, (SPAN, 2 * SPAN), 0)
    ki = lax.broadcasted_iota(jnp.int32, (SPAN, 2 * SPAN), 1)
    band = (ki >= qi) & (ki <= qi + SPAN)
    return band, band & ((ki >= SPAN) | (n > 0))


def _attn_fwd(qkr, proj, *, name):
    s = qkr.shape[0]
    t = ATTN_TILE
    assert s % t == 0
    nt = s // t
    scale = HEAD_DIM ** -0.5
    kcol, vcol = N_PATTERNS * N_KV_HEADS, (N_PATTERNS + 1) * N_KV_HEADS
    blocks = _attn_blocks()

    def body(q0, q1, q2, kc_ref, kp_ref, vc_ref, vp_ref, o_ref, l_ref, og, lg):
        n = pl.program_id(1)
        band, band_first = _band_masks(n)
        q_refs = (q0, q1, q2)
        for g, start, dil, prev_in_tile, prev_start in blocks:
            q = _rows(q_refs[g], start, dil).astype(BF16)
            if prev_in_tile is not None:
                kp, vp, mask = _rows(kc_ref, prev_in_tile, dil), _rows(vc_ref, prev_in_tile, dil), band
            else:
                kp, vp, mask = _rows(kp_ref, prev_start, dil), _rows(vp_ref, prev_start, dil), band_first
            kk = jnp.concatenate([kp, _rows(kc_ref, start, dil)], axis=0).astype(BF16)
            vv = jnp.concatenate([vp, _rows(vc_ref, start, dil)], axis=0).astype(BF16)
            sc = lax.dot_general(q, kk, (((1,), (1,)), ((), ())), preferred_element_type=F32) * scale
            sc = jnp.where(mask, sc, -1e30)
            m = jnp.max(sc, axis=-1, keepdims=True)
            e = jnp.exp(sc - m)
            den = jnp.sum(e, axis=-1, keepdims=True)
            o = jnp.dot(e.astype(BF16), vv, preferred_element_type=F32) / den
            idx = pl.ds(start, SPAN, stride=dil) if dil > 1 else pl.ds(start, SPAN)
            og[g, idx, :] = o
            lg[g, idx, :] = jnp.broadcast_to(m + jnp.log(den), (SPAN, HEAD_DIM))
        l0, l1, l2 = lg[0], lg[1], lg[2]
        m = jnp.maximum(jnp.maximum(l0, l1), l2)
        w0, w1, w2 = jnp.exp(l0 - m), jnp.exp(l1 - m), jnp.exp(l2 - m)
        den = w0 + w1 + w2
        o_ref[...] = (w0 * og[0] + w1 * og[1] + w2 * og[2]) / den
        l_ref[...] = m + jnp.log(den)

    def col(c, prev=False):
        if prev:
            return pl.BlockSpec((t, HEAD_DIM), lambda h, n: (jnp.maximum(n - 1, 0), c + h))
        return pl.BlockSpec((t, HEAD_DIM), lambda h, n: (n, c + h))

    out = jax.ShapeDtypeStruct((s, N_KV_HEADS * HEAD_DIM), F32)
    return pl.pallas_call(
        body, name=name, out_shape=(out, out),
        grid=(N_KV_HEADS, nt),
        in_specs=[col(0), col(N_KV_HEADS), col(2 * N_KV_HEADS), col(kcol), col(kcol, True), col(vcol), col(vcol, True)],
        out_specs=(col(0), col(0)),
        scratch_shapes=[pltpu.VMEM((N_PATTERNS, t, HEAD_DIM), F32), pltpu.VMEM((N_PATTERNS, t, HEAD_DIM), F32)],
        compiler_params=_cp(("parallel", "arbitrary"), 48),
    )(qkr, qkr, qkr, qkr, qkr, proj, proj)


def _attn_bwd(qkr, proj, attn, lse, dcat, *, name):
    s = qkr.shape[0]
    t = ATTN_TILE
    nt = s // t
    scale = HEAD_DIM ** -0.5
    kcol, vcol = N_PATTERNS * N_KV_HEADS, (N_PATTERNS + 1) * N_KV_HEADS
    blocks = _attn_blocks()

    def body(q0, q1, q2, kc_ref, kp_ref, vc_ref, vp_ref, o_ref, l_ref, do_ref,
             dq0, dq1, dq2, dk_ref, dv_ref, ck, cv, tkc, tvc, tkp, tvp):
        n = pl.program_id(1)
        for ref in (tkc, tvc, tkp, tvp):
            ref[...] = jnp.zeros_like(ref)

        @pl.when(n < nt)
        def _():
            band, band_first = _band_masks(n)
            q_refs, dq_refs = (q0, q1, q2), (dq0, dq1, dq2)
            for g, start, dil, prev_in_tile, prev_start in blocks:
                idx = pl.ds(start, SPAN, stride=dil) if dil > 1 else pl.ds(start, SPAN)
                q = q_refs[g][idx, :].astype(BF16)
                if prev_in_tile is not None:
                    kp, vp, mask = _rows(kc_ref, prev_in_tile, dil), _rows(vc_ref, prev_in_tile, dil), band
                else:
                    kp, vp, mask = _rows(kp_ref, prev_start, dil), _rows(vp_ref, prev_start, dil), band_first
                kk = jnp.concatenate([kp, kc_ref[idx, :]], axis=0).astype(BF16)
                vv = jnp.concatenate([vp, vc_ref[idx, :]], axis=0).astype(BF16)
                do = do_ref[idx, :]
                dsum = jnp.sum(do * o_ref[idx, :], axis=-1, keepdims=True)
                lrow = l_ref[idx, :][:, :1]
                dob = do.astype(BF16)
                sc = lax.dot_general(q, kk, (((1,), (1,)), ((), ())), preferred_element_type=F32) * scale
                p = jnp.where(mask, jnp.exp(sc - lrow), 0.0)
                dp = lax.dot_general(dob, vv, (((1,), (1,)), ((), ())), preferred_element_type=F32)
                ds = (p * (dp - dsum) * scale).astype(BF16)
                pb = p.astype(BF16)
                dq_refs[g][idx, :] = jnp.dot(ds, kk, preferred_element_type=F32)
                dkk = lax.dot_general(ds, q, (((0,), (0,)), ((), ())), preferred_element_type=F32)
                dvv = lax.dot_general(pb, dob, (((0,), (0,)), ((), ())), preferred_element_type=F32)
                tkc[idx, :] += dkk[SPAN:]
                tvc[idx, :] += dvv[SPAN:]
                if prev_in_tile is not None:
                    pidx = pl.ds(prev_in_tile, SPAN, stride=dil) if dil > 1 else pl.ds(prev_in_tile, SPAN)
                    tkc[pidx, :] += dkk[:SPAN]
                    tvc[pidx, :] += dvv[:SPAN]
                else:
                    pidx = pl.ds(prev_start, SPAN, stride=dil) if dil > 1 else pl.ds(prev_start, SPAN)
                    tkp[pidx, :] += dkk[:SPAN]
                    tvp[pidx, :] += dvv[:SPAN]

        @pl.when(n > 0)
        def _():
            dk_ref[...] = ck[...] + tkp[...]
            dv_ref[...] = (cv[...] + tvp[...]).astype(BF16)

        ck[...] = tkc[...]
        cv[...] = tvc[...]

    def col(c, prev=False):
        if prev:
            return pl.BlockSpec((t, HEAD_DIM), lambda h, n: (jnp.maximum(jnp.minimum(n, nt - 1) - 1, 0), c + h))
        return pl.BlockSpec((t, HEAD_DIM), lambda h, n: (jnp.minimum(n, nt - 1), c + h))

    kv_out = pl.BlockSpec((t, HEAD_DIM), lambda h, n: (jnp.maximum(n - 1, 0), h))
    tile = pltpu.VMEM((t, HEAD_DIM), F32)
    per_head = jax.ShapeDtypeStruct((s, N_KV_HEADS * HEAD_DIM), F32)
    return pl.pallas_call(
        body, name=name,
        out_shape=(per_head, per_head, per_head, per_head, jax.ShapeDtypeStruct((s, N_KV_HEADS * HEAD_DIM), BF16)),
        grid=(N_KV_HEADS, nt + 1),
        in_specs=[col(0), col(N_KV_HEADS), col(2 * N_KV_HEADS), col(kcol), col(kcol, True), col(vcol), col(vcol, True),
                  col(0), col(0), col(0)],
        out_specs=(col(0), col(0), col(0), kv_out, kv_out),
        scratch_shapes=[tile] * 6,
        compiler_params=_cp(("parallel", "arbitrary"), 48),
    )(qkr, qkr, qkr, qkr, qkr, proj, proj, attn, lse, dcat)


GELU_C0 = 0.7978845608028654
GELU_C1 = 0.044715


def _softplus_neg(lam):
    y = jnp.exp(-jnp.abs(lam))
    w = 1.0 + y
    log1p = jnp.where(w == 1.0, y, jnp.log(w) * (y / jnp.where(w == 1.0, 1.0, w - 1.0)))
    return jnp.maximum(-lam, 0.0) + log1p


def _down(cur, prev, k, row):
    if k == 0:
        return cur
    return jnp.where(row < k, pltpu.roll(prev, k, 0), pltpu.roll(cur, k, 0))


def _up(cur, nxt, k, row, tt):
    if k == 0:
        return cur
    return jnp.where(row >= tt - k, pltpu.roll(nxt, tt - k, 0), pltpu.roll(cur, tt - k, 0))


def _lru_gates(x, xp, cw, cb, wr, br, wi, bi, lam, row):
    shifts = [_down(x, xp, k, row) for k in range(CONV_WIDTH)]
    xc = cb
    for j in range(CONV_WIDTH):
        xc = xc + cw[j:j + 1, :] * shifts[CONV_WIDTH - 1 - j]
    xcb = xc.astype(BF16)
    r = jax.nn.sigmoid(jnp.dot(xcb, wr, preferred_element_type=F32) + br)
    i = jax.nn.sigmoid(jnp.dot(xcb, wi, preferred_element_type=F32) + bi)
    c = -LRU_C * _softplus_neg(lam)
    la = c * r
    a = jnp.exp(la)
    mult = jnp.sqrt(jnp.tanh(-la) * (a * a + 1.0))
    return shifts, xc, xcb, r, i, c, a, mult


def _lru_fwd(proj, cw, cb, wr, br, wi, bi, lam, *, tt, name):
    s = proj.shape[0]
    nblk = wr.shape[0]
    c = nblk * LANES
    tt = min(tt, s)
    xcol0 = (N_PATTERNS + 2) * N_KV_HEADS
    ycol0 = xcol0 + nblk

    def body(x_ref, y_ref, cw_ref, cb_ref, wr_ref, br_ref, wi_ref, bi_ref, lam_ref, rec_ref, h_ref, xprev, hc):
        n = pl.program_id(1)

        @pl.when(n == 0)
        def _():
            xprev[...] = jnp.zeros_like(xprev)
            hc[...] = jnp.zeros_like(hc)

        row = lax.broadcasted_iota(jnp.int32, (tt, LANES), 0)
        x = x_ref[...]
        _, xc, _, _, i, _, a, mult = _lru_gates(
            x, xprev[...], cw_ref[...], cb_ref[...], wr_ref[0].astype(BF16), br_ref[...],
            wi_ref[0].astype(BF16), bi_ref[...], lam_ref[...], row)
        av, bv = a, mult * (i * xc)
        k = 1
        while k < tt:
            bs = jnp.where(row < k, 0.0, pltpu.roll(bv, k, 0))
            as_ = jnp.where(row < k, 1.0, pltpu.roll(av, k, 0))
            bv = bv + av * bs
            av = av * as_
            k *= 2
        h = bv + av * hc[0:1, :]
        hc[...] = jnp.broadcast_to(h[tt - 1:tt, :], hc.shape)
        h_ref[...] = h
        y = y_ref[...]
        gel = 0.5 * y * (1.0 + jnp.tanh(GELU_C0 * (y + GELU_C1 * y * y * y)))
        rec_ref[...] = (h * gel).astype(BF16)
        xprev[...] = x

    vec = pl.BlockSpec((1, LANES), lambda b, n: (0, b))
    wblk = pl.BlockSpec((1, LANES, LANES), lambda b, n: (b, 0, 0))
    out = pl.BlockSpec((tt, LANES), lambda b, n: (n, b))
    return pl.pallas_call(
        body, name=name,
        out_shape=(jax.ShapeDtypeStruct((s, c), BF16), jax.ShapeDtypeStruct((s, c), F32)),
        grid=(nblk, s // tt),
        in_specs=[pl.BlockSpec((tt, LANES), lambda b, n: (n, xcol0 + b)),
                  pl.BlockSpec((tt, LANES), lambda b, n: (n, ycol0 + b)),
                  pl.BlockSpec((CONV_WIDTH, LANES), lambda b, n: (0, b)), vec, wblk, vec, wblk, vec, vec],
        out_specs=(out, out),
        scratch_shapes=[pltpu.VMEM((tt, LANES), F32), pltpu.VMEM((8, LANES), F32)],
        compiler_params=_cp(("parallel", "arbitrary"), 32),
    )(proj, proj, cw, cb, wr, br, wi, bi, lam)


def _lru_bwd(proj, hseq, dcat, cw, cb, wr, br, wi, bi, lam, *, tt, name):
    s = proj.shape[0]
    nblk = wr.shape[0]
    c = nblk * LANES
    tt = min(tt, s)
    nt = s // tt
    xcol0 = (N_PATTERNS + 2) * N_KV_HEADS
    ycol0 = xcol0 + nblk
    rcol0 = N_KV_HEADS

    def body(x_ref, xp_ref, y_ref, h_ref, hp_ref, dr_ref, cw_ref, cb_ref, wr_ref, br_ref, wi_ref, bi_ref, lam_ref,
             dx_ref, dy_ref, dcw_ref, dcb_ref, dwr_ref, dbr_ref, dwi_ref, dbi_ref, dlam_ref, dxc_next, gcar, acar):
        n = pl.program_id(1)
        rt = nt - 1 - n

        @pl.when(n == 0)
        def _():
            for ref in (dxc_next, gcar, acar, dcw_ref, dcb_ref, dwr_ref, dbr_ref, dwi_ref, dbi_ref, dlam_ref):
                ref[...] = jnp.zeros_like(ref)

        row = lax.broadcasted_iota(jnp.int32, (tt, LANES), 0)
        x = x_ref[...]
        xp = jnp.where(rt > 0, xp_ref[...], 0.0)
        cwv = cw_ref[...]
        wrb, wib = wr_ref[0].astype(BF16), wi_ref[0].astype(BF16)
        lam_v = lam_ref[...]
        shifts, xc, xcb, r, i, cc, a, mult = _lru_gates(x, xp, cwv, cb_ref[...], wrb, br_ref[...], wib, bi_ref[...],
                                                        lam_v, row)
        h = h_ref[...]
        hp_last = jnp.where(rt > 0, hp_ref[7:8, :], 0.0)
        hprev = jnp.where(row < 1, hp_last, pltpu.roll(h, 1, 0))
        y = y_ref[...]
        y2 = y * y
        th = jnp.tanh(GELU_C0 * (y + GELU_C1 * y2 * y))
        gel = 0.5 * y * (1.0 + th)
        dgel = 0.5 * (1.0 + th) + 0.5 * y * (1.0 - th * th) * GELU_C0 * (1.0 + 3.0 * GELU_C1 * y2)
        drec = dr_ref[...]
        dy_ref[...] = (drec * h * dgel).astype(BF16)
        av = jnp.where(row >= tt - 1, acar[0:1, :], pltpu.roll(a, tt - 1, 0))
        bv = drec * gel
        k = 1
        while k < tt:
            bs = jnp.where(row >= tt - k, 0.0, pltpu.roll(bv, tt - k, 0))
            as_ = jnp.where(row >= tt - k, 1.0, pltpu.roll(av, tt - k, 0))
            bv = bv + av * bs
            av = av * as_
            k *= 2
        g = bv + av * gcar[0:1, :]
        gcar[...] = jnp.broadcast_to(g[0:1, :], gcar.shape)
        acar[...] = jnp.broadcast_to(a[0:1, :], acar.shape)
        da = g * hprev
        d_ixc = g * mult
        dmult = g * (i * xc)
        di = d_ixc * xc
        dxc = d_ixc * i
        a2 = a * a
        dla = da * a - dmult * (a2 / mult)
        dr = dla * cc
        dsp = jnp.sum(dla * r, axis=0, keepdims=True) * (-LRU_C)
        dlam_ref[...] += dsp * (-jax.nn.sigmoid(-lam_v))
        dzr = dr * r * (1.0 - r)
        dzi = di * i * (1.0 - i)
        dbr_ref[...] += jnp.sum(dzr, axis=0, keepdims=True)
        dbi_ref[...] += jnp.sum(dzi, axis=0, keepdims=True)
        dzrb, dzib = dzr.astype(BF16), dzi.astype(BF16)
        tn = (((0,), (0,)), ((), ()))
        ntd = (((1,), (1,)), ((), ()))
        dwr_ref[0] += lax.dot_general(xcb, dzrb, tn, preferred_element_type=F32)
        dwi_ref[0] += lax.dot_general(xcb, dzib, tn, preferred_element_type=F32)
        dxc = (dxc + lax.dot_general(dzrb, wrb, ntd, preferred_element_type=F32)
               + lax.dot_general(dzib, wib, ntd, preferred_element_type=F32))
        dcb_ref[...] += jnp.sum(dxc, axis=0, keepdims=True)
        dcw_ref[...] += jnp.concatenate(
            [jnp.sum(dxc * shifts[CONV_WIDTH - 1 - j], axis=0, keepdims=True) for j in range(CONV_WIDTH)], axis=0)
        nxt = dxc_next[...]
        dx = cwv[0:1, :] * _up(dxc, nxt, CONV_WIDTH - 1, row, tt)
        for j in range(1, CONV_WIDTH):
            dx = dx + cwv[j:j + 1, :] * _up(dxc, nxt, CONV_WIDTH - 1 - j, row, tt)
        dx_ref[...] = dx.astype(BF16)
        dxc_next[...] = dxc

    def tile(col0, prev=False):
        if prev:
            return pl.BlockSpec((tt, LANES), lambda b, n: (jnp.maximum(nt - 2 - n, 0), col0 + b))
        return pl.BlockSpec((tt, LANES), lambda b, n: (nt - 1 - n, col0 + b))

    vec = pl.BlockSpec((1, LANES), lambda b, n: (0, b))
    wblk = pl.BlockSpec((1, LANES, LANES), lambda b, n: (b, 0, 0))
    cwblk = pl.BlockSpec((CONV_WIDTH, LANES), lambda b, n: (0, b))
    hp8 = pl.BlockSpec((8, LANES), lambda b, n: (jnp.maximum((nt - 1 - n) * (tt // 8) - 1, 0), b))
    vshape = jax.ShapeDtypeStruct((1, c), F32)
    wshape = jax.ShapeDtypeStruct((nblk, LANES, LANES), F32)
    return pl.pallas_call(
        body, name=name,
        out_shape=(jax.ShapeDtypeStruct((s, c), BF16), jax.ShapeDtypeStruct((s, c), BF16),
                   jax.ShapeDtypeStruct((CONV_WIDTH, c), F32), vshape, wshape, vshape, wshape, vshape, vshape),
        grid=(nblk, nt),
        in_specs=[tile(xcol0), tile(xcol0, True), tile(ycol0), tile(0), hp8, tile(rcol0),
                  cwblk, vec, wblk, vec, wblk, vec, vec],
        out_specs=(tile(0), tile(0), cwblk, vec, wblk, vec, wblk, vec, vec),
        scratch_shapes=[pltpu.VMEM((tt, LANES), F32), pltpu.VMEM((8, LANES), F32), pltpu.VMEM((8, LANES), F32)],
        compiler_params=_cp(("parallel", "arbitrary"), 32),
    )(proj, proj, proj, hseq, hseq, dcat, cw, cb, wr, br, wi, bi, lam)


ROW_BLOCKS = (256, 128, 64, 32, 16, 8)


def _adamw(w, m, v, gparts, *, name):
    r, c = w.shape
    npart = gparts.shape[0]
    br = _pick(r, ROW_BLOCKS)
    c1 = 1.0 - ADAM_B1 ** ADAM_STEP
    c2 = 1.0 - ADAM_B2 ** ADAM_STEP

    def body(w_ref, m_ref, v_ref, g_ref, go_ref, d_ref, mo_ref, vo_ref):
        g = g_ref[0].astype(F32)
        for q in range(1, npart):
            g = g + g_ref[q].astype(F32)
        mn = ADAM_B1 * m_ref[...] + (1.0 - ADAM_B1) * g
        vn = ADAM_B2 * v_ref[...] + (1.0 - ADAM_B2) * (g * g)
        go_ref[...] = g
        mo_ref[...] = mn
        vo_ref[...] = vn
        d_ref[...] = -ADAM_LR * ((mn / c1) / (jnp.sqrt(vn / c2) + ADAM_EPS) + ADAM_WD * w_ref[...])

    blk = pl.BlockSpec((br, c), lambda i: (i, 0))
    out = jax.ShapeDtypeStruct((r, c), F32)
    return pl.pallas_call(
        body, name=name, out_shape=(out, out, out, out), grid=(r // br,),
        in_specs=[blk, blk, blk, pl.BlockSpec((npart, br, c), lambda i: (0, i, 0))],
        out_specs=(blk, blk, blk, blk),
        compiler_params=_cp(("parallel",), 48),
    )(w, m, v, gparts)


def _sum_parts(parts, *, name):
    npart, r, c = parts.shape
    br = next((b for b in range(min(r, 2048) // 8 * 8, 0, -8) if r % b == 0), r)

    def body(p_ref, o_ref):
        acc = p_ref[0]
        for q in range(1, npart):
            acc = acc + p_ref[q]
        o_ref[...] = acc

    return pl.pallas_call(
        body, name=name, out_shape=jax.ShapeDtypeStruct((r, c), F32), grid=(r // br,),
        in_specs=[pl.BlockSpec((npart, br, c), lambda i: (0, i, 0))],
        out_specs=pl.BlockSpec((br, c), lambda i: (i, 0)),
        compiler_params=_cp(("parallel",), 48),
    )(parts)


HBM = pl.BlockSpec(memory_space=pltpu.HBM)


def _mesh_pos():
    return lax.axis_index("x"), lax.axis_index("y"), lax.axis_index("c")


def _all_gather(shards, *, name):
    comm = _gather_comm(shards)
    na = len(shards)

    def body(*refs):
        ins, outs, sems = refs[:na], refs[na:2 * na], refs[2 * na:]
        comm.start(ins, outs, sems)
        comm.mid(ins, outs, sems)
        comm.end(ins, outs, sems)

    return pl.pallas_call(
        body, name=name, out_shape=tuple(comm.out_shapes),
        in_specs=[HBM] * na, out_specs=tuple([HBM] * na), scratch_shapes=comm.scratch,
    )(*shards)


def _gather_comm(shards):
    na = len(shards)

    def parts(x_refs, out_refs, sems):
        send_sems, recv_sems, local_sems = sems
        x, y, c = _mesh_pos()
        me, sibling = (x, y, c), (x, y, 1 - c)
        chips = [(1 - x, y), (x, 1 - y), (1 - x, 1 - y)]

        def copy(a, k, block, to, src=None):
            px, py, pc = block
            dst = out_refs[a].at[4 * px + 2 * py + pc]
            return pltpu.make_async_remote_copy(
                src_ref=dst if src is None else src, dst_ref=dst,
                send_sem=send_sems.at[a, k], recv_sem=recv_sems.at[a, k],
                device_id=to, device_id_type=MESH)

        def mine(a):
            return pltpu.make_async_copy(x_refs[a], out_refs[a].at[4 * x + 2 * y + c], local_sems.at[a])

        def first(a):
            return [copy(a, 0, me, sibling, src=x_refs[a])] + [
                copy(a, 1 + j, me, (*chip, c), src=x_refs[a]) for j, chip in enumerate(chips)]

        def passed(a, j):
            return copy(a, 4 + j, (*chips[j], c), sibling)

        return me, sibling, chips, c, copy, mine, first, passed

    def start(x_refs, out_refs, sems):
        *_, mine, first, _ = parts(x_refs, out_refs, sems)
        for a in range(na):
            mine(a).start()
            for cp in first(a):
                cp.start()

    def mid(x_refs, out_refs, sems):
        me, _, chips, c, copy, _, _, passed = parts(x_refs, out_refs, sems)
        for j, chip in enumerate(chips):
            for a in range(na):
                copy(a, 1 + j, (*chip, c), me).wait_recv()
                passed(a, j).start()

    def end(x_refs, out_refs, sems):
        me, sibling, chips, c, copy, mine, first, passed = parts(x_refs, out_refs, sems)
        for a in range(na):
            copy(a, 0, sibling, me).wait_recv()
            for j, chip in enumerate(chips):
                copy(a, 4 + j, (*chip, 1 - c), me).wait_recv()
        for a in range(na):
            for cp in first(a) + [passed(a, j) for j in range(3)]:
                cp.wait_send()
            mine(a).wait()

    return _Comm(
        shards, [jax.ShapeDtypeStruct((N_DEV,) + a.shape, a.dtype) for a in shards],
        [pltpu.SemaphoreType.DMA((na, 7)), pltpu.SemaphoreType.DMA((na, 7)), pltpu.SemaphoreType.DMA((na,))],
        start, end, mid)


def _scatter_comm(g8s):
    na = len(g8s)

    def parts(g_refs, buf_refs, sems):
        send_sems, recv_sems, local_sems = sems
        x, y, c = _mesh_pos()
        me_idx = 4 * x + 2 * y + c

        def peer(k):
            return (1 - x if k & 4 else x, 1 - y if k & 2 else y, 1 - c if k & 1 else c)

        def copy(a, k, slot):
            px, py, pc = peer(k)
            return pltpu.make_async_remote_copy(
                src_ref=g_refs[a].at[4 * px + 2 * py + pc], dst_ref=buf_refs[a].at[slot],
                send_sem=send_sems.at[a, k - 1], recv_sem=recv_sems.at[a, k - 1],
                device_id=(px, py, pc), device_id_type=MESH)

        def mine(a):
            return pltpu.make_async_copy(g_refs[a].at[me_idx], buf_refs[a].at[me_idx], local_sems.at[a])

        return me_idx, peer, copy, mine

    def start(g_refs, buf_refs, sems):
        me_idx, _, copy, mine = parts(g_refs, buf_refs, sems)
        for a in range(na):
            mine(a).start()
            for k in range(1, N_DEV):
                copy(a, k, me_idx).start()

    def end(g_refs, buf_refs, sems):
        me_idx, peer, copy, mine = parts(g_refs, buf_refs, sems)
        for a in range(na):
            for k in range(1, N_DEV):
                px, py, pc = peer(k)
                copy(a, k, 4 * px + 2 * py + pc).wait_recv()
        for a in range(na):
            for k in range(1, N_DEV):
                copy(a, k, me_idx).wait_send()
            mine(a).wait()

    return _Comm(
        g8s, [jax.ShapeDtypeStruct(g.shape, g.dtype) for g in g8s],
        [pltpu.SemaphoreType.DMA((na, N_DEV - 1)), pltpu.SemaphoreType.DMA((na, N_DEV - 1)),
         pltpu.SemaphoreType.DMA((na,))],
        start, end)


def _join_comm(c1, c2):
    n1i, n1o, n1s = len(c1.arrays), len(c1.out_shapes), len(c1.scratch)

    def both(f1, f2):
        def run(ins, outs, sems):
            if f1 is not None:
                f1(ins[:n1i], outs[:n1o], sems[:n1s])
            if f2 is not None:
                f2(ins[n1i:], outs[n1o:], sems[n1s:])
        return run

    mid = both(c1.mid, c2.mid) if (c1.mid is not None or c2.mid is not None) else None
    return _Comm(c1.arrays + c2.arrays, c1.out_shapes + c2.out_shapes, c1.scratch + c2.scratch,
                 both(c1.start, c2.start), both(c1.end, c2.end), mid)


BIG_WEIGHTS = ("ffn1_w_gate", "ffn1_w_up", "ffn1_w_down", "w_in", "w_out",
               "ffn2_w_gate", "ffn2_w_up", "ffn2_w_down", "w_ple_proj", "w_ple_gate")
COLUMN_SHARDED = ("ffn1_w_gate", "ffn1_w_up", "w_in", "ffn2_w_gate", "ffn2_w_up", "w_ple_proj", "conv_w")
SMALL_WEIGHTS = ("ln1_g", "ln1_b", "conv_b", "w_rgate", "b_rgate", "w_igate", "b_igate", "lru_lambda",
                 "ln2_g", "ln2_b", "ln3_g", "ln3_b")
SMALL_GRADS = SMALL_WEIGHTS + ("conv_w",)

BM_CANDIDATES = (1024, 512, 256, 128)
BN_CANDIDATES = (1408, 1024, 896, 512, 256, 128)


class _Exchange:
    def __init__(self, full):
        self.full = dict(full)
        self.grads = {}

    def __getitem__(self, name):
        return self.full[name]

    def gather(self, names):
        return None, None

    def scatter(self, names):
        return None, None

    def gather_small(self):
        return None, None


class _MeshExchange(_Exchange):
    def __init__(self, full, shards):
        super().__init__(full)
        self.shards = shards
        self.reduced = {}
        self.small_parts = None

    def gather(self, names):
        def done(outs):
            for n, o in zip(names, outs):
                self.full[n] = _to_full(n, o)
        return _gather_comm([self.shards[n] for n in names]), done

    def scatter(self, names):
        def done(outs):
            self.reduced.update(zip(names, outs))
        return _scatter_comm([_to_owner_blocks(n, self.grads[n]) for n in names]), done

    def gather_small(self):
        def done(outs):
            self.small_parts, = outs
        packed = jnp.concatenate([_rows128(self.grads[n]) for n in SMALL_GRADS], axis=0)
        return _gather_comm([packed]), done


def _carried(comm_done, call):
    comm, done = comm_done
    res = call(comm)
    if comm is None:
        return res
    res, outs = res
    done(outs)
    return res


def _dw(a, b, *, scale=1.0, name, comm=None):
    m, n = a.shape[1], b.shape[1]
    return _mm(a, b, ta=True, scale=scale, out_dtype=BF16, bm=_pick(m, BN_CANDIDATES), bn=_pick(n, BN_CANDIDATES),
               bk=1024, name=name, comm=comm)


def _ffn_bwd(ex, names, saved, xb_in, dz, dzb, ln_in, tag, also=None):
    gate, up, down = names
    g, u, h, _, _ = saved
    f = ex[gate].shape[1]
    ex.grads[down] = _dw(h, dzb, scale=0.5, name=f"{tag}_dwd")
    dg, du = _carried(ex.scatter((down,)), lambda c: _ffn_bwd_dh(
        dzb, ex[down], g, u, scale=0.5, bm=1024, bn=_pick(f, (512, 256, 128)), name=f"{tag}_dh", comm=c))
    ex.grads[gate] = _dw(xb_in, dg, name=f"{tag}_dwg")
    ex.grads[up] = _carried(ex.scatter((gate,)), lambda c: _dw(xb_in, du, name=f"{tag}_dwu", comm=c))
    last = ex.scatter((up,))
    if also is not None and also[0] is not None:
        done_up, done_also = last[1], also[1]
        n_up = len(last[0].out_shapes)
        last = (_join_comm(last[0], also[0]), lambda outs: (done_up(outs[:n_up]), done_also(outs[n_up:])))
    return _carried(last, lambda c: _mm_dx(
        [(dg, ex[gate]), (du, ex[up])], dz, extra_scale=DEEPNORM_ALPHA, ln=ln_in,
        bm=512, bk=_pick(f, (512, 256, 128)), name=f"{tag}_dx", comm=c))


def _local_step(x, p, target, positions, w):
    s, d = x.shape
    tabs = _rope_tables(positions)
    xb = x.astype(BF16)
    f = w["ffn1_w_gate"].shape[1]
    ffn_bn, down_bk = _pick(f, (512, 256, 128)), _pick(f, (1408, 512, 256, 128))
    g1, u1, h1 = _carried(w.gather(("ffn1_w_down", "w_in", "w_out")), lambda c: _ffn_up(
        xb, w["ffn1_w_gate"], w["ffn1_w_up"], bm=1024, bn=ffn_bn, name="ffn1_up", comm=c))
    x1, x1b, xh1, rs1 = _carried(w.gather(("ffn2_w_gate", "ffn2_w_up")), lambda c: _mm_ln(
        h1, w["ffn1_w_down"], x, w["ln1_g"], w["ln1_b"], res_scale=DEEPNORM_ALPHA, mm_scale=0.5,
        bm=512, bk=down_bk, name="ffn1_down_ln", comm=c))
    sv1 = (g1, u1, h1, xh1, rs1)
    pw = w["w_in"].shape[1]
    proj = _carried(w.gather(("ffn2_w_down", "w_ple_gate", "w_ple_proj")), lambda c: _mm(
        x1b, w["w_in"], bm=1024, bn=_pick(pw, BN_CANDIDATES), bk=d, name="in_proj", comm=c))
    nqk = (N_PATTERNS + 1) * N_KV_HEADS
    qkr = _rotary(proj, tabs, n_cols=nqk, inverse=False, out_dtype=F32, bs=1024, name="rotary")
    attn, lse = _attn_fwd(qkr, proj, name="attn_fwd")
    lru_w = (w["conv_w"], w["conv_b"], w["w_rgate"], w["b_rgate"], w["w_igate"], w["b_igate"], w["lru_lambda"])
    rec, hseq = _lru_fwd(proj, *lru_w, tt=512, name="lru_fwd")
    cat = jnp.concatenate([attn.astype(BF16), rec], axis=1)
    x2, x2b, xh2, rs2 = _mm_ln(cat, w["w_out"], x1, w["ln2_g"], w["ln2_b"], res_scale=DEEPNORM_ALPHA, mm_scale=1.0,
                               bm=512, bk=1024, name="out_proj_ln")
    g2, u2, h2 = _ffn_up(x2b, w["ffn2_w_gate"], w["ffn2_w_up"], bm=1024, bn=ffn_bn, name="ffn2_up")
    x3, x3b, xh3, rs3 = _mm_ln(h2, w["ffn2_w_down"], x2, w["ln3_g"], w["ln3_b"], res_scale=DEEPNORM_ALPHA, mm_scale=0.5,
                               bm=512, bk=down_bk, name="ffn2_down_ln")
    sv3 = (g2, u2, h2, xh3, rs3)
    lsum, dy, dgate, dple = _ple_loss(x3, x3b, p, w["w_ple_gate"], w["w_ple_proj"], target,
                                      bm=1024, bn=_pick(d, (512, 256, 128)), name="ple_loss")
    grads = w.grads
    grads["w_ple_gate"] = _dw(x3b, dgate, name="dw_ple_gate")
    grads["w_ple_proj"] = _dw(p, dple, name="dw_ple_proj")
    dz3, dz3b, grads["ln3_g"], grads["ln3_b"] = _carried(w.scatter(("w_ple_gate", "w_ple_proj")), lambda c: _mm_dx(
        [(dgate, w["w_ple_gate"])], dy, extra_scale=1.0, ln=(xh3, rs3, w["ln3_g"]), bm=512, bk=1024, name="ple_dx", comm=c))
    dz2, dz2b, grads["ln2_g"], grads["ln2_b"] = _ffn_bwd(
        w, ("ffn2_w_gate", "ffn2_w_up", "ffn2_w_down"), sv3, x2b, dz3, dz3b, (xh2, rs2, w["ln2_g"]), "ffn2")
    grads["w_out"] = _dw(cat, dz2b, name="dw_out")
    dcat = _carried(w.scatter(("w_out",)), lambda c: _mm(
        dz2b, w["w_out"], tb=True, bm=1024, bn=1024, bk=d, name="out_proj_dx", comm=c))
    dq0, dq1, dq2, dk, dvb = _attn_bwd(qkr, proj, attn, lse, dcat, name="attn_bwd")
    nh = N_KV_HEADS
    dqkv = [_rotary(t, tabs, n_cols=nh, inverse=True, out_dtype=BF16, bs=1024, name=f"rotary_bwd{i}")
            for i, t in enumerate((dq0, dq1, dq2, dk))]
    (dxb, dyb, grads["conv_w"], grads["conv_b"], grads["w_rgate"], grads["b_rgate"], grads["w_igate"],
     grads["b_igate"], grads["lru_lambda"]) = _lru_bwd(proj, hseq, dcat, *lru_w, tt=512, name="lru_bwd")
    dproj = jnp.concatenate(dqkv + [dvb, dxb, dyb], axis=1)
    grads["w_in"] = _dw(x1b, dproj, name="dw_in")
    dz1, dz1b, grads["ln1_g"], grads["ln1_b"] = _carried(w.scatter(("w_in",)), lambda c: _mm_dx(
        [(dproj, w["w_in"])], dz2, extra_scale=DEEPNORM_ALPHA, ln=(xh1, rs1, w["ln1_g"]),
        bm=512, bk=_pick(pw, (1408, 1024, 896, 512, 256, 128)), name="in_proj_dx", comm=c))
    grad_x = _ffn_bwd(w, ("ffn1_w_gate", "ffn1_w_up", "ffn1_w_down"), sv1, xb, dz1, dz1b, None, "ffn1",
                      also=w.gather_small())
    return lsum, grad_x


def _to_full(name, gathered):
    if name in COLUMN_SHARDED:
        _, r, c = gathered.shape
        return jnp.transpose(gathered, (1, 0, 2)).reshape(r, N_DEV * c)
    return gathered.reshape((N_DEV * gathered.shape[1],) + gathered.shape[2:])


def _to_owner_blocks(name, full):
    if name in COLUMN_SHARDED:
        r, c = full.shape
        return jnp.transpose(full.reshape(r, N_DEV, c // N_DEV), (1, 0, 2))
    return full.reshape((N_DEV, full.shape[0] // N_DEV) + full.shape[1:])


def _rows128(a):
    flat = a.reshape(-1, LANES)
    pad = (-flat.shape[0]) % 8
    return jnp.pad(flat, ((0, pad), (0, 0))) if pad else flat


def kernel(x, p, positions, ffn1_w_gate, ffn1_w_up, ffn1_w_down, ln1_g, ln1_b, w_in, conv_w, conv_b, w_rgate, b_rgate, w_igate, b_igate, lru_lambda, w_out, ln2_g, ln2_b, ffn2_w_gate, ffn2_w_up, ffn2_w_down, ln3_g, ln3_b, w_ple_proj, w_ple_gate, loss_target, m_ffn1_w_gate, m_ffn1_w_up, m_ffn1_w_down, m_ln1_g, m_ln1_b, m_w_in, m_conv_w, m_conv_b, m_w_rgate, m_b_rgate, m_w_igate, m_b_igate, m_lru_lambda, m_w_out, m_ln2_g, m_ln2_b, m_ffn2_w_gate, m_ffn2_w_up, m_ffn2_w_down, m_ln3_g, m_ln3_b, m_w_ple_proj, m_w_ple_gate, v_ffn1_w_gate, v_ffn1_w_up, v_ffn1_w_down, v_ln1_g, v_ln1_b, v_w_in, v_conv_w, v_conv_b, v_w_rgate, v_b_rgate, v_w_igate, v_b_igate, v_lru_lambda, v_w_out, v_ln2_g, v_ln2_b, v_ffn2_w_gate, v_ffn2_w_up, v_ffn2_w_down, v_ln3_g, v_ln3_b, v_w_ple_proj, v_w_ple_gate):
    names = ("ffn1_w_gate", "ffn1_w_up", "ffn1_w_down", "ln1_g", "ln1_b", "w_in", "conv_w", "conv_b", "w_rgate",
             "b_rgate", "w_igate", "b_igate", "lru_lambda", "w_out", "ln2_g", "ln2_b", "ffn2_w_gate", "ffn2_w_up",
             "ffn2_w_down", "ln3_g", "ln3_b", "w_ple_proj", "w_ple_gate")
    ws = (ffn1_w_gate, ffn1_w_up, ffn1_w_down, ln1_g, ln1_b, w_in, conv_w, conv_b, w_rgate, b_rgate, w_igate, b_igate,
          lru_lambda, w_out, ln2_g, ln2_b, ffn2_w_gate, ffn2_w_up, ffn2_w_down, ln3_g, ln3_b, w_ple_proj, w_ple_gate)
    ms = (m_ffn1_w_gate, m_ffn1_w_up, m_ffn1_w_down, m_ln1_g, m_ln1_b, m_w_in, m_conv_w, m_conv_b, m_w_rgate, m_b_rgate,
          m_w_igate, m_b_igate, m_lru_lambda, m_w_out, m_ln2_g, m_ln2_b, m_ffn2_w_gate, m_ffn2_w_up, m_ffn2_w_down,
          m_ln3_g, m_ln3_b, m_w_ple_proj, m_w_ple_gate)
    vs = (v_ffn1_w_gate, v_ffn1_w_up, v_ffn1_w_down, v_ln1_g, v_ln1_b, v_w_in, v_conv_w, v_conv_b, v_w_rgate, v_b_rgate,
          v_w_igate, v_b_igate, v_lru_lambda, v_w_out, v_ln2_g, v_ln2_b, v_ffn2_w_gate, v_ffn2_w_up, v_ffn2_w_down,
          v_ln3_g, v_ln3_b, v_w_ple_proj, v_w_ple_gate)
    def local(a):
        return a[0] if a.ndim >= 3 else a

    w_loc = {n: local(a) for n, a in zip(names, ws)}
    m_loc = {n: local(a) for n, a in zip(names, ms)}
    v_loc = {n: local(a) for n, a in zip(names, vs)}
    out_shapes = {n: a.shape for n, a in zip(names, ws)}

    first = ("ffn1_w_gate", "ffn1_w_up", "conv_w")
    shards = {n: w_loc[n].astype(BF16) for n in BIG_WEIGHTS}
    gathered = _all_gather([shards["ffn1_w_gate"], shards["ffn1_w_up"], w_loc["conv_w"]], name="gather_first")
    full = {n: _to_full(n, g) for n, g in zip(first, gathered)}
    for n in SMALL_WEIGHTS:
        full[n] = w_loc[n]
    ex = _MeshExchange(full, shards)

    lsum, grad_x = _local_step(x[0], p[0, 0], loss_target[0], positions[0], ex)
    grads, reduced = ex.grads, ex.reduced
    d_model = x.shape[-1]
    loss = lax.psum(lsum[0, 0] * (0.5 / d_model), ("x", "y", "c"))

    small = SMALL_GRADS
    summed = _sum_parts(ex.small_parts, name="sum_small_grads")
    small_grads, row = {}, 0
    for n in small:
        rows = grads[n].size // LANES
        small_grads[n] = summed[row:row + rows].reshape(grads[n].shape)
        row += rows + (-rows) % 8
    me = 4 * lax.axis_index("x") + 2 * lax.axis_index("y") + lax.axis_index("c")
    cw_cols = w_loc["conv_w"].shape[1]
    small_grads["conv_w"] = lax.dynamic_slice_in_dim(small_grads["conv_w"], me * cw_cols, cw_cols, axis=1)

    out_g, out_d, out_m, out_v = {}, {}, {}, {}
    for n in names:
        wl, ml, vl = w_loc[n], m_loc[n], v_loc[n]
        shape = wl.shape
        if n in BIG_WEIGHTS:
            gparts = reduced[n]
        else:
            gparts = small_grads[n].reshape((1,) + shape)
        if wl.ndim == 3:
            wl, ml, vl = (t.reshape(-1, shape[-1]) for t in (wl, ml, vl))
            gparts = gparts.reshape(gparts.shape[0], -1, shape[-1])
        res = _adamw(wl, ml, vl, gparts, name=f"adamw_{n}")
        out_g[n], out_d[n], out_m[n], out_v[n] = (t.reshape(out_shapes[n]) for t in res)

    return (loss, grad_x[None], *[out_g[n] for n in names], *[out_d[n] for n in names],
            *[out_m[n] for n in names], *[out_v[n] for n in names])
```

```python
import jax
import jax.numpy as jnp
from jax import lax
from jax.experimental import pallas as pl
from jax.experimental.pallas import tpu as pltpu

F32 = jnp.float32
BF16 = jnp.bfloat16

N_DEV = 8
LANES = 128
MIB = 1 << 20

HEAD_DIM = 128
N_KV_HEADS = 4
DILATIONS = (1, 4, 16)
N_PATTERNS = 3
SPAN = 128
ROT_DIMS = 32
ROPE_THETA = 500000.0
LRU_C = 8.0
CONV_WIDTH = 4
LN_EPS = 1e-5
DEEPNORM_ALPHA = 2.0 ** 0.25
ATTN_TILE = SPAN * DILATIONS[-1]

ADAM_LR = 0.001
ADAM_B1 = 0.9
ADAM_B2 = 0.999
ADAM_EPS = 1e-08
ADAM_WD = 0.01
ADAM_STEP = 10

MESH = pl.DeviceIdType.MESH
EPILOGUE_ROWS = 64


def _cp(semantics, vmem_mib):
    return pltpu.CompilerParams(dimension_semantics=semantics, vmem_limit_bytes=vmem_mib * MIB)


def _pick(n, candidates):
    for c in candidates:
        if n % c == 0:
            return c
    return n


class _Comm:
    def __init__(self, arrays, out_shapes, scratch, start, end, mid=None):
        self.arrays, self.out_shapes, self.scratch = list(arrays), list(out_shapes), list(scratch)
        self.start, self.mid, self.end = start, mid, end


def _call(body, *, name, grid, in_specs, out_specs, out_shape, args, scratch_shapes=(), vmem_mib, comm=None):
    single = not isinstance(out_shape, (tuple, list))
    out_shape_t = (out_shape,) if single else tuple(out_shape)
    out_specs_t = (out_specs,) if single else tuple(out_specs)
    params = _cp(("arbitrary",) * len(grid), vmem_mib)
    if comm is None:
        res = pl.pallas_call(body, name=name, grid=grid, in_specs=list(in_specs), out_specs=out_specs_t,
                             out_shape=out_shape_t, scratch_shapes=list(scratch_shapes), compiler_params=params)(*args)
        return res[0] if single else res
    n_in, n_out, n_scr = len(args), len(out_shape_t), len(scratch_shapes)
    nci, nco = len(comm.arrays), len(comm.out_shapes)
    total = 1
    for g in grid:
        total *= g

    def wrapped(*refs):
        ins, refs = refs[:n_in], refs[n_in:]
        cin, refs = refs[:nci], refs[nci:]
        outs, refs = refs[:n_out], refs[n_out:]
        cout, refs = refs[:nco], refs[nco:]
        scr, csem = refs[:n_scr], refs[n_scr:]
        step = pl.program_id(0)
        for ax in range(1, len(grid)):
            step = step * grid[ax] + pl.program_id(ax)

        @pl.when(step == 0)
        def _():
            comm.start(cin, cout, csem)

        body(*ins, *outs, *scr)
        if comm.mid is not None:
            @pl.when(step == (3 * total) // 4)
            def _():
                comm.mid(cin, cout, csem)

        @pl.when(step == total - 1)
        def _():
            comm.end(cin, cout, csem)

    hbm = pl.BlockSpec(memory_space=pltpu.HBM)
    res = pl.pallas_call(
        wrapped, name=name, grid=grid,
        in_specs=list(in_specs) + [hbm] * nci,
        out_specs=out_specs_t + (hbm,) * nco,
        out_shape=out_shape_t + tuple(comm.out_shapes),
        scratch_shapes=list(scratch_shapes) + comm.scratch,
        compiler_params=params)(*args, *comm.arrays)
    own, extra = res[:n_out], res[n_out:]
    return (own[0] if single else own), extra


def _mm(a, b, *, ta=False, tb=False, out_dtype=F32, scale=1.0, bm, bn, bk, name, comm=None):
    m, k = (a.shape[1], a.shape[0]) if ta else a.shape
    n = b.shape[0] if tb else b.shape[1]
    bm, bn, bk = min(bm, m), min(bn, n), min(bk, k)
    assert m % bm == 0 and n % bn == 0 and k % bk == 0, (name, m, n, k, bm, bn, bk)
    nk = k // bk
    a_spec = pl.BlockSpec((bk, bm), lambda i, j, kk: (kk, i)) if ta else pl.BlockSpec((bm, bk), lambda i, j, kk: (i, kk))
    b_spec = pl.BlockSpec((bn, bk), lambda i, j, kk: (j, kk)) if tb else pl.BlockSpec((bk, bn), lambda i, j, kk: (kk, j))
    dn = (((0 if ta else 1,), (1 if tb else 0,)), ((), ()))

    def body(a_ref, b_ref, o_ref, *acc):
        part = lax.dot_general(a_ref[...].astype(BF16), b_ref[...].astype(BF16), dn, preferred_element_type=F32)
        if nk == 1:
            o_ref[...] = (part * scale).astype(out_dtype)
            return
        acc_ref, = acc
        kk = pl.program_id(2)

        @pl.when(kk == 0)
        def _():
            acc_ref[...] = part

        @pl.when(kk > 0)
        def _():
            acc_ref[...] += part

        @pl.when(kk == nk - 1)
        def _():
            o_ref[...] = (acc_ref[...] * scale).astype(out_dtype)

    return _call(
        body, name=name,
        out_shape=jax.ShapeDtypeStruct((m, n), out_dtype),
        grid=(m // bm, n // bn, nk),
        in_specs=[a_spec, b_spec],
        out_specs=pl.BlockSpec((bm, bn), lambda i, j, kk: (i, j)),
        scratch_shapes=[pltpu.VMEM((bm, bn), F32)] if nk > 1 else [],
        args=(a, b), vmem_mib=56, comm=comm)


def _ffn_up(xb, wg, wu, *, bm, bn, name, comm=None):
    s, d = xb.shape
    f = wg.shape[1]
    bm, bn = min(bm, s), min(bn, f)
    assert s % bm == 0 and f % bn == 0

    def body(x_ref, wg_ref, wu_ref, hg_ref, hu_ref, h_ref):
        x = x_ref[...]
        g = jnp.dot(x, wg_ref[...], preferred_element_type=F32)
        u = jnp.dot(x, wu_ref[...], preferred_element_type=F32)
        sig = jax.nn.sigmoid(g)
        silu = g * sig
        hg_ref[...] = (u * (sig * (1.0 + g * (1.0 - sig)))).astype(BF16)
        hu_ref[...] = silu.astype(BF16)
        h_ref[...] = (silu * u).astype(BF16)

    out = jax.ShapeDtypeStruct((s, f), BF16)
    blk = pl.BlockSpec((bm, bn), lambda i, j: (i, j))
    return _call(
        body, name=name, out_shape=(out, out, out),
        grid=(s // bm, f // bn),
        in_specs=[pl.BlockSpec((bm, d), lambda i, j: (i, 0)),
                  pl.BlockSpec((d, bn), lambda i, j: (0, j)),
                  pl.BlockSpec((d, bn), lambda i, j: (0, j))],
        out_specs=(blk, blk, blk),
        args=(xb, wg, wu), vmem_mib=56, comm=comm)


def _ffn_bwd_dh(dzb, wdt, g, u, *, scale, bm, bn, name, chunks=2, comm=None):
    s, d = dzb.shape
    f = wdt.shape[1]
    bm, bn = min(bm, s), min(bn, f)
    assert s % bm == 0 and f % bn == 0

    cr = bm // chunks

    def body(dz_ref, wd_ref, hg_ref, hu_ref, dg_ref, du_ref):
        for r in range(chunks):
            rows = slice(r * cr, (r + 1) * cr)
            dh = jnp.dot(dz_ref[rows, :], wd_ref[...], preferred_element_type=F32) * scale
            dg_ref[rows, :] = (dh * hg_ref[rows, :].astype(F32)).astype(BF16)
            du_ref[rows, :] = (dh * hu_ref[rows, :].astype(F32)).astype(BF16)

    out = jax.ShapeDtypeStruct((s, f), BF16)
    blk = pl.BlockSpec((bm, bn), lambda i, j: (i, j))
    return _call(
        body, name=name, out_shape=(out, out),
        grid=(s // bm, f // bn),
        in_specs=[pl.BlockSpec((bm, d), lambda i, j: (i, 0)),
                  pl.BlockSpec((d, bn), lambda i, j: (0, j)), blk, blk],
        out_specs=(blk, blk),
        args=(dzb, wdt, g, u), vmem_mib=56, comm=comm)


def _mm_ln(a, b, res, gamma, beta, *, res_scale, mm_scale, bm, bk, name, comm=None):
    s, k = a.shape
    d = b.shape[1]
    bm, bk = min(bm, s), min(bk, k)
    assert s % bm == 0 and k % bk == 0
    nk = k // bk
    ch = min(EPILOGUE_ROWS, bm)

    def body(a_ref, b_ref, r_ref, g_ref, be_ref, y_ref, yb_ref, xh_ref, rs_ref, acc_ref):
        kk = pl.program_id(1)
        part = jnp.dot(a_ref[...], b_ref[...], preferred_element_type=F32)

        @pl.when(kk == 0)
        def _():
            acc_ref[...] = part

        @pl.when(kk > 0)
        def _():
            acc_ref[...] += part

        @pl.when(kk == nk - 1)
        def _():
            def chunk(ci, carry):
                rows = pl.ds(pl.multiple_of(ci * ch, ch), ch)
                z = res_scale * r_ref[rows, :] + mm_scale * acc_ref[rows, :]
                mu = jnp.mean(z, axis=-1, keepdims=True)
                zc = z - mu
                var = jnp.mean(zc * zc, axis=-1, keepdims=True)
                rstd = lax.rsqrt(var + LN_EPS)
                xh = zc * rstd
                y = xh * g_ref[...] + be_ref[...]
                y_ref[rows, :] = y
                yb_ref[rows, :] = y.astype(BF16)
                xh_ref[rows, :] = xh
                rs_ref[rows, :] = rstd
                return carry

            lax.fori_loop(0, bm // ch, chunk, 0)

    row = pl.BlockSpec((bm, d), lambda i, kk: (i, 0))
    vec = pl.BlockSpec((1, d), lambda i, kk: (0, 0))
    return _call(
        body, name=name,
        out_shape=(jax.ShapeDtypeStruct((s, d), F32), jax.ShapeDtypeStruct((s, d), BF16),
                   jax.ShapeDtypeStruct((s, d), F32), jax.ShapeDtypeStruct((s, 1), F32)),
        grid=(s // bm, nk),
        in_specs=[pl.BlockSpec((bm, bk), lambda i, kk: (i, kk)),
                  pl.BlockSpec((bk, d), lambda i, kk: (kk, 0)), row, vec, vec],
        out_specs=(row, row, row, pl.BlockSpec((bm, 1), lambda i, kk: (i, 0))),
        scratch_shapes=[pltpu.VMEM((bm, d), F32)],
        args=(a, b, res, gamma, beta), vmem_mib=56, comm=comm)


def _mm_dx(pairs, extra, *, extra_scale, ln, bm, bk, name, comm=None):
    s, k = pairs[0][0].shape
    d = pairs[0][1].shape[1]
    bm, bk = min(bm, s), min(bk, k)
    assert s % bm == 0 and k % bk == 0
    nk = k // bk
    npair = len(pairs)
    ch = min(EPILOGUE_ROWS, bm)

    def body(*refs):
        ab = refs[:2 * npair]
        e_ref = refs[2 * npair]
        pos = 2 * npair + 1
        if ln is not None:
            xh_ref, rs_ref, g_ref = refs[pos:pos + 3]
            dz_ref, dzb_ref, dg_ref, db_ref, acc_ref = refs[pos + 3:]
        else:
            dx_ref, acc_ref = refs[pos:]
        i = pl.program_id(0)
        kk = pl.program_id(1)
        part = None
        for p in range(npair):
            t = jnp.dot(ab[2 * p][...], ab[2 * p + 1][...], preferred_element_type=F32)
            part = t if part is None else part + t

        @pl.when(kk == 0)
        def _():
            acc_ref[...] = part

        @pl.when(kk > 0)
        def _():
            acc_ref[...] += part

        @pl.when(kk == nk - 1)
        def _():
            if ln is None:
                dx_ref[...] = extra_scale * e_ref[...] + acc_ref[...]
                return

            def chunk(ci, carry):
                dgp, dbp = carry
                rows = pl.ds(pl.multiple_of(ci * ch, ch), ch)
                dx = extra_scale * e_ref[rows, :] + acc_ref[rows, :]
                xh = xh_ref[rows, :]
                dxh = dx * g_ref[...]
                m1 = jnp.mean(dxh, axis=-1, keepdims=True)
                m2 = jnp.mean(dxh * xh, axis=-1, keepdims=True)
                dz = rs_ref[rows, :] * (dxh - m1 - xh * m2)
                dz_ref[rows, :] = dz
                dzb_ref[rows, :] = dz.astype(BF16)
                return dgp + jnp.sum(dx * xh, axis=0, keepdims=True), dbp + jnp.sum(dx, axis=0, keepdims=True)

            zero = jnp.zeros((1, d), F32)
            dgp, dbp = lax.fori_loop(0, bm // ch, chunk, (zero, zero))

            @pl.when(i == 0)
            def _():
                dg_ref[...] = dgp
                db_ref[...] = dbp

            @pl.when(i > 0)
            def _():
                dg_ref[...] += dgp
                db_ref[...] += dbp

    row = pl.BlockSpec((bm, d), lambda i, kk: (i, 0))
    vec = pl.BlockSpec((1, d), lambda i, kk: (0, 0))
    in_specs, args = [], []
    for a, w in pairs:
        in_specs += [pl.BlockSpec((bm, bk), lambda i, kk: (i, kk)), pl.BlockSpec((bk, d), lambda i, kk: (kk, 0))]
        args += [a, w]
    in_specs.append(row)
    args.append(extra)
    if ln is not None:
        in_specs += [row, pl.BlockSpec((bm, 1), lambda i, kk: (i, 0)), vec]
        args += list(ln)
        out_shape = (jax.ShapeDtypeStruct((s, d), F32), jax.ShapeDtypeStruct((s, d), BF16),
                     jax.ShapeDtypeStruct((1, d), F32), jax.ShapeDtypeStruct((1, d), F32))
        out_specs = (row, row, vec, vec)
    else:
        out_shape = jax.ShapeDtypeStruct((s, d), F32)
        out_specs = row
    return _call(
        body, name=name, out_shape=out_shape,
        grid=(s // bm, nk), in_specs=in_specs, out_specs=out_specs,
        scratch_shapes=[pltpu.VMEM((bm, d), F32)],
        args=args, vmem_mib=56, comm=comm)


def _ffn_dx(dg, du, wgt, wut, extra, *, extra_scale, bm, bn, name, comm=None):
    s, f = dg.shape
    d = wgt.shape[1]
    bm, bn = min(bm, s), min(bn, d)
    assert s % bm == 0 and d % bn == 0

    def body(dg_ref, du_ref, wg_ref, wu_ref, e_ref, o_ref):
        acc = jnp.dot(dg_ref[...], wg_ref[...], preferred_element_type=F32)
        acc = acc + jnp.dot(du_ref[...], wu_ref[...], preferred_element_type=F32)
        o_ref[...] = extra_scale * e_ref[...] + acc

    rows = pl.BlockSpec((bm, f), lambda i, j: (i, 0))
    cols = pl.BlockSpec((f, bn), lambda i, j: (0, j))
    blk = pl.BlockSpec((bm, bn), lambda i, j: (i, j))
    return _call(
        body, name=name, out_shape=jax.ShapeDtypeStruct((s, d), F32),
        grid=(s // bm, d // bn), in_specs=[rows, rows, cols, cols, blk], out_specs=blk,
        args=(dg, du, wgt, wut, extra), vmem_mib=58, comm=comm)


def _ln_bwd(dx, xhat, rstd, gamma, *, bm, name):
    s, d = dx.shape
    bm = min(bm, s)
    assert s % bm == 0
    ch = min(EPILOGUE_ROWS, bm)

    def body(dx_ref, xh_ref, rs_ref, g_ref, dz_ref, dzb_ref, dg_ref, db_ref):
        def chunk(ci, carry):
            dgp, dbp = carry
            rows = pl.ds(pl.multiple_of(ci * ch, ch), ch)
            dxv = dx_ref[rows, :]
            xh = xh_ref[rows, :]
            dxh = dxv * g_ref[...]
            m1 = jnp.mean(dxh, axis=-1, keepdims=True)
            m2 = jnp.mean(dxh * xh, axis=-1, keepdims=True)
            dz = rs_ref[rows, :] * (dxh - m1 - xh * m2)
            dz_ref[rows, :] = dz
            dzb_ref[rows, :] = dz.astype(BF16)
            return dgp + jnp.sum(dxv * xh, axis=0, keepdims=True), dbp + jnp.sum(dxv, axis=0, keepdims=True)

        zero = jnp.zeros((1, d), F32)
        dgp, dbp = lax.fori_loop(0, bm // ch, chunk, (zero, zero))
        i = pl.program_id(0)

        @pl.when(i == 0)
        def _():
            dg_ref[...] = dgp
            db_ref[...] = dbp

        @pl.when(i > 0)
        def _():
            dg_ref[...] += dgp
            db_ref[...] += dbp

    row = pl.BlockSpec((bm, d), lambda i: (i, 0))
    vec = pl.BlockSpec((1, d), lambda i: (0, 0))
    return _call(
        body, name=name,
        out_shape=(jax.ShapeDtypeStruct((s, d), F32), jax.ShapeDtypeStruct((s, d), BF16),
                   jax.ShapeDtypeStruct((1, d), F32), jax.ShapeDtypeStruct((1, d), F32)),
        grid=(s // bm,), in_specs=[row, row, pl.BlockSpec((bm, 1), lambda i: (i, 0)), vec],
        out_specs=(row, row, vec, vec), args=(dx, xhat, rstd, gamma), vmem_mib=48)


def _ple_loss(x3, x3b, p, wpg, wpp, target, *, bm, bn, name):
    s, d = x3.shape
    dp = p.shape[1]
    bm, bn = min(bm, s), min(bn, d)
    assert s % bm == 0 and d % bn == 0
    inv_d = 1.0 / d

    def body(x_ref, xb_ref, p_ref, wg_ref, wp_ref, t_ref, l_ref, dy_ref, dg_ref, dp_ref):
        first = (pl.program_id(0) == 0) & (pl.program_id(1) == 0)
        gp = jnp.dot(xb_ref[...], wg_ref[...], preferred_element_type=F32)
        pp = jnp.dot(p_ref[...].astype(BF16), wp_ref[...], preferred_element_type=F32)
        sig = jax.nn.sigmoid(gp)
        err = x_ref[...] + sig * pp - t_ref[...]
        part = jnp.sum(err * err)

        @pl.when(first)
        def _():
            l_ref[...] = jnp.zeros_like(l_ref)

        l_ref[...] += part
        dy = err * inv_d
        dy_ref[...] = dy
        dg_ref[...] = (dy * pp * sig * (1.0 - sig)).astype(BF16)
        dp_ref[...] = (dy * sig).astype(BF16)

    blk = pl.BlockSpec((bm, bn), lambda i, j: (i, j))
    return pl.pallas_call(
        body, name=name,
        out_shape=(jax.ShapeDtypeStruct((8, LANES), F32), jax.ShapeDtypeStruct((s, d), F32),
                   jax.ShapeDtypeStruct((s, d), BF16), jax.ShapeDtypeStruct((s, d), BF16)),
        grid=(s // bm, d // bn),
        in_specs=[blk, pl.BlockSpec((bm, d), lambda i, j: (i, 0)), pl.BlockSpec((bm, dp), lambda i, j: (i, 0)),
                  pl.BlockSpec((d, bn), lambda i, j: (0, j)), pl.BlockSpec((dp, bn), lambda i, j: (0, j)), blk],
        out_specs=(pl.BlockSpec((8, LANES), lambda i, j: (0, 0)), blk, blk, blk),
        compiler_params=_cp(("arbitrary", "arbitrary"), 56),
    )(x3, x3b, p, wpg, wpp, target)


def _rope_tables(positions):
    half = ROT_DIMS // 2
    inv_freq = jnp.power(jnp.float32(ROPE_THETA), -jnp.arange(half, dtype=F32) * (2.0 / ROT_DIMS))
    ang = positions.astype(F32)[:, None] * inv_freq
    cos, sin = jnp.cos(ang), jnp.sin(ang)
    s = positions.shape[0]
    zeros = jnp.zeros((s, half), F32)
    rest0 = jnp.zeros((s, HEAD_DIM - ROT_DIMS), F32)
    cf = jnp.concatenate([cos, cos, jnp.ones((s, HEAD_DIM - ROT_DIMS), F32)], axis=1)
    sa = jnp.concatenate([-sin, zeros, rest0], axis=1)
    sb = jnp.concatenate([zeros, sin, rest0], axis=1)
    return cf, sa, sb


def _rotary(t, tabs, *, n_cols, inverse, out_dtype, bs, name):
    s = t.shape[0]
    bs = min(bs, s)
    half = ROT_DIMS // 2

    def body(t_ref, cf_ref, sa_ref, sb_ref, o_ref):
        v = t_ref[...]
        if inverse:
            o = (v * cf_ref[...] + pltpu.roll(v * sa_ref[...], half, 1)
                 + pltpu.roll(v * sb_ref[...], HEAD_DIM - half, 1))
        else:
            o = (v * cf_ref[...] + pltpu.roll(v, HEAD_DIM - half, 1) * sa_ref[...]
                 + pltpu.roll(v, half, 1) * sb_ref[...])
        o_ref[...] = o.astype(out_dtype)

    blk = pl.BlockSpec((bs, HEAD_DIM), lambda i, j: (i, j))
    tab = pl.BlockSpec((bs, HEAD_DIM), lambda i, j: (i, 0))
    return pl.pallas_call(
        body, name=name, out_shape=jax.ShapeDtypeStruct((s, n_cols * HEAD_DIM), out_dtype),
        grid=(s // bs, n_cols), in_specs=[blk, tab, tab, tab], out_specs=blk,
        compiler_params=_cp(("parallel", "arbitrary"), 32),
    )(t, *tabs)


def _attn_blocks():
    out = []
    for g, dil in enumerate(DILATIONS):
        sup = SPAN * dil
        for j in range(ATTN_TILE // sup):
            for r in range(dil):
                out.append((g, j * sup + r, dil, (j - 1) * sup + r if j > 0 else None, ATTN_TILE - sup + r))
    return out


def _rows(ref, start, dil, lead=None):
    idx = pl.ds(start, SPAN, stride=dil) if dil > 1 else pl.ds(start, SPAN)
    return ref[idx, :] if lead is None else ref[lead, idx, :]


def _band_masks(n):
    qi = lax.broadcasted_iota(jnp.int32, (SPAN, 2 * SPAN), 0)
    ki = lax.broadcasted_iota(jnp.int32, (SPAN, 2 * SPAN), 1)
    band = (ki >= qi) & (ki <= qi + SPAN)
    return band, band & ((ki >= SPAN) | (n > 0))


def _attn_fwd(qkr, proj, *, name):
    s = qkr.shape[0]
    t = ATTN_TILE
    assert s % t == 0
    nt = s // t
    scale = HEAD_DIM ** -0.5
    kcol, vcol = N_PATTERNS * N_KV_HEADS, (N_PATTERNS + 1) * N_KV_HEADS
    blocks = _attn_blocks()

    def body(q0, q1, q2, kc_ref, kp_ref, vc_ref, vp_ref, o_ref, l_ref, og, lg):
        n = pl.program_id(1)
        band, band_first = _band_masks(n)
        q_refs = (q0, q1, q2)
        for g, start, dil, prev_in_tile, prev_start in blocks:
            q = _rows(q_refs[g], start, dil).astype(BF16)
            if prev_in_tile is not None:
                kp, vp, mask = _rows(kc_ref, prev_in_tile, dil), _rows(vc_ref, prev_in_tile, dil), band
            else:
                kp, vp, mask = _rows(kp_ref, prev_start, dil), _rows(vp_ref, prev_start, dil), band_first
            kk = jnp.concatenate([kp, _rows(kc_ref, start, dil)], axis=0).astype(BF16)
            vv = jnp.concatenate([vp, _rows(vc_ref, start, dil)], axis=0).astype(BF16)
            sc = lax.dot_general(q, kk, (((1,), (1,)), ((), ())), preferred_element_type=F32) * scale
            sc = jnp.where(mask, sc, -1e30)
            m = jnp.max(sc, axis=-1, keepdims=True)
            e = jnp.exp(sc - m)
            den = jnp.sum(e, axis=-1, keepdims=True)
            o = jnp.dot(e.astype(BF16), vv, preferred_element_type=F32) / den
            idx = pl.ds(start, SPAN, stride=dil) if dil > 1 else pl.ds(start, SPAN)
            og[g, idx, :] = o
            lg[g, idx, :] = jnp.broadcast_to(m + jnp.log(den), (SPAN, HEAD_DIM))
        l0, l1, l2 = lg[0], lg[1], lg[2]
        m = jnp.maximum(jnp.maximum(l0, l1), l2)
        w0, w1, w2 = jnp.exp(l0 - m), jnp.exp(l1 - m), jnp.exp(l2 - m)
        den = w0 + w1 + w2
        o_ref[...] = (w0 * og[0] + w1 * og[1] + w2 * og[2]) / den
        l_ref[...] = m + jnp.log(den)

    def col(c, prev=False):
        if prev:
            return pl.BlockSpec((t, HEAD_DIM), lambda h, n: (jnp.maximum(n - 1, 0), c + h))
        return pl.BlockSpec((t, HEAD_DIM), lambda h, n: (n, c + h))

    out = jax.ShapeDtypeStruct((s, N_KV_HEADS * HEAD_DIM), F32)
    return pl.pallas_call(
        body, name=name, out_shape=(out, out),
        grid=(N_KV_HEADS, nt),
        in_specs=[col(0), col(N_KV_HEADS), col(2 * N_KV_HEADS), col(kcol), col(kcol, True), col(vcol), col(vcol, True)],
        out_specs=(col(0), col(0)),
        scratch_shapes=[pltpu.VMEM((N_PATTERNS, t, HEAD_DIM), F32), pltpu.VMEM((N_PATTERNS, t, HEAD_DIM), F32)],
        compiler_params=_cp(("parallel", "arbitrary"), 48),
    )(qkr, qkr, qkr, qkr, qkr, proj, proj)


def _attn_bwd(qkr, proj, attn, lse, dcat, *, name):
    s = qkr.shape[0]
    t = ATTN_TILE
    nt = s // t
    scale = HEAD_DIM ** -0.5
    kcol, vcol = N_PATTERNS * N_KV_HEADS, (N_PATTERNS + 1) * N_KV_HEADS
    blocks = _attn_blocks()

    def body(q0, q1, q2, kc_ref, kp_ref, vc_ref, vp_ref, o_ref, l_ref, do_ref,
             dq0, dq1, dq2, dk_ref, dv_ref, ck, cv, tkc, tvc, tkp, tvp):
        n = pl.program_id(1)
        for ref in (tkc, tvc, tkp, tvp):
            ref[...] = jnp.zeros_like(ref)

        @pl.when(n < nt)
        def _():
            band, band_first = _band_masks(n)
            q_refs, dq_refs = (q0, q1, q2), (dq0, dq1, dq2)
            for g, start, dil, prev_in_tile, prev_start in blocks:
                idx = pl.ds(start, SPAN, stride=dil) if dil > 1 else pl.ds(start, SPAN)
                q = q_refs[g][idx, :].astype(BF16)
                if prev_in_tile is not None:
                    kp, vp, mask = _rows(kc_ref, prev_in_tile, dil), _rows(vc_ref, prev_in_tile, dil), band
                else:
                    kp, vp, mask = _rows(kp_ref, prev_start, dil), _rows(vp_ref, prev_start, dil), band_first
                kk = jnp.concatenate([kp, kc_ref[idx, :]], axis=0).astype(BF16)
                vv = jnp.concatenate([vp, vc_ref[idx, :]], axis=0).astype(BF16)
                do = do_ref[idx, :]
                dsum = jnp.sum(do * o_ref[idx, :], axis=-1, keepdims=True)
                lrow = l_ref[idx, :][:, :1]
                dob = do.astype(BF16)
                sc = lax.dot_general(q, kk, (((1,), (1,)), ((), ())), preferred_element_type=F32) * scale
                p = jnp.where(mask, jnp.exp(sc - lrow), 0.0)
                dp = lax.dot_general(dob, vv, (((1,), (1,)), ((), ())), preferred_element_type=F32)
                ds = (p * (dp - dsum) * scale).astype(BF16)
                pb = p.astype(BF16)
                dq_refs[g][idx, :] = jnp.dot(ds, kk, preferred_element_type=F32)
                dkk = lax.dot_general(ds, q, (((0,), (0,)), ((), ())), preferred_element_type=F32)
                dvv = lax.dot_general(pb, dob, (((0,), (0,)), ((), ())), preferred_element_type=F32)
                tkc[idx, :] += dkk[SPAN:]
                tvc[idx, :] += dvv[SPAN:]
                if prev_in_tile is not None:
                    pidx = pl.ds(prev_in_tile, SPAN, stride=dil) if dil > 1 else pl.ds(prev_in_tile, SPAN)
                    tkc[pidx, :] += dkk[:SPAN]
                    tvc[pidx, :] += dvv[:SPAN]
                else:
                    pidx = pl.ds(prev_start, SPAN, stride=dil) if dil > 1 else pl.ds(prev_start, SPAN)
                    tkp[pidx, :] += dkk[:SPAN]
                    tvp[pidx, :] += dvv[:SPAN]

        @pl.when(n > 0)
        def _():
            dk_ref[...] = ck[...] + tkp[...]
            dv_ref[...] = (cv[...] + tvp[...]).astype(BF16)

        ck[...] = tkc[...]
        cv[...] = tvc[...]

    def col(c, prev=False):
        if prev:
            return pl.BlockSpec((t, HEAD_DIM), lambda h, n: (jnp.maximum(jnp.minimum(n, nt - 1) - 1, 0), c + h))
        return pl.BlockSpec((t, HEAD_DIM), lambda h, n: (jnp.minimum(n, nt - 1), c + h))

    kv_out = pl.BlockSpec((t, HEAD_DIM), lambda h, n: (jnp.maximum(n - 1, 0), h))
    tile = pltpu.VMEM((t, HEAD_DIM), F32)
    per_head = jax.ShapeDtypeStruct((s, N_KV_HEADS * HEAD_DIM), F32)
    return pl.pallas_call(
        body, name=name,
        out_shape=(per_head, per_head, per_head, per_head, jax.ShapeDtypeStruct((s, N_KV_HEADS * HEAD_DIM), BF16)),
        grid=(N_KV_HEADS, nt + 1),
        in_specs=[col(0), col(N_KV_HEADS), col(2 * N_KV_HEADS), col(kcol), col(kcol, True), col(vcol), col(vcol, True),
                  col(0), col(0), col(0)],
        out_specs=(col(0), col(0), col(0), kv_out, kv_out),
        scratch_shapes=[tile] * 6,
        compiler_params=_cp(("parallel", "arbitrary"), 48),
    )(qkr, qkr, qkr, qkr, qkr, proj, proj, attn, lse, dcat)


GELU_C0 = 0.7978845608028654
GELU_C1 = 0.044715


def _softplus_neg(lam):
    y = jnp.exp(-jnp.abs(lam))
    w = 1.0 + y
    log1p = jnp.where(w == 1.0, y, jnp.log(w) * (y / jnp.where(w == 1.0, 1.0, w - 1.0)))
    return jnp.maximum(-lam, 0.0) + log1p


def _down(cur, prev, k, row):
    if k == 0:
        return cur
    return jnp.where(row < k, pltpu.roll(prev, k, 0), pltpu.roll(cur, k, 0))


def _up(cur, nxt, k, row, tt):
    if k == 0:
        return cur
    return jnp.where(row >= tt - k, pltpu.roll(nxt, tt - k, 0), pltpu.roll(cur, tt - k, 0))


def _lru_gates(x, xp, cw, cb, wr, br, wi, bi, lam, row):
    shifts = [_down(x, xp, k, row) for k in range(CONV_WIDTH)]
    xc = cb
    for j in range(CONV_WIDTH):
        xc = xc + cw[j:j + 1, :] * shifts[CONV_WIDTH - 1 - j]
    xcb = xc.astype(BF16)
    r = jax.nn.sigmoid(jnp.dot(xcb, wr, preferred_element_type=F32) + br)
    i = jax.nn.sigmoid(jnp.dot(xcb, wi, preferred_element_type=F32) + bi)
    c = -LRU_C * _softplus_neg(lam)
    la = c * r
    a = jnp.exp(la)
    mult = jnp.sqrt(jnp.tanh(-la) * (a * a + 1.0))
    return shifts, xc, xcb, r, i, c, a, mult


def _lru_fwd(proj, cw, cb, wr, br, wi, bi, lam, *, tt, name):
    s = proj.shape[0]
    nblk = wr.shape[0]
    c = nblk * LANES
    tt = min(tt, s)
    xcol0 = (N_PATTERNS + 2) * N_KV_HEADS
    ycol0 = xcol0 + nblk

    def body(x_ref, y_ref, cw_ref, cb_ref, wr_ref, br_ref, wi_ref, bi_ref, lam_ref, rec_ref, h_ref, xprev, hc):
        n = pl.program_id(1)

        @pl.when(n == 0)
        def _():
            xprev[...] = jnp.zeros_like(xprev)
            hc[...] = jnp.zeros_like(hc)

        row = lax.broadcasted_iota(jnp.int32, (tt, LANES), 0)
        x = x_ref[...]
        _, xc, _, _, i, _, a, mult = _lru_gates(
            x, xprev[...], cw_ref[...], cb_ref[...], wr_ref[0].astype(BF16), br_ref[...],
            wi_ref[0].astype(BF16), bi_ref[...], lam_ref[...], row)
        av, bv = a, mult * (i * xc)
        k = 1
        while k < tt:
            bs = jnp.where(row < k, 0.0, pltpu.roll(bv, k, 0))
            as_ = jnp.where(row < k, 1.0, pltpu.roll(av, k, 0))
            bv = bv + av * bs
            av = av * as_
            k *= 2
        h = bv + av * hc[0:1, :]
        hc[...] = jnp.broadcast_to(h[tt - 1:tt, :], hc.shape)
        h_ref[...] = h
        y = y_ref[...]
        gel = 0.5 * y * (1.0 + jnp.tanh(GELU_C0 * (y + GELU_C1 * y * y * y)))
        rec_ref[...] = (h * gel).astype(BF16)
        xprev[...] = x

    vec = pl.BlockSpec((1, LANES), lambda b, n: (0, b))
    wblk = pl.BlockSpec((1, LANES, LANES), lambda b, n: (b, 0, 0))
    out = pl.BlockSpec((tt, LANES), lambda b, n: (n, b))
    return pl.pallas_call(
        body, name=name,
        out_shape=(jax.ShapeDtypeStruct((s, c), BF16), jax.ShapeDtypeStruct((s, c), F32)),
        grid=(nblk, s // tt),
        in_specs=[pl.BlockSpec((tt, LANES), lambda b, n: (n, xcol0 + b)),
                  pl.BlockSpec((tt, LANES), lambda b, n: (n, ycol0 + b)),
                  pl.BlockSpec((CONV_WIDTH, LANES), lambda b, n: (0, b)), vec, wblk, vec, wblk, vec, vec],
        out_specs=(out, out),
        scratch_shapes=[pltpu.VMEM((tt, LANES), F32), pltpu.VMEM((8, LANES), F32)],
        compiler_params=_cp(("parallel", "arbitrary"), 32),
    )(proj, proj, cw, cb, wr, br, wi, bi, lam)


def _lru_bwd(proj, hseq, dcat, cw, cb, wr, br, wi, bi, lam, *, tt, name):
    s = proj.shape[0]
    nblk = wr.shape[0]
    c = nblk * LANES
    tt = min(tt, s)
    nt = s // tt
    xcol0 = (N_PATTERNS + 2) * N_KV_HEADS
    ycol0 = xcol0 + nblk
    rcol0 = N_KV_HEADS

    def body(x_ref, xp_ref, y_ref, h_ref, hp_ref, dr_ref, cw_ref, cb_ref, wr_ref, br_ref, wi_ref, bi_ref, lam_ref,
             dx_ref, dy_ref, dcw_ref, dcb_ref, dwr_ref, dbr_ref, dwi_ref, dbi_ref, dlam_ref, dxc_next, gcar, acar):
        n = pl.program_id(1)
        rt = nt - 1 - n

        @pl.when(n == 0)
        def _():
            for ref in (dxc_next, gcar, acar, dcw_ref, dcb_ref, dwr_ref, dbr_ref, dwi_ref, dbi_ref, dlam_ref):
                ref[...] = jnp.zeros_like(ref)

        row = lax.broadcasted_iota(jnp.int32, (tt, LANES), 0)
        x = x_ref[...]
        xp = jnp.where(rt > 0, xp_ref[...], 0.0)
        cwv = cw_ref[...]
        wrb, wib = wr_ref[0].astype(BF16), wi_ref[0].astype(BF16)
        lam_v = lam_ref[...]
        shifts, xc, xcb, r, i, cc, a, mult = _lru_gates(x, xp, cwv, cb_ref[...], wrb, br_ref[...], wib, bi_ref[...],
                                                        lam_v, row)
        h = h_ref[...]
        hp_last = jnp.where(rt > 0, hp_ref[7:8, :], 0.0)
        hprev = jnp.where(row < 1, hp_last, pltpu.roll(h, 1, 0))
        y = y_ref[...]
        y2 = y * y
        th = jnp.tanh(GELU_C0 * (y + GELU_C1 * y2 * y))
        gel = 0.5 * y * (1.0 + th)
        dgel = 0.5 * (1.0 + th) + 0.5 * y * (1.0 - th * th) * GELU_C0 * (1.0 + 3.0 * GELU_C1 * y2)
        drec = dr_ref[...]
        dy_ref[...] = (drec * h * dgel).astype(BF16)
        av = jnp.where(row >= tt - 1, acar[0:1, :], pltpu.roll(a, tt - 1, 0))
        bv = drec * gel
        k = 1
        while k < tt:
            bs = jnp.where(row >= tt - k, 0.0, pltpu.roll(bv, tt - k, 0))
            as_ = jnp.where(row >= tt - k, 1.0, pltpu.roll(av, tt - k, 0))
            bv = bv + av * bs
            av = av * as_
            k *= 2
        g = bv + av * gcar[0:1, :]
        gcar[...] = jnp.broadcast_to(g[0:1, :], gcar.shape)
        acar[...] = jnp.broadcast_to(a[0:1, :], acar.shape)
        da = g * hprev
        d_ixc = g * mult
        dmult = g * (i * xc)
        di = d_ixc * xc
        dxc = d_ixc * i
        a2 = a * a
        dla = da * a - dmult * (a2 / mult)
        dr = dla * cc
        dsp = jnp.sum(dla * r, axis=0, keepdims=True) * (-LRU_C)
        dlam_ref[...] += dsp * (-jax.nn.sigmoid(-lam_v))
        dzr = dr * r * (1.0 - r)
        dzi = di * i * (1.0 - i)
        dbr_ref[...] += jnp.sum(dzr, axis=0, keepdims=True)
        dbi_ref[...] += jnp.sum(dzi, axis=0, keepdims=True)
        dzrb, dzib = dzr.astype(BF16), dzi.astype(BF16)
        tn = (((0,), (0,)), ((), ()))
        ntd = (((1,), (1,)), ((), ()))
        dwr_ref[0] += lax.dot_general(xcb, dzrb, tn, preferred_element_type=F32)
        dwi_ref[0] += lax.dot_general(xcb, dzib, tn, preferred_element_type=F32)
        dxc = (dxc + lax.dot_general(dzrb, wrb, ntd, preferred_element_type=F32)
               + lax.dot_general(dzib, wib, ntd, preferred_element_type=F32))
        dcb_ref[...] += jnp.sum(dxc, axis=0, keepdims=True)
        dcw_ref[...] += jnp.concatenate(
            [jnp.sum(dxc * shifts[CONV_WIDTH - 1 - j], axis=0, keepdims=True) for j in range(CONV_WIDTH)], axis=0)
        nxt = dxc_next[...]
        dx = cwv[0:1, :] * _up(dxc, nxt, CONV_WIDTH - 1, row, tt)
        for j in range(1, CONV_WIDTH):
            dx = dx + cwv[j:j + 1, :] * _up(dxc, nxt, CONV_WIDTH - 1 - j, row, tt)
        dx_ref[...] = dx.astype(BF16)
        dxc_next[...] = dxc

    def tile(col0, prev=False):
        if prev:
            return pl.BlockSpec((tt, LANES), lambda b, n: (jnp.maximum(nt - 2 - n, 0), col0 + b))
        return pl.BlockSpec((tt, LANES), lambda b, n: (nt - 1 - n, col0 + b))

    vec = pl.BlockSpec((1, LANES), lambda b, n: (0, b))
    wblk = pl.BlockSpec((1, LANES, LANES), lambda b, n: (b, 0, 0))
    cwblk = pl.BlockSpec((CONV_WIDTH, LANES), lambda b, n: (0, b))
    hp8 = pl.BlockSpec((8, LANES), lambda b, n: (jnp.maximum((nt - 1 - n) * (tt // 8) - 1, 0), b))
    vshape = jax.ShapeDtypeStruct((1, c), F32)
    wshape = jax.ShapeDtypeStruct((nblk, LANES, LANES), F32)
    return pl.pallas_call(
        body, name=name,
        out_shape=(jax.ShapeDtypeStruct((s, c), BF16), jax.ShapeDtypeStruct((s, c), BF16),
                   jax.ShapeDtypeStruct((CONV_WIDTH, c), F32), vshape, wshape, vshape, wshape, vshape, vshape),
        grid=(nblk, nt),
        in_specs=[tile(xcol0), tile(xcol0, True), tile(ycol0), tile(0), hp8, tile(rcol0),
                  cwblk, vec, wblk, vec, wblk, vec, vec],
        out_specs=(tile(0), tile(0), cwblk, vec, wblk, vec, wblk, vec, vec),
        scratch_shapes=[pltpu.VMEM((tt, LANES), F32), pltpu.VMEM((8, LANES), F32), pltpu.VMEM((8, LANES), F32)],
        compiler_params=_cp(("parallel", "arbitrary"), 32),
    )(proj, proj, proj, hseq, hseq, dcat, cw, cb, wr, br, wi, bi, lam)


ROW_BLOCKS = (256, 128, 64, 32, 16, 8)


def _adamw(w, m, v, gparts, *, name):
    r, c = w.shape
    npart = gparts.shape[0]
    br = _pick(r, ROW_BLOCKS)
    c1 = 1.0 - ADAM_B1 ** ADAM_STEP
    c2 = 1.0 - ADAM_B2 ** ADAM_STEP

    def body(w_ref, m_ref, v_ref, g_ref, go_ref, d_ref, mo_ref, vo_ref):
        g = g_ref[0].astype(F32)
        for q in range(1, npart):
            g = g + g_ref[q].astype(F32)
        mn = ADAM_B1 * m_ref[...] + (1.0 - ADAM_B1) * g
        vn = ADAM_B2 * v_ref[...] + (1.0 - ADAM_B2) * (g * g)
        go_ref[...] = g
        mo_ref[...] = mn
        vo_ref[...] = vn
        d_ref[...] = -ADAM_LR * ((mn / c1) / (jnp.sqrt(vn / c2) + ADAM_EPS) + ADAM_WD * w_ref[...])

    blk = pl.BlockSpec((br, c), lambda i: (i, 0))
    out = jax.ShapeDtypeStruct((r, c), F32)
    return pl.pallas_call(
        body, name=name, out_shape=(out, out, out, out), grid=(r // br,),
        in_specs=[blk, blk, blk, pl.BlockSpec((npart, br, c), lambda i: (0, i, 0))],
        out_specs=(blk, blk, blk, blk),
        compiler_params=_cp(("parallel",), 48),
    )(w, m, v, gparts)


def _sum_parts(parts, *, name):
    npart, r, c = parts.shape
    br = next((b for b in range(min(r, 2048) // 8 * 8, 0, -8) if r % b == 0), r)

    def body(p_ref, o_ref):
        acc = p_ref[0]
        for q in range(1, npart):
            acc = acc + p_ref[q]
        o_ref[...] = acc

    return pl.pallas_call(
        body, name=name, out_shape=jax.ShapeDtypeStruct((r, c), F32), grid=(r // br,),
        in_specs=[pl.BlockSpec((npart, br, c), lambda i: (0, i, 0))],
        out_specs=pl.BlockSpec((br, c), lambda i: (i, 0)),
        compiler_params=_cp(("parallel",), 48),
    )(parts)


HBM = pl.BlockSpec(memory_space=pltpu.HBM)


def _mesh_pos():
    return lax.axis_index("x"), lax.axis_index("y"), lax.axis_index("c")


def _all_gather(shards, *, name):
    comm = _gather_comm(shards)
    na = len(shards)

    def body(*refs):
        ins, outs, sems = refs[:na], refs[na:2 * na], refs[2 * na:]
        comm.start(ins, outs, sems)
        comm.mid(ins, outs, sems)
        comm.end(ins, outs, sems)

    return pl.pallas_call(
        body, name=name, out_shape=tuple(comm.out_shapes),
        in_specs=[HBM] * na, out_specs=tuple([HBM] * na), scratch_shapes=comm.scratch,
    )(*shards)


def _gather_comm(shards):
    na = len(shards)

    def parts(x_refs, out_refs, sems):
        send_sems, recv_sems, local_sems = sems
        x, y, c = _mesh_pos()
        me, sibling = (x, y, c), (x, y, 1 - c)
        chips = [(1 - x, y), (x, 1 - y), (1 - x, 1 - y)]

        def copy(a, k, block, to, src=None):
            px, py, pc = block
            dst = out_refs[a].at[4 * px + 2 * py + pc]
            return pltpu.make_async_remote_copy(
                src_ref=dst if src is None else src, dst_ref=dst,
                send_sem=send_sems.at[a, k], recv_sem=recv_sems.at[a, k],
                device_id=to, device_id_type=MESH)

        def mine(a):
            return pltpu.make_async_copy(x_refs[a], out_refs[a].at[4 * x + 2 * y + c], local_sems.at[a])

        def first(a):
            return [copy(a, 0, me, sibling, src=x_refs[a])] + [
                copy(a, 1 + j, me, (*chip, c), src=x_refs[a]) for j, chip in enumerate(chips)]

        def passed(a, j):
            return copy(a, 4 + j, (*chips[j], c), sibling)

        return me, sibling, chips, c, copy, mine, first, passed

    def start(x_refs, out_refs, sems):
        *_, mine, first, _ = parts(x_refs, out_refs, sems)
        for a in range(na):
            mine(a).start()
            for cp in first(a):
                cp.start()

    def mid(x_refs, out_refs, sems):
        me, _, chips, c, copy, _, _, passed = parts(x_refs, out_refs, sems)
        for j, chip in enumerate(chips):
            for a in range(na):
                copy(a, 1 + j, (*chip, c), me).wait_recv()
                passed(a, j).start()

    def end(x_refs, out_refs, sems):
        me, sibling, chips, c, copy, mine, first, passed = parts(x_refs, out_refs, sems)
        for a in range(na):
            copy(a, 0, sibling, me).wait_recv()
            for j, chip in enumerate(chips):
                copy(a, 4 + j, (*chip, 1 - c), me).wait_recv()
        for a in range(na):
            for cp in first(a) + [passed(a, j) for j in range(3)]:
                cp.wait_send()
            mine(a).wait()

    return _Comm(
        shards, [jax.ShapeDtypeStruct((N_DEV,) + a.shape, a.dtype) for a in shards],
        [pltpu.SemaphoreType.DMA((na, 7)), pltpu.SemaphoreType.DMA((na, 7)), pltpu.SemaphoreType.DMA((na,))],
        start, end, mid)


def _scatter_comm(g8s):
    na = len(g8s)

    def parts(g_refs, buf_refs, sems):
        send_sems, recv_sems, local_sems = sems
        x, y, c = _mesh_pos()
        me_idx = 4 * x + 2 * y + c

        def peer(k):
            return (1 - x if k & 4 else x, 1 - y if k & 2 else y, 1 - c if k & 1 else c)

        def copy(a, k, slot):
            px, py, pc = peer(k)
            return pltpu.make_async_remote_copy(
                src_ref=g_refs[a].at[4 * px + 2 * py + pc], dst_ref=buf_refs[a].at[slot],
                send_sem=send_sems.at[a, k - 1], recv_sem=recv_sems.at[a, k - 1],
                device_id=(px, py, pc), device_id_type=MESH)

        def mine(a):
            return pltpu.make_async_copy(g_refs[a].at[me_idx], buf_refs[a].at[me_idx], local_sems.at[a])

        return me_idx, peer, copy, mine

    def start(g_refs, buf_refs, sems):
        me_idx, _, copy, mine = parts(g_refs, buf_refs, sems)
        for a in range(na):
            mine(a).start()
            for k in range(1, N_DEV):
                copy(a, k, me_idx).start()

    def end(g_refs, buf_refs, sems):
        me_idx, peer, copy, mine = parts(g_refs, buf_refs, sems)
        for a in range(na):
            for k in range(1, N_DEV):
                px, py, pc = peer(k)
                copy(a, k, 4 * px + 2 * py + pc).wait_recv()
        for a in range(na):
            for k in range(1, N_DEV):
                copy(a, k, me_idx).wait_send()
            mine(a).wait()

    return _Comm(
        g8s, [jax.ShapeDtypeStruct(g.shape, g.dtype) for g in g8s],
        [pltpu.SemaphoreType.DMA((na, N_DEV - 1)), pltpu.SemaphoreType.DMA((na, N_DEV - 1)),
         pltpu.SemaphoreType.DMA((na,))],
        start, end)


def _join_comm(c1, c2):
    n1i, n1o, n1s = len(c1.arrays), len(c1.out_shapes), len(c1.scratch)

    def both(f1, f2):
        def run(ins, outs, sems):
            if f1 is not None:
                f1(ins[:n1i], outs[:n1o], sems[:n1s])
            if f2 is not None:
                f2(ins[n1i:], outs[n1o:], sems[n1s:])
        return run

    mid = both(c1.mid, c2.mid) if (c1.mid is not None or c2.mid is not None) else None
    return _Comm(c1.arrays + c2.arrays, c1.out_shapes + c2.out_shapes, c1.scratch + c2.scratch,
                 both(c1.start, c2.start), both(c1.end, c2.end), mid)


BIG_WEIGHTS = ("ffn1_w_gate", "ffn1_w_up", "ffn1_w_down", "w_in", "w_out",
               "ffn2_w_gate", "ffn2_w_up", "ffn2_w_down", "w_ple_proj", "w_ple_gate")
COLUMN_SHARDED = ("ffn1_w_gate", "ffn1_w_up", "w_in", "ffn2_w_gate", "ffn2_w_up", "w_ple_proj", "conv_w")
SMALL_WEIGHTS = ("ln1_g", "ln1_b", "conv_b", "w_rgate", "b_rgate", "w_igate", "b_igate", "lru_lambda",
                 "ln2_g", "ln2_b", "ln3_g", "ln3_b")
SMALL_GRADS = SMALL_WEIGHTS + ("conv_w",)


class _Exchange:
    def __init__(self, full):
        self.full = dict(full)
        self.transposed = {}
        self.grads = {}

    def __getitem__(self, name):
        return self.full[name]

    def t(self, name):
        if name not in self.transposed:
            self.transposed[name] = self.full[name].T
        return self.transposed[name]

    def gather(self, names):
        return None, None

    def scatter(self, names):
        return None, None

    def gather_small(self):
        return None, None


class _MeshExchange(_Exchange):
    def __init__(self, full, shards):
        super().__init__(full)
        self.shards = shards
        self.reduced = {}
        self.small_parts = None

    def gather(self, names):
        def done(outs):
            for n, o in zip(names, outs):
                self.take(n, o)
        return _gather_comm([self.shards[n] for n in names]), done

    def take(self, name, gathered):
        stacked = gathered.reshape((N_DEV * gathered.shape[1],) + gathered.shape[2:])
        if name in COLUMN_SHARDED:
            self.transposed[name] = stacked
            self.full[name] = stacked.T
        else:
            self.full[name] = stacked

    def scatter(self, names):
        def done(outs):
            self.reduced.update(zip(names, outs))
        return _scatter_comm([_to_owner_blocks(n, self.grads[n]) for n in names]), done

    def gather_small(self):
        def done(outs):
            self.small_parts, = outs
        packed = jnp.concatenate([_rows128(self.grads[n]) for n in SMALL_GRADS], axis=0)
        return _gather_comm([packed]), done


def _carried(comm_done, call):
    comm, done = comm_done
    res = call(comm)
    if comm is None:
        return res
    res, outs = res
    done(outs)
    return res


def _dw(a, b, *, scale=1.0, name, comm=None):
    k, m = a.shape
    n = b.shape[1]
    return _mm(a, b, ta=True, scale=scale, out_dtype=BF16, bm=_pick(m, (1024, 512, 256, 128)),
               bn=_pick(n, (512, 256, 128)), bk=k, name=name, comm=comm)


def _ffn_bwd(ex, names, saved, xb_in, dz, dzb, ln_in, tag, also=None):
    gate, up, down = names
    g, u, h, _, _ = saved
    f = ex[gate].shape[1]
    ex.grads[down] = _dw(h, dzb, scale=0.5, name=f"{tag}_dwd")
    dg, du = _carried(ex.scatter((down,)), lambda c: _ffn_bwd_dh(
        dzb, ex.t(down), g, u, scale=0.5, bm=1024, bn=_pick(f, (512, 256, 128)), name=f"{tag}_dh", comm=c))
    ex.grads[gate] = _dw(xb_in, dg, name=f"{tag}_dwg")
    ex.grads[up] = _carried(ex.scatter((gate,)), lambda c: _dw(xb_in, du, name=f"{tag}_dwu", comm=c))
    last = ex.scatter((up,))
    if also is not None and also[0] is not None:
        done_up, done_also = last[1], also[1]
        n_up = len(last[0].out_shapes)
        last = (_join_comm(last[0], also[0]), lambda outs: (done_up(outs[:n_up]), done_also(outs[n_up:])))
    d = dz.shape[1]
    dx = _carried(last, lambda c: _ffn_dx(
        dg, du, ex.t(gate), ex.t(up), dz, extra_scale=DEEPNORM_ALPHA,
        bm=512, bn=_pick(d, (512, 256, 128)), name=f"{tag}_dx", comm=c))
    return dx if ln_in is None else _ln_bwd(dx, *ln_in, bm=256, name=f"{tag}_ln_bwd")


def _local_step(x, p, target, positions, w):
    s, d = x.shape
    tabs = _rope_tables(positions)
    xb = x.astype(BF16)
    f = w["ffn1_w_gate"].shape[1]
    ffn_bn, down_bk = _pick(f, (512, 256, 128)), _pick(f, (1408, 512, 256, 128))
    g1, u1, h1 = _carried(w.gather(("ffn1_w_down", "w_in", "w_out")), lambda c: _ffn_up(
        xb, w["ffn1_w_gate"], w["ffn1_w_up"], bm=1024, bn=ffn_bn, name="ffn1_up", comm=c))
    x1, x1b, xh1, rs1 = _carried(w.gather(("ffn2_w_gate", "ffn2_w_up")), lambda c: _mm_ln(
        h1, w["ffn1_w_down"], x, w["ln1_g"], w["ln1_b"], res_scale=DEEPNORM_ALPHA, mm_scale=0.5,
        bm=512, bk=down_bk, name="ffn1_down_ln", comm=c))
    sv1 = (g1, u1, h1, xh1, rs1)
    pw = w["w_in"].shape[1]
    proj = _carried(w.gather(("ffn2_w_down", "w_ple_gate", "w_ple_proj")), lambda c: _mm(
        x1b, w["w_in"], bm=1024, bn=_pick(pw, (512, 256, 128)), bk=d, name="in_proj", comm=c))
    nqk = (N_PATTERNS + 1) * N_KV_HEADS
    qkr = _rotary(proj, tabs, n_cols=nqk, inverse=False, out_dtype=F32, bs=1024, name="rotary")
    attn, lse = _attn_fwd(qkr, proj, name="attn_fwd")
    lru_w = (w["conv_w"], w["conv_b"], w["w_rgate"], w["b_rgate"], w["w_igate"], w["b_igate"], w["lru_lambda"])
    rec, hseq = _lru_fwd(proj, *lru_w, tt=512, name="lru_fwd")
    cat = jnp.concatenate([attn.astype(BF16), rec], axis=1)
    x2, x2b, xh2, rs2 = _mm_ln(cat, w["w_out"], x1, w["ln2_g"], w["ln2_b"], res_scale=DEEPNORM_ALPHA, mm_scale=1.0,
                               bm=512, bk=1024, name="out_proj_ln")
    g2, u2, h2 = _ffn_up(x2b, w["ffn2_w_gate"], w["ffn2_w_up"], bm=1024, bn=ffn_bn, name="ffn2_up")
    x3, x3b, xh3, rs3 = _mm_ln(h2, w["ffn2_w_down"], x2, w["ln3_g"], w["ln3_b"], res_scale=DEEPNORM_ALPHA, mm_scale=0.5,
                               bm=512, bk=down_bk, name="ffn2_down_ln")
    sv3 = (g2, u2, h2, xh3, rs3)
    lsum, dy, dgate, dple = _ple_loss(x3, x3b, p, w["w_ple_gate"], w["w_ple_proj"], target,
                                      bm=1024, bn=_pick(d, (512, 256, 128)), name="ple_loss")
    grads = w.grads
    grads["w_ple_gate"] = _dw(x3b, dgate, name="dw_ple_gate")
    grads["w_ple_proj"] = _dw(p, dple, name="dw_ple_proj")
    dz3, dz3b, grads["ln3_g"], grads["ln3_b"] = _carried(w.scatter(("w_ple_gate", "w_ple_proj")), lambda c: _mm_dx(
        [(dgate, w.t("w_ple_gate"))], dy, extra_scale=1.0, ln=(xh3, rs3, w["ln3_g"]), bm=512, bk=1024, name="ple_dx", comm=c))
    dz2, dz2b, grads["ln2_g"], grads["ln2_b"] = _ffn_bwd(
        w, ("ffn2_w_gate", "ffn2_w_up", "ffn2_w_down"), sv3, x2b, dz3, dz3b, (xh2, rs2, w["ln2_g"]), "ffn2")
    grads["w_out"] = _dw(cat, dz2b, name="dw_out")
    dcat = _carried(w.scatter(("w_out",)), lambda c: _mm(
        dz2b, w.t("w_out"), bm=1024, bn=1024, bk=d, name="out_proj_dx", comm=c))
    dq0, dq1, dq2, dk, dvb = _attn_bwd(qkr, proj, attn, lse, dcat, name="attn_bwd")
    nh = N_KV_HEADS
    dqkv = [_rotary(t, tabs, n_cols=nh, inverse=True, out_dtype=BF16, bs=1024, name=f"rotary_bwd{i}")
            for i, t in enumerate((dq0, dq1, dq2, dk))]
    (dxb, dyb, grads["conv_w"], grads["conv_b"], grads["w_rgate"], grads["b_rgate"], grads["w_igate"],
     grads["b_igate"], grads["lru_lambda"]) = _lru_bwd(proj, hseq, dcat, *lru_w, tt=512, name="lru_bwd")
    dproj = jnp.concatenate(dqkv + [dvb, dxb, dyb], axis=1)
    grads["w_in"] = _dw(x1b, dproj, name="dw_in")
    dz1, dz1b, grads["ln1_g"], grads["ln1_b"] = _carried(w.scatter(("w_in",)), lambda c: _mm_dx(
        [(dproj, w.t("w_in"))], dz2, extra_scale=DEEPNORM_ALPHA, ln=(xh1, rs1, w["ln1_g"]),
        bm=512, bk=_pick(pw, (1024, 512, 256, 128)), name="in_proj_dx", comm=c))
    grad_x = _ffn_bwd(w, ("ffn1_w_gate", "ffn1_w_up", "ffn1_w_down"), sv1, xb, dz1, dz1b, None, "ffn1",
                      also=w.gather_small())
    return lsum, grad_x


def _to_full(name, gathered):
    if name in COLUMN_SHARDED:
        _, r, c = gathered.shape
        return jnp.transpose(gathered, (1, 0, 2)).reshape(r, N_DEV * c)
    return gathered.reshape((N_DEV * gathered.shape[1],) + gathered.shape[2:])


def _to_owner_blocks(name, full):
    if name in COLUMN_SHARDED:
        r, c = full.shape
        return jnp.transpose(full.reshape(r, N_DEV, c // N_DEV), (1, 0, 2))
    return full.reshape((N_DEV, full.shape[0] // N_DEV) + full.shape[1:])


def _rows128(a):
    flat = a.reshape(-1, LANES)
    pad = (-flat.shape[0]) % 8
    return jnp.pad(flat, ((0, pad), (0, 0))) if pad else flat


def kernel(x, p, positions, ffn1_w_gate, ffn1_w_up, ffn1_w_down, ln1_g, ln1_b, w_in, conv_w, conv_b, w_rgate, b_rgate, w_igate, b_igate, lru_lambda, w_out, ln2_g, ln2_b, ffn2_w_gate, ffn2_w_up, ffn2_w_down, ln3_g, ln3_b, w_ple_proj, w_ple_gate, loss_target, m_ffn1_w_gate, m_ffn1_w_up, m_ffn1_w_down, m_ln1_g, m_ln1_b, m_w_in, m_conv_w, m_conv_b, m_w_rgate, m_b_rgate, m_w_igate, m_b_igate, m_lru_lambda, m_w_out, m_ln2_g, m_ln2_b, m_ffn2_w_gate, m_ffn2_w_up, m_ffn2_w_down, m_ln3_g, m_ln3_b, m_w_ple_proj, m_w_ple_gate, v_ffn1_w_gate, v_ffn1_w_up, v_ffn1_w_down, v_ln1_g, v_ln1_b, v_w_in, v_conv_w, v_conv_b, v_w_rgate, v_b_rgate, v_w_igate, v_b_igate, v_lru_lambda, v_w_out, v_ln2_g, v_ln2_b, v_ffn2_w_gate, v_ffn2_w_up, v_ffn2_w_down, v_ln3_g, v_ln3_b, v_w_ple_proj, v_w_ple_gate):
    names = ("ffn1_w_gate", "ffn1_w_up", "ffn1_w_down", "ln1_g", "ln1_b", "w_in", "conv_w", "conv_b", "w_rgate",
             "b_rgate", "w_igate", "b_igate", "lru_lambda", "w_out", "ln2_g", "ln2_b", "ffn2_w_gate", "ffn2_w_up",
             "ffn2_w_down", "ln3_g", "ln3_b", "w_ple_proj", "w_ple_gate")
    ws = (ffn1_w_gate, ffn1_w_up, ffn1_w_down, ln1_g, ln1_b, w_in, conv_w, conv_b, w_rgate, b_rgate, w_igate, b_igate,
          lru_lambda, w_out, ln2_g, ln2_b, ffn2_w_gate, ffn2_w_up, ffn2_w_down, ln3_g, ln3_b, w_ple_proj, w_ple_gate)
    ms = (m_ffn1_w_gate, m_ffn1_w_up, m_ffn1_w_down, m_ln1_g, m_ln1_b, m_w_in, m_conv_w, m_conv_b, m_w_rgate, m_b_rgate,
          m_w_igate, m_b_igate, m_lru_lambda, m_w_out, m_ln2_g, m_ln2_b, m_ffn2_w_gate, m_ffn2_w_up, m_ffn2_w_down,
          m_ln3_g, m_ln3_b, m_w_ple_proj, m_w_ple_gate)
    vs = (v_ffn1_w_gate, v_ffn1_w_up, v_ffn1_w_down, v_ln1_g, v_ln1_b, v_w_in, v_conv_w, v_conv_b, v_w_rgate, v_b_rgate,
          v_w_igate, v_b_igate, v_lru_lambda, v_w_out, v_ln2_g, v_ln2_b, v_ffn2_w_gate, v_ffn2_w_up, v_ffn2_w_down,
          v_ln3_g, v_ln3_b, v_w_ple_proj, v_w_ple_gate)
    def local(a):
        return a[0] if a.ndim >= 3 else a

    w_loc = {n: local(a) for n, a in zip(names, ws)}
    m_loc = {n: local(a) for n, a in zip(names, ms)}
    v_loc = {n: local(a) for n, a in zip(names, vs)}
    out_shapes = {n: a.shape for n, a in zip(names, ws)}

    shards = {n: (w_loc[n].T if n in COLUMN_SHARDED else w_loc[n]).astype(BF16) for n in BIG_WEIGHTS}
    gate1, up1, conv_all = _all_gather([shards["ffn1_w_gate"], shards["ffn1_w_up"], w_loc["conv_w"]], name="gather_first")
    ex = _MeshExchange({n: w_loc[n] for n in SMALL_WEIGHTS}, shards)
    ex.full["conv_w"] = _to_full("conv_w", conv_all)
    ex.take("ffn1_w_gate", gate1)
    ex.take("ffn1_w_up", up1)

    lsum, grad_x = _local_step(x[0], p[0, 0], loss_target[0], positions[0], ex)
    grads, reduced = ex.grads, ex.reduced
    d_model = x.shape[-1]
    loss = lax.psum(lsum[0, 0] * (0.5 / d_model), ("x", "y", "c"))

    small = SMALL_GRADS
    summed = _sum_parts(ex.small_parts, name="sum_small_grads")
    small_grads, row = {}, 0
    for n in small:
        rows = grads[n].size // LANES
        small_grads[n] = summed[row:row + rows].reshape(grads[n].shape)
        row += rows + (-rows) % 8
    me = 4 * lax.axis_index("x") + 2 * lax.axis_index("y") + lax.axis_index("c")
    cw_cols = w_loc["conv_w"].shape[1]
    small_grads["conv_w"] = lax.dynamic_slice_in_dim(small_grads["conv_w"], me * cw_cols, cw_cols, axis=1)

    out_g, out_d, out_m, out_v = {}, {}, {}, {}
    for n in names:
        wl, ml, vl = w_loc[n], m_loc[n], v_loc[n]
        shape = wl.shape
        if n in BIG_WEIGHTS:
            gparts = reduced[n]
        else:
            gparts = small_grads[n].reshape((1,) + shape)
        if wl.ndim == 3:
            wl, ml, vl = (t.reshape(-1, shape[-1]) for t in (wl, ml, vl))
            gparts = gparts.reshape(gparts.shape[0], -1, shape[-1])
        res = _adamw(wl, ml, vl, gparts, name=f"adamw_{n}")
        out_g[n], out_d[n], out_m[n], out_v[n] = (t.reshape(out_shapes[n]) for t in res)

    return (loss, grad_x[None], *[out_g[n] for n in names], *[out_d[n] for n in names],
            *[out_m[n] for n in names], *[out_v[n] for n in names])
```

```python
import jax
import jax.numpy as jnp
from jax import lax
from jax.experimental import pallas as pl
from jax.experimental.pallas import tpu as pltpu

F32 = jnp.float32
BF16 = jnp.bfloat16

N_DEV = 8
LANES = 128
MIB = 1 << 20

HEAD_DIM = 128
N_KV_HEADS = 4
DILATIONS = (1, 4, 16)
N_PATTERNS = 3
SPAN = 128
ROT_DIMS = 32
ROPE_THETA = 500000.0
LRU_C = 8.0
CONV_WIDTH = 4
LN_EPS = 1e-5
DEEPNORM_ALPHA = 2.0 ** 0.25
ATTN_TILE = SPAN * DILATIONS[-1]

ADAM_LR = 0.001
ADAM_B1 = 0.9
ADAM_B2 = 0.999
ADAM_EPS = 1e-08
ADAM_WD = 0.01
ADAM_STEP = 10

MESH = pl.DeviceIdType.MESH
EPILOGUE_ROWS = 64


def _cp(semantics, vmem_mib):
    return pltpu.CompilerParams(dimension_semantics=semantics, vmem_limit_bytes=vmem_mib * MIB)


def _pick(n, candidates):
    for c in candidates:
        if n % c == 0:
            return c
    return n


class _Comm:
    def __init__(self, arrays, out_shapes, scratch, start, end, mid=None):
        self.arrays, self.out_shapes, self.scratch = list(arrays), list(out_shapes), list(scratch)
        self.start, self.mid, self.end = start, mid, end


def _call(body, *, name, grid, in_specs, out_specs, out_shape, args, scratch_shapes=(), vmem_mib, comm=None):
    single = not isinstance(out_shape, (tuple, list))
    out_shape_t = (out_shape,) if single else tuple(out_shape)
    out_specs_t = (out_specs,) if single else tuple(out_specs)
    params = _cp(("arbitrary",) * len(grid), vmem_mib)
    if comm is None:
        res = pl.pallas_call(body, name=name, grid=grid, in_specs=list(in_specs), out_specs=out_specs_t,
                             out_shape=out_shape_t, scratch_shapes=list(scratch_shapes), compiler_params=params)(*args)
        return res[0] if single else res
    n_in, n_out, n_scr = len(args), len(out_shape_t), len(scratch_shapes)
    nci, nco = len(comm.arrays), len(comm.out_shapes)
    total = 1
    for g in grid:
        total *= g

    def wrapped(*refs):
        ins, refs = refs[:n_in], refs[n_in:]
        cin, refs = refs[:nci], refs[nci:]
        outs, refs = refs[:n_out], refs[n_out:]
        cout, refs = refs[:nco], refs[nco:]
        scr, csem = refs[:n_scr], refs[n_scr:]
        step = pl.program_id(0)
        for ax in range(1, len(grid)):
            step = step * grid[ax] + pl.program_id(ax)

        @pl.when(step == 0)
        def _():
            comm.start(cin, cout, csem)

        body(*ins, *outs, *scr)
        if comm.mid is not None:
            @pl.when(step == (3 * total) // 4)
            def _():
                comm.mid(cin, cout, csem)

        @pl.when(step == total - 1)
        def _():
            comm.end(cin, cout, csem)

    hbm = pl.BlockSpec(memory_space=pltpu.HBM)
    res = pl.pallas_call(
        wrapped, name=name, grid=grid,
        in_specs=list(in_specs) + [hbm] * nci,
        out_specs=out_specs_t + (hbm,) * nco,
        out_shape=out_shape_t + tuple(comm.out_shapes),
        scratch_shapes=list(scratch_shapes) + comm.scratch,
        compiler_params=params)(*args, *comm.arrays)
    own, extra = res[:n_out], res[n_out:]
    return (own[0] if single else own), extra


def _mm(a, b, *, ta=False, tb=False, out_dtype=F32, scale=1.0, bm, bn, bk, name, comm=None):
    m, k = (a.shape[1], a.shape[0]) if ta else a.shape
    n = b.shape[0] if tb else b.shape[1]
    bm, bn, bk = min(bm, m), min(bn, n), min(bk, k)
    assert m % bm == 0 and n % bn == 0 and k % bk == 0, (name, m, n, k, bm, bn, bk)
    nk = k // bk
    a_spec = pl.BlockSpec((bk, bm), lambda i, j, kk: (kk, i)) if ta else pl.BlockSpec((bm, bk), lambda i, j, kk: (i, kk))
    b_spec = pl.BlockSpec((bn, bk), lambda i, j, kk: (j, kk)) if tb else pl.BlockSpec((bk, bn), lambda i, j, kk: (kk, j))
    dn = (((0 if ta else 1,), (1 if tb else 0,)), ((), ()))

    def body(a_ref, b_ref, o_ref, *acc):
        part = lax.dot_general(a_ref[...].astype(BF16), b_ref[...].astype(BF16), dn, preferred_element_type=F32)
        if nk == 1:
            o_ref[...] = (part * scale).astype(out_dtype)
            return
        acc_ref, = acc
        kk = pl.program_id(2)

        @pl.when(kk == 0)
        def _():
            acc_ref[...] = part

        @pl.when(kk > 0)
        def _():
            acc_ref[...] += part

        @pl.when(kk == nk - 1)
        def _():
            o_ref[...] = (acc_ref[...] * scale).astype(out_dtype)

    return _call(
        body, name=name,
        out_shape=jax.ShapeDtypeStruct((m, n), out_dtype),
        grid=(m // bm, n // bn, nk),
        in_specs=[a_spec, b_spec],
        out_specs=pl.BlockSpec((bm, bn), lambda i, j, kk: (i, j)),
        scratch_shapes=[pltpu.VMEM((bm, bn), F32)] if nk > 1 else [],
        args=(a, b), vmem_mib=56, comm=comm)


def _ffn_up(xb, wg, wu, *, bm, bn, name, comm=None):
    s, d = xb.shape
    f = wg.shape[1]
    bm, bn = min(bm, s), min(bn, f)
    assert s % bm == 0 and f % bn == 0

    def body(x_ref, wg_ref, wu_ref, hg_ref, hu_ref, h_ref):
        x = x_ref[...]
        g = jnp.dot(x, wg_ref[...], preferred_element_type=F32)
        u = jnp.dot(x, wu_ref[...], preferred_element_type=F32)
        sig = jax.nn.sigmoid(g)
        silu = g * sig
        hg_ref[...] = (u * (sig * (1.0 + g * (1.0 - sig)))).astype(BF16)
        hu_ref[...] = silu.astype(BF16)
        h_ref[...] = (silu * u).astype(BF16)

    out = jax.ShapeDtypeStruct((s, f), BF16)
    blk = pl.BlockSpec((bm, bn), lambda i, j: (i, j))
    return _call(
        body, name=name, out_shape=(out, out, out),
        grid=(s // bm, f // bn),
        in_specs=[pl.BlockSpec((bm, d), lambda i, j: (i, 0)),
                  pl.BlockSpec((d, bn), lambda i, j: (0, j)),
                  pl.BlockSpec((d, bn), lambda i, j: (0, j))],
        out_specs=(blk, blk, blk),
        args=(xb, wg, wu), vmem_mib=56, comm=comm)


def _ffn_bwd_dh(dzb, wdt, g, u, *, scale, bm, bn, name, chunks=2, comm=None):
    s, d = dzb.shape
    f = wdt.shape[1]
    bm, bn = min(bm, s), min(bn, f)
    assert s % bm == 0 and f % bn == 0

    cr = bm // chunks

    def body(dz_ref, wd_ref, hg_ref, hu_ref, dg_ref, du_ref):
        for r in range(chunks):
            rows = slice(r * cr, (r + 1) * cr)
            dh = jnp.dot(dz_ref[rows, :], wd_ref[...], preferred_element_type=F32) * scale
            dg_ref[rows, :] = (dh * hg_ref[rows, :].astype(F32)).astype(BF16)
            du_ref[rows, :] = (dh * hu_ref[rows, :].astype(F32)).astype(BF16)

    out = jax.ShapeDtypeStruct((s, f), BF16)
    blk = pl.BlockSpec((bm, bn), lambda i, j: (i, j))
    return _call(
        body, name=name, out_shape=(out, out),
        grid=(s // bm, f // bn),
        in_specs=[pl.BlockSpec((bm, d), lambda i, j: (i, 0)),
                  pl.BlockSpec((d, bn), lambda i, j: (0, j)), blk, blk],
        out_specs=(blk, blk),
        args=(dzb, wdt, g, u), vmem_mib=56, comm=comm)


def _full_rows(acc_ref, rows, nj):
    return jnp.concatenate([acc_ref[jj, rows, :] for jj in range(nj)], axis=1)


def _mm_ln(a, b, res, gamma, beta, *, res_scale, mm_scale, bm, bn, name, comm=None):
    s, k = a.shape
    d = b.shape[1]
    bm, bn = min(bm, s), min(bn, d)
    assert s % bm == 0 and d % bn == 0
    nj = d // bn
    ch = min(EPILOGUE_ROWS, bm)

    def body(a_ref, b_ref, r_ref, g_ref, be_ref, y_ref, yb_ref, xh_ref, rs_ref, acc_ref):
        j = pl.program_id(1)
        acc_ref[j] = jnp.dot(a_ref[...], b_ref[...], preferred_element_type=F32)

        @pl.when(j == nj - 1)
        def _():
            def chunk(ci, carry):
                rows = pl.ds(pl.multiple_of(ci * ch, ch), ch)
                z = res_scale * r_ref[rows, :] + mm_scale * _full_rows(acc_ref, rows, nj)
                mu = jnp.mean(z, axis=-1, keepdims=True)
                zc = z - mu
                var = jnp.mean(zc * zc, axis=-1, keepdims=True)
                rstd = lax.rsqrt(var + LN_EPS)
                xh = zc * rstd
                y = xh * g_ref[...] + be_ref[...]
                y_ref[rows, :] = y
                yb_ref[rows, :] = y.astype(BF16)
                xh_ref[rows, :] = xh
                rs_ref[rows, :] = rstd
                return carry

            lax.fori_loop(0, bm // ch, chunk, 0)

    row = pl.BlockSpec((bm, d), lambda i, j: (i, 0))
    vec = pl.BlockSpec((1, d), lambda i, j: (0, 0))
    return _call(
        body, name=name,
        out_shape=(jax.ShapeDtypeStruct((s, d), F32), jax.ShapeDtypeStruct((s, d), BF16),
                   jax.ShapeDtypeStruct((s, d), F32), jax.ShapeDtypeStruct((s, 1), F32)),
        grid=(s // bm, nj),
        in_specs=[pl.BlockSpec((bm, k), lambda i, j: (i, 0)),
                  pl.BlockSpec((k, bn), lambda i, j: (0, j)), row, vec, vec],
        out_specs=(row, row, row, pl.BlockSpec((bm, 1), lambda i, j: (i, 0))),
        scratch_shapes=[pltpu.VMEM((nj, bm, bn), F32)],
        args=(a, b, res, gamma, beta), vmem_mib=58, comm=comm)


def _mm_dx(a, wt, extra, xhat, rstd, gamma, *, extra_scale, bm, bn, name, comm=None):
    s, k = a.shape
    d = wt.shape[1]
    bm, bn = min(bm, s), min(bn, d)
    assert s % bm == 0 and d % bn == 0
    nj = d // bn
    ch = min(EPILOGUE_ROWS, bm)

    def body(a_ref, w_ref, e_ref, xh_ref, rs_ref, g_ref, dz_ref, dzb_ref, dg_ref, db_ref, acc_ref):
        i = pl.program_id(0)
        j = pl.program_id(1)
        acc_ref[j] = jnp.dot(a_ref[...], w_ref[...], preferred_element_type=F32)

        @pl.when(j == nj - 1)
        def _():
            def chunk(ci, carry):
                dgp, dbp = carry
                rows = pl.ds(pl.multiple_of(ci * ch, ch), ch)
                dx = extra_scale * e_ref[rows, :] + _full_rows(acc_ref, rows, nj)
                xh = xh_ref[rows, :]
                dxh = dx * g_ref[...]
                m1 = jnp.mean(dxh, axis=-1, keepdims=True)
                m2 = jnp.mean(dxh * xh, axis=-1, keepdims=True)
                dz = rs_ref[rows, :] * (dxh - m1 - xh * m2)
                dz_ref[rows, :] = dz
                dzb_ref[rows, :] = dz.astype(BF16)
                return dgp + jnp.sum(dx * xh, axis=0, keepdims=True), dbp + jnp.sum(dx, axis=0, keepdims=True)

            zero = jnp.zeros((1, d), F32)
            dgp, dbp = lax.fori_loop(0, bm // ch, chunk, (zero, zero))

            @pl.when(i == 0)
            def _():
                dg_ref[...] = dgp
                db_ref[...] = dbp

            @pl.when(i > 0)
            def _():
                dg_ref[...] += dgp
                db_ref[...] += dbp

    row = pl.BlockSpec((bm, d), lambda i, j: (i, 0))
    vec = pl.BlockSpec((1, d), lambda i, j: (0, 0))
    return _call(
        body, name=name,
        out_shape=(jax.ShapeDtypeStruct((s, d), F32), jax.ShapeDtypeStruct((s, d), BF16),
                   jax.ShapeDtypeStruct((1, d), F32), jax.ShapeDtypeStruct((1, d), F32)),
        grid=(s // bm, nj),
        in_specs=[pl.BlockSpec((bm, k), lambda i, j: (i, 0)), pl.BlockSpec((k, bn), lambda i, j: (0, j)),
                  row, row, pl.BlockSpec((bm, 1), lambda i, j: (i, 0)), vec],
        out_specs=(row, row, vec, vec),
        scratch_shapes=[pltpu.VMEM((nj, bm, bn), F32)],
        args=(a, wt, extra, xhat, rstd, gamma), vmem_mib=58, comm=comm)


def _ffn_dx(dg, du, wgt, wut, extra, *, extra_scale, bm, bn, name, comm=None):
    s, f = dg.shape
    d = wgt.shape[1]
    bm, bn = min(bm, s), min(bn, d)
    assert s % bm == 0 and d % bn == 0

    def body(dg_ref, du_ref, wg_ref, wu_ref, e_ref, o_ref):
        acc = jnp.dot(dg_ref[...], wg_ref[...], preferred_element_type=F32)
        acc = acc + jnp.dot(du_ref[...], wu_ref[...], preferred_element_type=F32)
        o_ref[...] = extra_scale * e_ref[...] + acc

    rows = pl.BlockSpec((bm, f), lambda i, j: (i, 0))
    cols = pl.BlockSpec((f, bn), lambda i, j: (0, j))
    blk = pl.BlockSpec((bm, bn), lambda i, j: (i, j))
    return _call(
        body, name=name, out_shape=jax.ShapeDtypeStruct((s, d), F32),
        grid=(s // bm, d // bn), in_specs=[rows, rows, cols, cols, blk], out_specs=blk,
        args=(dg, du, wgt, wut, extra), vmem_mib=58, comm=comm)


def _ln_bwd(dx, xhat, rstd, gamma, *, bm, name):
    s, d = dx.shape
    bm = min(bm, s)
    assert s % bm == 0
    ch = min(EPILOGUE_ROWS, bm)

    def body(dx_ref, xh_ref, rs_ref, g_ref, dz_ref, dzb_ref, dg_ref, db_ref):
        def chunk(ci, carry):
            dgp, dbp = carry
            rows = pl.ds(pl.multiple_of(ci * ch, ch), ch)
            dxv = dx_ref[rows, :]
            xh = xh_ref[rows, :]
            dxh = dxv * g_ref[...]
            m1 = jnp.mean(dxh, axis=-1, keepdims=True)
            m2 = jnp.mean(dxh * xh, axis=-1, keepdims=True)
            dz = rs_ref[rows, :] * (dxh - m1 - xh * m2)
            dz_ref[rows, :] = dz
            dzb_ref[rows, :] = dz.astype(BF16)
            return dgp + jnp.sum(dxv * xh, axis=0, keepdims=True), dbp + jnp.sum(dxv, axis=0, keepdims=True)

        zero = jnp.zeros((1, d), F32)
        dgp, dbp = lax.fori_loop(0, bm // ch, chunk, (zero, zero))
        i = pl.program_id(0)

        @pl.when(i == 0)
        def _():
            dg_ref[...] = dgp
            db_ref[...] = dbp

        @pl.when(i > 0)
        def _():
            dg_ref[...] += dgp
            db_ref[...] += dbp

    row = pl.BlockSpec((bm, d), lambda i: (i, 0))
    vec = pl.BlockSpec((1, d), lambda i: (0, 0))
    return _call(
        body, name=name,
        out_shape=(jax.ShapeDtypeStruct((s, d), F32), jax.ShapeDtypeStruct((s, d), BF16),
                   jax.ShapeDtypeStruct((1, d), F32), jax.ShapeDtypeStruct((1, d), F32)),
        grid=(s // bm,), in_specs=[row, row, pl.BlockSpec((bm, 1), lambda i: (i, 0)), vec],
        out_specs=(row, row, vec, vec), args=(dx, xhat, rstd, gamma), vmem_mib=48)


def _ple_loss(x3, x3b, p, wpg, wpp, target, *, bm, bn, name):
    s, d = x3.shape
    dp = p.shape[1]
    bm, bn = min(bm, s), min(bn, d)
    assert s % bm == 0 and d % bn == 0
    inv_d = 1.0 / d

    def body(x_ref, xb_ref, p_ref, wg_ref, wp_ref, t_ref, l_ref, dy_ref, dg_ref, dp_ref):
        first = (pl.program_id(0) == 0) & (pl.program_id(1) == 0)
        gp = jnp.dot(xb_ref[...], wg_ref[...], preferred_element_type=F32)
        pp = jnp.dot(p_ref[...].astype(BF16), wp_ref[...], preferred_element_type=F32)
        sig = jax.nn.sigmoid(gp)
        err = x_ref[...] + sig * pp - t_ref[...]
        part = jnp.sum(err * err)

        @pl.when(first)
        def _():
            l_ref[...] = jnp.zeros_like(l_ref)

        l_ref[...] += part
        dy = err * inv_d
        dy_ref[...] = dy
        dg_ref[...] = (dy * pp * sig * (1.0 - sig)).astype(BF16)
        dp_ref[...] = (dy * sig).astype(BF16)

    blk = pl.BlockSpec((bm, bn), lambda i, j: (i, j))
    return pl.pallas_call(
        body, name=name,
        out_shape=(jax.ShapeDtypeStruct((8, LANES), F32), jax.ShapeDtypeStruct((s, d), F32),
                   jax.ShapeDtypeStruct((s, d), BF16), jax.ShapeDtypeStruct((s, d), BF16)),
        grid=(s // bm, d // bn),
        in_specs=[blk, pl.BlockSpec((bm, d), lambda i, j: (i, 0)), pl.BlockSpec((bm, dp), lambda i, j: (i, 0)),
                  pl.BlockSpec((d, bn), lambda i, j: (0, j)), pl.BlockSpec((dp, bn), lambda i, j: (0, j)), blk],
        out_specs=(pl.BlockSpec((8, LANES), lambda i, j: (0, 0)), blk, blk, blk),
        compiler_params=_cp(("arbitrary", "arbitrary"), 56),
    )(x3, x3b, p, wpg, wpp, target)


def _rope_tables(positions):
    half = ROT_DIMS // 2
    inv_freq = jnp.power(jnp.float32(ROPE_THETA), -jnp.arange(half, dtype=F32) * (2.0 / ROT_DIMS))
    ang = positions.astype(F32)[:, None] * inv_freq
    cos, sin = jnp.cos(ang), jnp.sin(ang)
    s = positions.shape[0]
    zeros = jnp.zeros((s, half), F32)
    rest0 = jnp.zeros((s, HEAD_DIM - ROT_DIMS), F32)
    cf = jnp.concatenate([cos, cos, jnp.ones((s, HEAD_DIM - ROT_DIMS), F32)], axis=1)
    sa = jnp.concatenate([-sin, zeros, rest0], axis=1)
    sb = jnp.concatenate([zeros, sin, rest0], axis=1)
    return cf, sa, sb


def _rotary(t, tabs, *, n_cols, inverse, out_dtype, bs, name):
    s = t.shape[0]
    bs = min(bs, s)
    half = ROT_DIMS // 2

    def body(t_ref, cf_ref, sa_ref, sb_ref, o_ref):
        v = t_ref[...]
        if inverse:
            o = (v * cf_ref[...] + pltpu.roll(v * sa_ref[...], half, 1)
                 + pltpu.roll(v * sb_ref[...], HEAD_DIM - half, 1))
        else:
            o = (v * cf_ref[...] + pltpu.roll(v, HEAD_DIM - half, 1) * sa_ref[...]
                 + pltpu.roll(v, half, 1) * sb_ref[...])
        o_ref[...] = o.astype(out_dtype)

    blk = pl.BlockSpec((bs, HEAD_DIM), lambda i, j: (i, j))
    tab = pl.BlockSpec((bs, HEAD_DIM), lambda i, j: (i, 0))
    return pl.pallas_call(
        body, name=name, out_shape=jax.ShapeDtypeStruct((s, n_cols * HEAD_DIM), out_dtype),
        grid=(s // bs, n_cols), in_specs=[blk, tab, tab, tab], out_specs=blk,
        compiler_params=_cp(("parallel", "arbitrary"), 32),
    )(t, *tabs)


def _attn_blocks():
    out = []
    for g, dil in enumerate(DILATIONS):
        sup = SPAN * dil
        for j in range(ATTN_TILE // sup):
            for r in range(dil):
                out.append((g, j * sup + r, dil, (j - 1) * sup + r if j > 0 else None, ATTN_TILE - sup + r))
    return out


def _rows(ref, start, dil, lead=None):
    idx = pl.ds(start, SPAN, stride=dil) if dil > 1 else pl.ds(start, SPAN)
    return ref[idx, :] if lead is None else ref[lead, idx, :]


def _band_masks(n):
    qi = lax.broadcasted_iota(jnp.int32, (SPAN, 2 * SPAN), 0)
    ki = lax.broadcasted_iota(jnp.int32, (SPAN, 2 * SPAN), 1)
    band = (ki >= qi) & (ki <= qi + SPAN)
    return band, band & ((ki >= SPAN) | (n > 0))


def _attn_fwd(qkr, proj, *, name):
    s = qkr.shape[0]
    t = ATTN_TILE
    assert s % t == 0
    nt = s // t
    scale = HEAD_DIM ** -0.5
    kcol, vcol = N_PATTERNS * N_KV_HEADS, (N_PATTERNS + 1) * N_KV_HEADS
    blocks = _attn_blocks()

    def body(q0, q1, q2, kc_ref, kp_ref, vc_ref, vp_ref, o_ref, l_ref, og, lg):
        n = pl.program_id(1)
        band, band_first = _band_masks(n)
        q_refs = (q0, q1, q2)
        for g, start, dil, prev_in_tile, prev_start in blocks:
            q = _rows(q_refs[g], start, dil).astype(BF16)
            if prev_in_tile is not None:
                kp, vp, mask = _rows(kc_ref, prev_in_tile, dil), _rows(vc_ref, prev_in_tile, dil), band
            else:
                kp, vp, mask = _rows(kp_ref, prev_start, dil), _rows(vp_ref, prev_start, dil), band_first
            kk = jnp.concatenate([kp, _rows(kc_ref, start, dil)], axis=0).astype(BF16)
            vv = jnp.concatenate([vp, _rows(vc_ref, start, dil)], axis=0).astype(BF16)
            sc = lax.dot_general(q, kk, (((1,), (1,)), ((), ())), preferred_element_type=F32) * scale
            sc = jnp.where(mask, sc, -1e30)
            m = jnp.max(sc, axis=-1, keepdims=True)
            e = jnp.exp(sc - m)
            den = jnp.sum(e, axis=-1, keepdims=True)
            o = jnp.dot(e.astype(BF16), vv, preferred_element_type=F32) / den
            idx = pl.ds(start, SPAN, stride=dil) if dil > 1 else pl.ds(start, SPAN)
            og[g, idx, :] = o
            lg[g, idx, :] = jnp.broadcast_to(m + jnp.log(den), (SPAN, HEAD_DIM))
        l0, l1, l2 = lg[0], lg[1], lg[2]
        m = jnp.maximum(jnp.maximum(l0, l1), l2)
        w0, w1, w2 = jnp.exp(l0 - m), jnp.exp(l1 - m), jnp.exp(l2 - m)
        den = w0 + w1 + w2
        o_ref[...] = (w0 * og[0] + w1 * og[1] + w2 * og[2]) / den
        l_ref[...] = m + jnp.log(den)

    def col(c, prev=False):
        if prev:
            return pl.BlockSpec((t, HEAD_DIM), lambda h, n: (jnp.maximum(n - 1, 0), c + h))
        return pl.BlockSpec((t, HEAD_DIM), lambda h, n: (n, c + h))

    out = jax.ShapeDtypeStruct((s, N_KV_HEADS * HEAD_DIM), F32)
    return pl.pallas_call(
        body, name=name, out_shape=(out, out),
        grid=(N_KV_HEADS, nt),
        in_specs=[col(0), col(N_KV_HEADS), col(2 * N_KV_HEADS), col(kcol), col(kcol, True), col(vcol), col(vcol, True)],
        out_specs=(col(0), col(0)),
        scratch_shapes=[pltpu.VMEM((N_PATTERNS, t, HEAD_DIM), F32), pltpu.VMEM((N_PATTERNS, t, HEAD_DIM), F32)],
        compiler_params=_cp(("parallel", "arbitrary"), 48),
    )(qkr, qkr, qkr, qkr, qkr, proj, proj)


def _attn_bwd(qkr, proj, attn, lse, dcat, *, name, comm=None):
    s = qkr.shape[0]
    t = ATTN_TILE
    nt = s // t
    scale = HEAD_DIM ** -0.5
    kcol, vcol = N_PATTERNS * N_KV_HEADS, (N_PATTERNS + 1) * N_KV_HEADS
    blocks = _attn_blocks()

    def body(q0, q1, q2, kc_ref, kp_ref, vc_ref, vp_ref, o_ref, l_ref, do_ref,
             dq0, dq1, dq2, dk_ref, dv_ref, ck, cv, tkc, tvc, tkp, tvp):
        n = pl.program_id(1)
        for ref in (tkc, tvc, tkp, tvp):
            ref[...] = jnp.zeros_like(ref)

        @pl.when(n < nt)
        def _():
            band, band_first = _band_masks(n)
            q_refs, dq_refs = (q0, q1, q2), (dq0, dq1, dq2)
            for g, start, dil, prev_in_tile, prev_start in blocks:
                idx = pl.ds(start, SPAN, stride=dil) if dil > 1 else pl.ds(start, SPAN)
                q = q_refs[g][idx, :].astype(BF16)
                if prev_in_tile is not None:
                    kp, vp, mask = _rows(kc_ref, prev_in_tile, dil), _rows(vc_ref, prev_in_tile, dil), band
                else:
                    kp, vp, mask = _rows(kp_ref, prev_start, dil), _rows(vp_ref, prev_start, dil), band_first
                kk = jnp.concatenate([kp, kc_ref[idx, :]], axis=0).astype(BF16)
                vv = jnp.concatenate([vp, vc_ref[idx, :]], axis=0).astype(BF16)
                do = do_ref[idx, :]
                dsum = jnp.sum(do * o_ref[idx, :], axis=-1, keepdims=True)
                lrow = l_ref[idx, :][:, :1]
                dob = do.astype(BF16)
                sc = lax.dot_general(q, kk, (((1,), (1,)), ((), ())), preferred_element_type=F32) * scale
                p = jnp.where(mask, jnp.exp(sc - lrow), 0.0)
                dp = lax.dot_general(dob, vv, (((1,), (1,)), ((), ())), preferred_element_type=F32)
                ds = (p * (dp - dsum) * scale).astype(BF16)
                pb = p.astype(BF16)
                dq_refs[g][idx, :] = jnp.dot(ds, kk, preferred_element_type=F32)
                dkk = lax.dot_general(ds, q, (((0,), (0,)), ((), ())), preferred_element_type=F32)
                dvv = lax.dot_general(pb, dob, (((0,), (0,)), ((), ())), preferred_element_type=F32)
                tkc[idx, :] += dkk[SPAN:]
                tvc[idx, :] += dvv[SPAN:]
                if prev_in_tile is not None:
                    pidx = pl.ds(prev_in_tile, SPAN, stride=dil) if dil > 1 else pl.ds(prev_in_tile, SPAN)
                    tkc[pidx, :] += dkk[:SPAN]
                    tvc[pidx, :] += dvv[:SPAN]
                else:
                    pidx = pl.ds(prev_start, SPAN, stride=dil) if dil > 1 else pl.ds(prev_start, SPAN)
                    tkp[pidx, :] += dkk[:SPAN]
                    tvp[pidx, :] += dvv[:SPAN]

        @pl.when(n > 0)
        def _():
            dk_ref[...] = ck[...] + tkp[...]
            dv_ref[...] = (cv[...] + tvp[...]).astype(BF16)

        ck[...] = tkc[...]
        cv[...] = tvc[...]

    def col(c, prev=False):
        if prev:
            return pl.BlockSpec((t, HEAD_DIM), lambda h, n: (jnp.maximum(jnp.minimum(n, nt - 1) - 1, 0), c + h))
        return pl.BlockSpec((t, HEAD_DIM), lambda h, n: (jnp.minimum(n, nt - 1), c + h))

    kv_out = pl.BlockSpec((t, HEAD_DIM), lambda h, n: (jnp.maximum(n - 1, 0), h))
    tile = pltpu.VMEM((t, HEAD_DIM), F32)
    per_head = jax.ShapeDtypeStruct((s, N_KV_HEADS * HEAD_DIM), F32)
    return _call(
        body, name=name,
        out_shape=(per_head, per_head, per_head, per_head, jax.ShapeDtypeStruct((s, N_KV_HEADS * HEAD_DIM), BF16)),
        grid=(N_KV_HEADS, nt + 1),
        in_specs=[col(0), col(N_KV_HEADS), col(2 * N_KV_HEADS), col(kcol), col(kcol, True), col(vcol), col(vcol, True),
                  col(0), col(0), col(0)],
        out_specs=(col(0), col(0), col(0), kv_out, kv_out),
        scratch_shapes=[tile] * 6,
        args=(qkr, qkr, qkr, qkr, qkr, proj, proj, attn, lse, dcat), vmem_mib=48, comm=comm)


GELU_C0 = 0.7978845608028654
GELU_C1 = 0.044715


def _softplus_neg(lam):
    y = jnp.exp(-jnp.abs(lam))
    w = 1.0 + y
    log1p = jnp.where(w == 1.0, y, jnp.log(w) * (y / jnp.where(w == 1.0, 1.0, w - 1.0)))
    return jnp.maximum(-lam, 0.0) + log1p


def _down(cur, prev, k, row):
    if k == 0:
        return cur
    return jnp.where(row < k, pltpu.roll(prev, k, 0), pltpu.roll(cur, k, 0))


def _up(cur, nxt, k, row, tt):
    if k == 0:
        return cur
    return jnp.where(row >= tt - k, pltpu.roll(nxt, tt - k, 0), pltpu.roll(cur, tt - k, 0))


def _lru_gates(x, xp, cw, cb, wr, br, wi, bi, lam, row):
    shifts = [_down(x, xp, k, row) for k in range(CONV_WIDTH)]
    xc = cb
    for j in range(CONV_WIDTH):
        xc = xc + cw[j:j + 1, :] * shifts[CONV_WIDTH - 1 - j]
    xcb = xc.astype(BF16)
    r = jax.nn.sigmoid(jnp.dot(xcb, wr, preferred_element_type=F32) + br)
    i = jax.nn.sigmoid(jnp.dot(xcb, wi, preferred_element_type=F32) + bi)
    c = -LRU_C * _softplus_neg(lam)
    la = c * r
    a = jnp.exp(la)
    mult = jnp.sqrt(jnp.tanh(-la) * (a * a + 1.0))
    return shifts, xc, xcb, r, i, c, a, mult


def _lru_fwd(proj, cw, cb, wr, br, wi, bi, lam, *, tt, name):
    s = proj.shape[0]
    nblk = wr.shape[0]
    c = nblk * LANES
    tt = min(tt, s)
    xcol0 = (N_PATTERNS + 2) * N_KV_HEADS
    ycol0 = xcol0 + nblk

    def body(x_ref, y_ref, cw_ref, cb_ref, wr_ref, br_ref, wi_ref, bi_ref, lam_ref, rec_ref, h_ref, xprev, hc):
        n = pl.program_id(1)

        @pl.when(n == 0)
        def _():
            xprev[...] = jnp.zeros_like(xprev)
            hc[...] = jnp.zeros_like(hc)

        row = lax.broadcasted_iota(jnp.int32, (tt, LANES), 0)
        x = x_ref[...]
        _, xc, _, _, i, _, a, mult = _lru_gates(
            x, xprev[...], cw_ref[...], cb_ref[...], wr_ref[0].astype(BF16), br_ref[...],
            wi_ref[0].astype(BF16), bi_ref[...], lam_ref[...], row)
        av, bv = a, mult * (i * xc)
        k = 1
        while k < tt:
            bs = jnp.where(row < k, 0.0, pltpu.roll(bv, k, 0))
            as_ = jnp.where(row < k, 1.0, pltpu.roll(av, k, 0))
            bv = bv + av * bs
            av = av * as_
            k *= 2
        h = bv + av * hc[0:1, :]
        hc[...] = jnp.broadcast_to(h[tt - 1:tt, :], hc.shape)
        h_ref[...] = h
        y = y_ref[...]
        gel = 0.5 * y * (1.0 + jnp.tanh(GELU_C0 * (y + GELU_C1 * y * y * y)))
        rec_ref[...] = (h * gel).astype(BF16)
        xprev[...] = x

    vec = pl.BlockSpec((1, LANES), lambda b, n: (0, b))
    wblk = pl.BlockSpec((1, LANES, LANES), lambda b, n: (b, 0, 0))
    out = pl.BlockSpec((tt, LANES), lambda b, n: (n, b))
    return pl.pallas_call(
        body, name=name,
        out_shape=(jax.ShapeDtypeStruct((s, c), BF16), jax.ShapeDtypeStruct((s, c), F32)),
        grid=(nblk, s // tt),
        in_specs=[pl.BlockSpec((tt, LANES), lambda b, n: (n, xcol0 + b)),
                  pl.BlockSpec((tt, LANES), lambda b, n: (n, ycol0 + b)),
                  pl.BlockSpec((CONV_WIDTH, LANES), lambda b, n: (0, b)), vec, wblk, vec, wblk, vec, vec],
        out_specs=(out, out),
        scratch_shapes=[pltpu.VMEM((tt, LANES), F32), pltpu.VMEM((8, LANES), F32)],
        compiler_params=_cp(("parallel", "arbitrary"), 32),
    )(proj, proj, cw, cb, wr, br, wi, bi, lam)


def _lru_bwd(proj, hseq, dcat, cw, cb, wr, br, wi, bi, lam, *, tt, name, comm=None):
    s = proj.shape[0]
    nblk = wr.shape[0]
    c = nblk * LANES
    tt = min(tt, s)
    nt = s // tt
    xcol0 = (N_PATTERNS + 2) * N_KV_HEADS
    ycol0 = xcol0 + nblk
    rcol0 = N_KV_HEADS

    def body(x_ref, xp_ref, y_ref, h_ref, hp_ref, dr_ref, cw_ref, cb_ref, wr_ref, br_ref, wi_ref, bi_ref, lam_ref,
             dx_ref, dy_ref, dcw_ref, dcb_ref, dwr_ref, dbr_ref, dwi_ref, dbi_ref, dlam_ref, dxc_next, gcar, acar):
        n = pl.program_id(1)
        rt = nt - 1 - n

        @pl.when(n == 0)
        def _():
            for ref in (dxc_next, gcar, acar, dcw_ref, dcb_ref, dwr_ref, dbr_ref, dwi_ref, dbi_ref, dlam_ref):
                ref[...] = jnp.zeros_like(ref)

        row = lax.broadcasted_iota(jnp.int32, (tt, LANES), 0)
        x = x_ref[...]
        xp = jnp.where(rt > 0, xp_ref[...], 0.0)
        cwv = cw_ref[...]
        wrb, wib = wr_ref[0].astype(BF16), wi_ref[0].astype(BF16)
        lam_v = lam_ref[...]
        shifts, xc, xcb, r, i, cc, a, mult = _lru_gates(x, xp, cwv, cb_ref[...], wrb, br_ref[...], wib, bi_ref[...],
                                                        lam_v, row)
        h = h_ref[...]
        hp_last = jnp.where(rt > 0, hp_ref[7:8, :], 0.0)
        hprev = jnp.where(row < 1, hp_last, pltpu.roll(h, 1, 0))
        y = y_ref[...]
        y2 = y * y
        th = jnp.tanh(GELU_C0 * (y + GELU_C1 * y2 * y))
        gel = 0.5 * y * (1.0 + th)
        dgel = 0.5 * (1.0 + th) + 0.5 * y * (1.0 - th * th) * GELU_C0 * (1.0 + 3.0 * GELU_C1 * y2)
        drec = dr_ref[...]
        dy_ref[...] = (drec * h * dgel).astype(BF16)
        av = jnp.where(row >= tt - 1, acar[0:1, :], pltpu.roll(a, tt - 1, 0))
        bv = drec * gel
        k = 1
        while k < tt:
            bs = jnp.where(row >= tt - k, 0.0, pltpu.roll(bv, tt - k, 0))
            as_ = jnp.where(row >= tt - k, 1.0, pltpu.roll(av, tt - k, 0))
            bv = bv + av * bs
            av = av * as_
            k *= 2
        g = bv + av * gcar[0:1, :]
        gcar[...] = jnp.broadcast_to(g[0:1, :], gcar.shape)
        acar[...] = jnp.broadcast_to(a[0:1, :], acar.shape)
        da = g * hprev
        d_ixc = g * mult
        dmult = g * (i * xc)
        di = d_ixc * xc
        dxc = d_ixc * i
        a2 = a * a
        dla = da * a - dmult * (a2 / mult)
        dr = dla * cc
        dsp = jnp.sum(dla * r, axis=0, keepdims=True) * (-LRU_C)
        dlam_ref[...] += dsp * (-jax.nn.sigmoid(-lam_v))
        dzr = dr * r * (1.0 - r)
        dzi = di * i * (1.0 - i)
        dbr_ref[...] += jnp.sum(dzr, axis=0, keepdims=True)
        dbi_ref[...] += jnp.sum(dzi, axis=0, keepdims=True)
        dzrb, dzib = dzr.astype(BF16), dzi.astype(BF16)
        tn = (((0,), (0,)), ((), ()))
        ntd = (((1,), (1,)), ((), ()))
        dwr_ref[0] += lax.dot_general(xcb, dzrb, tn, preferred_element_type=F32)
        dwi_ref[0] += lax.dot_general(xcb, dzib, tn, preferred_element_type=F32)
        dxc = (dxc + lax.dot_general(dzrb, wrb, ntd, preferred_element_type=F32)
               + lax.dot_general(dzib, wib, ntd, preferred_element_type=F32))
        dcb_ref[...] += jnp.sum(dxc, axis=0, keepdims=True)
        dcw_ref[...] += jnp.concatenate(
            [jnp.sum(dxc * shifts[CONV_WIDTH - 1 - j], axis=0, keepdims=True) for j in range(CONV_WIDTH)], axis=0)
        nxt = dxc_next[...]
        dx = cwv[0:1, :] * _up(dxc, nxt, CONV_WIDTH - 1, row, tt)
        for j in range(1, CONV_WIDTH):
            dx = dx + cwv[j:j + 1, :] * _up(dxc, nxt, CONV_WIDTH - 1 - j, row, tt)
        dx_ref[...] = dx.astype(BF16)
        dxc_next[...] = dxc

    def tile(col0, prev=False):
        if prev:
            return pl.BlockSpec((tt, LANES), lambda b, n: (jnp.maximum(nt - 2 - n, 0), col0 + b))
        return pl.BlockSpec((tt, LANES), lambda b, n: (nt - 1 - n, col0 + b))

    vec = pl.BlockSpec((1, LANES), lambda b, n: (0, b))
    wblk = pl.BlockSpec((1, LANES, LANES), lambda b, n: (b, 0, 0))
    cwblk = pl.BlockSpec((CONV_WIDTH, LANES), lambda b, n: (0, b))
    hp8 = pl.BlockSpec((8, LANES), lambda b, n: (jnp.maximum((nt - 1 - n) * (tt // 8) - 1, 0), b))
    vshape = jax.ShapeDtypeStruct((1, c), F32)
    wshape = jax.ShapeDtypeStruct((nblk, LANES, LANES), F32)
    return _call(
        body, name=name,
        out_shape=(jax.ShapeDtypeStruct((s, c), BF16), jax.ShapeDtypeStruct((s, c), BF16),
                   jax.ShapeDtypeStruct((CONV_WIDTH, c), F32), vshape, wshape, vshape, wshape, vshape, vshape),
        grid=(nblk, nt),
        in_specs=[tile(xcol0), tile(xcol0, True), tile(ycol0), tile(0), hp8, tile(rcol0),
                  cwblk, vec, wblk, vec, wblk, vec, vec],
        out_specs=(tile(0), tile(0), cwblk, vec, wblk, vec, wblk, vec, vec),
        scratch_shapes=[pltpu.VMEM((tt, LANES), F32), pltpu.VMEM((8, LANES), F32), pltpu.VMEM((8, LANES), F32)],
        args=(proj, proj, proj, hseq, hseq, dcat, cw, cb, wr, br, wi, bi, lam), vmem_mib=32, comm=comm)


ROW_BLOCKS = (256, 128, 64, 32, 16, 8)


def _adamw(w, m, v, gparts, *, name):
    r, c = w.shape
    npart = gparts.shape[0]
    br = _pick(r, ROW_BLOCKS)
    c1 = 1.0 - ADAM_B1 ** ADAM_STEP
    c2 = 1.0 - ADAM_B2 ** ADAM_STEP

    def body(w_ref, m_ref, v_ref, g_ref, go_ref, d_ref, mo_ref, vo_ref):
        g = g_ref[0].astype(F32)
        for q in range(1, npart):
            g = g + g_ref[q].astype(F32)
        mn = ADAM_B1 * m_ref[...] + (1.0 - ADAM_B1) * g
        vn = ADAM_B2 * v_ref[...] + (1.0 - ADAM_B2) * (g * g)
        go_ref[...] = g
        mo_ref[...] = mn
        vo_ref[...] = vn
        d_ref[...] = -ADAM_LR * ((mn / c1) / (jnp.sqrt(vn / c2) + ADAM_EPS) + ADAM_WD * w_ref[...])

    blk = pl.BlockSpec((br, c), lambda i: (i, 0))
    out = jax.ShapeDtypeStruct((r, c), F32)
    return pl.pallas_call(
        body, name=name, out_shape=(out, out, out, out), grid=(r // br,),
        in_specs=[blk, blk, blk, pl.BlockSpec((npart, br, c), lambda i: (0, i, 0))],
        out_specs=(blk, blk, blk, blk),
        compiler_params=_cp(("parallel",), 48),
    )(w, m, v, gparts)


def _sum_parts(parts, *, name):
    npart, r, c = parts.shape
    br = next((b for b in range(min(r, 2048) // 8 * 8, 0, -8) if r % b == 0), r)

    def body(p_ref, o_ref):
        acc = p_ref[0]
        for q in range(1, npart):
            acc = acc + p_ref[q]
        o_ref[...] = acc

    return pl.pallas_call(
        body, name=name, out_shape=jax.ShapeDtypeStruct((r, c), F32), grid=(r // br,),
        in_specs=[pl.BlockSpec((npart, br, c), lambda i: (0, i, 0))],
        out_specs=pl.BlockSpec((br, c), lambda i: (i, 0)),
        compiler_params=_cp(("parallel",), 48),
    )(parts)


HBM = pl.BlockSpec(memory_space=pltpu.HBM)


def _mesh_pos():
    return lax.axis_index("x"), lax.axis_index("y"), lax.axis_index("c")


def _all_gather(shards, *, name):
    comm = _gather_comm(shards)
    na = len(shards)

    def body(*refs):
        ins, outs, sems = refs[:na], refs[na:2 * na], refs[2 * na:]
        comm.start(ins, outs, sems)
        comm.mid(ins, outs, sems)
        comm.end(ins, outs, sems)

    return pl.pallas_call(
        body, name=name, out_shape=tuple(comm.out_shapes),
        in_specs=[HBM] * na, out_specs=tuple([HBM] * na), scratch_shapes=comm.scratch,
    )(*shards)


def _gather_comm(shards):
    na = len(shards)

    def parts(x_refs, out_refs, sems):
        send_sems, recv_sems, local_sems = sems
        x, y, c = _mesh_pos()
        me, sibling = (x, y, c), (x, y, 1 - c)
        chips = [(1 - x, y), (x, 1 - y), (1 - x, 1 - y)]

        def copy(a, k, block, to, src=None):
            px, py, pc = block
            dst = out_refs[a].at[4 * px + 2 * py + pc]
            return pltpu.make_async_remote_copy(
                src_ref=dst if src is None else src, dst_ref=dst,
                send_sem=send_sems.at[a, k], recv_sem=recv_sems.at[a, k],
                device_id=to, device_id_type=MESH)

        def mine(a):
            return pltpu.make_async_copy(x_refs[a], out_refs[a].at[4 * x + 2 * y + c], local_sems.at[a])

        def first(a):
            return [copy(a, 0, me, sibling, src=x_refs[a])] + [
                copy(a, 1 + j, me, (*chip, c), src=x_refs[a]) for j, chip in enumerate(chips)]

        def passed(a, j):
            return copy(a, 4 + j, (*chips[j], c), sibling)

        return me, sibling, chips, c, copy, mine, first, passed

    def start(x_refs, out_refs, sems):
        *_, mine, first, _ = parts(x_refs, out_refs, sems)
        for a in range(na):
            mine(a).start()
            for cp in first(a):
                cp.start()

    def mid(x_refs, out_refs, sems):
        me, _, chips, c, copy, _, _, passed = parts(x_refs, out_refs, sems)
        for j, chip in enumerate(chips):
            for a in range(na):
                copy(a, 1 + j, (*chip, c), me).wait_recv()
                passed(a, j).start()

    def end(x_refs, out_refs, sems):
        me, sibling, chips, c, copy, mine, first, passed = parts(x_refs, out_refs, sems)
        for a in range(na):
            copy(a, 0, sibling, me).wait_recv()
            for j, chip in enumerate(chips):
                copy(a, 4 + j, (*chip, 1 - c), me).wait_recv()
        for a in range(na):
            for cp in first(a) + [passed(a, j) for j in range(3)]:
                cp.wait_send()
            mine(a).wait()

    return _Comm(
        shards, [jax.ShapeDtypeStruct((N_DEV,) + a.shape, a.dtype) for a in shards],
        [pltpu.SemaphoreType.DMA((na, 7)), pltpu.SemaphoreType.DMA((na, 7)), pltpu.SemaphoreType.DMA((na,))],
        start, end, mid)


def _scatter_comm(g8s):
    na = len(g8s)

    def parts(g_refs, buf_refs, sems):
        send_sems, recv_sems, local_sems = sems
        x, y, c = _mesh_pos()
        me_idx = 4 * x + 2 * y + c

        def peer(k):
            return (1 - x if k & 4 else x, 1 - y if k & 2 else y, 1 - c if k & 1 else c)

        def copy(a, k, slot):
            px, py, pc = peer(k)
            return pltpu.make_async_remote_copy(
                src_ref=g_refs[a].at[4 * px + 2 * py + pc], dst_ref=buf_refs[a].at[slot],
                send_sem=send_sems.at[a, k - 1], recv_sem=recv_sems.at[a, k - 1],
                device_id=(px, py, pc), device_id_type=MESH)

        def mine(a):
            return pltpu.make_async_copy(g_refs[a].at[me_idx], buf_refs[a].at[me_idx], local_sems.at[a])

        return me_idx, peer, copy, mine

    def start(g_refs, buf_refs, sems):
        me_idx, _, copy, mine = parts(g_refs, buf_refs, sems)
        for a in range(na):
            mine(a).start()
            for k in range(1, N_DEV):
                copy(a, k, me_idx).start()

    def end(g_refs, buf_refs, sems):
        me_idx, peer, copy, mine = parts(g_refs, buf_refs, sems)
        for a in range(na):
            for k in range(1, N_DEV):
                px, py, pc = peer(k)
                copy(a, k, 4 * px + 2 * py + pc).wait_recv()
        for a in range(na):
            for k in range(1, N_DEV):
                copy(a, k, me_idx).wait_send()
            mine(a).wait()

    return _Comm(
        g8s, [jax.ShapeDtypeStruct(g.shape, g.dtype) for g in g8s],
        [pltpu.SemaphoreType.DMA((na, N_DEV - 1)), pltpu.SemaphoreType.DMA((na, N_DEV - 1)),
         pltpu.SemaphoreType.DMA((na,))],
        start, end)


def _join_comm(c1, c2):
    n1i, n1o, n1s = len(c1.arrays), len(c1.out_shapes), len(c1.scratch)

    def both(f1, f2):
        def run(ins, outs, sems):
            if f1 is not None:
                f1(ins[:n1i], outs[:n1o], sems[:n1s])
            if f2 is not None:
                f2(ins[n1i:], outs[n1o:], sems[n1s:])
        return run

    mid = both(c1.mid, c2.mid) if (c1.mid is not None or c2.mid is not None) else None
    return _Comm(c1.arrays + c2.arrays, c1.out_shapes + c2.out_shapes, c1.scratch + c2.scratch,
                 both(c1.start, c2.start), both(c1.end, c2.end), mid)


BIG_WEIGHTS = ("ffn1_w_gate", "ffn1_w_up", "ffn1_w_down", "w_in", "w_out",
               "ffn2_w_gate", "ffn2_w_up", "ffn2_w_down", "w_ple_proj", "w_ple_gate")
COLUMN_SHARDED = ("ffn1_w_gate", "ffn1_w_up", "w_in", "ffn2_w_gate", "ffn2_w_up", "w_ple_proj", "conv_w")
SMALL_WEIGHTS = ("ln1_g", "ln1_b", "conv_b", "w_rgate", "b_rgate", "w_igate", "b_igate", "lru_lambda",
                 "ln2_g", "ln2_b", "ln3_g", "ln3_b")
SMALL_GRADS = SMALL_WEIGHTS + ("conv_w",)


class _Exchange:
    def __init__(self, full):
        self.full = dict(full)
        self.transposed = {}
        self.grads = {}

    def __getitem__(self, name):
        return self.full[name]

    def t(self, name):
        if name not in self.transposed:
            self.transposed[name] = self.full[name].T
        return self.transposed[name]

    def gather(self, names):
        return None, None

    def scatter(self, names):
        return None, None

    def gather_small(self):
        return None, None


class _MeshExchange(_Exchange):
    def __init__(self, full, shards):
        super().__init__(full)
        self.shards = shards
        self.reduced = {}
        self.small_parts = None

    def gather(self, names):
        def done(outs):
            for n, o in zip(names, outs):
                self.take(n, o)
        return _gather_comm([self.shards[n] for n in names]), done

    def take(self, name, gathered):
        stacked = gathered.reshape((N_DEV * gathered.shape[1],) + gathered.shape[2:])
        if name in COLUMN_SHARDED:
            self.transposed[name] = stacked
            self.full[name] = stacked.T
        else:
            self.full[name] = stacked

    def scatter(self, names):
        def done(outs):
            self.reduced.update(zip(names, outs))
        return _scatter_comm([_to_owner_blocks(n, self.grads[n]) for n in names]), done

    def gather_small(self):
        def done(outs):
            self.small_parts, = outs
        packed = jnp.concatenate([_rows128(self.grads[n]) for n in SMALL_GRADS], axis=0)
        return _gather_comm([packed]), done


def _carried(comm_done, call):
    comm, done = comm_done
    res = call(comm)
    if comm is None:
        return res
    res, outs = res
    done(outs)
    return res


def _dw(a, b, *, scale=1.0, name, comm=None):
    k, m = a.shape
    n = b.shape[1]
    return _mm(a, b, ta=True, scale=scale, out_dtype=BF16, bm=_pick(m, (1024, 512, 256, 128)),
               bn=_pick(n, (512, 256, 128)), bk=k, name=name, comm=comm)


def _ffn_bwd(ex, names, saved, xb_in, dz, dzb, ln_in, tag, send_on_dx, also=None):
    gate, up, down = names
    g, u, h, _, _ = saved
    f = ex[gate].shape[1]
    ex.grads[down] = _dw(h, dzb, scale=0.5, name=f"{tag}_dwd")
    dg, du = _carried(ex.scatter((down,)), lambda c: _ffn_bwd_dh(
        dzb, ex.t(down), g, u, scale=0.5, bm=1024, bn=_pick(f, (512, 256, 128)), name=f"{tag}_dh", chunks=4, comm=c))
    ex.grads[gate] = _dw(xb_in, dg, name=f"{tag}_dwg")
    ex.grads[up] = _dw(xb_in, du, name=f"{tag}_dwu")
    last = ex.scatter(send_on_dx)
    if also is not None and also[0] is not None:
        done_up, done_also = last[1], also[1]
        n_up = len(last[0].out_shapes)
        last = (_join_comm(last[0], also[0]), lambda outs: (done_up(outs[:n_up]), done_also(outs[n_up:])))
    d = dz.shape[1]
    dx = _carried(last, lambda c: _ffn_dx(
        dg, du, ex.t(gate), ex.t(up), dz, extra_scale=DEEPNORM_ALPHA,
        bm=512, bn=_pick(d, (512, 256, 128)), name=f"{tag}_dx", comm=c))
    return dx if ln_in is None else _ln_bwd(dx, *ln_in, bm=256, name=f"{tag}_ln_bwd")


def _local_step(x, p, target, positions, w):
    s, d = x.shape
    tabs = _rope_tables(positions)
    xb = x.astype(BF16)
    f = w["ffn1_w_gate"].shape[1]
    ffn_bn, ln_bn = _pick(f, (512, 256, 128)), _pick(d, (256, 128))
    g1, u1, h1 = _carried(w.gather(("ffn1_w_down", "w_in", "w_out")), lambda c: _ffn_up(
        xb, w["ffn1_w_gate"], w["ffn1_w_up"], bm=1024, bn=ffn_bn, name="ffn1_up", comm=c))
    x1, x1b, xh1, rs1 = _carried(w.gather(("ffn2_w_gate", "ffn2_w_up")), lambda c: _mm_ln(
        h1, w["ffn1_w_down"], x, w["ln1_g"], w["ln1_b"], res_scale=DEEPNORM_ALPHA, mm_scale=0.5,
        bm=512, bn=ln_bn, name="ffn1_down_ln", comm=c))
    sv1 = (g1, u1, h1, xh1, rs1)
    pw = w["w_in"].shape[1]
    proj = _carried(w.gather(("ffn2_w_down", "w_ple_gate", "w_ple_proj")), lambda c: _mm(
        x1b, w["w_in"], bm=1024, bn=_pick(pw, (512, 256, 128)), bk=d, name="in_proj", comm=c))
    nqk = (N_PATTERNS + 1) * N_KV_HEADS
    qkr = _rotary(proj, tabs, n_cols=nqk, inverse=False, out_dtype=F32, bs=1024, name="rotary")
    attn, lse = _attn_fwd(qkr, proj, name="attn_fwd")
    lru_w = (w["conv_w"], w["conv_b"], w["w_rgate"], w["b_rgate"], w["w_igate"], w["b_igate"], w["lru_lambda"])
    rec, hseq = _lru_fwd(proj, *lru_w, tt=512, name="lru_fwd")
    cat = jnp.concatenate([attn.astype(BF16), rec], axis=1)
    x2, x2b, xh2, rs2 = _mm_ln(cat, w["w_out"], x1, w["ln2_g"], w["ln2_b"], res_scale=DEEPNORM_ALPHA, mm_scale=1.0,
                               bm=512, bn=ln_bn, name="out_proj_ln")
    g2, u2, h2 = _ffn_up(x2b, w["ffn2_w_gate"], w["ffn2_w_up"], bm=1024, bn=ffn_bn, name="ffn2_up")
    x3, x3b, xh3, rs3 = _mm_ln(h2, w["ffn2_w_down"], x2, w["ln3_g"], w["ln3_b"], res_scale=DEEPNORM_ALPHA, mm_scale=0.5,
                               bm=512, bn=ln_bn, name="ffn2_down_ln")
    sv3 = (g2, u2, h2, xh3, rs3)
    lsum, dy, dgate, dple = _ple_loss(x3, x3b, p, w["w_ple_gate"], w["w_ple_proj"], target,
                                      bm=1024, bn=_pick(d, (512, 256, 128)), name="ple_loss")
    grads = w.grads
    grads["w_ple_gate"] = _dw(x3b, dgate, name="dw_ple_gate")
    grads["w_ple_proj"] = _dw(p, dple, name="dw_ple_proj")
    dz3, dz3b, grads["ln3_g"], grads["ln3_b"] = _carried(w.scatter(("w_ple_gate", "w_ple_proj")), lambda c: _mm_dx(
        dgate, w.t("w_ple_gate"), dy, xh3, rs3, w["ln3_g"], extra_scale=1.0, bm=512, bn=ln_bn, name="ple_dx", comm=c))
    dz2, dz2b, grads["ln2_g"], grads["ln2_b"] = _ffn_bwd(
        w, ("ffn2_w_gate", "ffn2_w_up", "ffn2_w_down"), sv3, x2b, dz3, dz3b, (xh2, rs2, w["ln2_g"]), "ffn2",
        send_on_dx=("ffn2_w_gate",))
    grads["w_out"] = _dw(cat, dz2b, name="dw_out")
    dcat = _mm(dz2b, w.t("w_out"), bm=1024, bn=_pick(d, (512, 256, 128)), bk=d, name="out_proj_dx")
    dq0, dq1, dq2, dk, dvb = _carried(w.scatter(("ffn2_w_up",)), lambda c: _attn_bwd(
        qkr, proj, attn, lse, dcat, name="attn_bwd", comm=c))
    nh = N_KV_HEADS
    dqkv = [_rotary(t, tabs, n_cols=nh, inverse=True, out_dtype=BF16, bs=1024, name=f"rotary_bwd{i}")
            for i, t in enumerate((dq0, dq1, dq2, dk))]
    (dxb, dyb, grads["conv_w"], grads["conv_b"], grads["w_rgate"], grads["b_rgate"], grads["w_igate"],
     grads["b_igate"], grads["lru_lambda"]) = _carried(w.scatter(("w_out",)), lambda c: _lru_bwd(
         proj, hseq, dcat, *lru_w, tt=512, name="lru_bwd", comm=c))
    dproj = jnp.concatenate(dqkv + [dvb, dxb, dyb], axis=1)
    grads["w_in"] = _dw(x1b, dproj, name="dw_in")
    dz1, dz1b, grads["ln1_g"], grads["ln1_b"] = _carried(w.scatter(("w_in",)), lambda c: _mm_dx(
        dproj, w.t("w_in"), dz2, xh1, rs1, w["ln1_g"], extra_scale=DEEPNORM_ALPHA,
        bm=512, bn=ln_bn, name="in_proj_dx", comm=c))
    grad_x = _ffn_bwd(w, ("ffn1_w_gate", "ffn1_w_up", "ffn1_w_down"), sv1, xb, dz1, dz1b, None, "ffn1",
                      send_on_dx=("ffn1_w_gate", "ffn1_w_up"), also=w.gather_small())
    return lsum, grad_x


def _to_full(name, gathered):
    if name in COLUMN_SHARDED:
        _, r, c = gathered.shape
        return jnp.transpose(gathered, (1, 0, 2)).reshape(r, N_DEV * c)
    return gathered.reshape((N_DEV * gathered.shape[1],) + gathered.shape[2:])


def _to_owner_blocks(name, full):
    if name in COLUMN_SHARDED:
        r, c = full.shape
        return jnp.transpose(full.reshape(r, N_DEV, c // N_DEV), (1, 0, 2))
    return full.reshape((N_DEV, full.shape[0] // N_DEV) + full.shape[1:])


def _rows128(a):
    flat = a.reshape(-1, LANES)
    pad = (-flat.shape[0]) % 8
    return jnp.pad(flat, ((0, pad), (0, 0))) if pad else flat


def kernel(x, p, positions, ffn1_w_gate, ffn1_w_up, ffn1_w_down, ln1_g, ln1_b, w_in, conv_w, conv_b, w_rgate, b_rgate, w_igate, b_igate, lru_lambda, w_out, ln2_g, ln2_b, ffn2_w_gate, ffn2_w_up, ffn2_w_down, ln3_g, ln3_b, w_ple_proj, w_ple_gate, loss_target, m_ffn1_w_gate, m_ffn1_w_up, m_ffn1_w_down, m_ln1_g, m_ln1_b, m_w_in, m_conv_w, m_conv_b, m_w_rgate, m_b_rgate, m_w_igate, m_b_igate, m_lru_lambda, m_w_out, m_ln2_g, m_ln2_b, m_ffn2_w_gate, m_ffn2_w_up, m_ffn2_w_down, m_ln3_g, m_ln3_b, m_w_ple_proj, m_w_ple_gate, v_ffn1_w_gate, v_ffn1_w_up, v_ffn1_w_down, v_ln1_g, v_ln1_b, v_w_in, v_conv_w, v_conv_b, v_w_rgate, v_b_rgate, v_w_igate, v_b_igate, v_lru_lambda, v_w_out, v_ln2_g, v_ln2_b, v_ffn2_w_gate, v_ffn2_w_up, v_ffn2_w_down, v_ln3_g, v_ln3_b, v_w_ple_proj, v_w_ple_gate):
    names = ("ffn1_w_gate", "ffn1_w_up", "ffn1_w_down", "ln1_g", "ln1_b", "w_in", "conv_w", "conv_b", "w_rgate",
             "b_rgate", "w_igate", "b_igate", "lru_lambda", "w_out", "ln2_g", "ln2_b", "ffn2_w_gate", "ffn2_w_up",
             "ffn2_w_down", "ln3_g", "ln3_b", "w_ple_proj", "w_ple_gate")
    ws = (ffn1_w_gate, ffn1_w_up, ffn1_w_down, ln1_g, ln1_b, w_in, conv_w, conv_b, w_rgate, b_rgate, w_igate, b_igate,
          lru_lambda, w_out, ln2_g, ln2_b, ffn2_w_gate, ffn2_w_up, ffn2_w_down, ln3_g, ln3_b, w_ple_proj, w_ple_gate)
    ms = (m_ffn1_w_gate, m_ffn1_w_up, m_ffn1_w_down, m_ln1_g, m_ln1_b, m_w_in, m_conv_w, m_conv_b, m_w_rgate, m_b_rgate,
          m_w_igate, m_b_igate, m_lru_lambda, m_w_out, m_ln2_g, m_ln2_b, m_ffn2_w_gate, m_ffn2_w_up, m_ffn2_w_down,
          m_ln3_g, m_ln3_b, m_w_ple_proj, m_w_ple_gate)
    vs = (v_ffn1_w_gate, v_ffn1_w_up, v_ffn1_w_down, v_ln1_g, v_ln1_b, v_w_in, v_conv_w, v_conv_b, v_w_rgate, v_b_rgate,
          v_w_igate, v_b_igate, v_lru_lambda, v_w_out, v_ln2_g, v_ln2_b, v_ffn2_w_gate, v_ffn2_w_up, v_ffn2_w_down,
          v_ln3_g, v_ln3_b, v_w_ple_proj, v_w_ple_gate)
    def local(a):
        return a[0] if a.ndim >= 3 else a

    w_loc = {n: local(a) for n, a in zip(names, ws)}
    m_loc = {n: local(a) for n, a in zip(names, ms)}
    v_loc = {n: local(a) for n, a in zip(names, vs)}
    out_shapes = {n: a.shape for n, a in zip(names, ws)}

    shards = {n: (w_loc[n].T if n in COLUMN_SHARDED else w_loc[n]).astype(BF16) for n in BIG_WEIGHTS}
    gate1, up1, conv_all = _all_gather([shards["ffn1_w_gate"], shards["ffn1_w_up"], w_loc["conv_w"]], name="gather_first")
    ex = _MeshExchange({n: w_loc[n] for n in SMALL_WEIGHTS}, shards)
    ex.full["conv_w"] = _to_full("conv_w", conv_all)
    ex.take("ffn1_w_gate", gate1)
    ex.take("ffn1_w_up", up1)

    lsum, grad_x = _local_step(x[0], p[0, 0], loss_target[0], positions[0], ex)
    grads, reduced = ex.grads, ex.reduced
    d_model = x.shape[-1]
    loss = lax.psum(lsum[0, 0] * (0.5 / d_model), ("x", "y", "c"))

    small = SMALL_GRADS
    summed = _sum_parts(ex.small_parts, name="sum_small_grads")
    small_grads, row = {}, 0
    for n in small:
        rows = grads[n].size // LANES
        small_grads[n] = summed[row:row + rows].reshape(grads[n].shape)
        row += rows + (-rows) % 8
    me = 4 * lax.axis_index("x") + 2 * lax.axis_index("y") + lax.axis_index("c")
    cw_cols = w_loc["conv_w"].shape[1]
    small_grads["conv_w"] = lax.dynamic_slice_in_dim(small_grads["conv_w"], me * cw_cols, cw_cols, axis=1)

    out_g, out_d, out_m, out_v = {}, {}, {}, {}
    for n in names:
        wl, ml, vl = w_loc[n], m_loc[n], v_loc[n]
        shape = wl.shape
        if n in BIG_WEIGHTS:
            gparts = reduced[n]
        else:
            gparts = small_grads[n].reshape((1,) + shape)
        if wl.ndim == 3:
            wl, ml, vl = (t.reshape(-1, shape[-1]) for t in (wl, ml, vl))
            gparts = gparts.reshape(gparts.shape[0], -1, shape[-1])
        res = _adamw(wl, ml, vl, gparts, name=f"adamw_{n}")
        out_g[n], out_d[n], out_m[n], out_v[n] = (t.reshape(out_shapes[n]) for t in res)

    return (loss, grad_x[None], *[out_g[n] for n in names], *[out_d[n] for n in names],
            *[out_m[n] for n in names], *[out_v[n] for n in names])
```

```python
import jax
import jax.numpy as jnp
from jax import lax
from jax.experimental import pallas as pl
from jax.experimental.pallas import tpu as pltpu

F32 = jnp.float32
BF16 = jnp.bfloat16

N_DEV = 8
LANES = 128
MIB = 1 << 20

HEAD_DIM = 128
N_KV_HEADS = 4
DILATIONS = (1, 4, 16)
N_PATTERNS = 3
SPAN = 128
ROT_DIMS = 32
ROPE_THETA = 500000.0
LRU_C = 8.0
CONV_WIDTH = 4
LN_EPS = 1e-5
DEEPNORM_ALPHA = 2.0 ** 0.25
ATTN_TILE = SPAN * DILATIONS[-1]

ADAM_LR = 0.001
ADAM_B1 = 0.9
ADAM_B2 = 0.999
ADAM_EPS = 1e-08
ADAM_WD = 0.01
ADAM_STEP = 10

MESH = pl.DeviceIdType.MESH
EPILOGUE_ROWS = 64


def _cp(semantics, vmem_mib):
    return pltpu.CompilerParams(dimension_semantics=semantics, vmem_limit_bytes=vmem_mib * MIB)


def _pick(n, candidates):
    for c in candidates:
        if n % c == 0:
            return c
    return n


class _Comm:
    def __init__(self, arrays, out_shapes, scratch, start, end, mid=None):
        self.arrays, self.out_shapes, self.scratch = list(arrays), list(out_shapes), list(scratch)
        self.start, self.mid, self.end = start, mid, end


def _call(body, *, name, grid, in_specs, out_specs, out_shape, args, scratch_shapes=(), vmem_mib, comm=None):
    single = not isinstance(out_shape, (tuple, list))
    out_shape_t = (out_shape,) if single else tuple(out_shape)
    out_specs_t = (out_specs,) if single else tuple(out_specs)
    params = _cp(("arbitrary",) * len(grid), vmem_mib)
    if comm is None:
        res = pl.pallas_call(body, name=name, grid=grid, in_specs=list(in_specs), out_specs=out_specs_t,
                             out_shape=out_shape_t, scratch_shapes=list(scratch_shapes), compiler_params=params)(*args)
        return res[0] if single else res
    n_in, n_out, n_scr = len(args), len(out_shape_t), len(scratch_shapes)
    nci, nco = len(comm.arrays), len(comm.out_shapes)
    total = 1
    for g in grid:
        total *= g

    def wrapped(*refs):
        ins, refs = refs[:n_in], refs[n_in:]
        cin, refs = refs[:nci], refs[nci:]
        outs, refs = refs[:n_out], refs[n_out:]
        cout, refs = refs[:nco], refs[nco:]
        scr, csem = refs[:n_scr], refs[n_scr:]
        step = pl.program_id(0)
        for ax in range(1, len(grid)):
            step = step * grid[ax] + pl.program_id(ax)

        @pl.when(step == 0)
        def _():
            comm.start(cin, cout, csem)

        body(*ins, *outs, *scr)
        if comm.mid is not None:
            @pl.when(step == (3 * total) // 4)
            def _():
                comm.mid(cin, cout, csem)

        @pl.when(step == total - 1)
        def _():
            comm.end(cin, cout, csem)

    hbm = pl.BlockSpec(memory_space=pltpu.HBM)
    res = pl.pallas_call(
        wrapped, name=name, grid=grid,
        in_specs=list(in_specs) + [hbm] * nci,
        out_specs=out_specs_t + (hbm,) * nco,
        out_shape=out_shape_t + tuple(comm.out_shapes),
        scratch_shapes=list(scratch_shapes) + comm.scratch,
        compiler_params=params)(*args, *comm.arrays)
    own, extra = res[:n_out], res[n_out:]
    return (own[0] if single else own), extra


def _mm(a, b, *, ta=False, tb=False, out_dtype=F32, scale=1.0, bm, bn, bk, name, comm=None):
    m, k = (a.shape[1], a.shape[0]) if ta else a.shape
    n = b.shape[0] if tb else b.shape[1]
    bm, bn, bk = min(bm, m), min(bn, n), min(bk, k)
    assert m % bm == 0 and n % bn == 0 and k % bk == 0, (name, m, n, k, bm, bn, bk)
    nk = k // bk
    a_spec = pl.BlockSpec((bk, bm), lambda i, j, kk: (kk, i)) if ta else pl.BlockSpec((bm, bk), lambda i, j, kk: (i, kk))
    b_spec = pl.BlockSpec((bn, bk), lambda i, j, kk: (j, kk)) if tb else pl.BlockSpec((bk, bn), lambda i, j, kk: (kk, j))
    dn = (((0 if ta else 1,), (1 if tb else 0,)), ((), ()))

    def body(a_ref, b_ref, o_ref, *acc):
        part = lax.dot_general(a_ref[...].astype(BF16), b_ref[...].astype(BF16), dn, preferred_element_type=F32)
        if nk == 1:
            o_ref[...] = (part * scale).astype(out_dtype)
            return
        acc_ref, = acc
        kk = pl.program_id(2)

        @pl.when(kk == 0)
        def _():
            acc_ref[...] = part

        @pl.when(kk > 0)
        def _():
            acc_ref[...] += part

        @pl.when(kk == nk - 1)
        def _():
            o_ref[...] = (acc_ref[...] * scale).astype(out_dtype)

    return _call(
        body, name=name,
        out_shape=jax.ShapeDtypeStruct((m, n), out_dtype),
        grid=(m // bm, n // bn, nk),
        in_specs=[a_spec, b_spec],
        out_specs=pl.BlockSpec((bm, bn), lambda i, j, kk: (i, j)),
        scratch_shapes=[pltpu.VMEM((bm, bn), F32)] if nk > 1 else [],
        args=(a, b), vmem_mib=56, comm=comm)


def _ffn_up(xb, wg, wu, *, bm, bn, name, comm=None):
    s, d = xb.shape
    f = wg.shape[1]
    bm, bn = min(bm, s), min(bn, f)
    assert s % bm == 0 and f % bn == 0

    def body(x_ref, wg_ref, wu_ref, hg_ref, hu_ref, h_ref):
        x = x_ref[...]
        g = jnp.dot(x, wg_ref[...], preferred_element_type=F32)
        u = jnp.dot(x, wu_ref[...], preferred_element_type=F32)
        sig = jax.nn.sigmoid(g)
        silu = g * sig
        hg_ref[...] = (u * (sig * (1.0 + g * (1.0 - sig)))).astype(BF16)
        hu_ref[...] = silu.astype(BF16)
        h_ref[...] = (silu * u).astype(BF16)

    out = jax.ShapeDtypeStruct((s, f), BF16)
    blk = pl.BlockSpec((bm, bn), lambda i, j: (i, j))
    return _call(
        body, name=name, out_shape=(out, out, out),
        grid=(s // bm, f // bn),
        in_specs=[pl.BlockSpec((bm, d), lambda i, j: (i, 0)),
                  pl.BlockSpec((d, bn), lambda i, j: (0, j)),
                  pl.BlockSpec((d, bn), lambda i, j: (0, j))],
        out_specs=(blk, blk, blk),
        args=(xb, wg, wu), vmem_mib=56, comm=comm)


def _ffn_bwd_dh(dzb, wdt, g, u, *, scale, bm, bn, name, chunks=2, comm=None):
    s, d = dzb.shape
    f = wdt.shape[1]
    bm, bn = min(bm, s), min(bn, f)
    assert s % bm == 0 and f % bn == 0

    cr = bm // chunks

    def body(dz_ref, wd_ref, hg_ref, hu_ref, dg_ref, du_ref):
        for r in range(chunks):
            rows = slice(r * cr, (r + 1) * cr)
            dh = jnp.dot(dz_ref[rows, :], wd_ref[...], preferred_element_type=F32) * scale
            dg_ref[rows, :] = (dh * hg_ref[rows, :].astype(F32)).astype(BF16)
            du_ref[rows, :] = (dh * hu_ref[rows, :].astype(F32)).astype(BF16)

    out = jax.ShapeDtypeStruct((s, f), BF16)
    blk = pl.BlockSpec((bm, bn), lambda i, j: (i, j))
    return _call(
        body, name=name, out_shape=(out, out),
        grid=(s // bm, f // bn),
        in_specs=[pl.BlockSpec((bm, d), lambda i, j: (i, 0)),
                  pl.BlockSpec((d, bn), lambda i, j: (0, j)), blk, blk],
        out_specs=(blk, blk),
        args=(dzb, wdt, g, u), vmem_mib=56, comm=comm)


def _full_rows(acc_ref, rows, nj):
    return jnp.concatenate([acc_ref[jj, rows, :] for jj in range(nj)], axis=1)


def _mm_ln(a, b, res, gamma, beta, *, res_scale, mm_scale, bm, bn, name, comm=None):
    s, k = a.shape
    d = b.shape[1]
    bm, bn = min(bm, s), min(bn, d)
    assert s % bm == 0 and d % bn == 0
    nj = d // bn
    ch = min(EPILOGUE_ROWS, bm)

    def body(a_ref, b_ref, r_ref, g_ref, be_ref, y_ref, yb_ref, xh_ref, rs_ref, acc_ref):
        j = pl.program_id(1)
        acc_ref[j] = jnp.dot(a_ref[...], b_ref[...], preferred_element_type=F32)

        @pl.when(j == nj - 1)
        def _():
            def chunk(ci, carry):
                rows = pl.ds(pl.multiple_of(ci * ch, ch), ch)
                z = res_scale * r_ref[rows, :] + mm_scale * _full_rows(acc_ref, rows, nj)
                mu = jnp.mean(z, axis=-1, keepdims=True)
                zc = z - mu
                var = jnp.mean(zc * zc, axis=-1, keepdims=True)
                rstd = lax.rsqrt(var + LN_EPS)
                xh = zc * rstd
                y = xh * g_ref[...] + be_ref[...]
                y_ref[rows, :] = y
                yb_ref[rows, :] = y.astype(BF16)
                xh_ref[rows, :] = xh
                rs_ref[rows, :] = rstd
                return carry

            lax.fori_loop(0, bm // ch, chunk, 0)

    row = pl.BlockSpec((bm, d), lambda i, j: (i, 0))
    vec = pl.BlockSpec((1, d), lambda i, j: (0, 0))
    return _call(
        body, name=name,
        out_shape=(jax.ShapeDtypeStruct((s, d), F32), jax.ShapeDtypeStruct((s, d), BF16),
                   jax.ShapeDtypeStruct((s, d), F32), jax.ShapeDtypeStruct((s, 1), F32)),
        grid=(s // bm, nj),
        in_specs=[pl.BlockSpec((bm, k), lambda i, j: (i, 0)),
                  pl.BlockSpec((k, bn), lambda i, j: (0, j)), row, vec, vec],
        out_specs=(row, row, row, pl.BlockSpec((bm, 1), lambda i, j: (i, 0))),
        scratch_shapes=[pltpu.VMEM((nj, bm, bn), F32)],
        args=(a, b, res, gamma, beta), vmem_mib=58, comm=comm)


def _mm_dx(a, wt, extra, xhat, rstd, gamma, *, extra_scale, bm, bn, name, comm=None):
    s, k = a.shape
    d = wt.shape[1]
    bm, bn = min(bm, s), min(bn, d)
    assert s % bm == 0 and d % bn == 0
    nj = d // bn
    ch = min(EPILOGUE_ROWS, bm)

    def body(a_ref, w_ref, e_ref, xh_ref, rs_ref, g_ref, dz_ref, dzb_ref, dg_ref, db_ref, acc_ref):
        i = pl.program_id(0)
        j = pl.program_id(1)
        acc_ref[j] = jnp.dot(a_ref[...], w_ref[...], preferred_element_type=F32)

        @pl.when(j == nj - 1)
        def _():
            def chunk(ci, carry):
                dgp, dbp = carry
                rows = pl.ds(pl.multiple_of(ci * ch, ch), ch)
                dx = extra_scale * e_ref[rows, :] + _full_rows(acc_ref, rows, nj)
                xh = xh_ref[rows, :]
                dxh = dx * g_ref[...]
                m1 = jnp.mean(dxh, axis=-1, keepdims=True)
                m2 = jnp.mean(dxh * xh, axis=-1, keepdims=True)
                dz = rs_ref[rows, :] * (dxh - m1 - xh * m2)
                dz_ref[rows, :] = dz
                dzb_ref[rows, :] = dz.astype(BF16)
                return dgp + jnp.sum(dx * xh, axis=0, keepdims=True), dbp + jnp.sum(dx, axis=0, keepdims=True)

            zero = jnp.zeros((1, d), F32)
            dgp, dbp = lax.fori_loop(0, bm // ch, chunk, (zero, zero))

            @pl.when(i == 0)
            def _():
                dg_ref[...] = dgp
                db_ref[...] = dbp

            @pl.when(i > 0)
            def _():
                dg_ref[...] += dgp
                db_ref[...] += dbp

    row = pl.BlockSpec((bm, d), lambda i, j: (i, 0))
    vec = pl.BlockSpec((1, d), lambda i, j: (0, 0))
    return _call(
        body, name=name,
        out_shape=(jax.ShapeDtypeStruct((s, d), F32), jax.ShapeDtypeStruct((s, d), BF16),
                   jax.ShapeDtypeStruct((1, d), F32), jax.ShapeDtypeStruct((1, d), F32)),
        grid=(s // bm, nj),
        in_specs=[pl.BlockSpec((bm, k), lambda i, j: (i, 0)), pl.BlockSpec((k, bn), lambda i, j: (0, j)),
                  row, row, pl.BlockSpec((bm, 1), lambda i, j: (i, 0)), vec],
        out_specs=(row, row, vec, vec),
        scratch_shapes=[pltpu.VMEM((nj, bm, bn), F32)],
        args=(a, wt, extra, xhat, rstd, gamma), vmem_mib=58, comm=comm)


def _ffn_dx(dg, du, wgt, wut, extra, *, extra_scale, bm, bn, name, comm=None):
    s, f = dg.shape
    d = wgt.shape[1]
    bm, bn = min(bm, s), min(bn, d)
    assert s % bm == 0 and d % bn == 0

    def body(dg_ref, du_ref, wg_ref, wu_ref, e_ref, o_ref):
        acc = jnp.dot(dg_ref[...], wg_ref[...], preferred_element_type=F32)
        acc = acc + jnp.dot(du_ref[...], wu_ref[...], preferred_element_type=F32)
        o_ref[...] = extra_scale * e_ref[...] + acc

    rows = pl.BlockSpec((bm, f), lambda i, j: (i, 0))
    cols = pl.BlockSpec((f, bn), lambda i, j: (0, j))
    blk = pl.BlockSpec((bm, bn), lambda i, j: (i, j))
    return _call(
        body, name=name, out_shape=jax.ShapeDtypeStruct((s, d), F32),
        grid=(s // bm, d // bn), in_specs=[rows, rows, cols, cols, blk], out_specs=blk,
        args=(dg, du, wgt, wut, extra), vmem_mib=58, comm=comm)


def _ln_bwd(dx, xhat, rstd, gamma, *, bm, name):
    s, d = dx.shape
    bm = min(bm, s)
    assert s % bm == 0
    ch = min(EPILOGUE_ROWS, bm)

    def body(dx_ref, xh_ref, rs_ref, g_ref, dz_ref, dzb_ref, dg_ref, db_ref):
        def chunk(ci, carry):
            dgp, dbp = carry
            rows = pl.ds(pl.multiple_of(ci * ch, ch), ch)
            dxv = dx_ref[rows, :]
            xh = xh_ref[rows, :]
            dxh = dxv * g_ref[...]
            m1 = jnp.mean(dxh, axis=-1, keepdims=True)
            m2 = jnp.mean(dxh * xh, axis=-1, keepdims=True)
            dz = rs_ref[rows, :] * (dxh - m1 - xh * m2)
            dz_ref[rows, :] = dz
            dzb_ref[rows, :] = dz.astype(BF16)
            return dgp + jnp.sum(dxv * xh, axis=0, keepdims=True), dbp + jnp.sum(dxv, axis=0, keepdims=True)

        zero = jnp.zeros((1, d), F32)
        dgp, dbp = lax.fori_loop(0, bm // ch, chunk, (zero, zero))
        i = pl.program_id(0)

        @pl.when(i == 0)
        def _():
            dg_ref[...] = dgp
            db_ref[...] = dbp

        @pl.when(i > 0)
        def _():
            dg_ref[...] += dgp
            db_ref[...] += dbp

    row = pl.BlockSpec((bm, d), lambda i: (i, 0))
    vec = pl.BlockSpec((1, d), lambda i: (0, 0))
    return _call(
        body, name=name,
        out_shape=(jax.ShapeDtypeStruct((s, d), F32), jax.ShapeDtypeStruct((s, d), BF16),
                   jax.ShapeDtypeStruct((1, d), F32), jax.ShapeDtypeStruct((1, d), F32)),
        grid=(s // bm,), in_specs=[row, row, pl.BlockSpec((bm, 1), lambda i: (i, 0)), vec],
        out_specs=(row, row, vec, vec), args=(dx, xhat, rstd, gamma), vmem_mib=48)


def _ple_loss(x3, x3b, p, wpg, wpp, target, *, bm, bn, name):
    s, d = x3.shape
    dp = p.shape[1]
    bm, bn = min(bm, s), min(bn, d)
    assert s % bm == 0 and d % bn == 0
    inv_d = 1.0 / d
    chunks = 4 if bm % 64 == 0 else 1
    cr = bm // chunks

    def body(x_ref, xb_ref, p_ref, wg_ref, wp_ref, t_ref, l_ref, dy_ref, dg_ref, dp_ref):
        first = (pl.program_id(0) == 0) & (pl.program_id(1) == 0)

        @pl.when(first)
        def _():
            l_ref[...] = jnp.zeros_like(l_ref)

        part = 0.0
        for r in range(chunks):
            rows = slice(r * cr, (r + 1) * cr)
            gp = jnp.dot(xb_ref[rows, :], wg_ref[...], preferred_element_type=F32)
            pp = jnp.dot(p_ref[rows, :].astype(BF16), wp_ref[...], preferred_element_type=F32)
            sig = jax.nn.sigmoid(gp)
            err = x_ref[rows, :] + sig * pp - t_ref[rows, :]
            part = part + jnp.sum(err * err)
            dy = err * inv_d
            dy_ref[rows, :] = dy
            dg_ref[rows, :] = (dy * pp * sig * (1.0 - sig)).astype(BF16)
            dp_ref[rows, :] = (dy * sig).astype(BF16)
        l_ref[...] += part

    blk = pl.BlockSpec((bm, bn), lambda i, j: (i, j))
    return pl.pallas_call(
        body, name=name,
        out_shape=(jax.ShapeDtypeStruct((8, LANES), F32), jax.ShapeDtypeStruct((s, d), F32),
                   jax.ShapeDtypeStruct((s, d), BF16), jax.ShapeDtypeStruct((s, d), BF16)),
        grid=(s // bm, d // bn),
        in_specs=[blk, pl.BlockSpec((bm, d), lambda i, j: (i, 0)), pl.BlockSpec((bm, dp), lambda i, j: (i, 0)),
                  pl.BlockSpec((d, bn), lambda i, j: (0, j)), pl.BlockSpec((dp, bn), lambda i, j: (0, j)), blk],
        out_specs=(pl.BlockSpec((8, LANES), lambda i, j: (0, 0)), blk, blk, blk),
        compiler_params=_cp(("arbitrary", "arbitrary"), 56),
    )(x3, x3b, p, wpg, wpp, target)


def _rope_tables(positions):
    half = ROT_DIMS // 2
    inv_freq = jnp.power(jnp.float32(ROPE_THETA), -jnp.arange(half, dtype=F32) * (2.0 / ROT_DIMS))
    ang = positions.astype(F32)[:, None] * inv_freq
    cos, sin = jnp.cos(ang), jnp.sin(ang)
    s = positions.shape[0]
    zeros = jnp.zeros((s, half), F32)
    rest0 = jnp.zeros((s, HEAD_DIM - ROT_DIMS), F32)
    cf = jnp.concatenate([cos, cos, jnp.ones((s, HEAD_DIM - ROT_DIMS), F32)], axis=1)
    sa = jnp.concatenate([-sin, zeros, rest0], axis=1)
    sb = jnp.concatenate([zeros, sin, rest0], axis=1)
    return cf, sa, sb


def _rotary(t, tabs, *, n_cols, inverse, out_dtype, bs, name):
    s = t.shape[0]
    bs = min(bs, s)
    half = ROT_DIMS // 2

    def body(t_ref, cf_ref, sa_ref, sb_ref, o_ref):
        v = t_ref[...]
        if inverse:
            o = (v * cf_ref[...] + pltpu.roll(v * sa_ref[...], half, 1)
                 + pltpu.roll(v * sb_ref[...], HEAD_DIM - half, 1))
        else:
            o = (v * cf_ref[...] + pltpu.roll(v, HEAD_DIM - half, 1) * sa_ref[...]
                 + pltpu.roll(v, half, 1) * sb_ref[...])
        o_ref[...] = o.astype(out_dtype)

    blk = pl.BlockSpec((bs, HEAD_DIM), lambda i, j: (i, j))
    tab = pl.BlockSpec((bs, HEAD_DIM), lambda i, j: (i, 0))
    return pl.pallas_call(
        body, name=name, out_shape=jax.ShapeDtypeStruct((s, n_cols * HEAD_DIM), out_dtype),
        grid=(s // bs, n_cols), in_specs=[blk, tab, tab, tab], out_specs=blk,
        compiler_params=_cp(("parallel", "arbitrary"), 32),
    )(t, *tabs)


def _attn_blocks():
    out = []
    for g, dil in enumerate(DILATIONS):
        sup = SPAN * dil
        for j in range(ATTN_TILE // sup):
            for r in range(dil):
                out.append((g, j * sup + r, dil, (j - 1) * sup + r if j > 0 else None, ATTN_TILE - sup + r))
    return out


def _rows(ref, start, dil, lead=None):
    idx = pl.ds(start, SPAN, stride=dil) if dil > 1 else pl.ds(start, SPAN)
    return ref[idx, :] if lead is None else ref[lead, idx, :]


def _band_masks(n):
    qi = lax.broadcasted_iota(jnp.int32, (SPAN, 2 * SPAN), 0)
    ki = lax.broadcasted_iota(jnp.int32, (SPAN, 2 * SPAN), 1)
    band = (ki >= qi) & (ki <= qi + SPAN)
    return band, band & ((ki >= SPAN) | (n > 0))


def _attn_fwd(qkr, proj, *, name):
    s = qkr.shape[0]
    t = ATTN_TILE
    assert s % t == 0
    nt = s // t
    scale = HEAD_DIM ** -0.5
    kcol, vcol = N_PATTERNS * N_KV_HEADS, (N_PATTERNS + 1) * N_KV_HEADS
    blocks = _attn_blocks()

    def body(q0, q1, q2, kc_ref, kp_ref, vc_ref, vp_ref, o_ref, l_ref, og, lg):
        n = pl.program_id(1)
        band, band_first = _band_masks(n)
        q_refs = (q0, q1, q2)
        for g, start, dil, prev_in_tile, prev_start in blocks:
            q = _rows(q_refs[g], start, dil).astype(BF16)
            if prev_in_tile is not None:
                kp, vp, mask = _rows(kc_ref, prev_in_tile, dil), _rows(vc_ref, prev_in_tile, dil), band
            else:
                kp, vp, mask = _rows(kp_ref, prev_start, dil), _rows(vp_ref, prev_start, dil), band_first
            kk = jnp.concatenate([kp, _rows(kc_ref, start, dil)], axis=0).astype(BF16)
            vv = jnp.concatenate([vp, _rows(vc_ref, start, dil)], axis=0).astype(BF16)
            sc = lax.dot_general(q, kk, (((1,), (1,)), ((), ())), preferred_element_type=F32) * scale
            sc = jnp.where(mask, sc, -1e30)
            m = jnp.max(sc, axis=-1, keepdims=True)
            e = jnp.exp(sc - m)
            den = jnp.sum(e, axis=-1, keepdims=True)
            o = jnp.dot(e.astype(BF16), vv, preferred_element_type=F32) / den
            idx = pl.ds(start, SPAN, stride=dil) if dil > 1 else pl.ds(start, SPAN)
            og[g, idx, :] = o
            lg[g, idx, :] = jnp.broadcast_to(m + jnp.log(den), (SPAN, HEAD_DIM))
        l0, l1, l2 = lg[0], lg[1], lg[2]
        m = jnp.maximum(jnp.maximum(l0, l1), l2)
        w0, w1, w2 = jnp.exp(l0 - m), jnp.exp(l1 - m), jnp.exp(l2 - m)
        den = w0 + w1 + w2
        o_ref[...] = (w0 * og[0] + w1 * og[1] + w2 * og[2]) / den
        l_ref[...] = m + jnp.log(den)

    def col(c, prev=False):
        if prev:
            return pl.BlockSpec((t, HEAD_DIM), lambda h, n: (jnp.maximum(n - 1, 0), c + h))
        return pl.BlockSpec((t, HEAD_DIM), lambda h, n: (n, c + h))

    out = jax.ShapeDtypeStruct((s, N_KV_HEADS * HEAD_DIM), F32)
    return pl.pallas_call(
        body, name=name, out_shape=(out, out),
        grid=(N_KV_HEADS, nt),
        in_specs=[col(0), col(N_KV_HEADS), col(2 * N_KV_HEADS), col(kcol), col(kcol, True), col(vcol), col(vcol, True)],
        out_specs=(col(0), col(0)),
        scratch_shapes=[pltpu.VMEM((N_PATTERNS, t, HEAD_DIM), F32), pltpu.VMEM((N_PATTERNS, t, HEAD_DIM), F32)],
        compiler_params=_cp(("parallel", "arbitrary"), 48),
    )(qkr, qkr, qkr, qkr, qkr, proj, proj)


def _attn_bwd(qkr, proj, attn, lse, dcat, *, name, comm=None):
    s = qkr.shape[0]
    t = ATTN_TILE
    nt = s // t
    scale = HEAD_DIM ** -0.5
    kcol, vcol = N_PATTERNS * N_KV_HEADS, (N_PATTERNS + 1) * N_KV_HEADS
    blocks = _attn_blocks()

    def body(q0, q1, q2, kc_ref, kp_ref, vc_ref, vp_ref, o_ref, l_ref, do_ref,
             dq0, dq1, dq2, dk_ref, dv_ref, ck, cv, tkc, tvc, tkp, tvp):
        n = pl.program_id(1)
        for ref in (tkc, tvc, tkp, tvp):
            ref[...] = jnp.zeros_like(ref)

        @pl.when(n < nt)
        def _():
            band, band_first = _band_masks(n)
            q_refs, dq_refs = (q0, q1, q2), (dq0, dq1, dq2)
            for g, start, dil, prev_in_tile, prev_start in blocks:
                idx = pl.ds(start, SPAN, stride=dil) if dil > 1 else pl.ds(start, SPAN)
                q = q_refs[g][idx, :].astype(BF16)
                if prev_in_tile is not None:
                    kp, vp, mask = _rows(kc_ref, prev_in_tile, dil), _rows(vc_ref, prev_in_tile, dil), band
                else:
                    kp, vp, mask = _rows(kp_ref, prev_start, dil), _rows(vp_ref, prev_start, dil), band_first
                kk = jnp.concatenate([kp, kc_ref[idx, :]], axis=0).astype(BF16)
                vv = jnp.concatenate([vp, vc_ref[idx, :]], axis=0).astype(BF16)
                do = do_ref[idx, :]
                dsum = jnp.sum(do * o_ref[idx, :], axis=-1, keepdims=True)
                lrow = l_ref[idx, :][:, :1]
                dob = do.astype(BF16)
                sc = lax.dot_general(q, kk, (((1,), (1,)), ((), ())), preferred_element_type=F32) * scale
                p = jnp.where(mask, jnp.exp(sc - lrow), 0.0)
                dp = lax.dot_general(dob, vv, (((1,), (1,)), ((), ())), preferred_element_type=F32)
                ds = (p * (dp - dsum) * scale).astype(BF16)
                pb = p.astype(BF16)
                dq_refs[g][idx, :] = jnp.dot(ds, kk, preferred_element_type=F32)
                dkk = lax.dot_general(ds, q, (((0,), (0,)), ((), ())), preferred_element_type=F32)
                dvv = lax.dot_general(pb, dob, (((0,), (0,)), ((), ())), preferred_element_type=F32)
                tkc[idx, :] += dkk[SPAN:]
                tvc[idx, :] += dvv[SPAN:]
                if prev_in_tile is not None:
                    pidx = pl.ds(prev_in_tile, SPAN, stride=dil) if dil > 1 else pl.ds(prev_in_tile, SPAN)
                    tkc[pidx, :] += dkk[:SPAN]
                    tvc[pidx, :] += dvv[:SPAN]
                else:
                    pidx = pl.ds(prev_start, SPAN, stride=dil) if dil > 1 else pl.ds(prev_start, SPAN)
                    tkp[pidx, :] += dkk[:SPAN]
                    tvp[pidx, :] += dvv[:SPAN]

        @pl.when(n > 0)
        def _():
            dk_ref[...] = ck[...] + tkp[...]
            dv_ref[...] = (cv[...] + tvp[...]).astype(BF16)

        ck[...] = tkc[...]
        cv[...] = tvc[...]

    def col(c, prev=False):
        if prev:
            return pl.BlockSpec((t, HEAD_DIM), lambda h, n: (jnp.maximum(jnp.minimum(n, nt - 1) - 1, 0), c + h))
        return pl.BlockSpec((t, HEAD_DIM), lambda h, n: (jnp.minimum(n, nt - 1), c + h))

    kv_out = pl.BlockSpec((t, HEAD_DIM), lambda h, n: (jnp.maximum(n - 1, 0), h))
    tile = pltpu.VMEM((t, HEAD_DIM), F32)
    per_head = jax.ShapeDtypeStruct((s, N_KV_HEADS * HEAD_DIM), F32)
    return _call(
        body, name=name,
        out_shape=(per_head, per_head, per_head, per_head, jax.ShapeDtypeStruct((s, N_KV_HEADS * HEAD_DIM), BF16)),
        grid=(N_KV_HEADS, nt + 1),
        in_specs=[col(0), col(N_KV_HEADS), col(2 * N_KV_HEADS), col(kcol), col(kcol, True), col(vcol), col(vcol, True),
                  col(0), col(0), col(0)],
        out_specs=(col(0), col(0), col(0), kv_out, kv_out),
        scratch_shapes=[tile] * 6,
        args=(qkr, qkr, qkr, qkr, qkr, proj, proj, attn, lse, dcat), vmem_mib=48, comm=comm)


GELU_C0 = 0.7978845608028654
GELU_C1 = 0.044715


def _softplus_neg(lam):
    y = jnp.exp(-jnp.abs(lam))
    w = 1.0 + y
    log1p = jnp.where(w == 1.0, y, jnp.log(w) * (y / jnp.where(w == 1.0, 1.0, w - 1.0)))
    return jnp.maximum(-lam, 0.0) + log1p


def _down(cur, prev, k, row):
    if k == 0:
        return cur
    return jnp.where(row < k, pltpu.roll(prev, k, 0), pltpu.roll(cur, k, 0))


def _up(cur, nxt, k, row, tt):
    if k == 0:
        return cur
    return jnp.where(row >= tt - k, pltpu.roll(nxt, tt - k, 0), pltpu.roll(cur, tt - k, 0))


def _lru_gates(x, xp, cw, cb, wr, br, wi, bi, lam, row):
    shifts = [_down(x, xp, k, row) for k in range(CONV_WIDTH)]
    xc = cb
    for j in range(CONV_WIDTH):
        xc = xc + cw[j:j + 1, :] * shifts[CONV_WIDTH - 1 - j]
    xcb = xc.astype(BF16)
    r = jax.nn.sigmoid(jnp.dot(xcb, wr, preferred_element_type=F32) + br)
    i = jax.nn.sigmoid(jnp.dot(xcb, wi, preferred_element_type=F32) + bi)
    c = -LRU_C * _softplus_neg(lam)
    la = c * r
    a = jnp.exp(la)
    mult = jnp.sqrt(jnp.tanh(-la) * (a * a + 1.0))
    return shifts, xc, xcb, r, i, c, a, mult


def _lru_fwd(proj, cw, cb, wr, br, wi, bi, lam, *, tt, name):
    s = proj.shape[0]
    nblk = wr.shape[0]
    c = nblk * LANES
    tt = min(tt, s)
    xcol0 = (N_PATTERNS + 2) * N_KV_HEADS
    ycol0 = xcol0 + nblk

    def body(x_ref, y_ref, cw_ref, cb_ref, wr_ref, br_ref, wi_ref, bi_ref, lam_ref, rec_ref, h_ref, xprev, hc):
        n = pl.program_id(1)

        @pl.when(n == 0)
        def _():
            xprev[...] = jnp.zeros_like(xprev)
            hc[...] = jnp.zeros_like(hc)

        row = lax.broadcasted_iota(jnp.int32, (tt, LANES), 0)
        x = x_ref[...]
        _, xc, _, _, i, _, a, mult = _lru_gates(
            x, xprev[...], cw_ref[...], cb_ref[...], wr_ref[0].astype(BF16), br_ref[...],
            wi_ref[0].astype(BF16), bi_ref[...], lam_ref[...], row)
        av, bv = a, mult * (i * xc)
        k = 1
        while k < tt:
            bs = jnp.where(row < k, 0.0, pltpu.roll(bv, k, 0))
            as_ = jnp.where(row < k, 1.0, pltpu.roll(av, k, 0))
            bv = bv + av * bs
            av = av * as_
            k *= 2
        h = bv + av * hc[0:1, :]
        hc[...] = jnp.broadcast_to(h[tt - 1:tt, :], hc.shape)
        h_ref[...] = h
        y = y_ref[...]
        gel = 0.5 * y * (1.0 + jnp.tanh(GELU_C0 * (y + GELU_C1 * y * y * y)))
        rec_ref[...] = (h * gel).astype(BF16)
        xprev[...] = x

    vec = pl.BlockSpec((1, LANES), lambda b, n: (0, b))
    wblk = pl.BlockSpec((1, LANES, LANES), lambda b, n: (b, 0, 0))
    out = pl.BlockSpec((tt, LANES), lambda b, n: (n, b))
    return pl.pallas_call(
        body, name=name,
        out_shape=(jax.ShapeDtypeStruct((s, c), BF16), jax.ShapeDtypeStruct((s, c), F32)),
        grid=(nblk, s // tt),
        in_specs=[pl.BlockSpec((tt, LANES), lambda b, n: (n, xcol0 + b)),
                  pl.BlockSpec((tt, LANES), lambda b, n: (n, ycol0 + b)),
                  pl.BlockSpec((CONV_WIDTH, LANES), lambda b, n: (0, b)), vec, wblk, vec, wblk, vec, vec],
        out_specs=(out, out),
        scratch_shapes=[pltpu.VMEM((tt, LANES), F32), pltpu.VMEM((8, LANES), F32)],
        compiler_params=_cp(("parallel", "arbitrary"), 32),
    )(proj, proj, cw, cb, wr, br, wi, bi, lam)


def _lru_bwd(proj, hseq, dcat, cw, cb, wr, br, wi, bi, lam, *, tt, name, comm=None):
    s = proj.shape[0]
    nblk = wr.shape[0]
    c = nblk * LANES
    tt = min(tt, s)
    nt = s // tt
    xcol0 = (N_PATTERNS + 2) * N_KV_HEADS
    ycol0 = xcol0 + nblk
    rcol0 = N_KV_HEADS

    def body(x_ref, xp_ref, y_ref, h_ref, hp_ref, dr_ref, cw_ref, cb_ref, wr_ref, br_ref, wi_ref, bi_ref, lam_ref,
             dx_ref, dy_ref, dcw_ref, dcb_ref, dwr_ref, dbr_ref, dwi_ref, dbi_ref, dlam_ref, dxc_next, gcar, acar):
        n = pl.program_id(1)
        rt = nt - 1 - n

        @pl.when(n == 0)
        def _():
            for ref in (dxc_next, gcar, acar, dcw_ref, dcb_ref, dwr_ref, dbr_ref, dwi_ref, dbi_ref, dlam_ref):
                ref[...] = jnp.zeros_like(ref)

        row = lax.broadcasted_iota(jnp.int32, (tt, LANES), 0)
        x = x_ref[...]
        xp = jnp.where(rt > 0, xp_ref[...], 0.0)
        cwv = cw_ref[...]
        wrb, wib = wr_ref[0].astype(BF16), wi_ref[0].astype(BF16)
        lam_v = lam_ref[...]
        shifts, xc, xcb, r, i, cc, a, mult = _lru_gates(x, xp, cwv, cb_ref[...], wrb, br_ref[...], wib, bi_ref[...],
                                                        lam_v, row)
        h = h_ref[...]
        hp_last = jnp.where(rt > 0, hp_ref[7:8, :], 0.0)
        hprev = jnp.where(row < 1, hp_last, pltpu.roll(h, 1, 0))
        y = y_ref[...]
        y2 = y * y
        th = jnp.tanh(GELU_C0 * (y + GELU_C1 * y2 * y))
        gel = 0.5 * y * (1.0 + th)
        dgel = 0.5 * (1.0 + th) + 0.5 * y * (1.0 - th * th) * GELU_C0 * (1.0 + 3.0 * GELU_C1 * y2)
        drec = dr_ref[...]
        dy_ref[...] = (drec * h * dgel).astype(BF16)
        av = jnp.where(row >= tt - 1, acar[0:1, :], pltpu.roll(a, tt - 1, 0))
        bv = drec * gel
        k = 1
        while k < tt:
            bs = jnp.where(row >= tt - k, 0.0, pltpu.roll(bv, tt - k, 0))
            as_ = jnp.where(row >= tt - k, 1.0, pltpu.roll(av, tt - k, 0))
            bv = bv + av * bs
            av = av * as_
            k *= 2
        g = bv + av * gcar[0:1, :]
        gcar[...] = jnp.broadcast_to(g[0:1, :], gcar.shape)
        acar[...] = jnp.broadcast_to(a[0:1, :], acar.shape)
        da = g * hprev
        d_ixc = g * mult
        dmult = g * (i * xc)
        di = d_ixc * xc
        dxc = d_ixc * i
        a2 = a * a
        dla = da * a - dmult * (a2 / mult)
        dr = dla * cc
        dsp = jnp.sum(dla * r, axis=0, keepdims=True) * (-LRU_C)
        dlam_ref[...] += dsp * (-jax.nn.sigmoid(-lam_v))
        dzr = dr * r * (1.0 - r)
        dzi = di * i * (1.0 - i)
        dbr_ref[...] += jnp.sum(dzr, axis=0, keepdims=True)
        dbi_ref[...] += jnp.sum(dzi, axis=0, keepdims=True)
        dzrb, dzib = dzr.astype(BF16), dzi.astype(BF16)
        tn = (((0,), (0,)), ((), ()))
        ntd = (((1,), (1,)), ((), ()))
        dwr_ref[0] += lax.dot_general(xcb, dzrb, tn, preferred_element_type=F32)
        dwi_ref[0] += lax.dot_general(xcb, dzib, tn, preferred_element_type=F32)
        dxc = (dxc + lax.dot_general(dzrb, wrb, ntd, preferred_element_type=F32)
               + lax.dot_general(dzib, wib, ntd, preferred_element_type=F32))
        dcb_ref[...] += jnp.sum(dxc, axis=0, keepdims=True)
        dcw_ref[...] += jnp.concatenate(
            [jnp.sum(dxc * shifts[CONV_WIDTH - 1 - j], axis=0, keepdims=True) for j in range(CONV_WIDTH)], axis=0)
        nxt = dxc_next[...]
        dx = cwv[0:1, :] * _up(dxc, nxt, CONV_WIDTH - 1, row, tt)
        for j in range(1, CONV_WIDTH):
            dx = dx + cwv[j:j + 1, :] * _up(dxc, nxt, CONV_WIDTH - 1 - j, row, tt)
        dx_ref[...] = dx.astype(BF16)
        dxc_next[...] = dxc

    def tile(col0, prev=False):
        if prev:
            return pl.BlockSpec((tt, LANES), lambda b, n: (jnp.maximum(nt - 2 - n, 0), col0 + b))
        return pl.BlockSpec((tt, LANES), lambda b, n: (nt - 1 - n, col0 + b))

    vec = pl.BlockSpec((1, LANES), lambda b, n: (0, b))
    wblk = pl.BlockSpec((1, LANES, LANES), lambda b, n: (b, 0, 0))
    cwblk = pl.BlockSpec((CONV_WIDTH, LANES), lambda b, n: (0, b))
    hp8 = pl.BlockSpec((8, LANES), lambda b, n: (jnp.maximum((nt - 1 - n) * (tt // 8) - 1, 0), b))
    vshape = jax.ShapeDtypeStruct((1, c), F32)
    wshape = jax.ShapeDtypeStruct((nblk, LANES, LANES), F32)
    return _call(
        body, name=name,
        out_shape=(jax.ShapeDtypeStruct((s, c), BF16), jax.ShapeDtypeStruct((s, c), BF16),
                   jax.ShapeDtypeStruct((CONV_WIDTH, c), F32), vshape, wshape, vshape, wshape, vshape, vshape),
        grid=(nblk, nt),
        in_specs=[tile(xcol0), tile(xcol0, True), tile(ycol0), tile(0), hp8, tile(rcol0),
                  cwblk, vec, wblk, vec, wblk, vec, vec],
        out_specs=(tile(0), tile(0), cwblk, vec, wblk, vec, wblk, vec, vec),
        scratch_shapes=[pltpu.VMEM((tt, LANES), F32), pltpu.VMEM((8, LANES), F32), pltpu.VMEM((8, LANES), F32)],
        args=(proj, proj, proj, hseq, hseq, dcat, cw, cb, wr, br, wi, bi, lam), vmem_mib=32, comm=comm)


ROW_BLOCKS = (256, 128, 64, 32, 16, 8)


def _adamw(w, m, v, gparts, *, name):
    r, c = w.shape
    npart = gparts.shape[0]
    br = _pick(r, ROW_BLOCKS)
    c1 = 1.0 - ADAM_B1 ** ADAM_STEP
    c2 = 1.0 - ADAM_B2 ** ADAM_STEP

    def body(w_ref, m_ref, v_ref, g_ref, go_ref, d_ref, mo_ref, vo_ref):
        g = g_ref[0].astype(F32)
        for q in range(1, npart):
            g = g + g_ref[q].astype(F32)
        mn = ADAM_B1 * m_ref[...] + (1.0 - ADAM_B1) * g
        vn = ADAM_B2 * v_ref[...] + (1.0 - ADAM_B2) * (g * g)
        go_ref[...] = g
        mo_ref[...] = mn
        vo_ref[...] = vn
        d_ref[...] = -ADAM_LR * ((mn / c1) / (jnp.sqrt(vn / c2) + ADAM_EPS) + ADAM_WD * w_ref[...])

    blk = pl.BlockSpec((br, c), lambda i: (i, 0))
    out = jax.ShapeDtypeStruct((r, c), F32)
    return pl.pallas_call(
        body, name=name, out_shape=(out, out, out, out), grid=(r // br,),
        in_specs=[blk, blk, blk, pl.BlockSpec((npart, br, c), lambda i: (0, i, 0))],
        out_specs=(blk, blk, blk, blk),
        compiler_params=_cp(("parallel",), 48),
    )(w, m, v, gparts)


def _sum_parts(parts, *, name):
    npart, r, c = parts.shape
    br = next((b for b in range(min(r, 2048) // 8 * 8, 0, -8) if r % b == 0), r)

    def body(p_ref, o_ref):
        acc = p_ref[0]
        for q in range(1, npart):
            acc = acc + p_ref[q]
        o_ref[...] = acc

    return pl.pallas_call(
        body, name=name, out_shape=jax.ShapeDtypeStruct((r, c), F32), grid=(r // br,),
        in_specs=[pl.BlockSpec((npart, br, c), lambda i: (0, i, 0))],
        out_specs=pl.BlockSpec((br, c), lambda i: (i, 0)),
        compiler_params=_cp(("parallel",), 48),
    )(parts)


HBM = pl.BlockSpec(memory_space=pltpu.HBM)


def _mesh_pos():
    return lax.axis_index("x"), lax.axis_index("y"), lax.axis_index("c")


def _all_gather(shards, *, name):
    comm = _gather_comm(shards)
    na = len(shards)

    def body(*refs):
        ins, outs, sems = refs[:na], refs[na:2 * na], refs[2 * na:]
        comm.start(ins, outs, sems)
        comm.mid(ins, outs, sems)
        comm.end(ins, outs, sems)

    return pl.pallas_call(
        body, name=name, out_shape=tuple(comm.out_shapes),
        in_specs=[HBM] * na, out_specs=tuple([HBM] * na), scratch_shapes=comm.scratch,
    )(*shards)


def _gather_comm(shards):
    na = len(shards)

    def parts(x_refs, out_refs, sems):
        send_sems, recv_sems, local_sems = sems
        x, y, c = _mesh_pos()
        me, sibling = (x, y, c), (x, y, 1 - c)
        chips = [(1 - x, y), (x, 1 - y), (1 - x, 1 - y)]

        def copy(a, k, block, to, src=None):
            px, py, pc = block
            dst = out_refs[a].at[4 * px + 2 * py + pc]
            return pltpu.make_async_remote_copy(
                src_ref=dst if src is None else src, dst_ref=dst,
                send_sem=send_sems.at[a, k], recv_sem=recv_sems.at[a, k],
                device_id=to, device_id_type=MESH)

        def mine(a):
            return pltpu.make_async_copy(x_refs[a], out_refs[a].at[4 * x + 2 * y + c], local_sems.at[a])

        def first(a):
            return [copy(a, 0, me, sibling, src=x_refs[a])] + [
                copy(a, 1 + j, me, (*chip, c), src=x_refs[a]) for j, chip in enumerate(chips)]

        def passed(a, j):
            return copy(a, 4 + j, (*chips[j], c), sibling)

        return me, sibling, chips, c, copy, mine, first, passed

    def start(x_refs, out_refs, sems):
        *_, mine, first, _ = parts(x_refs, out_refs, sems)
        for a in range(na):
            mine(a).start()
            for cp in first(a):
                cp.start()

    def mid(x_refs, out_refs, sems):
        me, _, chips, c, copy, _, _, passed = parts(x_refs, out_refs, sems)
        for j, chip in enumerate(chips):
            for a in range(na):
                copy(a, 1 + j, (*chip, c), me).wait_recv()
                passed(a, j).start()

    def end(x_refs, out_refs, sems):
        me, sibling, chips, c, copy, mine, first, passed = parts(x_refs, out_refs, sems)
        for a in range(na):
            copy(a, 0, sibling, me).wait_recv()
            for j, chip in enumerate(chips):
                copy(a, 4 + j, (*chip, 1 - c), me).wait_recv()
        for a in range(na):
            for cp in first(a) + [passed(a, j) for j in range(3)]:
                cp.wait_send()
            mine(a).wait()

    return _Comm(
        shards, [jax.ShapeDtypeStruct((N_DEV,) + a.shape, a.dtype) for a in shards],
        [pltpu.SemaphoreType.DMA((na, 7)), pltpu.SemaphoreType.DMA((na, 7)), pltpu.SemaphoreType.DMA((na,))],
        start, end, mid)


def _scatter_comm(g8s):
    na = len(g8s)

    def parts(g_refs, buf_refs, sems):
        send_sems, recv_sems, local_sems = sems
        x, y, c = _mesh_pos()
        me_idx = 4 * x + 2 * y + c

        def peer(k):
            return (1 - x if k & 4 else x, 1 - y if k & 2 else y, 1 - c if k & 1 else c)

        def copy(a, k, slot):
            px, py, pc = peer(k)
            return pltpu.make_async_remote_copy(
                src_ref=g_refs[a].at[4 * px + 2 * py + pc], dst_ref=buf_refs[a].at[slot],
                send_sem=send_sems.at[a, k - 1], recv_sem=recv_sems.at[a, k - 1],
                device_id=(px, py, pc), device_id_type=MESH)

        def mine(a):
            return pltpu.make_async_copy(g_refs[a].at[me_idx], buf_refs[a].at[me_idx], local_sems.at[a])

        return me_idx, peer, copy, mine

    def start(g_refs, buf_refs, sems):
        me_idx, _, copy, mine = parts(g_refs, buf_refs, sems)
        for a in range(na):
            mine(a).start()
            for k in range(1, N_DEV):
                copy(a, k, me_idx).start()

    def end(g_refs, buf_refs, sems):
        me_idx, peer, copy, mine = parts(g_refs, buf_refs, sems)
        for a in range(na):
            for k in range(1, N_DEV):
                px, py, pc = peer(k)
                copy(a, k, 4 * px + 2 * py + pc).wait_recv()
        for a in range(na):
            for k in range(1, N_DEV):
                copy(a, k, me_idx).wait_send()
            mine(a).wait()

    return _Comm(
        g8s, [jax.ShapeDtypeStruct(g.shape, g.dtype) for g in g8s],
        [pltpu.SemaphoreType.DMA((na, N_DEV - 1)), pltpu.SemaphoreType.DMA((na, N_DEV - 1)),
         pltpu.SemaphoreType.DMA((na,))],
        start, end)


def _join_comm(c1, c2):
    n1i, n1o, n1s = len(c1.arrays), len(c1.out_shapes), len(c1.scratch)

    def both(f1, f2):
        def run(ins, outs, sems):
            if f1 is not None:
                f1(ins[:n1i], outs[:n1o], sems[:n1s])
            if f2 is not None:
                f2(ins[n1i:], outs[n1o:], sems[n1s:])
        return run

    mid = both(c1.mid, c2.mid) if (c1.mid is not None or c2.mid is not None) else None
    return _Comm(c1.arrays + c2.arrays, c1.out_shapes + c2.out_shapes, c1.scratch + c2.scratch,
                 both(c1.start, c2.start), both(c1.end, c2.end), mid)


BIG_WEIGHTS = ("ffn1_w_gate", "ffn1_w_up", "ffn1_w_down", "w_in", "w_out",
               "ffn2_w_gate", "ffn2_w_up", "ffn2_w_down", "w_ple_proj", "w_ple_gate")
COLUMN_SHARDED = ("ffn1_w_gate", "ffn1_w_up", "w_in", "ffn2_w_gate", "ffn2_w_up", "w_ple_proj", "conv_w")
SMALL_WEIGHTS = ("ln1_g", "ln1_b", "conv_b", "w_rgate", "b_rgate", "w_igate", "b_igate", "lru_lambda",
                 "ln2_g", "ln2_b", "ln3_g", "ln3_b")
SMALL_GRADS = SMALL_WEIGHTS + ("conv_w",)


class _Exchange:
    def __init__(self, full):
        self.full = dict(full)
        self.transposed = {}
        self.grads = {}

    def __getitem__(self, name):
        return self.full[name]

    def t(self, name):
        if name not in self.transposed:
            self.transposed[name] = self.full[name].T
        return self.transposed[name]

    def gather(self, names):
        return None, None

    def scatter(self, names):
        return None, None

    def gather_small(self):
        return None, None


class _MeshExchange(_Exchange):
    def __init__(self, full, shards):
        super().__init__(full)
        self.shards = shards
        self.reduced = {}
        self.small_parts = None

    def gather(self, names):
        def done(outs):
            for n, o in zip(names, outs):
                self.take(n, o)
        return _gather_comm([self.shards[n] for n in names]), done

    def take(self, name, gathered):
        stacked = gathered.reshape((N_DEV * gathered.shape[1],) + gathered.shape[2:])
        if name in COLUMN_SHARDED:
            self.transposed[name] = stacked
            self.full[name] = stacked.T
        else:
            self.full[name] = stacked

    def scatter(self, names):
        def done(outs):
            self.reduced.update(zip(names, outs))
        if not names:
            return None, None
        return _scatter_comm([_to_owner_blocks(n, self.grads[n]) for n in names]), done

    def gather_small(self):
        def done(outs):
            self.small_parts, = outs
        packed = jnp.concatenate([_rows128(self.grads[n]) for n in SMALL_GRADS], axis=0)
        return _gather_comm([packed]), done


def _carried(comm_done, call):
    comm, done = comm_done
    res = call(comm)
    if comm is None:
        return res
    res, outs = res
    done(outs)
    return res


def _dw(a, b, *, scale=1.0, name, comm=None):
    k, m = a.shape
    n = b.shape[1]
    return _mm(a, b, ta=True, scale=scale, out_dtype=BF16, bm=_pick(m, (1024, 512, 256, 128)),
               bn=_pick(n, (512, 256, 128)), bk=k, name=name, comm=comm)


def _ffn_bwd(ex, names, saved, xb_in, dz, dzb, ln_in, tag, send_on_dh=(), send_on_dwu=(), send_on_dx=(), also=None):
    gate, up, down = names
    g, u, h, _, _ = saved
    f = ex[gate].shape[1]
    ex.grads[down] = _dw(h, dzb, scale=0.5, name=f"{tag}_dwd")
    dg, du = _carried(ex.scatter(send_on_dh), lambda c: _ffn_bwd_dh(
        dzb, ex.t(down), g, u, scale=0.5, bm=1024, bn=_pick(f, (512, 256, 128)), name=f"{tag}_dh", chunks=4, comm=c))
    ex.grads[gate] = _dw(xb_in, dg, name=f"{tag}_dwg")
    ex.grads[up] = _carried(ex.scatter(send_on_dwu), lambda c: _dw(xb_in, du, name=f"{tag}_dwu", comm=c))
    last = ex.scatter(send_on_dx)
    if also is not None and also[0] is not None:
        done_up, done_also = last[1], also[1]
        n_up = len(last[0].out_shapes)
        last = (_join_comm(last[0], also[0]), lambda outs: (done_up(outs[:n_up]), done_also(outs[n_up:])))
    d = dz.shape[1]
    dx = _carried(last, lambda c: _ffn_dx(
        dg, du, ex.t(gate), ex.t(up), dz, extra_scale=DEEPNORM_ALPHA,
        bm=512, bn=_pick(d, (512, 256, 128)), name=f"{tag}_dx", comm=c))
    return dx if ln_in is None else _ln_bwd(dx, *ln_in, bm=256, name=f"{tag}_ln_bwd")


def _local_step(x, p, target, positions, w):
    s, d = x.shape
    tabs = _rope_tables(positions)
    xb = x.astype(BF16)
    f = w["ffn1_w_gate"].shape[1]
    ffn_bn, ln_bn, ln_bn_short_k = _pick(f, (512, 256, 128)), _pick(d, (256, 128)), _pick(d, (1024, 512, 256, 128))
    g1, u1, h1 = _carried(w.gather(("ffn1_w_down", "w_in", "w_out")), lambda c: _ffn_up(
        xb, w["ffn1_w_gate"], w["ffn1_w_up"], bm=1024, bn=ffn_bn, name="ffn1_up", comm=c))
    x1, x1b, xh1, rs1 = _carried(w.gather(("ffn2_w_gate", "ffn2_w_up")), lambda c: _mm_ln(
        h1, w["ffn1_w_down"], x, w["ln1_g"], w["ln1_b"], res_scale=DEEPNORM_ALPHA, mm_scale=0.5,
        bm=512, bn=ln_bn, name="ffn1_down_ln", comm=c))
    sv1 = (g1, u1, h1, xh1, rs1)
    pw = w["w_in"].shape[1]
    proj = _carried(w.gather(("ffn2_w_down", "w_ple_gate", "w_ple_proj")), lambda c: _mm(
        x1b, w["w_in"], bm=1024, bn=_pick(pw, (512, 256, 128)), bk=d, name="in_proj", comm=c))
    nqk = (N_PATTERNS + 1) * N_KV_HEADS
    qkr = _rotary(proj, tabs, n_cols=nqk, inverse=False, out_dtype=F32, bs=1024, name="rotary")
    attn, lse = _attn_fwd(qkr, proj, name="attn_fwd")
    lru_w = (w["conv_w"], w["conv_b"], w["w_rgate"], w["b_rgate"], w["w_igate"], w["b_igate"], w["lru_lambda"])
    rec, hseq = _lru_fwd(proj, *lru_w, tt=512, name="lru_fwd")
    cat = jnp.concatenate([attn.astype(BF16), rec], axis=1)
    x2, x2b, xh2, rs2 = _mm_ln(cat, w["w_out"], x1, w["ln2_g"], w["ln2_b"], res_scale=DEEPNORM_ALPHA, mm_scale=1.0,
                               bm=512, bn=ln_bn_short_k, name="out_proj_ln")
    g2, u2, h2 = _ffn_up(x2b, w["ffn2_w_gate"], w["ffn2_w_up"], bm=1024, bn=ffn_bn, name="ffn2_up")
    x3, x3b, xh3, rs3 = _mm_ln(h2, w["ffn2_w_down"], x2, w["ln3_g"], w["ln3_b"], res_scale=DEEPNORM_ALPHA, mm_scale=0.5,
                               bm=512, bn=ln_bn, name="ffn2_down_ln")
    sv3 = (g2, u2, h2, xh3, rs3)
    lsum, dy, dgate, dple = _ple_loss(x3, x3b, p, w["w_ple_gate"], w["w_ple_proj"], target,
                                      bm=1024, bn=_pick(d, (512, 256, 128)), name="ple_loss")
    grads = w.grads
    grads["w_ple_gate"] = _dw(x3b, dgate, name="dw_ple_gate")
    grads["w_ple_proj"] = _dw(p, dple, name="dw_ple_proj")
    dz3, dz3b, grads["ln3_g"], grads["ln3_b"] = _carried(w.scatter(("w_ple_gate", "w_ple_proj")), lambda c: _mm_dx(
        dgate, w.t("w_ple_gate"), dy, xh3, rs3, w["ln3_g"], extra_scale=1.0, bm=512, bn=ln_bn_short_k,
        name="ple_dx", comm=c))
    dz2, dz2b, grads["ln2_g"], grads["ln2_b"] = _ffn_bwd(
        w, ("ffn2_w_gate", "ffn2_w_up", "ffn2_w_down"), sv3, x2b, dz3, dz3b, (xh2, rs2, w["ln2_g"]), "ffn2",
        send_on_dx=("ffn2_w_down",))
    grads["w_out"] = _dw(cat, dz2b, name="dw_out")
    dcat = _mm(dz2b, w.t("w_out"), bm=1024, bn=_pick(d, (512, 256, 128)), bk=d, name="out_proj_dx")
    dq0, dq1, dq2, dk, dvb = _carried(w.scatter(("ffn2_w_gate",)), lambda c: _attn_bwd(
        qkr, proj, attn, lse, dcat, name="attn_bwd", comm=c))
    nh = N_KV_HEADS
    dqkv = [_rotary(t, tabs, n_cols=nh, inverse=True, out_dtype=BF16, bs=1024, name=f"rotary_bwd{i}")
            for i, t in enumerate((dq0, dq1, dq2, dk))]
    (dxb, dyb, grads["conv_w"], grads["conv_b"], grads["w_rgate"], grads["b_rgate"], grads["w_igate"],
     grads["b_igate"], grads["lru_lambda"]) = _carried(w.scatter(("ffn2_w_up",)), lambda c: _lru_bwd(
         proj, hseq, dcat, *lru_w, tt=512, name="lru_bwd", comm=c))
    dproj = jnp.concatenate(dqkv + [dvb, dxb, dyb], axis=1)
    grads["w_in"] = _carried(w.scatter(("w_out",)), lambda c: _dw(x1b, dproj, name="dw_in", comm=c))
    dz1, dz1b, grads["ln1_g"], grads["ln1_b"] = _carried(w.scatter(("w_in",)), lambda c: _mm_dx(
        dproj, w.t("w_in"), dz2, xh1, rs1, w["ln1_g"], extra_scale=DEEPNORM_ALPHA,
        bm=512, bn=ln_bn, name="in_proj_dx", comm=c))
    grad_x = _ffn_bwd(w, ("ffn1_w_gate", "ffn1_w_up", "ffn1_w_down"), sv1, xb, dz1, dz1b, None, "ffn1",
                      send_on_dh=("ffn1_w_down",), send_on_dwu=("ffn1_w_gate",), send_on_dx=("ffn1_w_up",),
                      also=w.gather_small())
    return lsum, grad_x


def _to_full(name, gathered):
    if name in COLUMN_SHARDED:
        _, r, c = gathered.shape
        return jnp.transpose(gathered, (1, 0, 2)).reshape(r, N_DEV * c)
    return gathered.reshape((N_DEV * gathered.shape[1],) + gathered.shape[2:])


def _to_owner_blocks(name, full):
    if name in COLUMN_SHARDED:
        r, c = full.shape
        return jnp.transpose(full.reshape(r, N_DEV, c // N_DEV), (1, 0, 2))
    return full.reshape((N_DEV, full.shape[0] // N_DEV) + full.shape[1:])


def _rows128(a):
    flat = a.reshape(-1, LANES)
    pad = (-flat.shape[0]) % 8
    return jnp.pad(flat, ((0, pad), (0, 0))) if pad else flat


def kernel(x, p, positions, ffn1_w_gate, ffn1_w_up, ffn1_w_down, ln1_g, ln1_b, w_in, conv_w, conv_b, w_rgate, b_rgate, w_igate, b_igate, lru_lambda, w_out, ln2_g, ln2_b, ffn2_w_gate, ffn2_w_up, ffn2_w_down, ln3_g, ln3_b, w_ple_proj, w_ple_gate, loss_target, m_ffn1_w_gate, m_ffn1_w_up, m_ffn1_w_down, m_ln1_g, m_ln1_b, m_w_in, m_conv_w, m_conv_b, m_w_rgate, m_b_rgate, m_w_igate, m_b_igate, m_lru_lambda, m_w_out, m_ln2_g, m_ln2_b, m_ffn2_w_gate, m_ffn2_w_up, m_ffn2_w_down, m_ln3_g, m_ln3_b, m_w_ple_proj, m_w_ple_gate, v_ffn1_w_gate, v_ffn1_w_up, v_ffn1_w_down, v_ln1_g, v_ln1_b, v_w_in, v_conv_w, v_conv_b, v_w_rgate, v_b_rgate, v_w_igate, v_b_igate, v_lru_lambda, v_w_out, v_ln2_g, v_ln2_b, v_ffn2_w_gate, v_ffn2_w_up, v_ffn2_w_down, v_ln3_g, v_ln3_b, v_w_ple_proj, v_w_ple_gate):
    names = ("ffn1_w_gate", "ffn1_w_up", "ffn1_w_down", "ln1_g", "ln1_b", "w_in", "conv_w", "conv_b", "w_rgate",
             "b_rgate", "w_igate", "b_igate", "lru_lambda", "w_out", "ln2_g", "ln2_b", "ffn2_w_gate", "ffn2_w_up",
             "ffn2_w_down", "ln3_g", "ln3_b", "w_ple_proj", "w_ple_gate")
    ws = (ffn1_w_gate, ffn1_w_up, ffn1_w_down, ln1_g, ln1_b, w_in, conv_w, conv_b, w_rgate, b_rgate, w_igate, b_igate,
          lru_lambda, w_out, ln2_g, ln2_b, ffn2_w_gate, ffn2_w_up, ffn2_w_down, ln3_g, ln3_b, w_ple_proj, w_ple_gate)
    ms = (m_ffn1_w_gate, m_ffn1_w_up, m_ffn1_w_down, m_ln1_g, m_ln1_b, m_w_in, m_conv_w, m_conv_b, m_w_rgate, m_b_rgate,
          m_w_igate, m_b_igate, m_lru_lambda, m_w_out, m_ln2_g, m_ln2_b, m_ffn2_w_gate, m_ffn2_w_up, m_ffn2_w_down,
          m_ln3_g, m_ln3_b, m_w_ple_proj, m_w_ple_gate)
    vs = (v_ffn1_w_gate, v_ffn1_w_up, v_ffn1_w_down, v_ln1_g, v_ln1_b, v_w_in, v_conv_w, v_conv_b, v_w_rgate, v_b_rgate,
          v_w_igate, v_b_igate, v_lru_lambda, v_w_out, v_ln2_g, v_ln2_b, v_ffn2_w_gate, v_ffn2_w_up, v_ffn2_w_down,
          v_ln3_g, v_ln3_b, v_w_ple_proj, v_w_ple_gate)
    def local(a):
        return a[0] if a.ndim >= 3 else a

    w_loc = {n: local(a) for n, a in zip(names, ws)}
    m_loc = {n: local(a) for n, a in zip(names, ms)}
    v_loc = {n: local(a) for n, a in zip(names, vs)}
    out_shapes = {n: a.shape for n, a in zip(names, ws)}

    shards = {n: (w_loc[n].T if n in COLUMN_SHARDED else w_loc[n]).astype(BF16) for n in BIG_WEIGHTS}
    gate1, up1, conv_all = _all_gather([shards["ffn1_w_gate"], shards["ffn1_w_up"], w_loc["conv_w"]], name="gather_first")
    ex = _MeshExchange({n: w_loc[n] for n in SMALL_WEIGHTS}, shards)
    ex.full["conv_w"] = _to_full("conv_w", conv_all)
    ex.take("ffn1_w_gate", gate1)
    ex.take("ffn1_w_up", up1)

    lsum, grad_x = _local_step(x[0], p[0, 0], loss_target[0], positions[0], ex)
    grads, reduced = ex.grads, ex.reduced
    d_model = x.shape[-1]
    loss = lax.psum(lsum[0, 0] * (0.5 / d_model), ("x", "y", "c"))

    small = SMALL_GRADS
    summed = _sum_parts(ex.small_parts, name="sum_small_grads")
    small_grads, row = {}, 0
    for n in small:
        rows = grads[n].size // LANES
        small_grads[n] = summed[row:row + rows].reshape(grads[n].shape)
        row += rows + (-rows) % 8
    me = 4 * lax.axis_index("x") + 2 * lax.axis_index("y") + lax.axis_index("c")
    cw_cols = w_loc["conv_w"].shape[1]
    small_grads["conv_w"] = lax.dynamic_slice_in_dim(small_grads["conv_w"], me * cw_cols, cw_cols, axis=1)

    out_g, out_d, out_m, out_v = {}, {}, {}, {}
    for n in names:
        wl, ml, vl = w_loc[n], m_loc[n], v_loc[n]
        shape = wl.shape
        if n in BIG_WEIGHTS:
            gparts = reduced[n]
        else:
            gparts = small_grads[n].reshape((1,) + shape)
        if wl.ndim == 3:
            wl, ml, vl = (t.reshape(-1, shape[-1]) for t in (wl, ml, vl))
            gparts = gparts.reshape(gparts.shape[0], -1, shape[-1])
        res = _adamw(wl, ml, vl, gparts, name=f"adamw_{n}")
        out_g[n], out_d[n], out_m[n], out_v[n] = (t.reshape(out_shapes[n]) for t in res)

    return (loss, grad_x[None], *[out_g[n] for n in names], *[out_d[n] for n in names],
            *[out_m[n] for n in names], *[out_v[n] for n in names])
```

```python
import jax
import jax.numpy as jnp
from jax import lax
from jax.experimental import pallas as pl
from jax.experimental.pallas import tpu as pltpu

F32 = jnp.float32
BF16 = jnp.bfloat16

N_DEV = 8
LANES = 128
MIB = 1 << 20

HEAD_DIM = 128
N_KV_HEADS = 4
DILATIONS = (1, 4, 16)
N_PATTERNS = 3
SPAN = 128
ROT_DIMS = 32
ROPE_THETA = 500000.0
LRU_C = 8.0
CONV_WIDTH = 4
LN_EPS = 1e-5
DEEPNORM_ALPHA = 2.0 ** 0.25
ATTN_TILE = SPAN * DILATIONS[-1]

ADAM_LR = 0.001
ADAM_B1 = 0.9
ADAM_B2 = 0.999
ADAM_EPS = 1e-08
ADAM_WD = 0.01
ADAM_STEP = 10

MESH = pl.DeviceIdType.MESH
NT_DIMS = (((1,), (1,)), ((), ()))
EPILOGUE_ROWS = 64


def _cp(semantics, vmem_mib):
    return pltpu.CompilerParams(dimension_semantics=semantics, vmem_limit_bytes=vmem_mib * MIB)


def _pick(n, candidates):
    for c in candidates:
        if n % c == 0:
            return c
    return n


class _Comm:
    def __init__(self, arrays, out_shapes, scratch, start, end, mid=None):
        self.arrays, self.out_shapes, self.scratch = list(arrays), list(out_shapes), list(scratch)
        self.start, self.mid, self.end = start, mid, end


def _call(body, *, name, grid, in_specs, out_specs, out_shape, args, scratch_shapes=(), vmem_mib, comm=None):
    single = not isinstance(out_shape, (tuple, list))
    out_shape_t = (out_shape,) if single else tuple(out_shape)
    out_specs_t = (out_specs,) if single else tuple(out_specs)
    params = _cp(("arbitrary",) * len(grid), vmem_mib)
    if comm is None:
        res = pl.pallas_call(body, name=name, grid=grid, in_specs=list(in_specs), out_specs=out_specs_t,
                             out_shape=out_shape_t, scratch_shapes=list(scratch_shapes), compiler_params=params)(*args)
        return res[0] if single else res
    n_in, n_out, n_scr = len(args), len(out_shape_t), len(scratch_shapes)
    nci, nco = len(comm.arrays), len(comm.out_shapes)
    total = 1
    for g in grid:
        total *= g

    def wrapped(*refs):
        ins, refs = refs[:n_in], refs[n_in:]
        cin, refs = refs[:nci], refs[nci:]
        outs, refs = refs[:n_out], refs[n_out:]
        cout, refs = refs[:nco], refs[nco:]
        scr, csem = refs[:n_scr], refs[n_scr:]
        step = pl.program_id(0)
        for ax in range(1, len(grid)):
            step = step * grid[ax] + pl.program_id(ax)

        @pl.when(step == 0)
        def _():
            comm.start(cin, cout, csem)

        body(*ins, *outs, *scr)
        if comm.mid is not None:
            @pl.when(step == (3 * total) // 4)
            def _():
                comm.mid(cin, cout, csem)

        @pl.when(step == total - 1)
        def _():
            comm.end(cin, cout, csem)

    hbm = pl.BlockSpec(memory_space=pltpu.HBM)
    res = pl.pallas_call(
        wrapped, name=name, grid=grid,
        in_specs=list(in_specs) + [hbm] * nci,
        out_specs=out_specs_t + (hbm,) * nco,
        out_shape=out_shape_t + tuple(comm.out_shapes),
        scratch_shapes=list(scratch_shapes) + comm.scratch,
        compiler_params=params)(*args, *comm.arrays)
    own, extra = res[:n_out], res[n_out:]
    return (own[0] if single else own), extra


def _mm(a, b, *, ta=False, tb=False, out_dtype=F32, scale=1.0, bm, bn, bk, name, comm=None):
    m, k = (a.shape[1], a.shape[0]) if ta else a.shape
    n = b.shape[0] if tb else b.shape[1]
    bm, bn, bk = min(bm, m), min(bn, n), min(bk, k)
    assert m % bm == 0 and n % bn == 0 and k % bk == 0, (name, m, n, k, bm, bn, bk)
    nk = k // bk
    a_spec = pl.BlockSpec((bk, bm), lambda i, j, kk: (kk, i)) if ta else pl.BlockSpec((bm, bk), lambda i, j, kk: (i, kk))
    b_spec = pl.BlockSpec((bn, bk), lambda i, j, kk: (j, kk)) if tb else pl.BlockSpec((bk, bn), lambda i, j, kk: (kk, j))
    dn = (((0 if ta else 1,), (1 if tb else 0,)), ((), ()))

    def body(a_ref, b_ref, o_ref, *acc):
        part = lax.dot_general(a_ref[...].astype(BF16), b_ref[...].astype(BF16), dn, preferred_element_type=F32)
        if nk == 1:
            o_ref[...] = (part * scale).astype(out_dtype)
            return
        acc_ref, = acc
        kk = pl.program_id(2)

        @pl.when(kk == 0)
        def _():
            acc_ref[...] = part

        @pl.when(kk > 0)
        def _():
            acc_ref[...] += part

        @pl.when(kk == nk - 1)
        def _():
            o_ref[...] = (acc_ref[...] * scale).astype(out_dtype)

    return _call(
        body, name=name,
        out_shape=jax.ShapeDtypeStruct((m, n), out_dtype),
        grid=(m // bm, n // bn, nk),
        in_specs=[a_spec, b_spec],
        out_specs=pl.BlockSpec((bm, bn), lambda i, j, kk: (i, j)),
        scratch_shapes=[pltpu.VMEM((bm, bn), F32)] if nk > 1 else [],
        args=(a, b), vmem_mib=56, comm=comm)


def _ffn_up(xb, wg, wu, *, bm, bn, name, comm=None):
    s, d = xb.shape
    f = wg.shape[0]
    bm, bn = min(bm, s), min(bn, f)
    assert s % bm == 0 and f % bn == 0

    def body(x_ref, wg_ref, wu_ref, hg_ref, hu_ref, h_ref):
        x = x_ref[...]
        g = lax.dot_general(x, wg_ref[...], NT_DIMS, preferred_element_type=F32)
        u = lax.dot_general(x, wu_ref[...], NT_DIMS, preferred_element_type=F32)
        sig = jax.nn.sigmoid(g)
        silu = g * sig
        hg_ref[...] = (u * (sig * (1.0 + g * (1.0 - sig)))).astype(BF16)
        hu_ref[...] = silu.astype(BF16)
        h_ref[...] = (silu * u).astype(BF16)

    out = jax.ShapeDtypeStruct((s, f), BF16)
    blk = pl.BlockSpec((bm, bn), lambda i, j: (i, j))
    return _call(
        body, name=name, out_shape=(out, out, out),
        grid=(s // bm, f // bn),
        in_specs=[pl.BlockSpec((bm, d), lambda i, j: (i, 0)),
                  pl.BlockSpec((bn, d), lambda i, j: (j, 0)),
                  pl.BlockSpec((bn, d), lambda i, j: (j, 0))],
        out_specs=(blk, blk, blk),
        args=(xb, wg, wu), vmem_mib=56, comm=comm)


def _ffn_bwd_dh(dzb, wd, g, u, *, scale, bm, bn, name, chunks=2, comm=None):
    s, d = dzb.shape
    f = wd.shape[0]
    bm, bn = min(bm, s), min(bn, f)
    assert s % bm == 0 and f % bn == 0

    cr = bm // chunks

    def body(dz_ref, wd_ref, hg_ref, hu_ref, dg_ref, du_ref):
        for r in range(chunks):
            rows = slice(r * cr, (r + 1) * cr)
            dh = lax.dot_general(dz_ref[rows, :], wd_ref[...], NT_DIMS, preferred_element_type=F32) * scale
            dg_ref[rows, :] = (dh * hg_ref[rows, :].astype(F32)).astype(BF16)
            du_ref[rows, :] = (dh * hu_ref[rows, :].astype(F32)).astype(BF16)

    out = jax.ShapeDtypeStruct((s, f), BF16)
    blk = pl.BlockSpec((bm, bn), lambda i, j: (i, j))
    return _call(
        body, name=name, out_shape=(out, out),
        grid=(s // bm, f // bn),
        in_specs=[pl.BlockSpec((bm, d), lambda i, j: (i, 0)),
                  pl.BlockSpec((bn, d), lambda i, j: (j, 0)), blk, blk],
        out_specs=(blk, blk),
        args=(dzb, wd, g, u), vmem_mib=56, comm=comm)


def _full_rows(acc_ref, rows, nj):
    return jnp.concatenate([acc_ref[jj, rows, :] for jj in range(nj)], axis=1)


def _mm_ln(a, b, res, gamma, beta, *, res_scale, mm_scale, bm, bn, name, comm=None):
    s, k = a.shape
    d = b.shape[1]
    bm, bn = min(bm, s), min(bn, d)
    assert s % bm == 0 and d % bn == 0
    nj = d // bn
    ch = min(EPILOGUE_ROWS, bm)

    def body(a_ref, b_ref, r_ref, g_ref, be_ref, y_ref, yb_ref, xh_ref, rs_ref, acc_ref):
        j = pl.program_id(1)
        acc_ref[j] = jnp.dot(a_ref[...], b_ref[...], preferred_element_type=F32)

        @pl.when(j == nj - 1)
        def _():
            def chunk(ci, carry):
                rows = pl.ds(pl.multiple_of(ci * ch, ch), ch)
                z = res_scale * r_ref[rows, :] + mm_scale * _full_rows(acc_ref, rows, nj)
                mu = jnp.mean(z, axis=-1, keepdims=True)
                zc = z - mu
                var = jnp.mean(zc * zc, axis=-1, keepdims=True)
                rstd = lax.rsqrt(var + LN_EPS)
                xh = zc * rstd
                y = xh * g_ref[...] + be_ref[...]
                y_ref[rows, :] = y
                yb_ref[rows, :] = y.astype(BF16)
                xh_ref[rows, :] = xh
                rs_ref[rows, :] = rstd
                return carry

            lax.fori_loop(0, bm // ch, chunk, 0)

    row = pl.BlockSpec((bm, d), lambda i, j: (i, 0))
    vec = pl.BlockSpec((1, d), lambda i, j: (0, 0))
    return _call(
        body, name=name,
        out_shape=(jax.ShapeDtypeStruct((s, d), F32), jax.ShapeDtypeStruct((s, d), BF16),
                   jax.ShapeDtypeStruct((s, d), F32), jax.ShapeDtypeStruct((s, 1), F32)),
        grid=(s // bm, nj),
        in_specs=[pl.BlockSpec((bm, k), lambda i, j: (i, 0)),
                  pl.BlockSpec((k, bn), lambda i, j: (0, j)), row, vec, vec],
        out_specs=(row, row, row, pl.BlockSpec((bm, 1), lambda i, j: (i, 0))),
        scratch_shapes=[pltpu.VMEM((nj, bm, bn), F32)],
        args=(a, b, res, gamma, beta), vmem_mib=58, comm=comm)


def _mm_dx(a, wt, extra, xhat, rstd, gamma, *, extra_scale, bm, bn, name, tb=False, comm=None):
    s, k = a.shape
    d = wt.shape[0] if tb else wt.shape[1]
    bm, bn = min(bm, s), min(bn, d)
    assert s % bm == 0 and d % bn == 0
    nj = d // bn
    ch = min(EPILOGUE_ROWS, bm)
    dims = NT_DIMS if tb else (((1,), (0,)), ((), ()))

    def body(a_ref, w_ref, e_ref, xh_ref, rs_ref, g_ref, dz_ref, dzb_ref, dg_ref, db_ref, acc_ref):
        i = pl.program_id(0)
        j = pl.program_id(1)
        acc_ref[j] = lax.dot_general(a_ref[...], w_ref[...], dims, preferred_element_type=F32)

        @pl.when(j == nj - 1)
        def _():
            def chunk(ci, carry):
                dgp, dbp = carry
                rows = pl.ds(pl.multiple_of(ci * ch, ch), ch)
                dx = extra_scale * e_ref[rows, :] + _full_rows(acc_ref, rows, nj)
                xh = xh_ref[rows, :]
                dxh = dx * g_ref[...]
                m1 = jnp.mean(dxh, axis=-1, keepdims=True)
                m2 = jnp.mean(dxh * xh, axis=-1, keepdims=True)
                dz = rs_ref[rows, :] * (dxh - m1 - xh * m2)
                dz_ref[rows, :] = dz
                dzb_ref[rows, :] = dz.astype(BF16)
                return dgp + jnp.sum(dx * xh, axis=0, keepdims=True), dbp + jnp.sum(dx, axis=0, keepdims=True)

            zero = jnp.zeros((1, d), F32)
            dgp, dbp = lax.fori_loop(0, bm // ch, chunk, (zero, zero))

            @pl.when(i == 0)
            def _():
                dg_ref[...] = dgp
                db_ref[...] = dbp

            @pl.when(i > 0)
            def _():
                dg_ref[...] += dgp
                db_ref[...] += dbp

    row = pl.BlockSpec((bm, d), lambda i, j: (i, 0))
    vec = pl.BlockSpec((1, d), lambda i, j: (0, 0))
    return _call(
        body, name=name,
        out_shape=(jax.ShapeDtypeStruct((s, d), F32), jax.ShapeDtypeStruct((s, d), BF16),
                   jax.ShapeDtypeStruct((1, d), F32), jax.ShapeDtypeStruct((1, d), F32)),
        grid=(s // bm, nj),
        in_specs=[pl.BlockSpec((bm, k), lambda i, j: (i, 0)),
                  pl.BlockSpec((bn, k), lambda i, j: (j, 0)) if tb else pl.BlockSpec((k, bn), lambda i, j: (0, j)),
                  row, row, pl.BlockSpec((bm, 1), lambda i, j: (i, 0)), vec],
        out_specs=(row, row, vec, vec),
        scratch_shapes=[pltpu.VMEM((nj, bm, bn), F32)],
        args=(a, wt, extra, xhat, rstd, gamma), vmem_mib=58, comm=comm)


def _ffn_dx(dg, du, wgt, wut, extra, *, extra_scale, bm, bn, name, comm=None):
    s, f = dg.shape
    d = wgt.shape[1]
    bm, bn = min(bm, s), min(bn, d)
    assert s % bm == 0 and d % bn == 0

    def body(dg_ref, du_ref, wg_ref, wu_ref, e_ref, o_ref):
        acc = jnp.dot(dg_ref[...], wg_ref[...], preferred_element_type=F32)
        acc = acc + jnp.dot(du_ref[...], wu_ref[...], preferred_element_type=F32)
        o_ref[...] = extra_scale * e_ref[...] + acc

    rows = pl.BlockSpec((bm, f), lambda i, j: (i, 0))
    cols = pl.BlockSpec((f, bn), lambda i, j: (0, j))
    blk = pl.BlockSpec((bm, bn), lambda i, j: (i, j))
    return _call(
        body, name=name, out_shape=jax.ShapeDtypeStruct((s, d), F32),
        grid=(s // bm, d // bn), in_specs=[rows, rows, cols, cols, blk], out_specs=blk,
        args=(dg, du, wgt, wut, extra), vmem_mib=58, comm=comm)


def _ln_bwd(dx, xhat, rstd, gamma, *, bm, name):
    s, d = dx.shape
    bm = min(bm, s)
    assert s % bm == 0
    ch = min(EPILOGUE_ROWS, bm)

    def body(dx_ref, xh_ref, rs_ref, g_ref, dz_ref, dzb_ref, dg_ref, db_ref):
        def chunk(ci, carry):
            dgp, dbp = carry
            rows = pl.ds(pl.multiple_of(ci * ch, ch), ch)
            dxv = dx_ref[rows, :]
            xh = xh_ref[rows, :]
            dxh = dxv * g_ref[...]
            m1 = jnp.mean(dxh, axis=-1, keepdims=True)
            m2 = jnp.mean(dxh * xh, axis=-1, keepdims=True)
            dz = rs_ref[rows, :] * (dxh - m1 - xh * m2)
            dz_ref[rows, :] = dz
            dzb_ref[rows, :] = dz.astype(BF16)
            return dgp + jnp.sum(dxv * xh, axis=0, keepdims=True), dbp + jnp.sum(dxv, axis=0, keepdims=True)

        zero = jnp.zeros((1, d), F32)
        dgp, dbp = lax.fori_loop(0, bm // ch, chunk, (zero, zero))
        i = pl.program_id(0)

        @pl.when(i == 0)
        def _():
            dg_ref[...] = dgp
            db_ref[...] = dbp

        @pl.when(i > 0)
        def _():
            dg_ref[...] += dgp
            db_ref[...] += dbp

    row = pl.BlockSpec((bm, d), lambda i: (i, 0))
    vec = pl.BlockSpec((1, d), lambda i: (0, 0))
    return _call(
        body, name=name,
        out_shape=(jax.ShapeDtypeStruct((s, d), F32), jax.ShapeDtypeStruct((s, d), BF16),
                   jax.ShapeDtypeStruct((1, d), F32), jax.ShapeDtypeStruct((1, d), F32)),
        grid=(s // bm,), in_specs=[row, row, pl.BlockSpec((bm, 1), lambda i: (i, 0)), vec],
        out_specs=(row, row, vec, vec), args=(dx, xhat, rstd, gamma), vmem_mib=48)


def _ple_loss(x3, x3b, p, wpg, wpp, target, *, bm, bn, name):
    s, d = x3.shape
    dp = p.shape[1]
    bm, bn = min(bm, s), min(bn, d)
    assert s % bm == 0 and d % bn == 0
    inv_d = 1.0 / d
    chunks = 4 if bm % 64 == 0 else 1
    cr = bm // chunks

    def body(x_ref, xb_ref, p_ref, wg_ref, wp_ref, t_ref, l_ref, dy_ref, dg_ref, dp_ref):
        first = (pl.program_id(0) == 0) & (pl.program_id(1) == 0)

        @pl.when(first)
        def _():
            l_ref[...] = jnp.zeros_like(l_ref)

        part = 0.0
        for r in range(chunks):
            rows = slice(r * cr, (r + 1) * cr)
            gp = jnp.dot(xb_ref[rows, :], wg_ref[...], preferred_element_type=F32)
            pp = lax.dot_general(p_ref[rows, :].astype(BF16), wp_ref[...], NT_DIMS, preferred_element_type=F32)
            sig = jax.nn.sigmoid(gp)
            err = x_ref[rows, :] + sig * pp - t_ref[rows, :]
            part = part + jnp.sum(err * err)
            dy = err * inv_d
            dy_ref[rows, :] = dy
            dg_ref[rows, :] = (dy * pp * sig * (1.0 - sig)).astype(BF16)
            dp_ref[rows, :] = (dy * sig).astype(BF16)
        l_ref[...] += part

    blk = pl.BlockSpec((bm, bn), lambda i, j: (i, j))
    return pl.pallas_call(
        body, name=name,
        out_shape=(jax.ShapeDtypeStruct((8, LANES), F32), jax.ShapeDtypeStruct((s, d), F32),
                   jax.ShapeDtypeStruct((s, d), BF16), jax.ShapeDtypeStruct((s, d), BF16)),
        grid=(s // bm, d // bn),
        in_specs=[blk, pl.BlockSpec((bm, d), lambda i, j: (i, 0)), pl.BlockSpec((bm, dp), lambda i, j: (i, 0)),
                  pl.BlockSpec((d, bn), lambda i, j: (0, j)), pl.BlockSpec((bn, dp), lambda i, j: (j, 0)), blk],
        out_specs=(pl.BlockSpec((8, LANES), lambda i, j: (0, 0)), blk, blk, blk),
        compiler_params=_cp(("arbitrary", "arbitrary"), 56),
    )(x3, x3b, p, wpg, wpp, target)


def _rope_tables(positions):
    half = ROT_DIMS // 2
    inv_freq = jnp.power(jnp.float32(ROPE_THETA), -jnp.arange(half, dtype=F32) * (2.0 / ROT_DIMS))
    ang = positions.astype(F32)[:, None] * inv_freq
    cos, sin = jnp.cos(ang), jnp.sin(ang)
    s = positions.shape[0]
    zeros = jnp.zeros((s, half), F32)
    rest0 = jnp.zeros((s, HEAD_DIM - ROT_DIMS), F32)
    cf = jnp.concatenate([cos, cos, jnp.ones((s, HEAD_DIM - ROT_DIMS), F32)], axis=1)
    sa = jnp.concatenate([-sin, zeros, rest0], axis=1)
    sb = jnp.concatenate([zeros, sin, rest0], axis=1)
    return cf, sa, sb


def _rotary(t, tabs, *, n_cols, inverse, out_dtype, bs, name):
    s = t.shape[0]
    bs = min(bs, s)
    half = ROT_DIMS // 2
    heads = N_KV_HEADS
    assert n_cols % heads == 0

    def body(t_ref, cf_ref, sa_ref, sb_ref, o_ref):
        cf, sa, sb = cf_ref[...], sa_ref[...], sb_ref[...]
        for hd in range(heads):
            lanes = slice(hd * HEAD_DIM, (hd + 1) * HEAD_DIM)
            v = t_ref[:, lanes]
            if inverse:
                o = v * cf + pltpu.roll(v * sa, half, 1) + pltpu.roll(v * sb, HEAD_DIM - half, 1)
            else:
                o = v * cf + pltpu.roll(v, HEAD_DIM - half, 1) * sa + pltpu.roll(v, half, 1) * sb
            o_ref[:, lanes] = o.astype(out_dtype)

    blk = pl.BlockSpec((bs, heads * HEAD_DIM), lambda i, j: (i, j))
    tab = pl.BlockSpec((bs, HEAD_DIM), lambda i, j: (i, 0))
    return pl.pallas_call(
        body, name=name, out_shape=jax.ShapeDtypeStruct((s, n_cols * HEAD_DIM), out_dtype),
        grid=(s // bs, n_cols // heads), in_specs=[blk, tab, tab, tab], out_specs=blk,
        compiler_params=_cp(("parallel", "arbitrary"), 32),
    )(t, *tabs)


def _attn_blocks():
    out = []
    for g, dil in enumerate(DILATIONS):
        sup = SPAN * dil
        for j in range(ATTN_TILE // sup):
            for r in range(dil):
                out.append((g, j * sup + r, dil, (j - 1) * sup + r if j > 0 else None, ATTN_TILE - sup + r))
    return out


def _rows(ref, start, dil, lead=None):
    idx = pl.ds(start, SPAN, stride=dil) if dil > 1 else pl.ds(start, SPAN)
    return ref[idx, :] if lead is None else ref[lead, idx, :]


def _band_masks(n):
    qi = lax.broadcasted_iota(jnp.int32, (SPAN, 2 * SPAN), 0)
    ki = lax.broadcasted_iota(jnp.int32, (SPAN, 2 * SPAN), 1)
    band = (ki >= qi) & (ki <= qi + SPAN)
    return band, band & ((ki >= SPAN) | (n > 0))


def _attn_fwd(qkr, proj, *, name):
    s = qkr.shape[0]
    t = ATTN_TILE
    assert s % t == 0
    nt = s // t
    scale = HEAD_DIM ** -0.5
    kcol, vcol = N_PATTERNS * N_KV_HEADS, (N_PATTERNS + 1) * N_KV_HEADS
    blocks = _attn_blocks()

    def body(q0, q1, q2, kc_ref, kp_ref, vc_ref, vp_ref, o_ref, l_ref, og, lg):
        n = pl.program_id(1)
        band, band_first = _band_masks(n)
        q_refs = (q0, q1, q2)
        for g, start, dil, prev_in_tile, prev_start in blocks:
            q = _rows(q_refs[g], start, dil).astype(BF16)
            if prev_in_tile is not None:
                kp, vp, mask = _rows(kc_ref, prev_in_tile, dil), _rows(vc_ref, prev_in_tile, dil), band
            else:
                kp, vp, mask = _rows(kp_ref, prev_start, dil), _rows(vp_ref, prev_start, dil), band_first
            kk = jnp.concatenate([kp, _rows(kc_ref, start, dil)], axis=0).astype(BF16)
            vv = jnp.concatenate([vp, _rows(vc_ref, start, dil)], axis=0).astype(BF16)
            sc = lax.dot_general(q, kk, (((1,), (1,)), ((), ())), preferred_element_type=F32) * scale
            sc = jnp.where(mask, sc, -1e30)
            m = jnp.max(sc, axis=-1, keepdims=True)
            e = jnp.exp(sc - m)
            den = jnp.sum(e, axis=-1, keepdims=True)
            o = jnp.dot(e.astype(BF16), vv, preferred_element_type=F32) / den
            idx = pl.ds(start, SPAN, stride=dil) if dil > 1 else pl.ds(start, SPAN)
            og[g, idx, :] = o
            lg[g, idx, :] = jnp.broadcast_to(m + jnp.log(den), (SPAN, HEAD_DIM))
        l0, l1, l2 = lg[0], lg[1], lg[2]
        m = jnp.maximum(jnp.maximum(l0, l1), l2)
        w0, w1, w2 = jnp.exp(l0 - m), jnp.exp(l1 - m), jnp.exp(l2 - m)
        den = w0 + w1 + w2
        o_ref[...] = (w0 * og[0] + w1 * og[1] + w2 * og[2]) / den
        l_ref[...] = m + jnp.log(den)

    def col(c, prev=False):
        if prev:
            return pl.BlockSpec((t, HEAD_DIM), lambda h, n: (jnp.maximum(n - 1, 0), c + h))
        return pl.BlockSpec((t, HEAD_DIM), lambda h, n: (n, c + h))

    out = jax.ShapeDtypeStruct((s, N_KV_HEADS * HEAD_DIM), F32)
    return pl.pallas_call(
        body, name=name, out_shape=(out, out),
        grid=(N_KV_HEADS, nt),
        in_specs=[col(0), col(N_KV_HEADS), col(2 * N_KV_HEADS), col(kcol), col(kcol, True), col(vcol), col(vcol, True)],
        out_specs=(col(0), col(0)),
        scratch_shapes=[pltpu.VMEM((N_PATTERNS, t, HEAD_DIM), F32), pltpu.VMEM((N_PATTERNS, t, HEAD_DIM), F32)],
        compiler_params=_cp(("parallel", "arbitrary"), 48),
    )(qkr, qkr, qkr, qkr, qkr, proj, proj)


def _attn_bwd(qkr, proj, attn, lse, dcat, *, name, comm=None):
    s = qkr.shape[0]
    t = ATTN_TILE
    nt = s // t
    scale = HEAD_DIM ** -0.5
    kcol, vcol = N_PATTERNS * N_KV_HEADS, (N_PATTERNS + 1) * N_KV_HEADS
    blocks = _attn_blocks()

    def body(q0, q1, q2, kc_ref, kp_ref, vc_ref, vp_ref, o_ref, l_ref, do_ref,
             dq0, dq1, dq2, dk_ref, dv_ref, ck, cv, tkc, tvc, tkp, tvp):
        n = pl.program_id(1)
        for ref in (tkc, tvc, tkp, tvp):
            ref[...] = jnp.zeros_like(ref)

        @pl.when(n < nt)
        def _():
            band, band_first = _band_masks(n)
            q_refs, dq_refs = (q0, q1, q2), (dq0, dq1, dq2)
            for g, start, dil, prev_in_tile, prev_start in blocks:
                idx = pl.ds(start, SPAN, stride=dil) if dil > 1 else pl.ds(start, SPAN)
                q = q_refs[g][idx, :].astype(BF16)
                if prev_in_tile is not None:
                    kp, vp, mask = _rows(kc_ref, prev_in_tile, dil), _rows(vc_ref, prev_in_tile, dil), band
                else:
                    kp, vp, mask = _rows(kp_ref, prev_start, dil), _rows(vp_ref, prev_start, dil), band_first
                kk = jnp.concatenate([kp, kc_ref[idx, :]], axis=0).astype(BF16)
                vv = jnp.concatenate([vp, vc_ref[idx, :]], axis=0).astype(BF16)
                do = do_ref[idx, :]
                dsum = jnp.sum(do * o_ref[idx, :], axis=-1, keepdims=True)
                lrow = l_ref[idx, :][:, :1]
                dob = do.astype(BF16)
                sc = lax.dot_general(q, kk, (((1,), (1,)), ((), ())), preferred_element_type=F32) * scale
                p = jnp.where(mask, jnp.exp(sc - lrow), 0.0)
                dp = lax.dot_general(dob, vv, (((1,), (1,)), ((), ())), preferred_element_type=F32)
                ds = (p * (dp - dsum) * scale).astype(BF16)
                pb = p.astype(BF16)
                dq_refs[g][idx, :] = jnp.dot(ds, kk, preferred_element_type=F32)
                dkk = lax.dot_general(ds, q, (((0,), (0,)), ((), ())), preferred_element_type=F32)
                dvv = lax.dot_general(pb, dob, (((0,), (0,)), ((), ())), preferred_element_type=F32)
                tkc[idx, :] += dkk[SPAN:]
                tvc[idx, :] += dvv[SPAN:]
                if prev_in_tile is not None:
                    pidx = pl.ds(prev_in_tile, SPAN, stride=dil) if dil > 1 else pl.ds(prev_in_tile, SPAN)
                    tkc[pidx, :] += dkk[:SPAN]
                    tvc[pidx, :] += dvv[:SPAN]
                else:
                    pidx = pl.ds(prev_start, SPAN, stride=dil) if dil > 1 else pl.ds(prev_start, SPAN)
                    tkp[pidx, :] += dkk[:SPAN]
                    tvp[pidx, :] += dvv[:SPAN]

        @pl.when(n > 0)
        def _():
            dk_ref[...] = ck[...] + tkp[...]
            dv_ref[...] = (cv[...] + tvp[...]).astype(BF16)

        ck[...] = tkc[...]
        cv[...] = tvc[...]

    def col(c, prev=False):
        if prev:
            return pl.BlockSpec((t, HEAD_DIM), lambda h, n: (jnp.maximum(jnp.minimum(n, nt - 1) - 1, 0), c + h))
        return pl.BlockSpec((t, HEAD_DIM), lambda h, n: (jnp.minimum(n, nt - 1), c + h))

    kv_out = pl.BlockSpec((t, HEAD_DIM), lambda h, n: (jnp.maximum(n - 1, 0), h))
    tile = pltpu.VMEM((t, HEAD_DIM), F32)
    per_head = jax.ShapeDtypeStruct((s, N_KV_HEADS * HEAD_DIM), F32)
    return _call(
        body, name=name,
        out_shape=(per_head, per_head, per_head, per_head, jax.ShapeDtypeStruct((s, N_KV_HEADS * HEAD_DIM), BF16)),
        grid=(N_KV_HEADS, nt + 1),
        in_specs=[col(0), col(N_KV_HEADS), col(2 * N_KV_HEADS), col(kcol), col(kcol, True), col(vcol), col(vcol, True),
                  col(0), col(0), col(0)],
        out_specs=(col(0), col(0), col(0), kv_out, kv_out),
        scratch_shapes=[tile] * 6,
        args=(qkr, qkr, qkr, qkr, qkr, proj, proj, attn, lse, dcat), vmem_mib=48, comm=comm)


GELU_C0 = 0.7978845608028654
GELU_C1 = 0.044715


def _softplus_neg(lam):
    y = jnp.exp(-jnp.abs(lam))
    w = 1.0 + y
    log1p = jnp.where(w == 1.0, y, jnp.log(w) * (y / jnp.where(w == 1.0, 1.0, w - 1.0)))
    return jnp.maximum(-lam, 0.0) + log1p


def _down(cur, prev, k, row):
    if k == 0:
        return cur
    return jnp.where(row < k, pltpu.roll(prev, k, 0), pltpu.roll(cur, k, 0))


def _up(cur, nxt, k, row, tt):
    if k == 0:
        return cur
    return jnp.where(row >= tt - k, pltpu.roll(nxt, tt - k, 0), pltpu.roll(cur, tt - k, 0))


def _lru_gates(x, xp, cw, cb, wr, br, wi, bi, lam, row):
    shifts = [_down(x, xp, k, row) for k in range(CONV_WIDTH)]
    xc = cb
    for j in range(CONV_WIDTH):
        xc = xc + cw[j:j + 1, :] * shifts[CONV_WIDTH - 1 - j]
    xcb = xc.astype(BF16)
    r = jax.nn.sigmoid(jnp.dot(xcb, wr, preferred_element_type=F32) + br)
    i = jax.nn.sigmoid(jnp.dot(xcb, wi, preferred_element_type=F32) + bi)
    c = -LRU_C * _softplus_neg(lam)
    la = c * r
    a = jnp.exp(la)
    mult = jnp.sqrt(jnp.tanh(-la) * (a * a + 1.0))
    return shifts, xc, xcb, r, i, c, a, mult


def _lru_fwd(proj, cw, cb, wr, br, wi, bi, lam, *, tt, name):
    s = proj.shape[0]
    nblk = wr.shape[0]
    c = nblk * LANES
    tt = min(tt, s)
    xcol0 = (N_PATTERNS + 2) * N_KV_HEADS
    ycol0 = xcol0 + nblk

    def body(x_ref, y_ref, cw_ref, cb_ref, wr_ref, br_ref, wi_ref, bi_ref, lam_ref, rec_ref, h_ref, xprev, hc):
        n = pl.program_id(1)

        @pl.when(n == 0)
        def _():
            xprev[...] = jnp.zeros_like(xprev)
            hc[...] = jnp.zeros_like(hc)

        row = lax.broadcasted_iota(jnp.int32, (tt, LANES), 0)
        x = x_ref[...]
        _, xc, _, _, i, _, a, mult = _lru_gates(
            x, xprev[...], cw_ref[...], cb_ref[...], wr_ref[0].astype(BF16), br_ref[...],
            wi_ref[0].astype(BF16), bi_ref[...], lam_ref[...], row)
        av, bv = a, mult * (i * xc)
        k = 1
        while k < tt:
            bs = jnp.where(row < k, 0.0, pltpu.roll(bv, k, 0))
            as_ = jnp.where(row < k, 1.0, pltpu.roll(av, k, 0))
            bv = bv + av * bs
            av = av * as_
            k *= 2
        h = bv + av * hc[0:1, :]
        hc[...] = jnp.broadcast_to(h[tt - 1:tt, :], hc.shape)
        h_ref[...] = h
        y = y_ref[...]
        gel = 0.5 * y * (1.0 + jnp.tanh(GELU_C0 * (y + GELU_C1 * y * y * y)))
        rec_ref[...] = (h * gel).astype(BF16)
        xprev[...] = x

    vec = pl.BlockSpec((1, LANES), lambda b, n: (0, b))
    wblk = pl.BlockSpec((1, LANES, LANES), lambda b, n: (b, 0, 0))
    out = pl.BlockSpec((tt, LANES), lambda b, n: (n, b))
    return pl.pallas_call(
        body, name=name,
        out_shape=(jax.ShapeDtypeStruct((s, c), BF16), jax.ShapeDtypeStruct((s, c), F32)),
        grid=(nblk, s // tt),
        in_specs=[pl.BlockSpec((tt, LANES), lambda b, n: (n, xcol0 + b)),
                  pl.BlockSpec((tt, LANES), lambda b, n: (n, ycol0 + b)),
                  pl.BlockSpec((CONV_WIDTH, LANES), lambda b, n: (0, b)), vec, wblk, vec, wblk, vec, vec],
        out_specs=(out, out),
        scratch_shapes=[pltpu.VMEM((tt, LANES), F32), pltpu.VMEM((8, LANES), F32)],
        compiler_params=_cp(("parallel", "arbitrary"), 32),
    )(proj, proj, cw, cb, wr, br, wi, bi, lam)


def _lru_bwd(proj, hseq, dcat, cw, cb, wr, br, wi, bi, lam, *, tt, name, comm=None):
    s = proj.shape[0]
    nblk = wr.shape[0]
    c = nblk * LANES
    tt = min(tt, s)
    nt = s // tt
    xcol0 = (N_PATTERNS + 2) * N_KV_HEADS
    ycol0 = xcol0 + nblk
    rcol0 = N_KV_HEADS

    def body(x_ref, xp_ref, y_ref, h_ref, hp_ref, dr_ref, cw_ref, cb_ref, wr_ref, br_ref, wi_ref, bi_ref, lam_ref,
             dx_ref, dy_ref, dcw_ref, dcb_ref, dwr_ref, dbr_ref, dwi_ref, dbi_ref, dlam_ref, dxc_next, gcar, acar):
        n = pl.program_id(1)
        rt = nt - 1 - n

        @pl.when(n == 0)
        def _():
            for ref in (dxc_next, gcar, acar, dcw_ref, dcb_ref, dwr_ref, dbr_ref, dwi_ref, dbi_ref, dlam_ref):
                ref[...] = jnp.zeros_like(ref)

        row = lax.broadcasted_iota(jnp.int32, (tt, LANES), 0)
        x = x_ref[...]
        xp = jnp.where(rt > 0, xp_ref[...], 0.0)
        cwv = cw_ref[...]
        wrb, wib = wr_ref[0].astype(BF16), wi_ref[0].astype(BF16)
        lam_v = lam_ref[...]
        shifts, xc, xcb, r, i, cc, a, mult = _lru_gates(x, xp, cwv, cb_ref[...], wrb, br_ref[...], wib, bi_ref[...],
                                                        lam_v, row)
        h = h_ref[...]
        hp_last = jnp.where(rt > 0, hp_ref[7:8, :], 0.0)
        hprev = jnp.where(row < 1, hp_last, pltpu.roll(h, 1, 0))
        y = y_ref[...]
        y2 = y * y
        th = jnp.tanh(GELU_C0 * (y + GELU_C1 * y2 * y))
        gel = 0.5 * y * (1.0 + th)
        dgel = 0.5 * (1.0 + th) + 0.5 * y * (1.0 - th * th) * GELU_C0 * (1.0 + 3.0 * GELU_C1 * y2)
        drec = dr_ref[...]
        dy_ref[...] = (drec * h * dgel).astype(BF16)
        av = jnp.where(row >= tt - 1, acar[0:1, :], pltpu.roll(a, tt - 1, 0))
        bv = drec * gel
        k = 1
        while k < tt:
            bs = jnp.where(row >= tt - k, 0.0, pltpu.roll(bv, tt - k, 0))
            as_ = jnp.where(row >= tt - k, 1.0, pltpu.roll(av, tt - k, 0))
            bv = bv + av * bs
            av = av * as_
            k *= 2
        g = bv + av * gcar[0:1, :]
        gcar[...] = jnp.broadcast_to(g[0:1, :], gcar.shape)
        acar[...] = jnp.broadcast_to(a[0:1, :], acar.shape)
        da = g * hprev
        d_ixc = g * mult
        dmult = g * (i * xc)
        di = d_ixc * xc
        dxc = d_ixc * i
        a2 = a * a
        dla = da * a - dmult * (a2 / mult)
        dr = dla * cc
        dsp = jnp.sum(dla * r, axis=0, keepdims=True) * (-LRU_C)
        dlam_ref[...] += dsp * (-jax.nn.sigmoid(-lam_v))
        dzr = dr * r * (1.0 - r)
        dzi = di * i * (1.0 - i)
        dbr_ref[...] += jnp.sum(dzr, axis=0, keepdims=True)
        dbi_ref[...] += jnp.sum(dzi, axis=0, keepdims=True)
        dzrb, dzib = dzr.astype(BF16), dzi.astype(BF16)
        tn = (((0,), (0,)), ((), ()))
        ntd = (((1,), (1,)), ((), ()))
        dwr_ref[0] += lax.dot_general(xcb, dzrb, tn, preferred_element_type=F32)
        dwi_ref[0] += lax.dot_general(xcb, dzib, tn, preferred_element_type=F32)
        dxc = (dxc + lax.dot_general(dzrb, wrb, ntd, preferred_element_type=F32)
               + lax.dot_general(dzib, wib, ntd, preferred_element_type=F32))
        dcb_ref[...] += jnp.sum(dxc, axis=0, keepdims=True)
        dcw_ref[...] += jnp.concatenate(
            [jnp.sum(dxc * shifts[CONV_WIDTH - 1 - j], axis=0, keepdims=True) for j in range(CONV_WIDTH)], axis=0)
        nxt = dxc_next[...]
        dx = cwv[0:1, :] * _up(dxc, nxt, CONV_WIDTH - 1, row, tt)
        for j in range(1, CONV_WIDTH):
            dx = dx + cwv[j:j + 1, :] * _up(dxc, nxt, CONV_WIDTH - 1 - j, row, tt)
        dx_ref[...] = dx.astype(BF16)
        dxc_next[...] = dxc

    def tile(col0, prev=False):
        if prev:
            return pl.BlockSpec((tt, LANES), lambda b, n: (jnp.maximum(nt - 2 - n, 0), col0 + b))
        return pl.BlockSpec((tt, LANES), lambda b, n: (nt - 1 - n, col0 + b))

    vec = pl.BlockSpec((1, LANES), lambda b, n: (0, b))
    wblk = pl.BlockSpec((1, LANES, LANES), lambda b, n: (b, 0, 0))
    cwblk = pl.BlockSpec((CONV_WIDTH, LANES), lambda b, n: (0, b))
    hp8 = pl.BlockSpec((8, LANES), lambda b, n: (jnp.maximum((nt - 1 - n) * (tt // 8) - 1, 0), b))
    vshape = jax.ShapeDtypeStruct((1, c), F32)
    wshape = jax.ShapeDtypeStruct((nblk, LANES, LANES), F32)
    return _call(
        body, name=name,
        out_shape=(jax.ShapeDtypeStruct((s, c), BF16), jax.ShapeDtypeStruct((s, c), BF16),
                   jax.ShapeDtypeStruct((CONV_WIDTH, c), F32), vshape, wshape, vshape, wshape, vshape, vshape),
        grid=(nblk, nt),
        in_specs=[tile(xcol0), tile(xcol0, True), tile(ycol0), tile(0), hp8, tile(rcol0),
                  cwblk, vec, wblk, vec, wblk, vec, vec],
        out_specs=(tile(0), tile(0), cwblk, vec, wblk, vec, wblk, vec, vec),
        scratch_shapes=[pltpu.VMEM((tt, LANES), F32), pltpu.VMEM((8, LANES), F32), pltpu.VMEM((8, LANES), F32)],
        args=(proj, proj, proj, hseq, hseq, dcat, cw, cb, wr, br, wi, bi, lam), vmem_mib=32, comm=comm)


ROW_BLOCKS = (256, 128, 64, 32, 16, 8)


def _adamw(w, m, v, gparts, *, name):
    r, c = w.shape
    npart = gparts.shape[0]
    br = _pick(r, ROW_BLOCKS)
    c1 = 1.0 - ADAM_B1 ** ADAM_STEP
    c2 = 1.0 - ADAM_B2 ** ADAM_STEP

    def body(w_ref, m_ref, v_ref, g_ref, go_ref, d_ref, mo_ref, vo_ref):
        g = g_ref[0].astype(F32)
        for q in range(1, npart):
            g = g + g_ref[q].astype(F32)
        mn = ADAM_B1 * m_ref[...] + (1.0 - ADAM_B1) * g
        vn = ADAM_B2 * v_ref[...] + (1.0 - ADAM_B2) * (g * g)
        go_ref[...] = g
        mo_ref[...] = mn
        vo_ref[...] = vn
        d_ref[...] = -ADAM_LR * ((mn / c1) / (jnp.sqrt(vn / c2) + ADAM_EPS) + ADAM_WD * w_ref[...])

    blk = pl.BlockSpec((br, c), lambda i: (i, 0))
    out = jax.ShapeDtypeStruct((r, c), F32)
    return pl.pallas_call(
        body, name=name, out_shape=(out, out, out, out), grid=(r // br,),
        in_specs=[blk, blk, blk, pl.BlockSpec((npart, br, c), lambda i: (0, i, 0))],
        out_specs=(blk, blk, blk, blk),
        compiler_params=_cp(("parallel",), 48),
    )(w, m, v, gparts)


def _sum_parts(parts, *, name):
    npart, r, c = parts.shape
    br = next((b for b in range(min(r, 2048) // 8 * 8, 0, -8) if r % b == 0), r)

    def body(p_ref, o_ref):
        acc = p_ref[0]
        for q in range(1, npart):
            acc = acc + p_ref[q]
        o_ref[...] = acc

    return pl.pallas_call(
        body, name=name, out_shape=jax.ShapeDtypeStruct((r, c), F32), grid=(r // br,),
        in_specs=[pl.BlockSpec((npart, br, c), lambda i: (0, i, 0))],
        out_specs=pl.BlockSpec((br, c), lambda i: (i, 0)),
        compiler_params=_cp(("parallel",), 48),
    )(parts)


HBM = pl.BlockSpec(memory_space=pltpu.HBM)


def _mesh_pos():
    return lax.axis_index("x"), lax.axis_index("y"), lax.axis_index("c")


def _all_gather(shards, *, name):
    comm = _gather_comm(shards)
    na = len(shards)

    def body(*refs):
        ins, outs, sems = refs[:na], refs[na:2 * na], refs[2 * na:]
        comm.start(ins, outs, sems)
        comm.mid(ins, outs, sems)
        comm.end(ins, outs, sems)

    return pl.pallas_call(
        body, name=name, out_shape=tuple(comm.out_shapes),
        in_specs=[HBM] * na, out_specs=tuple([HBM] * na), scratch_shapes=comm.scratch,
    )(*shards)


def _gather_comm(shards):
    na = len(shards)

    def parts(x_refs, out_refs, sems):
        send_sems, recv_sems, local_sems = sems
        x, y, c = _mesh_pos()
        me, sibling = (x, y, c), (x, y, 1 - c)
        chips = [(1 - x, y), (x, 1 - y), (1 - x, 1 - y)]

        def copy(a, k, block, to, src=None):
            px, py, pc = block
            dst = out_refs[a].at[4 * px + 2 * py + pc]
            return pltpu.make_async_remote_copy(
                src_ref=dst if src is None else src, dst_ref=dst,
                send_sem=send_sems.at[a, k], recv_sem=recv_sems.at[a, k],
                device_id=to, device_id_type=MESH)

        def mine(a):
            return pltpu.make_async_copy(x_refs[a], out_refs[a].at[4 * x + 2 * y + c], local_sems.at[a])

        def first(a):
            return [copy(a, 0, me, sibling, src=x_refs[a])] + [
                copy(a, 1 + j, me, (*chip, c), src=x_refs[a]) for j, chip in enumerate(chips)]

        def passed(a, j):
            return copy(a, 4 + j, (*chips[j], c), sibling)

        return me, sibling, chips, c, copy, mine, first, passed

    def start(x_refs, out_refs, sems):
        *_, mine, first, _ = parts(x_refs, out_refs, sems)
        for a in range(na):
            mine(a).start()
            for cp in first(a):
                cp.start()

    def mid(x_refs, out_refs, sems):
        me, _, chips, c, copy, _, _, passed = parts(x_refs, out_refs, sems)
        for j, chip in enumerate(chips):
            for a in range(na):
                copy(a, 1 + j, (*chip, c), me).wait_recv()
                passed(a, j).start()

    def end(x_refs, out_refs, sems):
        me, sibling, chips, c, copy, mine, first, passed = parts(x_refs, out_refs, sems)
        for a in range(na):
            copy(a, 0, sibling, me).wait_recv()
            for j, chip in enumerate(chips):
                copy(a, 4 + j, (*chip, 1 - c), me).wait_recv()
        for a in range(na):
            for cp in first(a) + [passed(a, j) for j in range(3)]:
                cp.wait_send()
            mine(a).wait()

    return _Comm(
        shards, [jax.ShapeDtypeStruct((N_DEV,) + a.shape, a.dtype) for a in shards],
        [pltpu.SemaphoreType.DMA((na, 7)), pltpu.SemaphoreType.DMA((na, 7)), pltpu.SemaphoreType.DMA((na,))],
        start, end, mid)


def _scatter_comm(g8s):
    na = len(g8s)

    def parts(g_refs, buf_refs, sems):
        send_sems, recv_sems, local_sems = sems
        x, y, c = _mesh_pos()
        me_idx = 4 * x + 2 * y + c

        def peer(k):
            return (1 - x if k & 4 else x, 1 - y if k & 2 else y, 1 - c if k & 1 else c)

        def copy(a, k, slot):
            px, py, pc = peer(k)
            return pltpu.make_async_remote_copy(
                src_ref=g_refs[a].at[4 * px + 2 * py + pc], dst_ref=buf_refs[a].at[slot],
                send_sem=send_sems.at[a, k - 1], recv_sem=recv_sems.at[a, k - 1],
                device_id=(px, py, pc), device_id_type=MESH)

        def mine(a):
            return pltpu.make_async_copy(g_refs[a].at[me_idx], buf_refs[a].at[me_idx], local_sems.at[a])

        return me_idx, peer, copy, mine

    def start(g_refs, buf_refs, sems):
        me_idx, _, copy, mine = parts(g_refs, buf_refs, sems)
        for a in range(na):
            mine(a).start()
            for k in range(1, N_DEV):
                copy(a, k, me_idx).start()

    def end(g_refs, buf_refs, sems):
        me_idx, peer, copy, mine = parts(g_refs, buf_refs, sems)
        for a in range(na):
            for k in range(1, N_DEV):
                px, py, pc = peer(k)
                copy(a, k, 4 * px + 2 * py + pc).wait_recv()
        for a in range(na):
            for k in range(1, N_DEV):
                copy(a, k, me_idx).wait_send()
            mine(a).wait()

    return _Comm(
        g8s, [jax.ShapeDtypeStruct(g.shape, g.dtype) for g in g8s],
        [pltpu.SemaphoreType.DMA((na, N_DEV - 1)), pltpu.SemaphoreType.DMA((na, N_DEV - 1)),
         pltpu.SemaphoreType.DMA((na,))],
        start, end)


def _join_comm(c1, c2):
    n1i, n1o, n1s = len(c1.arrays), len(c1.out_shapes), len(c1.scratch)

    def both(f1, f2):
        def run(ins, outs, sems):
            if f1 is not None:
                f1(ins[:n1i], outs[:n1o], sems[:n1s])
            if f2 is not None:
                f2(ins[n1i:], outs[n1o:], sems[n1s:])
        return run

    mid = both(c1.mid, c2.mid) if (c1.mid is not None or c2.mid is not None) else None
    return _Comm(c1.arrays + c2.arrays, c1.out_shapes + c2.out_shapes, c1.scratch + c2.scratch,
                 both(c1.start, c2.start), both(c1.end, c2.end), mid)


BIG_WEIGHTS = ("ffn1_w_gate", "ffn1_w_up", "ffn1_w_down", "w_in", "w_out",
               "ffn2_w_gate", "ffn2_w_up", "ffn2_w_down", "w_ple_proj", "w_ple_gate")
COLUMN_SHARDED = ("ffn1_w_gate", "ffn1_w_up", "w_in", "ffn2_w_gate", "ffn2_w_up", "w_ple_proj", "conv_w")
SMALL_WEIGHTS = ("ln1_g", "ln1_b", "conv_b", "w_rgate", "b_rgate", "w_igate", "b_igate", "lru_lambda",
                 "ln2_g", "ln2_b", "ln3_g", "ln3_b")
SMALL_GRADS = SMALL_WEIGHTS + ("conv_w",)


class _Exchange:
    def __init__(self, full):
        self.full = dict(full)
        self.grads = {}

    def __getitem__(self, name):
        return self.full[name]

    def gather(self, names):
        return None, None

    def scatter(self, names):
        return None, None

    def gather_small(self):
        return None, None


class _MeshExchange(_Exchange):
    def __init__(self, full, shards):
        super().__init__(full)
        self.shards = shards
        self.reduced = {}
        self.small_parts = None

    def gather(self, names):
        def done(outs):
            for n, o in zip(names, outs):
                self.take(n, o)
        return _gather_comm([self.shards[n] for n in names]), done

    def take(self, name, gathered):
        self.full[name] = gathered.reshape((N_DEV * gathered.shape[1],) + gathered.shape[2:])

    def scatter(self, names):
        def done(outs):
            self.reduced.update(zip(names, outs))
        if not names:
            return None, None
        return _scatter_comm([_to_owner_blocks(n, self.grads[n]) for n in names]), done

    def gather_small(self):
        def done(outs):
            self.small_parts, = outs
        packed = jnp.concatenate([_rows128(self.grads[n]) for n in SMALL_GRADS], axis=0)
        return _gather_comm([packed]), done


def _carried(comm_done, call):
    comm, done = comm_done
    res = call(comm)
    if comm is None:
        return res
    res, outs = res
    done(outs)
    return res


def _dw(a, b, *, scale=1.0, name, comm=None):
    k, m = a.shape
    n = b.shape[1]
    return _mm(a, b, ta=True, scale=scale, out_dtype=BF16, bm=_pick(m, (1024, 512, 256, 128)),
               bn=_pick(n, (512, 256, 128)), bk=k, name=name, comm=comm)


def _ffn_bwd(ex, names, saved, xb_in, dz, dzb, ln_in, tag, send_on_dh=(), send_on_dwu=(), send_on_dx=(), also=None):
    gate, up, down = names
    g, u, h, _, _ = saved
    f = ex[gate].shape[0]
    ex.grads[down] = _dw(h, dzb, scale=0.5, name=f"{tag}_dwd")
    dg, du = _carried(ex.scatter(send_on_dh), lambda c: _ffn_bwd_dh(
        dzb, ex[down], g, u, scale=0.5, bm=1024, bn=_pick(f, (512, 256, 128)), name=f"{tag}_dh", chunks=4, comm=c))
    ex.grads[gate] = _dw(xb_in, dg, name=f"{tag}_dwg")
    ex.grads[up] = _carried(ex.scatter(send_on_dwu), lambda c: _dw(xb_in, du, name=f"{tag}_dwu", comm=c))
    last = ex.scatter(send_on_dx)
    if also is not None and also[0] is not None:
        done_up, done_also = last[1], also[1]
        n_up = len(last[0].out_shapes)
        last = (_join_comm(last[0], also[0]), lambda outs: (done_up(outs[:n_up]), done_also(outs[n_up:])))
    d = dz.shape[1]
    dx = _carried(last, lambda c: _ffn_dx(
        dg, du, ex[gate], ex[up], dz, extra_scale=DEEPNORM_ALPHA,
        bm=512, bn=_pick(d, (512, 256, 128)), name=f"{tag}_dx", comm=c))
    return dx if ln_in is None else _ln_bwd(dx, *ln_in, bm=256, name=f"{tag}_ln_bwd")


def _local_step(x, p, target, positions, w):
    s, d = x.shape
    tabs = _rope_tables(positions)
    xb = x.astype(BF16)
    f = w["ffn1_w_gate"].shape[0]
    ffn_bn, ln_bn, ln_bn_short_k = _pick(f, (512, 256, 128)), _pick(d, (256, 128)), _pick(d, (1024, 512, 256, 128))
    g1, u1, h1 = _carried(w.gather(("ffn1_w_down", "w_in", "w_out")), lambda c: _ffn_up(
        xb, w["ffn1_w_gate"], w["ffn1_w_up"], bm=1024, bn=ffn_bn, name="ffn1_up", comm=c))
    x1, x1b, xh1, rs1 = _carried(w.gather(("ffn2_w_gate", "ffn2_w_up")), lambda c: _mm_ln(
        h1, w["ffn1_w_down"], x, w["ln1_g"], w["ln1_b"], res_scale=DEEPNORM_ALPHA, mm_scale=0.5,
        bm=512, bn=ln_bn, name="ffn1_down_ln", comm=c))
    sv1 = (g1, u1, h1, xh1, rs1)
    pw = w["w_in"].shape[0]
    proj = _carried(w.gather(("ffn2_w_down", "w_ple_gate", "w_ple_proj")), lambda c: _mm(
        x1b, w["w_in"], tb=True, bm=1024, bn=_pick(pw, (512, 256, 128)), bk=d, name="in_proj", comm=c))
    nqk = (N_PATTERNS + 1) * N_KV_HEADS
    qkr = _rotary(proj, tabs, n_cols=nqk, inverse=False, out_dtype=F32, bs=1024, name="rotary")
    attn, lse = _attn_fwd(qkr, proj, name="attn_fwd")
    lru_w = (w["conv_w"], w["conv_b"], w["w_rgate"], w["b_rgate"], w["w_igate"], w["b_igate"], w["lru_lambda"])
    rec, hseq = _lru_fwd(proj, *lru_w, tt=512, name="lru_fwd")
    cat = jnp.concatenate([attn.astype(BF16), rec], axis=1)
    x2, x2b, xh2, rs2 = _mm_ln(cat, w["w_out"], x1, w["ln2_g"], w["ln2_b"], res_scale=DEEPNORM_ALPHA, mm_scale=1.0,
                               bm=512, bn=ln_bn_short_k, name="out_proj_ln")
    g2, u2, h2 = _ffn_up(x2b, w["ffn2_w_gate"], w["ffn2_w_up"], bm=1024, bn=ffn_bn, name="ffn2_up")
    x3, x3b, xh3, rs3 = _mm_ln(h2, w["ffn2_w_down"], x2, w["ln3_g"], w["ln3_b"], res_scale=DEEPNORM_ALPHA, mm_scale=0.5,
                               bm=512, bn=ln_bn, name="ffn2_down_ln")
    sv3 = (g2, u2, h2, xh3, rs3)
    lsum, dy, dgate, dple = _ple_loss(x3, x3b, p, w["w_ple_gate"], w["w_ple_proj"], target,
                                      bm=1024, bn=_pick(d, (512, 256, 128)), name="ple_loss")
    grads = w.grads
    grads["w_ple_gate"] = _dw(x3b, dgate, name="dw_ple_gate")
    grads["w_ple_proj"] = _dw(p, dple, name="dw_ple_proj")
    dz3, dz3b, grads["ln3_g"], grads["ln3_b"] = _carried(w.scatter(("w_ple_gate", "w_ple_proj")), lambda c: _mm_dx(
        dgate, w["w_ple_gate"], dy, xh3, rs3, w["ln3_g"], extra_scale=1.0, bm=512, bn=ln_bn_short_k,
        name="ple_dx", tb=True, comm=c))
    dz2, dz2b, grads["ln2_g"], grads["ln2_b"] = _ffn_bwd(
        w, ("ffn2_w_gate", "ffn2_w_up", "ffn2_w_down"), sv3, x2b, dz3, dz3b, (xh2, rs2, w["ln2_g"]), "ffn2",
        send_on_dx=("ffn2_w_down",))
    grads["w_out"] = _dw(cat, dz2b, name="dw_out")
    dcat = _mm(dz2b, w["w_out"], tb=True, bm=1024, bn=_pick(d, (512, 256, 128)), bk=d, name="out_proj_dx")
    dq0, dq1, dq2, dk, dvb = _carried(w.scatter(("ffn2_w_gate",)), lambda c: _attn_bwd(
        qkr, proj, attn, lse, dcat, name="attn_bwd", comm=c))
    nh = N_KV_HEADS
    dqkv = [_rotary(t, tabs, n_cols=nh, inverse=True, out_dtype=BF16, bs=1024, name=f"rotary_bwd{i}")
            for i, t in enumerate((dq0, dq1, dq2, dk))]
    (dxb, dyb, grads["conv_w"], grads["conv_b"], grads["w_rgate"], grads["b_rgate"], grads["w_igate"],
     grads["b_igate"], grads["lru_lambda"]) = _carried(w.scatter(("ffn2_w_up",)), lambda c: _lru_bwd(
         proj, hseq, dcat, *lru_w, tt=512, name="lru_bwd", comm=c))
    dproj = jnp.concatenate(dqkv + [dvb, dxb, dyb], axis=1)
    grads["w_in"] = _carried(w.scatter(("w_out",)), lambda c: _dw(x1b, dproj, name="dw_in", comm=c))
    dz1, dz1b, grads["ln1_g"], grads["ln1_b"] = _carried(w.scatter(("w_in",)), lambda c: _mm_dx(
        dproj, w["w_in"], dz2, xh1, rs1, w["ln1_g"], extra_scale=DEEPNORM_ALPHA,
        bm=512, bn=ln_bn, name="in_proj_dx", comm=c))
    grad_x = _ffn_bwd(w, ("ffn1_w_gate", "ffn1_w_up", "ffn1_w_down"), sv1, xb, dz1, dz1b, None, "ffn1",
                      send_on_dh=("ffn1_w_down",), send_on_dwu=("ffn1_w_gate",), send_on_dx=("ffn1_w_up",),
                      also=w.gather_small())
    return lsum, grad_x


def _to_full(name, gathered):
    if name in COLUMN_SHARDED:
        _, r, c = gathered.shape
        return jnp.transpose(gathered, (1, 0, 2)).reshape(r, N_DEV * c)
    return gathered.reshape((N_DEV * gathered.shape[1],) + gathered.shape[2:])


def _to_owner_blocks(name, full):
    if name in COLUMN_SHARDED:
        r, c = full.shape
        return jnp.transpose(full.reshape(r, N_DEV, c // N_DEV), (1, 0, 2))
    return full.reshape((N_DEV, full.shape[0] // N_DEV) + full.shape[1:])


def _rows128(a):
    flat = a.reshape(-1, LANES)
    pad = (-flat.shape[0]) % 8
    return jnp.pad(flat, ((0, pad), (0, 0))) if pad else flat


def kernel(x, p, positions, ffn1_w_gate, ffn1_w_up, ffn1_w_down, ln1_g, ln1_b, w_in, conv_w, conv_b, w_rgate, b_rgate, w_igate, b_igate, lru_lambda, w_out, ln2_g, ln2_b, ffn2_w_gate, ffn2_w_up, ffn2_w_down, ln3_g, ln3_b, w_ple_proj, w_ple_gate, loss_target, m_ffn1_w_gate, m_ffn1_w_up, m_ffn1_w_down, m_ln1_g, m_ln1_b, m_w_in, m_conv_w, m_conv_b, m_w_rgate, m_b_rgate, m_w_igate, m_b_igate, m_lru_lambda, m_w_out, m_ln2_g, m_ln2_b, m_ffn2_w_gate, m_ffn2_w_up, m_ffn2_w_down, m_ln3_g, m_ln3_b, m_w_ple_proj, m_w_ple_gate, v_ffn1_w_gate, v_ffn1_w_up, v_ffn1_w_down, v_ln1_g, v_ln1_b, v_w_in, v_conv_w, v_conv_b, v_w_rgate, v_b_rgate, v_w_igate, v_b_igate, v_lru_lambda, v_w_out, v_ln2_g, v_ln2_b, v_ffn2_w_gate, v_ffn2_w_up, v_ffn2_w_down, v_ln3_g, v_ln3_b, v_w_ple_proj, v_w_ple_gate):
    names = ("ffn1_w_gate", "ffn1_w_up", "ffn1_w_down", "ln1_g", "ln1_b", "w_in", "conv_w", "conv_b", "w_rgate",
             "b_rgate", "w_igate", "b_igate", "lru_lambda", "w_out", "ln2_g", "ln2_b", "ffn2_w_gate", "ffn2_w_up",
             "ffn2_w_down", "ln3_g", "ln3_b", "w_ple_proj", "w_ple_gate")
    ws = (ffn1_w_gate, ffn1_w_up, ffn1_w_down, ln1_g, ln1_b, w_in, conv_w, conv_b, w_rgate, b_rgate, w_igate, b_igate,
          lru_lambda, w_out, ln2_g, ln2_b, ffn2_w_gate, ffn2_w_up, ffn2_w_down, ln3_g, ln3_b, w_ple_proj, w_ple_gate)
    ms = (m_ffn1_w_gate, m_ffn1_w_up, m_ffn1_w_down, m_ln1_g, m_ln1_b, m_w_in, m_conv_w, m_conv_b, m_w_rgate, m_b_rgate,
          m_w_igate, m_b_igate, m_lru_lambda, m_w_out, m_ln2_g, m_ln2_b, m_ffn2_w_gate, m_ffn2_w_up, m_ffn2_w_down,
          m_ln3_g, m_ln3_b, m_w_ple_proj, m_w_ple_gate)
    vs = (v_ffn1_w_gate, v_ffn1_w_up, v_ffn1_w_down, v_ln1_g, v_ln1_b, v_w_in, v_conv_w, v_conv_b, v_w_rgate, v_b_rgate,
          v_w_igate, v_b_igate, v_lru_lambda, v_w_out, v_ln2_g, v_ln2_b, v_ffn2_w_gate, v_ffn2_w_up, v_ffn2_w_down,
          v_ln3_g, v_ln3_b, v_w_ple_proj, v_w_ple_gate)
    def local(a):
        return a[0] if a.ndim >= 3 else a

    w_loc = {n: local(a) for n, a in zip(names, ws)}
    m_loc = {n: local(a) for n, a in zip(names, ms)}
    v_loc = {n: local(a) for n, a in zip(names, vs)}
    out_shapes = {n: a.shape for n, a in zip(names, ws)}

    shards = {n: (w_loc[n].T if n in COLUMN_SHARDED else w_loc[n]).astype(BF16) for n in BIG_WEIGHTS}
    gate1, up1, conv_all = _all_gather([shards["ffn1_w_gate"], shards["ffn1_w_up"], w_loc["conv_w"]], name="gather_first")
    ex = _MeshExchange({n: w_loc[n] for n in SMALL_WEIGHTS}, shards)
    ex.full["conv_w"] = _to_full("conv_w", conv_all)
    ex.take("ffn1_w_gate", gate1)
    ex.take("ffn1_w_up", up1)

    lsum, grad_x = _local_step(x[0], p[0, 0], loss_target[0], positions[0], ex)
    grads, reduced = ex.grads, ex.reduced
    d_model = x.shape[-1]
    loss = lax.psum(lsum[0, 0] * (0.5 / d_model), ("x", "y", "c"))

    small = SMALL_GRADS
    summed = _sum_parts(ex.small_parts, name="sum_small_grads")
    small_grads, row = {}, 0
    for n in small:
        rows = grads[n].size // LANES
        small_grads[n] = summed[row:row + rows].reshape(grads[n].shape)
        row += rows + (-rows) % 8
    me = 4 * lax.axis_index("x") + 2 * lax.axis_index("y") + lax.axis_index("c")
    cw_cols = w_loc["conv_w"].shape[1]
    small_grads["conv_w"] = lax.dynamic_slice_in_dim(small_grads["conv_w"], me * cw_cols, cw_cols, axis=1)

    out_g, out_d, out_m, out_v = {}, {}, {}, {}
    for n in names:
        wl, ml, vl = w_loc[n], m_loc[n], v_loc[n]
        shape = wl.shape
        if n in BIG_WEIGHTS:
            gparts = reduced[n]
        else:
            gparts = small_grads[n].reshape((1,) + shape)
        if wl.ndim == 3:
            wl, ml, vl = (t.reshape(-1, shape[-1]) for t in (wl, ml, vl))
            gparts = gparts.reshape(gparts.shape[0], -1, shape[-1])
        res = _adamw(wl, ml, vl, gparts, name=f"adamw_{n}")
        out_g[n], out_d[n], out_m[n], out_v[n] = (t.reshape(out_shapes[n]) for t in res)

    return (loss, grad_x[None], *[out_g[n] for n in names], *[out_d[n] for n in names],
            *[out_m[n] for n in names], *[out_v[n] for n in names])
```

```python
import jax
import jax.numpy as jnp
from jax import lax
from jax.experimental import pallas as pl
from jax.experimental.pallas import tpu as pltpu

F32 = jnp.float32
BF16 = jnp.bfloat16

N_DEV = 8
LANES = 128
MIB = 1 << 20

HEAD_DIM = 128
N_KV_HEADS = 4
DILATIONS = (1, 4, 16)
N_PATTERNS = 3
SPAN = 128
ROT_DIMS = 32
ROPE_THETA = 500000.0
LRU_C = 8.0
CONV_WIDTH = 4
LN_EPS = 1e-5
DEEPNORM_ALPHA = 2.0 ** 0.25
ATTN_TILE = SPAN * DILATIONS[-1]

ADAM_LR = 0.001
ADAM_B1 = 0.9
ADAM_B2 = 0.999
ADAM_EPS = 1e-08
ADAM_WD = 0.01
ADAM_STEP = 10

MESH = pl.DeviceIdType.MESH
NT_DIMS = (((1,), (1,)), ((), ()))
EPILOGUE_ROWS = 64


def _cp(semantics, vmem_mib):
    return pltpu.CompilerParams(dimension_semantics=semantics, vmem_limit_bytes=vmem_mib * MIB)


def _pick(n, candidates):
    for c in candidates:
        if n % c == 0:
            return c
    return n


class _Comm:
    def __init__(self, arrays, out_shapes, scratch, start, end, mid=None, split=False):
        self.arrays, self.out_shapes, self.scratch = list(arrays), list(out_shapes), list(scratch)
        self.start, self.mid, self.end, self.split = start, mid, end, split


def _call(body, *, name, grid, in_specs, out_specs, out_shape, args, scratch_shapes=(), vmem_mib, comm=None):
    single = not isinstance(out_shape, (tuple, list))
    out_shape_t = (out_shape,) if single else tuple(out_shape)
    out_specs_t = (out_specs,) if single else tuple(out_specs)
    params = _cp(("arbitrary",) * len(grid), vmem_mib)
    if comm is None:
        res = pl.pallas_call(body, name=name, grid=grid, in_specs=list(in_specs), out_specs=out_specs_t,
                             out_shape=out_shape_t, scratch_shapes=list(scratch_shapes), compiler_params=params)(*args)
        return res[0] if single else res
    n_in, n_out, n_scr = len(args), len(out_shape_t), len(scratch_shapes)
    nci, nco = len(comm.arrays), len(comm.out_shapes)
    total = 1
    for g in grid:
        total *= g

    def wrapped(*refs):
        ins, refs = refs[:n_in], refs[n_in:]
        cin, refs = refs[:nci], refs[nci:]
        outs, refs = refs[:n_out], refs[n_out:]
        cout, refs = refs[:nco], refs[nco:]
        scr, csem = refs[:n_scr], refs[n_scr:]
        step = pl.program_id(0)
        for ax in range(1, len(grid)):
            step = step * grid[ax] + pl.program_id(ax)

        @pl.when(step == 0)
        def _():
            comm.start(cin, cout, csem)

        body(*ins, *outs, *scr)
        if comm.mid is not None:
            @pl.when(step == (3 * total) // 4)
            def _():
                comm.mid(cin, cout, csem)

        @pl.when(step == total - 1)
        def _():
            comm.end(cin, cout, csem)

    hbm = pl.BlockSpec(memory_space=pltpu.HBM)
    if comm.split:
        sem = pl.BlockSpec(memory_space=pltpu.SEMAPHORE)
        n_sems = nco - nci
        res = pl.pallas_call(
            wrapped, name=name, grid=grid,
            in_specs=list(in_specs) + [hbm] * nci,
            out_specs=out_specs_t + (sem,) * n_sems + (hbm,) * nci,
            out_shape=out_shape_t + tuple(comm.out_shapes),
            scratch_shapes=list(scratch_shapes) + comm.scratch,
            input_output_aliases={n_in + k: n_out + n_sems + k for k in range(nci)},
            compiler_params=pltpu.CompilerParams(
                dimension_semantics=("arbitrary",) * len(grid), vmem_limit_bytes=vmem_mib * MIB,
                has_side_effects=pltpu.SideEffectType.DATAFLOW_SIDE_EFFECTING),
        )(*args, *[pltpu.with_memory_space_constraint(a, pltpu.HBM) for a in comm.arrays])
    else:
        res = pl.pallas_call(
            wrapped, name=name, grid=grid,
            in_specs=list(in_specs) + [hbm] * nci,
            out_specs=out_specs_t + (hbm,) * nco,
            out_shape=out_shape_t + tuple(comm.out_shapes),
            scratch_shapes=list(scratch_shapes) + comm.scratch,
            compiler_params=params)(*args, *comm.arrays)
    own, extra = res[:n_out], res[n_out:]
    return (own[0] if single else own), extra


def _mm(a, b, *, ta=False, tb=False, out_dtype=F32, scale=1.0, bm, bn, bk, name, comm=None):
    m, k = (a.shape[1], a.shape[0]) if ta else a.shape
    n = b.shape[0] if tb else b.shape[1]
    bm, bn, bk = min(bm, m), min(bn, n), min(bk, k)
    assert m % bm == 0 and n % bn == 0 and k % bk == 0, (name, m, n, k, bm, bn, bk)
    nk = k // bk
    a_spec = pl.BlockSpec((bk, bm), lambda i, j, kk: (kk, i)) if ta else pl.BlockSpec((bm, bk), lambda i, j, kk: (i, kk))
    b_spec = pl.BlockSpec((bn, bk), lambda i, j, kk: (j, kk)) if tb else pl.BlockSpec((bk, bn), lambda i, j, kk: (kk, j))
    dn = (((0 if ta else 1,), (1 if tb else 0,)), ((), ()))

    def body(a_ref, b_ref, o_ref, *acc):
        part = lax.dot_general(a_ref[...].astype(BF16), b_ref[...].astype(BF16), dn, preferred_element_type=F32)
        if nk == 1:
            o_ref[...] = (part * scale).astype(out_dtype)
            return
        acc_ref, = acc
        kk = pl.program_id(2)

        @pl.when(kk == 0)
        def _():
            acc_ref[...] = part

        @pl.when(kk > 0)
        def _():
            acc_ref[...] += part

        @pl.when(kk == nk - 1)
        def _():
            o_ref[...] = (acc_ref[...] * scale).astype(out_dtype)

    return _call(
        body, name=name,
        out_shape=jax.ShapeDtypeStruct((m, n), out_dtype),
        grid=(m // bm, n // bn, nk),
        in_specs=[a_spec, b_spec],
        out_specs=pl.BlockSpec((bm, bn), lambda i, j, kk: (i, j)),
        scratch_shapes=[pltpu.VMEM((bm, bn), F32)] if nk > 1 else [],
        args=(a, b), vmem_mib=56, comm=comm)


def _ffn_up(xb, wg, wu, *, bm, bn, name, comm=None):
    s, d = xb.shape
    f = wg.shape[0]
    bm, bn = min(bm, s), min(bn, f)
    assert s % bm == 0 and f % bn == 0

    def body(x_ref, wg_ref, wu_ref, hg_ref, hu_ref, h_ref):
        x = x_ref[...]
        g = lax.dot_general(x, wg_ref[...], NT_DIMS, preferred_element_type=F32)
        u = lax.dot_general(x, wu_ref[...], NT_DIMS, preferred_element_type=F32)
        sig = jax.nn.sigmoid(g)
        silu = g * sig
        hg_ref[...] = (u * (sig * (1.0 + g * (1.0 - sig)))).astype(BF16)
        hu_ref[...] = silu.astype(BF16)
        h_ref[...] = (silu * u).astype(BF16)

    out = jax.ShapeDtypeStruct((s, f), BF16)
    blk = pl.BlockSpec((bm, bn), lambda i, j: (i, j))
    return _call(
        body, name=name, out_shape=(out, out, out),
        grid=(s // bm, f // bn),
        in_specs=[pl.BlockSpec((bm, d), lambda i, j: (i, 0)),
                  pl.BlockSpec((bn, d), lambda i, j: (j, 0)),
                  pl.BlockSpec((bn, d), lambda i, j: (j, 0))],
        out_specs=(blk, blk, blk),
        args=(xb, wg, wu), vmem_mib=56, comm=comm)


def _ffn_bwd_dh(dzb, wd, g, u, *, scale, bm, bn, name, chunks=2, comm=None):
    s, d = dzb.shape
    f = wd.shape[0]
    bm, bn = min(bm, s), min(bn, f)
    assert s % bm == 0 and f % bn == 0

    cr = bm // chunks

    def body(dz_ref, wd_ref, hg_ref, hu_ref, dg_ref, du_ref):
        for r in range(chunks):
            rows = slice(r * cr, (r + 1) * cr)
            dh = lax.dot_general(dz_ref[rows, :], wd_ref[...], NT_DIMS, preferred_element_type=F32) * scale
            dg_ref[rows, :] = (dh * hg_ref[rows, :].astype(F32)).astype(BF16)
            du_ref[rows, :] = (dh * hu_ref[rows, :].astype(F32)).astype(BF16)

    out = jax.ShapeDtypeStruct((s, f), BF16)
    blk = pl.BlockSpec((bm, bn), lambda i, j: (i, j))
    return _call(
        body, name=name, out_shape=(out, out),
        grid=(s // bm, f // bn),
        in_specs=[pl.BlockSpec((bm, d), lambda i, j: (i, 0)),
                  pl.BlockSpec((bn, d), lambda i, j: (j, 0)), blk, blk],
        out_specs=(blk, blk),
        args=(dzb, wd, g, u), vmem_mib=56, comm=comm)


def _full_rows(acc_ref, rows, nj):
    return jnp.concatenate([acc_ref[jj, rows, :] for jj in range(nj)], axis=1)


def _mm_ln(a, b, res, gamma, beta, *, res_scale, mm_scale, bm, bn, name, comm=None):
    s, k = a.shape
    d = b.shape[1]
    bm, bn = min(bm, s), min(bn, d)
    assert s % bm == 0 and d % bn == 0
    nj = d // bn
    ch = min(EPILOGUE_ROWS, bm)

    def body(a_ref, b_ref, r_ref, g_ref, be_ref, y_ref, yb_ref, xh_ref, rs_ref, acc_ref):
        j = pl.program_id(1)
        acc_ref[j] = jnp.dot(a_ref[...], b_ref[...], preferred_element_type=F32)

        @pl.when(j == nj - 1)
        def _():
            def chunk(ci, carry):
                rows = pl.ds(pl.multiple_of(ci * ch, ch), ch)
                z = res_scale * r_ref[rows, :] + mm_scale * _full_rows(acc_ref, rows, nj)
                mu = jnp.mean(z, axis=-1, keepdims=True)
                zc = z - mu
                var = jnp.mean(zc * zc, axis=-1, keepdims=True)
                rstd = lax.rsqrt(var + LN_EPS)
                xh = zc * rstd
                y = xh * g_ref[...] + be_ref[...]
                y_ref[rows, :] = y
                yb_ref[rows, :] = y.astype(BF16)
                xh_ref[rows, :] = xh
                rs_ref[rows, :] = rstd
                return carry

            lax.fori_loop(0, bm // ch, chunk, 0)

    row = pl.BlockSpec((bm, d), lambda i, j: (i, 0))
    vec = pl.BlockSpec((1, d), lambda i, j: (0, 0))
    return _call(
        body, name=name,
        out_shape=(jax.ShapeDtypeStruct((s, d), F32), jax.ShapeDtypeStruct((s, d), BF16),
                   jax.ShapeDtypeStruct((s, d), F32), jax.ShapeDtypeStruct((s, 1), F32)),
        grid=(s // bm, nj),
        in_specs=[pl.BlockSpec((bm, k), lambda i, j: (i, 0)),
                  pl.BlockSpec((k, bn), lambda i, j: (0, j)), row, vec, vec],
        out_specs=(row, row, row, pl.BlockSpec((bm, 1), lambda i, j: (i, 0))),
        scratch_shapes=[pltpu.VMEM((nj, bm, bn), F32)],
        args=(a, b, res, gamma, beta), vmem_mib=58, comm=comm)


def _mm_dx(a, wt, extra, xhat, rstd, gamma, *, extra_scale, bm, bn, name, tb=False, comm=None):
    s, k = a.shape
    d = wt.shape[0] if tb else wt.shape[1]
    bm, bn = min(bm, s), min(bn, d)
    assert s % bm == 0 and d % bn == 0
    nj = d // bn
    ch = min(EPILOGUE_ROWS, bm)
    dims = NT_DIMS if tb else (((1,), (0,)), ((), ()))

    def body(a_ref, w_ref, e_ref, xh_ref, rs_ref, g_ref, dz_ref, dzb_ref, dg_ref, db_ref, acc_ref):
        i = pl.program_id(0)
        j = pl.program_id(1)
        acc_ref[j] = lax.dot_general(a_ref[...], w_ref[...], dims, preferred_element_type=F32)

        @pl.when(j == nj - 1)
        def _():
            def chunk(ci, carry):
                dgp, dbp = carry
                rows = pl.ds(pl.multiple_of(ci * ch, ch), ch)
                dx = extra_scale * e_ref[rows, :] + _full_rows(acc_ref, rows, nj)
                xh = xh_ref[rows, :]
                dxh = dx * g_ref[...]
                m1 = jnp.mean(dxh, axis=-1, keepdims=True)
                m2 = jnp.mean(dxh * xh, axis=-1, keepdims=True)
                dz = rs_ref[rows, :] * (dxh - m1 - xh * m2)
                dz_ref[rows, :] = dz
                dzb_ref[rows, :] = dz.astype(BF16)
                return dgp + jnp.sum(dx * xh, axis=0, keepdims=True), dbp + jnp.sum(dx, axis=0, keepdims=True)

            zero = jnp.zeros((1, d), F32)
            dgp, dbp = lax.fori_loop(0, bm // ch, chunk, (zero, zero))

            @pl.when(i == 0)
            def _():
                dg_ref[...] = dgp
                db_ref[...] = dbp

            @pl.when(i > 0)
            def _():
                dg_ref[...] += dgp
                db_ref[...] += dbp

    row = pl.BlockSpec((bm, d), lambda i, j: (i, 0))
    vec = pl.BlockSpec((1, d), lambda i, j: (0, 0))
    return _call(
        body, name=name,
        out_shape=(jax.ShapeDtypeStruct((s, d), F32), jax.ShapeDtypeStruct((s, d), BF16),
                   jax.ShapeDtypeStruct((1, d), F32), jax.ShapeDtypeStruct((1, d), F32)),
        grid=(s // bm, nj),
        in_specs=[pl.BlockSpec((bm, k), lambda i, j: (i, 0)),
                  pl.BlockSpec((bn, k), lambda i, j: (j, 0)) if tb else pl.BlockSpec((k, bn), lambda i, j: (0, j)),
                  row, row, pl.BlockSpec((bm, 1), lambda i, j: (i, 0)), vec],
        out_specs=(row, row, vec, vec),
        scratch_shapes=[pltpu.VMEM((nj, bm, bn), F32)],
        args=(a, wt, extra, xhat, rstd, gamma), vmem_mib=58, comm=comm)


def _ffn_dx(dg, du, wgt, wut, extra, *, extra_scale, bm, bn, name, comm=None):
    s, f = dg.shape
    d = wgt.shape[1]
    bm, bn = min(bm, s), min(bn, d)
    assert s % bm == 0 and d % bn == 0

    def body(dg_ref, du_ref, wg_ref, wu_ref, e_ref, o_ref):
        acc = jnp.dot(dg_ref[...], wg_ref[...], preferred_element_type=F32)
        acc = acc + jnp.dot(du_ref[...], wu_ref[...], preferred_element_type=F32)
        o_ref[...] = extra_scale * e_ref[...] + acc

    rows = pl.BlockSpec((bm, f), lambda i, j: (i, 0))
    cols = pl.BlockSpec((f, bn), lambda i, j: (0, j))
    blk = pl.BlockSpec((bm, bn), lambda i, j: (i, j))
    return _call(
        body, name=name, out_shape=jax.ShapeDtypeStruct((s, d), F32),
        grid=(s // bm, d // bn), in_specs=[rows, rows, cols, cols, blk], out_specs=blk,
        args=(dg, du, wgt, wut, extra), vmem_mib=58, comm=comm)


def _ln_bwd(dx, xhat, rstd, gamma, *, bm, name):
    s, d = dx.shape
    bm = min(bm, s)
    assert s % bm == 0
    ch = min(EPILOGUE_ROWS, bm)

    def body(dx_ref, xh_ref, rs_ref, g_ref, dz_ref, dzb_ref, dg_ref, db_ref):
        def chunk(ci, carry):
            dgp, dbp = carry
            rows = pl.ds(pl.multiple_of(ci * ch, ch), ch)
            dxv = dx_ref[rows, :]
            xh = xh_ref[rows, :]
            dxh = dxv * g_ref[...]
            m1 = jnp.mean(dxh, axis=-1, keepdims=True)
            m2 = jnp.mean(dxh * xh, axis=-1, keepdims=True)
            dz = rs_ref[rows, :] * (dxh - m1 - xh * m2)
            dz_ref[rows, :] = dz
            dzb_ref[rows, :] = dz.astype(BF16)
            return dgp + jnp.sum(dxv * xh, axis=0, keepdims=True), dbp + jnp.sum(dxv, axis=0, keepdims=True)

        zero = jnp.zeros((1, d), F32)
        dgp, dbp = lax.fori_loop(0, bm // ch, chunk, (zero, zero))
        i = pl.program_id(0)

        @pl.when(i == 0)
        def _():
            dg_ref[...] = dgp
            db_ref[...] = dbp

        @pl.when(i > 0)
        def _():
            dg_ref[...] += dgp
            db_ref[...] += dbp

    row = pl.BlockSpec((bm, d), lambda i: (i, 0))
    vec = pl.BlockSpec((1, d), lambda i: (0, 0))
    return _call(
        body, name=name,
        out_shape=(jax.ShapeDtypeStruct((s, d), F32), jax.ShapeDtypeStruct((s, d), BF16),
                   jax.ShapeDtypeStruct((1, d), F32), jax.ShapeDtypeStruct((1, d), F32)),
        grid=(s // bm,), in_specs=[row, row, pl.BlockSpec((bm, 1), lambda i: (i, 0)), vec],
        out_specs=(row, row, vec, vec), args=(dx, xhat, rstd, gamma), vmem_mib=48)


def _ple_loss(x3, x3b, p, wpg, wpp, target, *, bm, bn, name):
    s, d = x3.shape
    dp = p.shape[1]
    bm, bn = min(bm, s), min(bn, d)
    assert s % bm == 0 and d % bn == 0
    inv_d = 1.0 / d
    chunks = 4 if bm % 64 == 0 else 1
    cr = bm // chunks

    def body(x_ref, xb_ref, p_ref, wg_ref, wp_ref, t_ref, l_ref, dy_ref, dg_ref, dp_ref):
        first = (pl.program_id(0) == 0) & (pl.program_id(1) == 0)

        @pl.when(first)
        def _():
            l_ref[...] = jnp.zeros_like(l_ref)

        part = 0.0
        for r in range(chunks):
            rows = slice(r * cr, (r + 1) * cr)
            gp = jnp.dot(xb_ref[rows, :], wg_ref[...], preferred_element_type=F32)
            pp = lax.dot_general(p_ref[rows, :].astype(BF16), wp_ref[...], NT_DIMS, preferred_element_type=F32)
            sig = jax.nn.sigmoid(gp)
            err = x_ref[rows, :] + sig * pp - t_ref[rows, :]
            part = part + jnp.sum(err * err)
            dy = err * inv_d
            dy_ref[rows, :] = dy
            dg_ref[rows, :] = (dy * pp * sig * (1.0 - sig)).astype(BF16)
            dp_ref[rows, :] = (dy * sig).astype(BF16)
        l_ref[...] += part

    blk = pl.BlockSpec((bm, bn), lambda i, j: (i, j))
    return pl.pallas_call(
        body, name=name,
        out_shape=(jax.ShapeDtypeStruct((8, LANES), F32), jax.ShapeDtypeStruct((s, d), F32),
                   jax.ShapeDtypeStruct((s, d), BF16), jax.ShapeDtypeStruct((s, d), BF16)),
        grid=(s // bm, d // bn),
        in_specs=[blk, pl.BlockSpec((bm, d), lambda i, j: (i, 0)), pl.BlockSpec((bm, dp), lambda i, j: (i, 0)),
                  pl.BlockSpec((d, bn), lambda i, j: (0, j)), pl.BlockSpec((bn, dp), lambda i, j: (j, 0)), blk],
        out_specs=(pl.BlockSpec((8, LANES), lambda i, j: (0, 0)), blk, blk, blk),
        compiler_params=_cp(("arbitrary", "arbitrary"), 56),
    )(x3, x3b, p, wpg, wpp, target)


def _rope_tables(positions):
    half = ROT_DIMS // 2
    lane = jnp.arange(HEAD_DIM)
    inv_freq = jnp.power(jnp.float32(ROPE_THETA), -(lane % half).astype(F32) * (2.0 / ROT_DIMS))
    ang = positions.astype(F32)[:, None] * inv_freq
    cos, sin = jnp.cos(ang), jnp.sin(ang)
    cf = jnp.where(lane < ROT_DIMS, cos, 1.0)
    sa = jnp.where(lane < half, -sin, 0.0)
    sb = jnp.where((lane >= half) & (lane < ROT_DIMS), sin, 0.0)
    return cf, sa, sb


def _rotary(t, tabs, *, n_cols, inverse, out_dtype, bs, name):
    s = t.shape[0]
    bs = min(bs, s)
    half = ROT_DIMS // 2
    heads = N_KV_HEADS
    assert n_cols % heads == 0

    def body(t_ref, cf_ref, sa_ref, sb_ref, o_ref):
        cf, sa, sb = cf_ref[...], sa_ref[...], sb_ref[...]
        for hd in range(heads):
            lanes = slice(hd * HEAD_DIM, (hd + 1) * HEAD_DIM)
            v = t_ref[:, lanes]
            if inverse:
                o = v * cf + pltpu.roll(v * sa, half, 1) + pltpu.roll(v * sb, HEAD_DIM - half, 1)
            else:
                o = v * cf + pltpu.roll(v, HEAD_DIM - half, 1) * sa + pltpu.roll(v, half, 1) * sb
            o_ref[:, lanes] = o.astype(out_dtype)

    blk = pl.BlockSpec((bs, heads * HEAD_DIM), lambda i, j: (i, j))
    tab = pl.BlockSpec((bs, HEAD_DIM), lambda i, j: (i, 0))
    return pl.pallas_call(
        body, name=name, out_shape=jax.ShapeDtypeStruct((s, n_cols * HEAD_DIM), out_dtype),
        grid=(s // bs, n_cols // heads), in_specs=[blk, tab, tab, tab], out_specs=blk,
        compiler_params=_cp(("parallel", "arbitrary"), 32),
    )(t, *tabs)


def _attn_blocks():
    out = []
    for g, dil in enumerate(DILATIONS):
        sup = SPAN * dil
        for j in range(ATTN_TILE // sup):
            for r in range(dil):
                out.append((g, j * sup + r, dil, (j - 1) * sup + r if j > 0 else None, ATTN_TILE - sup + r))
    return out


def _rows(ref, start, dil, lead=None):
    idx = pl.ds(start, SPAN, stride=dil) if dil > 1 else pl.ds(start, SPAN)
    return ref[idx, :] if lead is None else ref[lead, idx, :]


def _band_masks(n):
    qi = lax.broadcasted_iota(jnp.int32, (SPAN, 2 * SPAN), 0)
    ki = lax.broadcasted_iota(jnp.int32, (SPAN, 2 * SPAN), 1)
    band = (ki >= qi) & (ki <= qi + SPAN)
    return band, band & ((ki >= SPAN) | (n > 0))


def _attn_fwd(qkr, proj, *, name):
    s = qkr.shape[0]
    t = ATTN_TILE
    assert s % t == 0
    nt = s // t
    scale = HEAD_DIM ** -0.5
    kcol, vcol = N_PATTERNS * N_KV_HEADS, (N_PATTERNS + 1) * N_KV_HEADS
    blocks = _attn_blocks()

    def body(q0, q1, q2, kc_ref, kp_ref, vc_ref, vp_ref, o_ref, l_ref, og, lg):
        n = pl.program_id(1)
        band, band_first = _band_masks(n)
        q_refs = (q0, q1, q2)
        for g, start, dil, prev_in_tile, prev_start in blocks:
            q = _rows(q_refs[g], start, dil).astype(BF16)
            if prev_in_tile is not None:
                kp, vp, mask = _rows(kc_ref, prev_in_tile, dil), _rows(vc_ref, prev_in_tile, dil), band
            else:
                kp, vp, mask = _rows(kp_ref, prev_start, dil), _rows(vp_ref, prev_start, dil), band_first
            kk = jnp.concatenate([kp, _rows(kc_ref, start, dil)], axis=0).astype(BF16)
            vv = jnp.concatenate([vp, _rows(vc_ref, start, dil)], axis=0).astype(BF16)
            sc = lax.dot_general(q, kk, (((1,), (1,)), ((), ())), preferred_element_type=F32) * scale
            sc = jnp.where(mask, sc, -1e30)
            m = jnp.max(sc, axis=-1, keepdims=True)
            e = jnp.exp(sc - m)
            den = jnp.sum(e, axis=-1, keepdims=True)
            o = jnp.dot(e.astype(BF16), vv, preferred_element_type=F32) / den
            idx = pl.ds(start, SPAN, stride=dil) if dil > 1 else pl.ds(start, SPAN)
            og[g, idx, :] = o
            lg[g, idx, :] = jnp.broadcast_to(m + jnp.log(den), (SPAN, HEAD_DIM))
        l0, l1, l2 = lg[0], lg[1], lg[2]
        m = jnp.maximum(jnp.maximum(l0, l1), l2)
        w0, w1, w2 = jnp.exp(l0 - m), jnp.exp(l1 - m), jnp.exp(l2 - m)
        den = w0 + w1 + w2
        o_ref[...] = (w0 * og[0] + w1 * og[1] + w2 * og[2]) / den
        l_ref[...] = m + jnp.log(den)

    def col(c, prev=False):
        if prev:
            return pl.BlockSpec((t, HEAD_DIM), lambda h, n: (jnp.maximum(n - 1, 0), c + h))
        return pl.BlockSpec((t, HEAD_DIM), lambda h, n: (n, c + h))

    out = jax.ShapeDtypeStruct((s, N_KV_HEADS * HEAD_DIM), F32)
    return pl.pallas_call(
        body, name=name, out_shape=(out, out),
        grid=(N_KV_HEADS, nt),
        in_specs=[col(0), col(N_KV_HEADS), col(2 * N_KV_HEADS), col(kcol), col(kcol, True), col(vcol), col(vcol, True)],
        out_specs=(col(0), col(0)),
        scratch_shapes=[pltpu.VMEM((N_PATTERNS, t, HEAD_DIM), F32), pltpu.VMEM((N_PATTERNS, t, HEAD_DIM), F32)],
        compiler_params=_cp(("parallel", "arbitrary"), 48),
    )(qkr, qkr, qkr, qkr, qkr, proj, proj)


def _attn_bwd(qkr, proj, attn, lse, dcat, *, name, comm=None):
    s = qkr.shape[0]
    t = ATTN_TILE
    nt = s // t
    scale = HEAD_DIM ** -0.5
    kcol, vcol = N_PATTERNS * N_KV_HEADS, (N_PATTERNS + 1) * N_KV_HEADS
    blocks = _attn_blocks()

    def body(q0, q1, q2, kc_ref, kp_ref, vc_ref, vp_ref, o_ref, l_ref, do_ref,
             dq0, dq1, dq2, dk_ref, dv_ref, ck, cv, tkc, tvc, tkp, tvp):
        n = pl.program_id(1)
        for ref in (tkc, tvc, tkp, tvp):
            ref[...] = jnp.zeros_like(ref)

        @pl.when(n < nt)
        def _():
            band, band_first = _band_masks(n)
            q_refs, dq_refs = (q0, q1, q2), (dq0, dq1, dq2)
            for g, start, dil, prev_in_tile, prev_start in blocks:
                idx = pl.ds(start, SPAN, stride=dil) if dil > 1 else pl.ds(start, SPAN)
                q = q_refs[g][idx, :].astype(BF16)
                if prev_in_tile is not None:
                    kp, vp, mask = _rows(kc_ref, prev_in_tile, dil), _rows(vc_ref, prev_in_tile, dil), band
                else:
                    kp, vp, mask = _rows(kp_ref, prev_start, dil), _rows(vp_ref, prev_start, dil), band_first
                kk = jnp.concatenate([kp, kc_ref[idx, :]], axis=0).astype(BF16)
                vv = jnp.concatenate([vp, vc_ref[idx, :]], axis=0).astype(BF16)
                do = do_ref[idx, :]
                dsum = jnp.sum(do * o_ref[idx, :], axis=-1, keepdims=True)
                lrow = l_ref[idx, :][:, :1]
                dob = do.astype(BF16)
                sc = lax.dot_general(q, kk, (((1,), (1,)), ((), ())), preferred_element_type=F32) * scale
                p = jnp.where(mask, jnp.exp(sc - lrow), 0.0)
                dp = lax.dot_general(dob, vv, (((1,), (1,)), ((), ())), preferred_element_type=F32)
                ds = (p * (dp - dsum) * scale).astype(BF16)
                pb = p.astype(BF16)
                dq_refs[g][idx, :] = jnp.dot(ds, kk, preferred_element_type=F32)
                dkk = lax.dot_general(ds, q, (((0,), (0,)), ((), ())), preferred_element_type=F32)
                dvv = lax.dot_general(pb, dob, (((0,), (0,)), ((), ())), preferred_element_type=F32)
                tkc[idx, :] += dkk[SPAN:]
                tvc[idx, :] += dvv[SPAN:]
                if prev_in_tile is not None:
                    pidx = pl.ds(prev_in_tile, SPAN, stride=dil) if dil > 1 else pl.ds(prev_in_tile, SPAN)
                    tkc[pidx, :] += dkk[:SPAN]
                    tvc[pidx, :] += dvv[:SPAN]
                else:
                    pidx = pl.ds(prev_start, SPAN, stride=dil) if dil > 1 else pl.ds(prev_start, SPAN)
                    tkp[pidx, :] += dkk[:SPAN]
                    tvp[pidx, :] += dvv[:SPAN]

        @pl.when(n > 0)
        def _():
            dk_ref[...] = ck[...] + tkp[...]
            dv_ref[...] = (cv[...] + tvp[...]).astype(BF16)

        ck[...] = tkc[...]
        cv[...] = tvc[...]

    def col(c, prev=False):
        if prev:
            return pl.BlockSpec((t, HEAD_DIM), lambda h, n: (jnp.maximum(jnp.minimum(n, nt - 1) - 1, 0), c + h))
        return pl.BlockSpec((t, HEAD_DIM), lambda h, n: (jnp.minimum(n, nt - 1), c + h))

    kv_out = pl.BlockSpec((t, HEAD_DIM), lambda h, n: (jnp.maximum(n - 1, 0), h))
    tile = pltpu.VMEM((t, HEAD_DIM), F32)
    per_head = jax.ShapeDtypeStruct((s, N_KV_HEADS * HEAD_DIM), F32)
    return _call(
        body, name=name,
        out_shape=(per_head, per_head, per_head, per_head, jax.ShapeDtypeStruct((s, N_KV_HEADS * HEAD_DIM), BF16)),
        grid=(N_KV_HEADS, nt + 1),
        in_specs=[col(0), col(N_KV_HEADS), col(2 * N_KV_HEADS), col(kcol), col(kcol, True), col(vcol), col(vcol, True),
                  col(0), col(0), col(0)],
        out_specs=(col(0), col(0), col(0), kv_out, kv_out),
        scratch_shapes=[tile] * 6,
        args=(qkr, qkr, qkr, qkr, qkr, proj, proj, attn, lse, dcat), vmem_mib=48, comm=comm)


GELU_C0 = 0.7978845608028654
GELU_C1 = 0.044715


def _softplus_neg(lam):
    y = jnp.exp(-jnp.abs(lam))
    w = 1.0 + y
    log1p = jnp.where(w == 1.0, y, jnp.log(w) * (y / jnp.where(w == 1.0, 1.0, w - 1.0)))
    return jnp.maximum(-lam, 0.0) + log1p


def _down(cur, prev, k, row):
    if k == 0:
        return cur
    return jnp.where(row < k, pltpu.roll(prev, k, 0), pltpu.roll(cur, k, 0))


def _up(cur, nxt, k, row, tt):
    if k == 0:
        return cur
    return jnp.where(row >= tt - k, pltpu.roll(nxt, tt - k, 0), pltpu.roll(cur, tt - k, 0))


def _lru_gates(x, xp, cw, cb, wr, br, wi, bi, lam, row):
    shifts = [_down(x, xp, k, row) for k in range(CONV_WIDTH)]
    xc = cb
    for j in range(CONV_WIDTH):
        xc = xc + cw[j:j + 1, :] * shifts[CONV_WIDTH - 1 - j]
    xcb = xc.astype(BF16)
    r = jax.nn.sigmoid(jnp.dot(xcb, wr, preferred_element_type=F32) + br)
    i = jax.nn.sigmoid(jnp.dot(xcb, wi, preferred_element_type=F32) + bi)
    c = -LRU_C * _softplus_neg(lam)
    la = c * r
    a = jnp.exp(la)
    mult = jnp.sqrt(jnp.tanh(-la) * (a * a + 1.0))
    return shifts, xc, xcb, r, i, c, a, mult


def _lru_fwd(proj, cw, cb, wr, br, wi, bi, lam, *, tt, name):
    s = proj.shape[0]
    nblk = wr.shape[0]
    c = nblk * LANES
    tt = min(tt, s)
    xcol0 = (N_PATTERNS + 2) * N_KV_HEADS
    ycol0 = xcol0 + nblk

    def body(x_ref, y_ref, cw_ref, cb_ref, wr_ref, br_ref, wi_ref, bi_ref, lam_ref, rec_ref, h_ref, xprev, hc):
        n = pl.program_id(1)

        @pl.when(n == 0)
        def _():
            xprev[...] = jnp.zeros_like(xprev)
            hc[...] = jnp.zeros_like(hc)

        row = lax.broadcasted_iota(jnp.int32, (tt, LANES), 0)
        x = x_ref[...]
        _, xc, _, _, i, _, a, mult = _lru_gates(
            x, xprev[...], cw_ref[...], cb_ref[...], wr_ref[0].astype(BF16), br_ref[...],
            wi_ref[0].astype(BF16), bi_ref[...], lam_ref[...], row)
        av, bv = a, mult * (i * xc)
        k = 1
        while k < tt:
            bs = jnp.where(row < k, 0.0, pltpu.roll(bv, k, 0))
            as_ = jnp.where(row < k, 1.0, pltpu.roll(av, k, 0))
            bv = bv + av * bs
            av = av * as_
            k *= 2
        h = bv + av * hc[0:1, :]
        hc[...] = jnp.broadcast_to(h[tt - 1:tt, :], hc.shape)
        h_ref[...] = h
        y = y_ref[...]
        gel = 0.5 * y * (1.0 + jnp.tanh(GELU_C0 * (y + GELU_C1 * y * y * y)))
        rec_ref[...] = (h * gel).astype(BF16)
        xprev[...] = x

    vec = pl.BlockSpec((1, LANES), lambda b, n: (0, b))
    wblk = pl.BlockSpec((1, LANES, LANES), lambda b, n: (b, 0, 0))
    out = pl.BlockSpec((tt, LANES), lambda b, n: (n, b))
    return pl.pallas_call(
        body, name=name,
        out_shape=(jax.ShapeDtypeStruct((s, c), BF16), jax.ShapeDtypeStruct((s, c), F32)),
        grid=(nblk, s // tt),
        in_specs=[pl.BlockSpec((tt, LANES), lambda b, n: (n, xcol0 + b)),
                  pl.BlockSpec((tt, LANES), lambda b, n: (n, ycol0 + b)),
                  pl.BlockSpec((CONV_WIDTH, LANES), lambda b, n: (0, b)), vec, wblk, vec, wblk, vec, vec],
        out_specs=(out, out),
        scratch_shapes=[pltpu.VMEM((tt, LANES), F32), pltpu.VMEM((8, LANES), F32)],
        compiler_params=_cp(("parallel", "arbitrary"), 32),
    )(proj, proj, cw, cb, wr, br, wi, bi, lam)


def _lru_bwd(proj, hseq, dcat, cw, cb, wr, br, wi, bi, lam, *, tt, name, comm=None):
    s = proj.shape[0]
    nblk = wr.shape[0]
    c = nblk * LANES
    tt = min(tt, s)
    nt = s // tt
    xcol0 = (N_PATTERNS + 2) * N_KV_HEADS
    ycol0 = xcol0 + nblk
    rcol0 = N_KV_HEADS

    def body(x_ref, xp_ref, y_ref, h_ref, hp_ref, dr_ref, cw_ref, cb_ref, wr_ref, br_ref, wi_ref, bi_ref, lam_ref,
             dx_ref, dy_ref, dcw_ref, dcb_ref, dwr_ref, dbr_ref, dwi_ref, dbi_ref, dlam_ref, dxc_next, gcar, acar):
        n = pl.program_id(1)
        rt = nt - 1 - n

        @pl.when(n == 0)
        def _():
            for ref in (dxc_next, gcar, acar, dcw_ref, dcb_ref, dwr_ref, dbr_ref, dwi_ref, dbi_ref, dlam_ref):
                ref[...] = jnp.zeros_like(ref)

        row = lax.broadcasted_iota(jnp.int32, (tt, LANES), 0)
        x = x_ref[...]
        xp = jnp.where(rt > 0, xp_ref[...], 0.0)
        cwv = cw_ref[...]
        wrb, wib = wr_ref[0].astype(BF16), wi_ref[0].astype(BF16)
        lam_v = lam_ref[...]
        shifts, xc, xcb, r, i, cc, a, mult = _lru_gates(x, xp, cwv, cb_ref[...], wrb, br_ref[...], wib, bi_ref[...],
                                                        lam_v, row)
        h = h_ref[...]
        hp_last = jnp.where(rt > 0, hp_ref[7:8, :], 0.0)
        hprev = jnp.where(row < 1, hp_last, pltpu.roll(h, 1, 0))
        y = y_ref[...]
        y2 = y * y
        th = jnp.tanh(GELU_C0 * (y + GELU_C1 * y2 * y))
        gel = 0.5 * y * (1.0 + th)
        dgel = 0.5 * (1.0 + th) + 0.5 * y * (1.0 - th * th) * GELU_C0 * (1.0 + 3.0 * GELU_C1 * y2)
        drec = dr_ref[...]
        dy_ref[...] = (drec * h * dgel).astype(BF16)
        av = jnp.where(row >= tt - 1, acar[0:1, :], pltpu.roll(a, tt - 1, 0))
        bv = drec * gel
        k = 1
        while k < tt:
            bs = jnp.where(row >= tt - k, 0.0, pltpu.roll(bv, tt - k, 0))
            as_ = jnp.where(row >= tt - k, 1.0, pltpu.roll(av, tt - k, 0))
            bv = bv + av * bs
            av = av * as_
            k *= 2
        g = bv + av * gcar[0:1, :]
        gcar[...] = jnp.broadcast_to(g[0:1, :], gcar.shape)
        acar[...] = jnp.broadcast_to(a[0:1, :], acar.shape)
        da = g * hprev
        d_ixc = g * mult
        dmult = g * (i * xc)
        di = d_ixc * xc
        dxc = d_ixc * i
        a2 = a * a
        dla = da * a - dmult * (a2 / mult)
        dr = dla * cc
        dsp = jnp.sum(dla * r, axis=0, keepdims=True) * (-LRU_C)
        dlam_ref[...] += dsp * (-jax.nn.sigmoid(-lam_v))
        dzr = dr * r * (1.0 - r)
        dzi = di * i * (1.0 - i)
        dbr_ref[...] += jnp.sum(dzr, axis=0, keepdims=True)
        dbi_ref[...] += jnp.sum(dzi, axis=0, keepdims=True)
        dzrb, dzib = dzr.astype(BF16), dzi.astype(BF16)
        tn = (((0,), (0,)), ((), ()))
        ntd = (((1,), (1,)), ((), ()))
        dwr_ref[0] += lax.dot_general(xcb, dzrb, tn, preferred_element_type=F32)
        dwi_ref[0] += lax.dot_general(xcb, dzib, tn, preferred_element_type=F32)
        dxc = (dxc + lax.dot_general(dzrb, wrb, ntd, preferred_element_type=F32)
               + lax.dot_general(dzib, wib, ntd, preferred_element_type=F32))
        dcb_ref[...] += jnp.sum(dxc, axis=0, keepdims=True)
        dcw_ref[...] += jnp.concatenate(
            [jnp.sum(dxc * shifts[CONV_WIDTH - 1 - j], axis=0, keepdims=True) for j in range(CONV_WIDTH)], axis=0)
        nxt = dxc_next[...]
        dx = cwv[0:1, :] * _up(dxc, nxt, CONV_WIDTH - 1, row, tt)
        for j in range(1, CONV_WIDTH):
            dx = dx + cwv[j:j + 1, :] * _up(dxc, nxt, CONV_WIDTH - 1 - j, row, tt)
        dx_ref[...] = dx.astype(BF16)
        dxc_next[...] = dxc

    def tile(col0, prev=False):
        if prev:
            return pl.BlockSpec((tt, LANES), lambda b, n: (jnp.maximum(nt - 2 - n, 0), col0 + b))
        return pl.BlockSpec((tt, LANES), lambda b, n: (nt - 1 - n, col0 + b))

    vec = pl.BlockSpec((1, LANES), lambda b, n: (0, b))
    wblk = pl.BlockSpec((1, LANES, LANES), lambda b, n: (b, 0, 0))
    cwblk = pl.BlockSpec((CONV_WIDTH, LANES), lambda b, n: (0, b))
    hp8 = pl.BlockSpec((8, LANES), lambda b, n: (jnp.maximum((nt - 1 - n) * (tt // 8) - 1, 0), b))
    vshape = jax.ShapeDtypeStruct((1, c), F32)
    wshape = jax.ShapeDtypeStruct((nblk, LANES, LANES), F32)
    return _call(
        body, name=name,
        out_shape=(jax.ShapeDtypeStruct((s, c), BF16), jax.ShapeDtypeStruct((s, c), BF16),
                   jax.ShapeDtypeStruct((CONV_WIDTH, c), F32), vshape, wshape, vshape, wshape, vshape, vshape),
        grid=(nblk, nt),
        in_specs=[tile(xcol0), tile(xcol0, True), tile(ycol0), tile(0), hp8, tile(rcol0),
                  cwblk, vec, wblk, vec, wblk, vec, vec],
        out_specs=(tile(0), tile(0), cwblk, vec, wblk, vec, wblk, vec, vec),
        scratch_shapes=[pltpu.VMEM((tt, LANES), F32), pltpu.VMEM((8, LANES), F32), pltpu.VMEM((8, LANES), F32)],
        args=(proj, proj, proj, hseq, hseq, dcat, cw, cb, wr, br, wi, bi, lam), vmem_mib=32, comm=comm)


ROW_BLOCKS = (256, 128, 64, 32, 16, 8)


def _adamw(w, m, v, gparts, *, name):
    r, c = w.shape
    npart = gparts.shape[0]
    br = _pick(r, ROW_BLOCKS)
    c1 = 1.0 - ADAM_B1 ** ADAM_STEP
    c2 = 1.0 - ADAM_B2 ** ADAM_STEP

    def body(w_ref, m_ref, v_ref, g_ref, go_ref, d_ref, mo_ref, vo_ref):
        g = g_ref[0].astype(F32)
        for q in range(1, npart):
            g = g + g_ref[q].astype(F32)
        mn = ADAM_B1 * m_ref[...] + (1.0 - ADAM_B1) * g
        vn = ADAM_B2 * v_ref[...] + (1.0 - ADAM_B2) * (g * g)
        go_ref[...] = g
        mo_ref[...] = mn
        vo_ref[...] = vn
        d_ref[...] = -ADAM_LR * ((mn / c1) / (jnp.sqrt(vn / c2) + ADAM_EPS) + ADAM_WD * w_ref[...])

    blk = pl.BlockSpec((br, c), lambda i: (i, 0))
    out = jax.ShapeDtypeStruct((r, c), F32)
    return pl.pallas_call(
        body, name=name, out_shape=(out, out, out, out), grid=(r // br,),
        in_specs=[blk, blk, blk, pl.BlockSpec((npart, br, c), lambda i: (0, i, 0))],
        out_specs=(blk, blk, blk, blk),
        compiler_params=_cp(("parallel",), 48),
    )(w, m, v, gparts)


def _sum_parts(parts, *, name):
    npart, r, c = parts.shape
    br = next((b for b in range(min(r, 2048) // 8 * 8, 0, -8) if r % b == 0), r)

    def body(p_ref, o_ref):
        acc = p_ref[0]
        for q in range(1, npart):
            acc = acc + p_ref[q]
        o_ref[...] = acc

    return pl.pallas_call(
        body, name=name, out_shape=jax.ShapeDtypeStruct((r, c), F32), grid=(r // br,),
        in_specs=[pl.BlockSpec((npart, br, c), lambda i: (0, i, 0))],
        out_specs=pl.BlockSpec((br, c), lambda i: (i, 0)),
        compiler_params=_cp(("parallel",), 48),
    )(parts)


HBM = pl.BlockSpec(memory_space=pltpu.HBM)


def _mesh_pos():
    return lax.axis_index("x"), lax.axis_index("y"), lax.axis_index("c")


def _all_gather(shards, *, name):
    comm = _gather_comm(shards)
    na = len(shards)

    def body(*refs):
        ins, outs, sems = refs[:na], refs[na:2 * na], refs[2 * na:]
        comm.start(ins, outs, sems)
        comm.mid(ins, outs, sems)
        comm.end(ins, outs, sems)

    return pl.pallas_call(
        body, name=name, out_shape=tuple(comm.out_shapes),
        in_specs=[HBM] * na, out_specs=tuple([HBM] * na), scratch_shapes=comm.scratch,
    )(*shards)


def _gather_comm(shards):
    na = len(shards)

    def parts(x_refs, out_refs, sems):
        send_sems, recv_sems, local_sems = sems
        x, y, c = _mesh_pos()
        me, sibling = (x, y, c), (x, y, 1 - c)
        chips = [(1 - x, y), (x, 1 - y), (1 - x, 1 - y)]

        def copy(a, k, block, to, src=None):
            px, py, pc = block
            dst = out_refs[a].at[4 * px + 2 * py + pc]
            return pltpu.make_async_remote_copy(
                src_ref=dst if src is None else src, dst_ref=dst,
                send_sem=send_sems.at[a, k], recv_sem=recv_sems.at[a, k],
                device_id=to, device_id_type=MESH)

        def mine(a):
            return pltpu.make_async_copy(x_refs[a], out_refs[a].at[4 * x + 2 * y + c], local_sems.at[a])

        def first(a):
            return [copy(a, 0, me, sibling, src=x_refs[a])] + [
                copy(a, 1 + j, me, (*chip, c), src=x_refs[a]) for j, chip in enumerate(chips)]

        def passed(a, j):
            return copy(a, 4 + j, (*chips[j], c), sibling)

        return me, sibling, chips, c, copy, mine, first, passed

    def start(x_refs, out_refs, sems):
        *_, mine, first, _ = parts(x_refs, out_refs, sems)
        for a in range(na):
            mine(a).start()
            for cp in first(a):
                cp.start()

    def mid(x_refs, out_refs, sems):
        me, _, chips, c, copy, _, _, passed = parts(x_refs, out_refs, sems)
        for j, chip in enumerate(chips):
            for a in range(na):
                copy(a, 1 + j, (*chip, c), me).wait_recv()
                passed(a, j).start()

    def end(x_refs, out_refs, sems):
        me, sibling, chips, c, copy, mine, first, passed = parts(x_refs, out_refs, sems)
        for a in range(na):
            copy(a, 0, sibling, me).wait_recv()
            for j, chip in enumerate(chips):
                copy(a, 4 + j, (*chip, 1 - c), me).wait_recv()
        for a in range(na):
            for cp in first(a) + [passed(a, j) for j in range(3)]:
                cp.wait_send()
            mine(a).wait()

    return _Comm(
        shards, [jax.ShapeDtypeStruct((N_DEV,) + a.shape, a.dtype) for a in shards],
        [pltpu.SemaphoreType.DMA((na, 7)), pltpu.SemaphoreType.DMA((na, 7)), pltpu.SemaphoreType.DMA((na,))],
        start, end, mid)


def _scatter_comm(g8s):
    na = len(g8s)

    def parts(g_refs, buf_refs, sems):
        send_sems, recv_sems, local_sems = sems
        x, y, c = _mesh_pos()
        me_idx = 4 * x + 2 * y + c

        def peer(k):
            return (1 - x if k & 4 else x, 1 - y if k & 2 else y, 1 - c if k & 1 else c)

        def copy(a, k, slot):
            px, py, pc = peer(k)
            return pltpu.make_async_remote_copy(
                src_ref=g_refs[a].at[4 * px + 2 * py + pc], dst_ref=buf_refs[a].at[slot],
                send_sem=send_sems.at[a, k - 1], recv_sem=recv_sems.at[a, k - 1],
                device_id=(px, py, pc), device_id_type=MESH)

        def mine(a):
            return pltpu.make_async_copy(g_refs[a].at[me_idx], buf_refs[a].at[me_idx], local_sems.at[a])

        return me_idx, peer, copy, mine

    def start(g_refs, buf_refs, sems):
        me_idx, _, copy, mine = parts(g_refs, buf_refs, sems)
        for a in range(na):
            mine(a).start()
            for k in range(1, N_DEV):
                copy(a, k, me_idx).start()

    def end(g_refs, buf_refs, sems):
        me_idx, peer, copy, mine = parts(g_refs, buf_refs, sems)
        for a in range(na):
            for k in range(1, N_DEV):
                px, py, pc = peer(k)
                copy(a, k, 4 * px + 2 * py + pc).wait_recv()
        for a in range(na):
            for k in range(1, N_DEV):
                copy(a, k, me_idx).wait_send()
            mine(a).wait()

    return _Comm(
        g8s, [jax.ShapeDtypeStruct(g.shape, g.dtype) for g in g8s],
        [pltpu.SemaphoreType.DMA((na, N_DEV - 1)), pltpu.SemaphoreType.DMA((na, N_DEV - 1)),
         pltpu.SemaphoreType.DMA((na,))],
        start, end)


def _scatter_peer(k, x, y, c):
    px, py, pc = (1 - x if k & 4 else x, 1 - y if k & 2 else y, 1 - c if k & 1 else c)
    return (px, py, pc), 4 * px + 2 * py + pc


def _scatter_start_comm(g8):
    land = lax.empty(g8.shape, g8.dtype)

    def start(refs, outs, sems):
        g_ref, land_ref = refs
        send_sems, recv_sems = outs[:2]
        x, y, c = _mesh_pos()
        me_idx = 4 * x + 2 * y + c
        pltpu.make_async_copy(g_ref.at[me_idx], land_ref.at[me_idx], sems[0]).start()
        for k in range(1, N_DEV):
            peer, peer_idx = _scatter_peer(k, x, y, c)
            pltpu.make_async_remote_copy(
                src_ref=g_ref.at[peer_idx], dst_ref=land_ref.at[me_idx],
                send_sem=send_sems.at[k - 1], recv_sem=recv_sems.at[k - 1],
                device_id=peer, device_id_type=MESH).start()

    def end(refs, outs, sems):
        g_ref, land_ref = refs
        x, y, c = _mesh_pos()
        me_idx = 4 * x + 2 * y + c
        pltpu.make_async_copy(g_ref.at[me_idx], land_ref.at[me_idx], sems[0]).wait()

    sem_shape = pltpu.SemaphoreType.DMA((N_DEV - 1,))
    return _Comm([g8, land], [sem_shape, sem_shape, pltpu.HBM(g8.shape, g8.dtype), pltpu.HBM(g8.shape, g8.dtype)],
                 [pltpu.SemaphoreType.DMA(())], start, end, split=True)


def _scatter_wait(started, after, *, name):
    send_sems, recv_sems, g8, land = started

    def body(g_ref, land_ref, send_ref, recv_ref, after_ref, g_dead, got_ref):
        x, y, c = _mesh_pos()
        me_idx = 4 * x + 2 * y + c
        for k in range(1, N_DEV):
            peer, peer_idx = _scatter_peer(k, x, y, c)
            pltpu.make_async_remote_copy(
                src_ref=g_ref.at[peer_idx], dst_ref=land_ref.at[peer_idx],
                send_sem=send_ref.at[k - 1], recv_sem=recv_ref.at[k - 1],
                device_id=peer, device_id_type=MESH).wait()

    sem = pl.BlockSpec(memory_space=pltpu.SEMAPHORE)
    return pl.pallas_call(
        body, name=name, out_shape=(pltpu.HBM(g8.shape, g8.dtype), pltpu.HBM(land.shape, land.dtype)),
        in_specs=[HBM, HBM, sem, sem, pl.BlockSpec(memory_space=pl.ANY)], out_specs=(HBM, HBM),
        input_output_aliases={0: 0, 1: 1},
        compiler_params=pltpu.CompilerParams(has_side_effects=pltpu.SideEffectType.DATAFLOW_SIDE_EFFECTING),
    )(g8, land, send_sems, recv_sems, after)[1]


def _join_comm(c1, c2):
    n1i, n1o, n1s = len(c1.arrays), len(c1.out_shapes), len(c1.scratch)

    def both(f1, f2):
        def run(ins, outs, sems):
            if f1 is not None:
                f1(ins[:n1i], outs[:n1o], sems[:n1s])
            if f2 is not None:
                f2(ins[n1i:], outs[n1o:], sems[n1s:])
        return run

    mid = both(c1.mid, c2.mid) if (c1.mid is not None or c2.mid is not None) else None
    return _Comm(c1.arrays + c2.arrays, c1.out_shapes + c2.out_shapes, c1.scratch + c2.scratch,
                 both(c1.start, c2.start), both(c1.end, c2.end), mid)


BIG_WEIGHTS = ("ffn1_w_gate", "ffn1_w_up", "ffn1_w_down", "w_in", "w_out",
               "ffn2_w_gate", "ffn2_w_up", "ffn2_w_down", "w_ple_proj", "w_ple_gate")
COLUMN_SHARDED = ("ffn1_w_gate", "ffn1_w_up", "w_in", "ffn2_w_gate", "ffn2_w_up", "w_ple_proj", "conv_w")
SMALL_WEIGHTS = ("ln1_g", "ln1_b", "conv_b", "w_rgate", "b_rgate", "w_igate", "b_igate", "lru_lambda",
                 "ln2_g", "ln2_b", "ln3_g", "ln3_b")
SMALL_GRADS = SMALL_WEIGHTS + ("conv_w",)


class _Exchange:
    def __init__(self, full):
        self.full = dict(full)
        self.grads = {}

    def __getitem__(self, name):
        return self.full[name]

    def gather(self, names):
        return None, None

    def scatter(self, names):
        return None, None

    def scatter_start(self, name):
        return None, None

    def gather_small(self):
        return None, None


class _MeshExchange(_Exchange):
    def __init__(self, full, shards):
        super().__init__(full)
        self.shards = shards
        self.reduced = {}
        self.started = {}
        self.small_parts = None

    def gather(self, names):
        def done(outs):
            for n, o in zip(names, outs):
                self.take(n, o)
        return _gather_comm([self.shards[n] for n in names]), done

    def take(self, name, gathered):
        self.full[name] = gathered.reshape((N_DEV * gathered.shape[1],) + gathered.shape[2:])

    def scatter(self, names):
        def done(outs):
            self.reduced.update(zip(names, outs))
        if not names:
            return None, None
        return _scatter_comm([_to_owner_blocks(n, self.grads[n]) for n in names]), done

    def scatter_start(self, name):
        def done(outs):
            self.started[name] = outs
        return _scatter_start_comm(_to_owner_blocks(name, self.grads[name])), done

    def finish(self, after):
        for name, started in self.started.items():
            self.reduced[name] = _scatter_wait(started, after, name=f"scatter_wait_{name}")

    def gather_small(self):
        def done(outs):
            self.small_parts, = outs
        packed = jnp.concatenate([_rows128(self.grads[n]) for n in SMALL_GRADS], axis=0)
        return _gather_comm([packed]), done


def _carried(comm_done, call):
    comm, done = comm_done
    res = call(comm)
    if comm is None:
        return res
    res, outs = res
    done(outs)
    return res


def _dw(a, b, *, scale=1.0, name, comm=None):
    k, m = a.shape
    n = b.shape[1]
    return _mm(a, b, ta=True, scale=scale, out_dtype=BF16, bm=_pick(m, (1024, 512, 256, 128)),
               bn=_pick(n, (512, 256, 128)), bk=k, name=name, comm=comm)


def _ffn_bwd(ex, names, saved, xb_in, dz, dzb, ln_in, tag, on_dwd=None, on_dh=None, on_dwu=None, on_dx=None):
    gate, up, down = names
    g, u, h, _, _ = saved
    f = ex[gate].shape[0]

    def request(fn):
        return (None, None) if fn is None else fn(ex)

    ex.grads[down] = _carried(request(on_dwd), lambda c: _dw(h, dzb, scale=0.5, name=f"{tag}_dwd", comm=c))
    dg, du = _carried(request(on_dh), lambda c: _ffn_bwd_dh(
        dzb, ex[down], g, u, scale=0.5, bm=1024, bn=_pick(f, (512, 256, 128)), name=f"{tag}_dh", chunks=4, comm=c))
    ex.grads[gate] = _dw(xb_in, dg, name=f"{tag}_dwg")
    ex.grads[up] = _carried(request(on_dwu), lambda c: _dw(xb_in, du, name=f"{tag}_dwu", comm=c))
    d = dz.shape[1]
    dx = _carried(request(on_dx), lambda c: _ffn_dx(
        dg, du, ex[gate], ex[up], dz, extra_scale=DEEPNORM_ALPHA,
        bm=512, bn=_pick(d, (512, 256, 128)), name=f"{tag}_dx", comm=c))
    return dx if ln_in is None else _ln_bwd(dx, *ln_in, bm=256, name=f"{tag}_ln_bwd")


def _local_step(x, p, target, positions, w):
    s, d = x.shape
    tabs = _rope_tables(positions)
    xb = x.astype(BF16)
    f = w["ffn1_w_gate"].shape[0]
    ffn_bn, ln_bn, ln_bn_short_k = _pick(f, (512, 256, 128)), _pick(d, (512, 256, 128)), _pick(d, (1024, 512, 256, 128))
    g1, u1, h1 = _carried(w.gather(("ffn1_w_down", "w_in", "w_out")), lambda c: _ffn_up(
        xb, w["ffn1_w_gate"], w["ffn1_w_up"], bm=1024, bn=ffn_bn, name="ffn1_up", comm=c))
    x1, x1b, xh1, rs1 = _carried(w.gather(("ffn2_w_gate", "ffn2_w_up")), lambda c: _mm_ln(
        h1, w["ffn1_w_down"], x, w["ln1_g"], w["ln1_b"], res_scale=DEEPNORM_ALPHA, mm_scale=0.5,
        bm=512, bn=ln_bn, name="ffn1_down_ln", comm=c))
    sv1 = (g1, u1, h1, xh1, rs1)
    pw = w["w_in"].shape[0]
    proj = _carried(w.gather(("ffn2_w_down", "w_ple_gate", "w_ple_proj")), lambda c: _mm(
        x1b, w["w_in"], tb=True, bm=1024, bn=_pick(pw, (512, 256, 128)), bk=d, name="in_proj", comm=c))
    nqk = (N_PATTERNS + 1) * N_KV_HEADS
    qkr = _rotary(proj, tabs, n_cols=nqk, inverse=False, out_dtype=F32, bs=1024, name="rotary")
    attn, lse = _attn_fwd(qkr, proj, name="attn_fwd")
    lru_w = (w["conv_w"], w["conv_b"], w["w_rgate"], w["b_rgate"], w["w_igate"], w["b_igate"], w["lru_lambda"])
    rec, hseq = _lru_fwd(proj, *lru_w, tt=512, name="lru_fwd")
    cat = jnp.concatenate([attn.astype(BF16), rec], axis=1)
    x2, x2b, xh2, rs2 = _mm_ln(cat, w["w_out"], x1, w["ln2_g"], w["ln2_b"], res_scale=DEEPNORM_ALPHA, mm_scale=1.0,
                               bm=512, bn=ln_bn_short_k, name="out_proj_ln")
    g2, u2, h2 = _ffn_up(x2b, w["ffn2_w_gate"], w["ffn2_w_up"], bm=1024, bn=ffn_bn, name="ffn2_up")
    x3, x3b, xh3, rs3 = _mm_ln(h2, w["ffn2_w_down"], x2, w["ln3_g"], w["ln3_b"], res_scale=DEEPNORM_ALPHA, mm_scale=0.5,
                               bm=512, bn=ln_bn, name="ffn2_down_ln")
    sv3 = (g2, u2, h2, xh3, rs3)
    lsum, dy, dgate, dple = _ple_loss(x3, x3b, p, w["w_ple_gate"], w["w_ple_proj"], target,
                                      bm=1024, bn=_pick(d, (512, 256, 128)), name="ple_loss")
    grads = w.grads
    grads["w_ple_gate"] = _dw(x3b, dgate, name="dw_ple_gate")
    grads["w_ple_proj"] = _dw(p, dple, name="dw_ple_proj")
    dz3, dz3b, grads["ln3_g"], grads["ln3_b"] = _carried(w.scatter(("w_ple_gate", "w_ple_proj")), lambda c: _mm_dx(
        dgate, w["w_ple_gate"], dy, xh3, rs3, w["ln3_g"], extra_scale=1.0, bm=512, bn=ln_bn_short_k,
        name="ple_dx", tb=True, comm=c))
    dz2, dz2b, grads["ln2_g"], grads["ln2_b"] = _ffn_bwd(
        w, ("ffn2_w_gate", "ffn2_w_up", "ffn2_w_down"), sv3, x2b, dz3, dz3b, (xh2, rs2, w["ln2_g"]), "ffn2",
        on_dx=lambda ex: ex.scatter(("ffn2_w_down",)))
    grads["w_out"] = _dw(cat, dz2b, name="dw_out")
    dcat = _mm(dz2b, w["w_out"], tb=True, bm=1024, bn=_pick(d, (512, 256, 128)), bk=d, name="out_proj_dx")
    dq0, dq1, dq2, dk, dvb = _carried(w.scatter(("ffn2_w_gate",)), lambda c: _attn_bwd(
        qkr, proj, attn, lse, dcat, name="attn_bwd", comm=c))
    nh = N_KV_HEADS
    dqkv = [_rotary(t, tabs, n_cols=nh, inverse=True, out_dtype=BF16, bs=1024, name=f"rotary_bwd{i}")
            for i, t in enumerate((dq0, dq1, dq2, dk))]
    (dxb, dyb, grads["conv_w"], grads["conv_b"], grads["w_rgate"], grads["b_rgate"], grads["w_igate"],
     grads["b_igate"], grads["lru_lambda"]) = _carried(w.scatter(("ffn2_w_up",)), lambda c: _lru_bwd(
         proj, hseq, dcat, *lru_w, tt=512, name="lru_bwd", comm=c))
    dproj = jnp.concatenate(dqkv + [dvb, dxb, dyb], axis=1)
    grads["w_in"] = _carried(w.scatter(("w_out",)), lambda c: _dw(x1b, dproj, name="dw_in", comm=c))
    dz1, dz1b, grads["ln1_g"], grads["ln1_b"] = _carried(w.scatter(("w_in",)), lambda c: _mm_dx(
        dproj, w["w_in"], dz2, xh1, rs1, w["ln1_g"], extra_scale=DEEPNORM_ALPHA,
        bm=512, bn=ln_bn, name="in_proj_dx", comm=c))
    grad_x = _ffn_bwd(w, ("ffn1_w_gate", "ffn1_w_up", "ffn1_w_down"), sv1, xb, dz1, dz1b, None, "ffn1",
                      on_dwd=lambda ex: ex.gather_small(),
                      on_dh=lambda ex: ex.scatter_start("ffn1_w_down"),
                      on_dwu=lambda ex: ex.scatter_start("ffn1_w_gate"),
                      on_dx=lambda ex: ex.scatter_start("ffn1_w_up"))
    return lsum, grad_x


def _to_full(name, gathered):
    if name in COLUMN_SHARDED:
        _, r, c = gathered.shape
        return jnp.transpose(gathered, (1, 0, 2)).reshape(r, N_DEV * c)
    return gathered.reshape((N_DEV * gathered.shape[1],) + gathered.shape[2:])


def _to_owner_blocks(name, full):
    if name in COLUMN_SHARDED:
        r, c = full.shape
        return jnp.transpose(full.reshape(r, N_DEV, c // N_DEV), (1, 0, 2))
    return full.reshape((N_DEV, full.shape[0] // N_DEV) + full.shape[1:])


def _rows128(a):
    flat = a.reshape(-1, LANES)
    pad = (-flat.shape[0]) % 8
    return jnp.pad(flat, ((0, pad), (0, 0))) if pad else flat


def kernel(x, p, positions, ffn1_w_gate, ffn1_w_up, ffn1_w_down, ln1_g, ln1_b, w_in, conv_w, conv_b, w_rgate, b_rgate, w_igate, b_igate, lru_lambda, w_out, ln2_g, ln2_b, ffn2_w_gate, ffn2_w_up, ffn2_w_down, ln3_g, ln3_b, w_ple_proj, w_ple_gate, loss_target, m_ffn1_w_gate, m_ffn1_w_up, m_ffn1_w_down, m_ln1_g, m_ln1_b, m_w_in, m_conv_w, m_conv_b, m_w_rgate, m_b_rgate, m_w_igate, m_b_igate, m_lru_lambda, m_w_out, m_ln2_g, m_ln2_b, m_ffn2_w_gate, m_ffn2_w_up, m_ffn2_w_down, m_ln3_g, m_ln3_b, m_w_ple_proj, m_w_ple_gate, v_ffn1_w_gate, v_ffn1_w_up, v_ffn1_w_down, v_ln1_g, v_ln1_b, v_w_in, v_conv_w, v_conv_b, v_w_rgate, v_b_rgate, v_w_igate, v_b_igate, v_lru_lambda, v_w_out, v_ln2_g, v_ln2_b, v_ffn2_w_gate, v_ffn2_w_up, v_ffn2_w_down, v_ln3_g, v_ln3_b, v_w_ple_proj, v_w_ple_gate):
    names = ("ffn1_w_gate", "ffn1_w_up", "ffn1_w_down", "ln1_g", "ln1_b", "w_in", "conv_w", "conv_b", "w_rgate",
             "b_rgate", "w_igate", "b_igate", "lru_lambda", "w_out", "ln2_g", "ln2_b", "ffn2_w_gate", "ffn2_w_up",
             "ffn2_w_down", "ln3_g", "ln3_b", "w_ple_proj", "w_ple_gate")
    ws = (ffn1_w_gate, ffn1_w_up, ffn1_w_down, ln1_g, ln1_b, w_in, conv_w, conv_b, w_rgate, b_rgate, w_igate, b_igate,
          lru_lambda, w_out, ln2_g, ln2_b, ffn2_w_gate, ffn2_w_up, ffn2_w_down, ln3_g, ln3_b, w_ple_proj, w_ple_gate)
    ms = (m_ffn1_w_gate, m_ffn1_w_up, m_ffn1_w_down, m_ln1_g, m_ln1_b, m_w_in, m_conv_w, m_conv_b, m_w_rgate, m_b_rgate,
          m_w_igate, m_b_igate, m_lru_lambda, m_w_out, m_ln2_g, m_ln2_b, m_ffn2_w_gate, m_ffn2_w_up, m_ffn2_w_down,
          m_ln3_g, m_ln3_b, m_w_ple_proj, m_w_ple_gate)
    vs = (v_ffn1_w_gate, v_ffn1_w_up, v_ffn1_w_down, v_ln1_g, v_ln1_b, v_w_in, v_conv_w, v_conv_b, v_w_rgate, v_b_rgate,
          v_w_igate, v_b_igate, v_lru_lambda, v_w_out, v_ln2_g, v_ln2_b, v_ffn2_w_gate, v_ffn2_w_up, v_ffn2_w_down,
          v_ln3_g, v_ln3_b, v_w_ple_proj, v_w_ple_gate)
    def local(a):
        return a[0] if a.ndim >= 3 else a

    w_loc = {n: local(a) for n, a in zip(names, ws)}
    m_loc = {n: local(a) for n, a in zip(names, ms)}
    v_loc = {n: local(a) for n, a in zip(names, vs)}
    out_shapes = {n: a.shape for n, a in zip(names, ws)}

    shards = {n: (w_loc[n].T if n in COLUMN_SHARDED else w_loc[n]).astype(BF16) for n in BIG_WEIGHTS}
    gate1, up1, conv_all = _all_gather([shards["ffn1_w_gate"], shards["ffn1_w_up"], w_loc["conv_w"]], name="gather_first")
    ex = _MeshExchange({n: w_loc[n] for n in SMALL_WEIGHTS}, shards)
    ex.full["conv_w"] = _to_full("conv_w", conv_all)
    ex.take("ffn1_w_gate", gate1)
    ex.take("ffn1_w_up", up1)

    lsum, grad_x = _local_step(x[0], p[0, 0], loss_target[0], positions[0], ex)
    ex.finish(grad_x)
    grads, reduced = ex.grads, ex.reduced
    d_model = x.shape[-1]
    loss = lax.psum(lsum[0, 0] * (0.5 / d_model), ("x", "y", "c"))

    small = SMALL_GRADS
    summed = _sum_parts(ex.small_parts, name="sum_small_grads")
    small_grads, row = {}, 0
    for n in small:
        rows = grads[n].size // LANES
        small_grads[n] = summed[row:row + rows].reshape(grads[n].shape)
        row += rows + (-rows) % 8
    me = 4 * lax.axis_index("x") + 2 * lax.axis_index("y") + lax.axis_index("c")
    cw_cols = w_loc["conv_w"].shape[1]
    small_grads["conv_w"] = lax.dynamic_slice_in_dim(small_grads["conv_w"], me * cw_cols, cw_cols, axis=1)

    out_g, out_d, out_m, out_v = {}, {}, {}, {}
    for n in names:
        wl, ml, vl = w_loc[n], m_loc[n], v_loc[n]
        shape = wl.shape
        if n in BIG_WEIGHTS:
            gparts = reduced[n]
        else:
            gparts = small_grads[n].reshape((1,) + shape)
        if wl.ndim == 3:
            wl, ml, vl = (t.reshape(-1, shape[-1]) for t in (wl, ml, vl))
            gparts = gparts.reshape(gparts.shape[0], -1, shape[-1])
        res = _adamw(wl, ml, vl, gparts, name=f"adamw_{n}")
        out_g[n], out_d[n], out_m[n], out_v[n] = (t.reshape(out_shapes[n]) for t in res)

    return (loss, grad_x[None], *[out_g[n] for n in names], *[out_d[n] for n in names],
            *[out_m[n] for n in names], *[out_v[n] for n in names])
```

```python
import jax
import jax.numpy as jnp
from jax import lax
from jax.experimental import pallas as pl
from jax.experimental.pallas import tpu as pltpu

F32 = jnp.float32
BF16 = jnp.bfloat16

N_DEV = 8
LANES = 128
MIB = 1 << 20

HEAD_DIM = 128
N_KV_HEADS = 4
DILATIONS = (1, 4, 16)
N_PATTERNS = 3
SPAN = 128
ROT_DIMS = 32
ROPE_THETA = 500000.0
LRU_C = 8.0
CONV_WIDTH = 4
LN_EPS = 1e-5
DEEPNORM_ALPHA = 2.0 ** 0.25
ATTN_TILE = SPAN * DILATIONS[-1]

ADAM_LR = 0.001
ADAM_B1 = 0.9
ADAM_B2 = 0.999
ADAM_EPS = 1e-08
ADAM_WD = 0.01
ADAM_STEP = 10

MESH = pl.DeviceIdType.MESH
NT_DIMS = (((1,), (1,)), ((), ()))
EPILOGUE_ROWS = 64


def _cp(semantics, vmem_mib):
    return pltpu.CompilerParams(dimension_semantics=semantics, vmem_limit_bytes=vmem_mib * MIB)


def _pick(n, candidates):
    for c in candidates:
        if n % c == 0:
            return c
    return n


class _Comm:
    def __init__(self, arrays, out_shapes, scratch, start, end, mid=None, split=False):
        self.arrays, self.out_shapes, self.scratch = list(arrays), list(out_shapes), list(scratch)
        self.start, self.mid, self.end, self.split = start, mid, end, split


def _call(body, *, name, grid, in_specs, out_specs, out_shape, args, scratch_shapes=(), vmem_mib, comm=None):
    single = not isinstance(out_shape, (tuple, list))
    out_shape_t = (out_shape,) if single else tuple(out_shape)
    out_specs_t = (out_specs,) if single else tuple(out_specs)
    params = _cp(("arbitrary",) * len(grid), vmem_mib)
    if comm is None:
        res = pl.pallas_call(body, name=name, grid=grid, in_specs=list(in_specs), out_specs=out_specs_t,
                             out_shape=out_shape_t, scratch_shapes=list(scratch_shapes), compiler_params=params)(*args)
        return res[0] if single else res
    n_in, n_out, n_scr = len(args), len(out_shape_t), len(scratch_shapes)
    nci, nco = len(comm.arrays), len(comm.out_shapes)
    total = 1
    for g in grid:
        total *= g

    def wrapped(*refs):
        ins, refs = refs[:n_in], refs[n_in:]
        cin, refs = refs[:nci], refs[nci:]
        outs, refs = refs[:n_out], refs[n_out:]
        cout, refs = refs[:nco], refs[nco:]
        scr, csem = refs[:n_scr], refs[n_scr:]
        step = pl.program_id(0)
        for ax in range(1, len(grid)):
            step = step * grid[ax] + pl.program_id(ax)

        @pl.when(step == 0)
        def _():
            comm.start(cin, cout, csem)

        body(*ins, *outs, *scr)
        if comm.mid is not None:
            @pl.when(step == (3 * total) // 4)
            def _():
                comm.mid(cin, cout, csem)

        @pl.when(step == total - 1)
        def _():
            comm.end(cin, cout, csem)

    hbm = pl.BlockSpec(memory_space=pltpu.HBM)
    if comm.split:
        sem = pl.BlockSpec(memory_space=pltpu.SEMAPHORE)
        n_sems = nco - nci
        res = pl.pallas_call(
            wrapped, name=name, grid=grid,
            in_specs=list(in_specs) + [hbm] * nci,
            out_specs=out_specs_t + (sem,) * n_sems + (hbm,) * nci,
            out_shape=out_shape_t + tuple(comm.out_shapes),
            scratch_shapes=list(scratch_shapes) + comm.scratch,
            input_output_aliases={n_in + k: n_out + n_sems + k for k in range(nci)},
            compiler_params=pltpu.CompilerParams(
                dimension_semantics=("arbitrary",) * len(grid), vmem_limit_bytes=vmem_mib * MIB,
                has_side_effects=pltpu.SideEffectType.DATAFLOW_SIDE_EFFECTING),
        )(*args, *[pltpu.with_memory_space_constraint(a, pltpu.HBM) for a in comm.arrays])
    else:
        res = pl.pallas_call(
            wrapped, name=name, grid=grid,
            in_specs=list(in_specs) + [hbm] * nci,
            out_specs=out_specs_t + (hbm,) * nco,
            out_shape=out_shape_t + tuple(comm.out_shapes),
            scratch_shapes=list(scratch_shapes) + comm.scratch,
            compiler_params=params)(*args, *comm.arrays)
    own, extra = res[:n_out], res[n_out:]
    return (own[0] if single else own), extra


def _mm(a, b, *, ta=False, tb=False, out_dtype=F32, scale=1.0, bm, bn, bk, name, comm=None):
    m, k = (a.shape[1], a.shape[0]) if ta else a.shape
    n = b.shape[0] if tb else b.shape[1]
    bm, bn, bk = min(bm, m), min(bn, n), min(bk, k)
    assert m % bm == 0 and n % bn == 0 and k % bk == 0, (name, m, n, k, bm, bn, bk)
    nk = k // bk
    a_spec = pl.BlockSpec((bk, bm), lambda i, j, kk: (kk, i)) if ta else pl.BlockSpec((bm, bk), lambda i, j, kk: (i, kk))
    b_spec = pl.BlockSpec((bn, bk), lambda i, j, kk: (j, kk)) if tb else pl.BlockSpec((bk, bn), lambda i, j, kk: (kk, j))
    dn = (((0 if ta else 1,), (1 if tb else 0,)), ((), ()))

    def body(a_ref, b_ref, o_ref, *acc):
        part = lax.dot_general(a_ref[...].astype(BF16), b_ref[...].astype(BF16), dn, preferred_element_type=F32)
        if nk == 1:
            o_ref[...] = (part * scale).astype(out_dtype)
            return
        acc_ref, = acc
        kk = pl.program_id(2)

        @pl.when(kk == 0)
        def _():
            acc_ref[...] = part

        @pl.when(kk > 0)
        def _():
            acc_ref[...] += part

        @pl.when(kk == nk - 1)
        def _():
            o_ref[...] = (acc_ref[...] * scale).astype(out_dtype)

    return _call(
        body, name=name,
        out_shape=jax.ShapeDtypeStruct((m, n), out_dtype),
        grid=(m // bm, n // bn, nk),
        in_specs=[a_spec, b_spec],
        out_specs=pl.BlockSpec((bm, bn), lambda i, j, kk: (i, j)),
        scratch_shapes=[pltpu.VMEM((bm, bn), F32)] if nk > 1 else [],
        args=(a, b), vmem_mib=56, comm=comm)


def _ffn_up(xb, wg, wu, *, bm, bn, name, comm=None):
    s, d = xb.shape
    f = wg.shape[0]
    bm, bn = min(bm, s), min(bn, f)
    assert s % bm == 0 and f % bn == 0

    def body(x_ref, wg_ref, wu_ref, hg_ref, hu_ref, h_ref):
        x = x_ref[...]
        g = lax.dot_general(x, wg_ref[...], NT_DIMS, preferred_element_type=F32)
        u = lax.dot_general(x, wu_ref[...], NT_DIMS, preferred_element_type=F32)
        sig = jax.nn.sigmoid(g)
        silu = g * sig
        hg_ref[...] = (u * (sig * (1.0 + g * (1.0 - sig)))).astype(BF16)
        hu_ref[...] = silu.astype(BF16)
        h_ref[...] = (silu * u).astype(BF16)

    out = jax.ShapeDtypeStruct((s, f), BF16)
    blk = pl.BlockSpec((bm, bn), lambda i, j: (i, j))
    return _call(
        body, name=name, out_shape=(out, out, out),
        grid=(s // bm, f // bn),
        in_specs=[pl.BlockSpec((bm, d), lambda i, j: (i, 0)),
                  pl.BlockSpec((bn, d), lambda i, j: (j, 0)),
                  pl.BlockSpec((bn, d), lambda i, j: (j, 0))],
        out_specs=(blk, blk, blk),
        args=(xb, wg, wu), vmem_mib=56, comm=comm)


def _ffn_bwd_dh(dzb, wd, g, u, *, scale, bm, bn, name, chunks=2, comm=None):
    s, d = dzb.shape
    f = wd.shape[0]
    bm, bn = min(bm, s), min(bn, f)
    assert s % bm == 0 and f % bn == 0

    cr = bm // chunks

    def body(dz_ref, wd_ref, hg_ref, hu_ref, dg_ref, du_ref):
        for r in range(chunks):
            rows = slice(r * cr, (r + 1) * cr)
            dh = lax.dot_general(dz_ref[rows, :], wd_ref[...], NT_DIMS, preferred_element_type=F32) * scale
            dg_ref[rows, :] = (dh * hg_ref[rows, :].astype(F32)).astype(BF16)
            du_ref[rows, :] = (dh * hu_ref[rows, :].astype(F32)).astype(BF16)

    out = jax.ShapeDtypeStruct((s, f), BF16)
    blk = pl.BlockSpec((bm, bn), lambda i, j: (i, j))
    return _call(
        body, name=name, out_shape=(out, out),
        grid=(s // bm, f // bn),
        in_specs=[pl.BlockSpec((bm, d), lambda i, j: (i, 0)),
                  pl.BlockSpec((bn, d), lambda i, j: (j, 0)), blk, blk],
        out_specs=(blk, blk),
        args=(dzb, wd, g, u), vmem_mib=56, comm=comm)


def _full_rows(acc_ref, rows, nj):
    return jnp.concatenate([acc_ref[jj, rows, :] for jj in range(nj)], axis=1)


def _mm_ln(a, b, res, gamma, beta, *, res_scale, mm_scale, bm, bn, name, comm=None):
    s, k = a.shape
    d = b.shape[1]
    bm, bn = min(bm, s), min(bn, d)
    assert s % bm == 0 and d % bn == 0
    nj = d // bn
    ch = min(EPILOGUE_ROWS, bm)

    def body(a_ref, b_ref, r_ref, g_ref, be_ref, y_ref, yb_ref, xh_ref, rs_ref, acc_ref):
        j = pl.program_id(1)
        acc_ref[j] = jnp.dot(a_ref[...], b_ref[...], preferred_element_type=F32)

        @pl.when(j == nj - 1)
        def _():
            def chunk(ci, carry):
                rows = pl.ds(pl.multiple_of(ci * ch, ch), ch)
                z = res_scale * r_ref[rows, :] + mm_scale * _full_rows(acc_ref, rows, nj)
                mu = jnp.mean(z, axis=-1, keepdims=True)
                zc = z - mu
                var = jnp.mean(zc * zc, axis=-1, keepdims=True)
                rstd = lax.rsqrt(var + LN_EPS)
                xh = zc * rstd
                y = xh * g_ref[...] + be_ref[...]
                y_ref[rows, :] = y
                yb_ref[rows, :] = y.astype(BF16)
                xh_ref[rows, :] = xh
                rs_ref[rows, :] = rstd
                return carry

            lax.fori_loop(0, bm // ch, chunk, 0)

    row = pl.BlockSpec((bm, d), lambda i, j: (i, 0))
    vec = pl.BlockSpec((1, d), lambda i, j: (0, 0))
    return _call(
        body, name=name,
        out_shape=(jax.ShapeDtypeStruct((s, d), F32), jax.ShapeDtypeStruct((s, d), BF16),
                   jax.ShapeDtypeStruct((s, d), F32), jax.ShapeDtypeStruct((s, 1), F32)),
        grid=(s // bm, nj),
        in_specs=[pl.BlockSpec((bm, k), lambda i, j: (i, 0)),
                  pl.BlockSpec((k, bn), lambda i, j: (0, j)), row, vec, vec],
        out_specs=(row, row, row, pl.BlockSpec((bm, 1), lambda i, j: (i, 0))),
        scratch_shapes=[pltpu.VMEM((nj, bm, bn), F32)],
        args=(a, b, res, gamma, beta), vmem_mib=58, comm=comm)


def _mm_dx(a, wt, extra, xhat, rstd, gamma, *, extra_scale, bm, bn, name, tb=False, comm=None):
    s, k = a.shape
    d = wt.shape[0] if tb else wt.shape[1]
    bm, bn = min(bm, s), min(bn, d)
    assert s % bm == 0 and d % bn == 0
    nj = d // bn
    ch = min(EPILOGUE_ROWS, bm)
    dims = NT_DIMS if tb else (((1,), (0,)), ((), ()))

    def body(a_ref, w_ref, e_ref, xh_ref, rs_ref, g_ref, dz_ref, dzb_ref, dg_ref, db_ref, acc_ref):
        i = pl.program_id(0)
        j = pl.program_id(1)
        acc_ref[j] = lax.dot_general(a_ref[...], w_ref[...], dims, preferred_element_type=F32)

        @pl.when(j == nj - 1)
        def _():
            def chunk(ci, carry):
                dgp, dbp = carry
                rows = pl.ds(pl.multiple_of(ci * ch, ch), ch)
                dx = extra_scale * e_ref[rows, :] + _full_rows(acc_ref, rows, nj)
                xh = xh_ref[rows, :]
                dxh = dx * g_ref[...]
                m1 = jnp.mean(dxh, axis=-1, keepdims=True)
                m2 = jnp.mean(dxh * xh, axis=-1, keepdims=True)
                dz = rs_ref[rows, :] * (dxh - m1 - xh * m2)
                dz_ref[rows, :] = dz
                dzb_ref[rows, :] = dz.astype(BF16)
                return dgp + jnp.sum(dx * xh, axis=0, keepdims=True), dbp + jnp.sum(dx, axis=0, keepdims=True)

            zero = jnp.zeros((1, d), F32)
            dgp, dbp = lax.fori_loop(0, bm // ch, chunk, (zero, zero))

            @pl.when(i == 0)
            def _():
                dg_ref[...] = dgp
                db_ref[...] = dbp

            @pl.when(i > 0)
            def _():
                dg_ref[...] += dgp
                db_ref[...] += dbp

    row = pl.BlockSpec((bm, d), lambda i, j: (i, 0))
    vec = pl.BlockSpec((1, d), lambda i, j: (0, 0))
    return _call(
        body, name=name,
        out_shape=(jax.ShapeDtypeStruct((s, d), F32), jax.ShapeDtypeStruct((s, d), BF16),
                   jax.ShapeDtypeStruct((1, d), F32), jax.ShapeDtypeStruct((1, d), F32)),
        grid=(s // bm, nj),
        in_specs=[pl.BlockSpec((bm, k), lambda i, j: (i, 0)),
                  pl.BlockSpec((bn, k), lambda i, j: (j, 0)) if tb else pl.BlockSpec((k, bn), lambda i, j: (0, j)),
                  row, row, pl.BlockSpec((bm, 1), lambda i, j: (i, 0)), vec],
        out_specs=(row, row, vec, vec),
        scratch_shapes=[pltpu.VMEM((nj, bm, bn), F32)],
        args=(a, wt, extra, xhat, rstd, gamma), vmem_mib=58, comm=comm)


def _ffn_dx(dg, du, wgt, wut, extra, *, extra_scale, bm, bn, name, comm=None):
    s, f = dg.shape
    d = wgt.shape[1]
    bm, bn = min(bm, s), min(bn, d)
    assert s % bm == 0 and d % bn == 0

    def body(dg_ref, du_ref, wg_ref, wu_ref, e_ref, o_ref):
        acc = jnp.dot(dg_ref[...], wg_ref[...], preferred_element_type=F32)
        acc = acc + jnp.dot(du_ref[...], wu_ref[...], preferred_element_type=F32)
        o_ref[...] = extra_scale * e_ref[...] + acc

    rows = pl.BlockSpec((bm, f), lambda i, j: (i, 0))
    cols = pl.BlockSpec((f, bn), lambda i, j: (0, j))
    blk = pl.BlockSpec((bm, bn), lambda i, j: (i, j))
    return _call(
        body, name=name, out_shape=jax.ShapeDtypeStruct((s, d), F32),
        grid=(s // bm, d // bn), in_specs=[rows, rows, cols, cols, blk], out_specs=blk,
        args=(dg, du, wgt, wut, extra), vmem_mib=58, comm=comm)


def _ln_bwd(dx, xhat, rstd, gamma, *, bm, name):
    s, d = dx.shape
    bm = min(bm, s)
    assert s % bm == 0
    ch = min(EPILOGUE_ROWS, bm)

    def body(dx_ref, xh_ref, rs_ref, g_ref, dz_ref, dzb_ref, dg_ref, db_ref):
        def chunk(ci, carry):
            dgp, dbp = carry
            rows = pl.ds(pl.multiple_of(ci * ch, ch), ch)
            dxv = dx_ref[rows, :]
            xh = xh_ref[rows, :]
            dxh = dxv * g_ref[...]
            m1 = jnp.mean(dxh, axis=-1, keepdims=True)
            m2 = jnp.mean(dxh * xh, axis=-1, keepdims=True)
            dz = rs_ref[rows, :] * (dxh - m1 - xh * m2)
            dz_ref[rows, :] = dz
            dzb_ref[rows, :] = dz.astype(BF16)
            return dgp + jnp.sum(dxv * xh, axis=0, keepdims=True), dbp + jnp.sum(dxv, axis=0, keepdims=True)

        zero = jnp.zeros((1, d), F32)
        dgp, dbp = lax.fori_loop(0, bm // ch, chunk, (zero, zero))
        i = pl.program_id(0)

        @pl.when(i == 0)
        def _():
            dg_ref[...] = dgp
            db_ref[...] = dbp

        @pl.when(i > 0)
        def _():
            dg_ref[...] += dgp
            db_ref[...] += dbp

    row = pl.BlockSpec((bm, d), lambda i: (i, 0))
    vec = pl.BlockSpec((1, d), lambda i: (0, 0))
    return _call(
        body, name=name,
        out_shape=(jax.ShapeDtypeStruct((s, d), F32), jax.ShapeDtypeStruct((s, d), BF16),
                   jax.ShapeDtypeStruct((1, d), F32), jax.ShapeDtypeStruct((1, d), F32)),
        grid=(s // bm,), in_specs=[row, row, pl.BlockSpec((bm, 1), lambda i: (i, 0)), vec],
        out_specs=(row, row, vec, vec), args=(dx, xhat, rstd, gamma), vmem_mib=48)


def _ple_loss(x3, x3b, p, wpg, wpp, target, *, bm, bn, name):
    s, d = x3.shape
    dp = p.shape[1]
    bm, bn = min(bm, s), min(bn, d)
    assert s % bm == 0 and d % bn == 0
    inv_d = 1.0 / d
    chunks = 4 if bm % 64 == 0 else 1
    cr = bm // chunks

    def body(x_ref, xb_ref, p_ref, wg_ref, wp_ref, t_ref, l_ref, dy_ref, dg_ref, dp_ref):
        first = (pl.program_id(0) == 0) & (pl.program_id(1) == 0)

        @pl.when(first)
        def _():
            l_ref[...] = jnp.zeros_like(l_ref)

        part = 0.0
        for r in range(chunks):
            rows = slice(r * cr, (r + 1) * cr)
            gp = jnp.dot(xb_ref[rows, :], wg_ref[...], preferred_element_type=F32)
            pp = lax.dot_general(p_ref[rows, :].astype(BF16), wp_ref[...], NT_DIMS, preferred_element_type=F32)
            sig = jax.nn.sigmoid(gp)
            err = x_ref[rows, :] + sig * pp - t_ref[rows, :]
            part = part + jnp.sum(err * err)
            dy = err * inv_d
            dy_ref[rows, :] = dy
            dg_ref[rows, :] = (dy * pp * sig * (1.0 - sig)).astype(BF16)
            dp_ref[rows, :] = (dy * sig).astype(BF16)
        l_ref[...] += part

    blk = pl.BlockSpec((bm, bn), lambda i, j: (i, j))
    return pl.pallas_call(
        body, name=name,
        out_shape=(jax.ShapeDtypeStruct((8, LANES), F32), jax.ShapeDtypeStruct((s, d), F32),
                   jax.ShapeDtypeStruct((s, d), BF16), jax.ShapeDtypeStruct((s, d), BF16)),
        grid=(s // bm, d // bn),
        in_specs=[blk, pl.BlockSpec((bm, d), lambda i, j: (i, 0)), pl.BlockSpec((bm, dp), lambda i, j: (i, 0)),
                  pl.BlockSpec((d, bn), lambda i, j: (0, j)), pl.BlockSpec((bn, dp), lambda i, j: (j, 0)), blk],
        out_specs=(pl.BlockSpec((8, LANES), lambda i, j: (0, 0)), blk, blk, blk),
        compiler_params=_cp(("arbitrary", "arbitrary"), 56),
    )(x3, x3b, p, wpg, wpp, target)


def _rope_tables(positions):
    half = ROT_DIMS // 2
    lane = jnp.arange(HEAD_DIM)
    inv_freq = jnp.power(jnp.float32(ROPE_THETA), -(lane % half).astype(F32) * (2.0 / ROT_DIMS))
    ang = positions.astype(F32)[:, None] * inv_freq
    cos, sin = jnp.cos(ang), jnp.sin(ang)
    cf = jnp.where(lane < ROT_DIMS, cos, 1.0)
    sa = jnp.where(lane < half, -sin, 0.0)
    sb = jnp.where((lane >= half) & (lane < ROT_DIMS), sin, 0.0)
    return cf, sa, sb


def _rotary(t, tabs, *, n_cols, inverse, out_dtype, bs, name):
    s = t.shape[0]
    bs = min(bs, s)
    half = ROT_DIMS // 2
    heads = N_KV_HEADS
    assert n_cols % heads == 0

    def body(t_ref, cf_ref, sa_ref, sb_ref, o_ref):
        cf, sa, sb = cf_ref[...], sa_ref[...], sb_ref[...]
        for hd in range(heads):
            lanes = slice(hd * HEAD_DIM, (hd + 1) * HEAD_DIM)
            v = t_ref[:, lanes]
            if inverse:
                o = v * cf + pltpu.roll(v * sa, half, 1) + pltpu.roll(v * sb, HEAD_DIM - half, 1)
            else:
                o = v * cf + pltpu.roll(v, HEAD_DIM - half, 1) * sa + pltpu.roll(v, half, 1) * sb
            o_ref[:, lanes] = o.astype(out_dtype)

    blk = pl.BlockSpec((bs, heads * HEAD_DIM), lambda i, j: (i, j))
    tab = pl.BlockSpec((bs, HEAD_DIM), lambda i, j: (i, 0))
    return pl.pallas_call(
        body, name=name, out_shape=jax.ShapeDtypeStruct((s, n_cols * HEAD_DIM), out_dtype),
        grid=(s // bs, n_cols // heads), in_specs=[blk, tab, tab, tab], out_specs=blk,
        compiler_params=_cp(("parallel", "arbitrary"), 32),
    )(t, *tabs)


def _attn_blocks():
    out = []
    for g, dil in enumerate(DILATIONS):
        sup = SPAN * dil
        for j in range(ATTN_TILE // sup):
            for r in range(dil):
                out.append((g, j * sup + r, dil, (j - 1) * sup + r if j > 0 else None, ATTN_TILE - sup + r))
    return out


def _rows(ref, start, dil, lead=None):
    idx = pl.ds(start, SPAN, stride=dil) if dil > 1 else pl.ds(start, SPAN)
    return ref[idx, :] if lead is None else ref[lead, idx, :]


def _band_masks(n):
    qi = lax.broadcasted_iota(jnp.int32, (SPAN, 2 * SPAN), 0)
    ki = lax.broadcasted_iota(jnp.int32, (SPAN, 2 * SPAN), 1)
    band = (ki >= qi) & (ki <= qi + SPAN)
    return band, band & ((ki >= SPAN) | (n > 0))


def _attn_fwd(qkr, proj, *, name):
    s = qkr.shape[0]
    t = ATTN_TILE
    assert s % t == 0
    nt = s // t
    scale = HEAD_DIM ** -0.5
    kcol, vcol = N_PATTERNS * N_KV_HEADS, (N_PATTERNS + 1) * N_KV_HEADS
    blocks = _attn_blocks()

    def body(q0, q1, q2, kc_ref, kp_ref, vc_ref, vp_ref, o_ref, l_ref, og, lg):
        n = pl.program_id(1)
        band, band_first = _band_masks(n)
        q_refs = (q0, q1, q2)
        for g, start, dil, prev_in_tile, prev_start in blocks:
            q = _rows(q_refs[g], start, dil).astype(BF16)
            if prev_in_tile is not None:
                kp, vp, mask = _rows(kc_ref, prev_in_tile, dil), _rows(vc_ref, prev_in_tile, dil), band
            else:
                kp, vp, mask = _rows(kp_ref, prev_start, dil), _rows(vp_ref, prev_start, dil), band_first
            kk = jnp.concatenate([kp, _rows(kc_ref, start, dil)], axis=0).astype(BF16)
            vv = jnp.concatenate([vp, _rows(vc_ref, start, dil)], axis=0).astype(BF16)
            sc = lax.dot_general(q, kk, (((1,), (1,)), ((), ())), preferred_element_type=F32) * scale
            sc = jnp.where(mask, sc, -1e30)
            m = jnp.max(sc, axis=-1, keepdims=True)
            e = jnp.exp(sc - m)
            den = jnp.sum(e, axis=-1, keepdims=True)
            o = jnp.dot(e.astype(BF16), vv, preferred_element_type=F32) / den
            idx = pl.ds(start, SPAN, stride=dil) if dil > 1 else pl.ds(start, SPAN)
            og[g, idx, :] = o
            lg[g, idx, :] = jnp.broadcast_to(m + jnp.log(den), (SPAN, HEAD_DIM))
        l0, l1, l2 = lg[0], lg[1], lg[2]
        m = jnp.maximum(jnp.maximum(l0, l1), l2)
        w0, w1, w2 = jnp.exp(l0 - m), jnp.exp(l1 - m), jnp.exp(l2 - m)
        den = w0 + w1 + w2
        o_ref[...] = (w0 * og[0] + w1 * og[1] + w2 * og[2]) / den
        l_ref[...] = m + jnp.log(den)

    def col(c, prev=False):
        if prev:
            return pl.BlockSpec((t, HEAD_DIM), lambda h, n: (jnp.maximum(n - 1, 0), c + h))
        return pl.BlockSpec((t, HEAD_DIM), lambda h, n: (n, c + h))

    out = jax.ShapeDtypeStruct((s, N_KV_HEADS * HEAD_DIM), F32)
    return pl.pallas_call(
        body, name=name, out_shape=(out, out),
        grid=(N_KV_HEADS, nt),
        in_specs=[col(0), col(N_KV_HEADS), col(2 * N_KV_HEADS), col(kcol), col(kcol, True), col(vcol), col(vcol, True)],
        out_specs=(col(0), col(0)),
        scratch_shapes=[pltpu.VMEM((N_PATTERNS, t, HEAD_DIM), F32), pltpu.VMEM((N_PATTERNS, t, HEAD_DIM), F32)],
        compiler_params=_cp(("parallel", "arbitrary"), 48),
    )(qkr, qkr, qkr, qkr, qkr, proj, proj)


def _attn_bwd(qkr, proj, attn, lse, dcat, *, name, comm=None):
    s = qkr.shape[0]
    t = ATTN_TILE
    nt = s // t
    scale = HEAD_DIM ** -0.5
    kcol, vcol = N_PATTERNS * N_KV_HEADS, (N_PATTERNS + 1) * N_KV_HEADS
    blocks = _attn_blocks()

    def body(q0, q1, q2, kc_ref, kp_ref, vc_ref, vp_ref, o_ref, l_ref, do_ref,
             dq0, dq1, dq2, dk_ref, dv_ref, ck, cv, tkc, tvc, tkp, tvp):
        n = pl.program_id(1)
        for ref in (tkc, tvc, tkp, tvp):
            ref[...] = jnp.zeros_like(ref)

        @pl.when(n < nt)
        def _():
            band, band_first = _band_masks(n)
            q_refs, dq_refs = (q0, q1, q2), (dq0, dq1, dq2)
            for g, start, dil, prev_in_tile, prev_start in blocks:
                idx = pl.ds(start, SPAN, stride=dil) if dil > 1 else pl.ds(start, SPAN)
                q = q_refs[g][idx, :].astype(BF16)
                if prev_in_tile is not None:
                    kp, vp, mask = _rows(kc_ref, prev_in_tile, dil), _rows(vc_ref, prev_in_tile, dil), band
                else:
                    kp, vp, mask = _rows(kp_ref, prev_start, dil), _rows(vp_ref, prev_start, dil), band_first
                kk = jnp.concatenate([kp, kc_ref[idx, :]], axis=0).astype(BF16)
                vv = jnp.concatenate([vp, vc_ref[idx, :]], axis=0).astype(BF16)
                do = do_ref[idx, :]
                dsum = jnp.sum(do * o_ref[idx, :], axis=-1, keepdims=True)
                lrow = l_ref[idx, :][:, :1]
                dob = do.astype(BF16)
                sc = lax.dot_general(q, kk, (((1,), (1,)), ((), ())), preferred_element_type=F32) * scale
                p = jnp.where(mask, jnp.exp(sc - lrow), 0.0)
                dp = lax.dot_general(dob, vv, (((1,), (1,)), ((), ())), preferred_element_type=F32)
                ds = (p * (dp - dsum) * scale).astype(BF16)
                pb = p.astype(BF16)
                dq_refs[g][idx, :] = jnp.dot(ds, kk, preferred_element_type=F32)
                dkk = lax.dot_general(ds, q, (((0,), (0,)), ((), ())), preferred_element_type=F32)
                dvv = lax.dot_general(pb, dob, (((0,), (0,)), ((), ())), preferred_element_type=F32)
                tkc[idx, :] += dkk[SPAN:]
                tvc[idx, :] += dvv[SPAN:]
                if prev_in_tile is not None:
                    pidx = pl.ds(prev_in_tile, SPAN, stride=dil) if dil > 1 else pl.ds(prev_in_tile, SPAN)
                    tkc[pidx, :] += dkk[:SPAN]
                    tvc[pidx, :] += dvv[:SPAN]
                else:
                    pidx = pl.ds(prev_start, SPAN, stride=dil) if dil > 1 else pl.ds(prev_start, SPAN)
                    tkp[pidx, :] += dkk[:SPAN]
                    tvp[pidx, :] += dvv[:SPAN]

        @pl.when(n > 0)
        def _():
            dk_ref[...] = ck[...] + tkp[...]
            dv_ref[...] = (cv[...] + tvp[...]).astype(BF16)

        ck[...] = tkc[...]
        cv[...] = tvc[...]

    def col(c, prev=False):
        if prev:
            return pl.BlockSpec((t, HEAD_DIM), lambda h, n: (jnp.maximum(jnp.minimum(n, nt - 1) - 1, 0), c + h))
        return pl.BlockSpec((t, HEAD_DIM), lambda h, n: (jnp.minimum(n, nt - 1), c + h))

    kv_out = pl.BlockSpec((t, HEAD_DIM), lambda h, n: (jnp.maximum(n - 1, 0), h))
    tile = pltpu.VMEM((t, HEAD_DIM), F32)
    per_head = jax.ShapeDtypeStruct((s, N_KV_HEADS * HEAD_DIM), F32)
    return _call(
        body, name=name,
        out_shape=(per_head, per_head, per_head, per_head, jax.ShapeDtypeStruct((s, N_KV_HEADS * HEAD_DIM), BF16)),
        grid=(N_KV_HEADS, nt + 1),
        in_specs=[col(0), col(N_KV_HEADS), col(2 * N_KV_HEADS), col(kcol), col(kcol, True), col(vcol), col(vcol, True),
                  col(0), col(0), col(0)],
        out_specs=(col(0), col(0), col(0), kv_out, kv_out),
        scratch_shapes=[tile] * 6,
        args=(qkr, qkr, qkr, qkr, qkr, proj, proj, attn, lse, dcat), vmem_mib=48, comm=comm)


GELU_C0 = 0.7978845608028654
GELU_C1 = 0.044715


def _softplus_neg(lam):
    y = jnp.exp(-jnp.abs(lam))
    w = 1.0 + y
    log1p = jnp.where(w == 1.0, y, jnp.log(w) * (y / jnp.where(w == 1.0, 1.0, w - 1.0)))
    return jnp.maximum(-lam, 0.0) + log1p


def _down(cur, prev, k, row):
    if k == 0:
        return cur
    return jnp.where(row < k, pltpu.roll(prev, k, 0), pltpu.roll(cur, k, 0))


def _up(cur, nxt, k, row, tt):
    if k == 0:
        return cur
    return jnp.where(row >= tt - k, pltpu.roll(nxt, tt - k, 0), pltpu.roll(cur, tt - k, 0))


def _lru_gates(x, xp, cw, cb, wr, br, wi, bi, lam, row):
    shifts = [_down(x, xp, k, row) for k in range(CONV_WIDTH)]
    xc = cb
    for j in range(CONV_WIDTH):
        xc = xc + cw[j:j + 1, :] * shifts[CONV_WIDTH - 1 - j]
    xcb = xc.astype(BF16)
    r = jax.nn.sigmoid(jnp.dot(xcb, wr, preferred_element_type=F32) + br)
    i = jax.nn.sigmoid(jnp.dot(xcb, wi, preferred_element_type=F32) + bi)
    c = -LRU_C * _softplus_neg(lam)
    la = c * r
    a = jnp.exp(la)
    mult = jnp.sqrt(jnp.tanh(-la) * (a * a + 1.0))
    return shifts, xc, xcb, r, i, c, a, mult


def _lru_fwd(proj, cw, cb, wr, br, wi, bi, lam, *, tt, name):
    s = proj.shape[0]
    nblk = wr.shape[0]
    c = nblk * LANES
    tt = min(tt, s)
    xcol0 = (N_PATTERNS + 2) * N_KV_HEADS
    ycol0 = xcol0 + nblk

    def body(x_ref, y_ref, cw_ref, cb_ref, wr_ref, br_ref, wi_ref, bi_ref, lam_ref, rec_ref, h_ref, xprev, hc):
        n = pl.program_id(1)

        @pl.when(n == 0)
        def _():
            xprev[...] = jnp.zeros_like(xprev)
            hc[...] = jnp.zeros_like(hc)

        row = lax.broadcasted_iota(jnp.int32, (tt, LANES), 0)
        x = x_ref[...]
        _, xc, _, _, i, _, a, mult = _lru_gates(
            x, xprev[...], cw_ref[...], cb_ref[...], wr_ref[0].astype(BF16), br_ref[...],
            wi_ref[0].astype(BF16), bi_ref[...], lam_ref[...], row)
        av, bv = a, mult * (i * xc)
        k = 1
        while k < tt:
            bs = jnp.where(row < k, 0.0, pltpu.roll(bv, k, 0))
            as_ = jnp.where(row < k, 1.0, pltpu.roll(av, k, 0))
            bv = bv + av * bs
            av = av * as_
            k *= 2
        h = bv + av * hc[0:1, :]
        hc[...] = jnp.broadcast_to(h[tt - 1:tt, :], hc.shape)
        h_ref[...] = h
        y = y_ref[...]
        gel = 0.5 * y * (1.0 + jnp.tanh(GELU_C0 * (y + GELU_C1 * y * y * y)))
        rec_ref[...] = (h * gel).astype(BF16)
        xprev[...] = x

    vec = pl.BlockSpec((1, LANES), lambda b, n: (0, b))
    wblk = pl.BlockSpec((1, LANES, LANES), lambda b, n: (b, 0, 0))
    out = pl.BlockSpec((tt, LANES), lambda b, n: (n, b))
    return pl.pallas_call(
        body, name=name,
        out_shape=(jax.ShapeDtypeStruct((s, c), BF16), jax.ShapeDtypeStruct((s, c), F32)),
        grid=(nblk, s // tt),
        in_specs=[pl.BlockSpec((tt, LANES), lambda b, n: (n, xcol0 + b)),
                  pl.BlockSpec((tt, LANES), lambda b, n: (n, ycol0 + b)),
                  pl.BlockSpec((CONV_WIDTH, LANES), lambda b, n: (0, b)), vec, wblk, vec, wblk, vec, vec],
        out_specs=(out, out),
        scratch_shapes=[pltpu.VMEM((tt, LANES), F32), pltpu.VMEM((8, LANES), F32)],
        compiler_params=_cp(("parallel", "arbitrary"), 32),
    )(proj, proj, cw, cb, wr, br, wi, bi, lam)


def _lru_bwd(proj, hseq, dcat, cw, cb, wr, br, wi, bi, lam, *, tt, name, comm=None):
    s = proj.shape[0]
    nblk = wr.shape[0]
    c = nblk * LANES
    tt = min(tt, s)
    nt = s // tt
    xcol0 = (N_PATTERNS + 2) * N_KV_HEADS
    ycol0 = xcol0 + nblk
    rcol0 = N_KV_HEADS

    def body(x_ref, xp_ref, y_ref, h_ref, hp_ref, dr_ref, cw_ref, cb_ref, wr_ref, br_ref, wi_ref, bi_ref, lam_ref,
             dx_ref, dy_ref, dcw_ref, dcb_ref, dwr_ref, dbr_ref, dwi_ref, dbi_ref, dlam_ref, dxc_next, gcar, acar):
        n = pl.program_id(1)
        rt = nt - 1 - n

        @pl.when(n == 0)
        def _():
            for ref in (dxc_next, gcar, acar, dcw_ref, dcb_ref, dwr_ref, dbr_ref, dwi_ref, dbi_ref, dlam_ref):
                ref[...] = jnp.zeros_like(ref)

        row = lax.broadcasted_iota(jnp.int32, (tt, LANES), 0)
        x = x_ref[...]
        xp = jnp.where(rt > 0, xp_ref[...], 0.0)
        cwv = cw_ref[...]
        wrb, wib = wr_ref[0].astype(BF16), wi_ref[0].astype(BF16)
        lam_v = lam_ref[...]
        shifts, xc, xcb, r, i, cc, a, mult = _lru_gates(x, xp, cwv, cb_ref[...], wrb, br_ref[...], wib, bi_ref[...],
                                                        lam_v, row)
        h = h_ref[...]
        hp_last = jnp.where(rt > 0, hp_ref[7:8, :], 0.0)
        hprev = jnp.where(row < 1, hp_last, pltpu.roll(h, 1, 0))
        y = y_ref[...]
        y2 = y * y
        th = jnp.tanh(GELU_C0 * (y + GELU_C1 * y2 * y))
        gel = 0.5 * y * (1.0 + th)
        dgel = 0.5 * (1.0 + th) + 0.5 * y * (1.0 - th * th) * GELU_C0 * (1.0 + 3.0 * GELU_C1 * y2)
        drec = dr_ref[...]
        dy_ref[...] = (drec * h * dgel).astype(BF16)
        av = jnp.where(row >= tt - 1, acar[0:1, :], pltpu.roll(a, tt - 1, 0))
        bv = drec * gel
        k = 1
        while k < tt:
            bs = jnp.where(row >= tt - k, 0.0, pltpu.roll(bv, tt - k, 0))
            as_ = jnp.where(row >= tt - k, 1.0, pltpu.roll(av, tt - k, 0))
            bv = bv + av * bs
            av = av * as_
            k *= 2
        g = bv + av * gcar[0:1, :]
        gcar[...] = jnp.broadcast_to(g[0:1, :], gcar.shape)
        acar[...] = jnp.broadcast_to(a[0:1, :], acar.shape)
        da = g * hprev
        d_ixc = g * mult
        dmult = g * (i * xc)
        di = d_ixc * xc
        dxc = d_ixc * i
        a2 = a * a
        dla = da * a - dmult * (a2 / mult)
        dr = dla * cc
        dsp = jnp.sum(dla * r, axis=0, keepdims=True) * (-LRU_C)
        dlam_ref[...] += dsp * (-jax.nn.sigmoid(-lam_v))
        dzr = dr * r * (1.0 - r)
        dzi = di * i * (1.0 - i)
        dbr_ref[...] += jnp.sum(dzr, axis=0, keepdims=True)
        dbi_ref[...] += jnp.sum(dzi, axis=0, keepdims=True)
        dzrb, dzib = dzr.astype(BF16), dzi.astype(BF16)
        tn = (((0,), (0,)), ((), ()))
        ntd = (((1,), (1,)), ((), ()))
        dwr_ref[0] += lax.dot_general(xcb, dzrb, tn, preferred_element_type=F32)
        dwi_ref[0] += lax.dot_general(xcb, dzib, tn, preferred_element_type=F32)
        dxc = (dxc + lax.dot_general(dzrb, wrb, ntd, preferred_element_type=F32)
               + lax.dot_general(dzib, wib, ntd, preferred_element_type=F32))
        dcb_ref[...] += jnp.sum(dxc, axis=0, keepdims=True)
        dcw_ref[...] += jnp.concatenate(
            [jnp.sum(dxc * shifts[CONV_WIDTH - 1 - j], axis=0, keepdims=True) for j in range(CONV_WIDTH)], axis=0)
        nxt = dxc_next[...]
        dx = cwv[0:1, :] * _up(dxc, nxt, CONV_WIDTH - 1, row, tt)
        for j in range(1, CONV_WIDTH):
            dx = dx + cwv[j:j + 1, :] * _up(dxc, nxt, CONV_WIDTH - 1 - j, row, tt)
        dx_ref[...] = dx.astype(BF16)
        dxc_next[...] = dxc

    def tile(col0, prev=False):
        if prev:
            return pl.BlockSpec((tt, LANES), lambda b, n: (jnp.maximum(nt - 2 - n, 0), col0 + b))
        return pl.BlockSpec((tt, LANES), lambda b, n: (nt - 1 - n, col0 + b))

    vec = pl.BlockSpec((1, LANES), lambda b, n: (0, b))
    wblk = pl.BlockSpec((1, LANES, LANES), lambda b, n: (b, 0, 0))
    cwblk = pl.BlockSpec((CONV_WIDTH, LANES), lambda b, n: (0, b))
    hp8 = pl.BlockSpec((8, LANES), lambda b, n: (jnp.maximum((nt - 1 - n) * (tt // 8) - 1, 0), b))
    vshape = jax.ShapeDtypeStruct((1, c), F32)
    wshape = jax.ShapeDtypeStruct((nblk, LANES, LANES), F32)
    return _call(
        body, name=name,
        out_shape=(jax.ShapeDtypeStruct((s, c), BF16), jax.ShapeDtypeStruct((s, c), BF16),
                   jax.ShapeDtypeStruct((CONV_WIDTH, c), F32), vshape, wshape, vshape, wshape, vshape, vshape),
        grid=(nblk, nt),
        in_specs=[tile(xcol0), tile(xcol0, True), tile(ycol0), tile(0), hp8, tile(rcol0),
                  cwblk, vec, wblk, vec, wblk, vec, vec],
        out_specs=(tile(0), tile(0), cwblk, vec, wblk, vec, wblk, vec, vec),
        scratch_shapes=[pltpu.VMEM((tt, LANES), F32), pltpu.VMEM((8, LANES), F32), pltpu.VMEM((8, LANES), F32)],
        args=(proj, proj, proj, hseq, hseq, dcat, cw, cb, wr, br, wi, bi, lam), vmem_mib=32, comm=comm)


ROW_BLOCKS = (512, 256, 176, 128, 64, 32, 16, 8)


def _adamw(w, m, v, gparts, *, name):
    r, c = w.shape
    npart = gparts.shape[0]
    br = _pick(r, ROW_BLOCKS)
    c1 = 1.0 - ADAM_B1 ** ADAM_STEP
    c2 = 1.0 - ADAM_B2 ** ADAM_STEP

    def body(w_ref, m_ref, v_ref, g_ref, go_ref, d_ref, mo_ref, vo_ref):
        g = g_ref[0].astype(F32)
        for q in range(1, npart):
            g = g + g_ref[q].astype(F32)
        mn = ADAM_B1 * m_ref[...] + (1.0 - ADAM_B1) * g
        vn = ADAM_B2 * v_ref[...] + (1.0 - ADAM_B2) * (g * g)
        go_ref[...] = g
        mo_ref[...] = mn
        vo_ref[...] = vn
        d_ref[...] = -ADAM_LR * ((mn / c1) / (jnp.sqrt(vn / c2) + ADAM_EPS) + ADAM_WD * w_ref[...])

    blk = pl.BlockSpec((br, c), lambda i: (i, 0))
    out = jax.ShapeDtypeStruct((r, c), F32)
    return pl.pallas_call(
        body, name=name, out_shape=(out, out, out, out), grid=(r // br,),
        in_specs=[blk, blk, blk, pl.BlockSpec((npart, br, c), lambda i: (0, i, 0))],
        out_specs=(blk, blk, blk, blk),
        compiler_params=_cp(("parallel",), 48),
    )(w, m, v, gparts)


def _sum_parts(parts, *, name):
    npart, r, c = parts.shape
    br = next((b for b in range(min(r, 2048) // 8 * 8, 0, -8) if r % b == 0), r)

    def body(p_ref, o_ref):
        acc = p_ref[0]
        for q in range(1, npart):
            acc = acc + p_ref[q]
        o_ref[...] = acc

    return pl.pallas_call(
        body, name=name, out_shape=jax.ShapeDtypeStruct((r, c), F32), grid=(r // br,),
        in_specs=[pl.BlockSpec((npart, br, c), lambda i: (0, i, 0))],
        out_specs=pl.BlockSpec((br, c), lambda i: (i, 0)),
        compiler_params=_cp(("parallel",), 48),
    )(parts)


HBM = pl.BlockSpec(memory_space=pltpu.HBM)


def _mesh_pos():
    return lax.axis_index("x"), lax.axis_index("y"), lax.axis_index("c")


def _all_gather(shards, *, name):
    comm = _gather_comm(shards)
    na = len(shards)

    def body(*refs):
        ins, outs, sems = refs[:na], refs[na:2 * na], refs[2 * na:]
        comm.start(ins, outs, sems)
        comm.mid(ins, outs, sems)
        comm.end(ins, outs, sems)

    return pl.pallas_call(
        body, name=name, out_shape=tuple(comm.out_shapes),
        in_specs=[HBM] * na, out_specs=tuple([HBM] * na), scratch_shapes=comm.scratch,
    )(*shards)


def _gather_comm(shards):
    na = len(shards)

    def parts(x_refs, out_refs, sems):
        send_sems, recv_sems, local_sems = sems
        x, y, c = _mesh_pos()
        me, sibling = (x, y, c), (x, y, 1 - c)
        chips = [(1 - x, y), (x, 1 - y), (1 - x, 1 - y)]

        def copy(a, k, block, to, src=None):
            px, py, pc = block
            dst = out_refs[a].at[4 * px + 2 * py + pc]
            return pltpu.make_async_remote_copy(
                src_ref=dst if src is None else src, dst_ref=dst,
                send_sem=send_sems.at[a, k], recv_sem=recv_sems.at[a, k],
                device_id=to, device_id_type=MESH)

        def mine(a):
            return pltpu.make_async_copy(x_refs[a], out_refs[a].at[4 * x + 2 * y + c], local_sems.at[a])

        def first(a):
            return [copy(a, 0, me, sibling, src=x_refs[a])] + [
                copy(a, 1 + j, me, (*chip, c), src=x_refs[a]) for j, chip in enumerate(chips)]

        def passed(a, j):
            return copy(a, 4 + j, (*chips[j], c), sibling)

        return me, sibling, chips, c, copy, mine, first, passed

    def start(x_refs, out_refs, sems):
        *_, mine, first, _ = parts(x_refs, out_refs, sems)
        for a in range(na):
            mine(a).start()
            for cp in first(a):
                cp.start()

    def mid(x_refs, out_refs, sems):
        me, _, chips, c, copy, _, _, passed = parts(x_refs, out_refs, sems)
        for j, chip in enumerate(chips):
            for a in range(na):
                copy(a, 1 + j, (*chip, c), me).wait_recv()
                passed(a, j).start()

    def end(x_refs, out_refs, sems):
        me, sibling, chips, c, copy, mine, first, passed = parts(x_refs, out_refs, sems)
        for a in range(na):
            copy(a, 0, sibling, me).wait_recv()
            for j, chip in enumerate(chips):
                copy(a, 4 + j, (*chip, 1 - c), me).wait_recv()
        for a in range(na):
            for cp in first(a) + [passed(a, j) for j in range(3)]:
                cp.wait_send()
            mine(a).wait()

    return _Comm(
        shards, [jax.ShapeDtypeStruct((N_DEV,) + a.shape, a.dtype) for a in shards],
        [pltpu.SemaphoreType.DMA((na, 7)), pltpu.SemaphoreType.DMA((na, 7)), pltpu.SemaphoreType.DMA((na,))],
        start, end, mid)


def _scatter_peer(k, x, y, c):
    px, py, pc = (1 - x if k & 4 else x, 1 - y if k & 2 else y, 1 - c if k & 1 else c)
    return (px, py, pc), 4 * px + 2 * py + pc


def _scatter_start_comm(g8):
    land = lax.empty(g8.shape, g8.dtype)

    def start(refs, outs, sems):
        g_ref, land_ref = refs
        send_sems, recv_sems = outs[:2]
        x, y, c = _mesh_pos()
        me_idx = 4 * x + 2 * y + c
        pltpu.make_async_copy(g_ref.at[me_idx], land_ref.at[me_idx], sems[0]).start()
        for k in range(1, N_DEV):
            peer, peer_idx = _scatter_peer(k, x, y, c)
            pltpu.make_async_remote_copy(
                src_ref=g_ref.at[peer_idx], dst_ref=land_ref.at[me_idx],
                send_sem=send_sems.at[k - 1], recv_sem=recv_sems.at[k - 1],
                device_id=peer, device_id_type=MESH).start()

    def end(refs, outs, sems):
        g_ref, land_ref = refs
        x, y, c = _mesh_pos()
        me_idx = 4 * x + 2 * y + c
        pltpu.make_async_copy(g_ref.at[me_idx], land_ref.at[me_idx], sems[0]).wait()

    sem_shape = pltpu.SemaphoreType.DMA((N_DEV - 1,))
    return _Comm([g8, land], [sem_shape, sem_shape, pltpu.HBM(g8.shape, g8.dtype), pltpu.HBM(g8.shape, g8.dtype)],
                 [pltpu.SemaphoreType.DMA(())], start, end, split=True)


def _scatter_wait(started, after, *, name):
    send_sems, recv_sems, g8, land = started

    def body(g_ref, land_ref, send_ref, recv_ref, after_ref, g_dead, got_ref):
        x, y, c = _mesh_pos()
        me_idx = 4 * x + 2 * y + c
        for k in range(1, N_DEV):
            peer, peer_idx = _scatter_peer(k, x, y, c)
            pltpu.make_async_remote_copy(
                src_ref=g_ref.at[peer_idx], dst_ref=land_ref.at[peer_idx],
                send_sem=send_ref.at[k - 1], recv_sem=recv_ref.at[k - 1],
                device_id=peer, device_id_type=MESH).wait()

    sem = pl.BlockSpec(memory_space=pltpu.SEMAPHORE)
    return pl.pallas_call(
        body, name=name, out_shape=(pltpu.HBM(g8.shape, g8.dtype), pltpu.HBM(land.shape, land.dtype)),
        in_specs=[HBM, HBM, sem, sem, pl.BlockSpec(memory_space=pl.ANY)], out_specs=(HBM, HBM),
        input_output_aliases={0: 0, 1: 1},
        compiler_params=pltpu.CompilerParams(has_side_effects=pltpu.SideEffectType.DATAFLOW_SIDE_EFFECTING),
    )(g8, land, send_sems, recv_sems, after)[1]


BIG_WEIGHTS = ("ffn1_w_gate", "ffn1_w_up", "ffn1_w_down", "w_in", "w_out",
               "ffn2_w_gate", "ffn2_w_up", "ffn2_w_down", "w_ple_proj", "w_ple_gate")
COLUMN_SHARDED = ("ffn1_w_gate", "ffn1_w_up", "w_in", "ffn2_w_gate", "ffn2_w_up", "w_ple_proj", "conv_w")
SMALL_WEIGHTS = ("ln1_g", "ln1_b", "conv_b", "w_rgate", "b_rgate", "w_igate", "b_igate", "lru_lambda",
                 "ln2_g", "ln2_b", "ln3_g", "ln3_b")
SMALL_GRADS = SMALL_WEIGHTS + ("conv_w",)


class _Exchange:
    def __init__(self, full):
        self.full = dict(full)
        self.grads = {}

    def __getitem__(self, name):
        return self.full[name]

    def gather(self, names):
        return None, None

    def scatter_start(self, name):
        return None, None

    def gather_small(self):
        return None, None


class _MeshExchange(_Exchange):
    def __init__(self, full, shards):
        super().__init__(full)
        self.shards = shards
        self.reduced = {}
        self.started = {}
        self.small_parts = None

    def gather(self, names):
        def done(outs):
            for n, o in zip(names, outs):
                self.take(n, o)
        return _gather_comm([self.shards[n] for n in names]), done

    def take(self, name, gathered):
        self.full[name] = gathered.reshape((N_DEV * gathered.shape[1],) + gathered.shape[2:])

    def scatter_start(self, name):
        def done(outs):
            self.started[name] = outs
        return _scatter_start_comm(_to_owner_blocks(name, self.grads[name])), done

    def finish(self, after):
        for name, started in self.started.items():
            self.reduced[name] = _scatter_wait(started, after, name=f"scatter_wait_{name}")

    def gather_small(self):
        def done(outs):
            self.small_parts, = outs
        packed = jnp.concatenate([_rows128(self.grads[n]) for n in SMALL_GRADS], axis=0)
        return _gather_comm([packed]), done


def _carried(comm_done, call):
    comm, done = comm_done
    res = call(comm)
    if comm is None:
        return res
    res, outs = res
    done(outs)
    return res


def _dw(a, b, *, scale=1.0, name, comm=None):
    k, m = a.shape
    n = b.shape[1]
    return _mm(a, b, ta=True, scale=scale, out_dtype=BF16, bm=_pick(m, (1024, 512, 256, 128)),
               bn=_pick(n, (512, 256, 128)), bk=k, name=name, comm=comm)


def _ffn_bwd(ex, names, saved, xb_in, dz, dzb, ln_in, tag, on_dwd=None, on_dh=None, on_dwu=None, on_dx=None):
    gate, up, down = names
    g, u, h, _, _ = saved
    f = ex[gate].shape[0]

    def request(fn):
        return (None, None) if fn is None else fn(ex)

    ex.grads[down] = _carried(request(on_dwd), lambda c: _dw(h, dzb, scale=0.5, name=f"{tag}_dwd", comm=c))
    dg, du = _carried(request(on_dh), lambda c: _ffn_bwd_dh(
        dzb, ex[down], g, u, scale=0.5, bm=1024, bn=_pick(f, (512, 256, 128)), name=f"{tag}_dh", chunks=4, comm=c))
    ex.grads[gate] = _dw(xb_in, dg, name=f"{tag}_dwg")
    ex.grads[up] = _carried(request(on_dwu), lambda c: _dw(xb_in, du, name=f"{tag}_dwu", comm=c))
    d = dz.shape[1]
    dx = _carried(request(on_dx), lambda c: _ffn_dx(
        dg, du, ex[gate], ex[up], dz, extra_scale=DEEPNORM_ALPHA,
        bm=512, bn=_pick(d, (512, 256, 128)), name=f"{tag}_dx", comm=c))
    return dx if ln_in is None else _ln_bwd(dx, *ln_in, bm=256, name=f"{tag}_ln_bwd")


def _local_step(x, p, target, positions, w):
    s, d = x.shape
    tabs = _rope_tables(positions)
    xb = x.astype(BF16)
    f = w["ffn1_w_gate"].shape[0]
    ffn_bn, ln_bn, ln_bn_short_k = _pick(f, (512, 256, 128)), _pick(d, (512, 256, 128)), _pick(d, (1024, 512, 256, 128))
    g1, u1, h1 = _carried(w.gather(("ffn1_w_down", "w_in", "w_out")), lambda c: _ffn_up(
        xb, w["ffn1_w_gate"], w["ffn1_w_up"], bm=1024, bn=ffn_bn, name="ffn1_up", comm=c))
    x1, x1b, xh1, rs1 = _carried(w.gather(("ffn2_w_gate", "ffn2_w_up")), lambda c: _mm_ln(
        h1, w["ffn1_w_down"], x, w["ln1_g"], w["ln1_b"], res_scale=DEEPNORM_ALPHA, mm_scale=0.5,
        bm=512, bn=ln_bn, name="ffn1_down_ln", comm=c))
    sv1 = (g1, u1, h1, xh1, rs1)
    pw = w["w_in"].shape[0]
    proj = _carried(w.gather(("ffn2_w_down", "w_ple_gate", "w_ple_proj")), lambda c: _mm(
        x1b, w["w_in"], tb=True, bm=1024, bn=_pick(pw, (512, 256, 128)), bk=d, name="in_proj", comm=c))
    nqk = (N_PATTERNS + 1) * N_KV_HEADS
    qkr = _rotary(proj, tabs, n_cols=nqk, inverse=False, out_dtype=F32, bs=1024, name="rotary")
    attn, lse = _attn_fwd(qkr, proj, name="attn_fwd")
    lru_w = (w["conv_w"], w["conv_b"], w["w_rgate"], w["b_rgate"], w["w_igate"], w["b_igate"], w["lru_lambda"])
    rec, hseq = _lru_fwd(proj, *lru_w, tt=512, name="lru_fwd")
    cat = jnp.concatenate([attn.astype(BF16), rec], axis=1)
    x2, x2b, xh2, rs2 = _mm_ln(cat, w["w_out"], x1, w["ln2_g"], w["ln2_b"], res_scale=DEEPNORM_ALPHA, mm_scale=1.0,
                               bm=512, bn=ln_bn_short_k, name="out_proj_ln")
    g2, u2, h2 = _ffn_up(x2b, w["ffn2_w_gate"], w["ffn2_w_up"], bm=1024, bn=ffn_bn, name="ffn2_up")
    x3, x3b, xh3, rs3 = _mm_ln(h2, w["ffn2_w_down"], x2, w["ln3_g"], w["ln3_b"], res_scale=DEEPNORM_ALPHA, mm_scale=0.5,
                               bm=512, bn=ln_bn, name="ffn2_down_ln")
    sv3 = (g2, u2, h2, xh3, rs3)
    lsum, dy, dgate, dple = _ple_loss(x3, x3b, p, w["w_ple_gate"], w["w_ple_proj"], target,
                                      bm=1024, bn=_pick(d, (512, 256, 128)), name="ple_loss")
    grads = w.grads
    grads["w_ple_gate"] = _dw(x3b, dgate, name="dw_ple_gate")
    grads["w_ple_proj"] = _carried(w.scatter_start("w_ple_gate"), lambda c: _dw(p, dple, name="dw_ple_proj", comm=c))
    dz3, dz3b, grads["ln3_g"], grads["ln3_b"] = _carried(w.scatter_start("w_ple_proj"), lambda c: _mm_dx(
        dgate, w["w_ple_gate"], dy, xh3, rs3, w["ln3_g"], extra_scale=1.0, bm=512, bn=ln_bn_short_k,
        name="ple_dx", tb=True, comm=c))
    dz2, dz2b, grads["ln2_g"], grads["ln2_b"] = _ffn_bwd(
        w, ("ffn2_w_gate", "ffn2_w_up", "ffn2_w_down"), sv3, x2b, dz3, dz3b, (xh2, rs2, w["ln2_g"]), "ffn2",
        on_dh=lambda ex: ex.scatter_start("ffn2_w_down"), on_dwu=lambda ex: ex.scatter_start("ffn2_w_gate"),
        on_dx=lambda ex: ex.scatter_start("ffn2_w_up"))
    grads["w_out"] = _dw(cat, dz2b, name="dw_out")
    dcat = _carried(w.scatter_start("w_out"), lambda c: _mm(
        dz2b, w["w_out"], tb=True, bm=1024, bn=_pick(d, (512, 256, 128)), bk=d, name="out_proj_dx", comm=c))
    dq0, dq1, dq2, dk, dvb = _attn_bwd(qkr, proj, attn, lse, dcat, name="attn_bwd")
    nh = N_KV_HEADS
    dqkv = [_rotary(t, tabs, n_cols=nh, inverse=True, out_dtype=BF16, bs=1024, name=f"rotary_bwd{i}")
            for i, t in enumerate((dq0, dq1, dq2, dk))]
    (dxb, dyb, grads["conv_w"], grads["conv_b"], grads["w_rgate"], grads["b_rgate"], grads["w_igate"],
     grads["b_igate"], grads["lru_lambda"]) = _lru_bwd(proj, hseq, dcat, *lru_w, tt=512, name="lru_bwd")
    dproj = jnp.concatenate(dqkv + [dvb, dxb, dyb], axis=1)
    grads["w_in"] = _dw(x1b, dproj, name="dw_in")
    dz1, dz1b, grads["ln1_g"], grads["ln1_b"] = _carried(w.scatter_start("w_in"), lambda c: _mm_dx(
        dproj, w["w_in"], dz2, xh1, rs1, w["ln1_g"], extra_scale=DEEPNORM_ALPHA,
        bm=512, bn=ln_bn, name="in_proj_dx", comm=c))
    grad_x = _ffn_bwd(w, ("ffn1_w_gate", "ffn1_w_up", "ffn1_w_down"), sv1, xb, dz1, dz1b, None, "ffn1",
                      on_dwd=lambda ex: ex.gather_small(),
                      on_dh=lambda ex: ex.scatter_start("ffn1_w_down"),
                      on_dwu=lambda ex: ex.scatter_start("ffn1_w_gate"),
                      on_dx=lambda ex: ex.scatter_start("ffn1_w_up"))
    return lsum, grad_x


def _to_full(name, gathered):
    if name in COLUMN_SHARDED:
        _, r, c = gathered.shape
        return jnp.transpose(gathered, (1, 0, 2)).reshape(r, N_DEV * c)
    return gathered.reshape((N_DEV * gathered.shape[1],) + gathered.shape[2:])


def _to_owner_blocks(name, full):
    if name in COLUMN_SHARDED:
        r, c = full.shape
        return jnp.transpose(full.reshape(r, N_DEV, c // N_DEV), (1, 0, 2))
    return full.reshape((N_DEV, full.shape[0] // N_DEV) + full.shape[1:])


def _rows128(a):
    flat = a.reshape(-1, LANES)
    pad = (-flat.shape[0]) % 8
    return jnp.pad(flat, ((0, pad), (0, 0))) if pad else flat


def kernel(x, p, positions, ffn1_w_gate, ffn1_w_up, ffn1_w_down, ln1_g, ln1_b, w_in, conv_w, conv_b, w_rgate, b_rgate, w_igate, b_igate, lru_lambda, w_out, ln2_g, ln2_b, ffn2_w_gate, ffn2_w_up, ffn2_w_down, ln3_g, ln3_b, w_ple_proj, w_ple_gate, loss_target, m_ffn1_w_gate, m_ffn1_w_up, m_ffn1_w_down, m_ln1_g, m_ln1_b, m_w_in, m_conv_w, m_conv_b, m_w_rgate, m_b_rgate, m_w_igate, m_b_igate, m_lru_lambda, m_w_out, m_ln2_g, m_ln2_b, m_ffn2_w_gate, m_ffn2_w_up, m_ffn2_w_down, m_ln3_g, m_ln3_b, m_w_ple_proj, m_w_ple_gate, v_ffn1_w_gate, v_ffn1_w_up, v_ffn1_w_down, v_ln1_g, v_ln1_b, v_w_in, v_conv_w, v_conv_b, v_w_rgate, v_b_rgate, v_w_igate, v_b_igate, v_lru_lambda, v_w_out, v_ln2_g, v_ln2_b, v_ffn2_w_gate, v_ffn2_w_up, v_ffn2_w_down, v_ln3_g, v_ln3_b, v_w_ple_proj, v_w_ple_gate):
    names = ("ffn1_w_gate", "ffn1_w_up", "ffn1_w_down", "ln1_g", "ln1_b", "w_in", "conv_w", "conv_b", "w_rgate",
             "b_rgate", "w_igate", "b_igate", "lru_lambda", "w_out", "ln2_g", "ln2_b", "ffn2_w_gate", "ffn2_w_up",
             "ffn2_w_down", "ln3_g", "ln3_b", "w_ple_proj", "w_ple_gate")
    ws = (ffn1_w_gate, ffn1_w_up, ffn1_w_down, ln1_g, ln1_b, w_in, conv_w, conv_b, w_rgate, b_rgate, w_igate, b_igate,
          lru_lambda, w_out, ln2_g, ln2_b, ffn2_w_gate, ffn2_w_up, ffn2_w_down, ln3_g, ln3_b, w_ple_proj, w_ple_gate)
    ms = (m_ffn1_w_gate, m_ffn1_w_up, m_ffn1_w_down, m_ln1_g, m_ln1_b, m_w_in, m_conv_w, m_conv_b, m_w_rgate, m_b_rgate,
          m_w_igate, m_b_igate, m_lru_lambda, m_w_out, m_ln2_g, m_ln2_b, m_ffn2_w_gate, m_ffn2_w_up, m_ffn2_w_down,
          m_ln3_g, m_ln3_b, m_w_ple_proj, m_w_ple_gate)
    vs = (v_ffn1_w_gate, v_ffn1_w_up, v_ffn1_w_down, v_ln1_g, v_ln1_b, v_w_in, v_conv_w, v_conv_b, v_w_rgate, v_b_rgate,
          v_w_igate, v_b_igate, v_lru_lambda, v_w_out, v_ln2_g, v_ln2_b, v_ffn2_w_gate, v_ffn2_w_up, v_ffn2_w_down,
          v_ln3_g, v_ln3_b, v_w_ple_proj, v_w_ple_gate)
    def local(a):
        return a[0] if a.ndim >= 3 else a

    w_loc = {n: local(a) for n, a in zip(names, ws)}
    m_loc = {n: local(a) for n, a in zip(names, ms)}
    v_loc = {n: local(a) for n, a in zip(names, vs)}
    out_shapes = {n: a.shape for n, a in zip(names, ws)}

    shards = {n: (w_loc[n].T if n in COLUMN_SHARDED else w_loc[n]).astype(BF16) for n in BIG_WEIGHTS}
    gate1, up1, conv_all = _all_gather([shards["ffn1_w_gate"], shards["ffn1_w_up"], w_loc["conv_w"]], name="gather_first")
    ex = _MeshExchange({n: w_loc[n] for n in SMALL_WEIGHTS}, shards)
    ex.full["conv_w"] = _to_full("conv_w", conv_all)
    ex.take("ffn1_w_gate", gate1)
    ex.take("ffn1_w_up", up1)

    lsum, grad_x = _local_step(x[0], p[0, 0], loss_target[0], positions[0], ex)
    ex.finish(grad_x)
    grads, reduced = ex.grads, ex.reduced
    d_model = x.shape[-1]
    loss = lax.psum(lsum[0, 0] * (0.5 / d_model), ("x", "y", "c"))

    small = SMALL_GRADS
    summed = _sum_parts(ex.small_parts, name="sum_small_grads")
    small_grads, row = {}, 0
    for n in small:
        rows = grads[n].size // LANES
        small_grads[n] = summed[row:row + rows].reshape(grads[n].shape)
        row += rows + (-rows) % 8
    me = 4 * lax.axis_index("x") + 2 * lax.axis_index("y") + lax.axis_index("c")
    cw_cols = w_loc["conv_w"].shape[1]
    small_grads["conv_w"] = lax.dynamic_slice_in_dim(small_grads["conv_w"], me * cw_cols, cw_cols, axis=1)

    out_g, out_d, out_m, out_v = {}, {}, {}, {}
    for n in names:
        wl, ml, vl = w_loc[n], m_loc[n], v_loc[n]
        shape = wl.shape
        if n in BIG_WEIGHTS:
            gparts = reduced[n]
        else:
            gparts = small_grads[n].reshape((1,) + shape)
        if wl.ndim == 3:
            wl, ml, vl = (t.reshape(-1, shape[-1]) for t in (wl, ml, vl))
            gparts = gparts.reshape(gparts.shape[0], -1, shape[-1])
        res = _adamw(wl, ml, vl, gparts, name=f"adamw_{n}")
        out_g[n], out_d[n], out_m[n], out_v[n] = (t.reshape(out_shapes[n]) for t in res)

    return (loss, grad_x[None], *[out_g[n] for n in names], *[out_d[n] for n in names],
            *[out_m[n] for n in names], *[out_v[n] for n in names])
```

```python
import jax
import jax.numpy as jnp
from jax import lax
from jax.experimental import pallas as pl
from jax.experimental.pallas import tpu as pltpu

F32 = jnp.float32
BF16 = jnp.bfloat16

N_DEV = 8
LANES = 128
MIB = 1 << 20

HEAD_DIM = 128
N_KV_HEADS = 4
DILATIONS = (1, 4, 16)
N_PATTERNS = 3
SPAN = 128
ROT_DIMS = 32
ROPE_THETA = 500000.0
LRU_C = 8.0
CONV_WIDTH = 4
LN_EPS = 1e-5
DEEPNORM_ALPHA = 2.0 ** 0.25
ATTN_TILE = SPAN * DILATIONS[-1]

ADAM_LR = 0.001
ADAM_B1 = 0.9
ADAM_B2 = 0.999
ADAM_EPS = 1e-08
ADAM_WD = 0.01
ADAM_STEP = 10

MESH = pl.DeviceIdType.MESH
NT_DIMS = (((1,), (1,)), ((), ()))
EPILOGUE_ROWS = 64


def _cp(semantics, vmem_mib):
    return pltpu.CompilerParams(dimension_semantics=semantics, vmem_limit_bytes=vmem_mib * MIB)


def _pick(n, candidates):
    for c in candidates:
        if n % c == 0:
            return c
    return n


class _Comm:
    def __init__(self, arrays, out_shapes, scratch, start, end, mid=None, split=False):
        self.arrays, self.out_shapes, self.scratch = list(arrays), list(out_shapes), list(scratch)
        self.start, self.mid, self.end, self.split = start, mid, end, split


def _call(body, *, name, grid, in_specs, out_specs, out_shape, args, scratch_shapes=(), vmem_mib, comm=None):
    single = not isinstance(out_shape, (tuple, list))
    out_shape_t = (out_shape,) if single else tuple(out_shape)
    out_specs_t = (out_specs,) if single else tuple(out_specs)
    params = _cp(("arbitrary",) * len(grid), vmem_mib)
    if comm is None:
        res = pl.pallas_call(body, name=name, grid=grid, in_specs=list(in_specs), out_specs=out_specs_t,
                             out_shape=out_shape_t, scratch_shapes=list(scratch_shapes), compiler_params=params)(*args)
        return res[0] if single else res
    n_in, n_out, n_scr = len(args), len(out_shape_t), len(scratch_shapes)
    nci, nco = len(comm.arrays), len(comm.out_shapes)
    total = 1
    for g in grid:
        total *= g

    def wrapped(*refs):
        ins, refs = refs[:n_in], refs[n_in:]
        cin, refs = refs[:nci], refs[nci:]
        outs, refs = refs[:n_out], refs[n_out:]
        cout, refs = refs[:nco], refs[nco:]
        scr, csem = refs[:n_scr], refs[n_scr:]
        step = pl.program_id(0)
        for ax in range(1, len(grid)):
            step = step * grid[ax] + pl.program_id(ax)

        @pl.when(step == 0)
        def _():
            comm.start(cin, cout, csem)

        body(*ins, *outs, *scr)
        if comm.mid is not None:
            @pl.when(step == (3 * total) // 4)
            def _():
                comm.mid(cin, cout, csem)

        @pl.when(step == total - 1)
        def _():
            comm.end(cin, cout, csem)

    hbm = pl.BlockSpec(memory_space=pltpu.HBM)
    if comm.split:
        sem = pl.BlockSpec(memory_space=pltpu.SEMAPHORE)
        n_sems = nco - nci
        res = pl.pallas_call(
            wrapped, name=name, grid=grid,
            in_specs=list(in_specs) + [hbm] * nci,
            out_specs=out_specs_t + (sem,) * n_sems + (hbm,) * nci,
            out_shape=out_shape_t + tuple(comm.out_shapes),
            scratch_shapes=list(scratch_shapes) + comm.scratch,
            input_output_aliases={n_in + k: n_out + n_sems + k for k in range(nci)},
            compiler_params=pltpu.CompilerParams(
                dimension_semantics=("arbitrary",) * len(grid), vmem_limit_bytes=vmem_mib * MIB,
                has_side_effects=pltpu.SideEffectType.DATAFLOW_SIDE_EFFECTING),
        )(*args, *[pltpu.with_memory_space_constraint(a, pltpu.HBM) for a in comm.arrays])
    else:
        res = pl.pallas_call(
            wrapped, name=name, grid=grid,
            in_specs=list(in_specs) + [hbm] * nci,
            out_specs=out_specs_t + (hbm,) * nco,
            out_shape=out_shape_t + tuple(comm.out_shapes),
            scratch_shapes=list(scratch_shapes) + comm.scratch,
            compiler_params=params)(*args, *comm.arrays)
    own, extra = res[:n_out], res[n_out:]
    return (own[0] if single else own), extra


def _mm(a, b, *, ta=False, tb=False, out_dtype=F32, scale=1.0, bm, bn, bk, name, comm=None):
    m, k = (a.shape[1], a.shape[0]) if ta else a.shape
    n = b.shape[0] if tb else b.shape[1]
    bm, bn, bk = min(bm, m), min(bn, n), min(bk, k)
    assert m % bm == 0 and n % bn == 0 and k % bk == 0, (name, m, n, k, bm, bn, bk)
    nk = k // bk
    a_spec = pl.BlockSpec((bk, bm), lambda i, j, kk: (kk, i)) if ta else pl.BlockSpec((bm, bk), lambda i, j, kk: (i, kk))
    b_spec = pl.BlockSpec((bn, bk), lambda i, j, kk: (j, kk)) if tb else pl.BlockSpec((bk, bn), lambda i, j, kk: (kk, j))
    dn = (((0 if ta else 1,), (1 if tb else 0,)), ((), ()))

    def body(a_ref, b_ref, o_ref, *acc):
        part = lax.dot_general(a_ref[...].astype(BF16), b_ref[...].astype(BF16), dn, preferred_element_type=F32)
        if nk == 1:
            o_ref[...] = (part * scale).astype(out_dtype)
            return
        acc_ref, = acc
        kk = pl.program_id(2)

        @pl.when(kk == 0)
        def _():
            acc_ref[...] = part

        @pl.when(kk > 0)
        def _():
            acc_ref[...] += part

        @pl.when(kk == nk - 1)
        def _():
            o_ref[...] = (acc_ref[...] * scale).astype(out_dtype)

    return _call(
        body, name=name,
        out_shape=jax.ShapeDtypeStruct((m, n), out_dtype),
        grid=(m // bm, n // bn, nk),
        in_specs=[a_spec, b_spec],
        out_specs=pl.BlockSpec((bm, bn), lambda i, j, kk: (i, j)),
        scratch_shapes=[pltpu.VMEM((bm, bn), F32)] if nk > 1 else [],
        args=(a, b), vmem_mib=56, comm=comm)


def _ffn_up(xb, wg, wu, *, bm, bn, name, comm=None):
    s, d = xb.shape
    f = wg.shape[0]
    bm, bn = min(bm, s), min(bn, f)
    assert s % bm == 0 and f % bn == 0

    def body(x_ref, wg_ref, wu_ref, hg_ref, hu_ref, h_ref):
        x = x_ref[...]
        g = lax.dot_general(x, wg_ref[...], NT_DIMS, preferred_element_type=F32)
        u = lax.dot_general(x, wu_ref[...], NT_DIMS, preferred_element_type=F32)
        sig = jax.nn.sigmoid(g)
        silu = g * sig
        hg_ref[...] = (u * (sig * (1.0 + g * (1.0 - sig)))).astype(BF16)
        hu_ref[...] = silu.astype(BF16)
        h_ref[...] = (silu * u).astype(BF16)

    out = jax.ShapeDtypeStruct((s, f), BF16)
    blk = pl.BlockSpec((bm, bn), lambda i, j: (i, j))
    return _call(
        body, name=name, out_shape=(out, out, out),
        grid=(s // bm, f // bn),
        in_specs=[pl.BlockSpec((bm, d), lambda i, j: (i, 0)),
                  pl.BlockSpec((bn, d), lambda i, j: (j, 0)),
                  pl.BlockSpec((bn, d), lambda i, j: (j, 0))],
        out_specs=(blk, blk, blk),
        args=(xb, wg, wu), vmem_mib=56, comm=comm)


def _ffn_bwd_dh(dzb, wd, g, u, *, scale, bm, bn, name, chunks=2, comm=None):
    s, d = dzb.shape
    f = wd.shape[0]
    bm, bn = min(bm, s), min(bn, f)
    assert s % bm == 0 and f % bn == 0

    cr = bm // chunks

    def body(dz_ref, wd_ref, hg_ref, hu_ref, dg_ref, du_ref):
        for r in range(chunks):
            rows = slice(r * cr, (r + 1) * cr)
            dh = lax.dot_general(dz_ref[rows, :], wd_ref[...], NT_DIMS, preferred_element_type=F32) * scale
            dg_ref[rows, :] = (dh * hg_ref[rows, :].astype(F32)).astype(BF16)
            du_ref[rows, :] = (dh * hu_ref[rows, :].astype(F32)).astype(BF16)

    out = jax.ShapeDtypeStruct((s, f), BF16)
    blk = pl.BlockSpec((bm, bn), lambda i, j: (i, j))
    return _call(
        body, name=name, out_shape=(out, out),
        grid=(s // bm, f // bn),
        in_specs=[pl.BlockSpec((bm, d), lambda i, j: (i, 0)),
                  pl.BlockSpec((bn, d), lambda i, j: (j, 0)), blk, blk],
        out_specs=(blk, blk),
        args=(dzb, wd, g, u), vmem_mib=56, comm=comm)


def _full_rows(acc_ref, rows, nj):
    return jnp.concatenate([acc_ref[jj, rows, :] for jj in range(nj)], axis=1)


def _mm_ln(a, b, res, gamma, beta, *, res_scale, mm_scale, bm, bn, name, comm=None):
    s, k = a.shape
    d = b.shape[1]
    bm, bn = min(bm, s), min(bn, d)
    assert s % bm == 0 and d % bn == 0
    nj = d // bn
    ch = min(EPILOGUE_ROWS, bm)

    def body(a_ref, b_ref, r_ref, g_ref, be_ref, y_ref, yb_ref, xh_ref, rs_ref, acc_ref):
        j = pl.program_id(1)
        acc_ref[j] = jnp.dot(a_ref[...], b_ref[...], preferred_element_type=F32)

        @pl.when(j == nj - 1)
        def _():
            def chunk(ci, carry):
                rows = pl.ds(pl.multiple_of(ci * ch, ch), ch)
                z = res_scale * r_ref[rows, :] + mm_scale * _full_rows(acc_ref, rows, nj)
                mu = jnp.mean(z, axis=-1, keepdims=True)
                zc = z - mu
                var = jnp.mean(zc * zc, axis=-1, keepdims=True)
                rstd = lax.rsqrt(var + LN_EPS)
                xh = zc * rstd
                y = xh * g_ref[...] + be_ref[...]
                y_ref[rows, :] = y
                yb_ref[rows, :] = y.astype(BF16)
                xh_ref[rows, :] = xh
                rs_ref[rows, :] = rstd
                return carry

            lax.fori_loop(0, bm // ch, chunk, 0)

    row = pl.BlockSpec((bm, d), lambda i, j: (i, 0))
    vec = pl.BlockSpec((1, d), lambda i, j: (0, 0))
    return _call(
        body, name=name,
        out_shape=(jax.ShapeDtypeStruct((s, d), F32), jax.ShapeDtypeStruct((s, d), BF16),
                   jax.ShapeDtypeStruct((s, d), F32), jax.ShapeDtypeStruct((s, 1), F32)),
        grid=(s // bm, nj),
        in_specs=[pl.BlockSpec((bm, k), lambda i, j: (i, 0)),
                  pl.BlockSpec((k, bn), lambda i, j: (0, j)), row, vec, vec],
        out_specs=(row, row, row, pl.BlockSpec((bm, 1), lambda i, j: (i, 0))),
        scratch_shapes=[pltpu.VMEM((nj, bm, bn), F32)],
        args=(a, b, res, gamma, beta), vmem_mib=58, comm=comm)


def _mm_dx(a, wt, extra, xhat, rstd, gamma, *, extra_scale, bm, bn, name, tb=False, comm=None):
    s, k = a.shape
    d = wt.shape[0] if tb else wt.shape[1]
    bm, bn = min(bm, s), min(bn, d)
    assert s % bm == 0 and d % bn == 0
    nj = d // bn
    ch = min(EPILOGUE_ROWS, bm)
    dims = NT_DIMS if tb else (((1,), (0,)), ((), ()))

    def body(a_ref, w_ref, e_ref, xh_ref, rs_ref, g_ref, dz_ref, dzb_ref, dg_ref, db_ref, acc_ref):
        i = pl.program_id(0)
        j = pl.program_id(1)
        acc_ref[j] = lax.dot_general(a_ref[...], w_ref[...], dims, preferred_element_type=F32)

        @pl.when(j == nj - 1)
        def _():
            def chunk(ci, carry):
                dgp, dbp = carry
                rows = pl.ds(pl.multiple_of(ci * ch, ch), ch)
                dx = extra_scale * e_ref[rows, :] + _full_rows(acc_ref, rows, nj)
                xh = xh_ref[rows, :]
                dxh = dx * g_ref[...]
                m1 = jnp.mean(dxh, axis=-1, keepdims=True)
                m2 = jnp.mean(dxh * xh, axis=-1, keepdims=True)
                dz = rs_ref[rows, :] * (dxh - m1 - xh * m2)
                dz_ref[rows, :] = dz
                dzb_ref[rows, :] = dz.astype(BF16)
                return dgp + jnp.sum(dx * xh, axis=0, keepdims=True), dbp + jnp.sum(dx, axis=0, keepdims=True)

            zero = jnp.zeros((1, d), F32)
            dgp, dbp = lax.fori_loop(0, bm // ch, chunk, (zero, zero))

            @pl.when(i == 0)
            def _():
                dg_ref[...] = dgp
                db_ref[...] = dbp

            @pl.when(i > 0)
            def _():
                dg_ref[...] += dgp
                db_ref[...] += dbp

    row = pl.BlockSpec((bm, d), lambda i, j: (i, 0))
    vec = pl.BlockSpec((1, d), lambda i, j: (0, 0))
    return _call(
        body, name=name,
        out_shape=(jax.ShapeDtypeStruct((s, d), F32), jax.ShapeDtypeStruct((s, d), BF16),
                   jax.ShapeDtypeStruct((1, d), F32), jax.ShapeDtypeStruct((1, d), F32)),
        grid=(s // bm, nj),
        in_specs=[pl.BlockSpec((bm, k), lambda i, j: (i, 0)),
                  pl.BlockSpec((bn, k), lambda i, j: (j, 0)) if tb else pl.BlockSpec((k, bn), lambda i, j: (0, j)),
                  row, row, pl.BlockSpec((bm, 1), lambda i, j: (i, 0)), vec],
        out_specs=(row, row, vec, vec),
        scratch_shapes=[pltpu.VMEM((nj, bm, bn), F32)],
        args=(a, wt, extra, xhat, rstd, gamma), vmem_mib=58, comm=comm)


def _ffn_dx(dg, du, wgt, wut, extra, *, extra_scale, bm, bn, name, comm=None):
    s, f = dg.shape
    d = wgt.shape[1]
    bm, bn = min(bm, s), min(bn, d)
    assert s % bm == 0 and d % bn == 0

    def body(dg_ref, du_ref, wg_ref, wu_ref, e_ref, o_ref):
        acc = jnp.dot(dg_ref[...], wg_ref[...], preferred_element_type=F32)
        acc = acc + jnp.dot(du_ref[...], wu_ref[...], preferred_element_type=F32)
        o_ref[...] = extra_scale * e_ref[...] + acc

    rows = pl.BlockSpec((bm, f), lambda i, j: (i, 0))
    cols = pl.BlockSpec((f, bn), lambda i, j: (0, j))
    blk = pl.BlockSpec((bm, bn), lambda i, j: (i, j))
    return _call(
        body, name=name, out_shape=jax.ShapeDtypeStruct((s, d), F32),
        grid=(s // bm, d // bn), in_specs=[rows, rows, cols, cols, blk], out_specs=blk,
        args=(dg, du, wgt, wut, extra), vmem_mib=58, comm=comm)


def _ln_bwd(dx, xhat, rstd, gamma, *, bm, name):
    s, d = dx.shape
    bm = min(bm, s)
    assert s % bm == 0
    ch = min(EPILOGUE_ROWS, bm)

    def body(dx_ref, xh_ref, rs_ref, g_ref, dz_ref, dzb_ref, dg_ref, db_ref):
        def chunk(ci, carry):
            dgp, dbp = carry
            rows = pl.ds(pl.multiple_of(ci * ch, ch), ch)
            dxv = dx_ref[rows, :]
            xh = xh_ref[rows, :]
            dxh = dxv * g_ref[...]
            m1 = jnp.mean(dxh, axis=-1, keepdims=True)
            m2 = jnp.mean(dxh * xh, axis=-1, keepdims=True)
            dz = rs_ref[rows, :] * (dxh - m1 - xh * m2)
            dz_ref[rows, :] = dz
            dzb_ref[rows, :] = dz.astype(BF16)
            return dgp + jnp.sum(dxv * xh, axis=0, keepdims=True), dbp + jnp.sum(dxv, axis=0, keepdims=True)

        zero = jnp.zeros((1, d), F32)
        dgp, dbp = lax.fori_loop(0, bm // ch, chunk, (zero, zero))
        i = pl.program_id(0)

        @pl.when(i == 0)
        def _():
            dg_ref[...] = dgp
            db_ref[...] = dbp

        @pl.when(i > 0)
        def _():
            dg_ref[...] += dgp
            db_ref[...] += dbp

    row = pl.BlockSpec((bm, d), lambda i: (i, 0))
    vec = pl.BlockSpec((1, d), lambda i: (0, 0))
    return _call(
        body, name=name,
        out_shape=(jax.ShapeDtypeStruct((s, d), F32), jax.ShapeDtypeStruct((s, d), BF16),
                   jax.ShapeDtypeStruct((1, d), F32), jax.ShapeDtypeStruct((1, d), F32)),
        grid=(s // bm,), in_specs=[row, row, pl.BlockSpec((bm, 1), lambda i: (i, 0)), vec],
        out_specs=(row, row, vec, vec), args=(dx, xhat, rstd, gamma), vmem_mib=48)


def _ple_loss(x3, x3b, p, wpg, wpp, target, *, bm, bn, name):
    s, d = x3.shape
    dp = p.shape[1]
    bm, bn = min(bm, s), min(bn, d)
    assert s % bm == 0 and d % bn == 0
    inv_d = 1.0 / d
    chunks = 4 if bm % 64 == 0 else 1
    cr = bm // chunks

    def body(x_ref, xb_ref, p_ref, wg_ref, wp_ref, t_ref, l_ref, dy_ref, dg_ref, dp_ref):
        first = (pl.program_id(0) == 0) & (pl.program_id(1) == 0)

        @pl.when(first)
        def _():
            l_ref[...] = jnp.zeros_like(l_ref)

        part = 0.0
        for r in range(chunks):
            rows = slice(r * cr, (r + 1) * cr)
            gp = jnp.dot(xb_ref[rows, :], wg_ref[...], preferred_element_type=F32)
            pp = lax.dot_general(p_ref[rows, :].astype(BF16), wp_ref[...], NT_DIMS, preferred_element_type=F32)
            sig = jax.nn.sigmoid(gp)
            err = x_ref[rows, :] + sig * pp - t_ref[rows, :]
            part = part + jnp.sum(err * err)
            dy = err * inv_d
            dy_ref[rows, :] = dy
            dg_ref[rows, :] = (dy * pp * sig * (1.0 - sig)).astype(BF16)
            dp_ref[rows, :] = (dy * sig).astype(BF16)
        l_ref[...] += part

    blk = pl.BlockSpec((bm, bn), lambda i, j: (i, j))
    return pl.pallas_call(
        body, name=name,
        out_shape=(jax.ShapeDtypeStruct((8, LANES), F32), jax.ShapeDtypeStruct((s, d), F32),
                   jax.ShapeDtypeStruct((s, d), BF16), jax.ShapeDtypeStruct((s, d), BF16)),
        grid=(s // bm, d // bn),
        in_specs=[blk, pl.BlockSpec((bm, d), lambda i, j: (i, 0)), pl.BlockSpec((bm, dp), lambda i, j: (i, 0)),
                  pl.BlockSpec((d, bn), lambda i, j: (0, j)), pl.BlockSpec((bn, dp), lambda i, j: (j, 0)), blk],
        out_specs=(pl.BlockSpec((8, LANES), lambda i, j: (0, 0)), blk, blk, blk),
        compiler_params=_cp(("arbitrary", "arbitrary"), 56),
    )(x3, x3b, p, wpg, wpp, target)


def _rope_tables(positions):
    half = ROT_DIMS // 2
    lane = jnp.arange(HEAD_DIM)
    inv_freq = jnp.power(jnp.float32(ROPE_THETA), -(lane % half).astype(F32) * (2.0 / ROT_DIMS))
    ang = positions.astype(F32)[:, None] * inv_freq
    cos, sin = jnp.cos(ang), jnp.sin(ang)
    cf = jnp.where(lane < ROT_DIMS, cos, 1.0)
    sa = jnp.where(lane < half, -sin, 0.0)
    sb = jnp.where((lane >= half) & (lane < ROT_DIMS), sin, 0.0)
    return cf, sa, sb


def _rotary(t, tabs, *, n_cols, inverse, out_dtype, bs, name):
    s = t.shape[0]
    bs = min(bs, s)
    half = ROT_DIMS // 2
    heads = N_KV_HEADS
    assert n_cols % heads == 0

    def body(t_ref, cf_ref, sa_ref, sb_ref, o_ref):
        cf, sa, sb = cf_ref[...], sa_ref[...], sb_ref[...]
        for hd in range(heads):
            lanes = slice(hd * HEAD_DIM, (hd + 1) * HEAD_DIM)
            v = t_ref[:, lanes]
            if inverse:
                o = v * cf + pltpu.roll(v * sa, half, 1) + pltpu.roll(v * sb, HEAD_DIM - half, 1)
            else:
                o = v * cf + pltpu.roll(v, HEAD_DIM - half, 1) * sa + pltpu.roll(v, half, 1) * sb
            o_ref[:, lanes] = o.astype(out_dtype)

    blk = pl.BlockSpec((bs, heads * HEAD_DIM), lambda i, j: (i, j))
    tab = pl.BlockSpec((bs, HEAD_DIM), lambda i, j: (i, 0))
    return pl.pallas_call(
        body, name=name, out_shape=jax.ShapeDtypeStruct((s, n_cols * HEAD_DIM), out_dtype),
        grid=(s // bs, n_cols // heads), in_specs=[blk, tab, tab, tab], out_specs=blk,
        compiler_params=_cp(("parallel", "arbitrary"), 32),
    )(t, *tabs)


def _attn_blocks():
    out = []
    for g, dil in enumerate(DILATIONS):
        sup = SPAN * dil
        for j in range(ATTN_TILE // sup):
            for r in range(dil):
                out.append((g, j * sup + r, dil, (j - 1) * sup + r if j > 0 else None, ATTN_TILE - sup + r))
    return out


def _rows(ref, start, dil, lead=None):
    idx = pl.ds(start, SPAN, stride=dil) if dil > 1 else pl.ds(start, SPAN)
    return ref[idx, :] if lead is None else ref[lead, idx, :]


def _band_masks(n):
    qi = lax.broadcasted_iota(jnp.int32, (SPAN, 2 * SPAN), 0)
    ki = lax.broadcasted_iota(jnp.int32, (SPAN, 2 * SPAN), 1)
    band = (ki >= qi) & (ki <= qi + SPAN)
    return band, band & ((ki >= SPAN) | (n > 0))


def _attn_fwd(qkr, proj, *, name):
    s = qkr.shape[0]
    t = ATTN_TILE
    assert s % t == 0
    nt = s // t
    scale = HEAD_DIM ** -0.5
    kcol, vcol = N_PATTERNS * N_KV_HEADS, (N_PATTERNS + 1) * N_KV_HEADS
    blocks = _attn_blocks()

    def body(q0, q1, q2, kc_ref, kp_ref, vc_ref, vp_ref, o_ref, l_ref, og, lg):
        n = pl.program_id(1)
        band, band_first = _band_masks(n)
        q_refs = (q0, q1, q2)
        for g, start, dil, prev_in_tile, prev_start in blocks:
            q = _rows(q_refs[g], start, dil).astype(BF16)
            if prev_in_tile is not None:
                kp, vp, mask = _rows(kc_ref, prev_in_tile, dil), _rows(vc_ref, prev_in_tile, dil), band
            else:
                kp, vp, mask = _rows(kp_ref, prev_start, dil), _rows(vp_ref, prev_start, dil), band_first
            kk = jnp.concatenate([kp, _rows(kc_ref, start, dil)], axis=0).astype(BF16)
            vv = jnp.concatenate([vp, _rows(vc_ref, start, dil)], axis=0).astype(BF16)
            sc = lax.dot_general(q, kk, (((1,), (1,)), ((), ())), preferred_element_type=F32) * scale
            sc = jnp.where(mask, sc, -1e30)
            m = jnp.max(sc, axis=-1, keepdims=True)
            e = jnp.exp(sc - m)
            den = jnp.sum(e, axis=-1, keepdims=True)
            o = jnp.dot(e.astype(BF16), vv, preferred_element_type=F32) / den
            idx = pl.ds(start, SPAN, stride=dil) if dil > 1 else pl.ds(start, SPAN)
            og[g, idx, :] = o
            lg[g, idx, :] = jnp.broadcast_to(m + jnp.log(den), (SPAN, HEAD_DIM))
        l0, l1, l2 = lg[0], lg[1], lg[2]
        m = jnp.maximum(jnp.maximum(l0, l1), l2)
        w0, w1, w2 = jnp.exp(l0 - m), jnp.exp(l1 - m), jnp.exp(l2 - m)
        den = w0 + w1 + w2
        o_ref[...] = (w0 * og[0] + w1 * og[1] + w2 * og[2]) / den
        l_ref[...] = m + jnp.log(den)

    def col(c, prev=False):
        if prev:
            return pl.BlockSpec((t, HEAD_DIM), lambda h, n: (jnp.maximum(n - 1, 0), c + h))
        return pl.BlockSpec((t, HEAD_DIM), lambda h, n: (n, c + h))

    out = jax.ShapeDtypeStruct((s, N_KV_HEADS * HEAD_DIM), F32)
    return pl.pallas_call(
        body, name=name, out_shape=(out, out),
        grid=(N_KV_HEADS, nt),
        in_specs=[col(0), col(N_KV_HEADS), col(2 * N_KV_HEADS), col(kcol), col(kcol, True), col(vcol), col(vcol, True)],
        out_specs=(col(0), col(0)),
        scratch_shapes=[pltpu.VMEM((N_PATTERNS, t, HEAD_DIM), F32), pltpu.VMEM((N_PATTERNS, t, HEAD_DIM), F32)],
        compiler_params=_cp(("parallel", "arbitrary"), 48),
    )(qkr, qkr, qkr, qkr, qkr, proj, proj)


def _attn_bwd(qkr, proj, attn, lse, dcat, *, name, comm=None):
    s = qkr.shape[0]
    t = ATTN_TILE
    nt = s // t
    scale = HEAD_DIM ** -0.5
    kcol, vcol = N_PATTERNS * N_KV_HEADS, (N_PATTERNS + 1) * N_KV_HEADS
    blocks = _attn_blocks()

    def body(q0, q1, q2, kc_ref, kp_ref, vc_ref, vp_ref, o_ref, l_ref, do_ref,
             dq0, dq1, dq2, dk_ref, dv_ref, ck, cv, tkc, tvc, tkp, tvp):
        n = pl.program_id(1)
        for ref in (tkc, tvc, tkp, tvp):
            ref[...] = jnp.zeros_like(ref)

        @pl.when(n < nt)
        def _():
            band, band_first = _band_masks(n)
            q_refs, dq_refs = (q0, q1, q2), (dq0, dq1, dq2)
            for g, start, dil, prev_in_tile, prev_start in blocks:
                idx = pl.ds(start, SPAN, stride=dil) if dil > 1 else pl.ds(start, SPAN)
                q = q_refs[g][idx, :].astype(BF16)
                if prev_in_tile is not None:
                    kp, vp, mask = _rows(kc_ref, prev_in_tile, dil), _rows(vc_ref, prev_in_tile, dil), band
                else:
                    kp, vp, mask = _rows(kp_ref, prev_start, dil), _rows(vp_ref, prev_start, dil), band_first
                kk = jnp.concatenate([kp, kc_ref[idx, :]], axis=0).astype(BF16)
                vv = jnp.concatenate([vp, vc_ref[idx, :]], axis=0).astype(BF16)
                do = do_ref[idx, :]
                dsum = jnp.sum(do * o_ref[idx, :], axis=-1, keepdims=True)
                lrow = l_ref[idx, :][:, :1]
                dob = do.astype(BF16)
                sc = lax.dot_general(q, kk, (((1,), (1,)), ((), ())), preferred_element_type=F32) * scale
                p = jnp.where(mask, jnp.exp(sc - lrow), 0.0)
                dp = lax.dot_general(dob, vv, (((1,), (1,)), ((), ())), preferred_element_type=F32)
                ds = (p * (dp - dsum) * scale).astype(BF16)
                pb = p.astype(BF16)
                dq_refs[g][idx, :] = jnp.dot(ds, kk, preferred_element_type=F32)
                dkk = lax.dot_general(ds, q, (((0,), (0,)), ((), ())), preferred_element_type=F32)
                dvv = lax.dot_general(pb, dob, (((0,), (0,)), ((), ())), preferred_element_type=F32)
                tkc[idx, :] += dkk[SPAN:]
                tvc[idx, :] += dvv[SPAN:]
                if prev_in_tile is not None:
                    pidx = pl.ds(prev_in_tile, SPAN, stride=dil) if dil > 1 else pl.ds(prev_in_tile, SPAN)
                    tkc[pidx, :] += dkk[:SPAN]
                    tvc[pidx, :] += dvv[:SPAN]
                else:
                    pidx = pl.ds(prev_start, SPAN, stride=dil) if dil > 1 else pl.ds(prev_start, SPAN)
                    tkp[pidx, :] += dkk[:SPAN]
                    tvp[pidx, :] += dvv[:SPAN]

        @pl.when(n > 0)
        def _():
            dk_ref[...] = ck[...] + tkp[...]
            dv_ref[...] = (cv[...] + tvp[...]).astype(BF16)

        ck[...] = tkc[...]
        cv[...] = tvc[...]

    def col(c, prev=False):
        if prev:
            return pl.BlockSpec((t, HEAD_DIM), lambda h, n: (jnp.maximum(jnp.minimum(n, nt - 1) - 1, 0), c + h))
        return pl.BlockSpec((t, HEAD_DIM), lambda h, n: (jnp.minimum(n, nt - 1), c + h))

    kv_out = pl.BlockSpec((t, HEAD_DIM), lambda h, n: (jnp.maximum(n - 1, 0), h))
    tile = pltpu.VMEM((t, HEAD_DIM), F32)
    per_head = jax.ShapeDtypeStruct((s, N_KV_HEADS * HEAD_DIM), F32)
    return _call(
        body, name=name,
        out_shape=(per_head, per_head, per_head, per_head, jax.ShapeDtypeStruct((s, N_KV_HEADS * HEAD_DIM), BF16)),
        grid=(N_KV_HEADS, nt + 1),
        in_specs=[col(0), col(N_KV_HEADS), col(2 * N_KV_HEADS), col(kcol), col(kcol, True), col(vcol), col(vcol, True),
                  col(0), col(0), col(0)],
        out_specs=(col(0), col(0), col(0), kv_out, kv_out),
        scratch_shapes=[tile] * 6,
        args=(qkr, qkr, qkr, qkr, qkr, proj, proj, attn, lse, dcat), vmem_mib=48, comm=comm)


GELU_C0 = 0.7978845608028654
GELU_C1 = 0.044715


def _softplus_neg(lam):
    y = jnp.exp(-jnp.abs(lam))
    w = 1.0 + y
    log1p = jnp.where(w == 1.0, y, jnp.log(w) * (y / jnp.where(w == 1.0, 1.0, w - 1.0)))
    return jnp.maximum(-lam, 0.0) + log1p


def _down(cur, prev, k, row):
    if k == 0:
        return cur
    return jnp.where(row < k, pltpu.roll(prev, k, 0), pltpu.roll(cur, k, 0))


def _up(cur, nxt, k, row, tt):
    if k == 0:
        return cur
    return jnp.where(row >= tt - k, pltpu.roll(nxt, tt - k, 0), pltpu.roll(cur, tt - k, 0))


def _lru_gates(x, xp, cw, cb, wr, br, wi, bi, lam, row):
    shifts = [_down(x, xp, k, row) for k in range(CONV_WIDTH)]
    xc = cb
    for j in range(CONV_WIDTH):
        xc = xc + cw[j:j + 1, :] * shifts[CONV_WIDTH - 1 - j]
    xcb = xc.astype(BF16)
    r = jax.nn.sigmoid(jnp.dot(xcb, wr, preferred_element_type=F32) + br)
    i = jax.nn.sigmoid(jnp.dot(xcb, wi, preferred_element_type=F32) + bi)
    c = -LRU_C * _softplus_neg(lam)
    la = c * r
    a = jnp.exp(la)
    mult = jnp.sqrt(jnp.tanh(-la) * (a * a + 1.0))
    return shifts, xc, xcb, r, i, c, a, mult


def _lru_fwd(proj, cw, cb, wr, br, wi, bi, lam, *, tt, name):
    s = proj.shape[0]
    nblk = wr.shape[0]
    c = nblk * LANES
    tt = min(tt, s)
    xcol0 = (N_PATTERNS + 2) * N_KV_HEADS
    ycol0 = xcol0 + nblk

    def body(x_ref, y_ref, cw_ref, cb_ref, wr_ref, br_ref, wi_ref, bi_ref, lam_ref, rec_ref, h_ref, xprev, hc):
        n = pl.program_id(1)

        @pl.when(n == 0)
        def _():
            xprev[...] = jnp.zeros_like(xprev)
            hc[...] = jnp.zeros_like(hc)

        row = lax.broadcasted_iota(jnp.int32, (tt, LANES), 0)
        x = x_ref[...]
        _, xc, _, _, i, _, a, mult = _lru_gates(
            x, xprev[...], cw_ref[...], cb_ref[...], wr_ref[0].astype(BF16), br_ref[...],
            wi_ref[0].astype(BF16), bi_ref[...], lam_ref[...], row)
        av, bv = a, mult * (i * xc)
        k = 1
        while k < tt:
            bs = jnp.where(row < k, 0.0, pltpu.roll(bv, k, 0))
            as_ = jnp.where(row < k, 1.0, pltpu.roll(av, k, 0))
            bv = bv + av * bs
            av = av * as_
            k *= 2
        h = bv + av * hc[0:1, :]
        hc[...] = jnp.broadcast_to(h[tt - 1:tt, :], hc.shape)
        h_ref[...] = h
        y = y_ref[...]
        gel = 0.5 * y * (1.0 + jnp.tanh(GELU_C0 * (y + GELU_C1 * y * y * y)))
        rec_ref[...] = (h * gel).astype(BF16)
        xprev[...] = x

    vec = pl.BlockSpec((1, LANES), lambda b, n: (0, b))
    wblk = pl.BlockSpec((1, LANES, LANES), lambda b, n: (b, 0, 0))
    out = pl.BlockSpec((tt, LANES), lambda b, n: (n, b))
    return pl.pallas_call(
        body, name=name,
        out_shape=(jax.ShapeDtypeStruct((s, c), BF16), jax.ShapeDtypeStruct((s, c), F32)),
        grid=(nblk, s // tt),
        in_specs=[pl.BlockSpec((tt, LANES), lambda b, n: (n, xcol0 + b)),
                  pl.BlockSpec((tt, LANES), lambda b, n: (n, ycol0 + b)),
                  pl.BlockSpec((CONV_WIDTH, LANES), lambda b, n: (0, b)), vec, wblk, vec, wblk, vec, vec],
        out_specs=(out, out),
        scratch_shapes=[pltpu.VMEM((tt, LANES), F32), pltpu.VMEM((8, LANES), F32)],
        compiler_params=_cp(("parallel", "arbitrary"), 32),
    )(proj, proj, cw, cb, wr, br, wi, bi, lam)


def _lru_bwd(proj, hseq, dcat, cw, cb, wr, br, wi, bi, lam, *, tt, name, comm=None):
    s = proj.shape[0]
    nblk = wr.shape[0]
    c = nblk * LANES
    tt = min(tt, s)
    nt = s // tt
    xcol0 = (N_PATTERNS + 2) * N_KV_HEADS
    ycol0 = xcol0 + nblk
    rcol0 = N_KV_HEADS

    def body(x_ref, xp_ref, y_ref, h_ref, hp_ref, dr_ref, cw_ref, cb_ref, wr_ref, br_ref, wi_ref, bi_ref, lam_ref,
             dx_ref, dy_ref, dcw_ref, dcb_ref, dwr_ref, dbr_ref, dwi_ref, dbi_ref, dlam_ref, dxc_next, gcar, acar):
        n = pl.program_id(1)
        rt = nt - 1 - n

        @pl.when(n == 0)
        def _():
            for ref in (dxc_next, gcar, acar, dcw_ref, dcb_ref, dwr_ref, dbr_ref, dwi_ref, dbi_ref, dlam_ref):
                ref[...] = jnp.zeros_like(ref)

        row = lax.broadcasted_iota(jnp.int32, (tt, LANES), 0)
        x = x_ref[...]
        xp = jnp.where(rt > 0, xp_ref[...], 0.0)
        cwv = cw_ref[...]
        wrb, wib = wr_ref[0].astype(BF16), wi_ref[0].astype(BF16)
        lam_v = lam_ref[...]
        shifts, xc, xcb, r, i, cc, a, mult = _lru_gates(x, xp, cwv, cb_ref[...], wrb, br_ref[...], wib, bi_ref[...],
                                                        lam_v, row)
        h = h_ref[...]
        hp_last = jnp.where(rt > 0, hp_ref[7:8, :], 0.0)
        hprev = jnp.where(row < 1, hp_last, pltpu.roll(h, 1, 0))
        y = y_ref[...]
        y2 = y * y
        th = jnp.tanh(GELU_C0 * (y + GELU_C1 * y2 * y))
        gel = 0.5 * y * (1.0 + th)
        dgel = 0.5 * (1.0 + th) + 0.5 * y * (1.0 - th * th) * GELU_C0 * (1.0 + 3.0 * GELU_C1 * y2)
        drec = dr_ref[...]
        dy_ref[...] = (drec * h * dgel).astype(BF16)
        av = jnp.where(row >= tt - 1, acar[0:1, :], pltpu.roll(a, tt - 1, 0))
        bv = drec * gel
        k = 1
        while k < tt:
            bs = jnp.where(row >= tt - k, 0.0, pltpu.roll(bv, tt - k, 0))
            as_ = jnp.where(row >= tt - k, 1.0, pltpu.roll(av, tt - k, 0))
            bv = bv + av * bs
            av = av * as_
            k *= 2
        g = bv + av * gcar[0:1, :]
        gcar[...] = jnp.broadcast_to(g[0:1, :], gcar.shape)
        acar[...] = jnp.broadcast_to(a[0:1, :], acar.shape)
        da = g * hprev
        d_ixc = g * mult
        dmult = g * (i * xc)
        di = d_ixc * xc
        dxc = d_ixc * i
        a2 = a * a
        dla = da * a - dmult * (a2 / mult)
        dr = dla * cc
        dsp = jnp.sum(dla * r, axis=0, keepdims=True) * (-LRU_C)
        dlam_ref[...] += dsp * (-jax.nn.sigmoid(-lam_v))
        dzr = dr * r * (1.0 - r)
        dzi = di * i * (1.0 - i)
        dbr_ref[...] += jnp.sum(dzr, axis=0, keepdims=True)
        dbi_ref[...] += jnp.sum(dzi, axis=0, keepdims=True)
        dzrb, dzib = dzr.astype(BF16), dzi.astype(BF16)
        tn = (((0,), (0,)), ((), ()))
        ntd = (((1,), (1,)), ((), ()))
        dwr_ref[0] += lax.dot_general(xcb, dzrb, tn, preferred_element_type=F32)
        dwi_ref[0] += lax.dot_general(xcb, dzib, tn, preferred_element_type=F32)
        dxc = (dxc + lax.dot_general(dzrb, wrb, ntd, preferred_element_type=F32)
               + lax.dot_general(dzib, wib, ntd, preferred_element_type=F32))
        dcb_ref[...] += jnp.sum(dxc, axis=0, keepdims=True)
        dcw_ref[...] += jnp.concatenate(
            [jnp.sum(dxc * shifts[CONV_WIDTH - 1 - j], axis=0, keepdims=True) for j in range(CONV_WIDTH)], axis=0)
        nxt = dxc_next[...]
        dx = cwv[0:1, :] * _up(dxc, nxt, CONV_WIDTH - 1, row, tt)
        for j in range(1, CONV_WIDTH):
            dx = dx + cwv[j:j + 1, :] * _up(dxc, nxt, CONV_WIDTH - 1 - j, row, tt)
        dx_ref[...] = dx.astype(BF16)
        dxc_next[...] = dxc

    def tile(col0, prev=False):
        if prev:
            return pl.BlockSpec((tt, LANES), lambda b, n: (jnp.maximum(nt - 2 - n, 0), col0 + b))
        return pl.BlockSpec((tt, LANES), lambda b, n: (nt - 1 - n, col0 + b))

    vec = pl.BlockSpec((1, LANES), lambda b, n: (0, b))
    wblk = pl.BlockSpec((1, LANES, LANES), lambda b, n: (b, 0, 0))
    cwblk = pl.BlockSpec((CONV_WIDTH, LANES), lambda b, n: (0, b))
    hp8 = pl.BlockSpec((8, LANES), lambda b, n: (jnp.maximum((nt - 1 - n) * (tt // 8) - 1, 0), b))
    vshape = jax.ShapeDtypeStruct((1, c), F32)
    wshape = jax.ShapeDtypeStruct((nblk, LANES, LANES), F32)
    return _call(
        body, name=name,
        out_shape=(jax.ShapeDtypeStruct((s, c), BF16), jax.ShapeDtypeStruct((s, c), BF16),
                   jax.ShapeDtypeStruct((CONV_WIDTH, c), F32), vshape, wshape, vshape, wshape, vshape, vshape),
        grid=(nblk, nt),
        in_specs=[tile(xcol0), tile(xcol0, True), tile(ycol0), tile(0), hp8, tile(rcol0),
                  cwblk, vec, wblk, vec, wblk, vec, vec],
        out_specs=(tile(0), tile(0), cwblk, vec, wblk, vec, wblk, vec, vec),
        scratch_shapes=[pltpu.VMEM((tt, LANES), F32), pltpu.VMEM((8, LANES), F32), pltpu.VMEM((8, LANES), F32)],
        args=(proj, proj, proj, hseq, hseq, dcat, cw, cb, wr, br, wi, bi, lam), vmem_mib=32, comm=comm)


ROW_BLOCKS = (512, 256, 176, 128, 64, 32, 16, 8)


def _adamw(w, m, v, gparts, *, name):
    r, c = w.shape
    npart = gparts.shape[0]
    br = _pick(r, ROW_BLOCKS)
    c1 = 1.0 - ADAM_B1 ** ADAM_STEP
    c2 = 1.0 - ADAM_B2 ** ADAM_STEP

    def body(w_ref, m_ref, v_ref, g_ref, go_ref, d_ref, mo_ref, vo_ref):
        g = g_ref[0].astype(F32)
        for q in range(1, npart):
            g = g + g_ref[q].astype(F32)
        mn = ADAM_B1 * m_ref[...] + (1.0 - ADAM_B1) * g
        vn = ADAM_B2 * v_ref[...] + (1.0 - ADAM_B2) * (g * g)
        go_ref[...] = g
        mo_ref[...] = mn
        vo_ref[...] = vn
        d_ref[...] = -ADAM_LR * ((mn / c1) / (jnp.sqrt(vn / c2) + ADAM_EPS) + ADAM_WD * w_ref[...])

    blk = pl.BlockSpec((br, c), lambda i: (i, 0))
    out = jax.ShapeDtypeStruct((r, c), F32)
    return pl.pallas_call(
        body, name=name, out_shape=(out, out, out, out), grid=(r // br,),
        in_specs=[blk, blk, blk, pl.BlockSpec((npart, br, c), lambda i: (0, i, 0))],
        out_specs=(blk, blk, blk, blk),
        compiler_params=_cp(("parallel",), 48),
    )(w, m, v, gparts)


def _sum_parts(parts, *, name):
    npart, r, c = parts.shape
    br = next((b for b in range(min(r, 2048) // 8 * 8, 0, -8) if r % b == 0), r)

    def body(p_ref, o_ref):
        acc = p_ref[0]
        for q in range(1, npart):
            acc = acc + p_ref[q]
        o_ref[...] = acc

    return pl.pallas_call(
        body, name=name, out_shape=jax.ShapeDtypeStruct((r, c), F32), grid=(r // br,),
        in_specs=[pl.BlockSpec((npart, br, c), lambda i: (0, i, 0))],
        out_specs=pl.BlockSpec((br, c), lambda i: (i, 0)),
        compiler_params=_cp(("parallel",), 48),
    )(parts)


HBM = pl.BlockSpec(memory_space=pltpu.HBM)


def _mesh_pos():
    return lax.axis_index("x"), lax.axis_index("y"), lax.axis_index("c")


def _all_gather(shards, *, name):
    comm = _gather_comm(shards)
    na = len(shards)

    def body(*refs):
        ins, outs, sems = refs[:na], refs[na:2 * na], refs[2 * na:]
        comm.start(ins, outs, sems)
        comm.mid(ins, outs, sems)
        comm.end(ins, outs, sems)

    return pl.pallas_call(
        body, name=name, out_shape=tuple(comm.out_shapes),
        in_specs=[HBM] * na, out_specs=tuple([HBM] * na), scratch_shapes=comm.scratch,
    )(*shards)


def _gather_comm(shards):
    na = len(shards)

    def parts(x_refs, out_refs, sems):
        send_sems, recv_sems, local_sems = sems
        x, y, c = _mesh_pos()
        me, sibling = (x, y, c), (x, y, 1 - c)
        chips = [(1 - x, y), (x, 1 - y), (1 - x, 1 - y)]

        def copy(a, k, block, to, src=None):
            px, py, pc = block
            dst = out_refs[a].at[4 * px + 2 * py + pc]
            return pltpu.make_async_remote_copy(
                src_ref=dst if src is None else src, dst_ref=dst,
                send_sem=send_sems.at[a, k], recv_sem=recv_sems.at[a, k],
                device_id=to, device_id_type=MESH)

        def mine(a):
            return pltpu.make_async_copy(x_refs[a], out_refs[a].at[4 * x + 2 * y + c], local_sems.at[a])

        def first(a):
            return [copy(a, 0, me, sibling, src=x_refs[a])] + [
                copy(a, 1 + j, me, (*chip, c), src=x_refs[a]) for j, chip in enumerate(chips)]

        def passed(a, j):
            return copy(a, 4 + j, (*chips[j], c), sibling)

        return me, sibling, chips, c, copy, mine, first, passed

    def start(x_refs, out_refs, sems):
        *_, mine, first, _ = parts(x_refs, out_refs, sems)
        for a in range(na):
            mine(a).start()
            for cp in first(a):
                cp.start()

    def mid(x_refs, out_refs, sems):
        me, _, chips, c, copy, _, _, passed = parts(x_refs, out_refs, sems)
        for j, chip in enumerate(chips):
            for a in range(na):
                copy(a, 1 + j, (*chip, c), me).wait_recv()
                passed(a, j).start()

    def end(x_refs, out_refs, sems):
        me, sibling, chips, c, copy, mine, first, passed = parts(x_refs, out_refs, sems)
        for a in range(na):
            copy(a, 0, sibling, me).wait_recv()
            for j, chip in enumerate(chips):
                copy(a, 4 + j, (*chip, 1 - c), me).wait_recv()
        for a in range(na):
            for cp in first(a) + [passed(a, j) for j in range(3)]:
                cp.wait_send()
            mine(a).wait()

    return _Comm(
        shards, [jax.ShapeDtypeStruct((N_DEV,) + a.shape, a.dtype) for a in shards],
        [pltpu.SemaphoreType.DMA((na, 7)), pltpu.SemaphoreType.DMA((na, 7)), pltpu.SemaphoreType.DMA((na,))],
        start, end, mid)


def _scatter_comm(g8s):
    na = len(g8s)

    def parts(g_refs, buf_refs, sems):
        send_sems, recv_sems, local_sems = sems
        x, y, c = _mesh_pos()
        me_idx = 4 * x + 2 * y + c

        def copy(a, k, slot):
            peer, peer_idx = _scatter_peer(k, x, y, c)
            return pltpu.make_async_remote_copy(
                src_ref=g_refs[a].at[peer_idx], dst_ref=buf_refs[a].at[me_idx if slot is None else slot],
                send_sem=send_sems.at[a, k - 1], recv_sem=recv_sems.at[a, k - 1],
                device_id=peer, device_id_type=MESH)

        def mine(a):
            return pltpu.make_async_copy(g_refs[a].at[me_idx], buf_refs[a].at[me_idx], local_sems.at[a])

        return x, y, c, copy, mine

    def start(g_refs, buf_refs, sems):
        *_, copy, mine = parts(g_refs, buf_refs, sems)
        for a in range(na):
            mine(a).start()
            for k in range(1, N_DEV):
                copy(a, k, None).start()

    def end(g_refs, buf_refs, sems):
        x, y, c, copy, mine = parts(g_refs, buf_refs, sems)
        for a in range(na):
            for k in range(1, N_DEV):
                copy(a, k, _scatter_peer(k, x, y, c)[1]).wait_recv()
        for a in range(na):
            for k in range(1, N_DEV):
                copy(a, k, None).wait_send()
            mine(a).wait()

    return _Comm(
        g8s, [jax.ShapeDtypeStruct(g.shape, g.dtype) for g in g8s],
        [pltpu.SemaphoreType.DMA((na, N_DEV - 1)), pltpu.SemaphoreType.DMA((na, N_DEV - 1)),
         pltpu.SemaphoreType.DMA((na,))],
        start, end)


def _scatter_peer(k, x, y, c):
    px, py, pc = (1 - x if k & 4 else x, 1 - y if k & 2 else y, 1 - c if k & 1 else c)
    return (px, py, pc), 4 * px + 2 * py + pc


def _scatter_start_comm(g8):
    land = lax.empty(g8.shape, g8.dtype)

    def start(refs, outs, sems):
        g_ref, land_ref = refs
        send_sems, recv_sems = outs[:2]
        x, y, c = _mesh_pos()
        me_idx = 4 * x + 2 * y + c
        pltpu.make_async_copy(g_ref.at[me_idx], land_ref.at[me_idx], sems[0]).start()
        for k in range(1, N_DEV):
            peer, peer_idx = _scatter_peer(k, x, y, c)
            pltpu.make_async_remote_copy(
                src_ref=g_ref.at[peer_idx], dst_ref=land_ref.at[me_idx],
                send_sem=send_sems.at[k - 1], recv_sem=recv_sems.at[k - 1],
                device_id=peer, device_id_type=MESH).start()

    def end(refs, outs, sems):
        g_ref, land_ref = refs
        x, y, c = _mesh_pos()
        me_idx = 4 * x + 2 * y + c
        pltpu.make_async_copy(g_ref.at[me_idx], land_ref.at[me_idx], sems[0]).wait()

    sem_shape = pltpu.SemaphoreType.DMA((N_DEV - 1,))
    return _Comm([g8, land], [sem_shape, sem_shape, pltpu.HBM(g8.shape, g8.dtype), pltpu.HBM(g8.shape, g8.dtype)],
                 [pltpu.SemaphoreType.DMA(())], start, end, split=True)


def _scatter_wait(started, after, *, name):
    send_sems, recv_sems, g8, land = started

    def body(g_ref, land_ref, send_ref, recv_ref, after_ref, g_dead, got_ref):
        x, y, c = _mesh_pos()
        me_idx = 4 * x + 2 * y + c
        for k in range(1, N_DEV):
            peer, peer_idx = _scatter_peer(k, x, y, c)
            pltpu.make_async_remote_copy(
                src_ref=g_ref.at[peer_idx], dst_ref=land_ref.at[peer_idx],
                send_sem=send_ref.at[k - 1], recv_sem=recv_ref.at[k - 1],
                device_id=peer, device_id_type=MESH).wait()

    sem = pl.BlockSpec(memory_space=pltpu.SEMAPHORE)
    return pl.pallas_call(
        body, name=name, out_shape=(pltpu.HBM(g8.shape, g8.dtype), pltpu.HBM(land.shape, land.dtype)),
        in_specs=[HBM, HBM, sem, sem, pl.BlockSpec(memory_space=pl.ANY)], out_specs=(HBM, HBM),
        input_output_aliases={0: 0, 1: 1},
        compiler_params=pltpu.CompilerParams(has_side_effects=pltpu.SideEffectType.DATAFLOW_SIDE_EFFECTING),
    )(g8, land, send_sems, recv_sems, after)[1]


BIG_WEIGHTS = ("ffn1_w_gate", "ffn1_w_up", "ffn1_w_down", "w_in", "w_out",
               "ffn2_w_gate", "ffn2_w_up", "ffn2_w_down", "w_ple_proj", "w_ple_gate")
COLUMN_SHARDED = ("ffn1_w_gate", "ffn1_w_up", "w_in", "ffn2_w_gate", "ffn2_w_up", "w_ple_proj", "conv_w")
SMALL_WEIGHTS = ("ln1_g", "ln1_b", "conv_b", "w_rgate", "b_rgate", "w_igate", "b_igate", "lru_lambda",
                 "ln2_g", "ln2_b", "ln3_g", "ln3_b")
SMALL_GRADS = SMALL_WEIGHTS + ("conv_w",)


class _Exchange:
    def __init__(self, full):
        self.full = dict(full)
        self.grads = {}

    def __getitem__(self, name):
        return self.full[name]

    def gather(self, names):
        return None, None

    def scatter(self, names):
        return None, None

    def scatter_start(self, name):
        return None, None

    def gather_small(self):
        return None, None


class _MeshExchange(_Exchange):
    def __init__(self, full, shards):
        super().__init__(full)
        self.shards = shards
        self.reduced = {}
        self.started = {}
        self.small_parts = None

    def gather(self, names):
        def done(outs):
            for n, o in zip(names, outs):
                self.take(n, o)
        return _gather_comm([self.shards[n] for n in names]), done

    def take(self, name, gathered):
        self.full[name] = gathered.reshape((N_DEV * gathered.shape[1],) + gathered.shape[2:])

    def scatter(self, names):
        def done(outs):
            self.reduced.update(zip(names, outs))
        return _scatter_comm([_to_owner_blocks(n, self.grads[n]) for n in names]), done

    def scatter_start(self, name):
        def done(outs):
            self.started[name] = outs
        return _scatter_start_comm(_to_owner_blocks(name, self.grads[name])), done

    def finish(self, after):
        for name, started in self.started.items():
            self.reduced[name] = _scatter_wait(started, after, name=f"scatter_wait_{name}")

    def gather_small(self):
        def done(outs):
            self.small_parts, = outs
        packed = jnp.concatenate([_rows128(self.grads[n]) for n in SMALL_GRADS], axis=0)
        return _gather_comm([packed]), done


def _carried(comm_done, call):
    comm, done = comm_done
    res = call(comm)
    if comm is None:
        return res
    res, outs = res
    done(outs)
    return res


def _dw(a, b, *, scale=1.0, name, comm=None):
    k, m = a.shape
    n = b.shape[1]
    return _mm(a, b, ta=True, scale=scale, out_dtype=BF16, bm=_pick(m, (1024, 512, 256, 128)),
               bn=_pick(n, (512, 256, 128)), bk=k, name=name, comm=comm)


def _ffn_bwd(ex, names, saved, xb_in, dz, dzb, ln_in, tag, on_dwd=None, on_dh=None, on_dwu=None, on_dx=None):
    gate, up, down = names
    g, u, h, _, _ = saved
    f = ex[gate].shape[0]

    def request(fn):
        return (None, None) if fn is None else fn(ex)

    ex.grads[down] = _carried(request(on_dwd), lambda c: _dw(h, dzb, scale=0.5, name=f"{tag}_dwd", comm=c))
    dg, du = _carried(request(on_dh), lambda c: _ffn_bwd_dh(
        dzb, ex[down], g, u, scale=0.5, bm=2048, bn=_pick(f, (512, 256, 128)), name=f"{tag}_dh", chunks=8, comm=c))
    ex.grads[gate] = _dw(xb_in, dg, name=f"{tag}_dwg")
    ex.grads[up] = _carried(request(on_dwu), lambda c: _dw(xb_in, du, name=f"{tag}_dwu", comm=c))
    d = dz.shape[1]
    dx = _carried(request(on_dx), lambda c: _ffn_dx(
        dg, du, ex[gate], ex[up], dz, extra_scale=DEEPNORM_ALPHA,
        bm=512, bn=_pick(d, (512, 256, 128)), name=f"{tag}_dx", comm=c))
    return dx if ln_in is None else _ln_bwd(dx, *ln_in, bm=256, name=f"{tag}_ln_bwd")


def _local_step(x, p, target, positions, w):
    s, d = x.shape
    tabs = _rope_tables(positions)
    xb = x.astype(BF16)
    f = w["ffn1_w_gate"].shape[0]
    ffn_bn, ln_bn, ln_bn_short_k = _pick(f, (512, 256, 128)), _pick(d, (512, 256, 128)), _pick(d, (1024, 512, 256, 128))
    g1, u1, h1 = _carried(w.gather(("ffn1_w_down", "w_in", "w_out")), lambda c: _ffn_up(
        xb, w["ffn1_w_gate"], w["ffn1_w_up"], bm=1024, bn=ffn_bn, name="ffn1_up", comm=c))
    x1, x1b, xh1, rs1 = _carried(w.gather(("ffn2_w_gate", "ffn2_w_up")), lambda c: _mm_ln(
        h1, w["ffn1_w_down"], x, w["ln1_g"], w["ln1_b"], res_scale=DEEPNORM_ALPHA, mm_scale=0.5,
        bm=512, bn=ln_bn, name="ffn1_down_ln", comm=c))
    sv1 = (g1, u1, h1, xh1, rs1)
    pw = w["w_in"].shape[0]
    proj = _carried(w.gather(("ffn2_w_down", "w_ple_gate", "w_ple_proj")), lambda c: _mm(
        x1b, w["w_in"], tb=True, bm=1024, bn=_pick(pw, (512, 256, 128)), bk=d, name="in_proj", comm=c))
    nqk = (N_PATTERNS + 1) * N_KV_HEADS
    qkr = _rotary(proj, tabs, n_cols=nqk, inverse=False, out_dtype=F32, bs=1024, name="rotary")
    attn, lse = _attn_fwd(qkr, proj, name="attn_fwd")
    lru_w = (w["conv_w"], w["conv_b"], w["w_rgate"], w["b_rgate"], w["w_igate"], w["b_igate"], w["lru_lambda"])
    rec, hseq = _lru_fwd(proj, *lru_w, tt=512, name="lru_fwd")
    cat = jnp.concatenate([attn.astype(BF16), rec], axis=1)
    x2, x2b, xh2, rs2 = _mm_ln(cat, w["w_out"], x1, w["ln2_g"], w["ln2_b"], res_scale=DEEPNORM_ALPHA, mm_scale=1.0,
                               bm=512, bn=ln_bn_short_k, name="out_proj_ln")
    g2, u2, h2 = _ffn_up(x2b, w["ffn2_w_gate"], w["ffn2_w_up"], bm=1024, bn=ffn_bn, name="ffn2_up")
    x3, x3b, xh3, rs3 = _mm_ln(h2, w["ffn2_w_down"], x2, w["ln3_g"], w["ln3_b"], res_scale=DEEPNORM_ALPHA, mm_scale=0.5,
                               bm=512, bn=ln_bn, name="ffn2_down_ln")
    sv3 = (g2, u2, h2, xh3, rs3)
    lsum, dy, dgate, dple = _ple_loss(x3, x3b, p, w["w_ple_gate"], w["w_ple_proj"], target,
                                      bm=1024, bn=_pick(d, (512, 256, 128)), name="ple_loss")
    grads = w.grads
    grads["w_ple_gate"] = _dw(x3b, dgate, name="dw_ple_gate")
    grads["w_ple_proj"] = _dw(p, dple, name="dw_ple_proj")
    dz3, dz3b, grads["ln3_g"], grads["ln3_b"] = _carried(w.scatter(("w_ple_gate", "w_ple_proj")), lambda c: _mm_dx(
        dgate, w["w_ple_gate"], dy, xh3, rs3, w["ln3_g"], extra_scale=1.0, bm=512, bn=ln_bn_short_k,
        name="ple_dx", tb=True, comm=c))
    dz2, dz2b, grads["ln2_g"], grads["ln2_b"] = _ffn_bwd(
        w, ("ffn2_w_gate", "ffn2_w_up", "ffn2_w_down"), sv3, x2b, dz3, dz3b, (xh2, rs2, w["ln2_g"]), "ffn2",
        on_dx=lambda ex: ex.scatter(("ffn2_w_down",)))
    grads["w_out"] = _dw(cat, dz2b, name="dw_out")
    dcat = _mm(dz2b, w["w_out"], tb=True, bm=1024, bn=_pick(d, (512, 256, 128)), bk=d, name="out_proj_dx")
    dq0, dq1, dq2, dk, dvb = _carried(w.scatter(("ffn2_w_gate",)), lambda c: _attn_bwd(
        qkr, proj, attn, lse, dcat, name="attn_bwd", comm=c))
    nh = N_KV_HEADS
    dqkv = [_rotary(t, tabs, n_cols=nh, inverse=True, out_dtype=BF16, bs=1024, name=f"rotary_bwd{i}")
            for i, t in enumerate((dq0, dq1, dq2, dk))]
    (dxb, dyb, grads["conv_w"], grads["conv_b"], grads["w_rgate"], grads["b_rgate"], grads["w_igate"],
     grads["b_igate"], grads["lru_lambda"]) = _carried(w.scatter(("ffn2_w_up",)), lambda c: _lru_bwd(
         proj, hseq, dcat, *lru_w, tt=512, name="lru_bwd", comm=c))
    dproj = jnp.concatenate(dqkv + [dvb, dxb, dyb], axis=1)
    grads["w_in"] = _carried(w.scatter(("w_out",)), lambda c: _dw(x1b, dproj, name="dw_in", comm=c))
    dz1, dz1b, grads["ln1_g"], grads["ln1_b"] = _carried(w.scatter(("w_in",)), lambda c: _mm_dx(
        dproj, w["w_in"], dz2, xh1, rs1, w["ln1_g"], extra_scale=DEEPNORM_ALPHA,
        bm=512, bn=ln_bn, name="in_proj_dx", comm=c))
    grad_x = _ffn_bwd(w, ("ffn1_w_gate", "ffn1_w_up", "ffn1_w_down"), sv1, xb, dz1, dz1b, None, "ffn1",
                      on_dwd=lambda ex: ex.gather_small(),
                      on_dh=lambda ex: ex.scatter_start("ffn1_w_down"),
                      on_dwu=lambda ex: ex.scatter_start("ffn1_w_gate"),
                      on_dx=lambda ex: ex.scatter_start("ffn1_w_up"))
    return lsum, grad_x


def _to_full(name, gathered):
    if name in COLUMN_SHARDED:
        _, r, c = gathered.shape
        return jnp.transpose(gathered, (1, 0, 2)).reshape(r, N_DEV * c)
    return gathered.reshape((N_DEV * gathered.shape[1],) + gathered.shape[2:])


def _to_owner_blocks(name, full):
    if name in COLUMN_SHARDED:
        r, c = full.shape
        return jnp.transpose(full.reshape(r, N_DEV, c // N_DEV), (1, 0, 2))
    return full.reshape((N_DEV, full.shape[0] // N_DEV) + full.shape[1:])


def _rows128(a):
    flat = a.reshape(-1, LANES)
    pad = (-flat.shape[0]) % 8
    return jnp.pad(flat, ((0, pad), (0, 0))) if pad else flat


def kernel(x, p, positions, ffn1_w_gate, ffn1_w_up, ffn1_w_down, ln1_g, ln1_b, w_in, conv_w, conv_b, w_rgate, b_rgate, w_igate, b_igate, lru_lambda, w_out, ln2_g, ln2_b, ffn2_w_gate, ffn2_w_up, ffn2_w_down, ln3_g, ln3_b, w_ple_proj, w_ple_gate, loss_target, m_ffn1_w_gate, m_ffn1_w_up, m_ffn1_w_down, m_ln1_g, m_ln1_b, m_w_in, m_conv_w, m_conv_b, m_w_rgate, m_b_rgate, m_w_igate, m_b_igate, m_lru_lambda, m_w_out, m_ln2_g, m_ln2_b, m_ffn2_w_gate, m_ffn2_w_up, m_ffn2_w_down, m_ln3_g, m_ln3_b, m_w_ple_proj, m_w_ple_gate, v_ffn1_w_gate, v_ffn1_w_up, v_ffn1_w_down, v_ln1_g, v_ln1_b, v_w_in, v_conv_w, v_conv_b, v_w_rgate, v_b_rgate, v_w_igate, v_b_igate, v_lru_lambda, v_w_out, v_ln2_g, v_ln2_b, v_ffn2_w_gate, v_ffn2_w_up, v_ffn2_w_down, v_ln3_g, v_ln3_b, v_w_ple_proj, v_w_ple_gate):
    names = ("ffn1_w_gate", "ffn1_w_up", "ffn1_w_down", "ln1_g", "ln1_b", "w_in", "conv_w", "conv_b", "w_rgate",
             "b_rgate", "w_igate", "b_igate", "lru_lambda", "w_out", "ln2_g", "ln2_b", "ffn2_w_gate", "ffn2_w_up",
             "ffn2_w_down", "ln3_g", "ln3_b", "w_ple_proj", "w_ple_gate")
    ws = (ffn1_w_gate, ffn1_w_up, ffn1_w_down, ln1_g, ln1_b, w_in, conv_w, conv_b, w_rgate, b_rgate, w_igate, b_igate,
          lru_lambda, w_out, ln2_g, ln2_b, ffn2_w_gate, ffn2_w_up, ffn2_w_down, ln3_g, ln3_b, w_ple_proj, w_ple_gate)
    ms = (m_ffn1_w_gate, m_ffn1_w_up, m_ffn1_w_down, m_ln1_g, m_ln1_b, m_w_in, m_conv_w, m_conv_b, m_w_rgate, m_b_rgate,
          m_w_igate, m_b_igate, m_lru_lambda, m_w_out, m_ln2_g, m_ln2_b, m_ffn2_w_gate, m_ffn2_w_up, m_ffn2_w_down,
          m_ln3_g, m_ln3_b, m_w_ple_proj, m_w_ple_gate)
    vs = (v_ffn1_w_gate, v_ffn1_w_up, v_ffn1_w_down, v_ln1_g, v_ln1_b, v_w_in, v_conv_w, v_conv_b, v_w_rgate, v_b_rgate,
          v_w_igate, v_b_igate, v_lru_lambda, v_w_out, v_ln2_g, v_ln2_b, v_ffn2_w_gate, v_ffn2_w_up, v_ffn2_w_down,
          v_ln3_g, v_ln3_b, v_w_ple_proj, v_w_ple_gate)
    def local(a):
        return a[0] if a.ndim >= 3 else a

    w_loc = {n: local(a) for n, a in zip(names, ws)}
    m_loc = {n: local(a) for n, a in zip(names, ms)}
    v_loc = {n: local(a) for n, a in zip(names, vs)}
    out_shapes = {n: a.shape for n, a in zip(names, ws)}

    shards = {n: (w_loc[n].T if n in COLUMN_SHARDED else w_loc[n]).astype(BF16) for n in BIG_WEIGHTS}
    gate1, up1, conv_all = _all_gather([shards["ffn1_w_gate"], shards["ffn1_w_up"], w_loc["conv_w"]], name="gather_first")
    ex = _MeshExchange({n: w_loc[n] for n in SMALL_WEIGHTS}, shards)
    ex.full["conv_w"] = _to_full("conv_w", conv_all)
    ex.take("ffn1_w_gate", gate1)
    ex.take("ffn1_w_up", up1)

    lsum, grad_x = _local_step(x[0], p[0, 0], loss_target[0], positions[0], ex)
    ex.finish(grad_x)
    grads, reduced = ex.grads, ex.reduced
    d_model = x.shape[-1]
    loss = lax.psum(lsum[0, 0] * (0.5 / d_model), ("x", "y", "c"))

    small = SMALL_GRADS
    summed = _sum_parts(ex.small_parts, name="sum_small_grads")
    small_grads, row = {}, 0
    for n in small:
        rows = grads[n].size // LANES
        small_grads[n] = summed[row:row + rows].reshape(grads[n].shape)
        row += rows + (-rows) % 8
    me = 4 * lax.axis_index("x") + 2 * lax.axis_index("y") + lax.axis_index("c")
    cw_cols = w_loc["conv_w"].shape[1]
    small_grads["conv_w"] = lax.dynamic_slice_in_dim(small_grads["conv_w"], me * cw_cols, cw_cols, axis=1)

    out_g, out_d, out_m, out_v = {}, {}, {}, {}
    for n in names:
        wl, ml, vl = w_loc[n], m_loc[n], v_loc[n]
        shape = wl.shape
        if n in BIG_WEIGHTS:
            gparts = reduced[n]
        else:
            gparts = small_grads[n].reshape((1,) + shape)
        if wl.ndim == 3:
            wl, ml, vl = (t.reshape(-1, shape[-1]) for t in (wl, ml, vl))
            gparts = gparts.reshape(gparts.shape[0], -1, shape[-1])
        res = _adamw(wl, ml, vl, gparts, name=f"adamw_{n}")
        out_g[n], out_d[n], out_m[n], out_v[n] = (t.reshape(out_shapes[n]) for t in res)

    return (loss, grad_x[None], *[out_g[n] for n in names], *[out_d[n] for n in names],
            *[out_m[n] for n in names], *[out_v[n] for n in names])
```

```python
import jax
import jax.numpy as jnp
from jax import lax
from jax.experimental import pallas as pl
from jax.experimental.pallas import tpu as pltpu

F32 = jnp.float32
BF16 = jnp.bfloat16

N_DEV = 8
LANES = 128
MIB = 1 << 20

HEAD_DIM = 128
N_KV_HEADS = 4
DILATIONS = (1, 4, 16)
N_PATTERNS = 3
SPAN = 128
ROT_DIMS = 32
ROPE_THETA = 500000.0
LRU_C = 8.0
CONV_WIDTH = 4
LN_EPS = 1e-5
DEEPNORM_ALPHA = 2.0 ** 0.25
ATTN_TILE = SPAN * DILATIONS[-1]

ADAM_LR = 0.001
ADAM_B1 = 0.9
ADAM_B2 = 0.999
ADAM_EPS = 1e-08
ADAM_WD = 0.01
ADAM_STEP = 10

MESH = pl.DeviceIdType.MESH
NT_DIMS = (((1,), (1,)), ((), ()))
EPILOGUE_ROWS = 64


def _cp(semantics, vmem_mib):
    return pltpu.CompilerParams(dimension_semantics=semantics, vmem_limit_bytes=vmem_mib * MIB)


def _pick(n, candidates):
    for c in candidates:
        if n % c == 0:
            return c
    return n


class _Comm:
    def __init__(self, arrays, out_shapes, scratch, start, end, mid=None, split=False):
        self.arrays, self.out_shapes, self.scratch = list(arrays), list(out_shapes), list(scratch)
        self.start, self.mid, self.end, self.split = start, mid, end, split


def _call(body, *, name, grid, in_specs, out_specs, out_shape, args, scratch_shapes=(), vmem_mib, comm=None):
    single = not isinstance(out_shape, (tuple, list))
    out_shape_t = (out_shape,) if single else tuple(out_shape)
    out_specs_t = (out_specs,) if single else tuple(out_specs)
    params = _cp(("arbitrary",) * len(grid), vmem_mib)
    if comm is None:
        res = pl.pallas_call(body, name=name, grid=grid, in_specs=list(in_specs), out_specs=out_specs_t,
                             out_shape=out_shape_t, scratch_shapes=list(scratch_shapes), compiler_params=params)(*args)
        return res[0] if single else res
    n_in, n_out, n_scr = len(args), len(out_shape_t), len(scratch_shapes)
    nci, nco = len(comm.arrays), len(comm.out_shapes)
    total = 1
    for g in grid:
        total *= g

    def wrapped(*refs):
        ins, refs = refs[:n_in], refs[n_in:]
        cin, refs = refs[:nci], refs[nci:]
        outs, refs = refs[:n_out], refs[n_out:]
        cout, refs = refs[:nco], refs[nco:]
        scr, csem = refs[:n_scr], refs[n_scr:]
        step = pl.program_id(0)
        for ax in range(1, len(grid)):
            step = step * grid[ax] + pl.program_id(ax)

        @pl.when(step == 0)
        def _():
            comm.start(cin, cout, csem)

        body(*ins, *outs, *scr)
        if comm.mid is not None:
            @pl.when(step == (3 * total) // 4)
            def _():
                comm.mid(cin, cout, csem)

        @pl.when(step == total - 1)
        def _():
            comm.end(cin, cout, csem)

    hbm = pl.BlockSpec(memory_space=pltpu.HBM)
    if comm.split:
        sem = pl.BlockSpec(memory_space=pltpu.SEMAPHORE)
        n_sems = nco - nci
        res = pl.pallas_call(
            wrapped, name=name, grid=grid,
            in_specs=list(in_specs) + [hbm] * nci,
            out_specs=out_specs_t + (sem,) * n_sems + (hbm,) * nci,
            out_shape=out_shape_t + tuple(comm.out_shapes),
            scratch_shapes=list(scratch_shapes) + comm.scratch,
            input_output_aliases={n_in + k: n_out + n_sems + k for k in range(nci)},
            compiler_params=pltpu.CompilerParams(
                dimension_semantics=("arbitrary",) * len(grid), vmem_limit_bytes=vmem_mib * MIB,
                has_side_effects=pltpu.SideEffectType.DATAFLOW_SIDE_EFFECTING),
        )(*args, *[pltpu.with_memory_space_constraint(a, pltpu.HBM) for a in comm.arrays])
    else:
        res = pl.pallas_call(
            wrapped, name=name, grid=grid,
            in_specs=list(in_specs) + [hbm] * nci,
            out_specs=out_specs_t + (hbm,) * nco,
            out_shape=out_shape_t + tuple(comm.out_shapes),
            scratch_shapes=list(scratch_shapes) + comm.scratch,
            compiler_params=params)(*args, *comm.arrays)
    own, extra = res[:n_out], res[n_out:]
    return (own[0] if single else own), extra


def _mm(a, b, *, ta=False, tb=False, out_dtype=F32, scale=1.0, bm, bn, bk, name, comm=None):
    m, k = (a.shape[1], a.shape[0]) if ta else a.shape
    n = b.shape[0] if tb else b.shape[1]
    bm, bn, bk = min(bm, m), min(bn, n), min(bk, k)
    assert m % bm == 0 and n % bn == 0 and k % bk == 0, (name, m, n, k, bm, bn, bk)
    nk = k // bk
    a_spec = pl.BlockSpec((bk, bm), lambda i, j, kk: (kk, i)) if ta else pl.BlockSpec((bm, bk), lambda i, j, kk: (i, kk))
    b_spec = pl.BlockSpec((bn, bk), lambda i, j, kk: (j, kk)) if tb else pl.BlockSpec((bk, bn), lambda i, j, kk: (kk, j))
    dn = (((0 if ta else 1,), (1 if tb else 0,)), ((), ()))

    def body(a_ref, b_ref, o_ref, *acc):
        part = lax.dot_general(a_ref[...].astype(BF16), b_ref[...].astype(BF16), dn, preferred_element_type=F32)
        if nk == 1:
            o_ref[...] = (part * scale).astype(out_dtype)
            return
        acc_ref, = acc
        kk = pl.program_id(2)

        @pl.when(kk == 0)
        def _():
            acc_ref[...] = part

        @pl.when(kk > 0)
        def _():
            acc_ref[...] += part

        @pl.when(kk == nk - 1)
        def _():
            o_ref[...] = (acc_ref[...] * scale).astype(out_dtype)

    return _call(
        body, name=name,
        out_shape=jax.ShapeDtypeStruct((m, n), out_dtype),
        grid=(m // bm, n // bn, nk),
        in_specs=[a_spec, b_spec],
        out_specs=pl.BlockSpec((bm, bn), lambda i, j, kk: (i, j)),
        scratch_shapes=[pltpu.VMEM((bm, bn), F32)] if nk > 1 else [],
        args=(a, b), vmem_mib=56, comm=comm)


def _ffn_up(xb, wg, wu, *, bm, bn, name, comm=None):
    s, d = xb.shape
    f = wg.shape[0]
    bm, bn = min(bm, s), min(bn, f)
    assert s % bm == 0 and f % bn == 0

    def body(x_ref, wg_ref, wu_ref, hg_ref, hu_ref, h_ref):
        x = x_ref[...]
        g = lax.dot_general(x, wg_ref[...], NT_DIMS, preferred_element_type=F32)
        u = lax.dot_general(x, wu_ref[...], NT_DIMS, preferred_element_type=F32)
        sig = jax.nn.sigmoid(g)
        silu = g * sig
        hg_ref[...] = (u * (sig * (1.0 + g * (1.0 - sig)))).astype(BF16)
        hu_ref[...] = silu.astype(BF16)
        h_ref[...] = (silu * u).astype(BF16)

    out = jax.ShapeDtypeStruct((s, f), BF16)
    blk = pl.BlockSpec((bm, bn), lambda i, j: (i, j))
    return _call(
        body, name=name, out_shape=(out, out, out),
        grid=(s // bm, f // bn),
        in_specs=[pl.BlockSpec((bm, d), lambda i, j: (i, 0)),
                  pl.BlockSpec((bn, d), lambda i, j: (j, 0)),
                  pl.BlockSpec((bn, d), lambda i, j: (j, 0))],
        out_specs=(blk, blk, blk),
        args=(xb, wg, wu), vmem_mib=56, comm=comm)


def _ffn_bwd_dh(dzb, wd, g, u, *, scale, bm, bn, name, chunks=2, comm=None):
    s, d = dzb.shape
    f = wd.shape[0]
    bm, bn = min(bm, s), min(bn, f)
    assert s % bm == 0 and f % bn == 0

    cr = bm // chunks

    def body(dz_ref, wd_ref, hg_ref, hu_ref, dg_ref, du_ref):
        for r in range(chunks):
            rows = slice(r * cr, (r + 1) * cr)
            dh = lax.dot_general(dz_ref[rows, :], wd_ref[...], NT_DIMS, preferred_element_type=F32) * scale
            dg_ref[rows, :] = (dh * hg_ref[rows, :].astype(F32)).astype(BF16)
            du_ref[rows, :] = (dh * hu_ref[rows, :].astype(F32)).astype(BF16)

    out = jax.ShapeDtypeStruct((s, f), BF16)
    blk = pl.BlockSpec((bm, bn), lambda i, j: (i, j))
    return _call(
        body, name=name, out_shape=(out, out),
        grid=(s // bm, f // bn),
        in_specs=[pl.BlockSpec((bm, d), lambda i, j: (i, 0)),
                  pl.BlockSpec((bn, d), lambda i, j: (j, 0)), blk, blk],
        out_specs=(blk, blk),
        args=(dzb, wd, g, u), vmem_mib=56, comm=comm)


def _full_rows(acc_ref, rows, nj):
    return jnp.concatenate([acc_ref[jj, rows, :] for jj in range(nj)], axis=1)


def _mm_ln(a, b, res, gamma, beta, *, res_scale, mm_scale, bm, bn, name, comm=None):
    s, k = a.shape
    d = b.shape[1]
    bm, bn = min(bm, s), min(bn, d)
    assert s % bm == 0 and d % bn == 0
    nj = d // bn
    ch = min(EPILOGUE_ROWS, bm)

    def body(a_ref, b_ref, r_ref, g_ref, be_ref, y_ref, yb_ref, xh_ref, rs_ref, acc_ref):
        j = pl.program_id(1)
        acc_ref[j] = jnp.dot(a_ref[...], b_ref[...], preferred_element_type=F32)

        @pl.when(j == nj - 1)
        def _():
            def chunk(ci, carry):
                rows = pl.ds(pl.multiple_of(ci * ch, ch), ch)
                z = res_scale * r_ref[rows, :] + mm_scale * _full_rows(acc_ref, rows, nj)
                mu = jnp.mean(z, axis=-1, keepdims=True)
                zc = z - mu
                var = jnp.mean(zc * zc, axis=-1, keepdims=True)
                rstd = lax.rsqrt(var + LN_EPS)
                xh = zc * rstd
                y = xh * g_ref[...] + be_ref[...]
                y_ref[rows, :] = y
                yb_ref[rows, :] = y.astype(BF16)
                xh_ref[rows, :] = xh
                rs_ref[rows, :] = rstd
                return carry

            lax.fori_loop(0, bm // ch, chunk, 0)

    row = pl.BlockSpec((bm, d), lambda i, j: (i, 0))
    vec = pl.BlockSpec((1, d), lambda i, j: (0, 0))
    return _call(
        body, name=name,
        out_shape=(jax.ShapeDtypeStruct((s, d), F32), jax.ShapeDtypeStruct((s, d), BF16),
                   jax.ShapeDtypeStruct((s, d), F32), jax.ShapeDtypeStruct((s, 1), F32)),
        grid=(s // bm, nj),
        in_specs=[pl.BlockSpec((bm, k), lambda i, j: (i, 0)),
                  pl.BlockSpec((k, bn), lambda i, j: (0, j)), row, vec, vec],
        out_specs=(row, row, row, pl.BlockSpec((bm, 1), lambda i, j: (i, 0))),
        scratch_shapes=[pltpu.VMEM((nj, bm, bn), F32)],
        args=(a, b, res, gamma, beta), vmem_mib=58, comm=comm)


def _mm_dx(a, wt, extra, xhat, rstd, gamma, *, extra_scale, bm, bn, name, tb=False, comm=None):
    s, k = a.shape
    d = wt.shape[0] if tb else wt.shape[1]
    bm, bn = min(bm, s), min(bn, d)
    assert s % bm == 0 and d % bn == 0
    nj = d // bn
    ch = min(EPILOGUE_ROWS, bm)
    dims = NT_DIMS if tb else (((1,), (0,)), ((), ()))

    def body(a_ref, w_ref, e_ref, xh_ref, rs_ref, g_ref, dz_ref, dzb_ref, dg_ref, db_ref, acc_ref):
        i = pl.program_id(0)
        j = pl.program_id(1)
        acc_ref[j] = lax.dot_general(a_ref[...], w_ref[...], dims, preferred_element_type=F32)

        @pl.when(j == nj - 1)
        def _():
            def chunk(ci, carry):
                dgp, dbp = carry
                rows = pl.ds(pl.multiple_of(ci * ch, ch), ch)
                dx = extra_scale * e_ref[rows, :] + _full_rows(acc_ref, rows, nj)
                xh = xh_ref[rows, :]
                dxh = dx * g_ref[...]
                m1 = jnp.mean(dxh, axis=-1, keepdims=True)
                m2 = jnp.mean(dxh * xh, axis=-1, keepdims=True)
                dz = rs_ref[rows, :] * (dxh - m1 - xh * m2)
                dz_ref[rows, :] = dz
                dzb_ref[rows, :] = dz.astype(BF16)
                return dgp + jnp.sum(dx * xh, axis=0, keepdims=True), dbp + jnp.sum(dx, axis=0, keepdims=True)

            zero = jnp.zeros((1, d), F32)
            dgp, dbp = lax.fori_loop(0, bm // ch, chunk, (zero, zero))

            @pl.when(i == 0)
            def _():
                dg_ref[...] = dgp
                db_ref[...] = dbp

            @pl.when(i > 0)
            def _():
                dg_ref[...] += dgp
                db_ref[...] += dbp

    row = pl.BlockSpec((bm, d), lambda i, j: (i, 0))
    vec = pl.BlockSpec((1, d), lambda i, j: (0, 0))
    return _call(
        body, name=name,
        out_shape=(jax.ShapeDtypeStruct((s, d), F32), jax.ShapeDtypeStruct((s, d), BF16),
                   jax.ShapeDtypeStruct((1, d), F32), jax.ShapeDtypeStruct((1, d), F32)),
        grid=(s // bm, nj),
        in_specs=[pl.BlockSpec((bm, k), lambda i, j: (i, 0)),
                  pl.BlockSpec((bn, k), lambda i, j: (j, 0)) if tb else pl.BlockSpec((k, bn), lambda i, j: (0, j)),
                  row, row, pl.BlockSpec((bm, 1), lambda i, j: (i, 0)), vec],
        out_specs=(row, row, vec, vec),
        scratch_shapes=[pltpu.VMEM((nj, bm, bn), F32)],
        args=(a, wt, extra, xhat, rstd, gamma), vmem_mib=58, comm=comm)


def _ffn_dx(dg, du, wgt, wut, extra, *, extra_scale, bm, bn, name, comm=None):
    s, f = dg.shape
    d = wgt.shape[1]
    bm, bn = min(bm, s), min(bn, d)
    assert s % bm == 0 and d % bn == 0 and f % (2 * LANES) == 0
    nj, half = d // bn, f // 2

    def body(dg_ref, du_ref, wg_ref, wu_ref, e_ref, o_ref, acc_ref):
        kh, j = pl.program_id(1), pl.program_id(2)
        part = jnp.dot(dg_ref[...], wg_ref[...], preferred_element_type=F32)
        part = part + jnp.dot(du_ref[...], wu_ref[...], preferred_element_type=F32)

        @pl.when(kh == 0)
        def _():
            acc_ref[j] = part

        @pl.when(kh == 1)
        def _():
            o_ref[...] = extra_scale * e_ref[...] + (acc_ref[j] + part)

    rows = pl.BlockSpec((bm, half), lambda i, kh, j: (i, kh))
    cols = pl.BlockSpec((half, bn), lambda i, kh, j: (kh, j))
    blk = pl.BlockSpec((bm, bn), lambda i, kh, j: (i, j * kh))
    return _call(
        body, name=name, out_shape=jax.ShapeDtypeStruct((s, d), F32),
        grid=(s // bm, 2, nj), in_specs=[rows, rows, cols, cols, blk], out_specs=blk,
        scratch_shapes=[pltpu.VMEM((nj, bm, bn), F32)],
        args=(dg, du, wgt, wut, extra), vmem_mib=58, comm=comm)


def _ln_bwd(dx, xhat, rstd, gamma, *, bm, name):
    s, d = dx.shape
    bm = min(bm, s)
    assert s % bm == 0
    ch = min(EPILOGUE_ROWS, bm)

    def body(dx_ref, xh_ref, rs_ref, g_ref, dz_ref, dzb_ref, dg_ref, db_ref):
        def chunk(ci, carry):
            dgp, dbp = carry
            rows = pl.ds(pl.multiple_of(ci * ch, ch), ch)
            dxv = dx_ref[rows, :]
            xh = xh_ref[rows, :]
            dxh = dxv * g_ref[...]
            m1 = jnp.mean(dxh, axis=-1, keepdims=True)
            m2 = jnp.mean(dxh * xh, axis=-1, keepdims=True)
            dz = rs_ref[rows, :] * (dxh - m1 - xh * m2)
            dz_ref[rows, :] = dz
            dzb_ref[rows, :] = dz.astype(BF16)
            return dgp + jnp.sum(dxv * xh, axis=0, keepdims=True), dbp + jnp.sum(dxv, axis=0, keepdims=True)

        zero = jnp.zeros((1, d), F32)
        dgp, dbp = lax.fori_loop(0, bm // ch, chunk, (zero, zero))
        i = pl.program_id(0)

        @pl.when(i == 0)
        def _():
            dg_ref[...] = dgp
            db_ref[...] = dbp

        @pl.when(i > 0)
        def _():
            dg_ref[...] += dgp
            db_ref[...] += dbp

    row = pl.BlockSpec((bm, d), lambda i: (i, 0))
    vec = pl.BlockSpec((1, d), lambda i: (0, 0))
    return _call(
        body, name=name,
        out_shape=(jax.ShapeDtypeStruct((s, d), F32), jax.ShapeDtypeStruct((s, d), BF16),
                   jax.ShapeDtypeStruct((1, d), F32), jax.ShapeDtypeStruct((1, d), F32)),
        grid=(s // bm,), in_specs=[row, row, pl.BlockSpec((bm, 1), lambda i: (i, 0)), vec],
        out_specs=(row, row, vec, vec), args=(dx, xhat, rstd, gamma), vmem_mib=48)


def _to_bf16(x, *, bm, name, comm=None):
    s, d = x.shape
    bm = min(bm, s)
    assert s % bm == 0

    def body(x_ref, o_ref):
        o_ref[...] = x_ref[...].astype(BF16)

    row = pl.BlockSpec((bm, d), lambda i: (i, 0))
    return _call(body, name=name, out_shape=jax.ShapeDtypeStruct((s, d), BF16), grid=(s // bm,),
                 in_specs=[row], out_specs=row, args=(x,), vmem_mib=48, comm=comm)


def _ple_loss(x3, x3b, p, wpg, wpp, target, *, bm, bn, name):
    s, d = x3.shape
    dp = p.shape[1]
    bm, bn = min(bm, s), min(bn, d)
    assert s % bm == 0 and d % bn == 0
    inv_d = 1.0 / d
    chunks = 4 if bm % 64 == 0 else 1
    cr = bm // chunks

    def body(x_ref, xb_ref, p_ref, wg_ref, wp_ref, t_ref, l_ref, dy_ref, dg_ref, dp_ref):
        first = (pl.program_id(0) == 0) & (pl.program_id(1) == 0)

        @pl.when(first)
        def _():
            l_ref[...] = jnp.zeros_like(l_ref)

        part = 0.0
        for r in range(chunks):
            rows = slice(r * cr, (r + 1) * cr)
            gp = jnp.dot(xb_ref[rows, :], wg_ref[...], preferred_element_type=F32)
            pp = lax.dot_general(p_ref[rows, :].astype(BF16), wp_ref[...], NT_DIMS, preferred_element_type=F32)
            sig = jax.nn.sigmoid(gp)
            err = x_ref[rows, :] + sig * pp - t_ref[rows, :]
            part = part + jnp.sum(err * err)
            dy = err * inv_d
            dy_ref[rows, :] = dy
            dg_ref[rows, :] = (dy * pp * sig * (1.0 - sig)).astype(BF16)
            dp_ref[rows, :] = (dy * sig).astype(BF16)
        l_ref[...] += part

    blk = pl.BlockSpec((bm, bn), lambda i, j: (i, j))
    return pl.pallas_call(
        body, name=name,
        out_shape=(jax.ShapeDtypeStruct((8, LANES), F32), jax.ShapeDtypeStruct((s, d), F32),
                   jax.ShapeDtypeStruct((s, d), BF16), jax.ShapeDtypeStruct((s, d), BF16)),
        grid=(s // bm, d // bn),
        in_specs=[blk, pl.BlockSpec((bm, d), lambda i, j: (i, 0)), pl.BlockSpec((bm, dp), lambda i, j: (i, 0)),
                  pl.BlockSpec((d, bn), lambda i, j: (0, j)), pl.BlockSpec((bn, dp), lambda i, j: (j, 0)), blk],
        out_specs=(pl.BlockSpec((8, LANES), lambda i, j: (0, 0)), blk, blk, blk),
        compiler_params=_cp(("arbitrary", "arbitrary"), 56),
    )(x3, x3b, p, wpg, wpp, target)


def _rope_tables(positions):
    half = ROT_DIMS // 2
    lane = jnp.arange(HEAD_DIM)
    inv_freq = jnp.power(jnp.float32(ROPE_THETA), -(lane % half).astype(F32) * (2.0 / ROT_DIMS))
    ang = positions.astype(F32)[:, None] * inv_freq
    cos, sin = jnp.cos(ang), jnp.sin(ang)
    cf = jnp.where(lane < ROT_DIMS, cos, 1.0)
    sa = jnp.where(lane < half, -sin, 0.0)
    sb = jnp.where((lane >= half) & (lane < ROT_DIMS), sin, 0.0)
    return cf, sa, sb


def _rotary(t, tabs, *, n_cols, inverse, out_dtype, bs, name):
    s = t.shape[0]
    bs = min(bs, s)
    half = ROT_DIMS // 2
    heads = N_KV_HEADS
    assert n_cols % heads == 0

    def body(t_ref, cf_ref, sa_ref, sb_ref, o_ref):
        cf, sa, sb = cf_ref[...], sa_ref[...], sb_ref[...]
        for hd in range(heads):
            lanes = slice(hd * HEAD_DIM, (hd + 1) * HEAD_DIM)
            v = t_ref[:, lanes]
            if inverse:
                o = v * cf + pltpu.roll(v * sa, half, 1) + pltpu.roll(v * sb, HEAD_DIM - half, 1)
            else:
                o = v * cf + pltpu.roll(v, HEAD_DIM - half, 1) * sa + pltpu.roll(v, half, 1) * sb
            o_ref[:, lanes] = o.astype(out_dtype)

    blk = pl.BlockSpec((bs, heads * HEAD_DIM), lambda i, j: (i, j))
    tab = pl.BlockSpec((bs, HEAD_DIM), lambda i, j: (i, 0))
    return pl.pallas_call(
        body, name=name, out_shape=jax.ShapeDtypeStruct((s, n_cols * HEAD_DIM), out_dtype),
        grid=(s // bs, n_cols // heads), in_specs=[blk, tab, tab, tab], out_specs=blk,
        compiler_params=_cp(("parallel", "arbitrary"), 32),
    )(t, *tabs)


def _attn_blocks():
    out = []
    for g, dil in enumerate(DILATIONS):
        sup = SPAN * dil
        for j in range(ATTN_TILE // sup):
            for r in range(dil):
                out.append((g, j * sup + r, dil, (j - 1) * sup + r if j > 0 else None, ATTN_TILE - sup + r))
    return out


def _rows(ref, start, dil, lead=None):
    idx = pl.ds(start, SPAN, stride=dil) if dil > 1 else pl.ds(start, SPAN)
    return ref[idx, :] if lead is None else ref[lead, idx, :]


def _band_masks(n):
    qi = lax.broadcasted_iota(jnp.int32, (SPAN, 2 * SPAN), 0)
    ki = lax.broadcasted_iota(jnp.int32, (SPAN, 2 * SPAN), 1)
    band = (ki >= qi) & (ki <= qi + SPAN)
    return band, band & ((ki >= SPAN) | (n > 0))


def _attn_fwd(qkr, proj, *, name):
    s = qkr.shape[0]
    t = ATTN_TILE
    assert s % t == 0
    nt = s // t
    scale = HEAD_DIM ** -0.5
    kcol, vcol = N_PATTERNS * N_KV_HEADS, (N_PATTERNS + 1) * N_KV_HEADS
    blocks = _attn_blocks()

    def body(q0, q1, q2, kc_ref, kp_ref, vc_ref, vp_ref, o_ref, l_ref, og, lg):
        n = pl.program_id(1)
        band, band_first = _band_masks(n)
        q_refs = (q0, q1, q2)
        for g, start, dil, prev_in_tile, prev_start in blocks:
            q = _rows(q_refs[g], start, dil).astype(BF16)
            if prev_in_tile is not None:
                kp, vp, mask = _rows(kc_ref, prev_in_tile, dil), _rows(vc_ref, prev_in_tile, dil), band
            else:
                kp, vp, mask = _rows(kp_ref, prev_start, dil), _rows(vp_ref, prev_start, dil), band_first
            kk = jnp.concatenate([kp, _rows(kc_ref, start, dil)], axis=0).astype(BF16)
            vv = jnp.concatenate([vp, _rows(vc_ref, start, dil)], axis=0).astype(BF16)
            sc = lax.dot_general(q, kk, (((1,), (1,)), ((), ())), preferred_element_type=F32) * scale
            sc = jnp.where(mask, sc, -1e30)
            m = jnp.max(sc, axis=-1, keepdims=True)
            e = jnp.exp(sc - m)
            den = jnp.sum(e, axis=-1, keepdims=True)
            o = jnp.dot(e.astype(BF16), vv, preferred_element_type=F32) / den
            idx = pl.ds(start, SPAN, stride=dil) if dil > 1 else pl.ds(start, SPAN)
            og[g, idx, :] = o
            lg[g, idx, :] = jnp.broadcast_to(m + jnp.log(den), (SPAN, HEAD_DIM))
        l0, l1, l2 = lg[0], lg[1], lg[2]
        m = jnp.maximum(jnp.maximum(l0, l1), l2)
        w0, w1, w2 = jnp.exp(l0 - m), jnp.exp(l1 - m), jnp.exp(l2 - m)
        den = w0 + w1 + w2
        o_ref[...] = (w0 * og[0] + w1 * og[1] + w2 * og[2]) / den
        l_ref[...] = m + jnp.log(den)

    def col(c, prev=False):
        if prev:
            return pl.BlockSpec((t, HEAD_DIM), lambda h, n: (jnp.maximum(n - 1, 0), c + h))
        return pl.BlockSpec((t, HEAD_DIM), lambda h, n: (n, c + h))

    out = jax.ShapeDtypeStruct((s, N_KV_HEADS * HEAD_DIM), F32)
    return pl.pallas_call(
        body, name=name, out_shape=(out, out),
        grid=(N_KV_HEADS, nt),
        in_specs=[col(0), col(N_KV_HEADS), col(2 * N_KV_HEADS), col(kcol), col(kcol, True), col(vcol), col(vcol, True)],
        out_specs=(col(0), col(0)),
        scratch_shapes=[pltpu.VMEM((N_PATTERNS, t, HEAD_DIM), F32), pltpu.VMEM((N_PATTERNS, t, HEAD_DIM), F32)],
        compiler_params=_cp(("parallel", "arbitrary"), 48),
    )(qkr, qkr, qkr, qkr, qkr, proj, proj)


def _attn_bwd(qkr, proj, attn, lse, dcat, *, name, comm=None):
    s = qkr.shape[0]
    t = ATTN_TILE
    nt = s // t
    scale = HEAD_DIM ** -0.5
    kcol, vcol = N_PATTERNS * N_KV_HEADS, (N_PATTERNS + 1) * N_KV_HEADS
    blocks = _attn_blocks()

    def body(q0, q1, q2, kc_ref, kp_ref, vc_ref, vp_ref, o_ref, l_ref, do_ref,
             dq0, dq1, dq2, dk_ref, dv_ref, ck, cv, tkc, tvc, tkp, tvp):
        n = pl.program_id(1)
        for ref in (tkc, tvc, tkp, tvp):
            ref[...] = jnp.zeros_like(ref)

        @pl.when(n < nt)
        def _():
            band, band_first = _band_masks(n)
            q_refs, dq_refs = (q0, q1, q2), (dq0, dq1, dq2)
            for g, start, dil, prev_in_tile, prev_start in blocks:
                idx = pl.ds(start, SPAN, stride=dil) if dil > 1 else pl.ds(start, SPAN)
                q = q_refs[g][idx, :].astype(BF16)
                if prev_in_tile is not None:
                    kp, vp, mask = _rows(kc_ref, prev_in_tile, dil), _rows(vc_ref, prev_in_tile, dil), band
                else:
                    kp, vp, mask = _rows(kp_ref, prev_start, dil), _rows(vp_ref, prev_start, dil), band_first
                kk = jnp.concatenate([kp, kc_ref[idx, :]], axis=0).astype(BF16)
                vv = jnp.concatenate([vp, vc_ref[idx, :]], axis=0).astype(BF16)
                do = do_ref[idx, :]
                dsum = jnp.sum(do * o_ref[idx, :], axis=-1, keepdims=True)
                lrow = l_ref[idx, :][:, :1]
                dob = do.astype(BF16)
                sc = lax.dot_general(q, kk, (((1,), (1,)), ((), ())), preferred_element_type=F32) * scale
                p = jnp.where(mask, jnp.exp(sc - lrow), 0.0)
                dp = lax.dot_general(dob, vv, (((1,), (1,)), ((), ())), preferred_element_type=F32)
                ds = (p * (dp - dsum) * scale).astype(BF16)
                pb = p.astype(BF16)
                dq_refs[g][idx, :] = jnp.dot(ds, kk, preferred_element_type=F32)
                dkk = lax.dot_general(ds, q, (((0,), (0,)), ((), ())), preferred_element_type=F32)
                dvv = lax.dot_general(pb, dob, (((0,), (0,)), ((), ())), preferred_element_type=F32)
                tkc[idx, :] += dkk[SPAN:]
                tvc[idx, :] += dvv[SPAN:]
                if prev_in_tile is not None:
                    pidx = pl.ds(prev_in_tile, SPAN, stride=dil) if dil > 1 else pl.ds(prev_in_tile, SPAN)
                    tkc[pidx, :] += dkk[:SPAN]
                    tvc[pidx, :] += dvv[:SPAN]
                else:
                    pidx = pl.ds(prev_start, SPAN, stride=dil) if dil > 1 else pl.ds(prev_start, SPAN)
                    tkp[pidx, :] += dkk[:SPAN]
                    tvp[pidx, :] += dvv[:SPAN]

        @pl.when(n > 0)
        def _():
            dk_ref[...] = ck[...] + tkp[...]
            dv_ref[...] = (cv[...] + tvp[...]).astype(BF16)

        ck[...] = tkc[...]
        cv[...] = tvc[...]

    def col(c, prev=False):
        if prev:
            return pl.BlockSpec((t, HEAD_DIM), lambda h, n: (jnp.maximum(jnp.minimum(n, nt - 1) - 1, 0), c + h))
        return pl.BlockSpec((t, HEAD_DIM), lambda h, n: (jnp.minimum(n, nt - 1), c + h))

    kv_out = pl.BlockSpec((t, HEAD_DIM), lambda h, n: (jnp.maximum(n - 1, 0), h))
    tile = pltpu.VMEM((t, HEAD_DIM), F32)
    per_head = jax.ShapeDtypeStruct((s, N_KV_HEADS * HEAD_DIM), F32)
    return _call(
        body, name=name,
        out_shape=(per_head, per_head, per_head, per_head, jax.ShapeDtypeStruct((s, N_KV_HEADS * HEAD_DIM), BF16)),
        grid=(N_KV_HEADS, nt + 1),
        in_specs=[col(0), col(N_KV_HEADS), col(2 * N_KV_HEADS), col(kcol), col(kcol, True), col(vcol), col(vcol, True),
                  col(0), col(0), col(0)],
        out_specs=(col(0), col(0), col(0), kv_out, kv_out),
        scratch_shapes=[tile] * 6,
        args=(qkr, qkr, qkr, qkr, qkr, proj, proj, attn, lse, dcat), vmem_mib=48, comm=comm)


GELU_C0 = 0.7978845608028654
GELU_C1 = 0.044715


def _softplus_neg(lam):
    y = jnp.exp(-jnp.abs(lam))
    w = 1.0 + y
    log1p = jnp.where(w == 1.0, y, jnp.log(w) * (y / jnp.where(w == 1.0, 1.0, w - 1.0)))
    return jnp.maximum(-lam, 0.0) + log1p


def _down(cur, prev, k, row):
    if k == 0:
        return cur
    return jnp.where(row < k, pltpu.roll(prev, k, 0), pltpu.roll(cur, k, 0))


def _up(cur, nxt, k, row, tt):
    if k == 0:
        return cur
    return jnp.where(row >= tt - k, pltpu.roll(nxt, tt - k, 0), pltpu.roll(cur, tt - k, 0))


def _lru_gates(x, xp, cw, cb, wr, br, wi, bi, lam, row):
    shifts = [_down(x, xp, k, row) for k in range(CONV_WIDTH)]
    xc = cb
    for j in range(CONV_WIDTH):
        xc = xc + cw[j:j + 1, :] * shifts[CONV_WIDTH - 1 - j]
    xcb = xc.astype(BF16)
    r = jax.nn.sigmoid(jnp.dot(xcb, wr, preferred_element_type=F32) + br)
    i = jax.nn.sigmoid(jnp.dot(xcb, wi, preferred_element_type=F32) + bi)
    c = -LRU_C * _softplus_neg(lam)
    la = c * r
    a = jnp.exp(la)
    mult = jnp.sqrt(jnp.tanh(-la) * (a * a + 1.0))
    return shifts, xc, xcb, r, i, c, a, mult


def _lru_fwd(proj, cw, cb, wr, br, wi, bi, lam, *, tt, name):
    s = proj.shape[0]
    nblk = wr.shape[0]
    c = nblk * LANES
    tt = min(tt, s)
    xcol0 = (N_PATTERNS + 2) * N_KV_HEADS
    ycol0 = xcol0 + nblk

    def body(x_ref, y_ref, cw_ref, cb_ref, wr_ref, br_ref, wi_ref, bi_ref, lam_ref, rec_ref, h_ref, xprev, hc):
        n = pl.program_id(1)

        @pl.when(n == 0)
        def _():
            xprev[...] = jnp.zeros_like(xprev)
            hc[...] = jnp.zeros_like(hc)

        row = lax.broadcasted_iota(jnp.int32, (tt, LANES), 0)
        x = x_ref[...]
        _, xc, _, _, i, _, a, mult = _lru_gates(
            x, xprev[...], cw_ref[...], cb_ref[...], wr_ref[0].astype(BF16), br_ref[...],
            wi_ref[0].astype(BF16), bi_ref[...], lam_ref[...], row)
        av, bv = a, mult * (i * xc)
        k = 1
        while k < tt:
            bs = jnp.where(row < k, 0.0, pltpu.roll(bv, k, 0))
            as_ = jnp.where(row < k, 1.0, pltpu.roll(av, k, 0))
            bv = bv + av * bs
            av = av * as_
            k *= 2
        h = bv + av * hc[0:1, :]
        hc[...] = jnp.broadcast_to(h[tt - 1:tt, :], hc.shape)
        h_ref[...] = h
        y = y_ref[...]
        gel = 0.5 * y * (1.0 + jnp.tanh(GELU_C0 * (y + GELU_C1 * y * y * y)))
        rec_ref[...] = (h * gel).astype(BF16)
        xprev[...] = x

    vec = pl.BlockSpec((1, LANES), lambda b, n: (0, b))
    wblk = pl.BlockSpec((1, LANES, LANES), lambda b, n: (b, 0, 0))
    out = pl.BlockSpec((tt, LANES), lambda b, n: (n, b))
    return pl.pallas_call(
        body, name=name,
        out_shape=(jax.ShapeDtypeStruct((s, c), BF16), jax.ShapeDtypeStruct((s, c), F32)),
        grid=(nblk, s // tt),
        in_specs=[pl.BlockSpec((tt, LANES), lambda b, n: (n, xcol0 + b)),
                  pl.BlockSpec((tt, LANES), lambda b, n: (n, ycol0 + b)),
                  pl.BlockSpec((CONV_WIDTH, LANES), lambda b, n: (0, b)), vec, wblk, vec, wblk, vec, vec],
        out_specs=(out, out),
        scratch_shapes=[pltpu.VMEM((tt, LANES), F32), pltpu.VMEM((8, LANES), F32)],
        compiler_params=_cp(("parallel", "arbitrary"), 32),
    )(proj, proj, cw, cb, wr, br, wi, bi, lam)


def _lru_bwd(proj, hseq, dcat, cw, cb, wr, br, wi, bi, lam, *, tt, name, comm=None):
    s = proj.shape[0]
    nblk = wr.shape[0]
    c = nblk * LANES
    tt = min(tt, s)
    nt = s // tt
    xcol0 = (N_PATTERNS + 2) * N_KV_HEADS
    ycol0 = xcol0 + nblk
    rcol0 = N_KV_HEADS

    def body(x_ref, xp_ref, y_ref, h_ref, hp_ref, dr_ref, cw_ref, cb_ref, wr_ref, br_ref, wi_ref, bi_ref, lam_ref,
             dx_ref, dy_ref, dcw_ref, dcb_ref, dwr_ref, dbr_ref, dwi_ref, dbi_ref, dlam_ref, dxc_next, gcar, acar):
        n = pl.program_id(1)
        rt = nt - 1 - n

        @pl.when(n == 0)
        def _():
            for ref in (dxc_next, gcar, acar, dcw_ref, dcb_ref, dwr_ref, dbr_ref, dwi_ref, dbi_ref, dlam_ref):
                ref[...] = jnp.zeros_like(ref)

        row = lax.broadcasted_iota(jnp.int32, (tt, LANES), 0)
        x = x_ref[...]
        xp = jnp.where(rt > 0, xp_ref[...], 0.0)
        cwv = cw_ref[...]
        wrb, wib = wr_ref[0].astype(BF16), wi_ref[0].astype(BF16)
        lam_v = lam_ref[...]
        shifts, xc, xcb, r, i, cc, a, mult = _lru_gates(x, xp, cwv, cb_ref[...], wrb, br_ref[...], wib, bi_ref[...],
                                                        lam_v, row)
        h = h_ref[...]
        hp_last = jnp.where(rt > 0, hp_ref[7:8, :], 0.0)
        hprev = jnp.where(row < 1, hp_last, pltpu.roll(h, 1, 0))
        y = y_ref[...]
        y2 = y * y
        th = jnp.tanh(GELU_C0 * (y + GELU_C1 * y2 * y))
        gel = 0.5 * y * (1.0 + th)
        dgel = 0.5 * (1.0 + th) + 0.5 * y * (1.0 - th * th) * GELU_C0 * (1.0 + 3.0 * GELU_C1 * y2)
        drec = dr_ref[...]
        dy_ref[...] = (drec * h * dgel).astype(BF16)
        av = jnp.where(row >= tt - 1, acar[0:1, :], pltpu.roll(a, tt - 1, 0))
        bv = drec * gel
        k = 1
        while k < tt:
            bs = jnp.where(row >= tt - k, 0.0, pltpu.roll(bv, tt - k, 0))
            as_ = jnp.where(row >= tt - k, 1.0, pltpu.roll(av, tt - k, 0))
            bv = bv + av * bs
            av = av * as_
            k *= 2
        g = bv + av * gcar[0:1, :]
        gcar[...] = jnp.broadcast_to(g[0:1, :], gcar.shape)
        acar[...] = jnp.broadcast_to(a[0:1, :], acar.shape)
        da = g * hprev
        d_ixc = g * mult
        dmult = g * (i * xc)
        di = d_ixc * xc
        dxc = d_ixc * i
        a2 = a * a
        dla = da * a - dmult * (a2 / mult)
        dr = dla * cc
        dsp = jnp.sum(dla * r, axis=0, keepdims=True) * (-LRU_C)
        dlam_ref[...] += dsp * (-jax.nn.sigmoid(-lam_v))
        dzr = dr * r * (1.0 - r)
        dzi = di * i * (1.0 - i)
        dbr_ref[...] += jnp.sum(dzr, axis=0, keepdims=True)
        dbi_ref[...] += jnp.sum(dzi, axis=0, keepdims=True)
        dzrb, dzib = dzr.astype(BF16), dzi.astype(BF16)
        tn = (((0,), (0,)), ((), ()))
        ntd = (((1,), (1,)), ((), ()))
        dwr_ref[0] += lax.dot_general(xcb, dzrb, tn, preferred_element_type=F32)
        dwi_ref[0] += lax.dot_general(xcb, dzib, tn, preferred_element_type=F32)
        dxc = (dxc + lax.dot_general(dzrb, wrb, ntd, preferred_element_type=F32)
               + lax.dot_general(dzib, wib, ntd, preferred_element_type=F32))
        dcb_ref[...] += jnp.sum(dxc, axis=0, keepdims=True)
        dcw_ref[...] += jnp.concatenate(
            [jnp.sum(dxc * shifts[CONV_WIDTH - 1 - j], axis=0, keepdims=True) for j in range(CONV_WIDTH)], axis=0)
        nxt = dxc_next[...]
        dx = cwv[0:1, :] * _up(dxc, nxt, CONV_WIDTH - 1, row, tt)
        for j in range(1, CONV_WIDTH):
            dx = dx + cwv[j:j + 1, :] * _up(dxc, nxt, CONV_WIDTH - 1 - j, row, tt)
        dx_ref[...] = dx.astype(BF16)
        dxc_next[...] = dxc

    def tile(col0, prev=False):
        if prev:
            return pl.BlockSpec((tt, LANES), lambda b, n: (jnp.maximum(nt - 2 - n, 0), col0 + b))
        return pl.BlockSpec((tt, LANES), lambda b, n: (nt - 1 - n, col0 + b))

    vec = pl.BlockSpec((1, LANES), lambda b, n: (0, b))
    wblk = pl.BlockSpec((1, LANES, LANES), lambda b, n: (b, 0, 0))
    cwblk = pl.BlockSpec((CONV_WIDTH, LANES), lambda b, n: (0, b))
    hp8 = pl.BlockSpec((8, LANES), lambda b, n: (jnp.maximum((nt - 1 - n) * (tt // 8) - 1, 0), b))
    vshape = jax.ShapeDtypeStruct((1, c), F32)
    wshape = jax.ShapeDtypeStruct((nblk, LANES, LANES), F32)
    return _call(
        body, name=name,
        out_shape=(jax.ShapeDtypeStruct((s, c), BF16), jax.ShapeDtypeStruct((s, c), BF16),
                   jax.ShapeDtypeStruct((CONV_WIDTH, c), F32), vshape, wshape, vshape, wshape, vshape, vshape),
        grid=(nblk, nt),
        in_specs=[tile(xcol0), tile(xcol0, True), tile(ycol0), tile(0), hp8, tile(rcol0),
                  cwblk, vec, wblk, vec, wblk, vec, vec],
        out_specs=(tile(0), tile(0), cwblk, vec, wblk, vec, wblk, vec, vec),
        scratch_shapes=[pltpu.VMEM((tt, LANES), F32), pltpu.VMEM((8, LANES), F32), pltpu.VMEM((8, LANES), F32)],
        args=(proj, proj, proj, hseq, hseq, dcat, cw, cb, wr, br, wi, bi, lam), vmem_mib=32, comm=comm)


ROW_BLOCKS = (512, 256, 176, 128, 64, 32, 16, 8)


def _adamw(w, m, v, gparts, *, name):
    r, c = w.shape
    npart = gparts.shape[0]
    br = _pick(r, ROW_BLOCKS)
    c1 = 1.0 - ADAM_B1 ** ADAM_STEP
    c2 = 1.0 - ADAM_B2 ** ADAM_STEP

    def body(w_ref, m_ref, v_ref, g_ref, go_ref, d_ref, mo_ref, vo_ref):
        g = g_ref[0].astype(F32)
        for q in range(1, npart):
            g = g + g_ref[q].astype(F32)
        mn = ADAM_B1 * m_ref[...] + (1.0 - ADAM_B1) * g
        vn = ADAM_B2 * v_ref[...] + (1.0 - ADAM_B2) * (g * g)
        go_ref[...] = g
        mo_ref[...] = mn
        vo_ref[...] = vn
        d_ref[...] = -ADAM_LR * ((mn / c1) / (jnp.sqrt(vn / c2) + ADAM_EPS) + ADAM_WD * w_ref[...])

    blk = pl.BlockSpec((br, c), lambda i: (i, 0))
    out = jax.ShapeDtypeStruct((r, c), F32)
    return pl.pallas_call(
        body, name=name, out_shape=(out, out, out, out), grid=(r // br,),
        in_specs=[blk, blk, blk, pl.BlockSpec((npart, br, c), lambda i: (0, i, 0))],
        out_specs=(blk, blk, blk, blk),
        compiler_params=_cp(("parallel",), 48),
    )(w, m, v, gparts)


def _sum_parts(parts, *, name):
    npart, r, c = parts.shape
    br = next((b for b in range(min(r, 2048) // 8 * 8, 0, -8) if r % b == 0), r)

    def body(p_ref, o_ref):
        acc = p_ref[0]
        for q in range(1, npart):
            acc = acc + p_ref[q]
        o_ref[...] = acc

    return pl.pallas_call(
        body, name=name, out_shape=jax.ShapeDtypeStruct((r, c), F32), grid=(r // br,),
        in_specs=[pl.BlockSpec((npart, br, c), lambda i: (0, i, 0))],
        out_specs=pl.BlockSpec((br, c), lambda i: (i, 0)),
        compiler_params=_cp(("parallel",), 48),
    )(parts)


HBM = pl.BlockSpec(memory_space=pltpu.HBM)


def _mesh_pos():
    return lax.axis_index("x"), lax.axis_index("y"), lax.axis_index("c")


def _gather_comm(shards):
    na = len(shards)

    def parts(x_refs, out_refs, sems):
        send_sems, recv_sems, local_sems = sems
        x, y, c = _mesh_pos()
        me, sibling = (x, y, c), (x, y, 1 - c)
        chips = [(1 - x, y), (x, 1 - y), (1 - x, 1 - y)]

        def copy(a, k, block, to, src=None):
            px, py, pc = block
            dst = out_refs[a].at[4 * px + 2 * py + pc]
            return pltpu.make_async_remote_copy(
                src_ref=dst if src is None else src, dst_ref=dst,
                send_sem=send_sems.at[a, k], recv_sem=recv_sems.at[a, k],
                device_id=to, device_id_type=MESH)

        def mine(a):
            return pltpu.make_async_copy(x_refs[a], out_refs[a].at[4 * x + 2 * y + c], local_sems.at[a])

        def first(a):
            return [copy(a, 0, me, sibling, src=x_refs[a])] + [
                copy(a, 1 + j, me, (*chip, c), src=x_refs[a]) for j, chip in enumerate(chips)]

        def passed(a, j):
            return copy(a, 4 + j, (*chips[j], c), sibling)

        return me, sibling, chips, c, copy, mine, first, passed

    def start(x_refs, out_refs, sems):
        *_, mine, first, _ = parts(x_refs, out_refs, sems)
        for a in range(na):
            mine(a).start()
            for cp in first(a):
                cp.start()

    def mid(x_refs, out_refs, sems):
        me, _, chips, c, copy, _, _, passed = parts(x_refs, out_refs, sems)
        for j, chip in enumerate(chips):
            for a in range(na):
                copy(a, 1 + j, (*chip, c), me).wait_recv()
                passed(a, j).start()

    def end(x_refs, out_refs, sems):
        me, sibling, chips, c, copy, mine, first, passed = parts(x_refs, out_refs, sems)
        for a in range(na):
            copy(a, 0, sibling, me).wait_recv()
            for j, chip in enumerate(chips):
                copy(a, 4 + j, (*chip, 1 - c), me).wait_recv()
        for a in range(na):
            for cp in first(a) + [passed(a, j) for j in range(3)]:
                cp.wait_send()
            mine(a).wait()

    return _Comm(
        shards, [jax.ShapeDtypeStruct((N_DEV,) + a.shape, a.dtype) for a in shards],
        [pltpu.SemaphoreType.DMA((na, 7)), pltpu.SemaphoreType.DMA((na, 7)), pltpu.SemaphoreType.DMA((na,))],
        start, end, mid)


def _scatter_comm(g8s):
    na = len(g8s)

    def parts(g_refs, buf_refs, sems):
        send_sems, recv_sems, local_sems = sems
        x, y, c = _mesh_pos()
        me_idx = 4 * x + 2 * y + c

        def copy(a, k, slot):
            peer, peer_idx = _scatter_peer(k, x, y, c)
            return pltpu.make_async_remote_copy(
                src_ref=g_refs[a].at[peer_idx], dst_ref=buf_refs[a].at[me_idx if slot is None else slot],
                send_sem=send_sems.at[a, k - 1], recv_sem=recv_sems.at[a, k - 1],
                device_id=peer, device_id_type=MESH)

        def mine(a):
            return pltpu.make_async_copy(g_refs[a].at[me_idx], buf_refs[a].at[me_idx], local_sems.at[a])

        return x, y, c, copy, mine

    def start(g_refs, buf_refs, sems):
        *_, copy, mine = parts(g_refs, buf_refs, sems)
        for a in range(na):
            mine(a).start()
            for k in range(1, N_DEV):
                copy(a, k, None).start()

    def end(g_refs, buf_refs, sems):
        x, y, c, copy, mine = parts(g_refs, buf_refs, sems)
        for a in range(na):
            for k in range(1, N_DEV):
                copy(a, k, _scatter_peer(k, x, y, c)[1]).wait_recv()
        for a in range(na):
            for k in range(1, N_DEV):
                copy(a, k, None).wait_send()
            mine(a).wait()

    return _Comm(
        g8s, [jax.ShapeDtypeStruct(g.shape, g.dtype) for g in g8s],
        [pltpu.SemaphoreType.DMA((na, N_DEV - 1)), pltpu.SemaphoreType.DMA((na, N_DEV - 1)),
         pltpu.SemaphoreType.DMA((na,))],
        start, end)


def _scatter_peer(k, x, y, c):
    px, py, pc = (1 - x if k & 4 else x, 1 - y if k & 2 else y, 1 - c if k & 1 else c)
    return (px, py, pc), 4 * px + 2 * py + pc


def _scatter_start_comm(g8):
    land = lax.empty(g8.shape, g8.dtype)

    def start(refs, outs, sems):
        g_ref, land_ref = refs
        send_sems, recv_sems = outs[:2]
        x, y, c = _mesh_pos()
        me_idx = 4 * x + 2 * y + c
        pltpu.make_async_copy(g_ref.at[me_idx], land_ref.at[me_idx], sems[0]).start()
        for k in range(1, N_DEV):
            peer, peer_idx = _scatter_peer(k, x, y, c)
            pltpu.make_async_remote_copy(
                src_ref=g_ref.at[peer_idx], dst_ref=land_ref.at[me_idx],
                send_sem=send_sems.at[k - 1], recv_sem=recv_sems.at[k - 1],
                device_id=peer, device_id_type=MESH).start()

    def end(refs, outs, sems):
        g_ref, land_ref = refs
        x, y, c = _mesh_pos()
        me_idx = 4 * x + 2 * y + c
        pltpu.make_async_copy(g_ref.at[me_idx], land_ref.at[me_idx], sems[0]).wait()

    sem_shape = pltpu.SemaphoreType.DMA((N_DEV - 1,))
    return _Comm([g8, land], [sem_shape, sem_shape, pltpu.HBM(g8.shape, g8.dtype), pltpu.HBM(g8.shape, g8.dtype)],
                 [pltpu.SemaphoreType.DMA(())], start, end, split=True)


def _scatter_wait(started, after, *, name):
    send_sems, recv_sems, g8, land = started

    def body(g_ref, land_ref, send_ref, recv_ref, after_ref, g_dead, got_ref):
        x, y, c = _mesh_pos()
        me_idx = 4 * x + 2 * y + c
        for k in range(1, N_DEV):
            peer, peer_idx = _scatter_peer(k, x, y, c)
            pltpu.make_async_remote_copy(
                src_ref=g_ref.at[peer_idx], dst_ref=land_ref.at[peer_idx],
                send_sem=send_ref.at[k - 1], recv_sem=recv_ref.at[k - 1],
                device_id=peer, device_id_type=MESH).wait()

    sem = pl.BlockSpec(memory_space=pltpu.SEMAPHORE)
    return pl.pallas_call(
        body, name=name, out_shape=(pltpu.HBM(g8.shape, g8.dtype), pltpu.HBM(land.shape, land.dtype)),
        in_specs=[HBM, HBM, sem, sem, pl.BlockSpec(memory_space=pl.ANY)], out_specs=(HBM, HBM),
        input_output_aliases={0: 0, 1: 1},
        compiler_params=pltpu.CompilerParams(has_side_effects=pltpu.SideEffectType.DATAFLOW_SIDE_EFFECTING),
    )(g8, land, send_sems, recv_sems, after)[1]


BIG_WEIGHTS = ("ffn1_w_gate", "ffn1_w_up", "ffn1_w_down", "w_in", "w_out",
               "ffn2_w_gate", "ffn2_w_up", "ffn2_w_down", "w_ple_proj", "w_ple_gate")
COLUMN_SHARDED = ("ffn1_w_gate", "ffn1_w_up", "w_in", "ffn2_w_gate", "ffn2_w_up", "w_ple_proj", "conv_w")
SMALL_WEIGHTS = ("ln1_g", "ln1_b", "conv_b", "w_rgate", "b_rgate", "w_igate", "b_igate", "lru_lambda",
                 "ln2_g", "ln2_b", "ln3_g", "ln3_b")
SMALL_GRADS = SMALL_WEIGHTS + ("conv_w",)


class _Exchange:
    def __init__(self, full):
        self.full = dict(full)
        self.grads = {}

    def __getitem__(self, name):
        return self.full[name]

    def first_gather(self, x):
        return _to_bf16(x, bm=1024, name="x_bf16")

    def gather(self, names):
        return None, None

    def scatter(self, names):
        return None, None

    def scatter_start(self, name):
        return None, None

    def gather_small(self):
        return None, None


class _MeshExchange(_Exchange):
    def __init__(self, full, shards, conv_w):
        super().__init__(full)
        self.shards = shards
        self.conv_w = conv_w
        self.reduced = {}
        self.started = {}
        self.small_parts = None

    def first_gather(self, x):
        first = ("ffn1_w_gate", "ffn1_w_up")
        xb, (gate, up, conv_all) = _to_bf16(
            x, bm=1024, name="x_bf16", comm=_gather_comm([self.shards[n] for n in first] + [self.conv_w]))
        self.take(first[0], gate)
        self.take(first[1], up)
        self.full["conv_w"] = _to_full("conv_w", conv_all)
        return xb

    def gather(self, names):
        def done(outs):
            for n, o in zip(names, outs):
                self.take(n, o)
        return _gather_comm([self.shards[n] for n in names]), done

    def take(self, name, gathered):
        self.full[name] = gathered.reshape((N_DEV * gathered.shape[1],) + gathered.shape[2:])

    def scatter(self, names):
        def done(outs):
            self.reduced.update(zip(names, outs))
        return _scatter_comm([_to_owner_blocks(n, self.grads[n]) for n in names]), done

    def scatter_start(self, name):
        def done(outs):
            self.started[name] = outs
        return _scatter_start_comm(_to_owner_blocks(name, self.grads[name])), done

    def finish(self, after):
        for name, started in self.started.items():
            self.reduced[name] = _scatter_wait(started, after, name=f"scatter_wait_{name}")

    def gather_small(self):
        def done(outs):
            self.small_parts, = outs
        packed = jnp.concatenate([_rows128(self.grads[n]) for n in SMALL_GRADS], axis=0)
        return _gather_comm([packed]), done


def _carried(comm_done, call):
    comm, done = comm_done
    res = call(comm)
    if comm is None:
        return res
    res, outs = res
    done(outs)
    return res


def _dw(a, b, *, scale=1.0, name, comm=None):
    k, m = a.shape
    n = b.shape[1]
    return _mm(a, b, ta=True, scale=scale, out_dtype=BF16, bm=_pick(m, (1024, 512, 256, 128)),
               bn=_pick(n, (512, 256, 128)), bk=k, name=name, comm=comm)


def _ffn_bwd(ex, names, saved, xb_in, dz, dzb, ln_in, tag, on_dwd=None, on_dh=None, on_dwu=None, on_dx=None):
    gate, up, down = names
    g, u, h, _, _ = saved
    f = ex[gate].shape[0]

    def request(fn):
        return (None, None) if fn is None else fn(ex)

    ex.grads[down] = _carried(request(on_dwd), lambda c: _dw(h, dzb, scale=0.5, name=f"{tag}_dwd", comm=c))
    dg, du = _carried(request(on_dh), lambda c: _ffn_bwd_dh(
        dzb, ex[down], g, u, scale=0.5, bm=2048, bn=_pick(f, (512, 256, 128)), name=f"{tag}_dh", chunks=8, comm=c))
    ex.grads[gate] = _dw(xb_in, dg, name=f"{tag}_dwg")
    ex.grads[up] = _carried(request(on_dwu), lambda c: _dw(xb_in, du, name=f"{tag}_dwu", comm=c))
    d = dz.shape[1]
    dx = _carried(request(on_dx), lambda c: _ffn_dx(
        dg, du, ex[gate], ex[up], dz, extra_scale=DEEPNORM_ALPHA,
        bm=1024, bn=_pick(d, (512, 256, 128)), name=f"{tag}_dx", comm=c))
    return dx if ln_in is None else _ln_bwd(dx, *ln_in, bm=256, name=f"{tag}_ln_bwd")


def _local_step(x, p, target, positions, w):
    s, d = x.shape
    tabs = _rope_tables(positions)
    xb = w.first_gather(x)
    f = w["ffn1_w_gate"].shape[0]
    ffn_bn, ln_bn, ln_bn_short_k = _pick(f, (512, 256, 128)), _pick(d, (512, 256, 128)), _pick(d, (1024, 512, 256, 128))
    g1, u1, h1 = _carried(w.gather(("ffn1_w_down", "w_in", "w_out")), lambda c: _ffn_up(
        xb, w["ffn1_w_gate"], w["ffn1_w_up"], bm=1024, bn=ffn_bn, name="ffn1_up", comm=c))
    x1, x1b, xh1, rs1 = _carried(w.gather(("ffn2_w_gate", "ffn2_w_up")), lambda c: _mm_ln(
        h1, w["ffn1_w_down"], x, w["ln1_g"], w["ln1_b"], res_scale=DEEPNORM_ALPHA, mm_scale=0.5,
        bm=512, bn=ln_bn, name="ffn1_down_ln", comm=c))
    sv1 = (g1, u1, h1, xh1, rs1)
    pw = w["w_in"].shape[0]
    proj = _carried(w.gather(("ffn2_w_down", "w_ple_gate", "w_ple_proj")), lambda c: _mm(
        x1b, w["w_in"], tb=True, bm=1024, bn=_pick(pw, (512, 256, 128)), bk=d, name="in_proj", comm=c))
    nqk = (N_PATTERNS + 1) * N_KV_HEADS
    qkr = _rotary(proj, tabs, n_cols=nqk, inverse=False, out_dtype=F32, bs=1024, name="rotary")
    attn, lse = _attn_fwd(qkr, proj, name="attn_fwd")
    lru_w = (w["conv_w"], w["conv_b"], w["w_rgate"], w["b_rgate"], w["w_igate"], w["b_igate"], w["lru_lambda"])
    rec, hseq = _lru_fwd(proj, *lru_w, tt=512, name="lru_fwd")
    cat = jnp.concatenate([attn.astype(BF16), rec], axis=1)
    x2, x2b, xh2, rs2 = _mm_ln(cat, w["w_out"], x1, w["ln2_g"], w["ln2_b"], res_scale=DEEPNORM_ALPHA, mm_scale=1.0,
                               bm=512, bn=ln_bn_short_k, name="out_proj_ln")
    g2, u2, h2 = _ffn_up(x2b, w["ffn2_w_gate"], w["ffn2_w_up"], bm=1024, bn=ffn_bn, name="ffn2_up")
    x3, x3b, xh3, rs3 = _mm_ln(h2, w["ffn2_w_down"], x2, w["ln3_g"], w["ln3_b"], res_scale=DEEPNORM_ALPHA, mm_scale=0.5,
                               bm=512, bn=ln_bn, name="ffn2_down_ln")
    sv3 = (g2, u2, h2, xh3, rs3)
    lsum, dy, dgate, dple = _ple_loss(x3, x3b, p, w["w_ple_gate"], w["w_ple_proj"], target,
                                      bm=1024, bn=_pick(d, (512, 256, 128)), name="ple_loss")
    grads = w.grads
    grads["w_ple_gate"] = _dw(x3b, dgate, name="dw_ple_gate")
    grads["w_ple_proj"] = _dw(p, dple, name="dw_ple_proj")
    dz3, dz3b, grads["ln3_g"], grads["ln3_b"] = _carried(w.scatter(("w_ple_gate", "w_ple_proj")), lambda c: _mm_dx(
        dgate, w["w_ple_gate"], dy, xh3, rs3, w["ln3_g"], extra_scale=1.0, bm=512, bn=ln_bn_short_k,
        name="ple_dx", tb=True, comm=c))
    dz2, dz2b, grads["ln2_g"], grads["ln2_b"] = _ffn_bwd(
        w, ("ffn2_w_gate", "ffn2_w_up", "ffn2_w_down"), sv3, x2b, dz3, dz3b, (xh2, rs2, w["ln2_g"]), "ffn2",
        on_dx=lambda ex: ex.scatter(("ffn2_w_down",)))
    grads["w_out"] = _dw(cat, dz2b, name="dw_out")
    dcat = _mm(dz2b, w["w_out"], tb=True, bm=1024, bn=_pick(d, (512, 256, 128)), bk=d, name="out_proj_dx")
    dq0, dq1, dq2, dk, dvb = _carried(w.scatter(("ffn2_w_gate",)), lambda c: _attn_bwd(
        qkr, proj, attn, lse, dcat, name="attn_bwd", comm=c))
    nh = N_KV_HEADS
    dqkv = [_rotary(t, tabs, n_cols=nh, inverse=True, out_dtype=BF16, bs=1024, name=f"rotary_bwd{i}")
            for i, t in enumerate((dq0, dq1, dq2, dk))]
    (dxb, dyb, grads["conv_w"], grads["conv_b"], grads["w_rgate"], grads["b_rgate"], grads["w_igate"],
     grads["b_igate"], grads["lru_lambda"]) = _carried(w.scatter(("ffn2_w_up",)), lambda c: _lru_bwd(
         proj, hseq, dcat, *lru_w, tt=512, name="lru_bwd", comm=c))
    dproj = jnp.concatenate(dqkv + [dvb, dxb, dyb], axis=1)
    grads["w_in"] = _carried(w.scatter(("w_out",)), lambda c: _dw(x1b, dproj, name="dw_in", comm=c))
    dz1, dz1b, grads["ln1_g"], grads["ln1_b"] = _carried(w.scatter(("w_in",)), lambda c: _mm_dx(
        dproj, w["w_in"], dz2, xh1, rs1, w["ln1_g"], extra_scale=DEEPNORM_ALPHA,
        bm=512, bn=ln_bn, name="in_proj_dx", comm=c))
    grad_x = _ffn_bwd(w, ("ffn1_w_gate", "ffn1_w_up", "ffn1_w_down"), sv1, xb, dz1, dz1b, None, "ffn1",
                      on_dwd=lambda ex: ex.gather_small(),
                      on_dh=lambda ex: ex.scatter_start("ffn1_w_down"),
                      on_dwu=lambda ex: ex.scatter_start("ffn1_w_gate"),
                      on_dx=lambda ex: ex.scatter_start("ffn1_w_up"))
    return lsum, grad_x


def _to_full(name, gathered):
    if name in COLUMN_SHARDED:
        _, r, c = gathered.shape
        return jnp.transpose(gathered, (1, 0, 2)).reshape(r, N_DEV * c)
    return gathered.reshape((N_DEV * gathered.shape[1],) + gathered.shape[2:])


def _to_owner_blocks(name, full):
    if name in COLUMN_SHARDED:
        r, c = full.shape
        return jnp.transpose(full.reshape(r, N_DEV, c // N_DEV), (1, 0, 2))
    return full.reshape((N_DEV, full.shape[0] // N_DEV) + full.shape[1:])


def _rows128(a):
    flat = a.reshape(-1, LANES)
    pad = (-flat.shape[0]) % 8
    return jnp.pad(flat, ((0, pad), (0, 0))) if pad else flat


def kernel(x, p, positions, ffn1_w_gate, ffn1_w_up, ffn1_w_down, ln1_g, ln1_b, w_in, conv_w, conv_b, w_rgate, b_rgate, w_igate, b_igate, lru_lambda, w_out, ln2_g, ln2_b, ffn2_w_gate, ffn2_w_up, ffn2_w_down, ln3_g, ln3_b, w_ple_proj, w_ple_gate, loss_target, m_ffn1_w_gate, m_ffn1_w_up, m_ffn1_w_down, m_ln1_g, m_ln1_b, m_w_in, m_conv_w, m_conv_b, m_w_rgate, m_b_rgate, m_w_igate, m_b_igate, m_lru_lambda, m_w_out, m_ln2_g, m_ln2_b, m_ffn2_w_gate, m_ffn2_w_up, m_ffn2_w_down, m_ln3_g, m_ln3_b, m_w_ple_proj, m_w_ple_gate, v_ffn1_w_gate, v_ffn1_w_up, v_ffn1_w_down, v_ln1_g, v_ln1_b, v_w_in, v_conv_w, v_conv_b, v_w_rgate, v_b_rgate, v_w_igate, v_b_igate, v_lru_lambda, v_w_out, v_ln2_g, v_ln2_b, v_ffn2_w_gate, v_ffn2_w_up, v_ffn2_w_down, v_ln3_g, v_ln3_b, v_w_ple_proj, v_w_ple_gate):
    names = ("ffn1_w_gate", "ffn1_w_up", "ffn1_w_down", "ln1_g", "ln1_b", "w_in", "conv_w", "conv_b", "w_rgate",
             "b_rgate", "w_igate", "b_igate", "lru_lambda", "w_out", "ln2_g", "ln2_b", "ffn2_w_gate", "ffn2_w_up",
             "ffn2_w_down", "ln3_g", "ln3_b", "w_ple_proj", "w_ple_gate")
    ws = (ffn1_w_gate, ffn1_w_up, ffn1_w_down, ln1_g, ln1_b, w_in, conv_w, conv_b, w_rgate, b_rgate, w_igate, b_igate,
          lru_lambda, w_out, ln2_g, ln2_b, ffn2_w_gate, ffn2_w_up, ffn2_w_down, ln3_g, ln3_b, w_ple_proj, w_ple_gate)
    ms = (m_ffn1_w_gate, m_ffn1_w_up, m_ffn1_w_down, m_ln1_g, m_ln1_b, m_w_in, m_conv_w, m_conv_b, m_w_rgate, m_b_rgate,
          m_w_igate, m_b_igate, m_lru_lambda, m_w_out, m_ln2_g, m_ln2_b, m_ffn2_w_gate, m_ffn2_w_up, m_ffn2_w_down,
          m_ln3_g, m_ln3_b, m_w_ple_proj, m_w_ple_gate)
    vs = (v_ffn1_w_gate, v_ffn1_w_up, v_ffn1_w_down, v_ln1_g, v_ln1_b, v_w_in, v_conv_w, v_conv_b, v_w_rgate, v_b_rgate,
          v_w_igate, v_b_igate, v_lru_lambda, v_w_out, v_ln2_g, v_ln2_b, v_ffn2_w_gate, v_ffn2_w_up, v_ffn2_w_down,
          v_ln3_g, v_ln3_b, v_w_ple_proj, v_w_ple_gate)
    def local(a):
        return a[0] if a.ndim >= 3 else a

    w_loc = {n: local(a) for n, a in zip(names, ws)}
    m_loc = {n: local(a) for n, a in zip(names, ms)}
    v_loc = {n: local(a) for n, a in zip(names, vs)}
    out_shapes = {n: a.shape for n, a in zip(names, ws)}

    shards = {n: (w_loc[n].T if n in COLUMN_SHARDED else w_loc[n]).astype(BF16) for n in BIG_WEIGHTS}
    ex = _MeshExchange({n: w_loc[n] for n in SMALL_WEIGHTS}, shards, w_loc["conv_w"])

    lsum, grad_x = _local_step(x[0], p[0, 0], loss_target[0], positions[0], ex)
    ex.finish(grad_x)
    grads, reduced = ex.grads, ex.reduced
    d_model = x.shape[-1]
    loss = lax.psum(lsum[0, 0] * (0.5 / d_model), ("x", "y", "c"))

    small = SMALL_GRADS
    summed = _sum_parts(ex.small_parts, name="sum_small_grads")
    small_grads, row = {}, 0
    for n in small:
        rows = grads[n].size // LANES
        small_grads[n] = summed[row:row + rows].reshape(grads[n].shape)
        row += rows + (-rows) % 8
    me = 4 * lax.axis_index("x") + 2 * lax.axis_index("y") + lax.axis_index("c")
    cw_cols = w_loc["conv_w"].shape[1]
    small_grads["conv_w"] = lax.dynamic_slice_in_dim(small_grads["conv_w"], me * cw_cols, cw_cols, axis=1)

    out_g, out_d, out_m, out_v = {}, {}, {}, {}
    for n in names:
        wl, ml, vl = w_loc[n], m_loc[n], v_loc[n]
        shape = wl.shape
        if n in BIG_WEIGHTS:
            gparts = reduced[n]
        else:
            gparts = small_grads[n].reshape((1,) + shape)
        if wl.ndim == 3:
            wl, ml, vl = (t.reshape(-1, shape[-1]) for t in (wl, ml, vl))
            gparts = gparts.reshape(gparts.shape[0], -1, shape[-1])
        res = _adamw(wl, ml, vl, gparts, name=f"adamw_{n}")
        out_g[n], out_d[n], out_m[n], out_v[n] = (t.reshape(out_shapes[n]) for t in res)

    return (loss, grad_x[None], *[out_g[n] for n in names], *[out_d[n] for n in names],
            *[out_m[n] for n in names], *[out_v[n] for n in names])
```

```python
import jax
import jax.numpy as jnp
from jax import lax
from jax.experimental import pallas as pl
from jax.experimental.pallas import tpu as pltpu

F32 = jnp.float32
BF16 = jnp.bfloat16

N_DEV = 8
LANES = 128
MIB = 1 << 20

HEAD_DIM = 128
N_KV_HEADS = 4
DILATIONS = (1, 4, 16)
N_PATTERNS = 3
SPAN = 128
ROT_DIMS = 32
ROPE_THETA = 500000.0
LRU_C = 8.0
CONV_WIDTH = 4
LN_EPS = 1e-5
DEEPNORM_ALPHA = 2.0 ** 0.25
ATTN_TILE = SPAN * DILATIONS[-1]

ADAM_LR = 0.001
ADAM_B1 = 0.9
ADAM_B2 = 0.999
ADAM_EPS = 1e-08
ADAM_WD = 0.01
ADAM_STEP = 10

MESH = pl.DeviceIdType.MESH
NT_DIMS = (((1,), (1,)), ((), ()))
EPILOGUE_ROWS = 64


def _cp(semantics, vmem_mib):
    return pltpu.CompilerParams(dimension_semantics=semantics, vmem_limit_bytes=vmem_mib * MIB)


def _pick(n, candidates):
    for c in candidates:
        if n % c == 0:
            return c
    return n


class _Comm:
    def __init__(self, arrays, out_shapes, scratch, start, end, mid=None, split=False):
        self.arrays, self.out_shapes, self.scratch = list(arrays), list(out_shapes), list(scratch)
        self.start, self.mid, self.end, self.split = start, mid, end, split


def _call(body, *, name, grid, in_specs, out_specs, out_shape, args, scratch_shapes=(), vmem_mib, comm=None):
    single = not isinstance(out_shape, (tuple, list))
    out_shape_t = (out_shape,) if single else tuple(out_shape)
    out_specs_t = (out_specs,) if single else tuple(out_specs)
    params = _cp(("arbitrary",) * len(grid), vmem_mib)
    if comm is None:
        res = pl.pallas_call(body, name=name, grid=grid, in_specs=list(in_specs), out_specs=out_specs_t,
                             out_shape=out_shape_t, scratch_shapes=list(scratch_shapes), compiler_params=params)(*args)
        return res[0] if single else res
    n_in, n_out, n_scr = len(args), len(out_shape_t), len(scratch_shapes)
    nci, nco = len(comm.arrays), len(comm.out_shapes)
    total = 1
    for g in grid:
        total *= g

    def wrapped(*refs):
        ins, refs = refs[:n_in], refs[n_in:]
        cin, refs = refs[:nci], refs[nci:]
        outs, refs = refs[:n_out], refs[n_out:]
        cout, refs = refs[:nco], refs[nco:]
        scr, csem = refs[:n_scr], refs[n_scr:]
        step = pl.program_id(0)
        for ax in range(1, len(grid)):
            step = step * grid[ax] + pl.program_id(ax)

        @pl.when(step == 0)
        def _():
            comm.start(cin, cout, csem)

        body(*ins, *outs, *scr)
        if comm.mid is not None:
            @pl.when(step == (3 * total) // 4)
            def _():
                comm.mid(cin, cout, csem)

        @pl.when(step == total - 1)
        def _():
            comm.end(cin, cout, csem)

    hbm = pl.BlockSpec(memory_space=pltpu.HBM)
    if comm.split:
        sem = pl.BlockSpec(memory_space=pltpu.SEMAPHORE)
        n_sems = nco - nci
        res = pl.pallas_call(
            wrapped, name=name, grid=grid,
            in_specs=list(in_specs) + [hbm] * nci,
            out_specs=out_specs_t + (sem,) * n_sems + (hbm,) * nci,
            out_shape=out_shape_t + tuple(comm.out_shapes),
            scratch_shapes=list(scratch_shapes) + comm.scratch,
            input_output_aliases={n_in + k: n_out + n_sems + k for k in range(nci)},
            compiler_params=pltpu.CompilerParams(
                dimension_semantics=("arbitrary",) * len(grid), vmem_limit_bytes=vmem_mib * MIB,
                has_side_effects=pltpu.SideEffectType.DATAFLOW_SIDE_EFFECTING),
        )(*args, *[pltpu.with_memory_space_constraint(a, pltpu.HBM) for a in comm.arrays])
    else:
        res = pl.pallas_call(
            wrapped, name=name, grid=grid,
            in_specs=list(in_specs) + [hbm] * nci,
            out_specs=out_specs_t + (hbm,) * nco,
            out_shape=out_shape_t + tuple(comm.out_shapes),
            scratch_shapes=list(scratch_shapes) + comm.scratch,
            compiler_params=params)(*args, *comm.arrays)
    own, extra = res[:n_out], res[n_out:]
    return (own[0] if single else own), extra


def _mm(a, b, *, ta=False, tb=False, out_dtype=F32, scale=1.0, bm, bn, bk, name, comm=None):
    m, k = (a.shape[1], a.shape[0]) if ta else a.shape
    n = b.shape[0] if tb else b.shape[1]
    bm, bn, bk = min(bm, m), min(bn, n), min(bk, k)
    assert m % bm == 0 and n % bn == 0 and k % bk == 0, (name, m, n, k, bm, bn, bk)
    nk = k // bk
    a_spec = pl.BlockSpec((bk, bm), lambda i, j, kk: (kk, i)) if ta else pl.BlockSpec((bm, bk), lambda i, j, kk: (i, kk))
    b_spec = pl.BlockSpec((bn, bk), lambda i, j, kk: (j, kk)) if tb else pl.BlockSpec((bk, bn), lambda i, j, kk: (kk, j))
    dn = (((0 if ta else 1,), (1 if tb else 0,)), ((), ()))

    def body(a_ref, b_ref, o_ref, *acc):
        part = lax.dot_general(a_ref[...].astype(BF16), b_ref[...].astype(BF16), dn, preferred_element_type=F32)
        if nk == 1:
            o_ref[...] = (part * scale).astype(out_dtype)
            return
        acc_ref, = acc
        kk = pl.program_id(2)

        @pl.when(kk == 0)
        def _():
            acc_ref[...] = part

        @pl.when(kk > 0)
        def _():
            acc_ref[...] += part

        @pl.when(kk == nk - 1)
        def _():
            o_ref[...] = (acc_ref[...] * scale).astype(out_dtype)

    return _call(
        body, name=name,
        out_shape=jax.ShapeDtypeStruct((m, n), out_dtype),
        grid=(m // bm, n // bn, nk),
        in_specs=[a_spec, b_spec],
        out_specs=pl.BlockSpec((bm, bn), lambda i, j, kk: (i, j)),
        scratch_shapes=[pltpu.VMEM((bm, bn), F32)] if nk > 1 else [],
        args=(a, b), vmem_mib=56, comm=comm)


def _ffn_up(xb, wg, wu, *, bm, bn, name, comm=None):
    s, d = xb.shape
    f = wg.shape[0]
    bm, bn = min(bm, s), min(bn, f)
    assert s % bm == 0 and f % bn == 0

    def body(x_ref, wg_ref, wu_ref, hg_ref, hu_ref, h_ref):
        x = x_ref[...]
        g = lax.dot_general(x, wg_ref[...], NT_DIMS, preferred_element_type=F32)
        u = lax.dot_general(x, wu_ref[...], NT_DIMS, preferred_element_type=F32)
        sig = jax.nn.sigmoid(g)
        silu = g * sig
        hg_ref[...] = (u * (sig * (1.0 + g * (1.0 - sig)))).astype(BF16)
        hu_ref[...] = silu.astype(BF16)
        h_ref[...] = (silu * u).astype(BF16)

    out = jax.ShapeDtypeStruct((s, f), BF16)
    blk = pl.BlockSpec((bm, bn), lambda i, j: (i, j))
    return _call(
        body, name=name, out_shape=(out, out, out),
        grid=(s // bm, f // bn),
        in_specs=[pl.BlockSpec((bm, d), lambda i, j: (i, 0)),
                  pl.BlockSpec((bn, d), lambda i, j: (j, 0)),
                  pl.BlockSpec((bn, d), lambda i, j: (j, 0))],
        out_specs=(blk, blk, blk),
        args=(xb, wg, wu), vmem_mib=56, comm=comm)


def _ffn_bwd_dh(dzb, wd, g, u, *, scale, bm, bn, name, chunks=2, comm=None):
    s, d = dzb.shape
    f = wd.shape[0]
    bm, bn = min(bm, s), min(bn, f)
    assert s % bm == 0 and f % bn == 0

    cr = bm // chunks

    def body(dz_ref, wd_ref, hg_ref, hu_ref, dg_ref, du_ref):
        for r in range(chunks):
            rows = slice(r * cr, (r + 1) * cr)
            dh = lax.dot_general(dz_ref[rows, :], wd_ref[...], NT_DIMS, preferred_element_type=F32) * scale
            dg_ref[rows, :] = (dh * hg_ref[rows, :].astype(F32)).astype(BF16)
            du_ref[rows, :] = (dh * hu_ref[rows, :].astype(F32)).astype(BF16)

    out = jax.ShapeDtypeStruct((s, f), BF16)
    blk = pl.BlockSpec((bm, bn), lambda i, j: (i, j))
    return _call(
        body, name=name, out_shape=(out, out),
        grid=(s // bm, f // bn),
        in_specs=[pl.BlockSpec((bm, d), lambda i, j: (i, 0)),
                  pl.BlockSpec((bn, d), lambda i, j: (j, 0)), blk, blk],
        out_specs=(blk, blk),
        args=(dzb, wd, g, u), vmem_mib=56, comm=comm)


def _full_rows(acc_ref, rows, nj):
    return jnp.concatenate([acc_ref[jj, rows, :] for jj in range(nj)], axis=1)


def _mm_ln(a, b, res, gamma, beta, *, res_scale, mm_scale, bm, bn, name, comm=None):
    s, k = a.shape
    d = b.shape[1]
    bm, bn = min(bm, s), min(bn, d)
    assert s % bm == 0 and d % bn == 0
    nj = d // bn
    ch = min(EPILOGUE_ROWS, bm)

    def body(a_ref, b_ref, r_ref, g_ref, be_ref, y_ref, yb_ref, xh_ref, rs_ref, acc_ref):
        j = pl.program_id(1)
        acc_ref[j] = jnp.dot(a_ref[...], b_ref[...], preferred_element_type=F32)

        @pl.when(j == nj - 1)
        def _():
            def chunk(ci, carry):
                rows = pl.ds(pl.multiple_of(ci * ch, ch), ch)
                z = res_scale * r_ref[rows, :] + mm_scale * _full_rows(acc_ref, rows, nj)
                mu = jnp.mean(z, axis=-1, keepdims=True)
                zc = z - mu
                var = jnp.mean(zc * zc, axis=-1, keepdims=True)
                rstd = lax.rsqrt(var + LN_EPS)
                xh = zc * rstd
                y = xh * g_ref[...] + be_ref[...]
                y_ref[rows, :] = y
                yb_ref[rows, :] = y.astype(BF16)
                xh_ref[rows, :] = xh
                rs_ref[rows, :] = rstd
                return carry

            lax.fori_loop(0, bm // ch, chunk, 0)

    row = pl.BlockSpec((bm, d), lambda i, j: (i, 0))
    vec = pl.BlockSpec((1, d), lambda i, j: (0, 0))
    return _call(
        body, name=name,
        out_shape=(jax.ShapeDtypeStruct((s, d), F32), jax.ShapeDtypeStruct((s, d), BF16),
                   jax.ShapeDtypeStruct((s, d), F32), jax.ShapeDtypeStruct((s, 1), F32)),
        grid=(s // bm, nj),
        in_specs=[pl.BlockSpec((bm, k), lambda i, j: (i, 0)),
                  pl.BlockSpec((k, bn), lambda i, j: (0, j)), row, vec, vec],
        out_specs=(row, row, row, pl.BlockSpec((bm, 1), lambda i, j: (i, 0))),
        scratch_shapes=[pltpu.VMEM((nj, bm, bn), F32)],
        args=(a, b, res, gamma, beta), vmem_mib=58, comm=comm)


def _mm_dx(a, wt, extra, xhat, rstd, gamma, *, extra_scale, bm, bn, name, tb=False, comm=None):
    s, k = a.shape
    d = wt.shape[0] if tb else wt.shape[1]
    bm, bn = min(bm, s), min(bn, d)
    assert s % bm == 0 and d % bn == 0
    nj = d // bn
    ch = min(EPILOGUE_ROWS, bm)
    dims = NT_DIMS if tb else (((1,), (0,)), ((), ()))

    def body(a_ref, w_ref, e_ref, xh_ref, rs_ref, g_ref, dz_ref, dzb_ref, dg_ref, db_ref, acc_ref):
        i = pl.program_id(0)
        j = pl.program_id(1)
        acc_ref[j] = lax.dot_general(a_ref[...], w_ref[...], dims, preferred_element_type=F32)

        @pl.when(j == nj - 1)
        def _():
            def chunk(ci, carry):
                dgp, dbp = carry
                rows = pl.ds(pl.multiple_of(ci * ch, ch), ch)
                dx = extra_scale * e_ref[rows, :] + _full_rows(acc_ref, rows, nj)
                xh = xh_ref[rows, :]
                dxh = dx * g_ref[...]
                m1 = jnp.mean(dxh, axis=-1, keepdims=True)
                m2 = jnp.mean(dxh * xh, axis=-1, keepdims=True)
                dz = rs_ref[rows, :] * (dxh - m1 - xh * m2)
                dz_ref[rows, :] = dz
                dzb_ref[rows, :] = dz.astype(BF16)
                return dgp + jnp.sum(dx * xh, axis=0, keepdims=True), dbp + jnp.sum(dx, axis=0, keepdims=True)

            zero = jnp.zeros((1, d), F32)
            dgp, dbp = lax.fori_loop(0, bm // ch, chunk, (zero, zero))

            @pl.when(i == 0)
            def _():
                dg_ref[...] = dgp
                db_ref[...] = dbp

            @pl.when(i > 0)
            def _():
                dg_ref[...] += dgp
                db_ref[...] += dbp

    row = pl.BlockSpec((bm, d), lambda i, j: (i, 0))
    vec = pl.BlockSpec((1, d), lambda i, j: (0, 0))
    return _call(
        body, name=name,
        out_shape=(jax.ShapeDtypeStruct((s, d), F32), jax.ShapeDtypeStruct((s, d), BF16),
                   jax.ShapeDtypeStruct((1, d), F32), jax.ShapeDtypeStruct((1, d), F32)),
        grid=(s // bm, nj),
        in_specs=[pl.BlockSpec((bm, k), lambda i, j: (i, 0)),
                  pl.BlockSpec((bn, k), lambda i, j: (j, 0)) if tb else pl.BlockSpec((k, bn), lambda i, j: (0, j)),
                  row, row, pl.BlockSpec((bm, 1), lambda i, j: (i, 0)), vec],
        out_specs=(row, row, vec, vec),
        scratch_shapes=[pltpu.VMEM((nj, bm, bn), F32)],
        args=(a, wt, extra, xhat, rstd, gamma), vmem_mib=58, comm=comm)


def _ffn_dx(dg, du, wgt, wut, extra, *, extra_scale, bm, bn, name, comm=None):
    s, f = dg.shape
    d = wgt.shape[1]
    bm, bn = min(bm, s), min(bn, d)
    assert s % bm == 0 and d % bn == 0 and f % (2 * LANES) == 0
    nj, half = d // bn, f // 2

    def body(dg_ref, du_ref, wg_ref, wu_ref, e_ref, o_ref, acc_ref):
        kh, j = pl.program_id(1), pl.program_id(2)
        part = jnp.dot(dg_ref[...], wg_ref[...], preferred_element_type=F32)
        part = part + jnp.dot(du_ref[...], wu_ref[...], preferred_element_type=F32)

        @pl.when(kh == 0)
        def _():
            acc_ref[j] = part

        @pl.when(kh == 1)
        def _():
            o_ref[...] = extra_scale * e_ref[...] + (acc_ref[j] + part)

    rows = pl.BlockSpec((bm, half), lambda i, kh, j: (i, kh))
    cols = pl.BlockSpec((half, bn), lambda i, kh, j: (kh, j))
    blk = pl.BlockSpec((bm, bn), lambda i, kh, j: (i, j * kh))
    return _call(
        body, name=name, out_shape=jax.ShapeDtypeStruct((s, d), F32),
        grid=(s // bm, 2, nj), in_specs=[rows, rows, cols, cols, blk], out_specs=blk,
        scratch_shapes=[pltpu.VMEM((nj, bm, bn), F32)],
        args=(dg, du, wgt, wut, extra), vmem_mib=58, comm=comm)


def _ln_bwd(dx, xhat, rstd, gamma, *, bm, name):
    s, d = dx.shape
    bm = min(bm, s)
    assert s % bm == 0
    ch = min(EPILOGUE_ROWS, bm)

    def body(dx_ref, xh_ref, rs_ref, g_ref, dz_ref, dzb_ref, dg_ref, db_ref):
        def chunk(ci, carry):
            dgp, dbp = carry
            rows = pl.ds(pl.multiple_of(ci * ch, ch), ch)
            dxv = dx_ref[rows, :]
            xh = xh_ref[rows, :]
            dxh = dxv * g_ref[...]
            m1 = jnp.mean(dxh, axis=-1, keepdims=True)
            m2 = jnp.mean(dxh * xh, axis=-1, keepdims=True)
            dz = rs_ref[rows, :] * (dxh - m1 - xh * m2)
            dz_ref[rows, :] = dz
            dzb_ref[rows, :] = dz.astype(BF16)
            return dgp + jnp.sum(dxv * xh, axis=0, keepdims=True), dbp + jnp.sum(dxv, axis=0, keepdims=True)

        zero = jnp.zeros((1, d), F32)
        dgp, dbp = lax.fori_loop(0, bm // ch, chunk, (zero, zero))
        i = pl.program_id(0)

        @pl.when(i == 0)
        def _():
            dg_ref[...] = dgp
            db_ref[...] = dbp

        @pl.when(i > 0)
        def _():
            dg_ref[...] += dgp
            db_ref[...] += dbp

    row = pl.BlockSpec((bm, d), lambda i: (i, 0))
    vec = pl.BlockSpec((1, d), lambda i: (0, 0))
    return _call(
        body, name=name,
        out_shape=(jax.ShapeDtypeStruct((s, d), F32), jax.ShapeDtypeStruct((s, d), BF16),
                   jax.ShapeDtypeStruct((1, d), F32), jax.ShapeDtypeStruct((1, d), F32)),
        grid=(s // bm,), in_specs=[row, row, pl.BlockSpec((bm, 1), lambda i: (i, 0)), vec],
        out_specs=(row, row, vec, vec), args=(dx, xhat, rstd, gamma), vmem_mib=48)


def _to_bf16(x, *, bm, name, comm=None):
    s, d = x.shape
    bm = min(bm, s)
    assert s % bm == 0

    def body(x_ref, o_ref):
        o_ref[...] = x_ref[...].astype(BF16)

    row = pl.BlockSpec((bm, d), lambda i: (i, 0))
    return _call(body, name=name, out_shape=jax.ShapeDtypeStruct((s, d), BF16), grid=(s // bm,),
                 in_specs=[row], out_specs=row, args=(x,), vmem_mib=48, comm=comm)


def _ple_loss(x3, x3b, p, wpg, wpp, target, *, bm, bn, name):
    s, d = x3.shape
    dp = p.shape[1]
    bm, bn = min(bm, s), min(bn, d)
    assert s % bm == 0 and d % bn == 0
    inv_d = 1.0 / d
    chunks = 4 if bm % 64 == 0 else 1
    cr = bm // chunks

    def body(x_ref, xb_ref, p_ref, wg_ref, wp_ref, t_ref, l_ref, dy_ref, dg_ref, dp_ref):
        first = (pl.program_id(0) == 0) & (pl.program_id(1) == 0)

        @pl.when(first)
        def _():
            l_ref[...] = jnp.zeros_like(l_ref)

        part = 0.0
        for r in range(chunks):
            rows = slice(r * cr, (r + 1) * cr)
            gp = jnp.dot(xb_ref[rows, :], wg_ref[...], preferred_element_type=F32)
            pp = lax.dot_general(p_ref[rows, :].astype(BF16), wp_ref[...], NT_DIMS, preferred_element_type=F32)
            sig = jax.nn.sigmoid(gp)
            err = x_ref[rows, :] + sig * pp - t_ref[rows, :]
            part = part + jnp.sum(err * err)
            dy = err * inv_d
            dy_ref[rows, :] = dy
            dg_ref[rows, :] = (dy * pp * sig * (1.0 - sig)).astype(BF16)
            dp_ref[rows, :] = (dy * sig).astype(BF16)
        l_ref[...] += part

    blk = pl.BlockSpec((bm, bn), lambda i, j: (i, j))
    return pl.pallas_call(
        body, name=name,
        out_shape=(jax.ShapeDtypeStruct((8, LANES), F32), jax.ShapeDtypeStruct((s, d), F32),
                   jax.ShapeDtypeStruct((s, d), BF16), jax.ShapeDtypeStruct((s, d), BF16)),
        grid=(s // bm, d // bn),
        in_specs=[blk, pl.BlockSpec((bm, d), lambda i, j: (i, 0)), pl.BlockSpec((bm, dp), lambda i, j: (i, 0)),
                  pl.BlockSpec((d, bn), lambda i, j: (0, j)), pl.BlockSpec((bn, dp), lambda i, j: (j, 0)), blk],
        out_specs=(pl.BlockSpec((8, LANES), lambda i, j: (0, 0)), blk, blk, blk),
        compiler_params=_cp(("arbitrary", "arbitrary"), 56),
    )(x3, x3b, p, wpg, wpp, target)


def _rope_tables(positions):
    half = ROT_DIMS // 2
    lane = jnp.arange(HEAD_DIM)
    inv_freq = jnp.power(jnp.float32(ROPE_THETA), -(lane % half).astype(F32) * (2.0 / ROT_DIMS))
    ang = positions.astype(F32)[:, None] * inv_freq
    cos, sin = jnp.cos(ang), jnp.sin(ang)
    cf = jnp.where(lane < ROT_DIMS, cos, 1.0)
    sa = jnp.where(lane < half, -sin, 0.0)
    sb = jnp.where((lane >= half) & (lane < ROT_DIMS), sin, 0.0)
    return cf, sa, sb


def _rotary(t, tabs, *, n_cols, inverse, out_dtype, bs, name):
    s = t.shape[0]
    bs = min(bs, s)
    half = ROT_DIMS // 2
    heads = N_KV_HEADS
    assert n_cols % heads == 0

    def body(t_ref, cf_ref, sa_ref, sb_ref, o_ref):
        cf, sa, sb = cf_ref[...], sa_ref[...], sb_ref[...]
        for hd in range(heads):
            lanes = slice(hd * HEAD_DIM, (hd + 1) * HEAD_DIM)
            v = t_ref[:, lanes]
            if inverse:
                o = v * cf + pltpu.roll(v * sa, half, 1) + pltpu.roll(v * sb, HEAD_DIM - half, 1)
            else:
                o = v * cf + pltpu.roll(v, HEAD_DIM - half, 1) * sa + pltpu.roll(v, half, 1) * sb
            o_ref[:, lanes] = o.astype(out_dtype)

    blk = pl.BlockSpec((bs, heads * HEAD_DIM), lambda i, j: (i, j))
    tab = pl.BlockSpec((bs, HEAD_DIM), lambda i, j: (i, 0))
    return pl.pallas_call(
        body, name=name, out_shape=jax.ShapeDtypeStruct((s, n_cols * HEAD_DIM), out_dtype),
        grid=(s // bs, n_cols // heads), in_specs=[blk, tab, tab, tab], out_specs=blk,
        compiler_params=_cp(("parallel", "arbitrary"), 32),
    )(t, *tabs)


def _attn_blocks():
    out = []
    for g, dil in enumerate(DILATIONS):
        sup = SPAN * dil
        for j in range(ATTN_TILE // sup):
            for r in range(dil):
                out.append((g, j * sup + r, dil, (j - 1) * sup + r if j > 0 else None, ATTN_TILE - sup + r))
    return out


def _rows(ref, start, dil, lead=None):
    idx = pl.ds(start, SPAN, stride=dil) if dil > 1 else pl.ds(start, SPAN)
    return ref[idx, :] if lead is None else ref[lead, idx, :]


def _band_masks(n):
    qi = lax.broadcasted_iota(jnp.int32, (SPAN, 2 * SPAN), 0)
    ki = lax.broadcasted_iota(jnp.int32, (SPAN, 2 * SPAN), 1)
    band = (ki >= qi) & (ki <= qi + SPAN)
    return band, band & ((ki >= SPAN) | (n > 0))


def _attn_fwd(qkr, proj, *, name):
    s = qkr.shape[0]
    t = ATTN_TILE
    assert s % t == 0
    nt = s // t
    scale = HEAD_DIM ** -0.5
    kcol, vcol = N_PATTERNS * N_KV_HEADS, (N_PATTERNS + 1) * N_KV_HEADS
    blocks = _attn_blocks()

    def body(q0, q1, q2, kc_ref, kp_ref, vc_ref, vp_ref, o_ref, l_ref, og, lg):
        n = pl.program_id(1)
        band, band_first = _band_masks(n)
        q_refs = (q0, q1, q2)
        for g, start, dil, prev_in_tile, prev_start in blocks:
            q = _rows(q_refs[g], start, dil).astype(BF16)
            if prev_in_tile is not None:
                kp, vp, mask = _rows(kc_ref, prev_in_tile, dil), _rows(vc_ref, prev_in_tile, dil), band
            else:
                kp, vp, mask = _rows(kp_ref, prev_start, dil), _rows(vp_ref, prev_start, dil), band_first
            kk = jnp.concatenate([kp, _rows(kc_ref, start, dil)], axis=0).astype(BF16)
            vv = jnp.concatenate([vp, _rows(vc_ref, start, dil)], axis=0).astype(BF16)
            sc = lax.dot_general(q, kk, (((1,), (1,)), ((), ())), preferred_element_type=F32) * scale
            sc = jnp.where(mask, sc, -1e30)
            m = jnp.max(sc, axis=-1, keepdims=True)
            e = jnp.exp(sc - m)
            den = jnp.sum(e, axis=-1, keepdims=True)
            o = jnp.dot(e.astype(BF16), vv, preferred_element_type=F32) / den
            idx = pl.ds(start, SPAN, stride=dil) if dil > 1 else pl.ds(start, SPAN)
            og[g, idx, :] = o
            lg[g, idx, :] = jnp.broadcast_to(m + jnp.log(den), (SPAN, HEAD_DIM))
        l0, l1, l2 = lg[0], lg[1], lg[2]
        m = jnp.maximum(jnp.maximum(l0, l1), l2)
        w0, w1, w2 = jnp.exp(l0 - m), jnp.exp(l1 - m), jnp.exp(l2 - m)
        den = w0 + w1 + w2
        o_ref[...] = (w0 * og[0] + w1 * og[1] + w2 * og[2]) / den
        l_ref[...] = m + jnp.log(den)

    def col(c, prev=False):
        if prev:
            return pl.BlockSpec((t, HEAD_DIM), lambda h, n: (jnp.maximum(n - 1, 0), c + h))
        return pl.BlockSpec((t, HEAD_DIM), lambda h, n: (n, c + h))

    out = jax.ShapeDtypeStruct((s, N_KV_HEADS * HEAD_DIM), F32)
    return pl.pallas_call(
        body, name=name, out_shape=(out, out),
        grid=(N_KV_HEADS, nt),
        in_specs=[col(0), col(N_KV_HEADS), col(2 * N_KV_HEADS), col(kcol), col(kcol, True), col(vcol), col(vcol, True)],
        out_specs=(col(0), col(0)),
        scratch_shapes=[pltpu.VMEM((N_PATTERNS, t, HEAD_DIM), F32), pltpu.VMEM((N_PATTERNS, t, HEAD_DIM), F32)],
        compiler_params=_cp(("parallel", "arbitrary"), 48),
    )(qkr, qkr, qkr, qkr, qkr, proj, proj)


def _attn_bwd(qkr, proj, attn, lse, dcat, *, name, comm=None):
    s = qkr.shape[0]
    t = ATTN_TILE
    nt = s // t
    scale = HEAD_DIM ** -0.5
    kcol, vcol = N_PATTERNS * N_KV_HEADS, (N_PATTERNS + 1) * N_KV_HEADS
    blocks = _attn_blocks()

    def body(q0, q1, q2, kc_ref, kp_ref, vc_ref, vp_ref, o_ref, l_ref, do_ref,
             dq0, dq1, dq2, dk_ref, dv_ref, ck, cv, tkc, tvc, tkp, tvp):
        n = pl.program_id(1)
        for ref in (tkc, tvc, tkp, tvp):
            ref[...] = jnp.zeros_like(ref)

        @pl.when(n < nt)
        def _():
            band, band_first = _band_masks(n)
            q_refs, dq_refs = (q0, q1, q2), (dq0, dq1, dq2)
            for g, start, dil, prev_in_tile, prev_start in blocks:
                idx = pl.ds(start, SPAN, stride=dil) if dil > 1 else pl.ds(start, SPAN)
                q = q_refs[g][idx, :].astype(BF16)
                if prev_in_tile is not None:
                    kp, vp, mask = _rows(kc_ref, prev_in_tile, dil), _rows(vc_ref, prev_in_tile, dil), band
                else:
                    kp, vp, mask = _rows(kp_ref, prev_start, dil), _rows(vp_ref, prev_start, dil), band_first
                kk = jnp.concatenate([kp, kc_ref[idx, :]], axis=0).astype(BF16)
                vv = jnp.concatenate([vp, vc_ref[idx, :]], axis=0).astype(BF16)
                do = do_ref[idx, :]
                dsum = jnp.sum(do * o_ref[idx, :], axis=-1, keepdims=True)
                lrow = l_ref[idx, :][:, :1]
                dob = do.astype(BF16)
                sc = lax.dot_general(q, kk, (((1,), (1,)), ((), ())), preferred_element_type=F32) * scale
                p = jnp.where(mask, jnp.exp(sc - lrow), 0.0)
                dp = lax.dot_general(dob, vv, (((1,), (1,)), ((), ())), preferred_element_type=F32)
                ds = (p * (dp - dsum) * scale).astype(BF16)
                pb = p.astype(BF16)
                dq_refs[g][idx, :] = jnp.dot(ds, kk, preferred_element_type=F32)
                dkk = lax.dot_general(ds, q, (((0,), (0,)), ((), ())), preferred_element_type=F32)
                dvv = lax.dot_general(pb, dob, (((0,), (0,)), ((), ())), preferred_element_type=F32)
                tkc[idx, :] += dkk[SPAN:]
                tvc[idx, :] += dvv[SPAN:]
                if prev_in_tile is not None:
                    pidx = pl.ds(prev_in_tile, SPAN, stride=dil) if dil > 1 else pl.ds(prev_in_tile, SPAN)
                    tkc[pidx, :] += dkk[:SPAN]
                    tvc[pidx, :] += dvv[:SPAN]
                else:
                    pidx = pl.ds(prev_start, SPAN, stride=dil) if dil > 1 else pl.ds(prev_start, SPAN)
                    tkp[pidx, :] += dkk[:SPAN]
                    tvp[pidx, :] += dvv[:SPAN]

        @pl.when(n > 0)
        def _():
            dk_ref[...] = ck[...] + tkp[...]
            dv_ref[...] = (cv[...] + tvp[...]).astype(BF16)

        ck[...] = tkc[...]
        cv[...] = tvc[...]

    def col(c, prev=False):
        if prev:
            return pl.BlockSpec((t, HEAD_DIM), lambda h, n: (jnp.maximum(jnp.minimum(n, nt - 1) - 1, 0), c + h))
        return pl.BlockSpec((t, HEAD_DIM), lambda h, n: (jnp.minimum(n, nt - 1), c + h))

    kv_out = pl.BlockSpec((t, HEAD_DIM), lambda h, n: (jnp.maximum(n - 1, 0), h))
    tile = pltpu.VMEM((t, HEAD_DIM), F32)
    per_head = jax.ShapeDtypeStruct((s, N_KV_HEADS * HEAD_DIM), F32)
    return _call(
        body, name=name,
        out_shape=(per_head, per_head, per_head, per_head, jax.ShapeDtypeStruct((s, N_KV_HEADS * HEAD_DIM), BF16)),
        grid=(N_KV_HEADS, nt + 1),
        in_specs=[col(0), col(N_KV_HEADS), col(2 * N_KV_HEADS), col(kcol), col(kcol, True), col(vcol), col(vcol, True),
                  col(0), col(0), col(0)],
        out_specs=(col(0), col(0), col(0), kv_out, kv_out),
        scratch_shapes=[tile] * 6,
        args=(qkr, qkr, qkr, qkr, qkr, proj, proj, attn, lse, dcat), vmem_mib=48, comm=comm)


GELU_C0 = 0.7978845608028654
GELU_C1 = 0.044715


def _softplus_neg(lam):
    y = jnp.exp(-jnp.abs(lam))
    w = 1.0 + y
    log1p = jnp.where(w == 1.0, y, jnp.log(w) * (y / jnp.where(w == 1.0, 1.0, w - 1.0)))
    return jnp.maximum(-lam, 0.0) + log1p


def _down(cur, prev, k, row):
    if k == 0:
        return cur
    return jnp.where(row < k, pltpu.roll(prev, k, 0), pltpu.roll(cur, k, 0))


def _up(cur, nxt, k, row, tt):
    if k == 0:
        return cur
    return jnp.where(row >= tt - k, pltpu.roll(nxt, tt - k, 0), pltpu.roll(cur, tt - k, 0))


def _lru_gates(x, xp, cw, cb, wr, br, wi, bi, lam, row):
    shifts = [_down(x, xp, k, row) for k in range(CONV_WIDTH)]
    xc = cb
    for j in range(CONV_WIDTH):
        xc = xc + cw[j:j + 1, :] * shifts[CONV_WIDTH - 1 - j]
    xcb = xc.astype(BF16)
    r = jax.nn.sigmoid(jnp.dot(xcb, wr, preferred_element_type=F32) + br)
    i = jax.nn.sigmoid(jnp.dot(xcb, wi, preferred_element_type=F32) + bi)
    c = -LRU_C * _softplus_neg(lam)
    la = c * r
    a = jnp.exp(la)
    mult = jnp.sqrt(jnp.tanh(-la) * (a * a + 1.0))
    return shifts, xc, xcb, r, i, c, a, mult


def _lru_fwd(proj, cw, cb, wr, br, wi, bi, lam, *, tt, name):
    s = proj.shape[0]
    nblk = wr.shape[0]
    c = nblk * LANES
    tt = min(tt, s)
    xcol0 = (N_PATTERNS + 2) * N_KV_HEADS
    ycol0 = xcol0 + nblk

    def body(x_ref, y_ref, cw_ref, cb_ref, wr_ref, br_ref, wi_ref, bi_ref, lam_ref, rec_ref, h_ref, xprev, hc):
        n = pl.program_id(1)

        @pl.when(n == 0)
        def _():
            xprev[...] = jnp.zeros_like(xprev)
            hc[...] = jnp.zeros_like(hc)

        row = lax.broadcasted_iota(jnp.int32, (tt, LANES), 0)
        x = x_ref[...]
        _, xc, _, _, i, _, a, mult = _lru_gates(
            x, xprev[...], cw_ref[...], cb_ref[...], wr_ref[0].astype(BF16), br_ref[...],
            wi_ref[0].astype(BF16), bi_ref[...], lam_ref[...], row)
        av, bv = a, mult * (i * xc)
        k = 1
        while k < tt:
            bs = jnp.where(row < k, 0.0, pltpu.roll(bv, k, 0))
            as_ = jnp.where(row < k, 1.0, pltpu.roll(av, k, 0))
            bv = bv + av * bs
            av = av * as_
            k *= 2
        h = bv + av * hc[0:1, :]
        hc[...] = jnp.broadcast_to(h[tt - 1:tt, :], hc.shape)
        h_ref[...] = h
        y = y_ref[...]
        gel = 0.5 * y * (1.0 + jnp.tanh(GELU_C0 * (y + GELU_C1 * y * y * y)))
        rec_ref[...] = (h * gel).astype(BF16)
        xprev[...] = x

    vec = pl.BlockSpec((1, LANES), lambda b, n: (0, b))
    wblk = pl.BlockSpec((1, LANES, LANES), lambda b, n: (b, 0, 0))
    out = pl.BlockSpec((tt, LANES), lambda b, n: (n, b))
    return pl.pallas_call(
        body, name=name,
        out_shape=(jax.ShapeDtypeStruct((s, c), BF16), jax.ShapeDtypeStruct((s, c), F32)),
        grid=(nblk, s // tt),
        in_specs=[pl.BlockSpec((tt, LANES), lambda b, n: (n, xcol0 + b)),
                  pl.BlockSpec((tt, LANES), lambda b, n: (n, ycol0 + b)),
                  pl.BlockSpec((CONV_WIDTH, LANES), lambda b, n: (0, b)), vec, wblk, vec, wblk, vec, vec],
        out_specs=(out, out),
        scratch_shapes=[pltpu.VMEM((tt, LANES), F32), pltpu.VMEM((8, LANES), F32)],
        compiler_params=_cp(("parallel", "arbitrary"), 32),
    )(proj, proj, cw, cb, wr, br, wi, bi, lam)


def _lru_bwd(proj, hseq, dcat, cw, cb, wr, br, wi, bi, lam, *, tt, name, comm=None):
    s = proj.shape[0]
    nblk = wr.shape[0]
    c = nblk * LANES
    tt = min(tt, s)
    nt = s // tt
    xcol0 = (N_PATTERNS + 2) * N_KV_HEADS
    ycol0 = xcol0 + nblk
    rcol0 = N_KV_HEADS

    def body(x_ref, xp_ref, y_ref, h_ref, hp_ref, dr_ref, cw_ref, cb_ref, wr_ref, br_ref, wi_ref, bi_ref, lam_ref,
             dx_ref, dy_ref, dcw_ref, dcb_ref, dwr_ref, dbr_ref, dwi_ref, dbi_ref, dlam_ref, dxc_next, gcar, acar):
        n = pl.program_id(1)
        rt = nt - 1 - n

        @pl.when(n == 0)
        def _():
            for ref in (dxc_next, gcar, acar, dcw_ref, dcb_ref, dwr_ref, dbr_ref, dwi_ref, dbi_ref, dlam_ref):
                ref[...] = jnp.zeros_like(ref)

        row = lax.broadcasted_iota(jnp.int32, (tt, LANES), 0)
        x = x_ref[...]
        xp = jnp.where(rt > 0, xp_ref[...], 0.0)
        cwv = cw_ref[...]
        wrb, wib = wr_ref[0].astype(BF16), wi_ref[0].astype(BF16)
        lam_v = lam_ref[...]
        shifts, xc, xcb, r, i, cc, a, mult = _lru_gates(x, xp, cwv, cb_ref[...], wrb, br_ref[...], wib, bi_ref[...],
                                                        lam_v, row)
        h = h_ref[...]
        hp_last = jnp.where(rt > 0, hp_ref[7:8, :], 0.0)
        hprev = jnp.where(row < 1, hp_last, pltpu.roll(h, 1, 0))
        y = y_ref[...]
        y2 = y * y
        th = jnp.tanh(GELU_C0 * (y + GELU_C1 * y2 * y))
        gel = 0.5 * y * (1.0 + th)
        dgel = 0.5 * (1.0 + th) + 0.5 * y * (1.0 - th * th) * GELU_C0 * (1.0 + 3.0 * GELU_C1 * y2)
        drec = dr_ref[...]
        dy_ref[...] = (drec * h * dgel).astype(BF16)
        av = jnp.where(row >= tt - 1, acar[0:1, :], pltpu.roll(a, tt - 1, 0))
        bv = drec * gel
        k = 1
        while k < tt:
            bs = jnp.where(row >= tt - k, 0.0, pltpu.roll(bv, tt - k, 0))
            as_ = jnp.where(row >= tt - k, 1.0, pltpu.roll(av, tt - k, 0))
            bv = bv + av * bs
            av = av * as_
            k *= 2
        g = bv + av * gcar[0:1, :]
        gcar[...] = jnp.broadcast_to(g[0:1, :], gcar.shape)
        acar[...] = jnp.broadcast_to(a[0:1, :], acar.shape)
        da = g * hprev
        d_ixc = g * mult
        dmult = g * (i * xc)
        di = d_ixc * xc
        dxc = d_ixc * i
        a2 = a * a
        dla = da * a - dmult * (a2 / mult)
        dr = dla * cc
        dsp = jnp.sum(dla * r, axis=0, keepdims=True) * (-LRU_C)
        dlam_ref[...] += dsp * (-jax.nn.sigmoid(-lam_v))
        dzr = dr * r * (1.0 - r)
        dzi = di * i * (1.0 - i)
        dbr_ref[...] += jnp.sum(dzr, axis=0, keepdims=True)
        dbi_ref[...] += jnp.sum(dzi, axis=0, keepdims=True)
        dzrb, dzib = dzr.astype(BF16), dzi.astype(BF16)
        tn = (((0,), (0,)), ((), ()))
        ntd = (((1,), (1,)), ((), ()))
        dwr_ref[0] += lax.dot_general(xcb, dzrb, tn, preferred_element_type=F32)
        dwi_ref[0] += lax.dot_general(xcb, dzib, tn, preferred_element_type=F32)
        dxc = (dxc + lax.dot_general(dzrb, wrb, ntd, preferred_element_type=F32)
               + lax.dot_general(dzib, wib, ntd, preferred_element_type=F32))
        dcb_ref[...] += jnp.sum(dxc, axis=0, keepdims=True)
        dcw_ref[...] += jnp.concatenate(
            [jnp.sum(dxc * shifts[CONV_WIDTH - 1 - j], axis=0, keepdims=True) for j in range(CONV_WIDTH)], axis=0)
        nxt = dxc_next[...]
        dx = cwv[0:1, :] * _up(dxc, nxt, CONV_WIDTH - 1, row, tt)
        for j in range(1, CONV_WIDTH):
            dx = dx + cwv[j:j + 1, :] * _up(dxc, nxt, CONV_WIDTH - 1 - j, row, tt)
        dx_ref[...] = dx.astype(BF16)
        dxc_next[...] = dxc

    def tile(col0, prev=False):
        if prev:
            return pl.BlockSpec((tt, LANES), lambda b, n: (jnp.maximum(nt - 2 - n, 0), col0 + b))
        return pl.BlockSpec((tt, LANES), lambda b, n: (nt - 1 - n, col0 + b))

    vec = pl.BlockSpec((1, LANES), lambda b, n: (0, b))
    wblk = pl.BlockSpec((1, LANES, LANES), lambda b, n: (b, 0, 0))
    cwblk = pl.BlockSpec((CONV_WIDTH, LANES), lambda b, n: (0, b))
    hp8 = pl.BlockSpec((8, LANES), lambda b, n: (jnp.maximum((nt - 1 - n) * (tt // 8) - 1, 0), b))
    vshape = jax.ShapeDtypeStruct((1, c), F32)
    wshape = jax.ShapeDtypeStruct((nblk, LANES, LANES), F32)
    return _call(
        body, name=name,
        out_shape=(jax.ShapeDtypeStruct((s, c), BF16), jax.ShapeDtypeStruct((s, c), BF16),
                   jax.ShapeDtypeStruct((CONV_WIDTH, c), F32), vshape, wshape, vshape, wshape, vshape, vshape),
        grid=(nblk, nt),
        in_specs=[tile(xcol0), tile(xcol0, True), tile(ycol0), tile(0), hp8, tile(rcol0),
                  cwblk, vec, wblk, vec, wblk, vec, vec],
        out_specs=(tile(0), tile(0), cwblk, vec, wblk, vec, wblk, vec, vec),
        scratch_shapes=[pltpu.VMEM((tt, LANES), F32), pltpu.VMEM((8, LANES), F32), pltpu.VMEM((8, LANES), F32)],
        args=(proj, proj, proj, hseq, hseq, dcat, cw, cb, wr, br, wi, bi, lam), vmem_mib=32, comm=comm)


ROW_BLOCKS = (512, 256, 176, 128, 64, 32, 16, 8)


def _adamw(w, m, v, gparts, *, name):
    r, c = w.shape
    npart = gparts.shape[0]
    br = _pick(r, ROW_BLOCKS)
    c1 = 1.0 - ADAM_B1 ** ADAM_STEP
    c2 = 1.0 - ADAM_B2 ** ADAM_STEP

    def body(w_ref, m_ref, v_ref, g_ref, go_ref, d_ref, mo_ref, vo_ref):
        g = g_ref[0].astype(F32)
        for q in range(1, npart):
            g = g + g_ref[q].astype(F32)
        mn = ADAM_B1 * m_ref[...] + (1.0 - ADAM_B1) * g
        vn = ADAM_B2 * v_ref[...] + (1.0 - ADAM_B2) * (g * g)
        go_ref[...] = g
        mo_ref[...] = mn
        vo_ref[...] = vn
        d_ref[...] = -ADAM_LR * ((mn / c1) / (jnp.sqrt(vn / c2) + ADAM_EPS) + ADAM_WD * w_ref[...])

    blk = pl.BlockSpec((br, c), lambda i: (i, 0))
    out = jax.ShapeDtypeStruct((r, c), F32)
    return pl.pallas_call(
        body, name=name, out_shape=(out, out, out, out), grid=(r // br,),
        in_specs=[blk, blk, blk, pl.BlockSpec((npart, br, c), lambda i: (0, i, 0))],
        out_specs=(blk, blk, blk, blk),
        compiler_params=_cp(("parallel",), 48),
    )(w, m, v, gparts)


def _sum_parts(parts, *, name):
    npart, r, c = parts.shape
    br = next((b for b in range(min(r, 2048) // 8 * 8, 0, -8) if r % b == 0), r)

    def body(p_ref, o_ref):
        acc = p_ref[0]
        for q in range(1, npart):
            acc = acc + p_ref[q]
        o_ref[...] = acc

    return pl.pallas_call(
        body, name=name, out_shape=jax.ShapeDtypeStruct((r, c), F32), grid=(r // br,),
        in_specs=[pl.BlockSpec((npart, br, c), lambda i: (0, i, 0))],
        out_specs=pl.BlockSpec((br, c), lambda i: (i, 0)),
        compiler_params=_cp(("parallel",), 48),
    )(parts)


HBM = pl.BlockSpec(memory_space=pltpu.HBM)


def _mesh_pos():
    return lax.axis_index("x"), lax.axis_index("y"), lax.axis_index("c")


def _gather_comm(shards):
    na = len(shards)

    def parts(x_refs, out_refs, sems):
        send_sems, recv_sems, local_sems = sems
        x, y, c = _mesh_pos()
        me, sibling = (x, y, c), (x, y, 1 - c)
        chips = [(1 - x, y), (x, 1 - y), (1 - x, 1 - y)]

        def copy(a, k, block, to, src=None):
            px, py, pc = block
            dst = out_refs[a].at[4 * px + 2 * py + pc]
            return pltpu.make_async_remote_copy(
                src_ref=dst if src is None else src, dst_ref=dst,
                send_sem=send_sems.at[a, k], recv_sem=recv_sems.at[a, k],
                device_id=to, device_id_type=MESH)

        def mine(a):
            return pltpu.make_async_copy(x_refs[a], out_refs[a].at[4 * x + 2 * y + c], local_sems.at[a])

        def first(a):
            return [copy(a, 0, me, sibling, src=x_refs[a])] + [
                copy(a, 1 + j, me, (*chip, c), src=x_refs[a]) for j, chip in enumerate(chips)]

        def passed(a, j):
            return copy(a, 4 + j, (*chips[j], c), sibling)

        return me, sibling, chips, c, copy, mine, first, passed

    def start(x_refs, out_refs, sems):
        *_, mine, first, _ = parts(x_refs, out_refs, sems)
        for a in range(na):
            mine(a).start()
            for cp in first(a):
                cp.start()

    def mid(x_refs, out_refs, sems):
        me, _, chips, c, copy, _, _, passed = parts(x_refs, out_refs, sems)
        for j, chip in enumerate(chips):
            for a in range(na):
                copy(a, 1 + j, (*chip, c), me).wait_recv()
                passed(a, j).start()

    def end(x_refs, out_refs, sems):
        me, sibling, chips, c, copy, mine, first, passed = parts(x_refs, out_refs, sems)
        for a in range(na):
            copy(a, 0, sibling, me).wait_recv()
            for j, chip in enumerate(chips):
                copy(a, 4 + j, (*chip, 1 - c), me).wait_recv()
        for a in range(na):
            for cp in first(a) + [passed(a, j) for j in range(3)]:
                cp.wait_send()
            mine(a).wait()

    return _Comm(
        shards, [jax.ShapeDtypeStruct((N_DEV,) + a.shape, a.dtype) for a in shards],
        [pltpu.SemaphoreType.DMA((na, 7)), pltpu.SemaphoreType.DMA((na, 7)), pltpu.SemaphoreType.DMA((na,))],
        start, end, mid)


def _scatter_comm(g8s):
    na = len(g8s)

    def parts(g_refs, buf_refs, sems):
        send_sems, recv_sems, local_sems = sems
        x, y, c = _mesh_pos()
        me_idx = 4 * x + 2 * y + c

        def copy(a, k, slot):
            peer, peer_idx = _scatter_peer(k, x, y, c)
            return pltpu.make_async_remote_copy(
                src_ref=g_refs[a].at[peer_idx], dst_ref=buf_refs[a].at[me_idx if slot is None else slot],
                send_sem=send_sems.at[a, k - 1], recv_sem=recv_sems.at[a, k - 1],
                device_id=peer, device_id_type=MESH)

        def mine(a):
            return pltpu.make_async_copy(g_refs[a].at[me_idx], buf_refs[a].at[me_idx], local_sems.at[a])

        return x, y, c, copy, mine

    def start(g_refs, buf_refs, sems):
        *_, copy, mine = parts(g_refs, buf_refs, sems)
        for a in range(na):
            mine(a).start()
            for k in range(1, N_DEV):
                copy(a, k, None).start()

    def end(g_refs, buf_refs, sems):
        x, y, c, copy, mine = parts(g_refs, buf_refs, sems)
        for a in range(na):
            for k in range(1, N_DEV):
                copy(a, k, _scatter_peer(k, x, y, c)[1]).wait_recv()
        for a in range(na):
            for k in range(1, N_DEV):
                copy(a, k, None).wait_send()
            mine(a).wait()

    return _Comm(
        g8s, [jax.ShapeDtypeStruct(g.shape, g.dtype) for g in g8s],
        [pltpu.SemaphoreType.DMA((na, N_DEV - 1)), pltpu.SemaphoreType.DMA((na, N_DEV - 1)),
         pltpu.SemaphoreType.DMA((na,))],
        start, end)


def _scatter_peer(k, x, y, c):
    px, py, pc = (1 - x if k & 4 else x, 1 - y if k & 2 else y, 1 - c if k & 1 else c)
    return (px, py, pc), 4 * px + 2 * py + pc


def _scatter_start_comm(g8s):
    na = len(g8s)
    arrays = []
    for g8 in g8s:
        arrays += [g8, lax.empty(g8.shape, g8.dtype)]

    def local(refs, sems, a, me_idx):
        return pltpu.make_async_copy(refs[2 * a].at[me_idx], refs[2 * a + 1].at[me_idx], sems[0].at[a])

    def start(refs, outs, sems):
        send_sems, recv_sems = outs[:2]
        x, y, c = _mesh_pos()
        me_idx = 4 * x + 2 * y + c
        for a in range(na):
            local(refs, sems, a, me_idx).start()
            for k in range(1, N_DEV):
                peer, peer_idx = _scatter_peer(k, x, y, c)
                pltpu.make_async_remote_copy(
                    src_ref=refs[2 * a].at[peer_idx], dst_ref=refs[2 * a + 1].at[me_idx],
                    send_sem=send_sems.at[a * (N_DEV - 1) + k - 1], recv_sem=recv_sems.at[a * (N_DEV - 1) + k - 1],
                    device_id=peer, device_id_type=MESH).start()

    def end(refs, outs, sems):
        x, y, c = _mesh_pos()
        for a in range(na):
            local(refs, sems, a, 4 * x + 2 * y + c).wait()

    sem_shape = pltpu.SemaphoreType.DMA((na * (N_DEV - 1),))
    return _Comm(arrays, [sem_shape, sem_shape] + [pltpu.HBM(a.shape, a.dtype) for a in arrays],
                 [pltpu.SemaphoreType.DMA((na,))], start, end, split=True)


def _scatter_wait(started, after, *, name):
    send_sems, recv_sems, *arrays = started
    na = len(arrays) // 2

    def body(*refs):
        send_ref, recv_ref = refs[2 * na], refs[2 * na + 1]
        x, y, c = _mesh_pos()
        for a in range(na):
            for k in range(1, N_DEV):
                peer, peer_idx = _scatter_peer(k, x, y, c)
                pltpu.make_async_remote_copy(
                    src_ref=refs[2 * a].at[peer_idx], dst_ref=refs[2 * a + 1].at[peer_idx],
                    send_sem=send_ref.at[a * (N_DEV - 1) + k - 1], recv_sem=recv_ref.at[a * (N_DEV - 1) + k - 1],
                    device_id=peer, device_id_type=MESH).wait()

    sem = pl.BlockSpec(memory_space=pltpu.SEMAPHORE)
    outs = pl.pallas_call(
        body, name=name, out_shape=tuple(pltpu.HBM(a.shape, a.dtype) for a in arrays),
        in_specs=[HBM] * (2 * na) + [sem, sem, pl.BlockSpec(memory_space=pl.ANY)], out_specs=(HBM,) * (2 * na),
        input_output_aliases={k: k for k in range(2 * na)},
        compiler_params=pltpu.CompilerParams(has_side_effects=pltpu.SideEffectType.DATAFLOW_SIDE_EFFECTING),
    )(*arrays, send_sems, recv_sems, after)
    return outs[1::2]


BIG_WEIGHTS = ("ffn1_w_gate", "ffn1_w_up", "ffn1_w_down", "w_in", "w_out",
               "ffn2_w_gate", "ffn2_w_up", "ffn2_w_down", "w_ple_proj", "w_ple_gate")
COLUMN_SHARDED = ("ffn1_w_gate", "ffn1_w_up", "w_in", "ffn2_w_gate", "ffn2_w_up", "w_ple_proj", "conv_w")
SMALL_WEIGHTS = ("ln1_g", "ln1_b", "conv_b", "w_rgate", "b_rgate", "w_igate", "b_igate", "lru_lambda",
                 "ln2_g", "ln2_b", "ln3_g", "ln3_b")
SMALL_GRADS = SMALL_WEIGHTS + ("conv_w",)


class _Exchange:
    def __init__(self, full):
        self.full = dict(full)
        self.grads = {}

    def __getitem__(self, name):
        return self.full[name]

    def first_gather(self, x):
        return _to_bf16(x, bm=1024, name="x_bf16")

    def gather(self, names):
        return None, None

    def scatter(self, names):
        return None, None

    def scatter_start(self, names):
        return None, None

    def gather_small(self):
        return None, None


class _MeshExchange(_Exchange):
    def __init__(self, full, shards, conv_w):
        super().__init__(full)
        self.shards = shards
        self.conv_w = conv_w
        self.reduced = {}
        self.started = {}
        self.small_parts = None

    def first_gather(self, x):
        first = ("ffn1_w_gate", "ffn1_w_up")
        xb, (gate, up, conv_all) = _to_bf16(
            x, bm=1024, name="x_bf16", comm=_gather_comm([self.shards[n] for n in first] + [self.conv_w]))
        self.take(first[0], gate)
        self.take(first[1], up)
        self.full["conv_w"] = _to_full("conv_w", conv_all)
        return xb

    def gather(self, names):
        def done(outs):
            for n, o in zip(names, outs):
                self.take(n, o)
        return _gather_comm([self.shards[n] for n in names]), done

    def take(self, name, gathered):
        self.full[name] = gathered.reshape((N_DEV * gathered.shape[1],) + gathered.shape[2:])

    def scatter(self, names):
        def done(outs):
            self.reduced.update(zip(names, outs))
        return _scatter_comm([_to_owner_blocks(n, self.grads[n]) for n in names]), done

    def scatter_start(self, names):
        def done(outs):
            self.started[names] = outs
        return _scatter_start_comm([_to_owner_blocks(n, self.grads[n]) for n in names]), done

    def finish(self, after):
        for names, started in self.started.items():
            self.reduced.update(zip(names, _scatter_wait(started, after, name=f"scatter_wait_{names[0]}")))

    def gather_small(self):
        def done(outs):
            self.small_parts, = outs
        packed = jnp.concatenate([_rows128(self.grads[n]) for n in SMALL_GRADS], axis=0)
        return _gather_comm([packed]), done


def _carried(comm_done, call):
    comm, done = comm_done
    res = call(comm)
    if comm is None:
        return res
    res, outs = res
    done(outs)
    return res


def _dw(a, b, *, scale=1.0, name, comm=None):
    k, m = a.shape
    n = b.shape[1]
    return _mm(a, b, ta=True, scale=scale, out_dtype=BF16, bm=_pick(m, (1024, 512, 256, 128)),
               bn=_pick(n, (512, 256, 128)), bk=k, name=name, comm=comm)


def _ffn_bwd(ex, names, saved, xb_in, dz, dzb, ln_in, tag, on_dwd=None, on_dh=None, on_dwu=None, on_dx=None):
    gate, up, down = names
    g, u, h, _, _ = saved
    f = ex[gate].shape[0]

    def request(fn):
        return (None, None) if fn is None else fn(ex)

    ex.grads[down] = _carried(request(on_dwd), lambda c: _dw(h, dzb, scale=0.5, name=f"{tag}_dwd", comm=c))
    dg, du = _carried(request(on_dh), lambda c: _ffn_bwd_dh(
        dzb, ex[down], g, u, scale=0.5, bm=2048, bn=_pick(f, (512, 256, 128)), name=f"{tag}_dh", chunks=8, comm=c))
    ex.grads[gate] = _dw(xb_in, dg, name=f"{tag}_dwg")
    ex.grads[up] = _carried(request(on_dwu), lambda c: _dw(xb_in, du, name=f"{tag}_dwu", comm=c))
    d = dz.shape[1]
    dx = _carried(request(on_dx), lambda c: _ffn_dx(
        dg, du, ex[gate], ex[up], dz, extra_scale=DEEPNORM_ALPHA,
        bm=1024, bn=_pick(d, (512, 256, 128)), name=f"{tag}_dx", comm=c))
    return dx if ln_in is None else _ln_bwd(dx, *ln_in, bm=256, name=f"{tag}_ln_bwd")


def _local_step(x, p, target, positions, w):
    s, d = x.shape
    tabs = _rope_tables(positions)
    xb = w.first_gather(x)
    f = w["ffn1_w_gate"].shape[0]
    ffn_bn, ln_bn, ln_bn_short_k = _pick(f, (512, 256, 128)), _pick(d, (512, 256, 128)), _pick(d, (1024, 512, 256, 128))
    g1, u1, h1 = _carried(w.gather(("ffn1_w_down", "w_in", "w_out")), lambda c: _ffn_up(
        xb, w["ffn1_w_gate"], w["ffn1_w_up"], bm=1024, bn=ffn_bn, name="ffn1_up", comm=c))
    x1, x1b, xh1, rs1 = _carried(w.gather(("ffn2_w_gate", "ffn2_w_up")), lambda c: _mm_ln(
        h1, w["ffn1_w_down"], x, w["ln1_g"], w["ln1_b"], res_scale=DEEPNORM_ALPHA, mm_scale=0.5,
        bm=512, bn=ln_bn, name="ffn1_down_ln", comm=c))
    sv1 = (g1, u1, h1, xh1, rs1)
    pw = w["w_in"].shape[0]
    proj = _carried(w.gather(("ffn2_w_down", "w_ple_gate", "w_ple_proj")), lambda c: _mm(
        x1b, w["w_in"], tb=True, bm=1024, bn=_pick(pw, (512, 256, 128)), bk=d, name="in_proj", comm=c))
    nqk = (N_PATTERNS + 1) * N_KV_HEADS
    qkr = _rotary(proj, tabs, n_cols=nqk, inverse=False, out_dtype=F32, bs=1024, name="rotary")
    attn, lse = _attn_fwd(qkr, proj, name="attn_fwd")
    lru_w = (w["conv_w"], w["conv_b"], w["w_rgate"], w["b_rgate"], w["w_igate"], w["b_igate"], w["lru_lambda"])
    rec, hseq = _lru_fwd(proj, *lru_w, tt=512, name="lru_fwd")
    cat = jnp.concatenate([attn.astype(BF16), rec], axis=1)
    x2, x2b, xh2, rs2 = _mm_ln(cat, w["w_out"], x1, w["ln2_g"], w["ln2_b"], res_scale=DEEPNORM_ALPHA, mm_scale=1.0,
                               bm=512, bn=ln_bn_short_k, name="out_proj_ln")
    g2, u2, h2 = _ffn_up(x2b, w["ffn2_w_gate"], w["ffn2_w_up"], bm=1024, bn=ffn_bn, name="ffn2_up")
    x3, x3b, xh3, rs3 = _mm_ln(h2, w["ffn2_w_down"], x2, w["ln3_g"], w["ln3_b"], res_scale=DEEPNORM_ALPHA, mm_scale=0.5,
                               bm=512, bn=ln_bn, name="ffn2_down_ln")
    sv3 = (g2, u2, h2, xh3, rs3)
    lsum, dy, dgate, dple = _ple_loss(x3, x3b, p, w["w_ple_gate"], w["w_ple_proj"], target,
                                      bm=1024, bn=_pick(d, (512, 256, 128)), name="ple_loss")
    grads = w.grads
    grads["w_ple_gate"] = _dw(x3b, dgate, name="dw_ple_gate")
    grads["w_ple_proj"] = _dw(p, dple, name="dw_ple_proj")
    dz3, dz3b, grads["ln3_g"], grads["ln3_b"] = _carried(w.scatter(("w_ple_gate", "w_ple_proj")), lambda c: _mm_dx(
        dgate, w["w_ple_gate"], dy, xh3, rs3, w["ln3_g"], extra_scale=1.0, bm=512, bn=ln_bn_short_k,
        name="ple_dx", tb=True, comm=c))
    dz2, dz2b, grads["ln2_g"], grads["ln2_b"] = _ffn_bwd(
        w, ("ffn2_w_gate", "ffn2_w_up", "ffn2_w_down"), sv3, x2b, dz3, dz3b, (xh2, rs2, w["ln2_g"]), "ffn2",
        on_dx=lambda ex: ex.scatter_start(("ffn2_w_down", "ffn2_w_gate", "ffn2_w_up")))
    grads["w_out"] = _dw(cat, dz2b, name="dw_out")
    dcat = _mm(dz2b, w["w_out"], tb=True, bm=1024, bn=_pick(d, (512, 256, 128)), bk=d, name="out_proj_dx")
    dq0, dq1, dq2, dk, dvb = _attn_bwd(qkr, proj, attn, lse, dcat, name="attn_bwd")
    nh = N_KV_HEADS
    dqkv = [_rotary(t, tabs, n_cols=nh, inverse=True, out_dtype=BF16, bs=1024, name=f"rotary_bwd{i}")
            for i, t in enumerate((dq0, dq1, dq2, dk))]
    (dxb, dyb, grads["conv_w"], grads["conv_b"], grads["w_rgate"], grads["b_rgate"], grads["w_igate"],
     grads["b_igate"], grads["lru_lambda"]) = _lru_bwd(proj, hseq, dcat, *lru_w, tt=512, name="lru_bwd")
    dproj = jnp.concatenate(dqkv + [dvb, dxb, dyb], axis=1)
    grads["w_in"] = _dw(x1b, dproj, name="dw_in")
    dz1, dz1b, grads["ln1_g"], grads["ln1_b"] = _carried(w.scatter_start(("w_out", "w_in")), lambda c: _mm_dx(
        dproj, w["w_in"], dz2, xh1, rs1, w["ln1_g"], extra_scale=DEEPNORM_ALPHA,
        bm=512, bn=ln_bn, name="in_proj_dx", comm=c))
    grad_x = _ffn_bwd(w, ("ffn1_w_gate", "ffn1_w_up", "ffn1_w_down"), sv1, xb, dz1, dz1b, None, "ffn1",
                      on_dwd=lambda ex: ex.gather_small(),
                      on_dh=lambda ex: ex.scatter_start(("ffn1_w_down",)),
                      on_dwu=lambda ex: ex.scatter_start(("ffn1_w_gate",)),
                      on_dx=lambda ex: ex.scatter_start(("ffn1_w_up",)))
    return lsum, grad_x


def _to_full(name, gathered):
    if name in COLUMN_SHARDED:
        _, r, c = gathered.shape
        return jnp.transpose(gathered, (1, 0, 2)).reshape(r, N_DEV * c)
    return gathered.reshape((N_DEV * gathered.shape[1],) + gathered.shape[2:])


def _to_owner_blocks(name, full):
    if name in COLUMN_SHARDED:
        r, c = full.shape
        return jnp.transpose(full.reshape(r, N_DEV, c // N_DEV), (1, 0, 2))
    return full.reshape((N_DEV, full.shape[0] // N_DEV) + full.shape[1:])


def _rows128(a):
    flat = a.reshape(-1, LANES)
    pad = (-flat.shape[0]) % 8
    return jnp.pad(flat, ((0, pad), (0, 0))) if pad else flat


def kernel(x, p, positions, ffn1_w_gate, ffn1_w_up, ffn1_w_down, ln1_g, ln1_b, w_in, conv_w, conv_b, w_rgate, b_rgate, w_igate, b_igate, lru_lambda, w_out, ln2_g, ln2_b, ffn2_w_gate, ffn2_w_up, ffn2_w_down, ln3_g, ln3_b, w_ple_proj, w_ple_gate, loss_target, m_ffn1_w_gate, m_ffn1_w_up, m_ffn1_w_down, m_ln1_g, m_ln1_b, m_w_in, m_conv_w, m_conv_b, m_w_rgate, m_b_rgate, m_w_igate, m_b_igate, m_lru_lambda, m_w_out, m_ln2_g, m_ln2_b, m_ffn2_w_gate, m_ffn2_w_up, m_ffn2_w_down, m_ln3_g, m_ln3_b, m_w_ple_proj, m_w_ple_gate, v_ffn1_w_gate, v_ffn1_w_up, v_ffn1_w_down, v_ln1_g, v_ln1_b, v_w_in, v_conv_w, v_conv_b, v_w_rgate, v_b_rgate, v_w_igate, v_b_igate, v_lru_lambda, v_w_out, v_ln2_g, v_ln2_b, v_ffn2_w_gate, v_ffn2_w_up, v_ffn2_w_down, v_ln3_g, v_ln3_b, v_w_ple_proj, v_w_ple_gate):
    names = ("ffn1_w_gate", "ffn1_w_up", "ffn1_w_down", "ln1_g", "ln1_b", "w_in", "conv_w", "conv_b", "w_rgate",
             "b_rgate", "w_igate", "b_igate", "lru_lambda", "w_out", "ln2_g", "ln2_b", "ffn2_w_gate", "ffn2_w_up",
             "ffn2_w_down", "ln3_g", "ln3_b", "w_ple_proj", "w_ple_gate")
    ws = (ffn1_w_gate, ffn1_w_up, ffn1_w_down, ln1_g, ln1_b, w_in, conv_w, conv_b, w_rgate, b_rgate, w_igate, b_igate,
          lru_lambda, w_out, ln2_g, ln2_b, ffn2_w_gate, ffn2_w_up, ffn2_w_down, ln3_g, ln3_b, w_ple_proj, w_ple_gate)
    ms = (m_ffn1_w_gate, m_ffn1_w_up, m_ffn1_w_down, m_ln1_g, m_ln1_b, m_w_in, m_conv_w, m_conv_b, m_w_rgate, m_b_rgate,
          m_w_igate, m_b_igate, m_lru_lambda, m_w_out, m_ln2_g, m_ln2_b, m_ffn2_w_gate, m_ffn2_w_up, m_ffn2_w_down,
          m_ln3_g, m_ln3_b, m_w_ple_proj, m_w_ple_gate)
    vs = (v_ffn1_w_gate, v_ffn1_w_up, v_ffn1_w_down, v_ln1_g, v_ln1_b, v_w_in, v_conv_w, v_conv_b, v_w_rgate, v_b_rgate,
          v_w_igate, v_b_igate, v_lru_lambda, v_w_out, v_ln2_g, v_ln2_b, v_ffn2_w_gate, v_ffn2_w_up, v_ffn2_w_down,
          v_ln3_g, v_ln3_b, v_w_ple_proj, v_w_ple_gate)
    def local(a):
        return a[0] if a.ndim >= 3 else a

    w_loc = {n: local(a) for n, a in zip(names, ws)}
    m_loc = {n: local(a) for n, a in zip(names, ms)}
    v_loc = {n: local(a) for n, a in zip(names, vs)}
    out_shapes = {n: a.shape for n, a in zip(names, ws)}

    shards = {n: (w_loc[n].T if n in COLUMN_SHARDED else w_loc[n]).astype(BF16) for n in BIG_WEIGHTS}
    ex = _MeshExchange({n: w_loc[n] for n in SMALL_WEIGHTS}, shards, w_loc["conv_w"])

    lsum, grad_x = _local_step(x[0], p[0, 0], loss_target[0], positions[0], ex)
    ex.finish(grad_x)
    grads, reduced = ex.grads, ex.reduced
    d_model = x.shape[-1]
    loss = lax.psum(lsum[0, 0] * (0.5 / d_model), ("x", "y", "c"))

    small = SMALL_GRADS
    summed = _sum_parts(ex.small_parts, name="sum_small_grads")
    small_grads, row = {}, 0
    for n in small:
        rows = grads[n].size // LANES
        small_grads[n] = summed[row:row + rows].reshape(grads[n].shape)
        row += rows + (-rows) % 8
    me = 4 * lax.axis_index("x") + 2 * lax.axis_index("y") + lax.axis_index("c")
    cw_cols = w_loc["conv_w"].shape[1]
    small_grads["conv_w"] = lax.dynamic_slice_in_dim(small_grads["conv_w"], me * cw_cols, cw_cols, axis=1)

    out_g, out_d, out_m, out_v = {}, {}, {}, {}
    for n in names:
        wl, ml, vl = w_loc[n], m_loc[n], v_loc[n]
        shape = wl.shape
        if n in BIG_WEIGHTS:
            gparts = reduced[n]
        else:
            gparts = small_grads[n].reshape((1,) + shape)
        if wl.ndim == 3:
            wl, ml, vl = (t.reshape(-1, shape[-1]) for t in (wl, ml, vl))
            gparts = gparts.reshape(gparts.shape[0], -1, shape[-1])
        res = _adamw(wl, ml, vl, gparts, name=f"adamw_{n}")
        out_g[n], out_d[n], out_m[n], out_v[n] = (t.reshape(out_shapes[n]) for t in res)

    return (loss, grad_x[None], *[out_g[n] for n in names], *[out_d[n] for n in names],
            *[out_m[n] for n in names], *[out_v[n] for n in names])
```

```python
import jax
import jax.numpy as jnp
from jax import lax
from jax.experimental import pallas as pl
from jax.experimental.pallas import tpu as pltpu

F32 = jnp.float32
BF16 = jnp.bfloat16

N_DEV = 8
LANES = 128
MIB = 1 << 20

HEAD_DIM = 128
N_KV_HEADS = 4
DILATIONS = (1, 4, 16)
N_PATTERNS = 3
SPAN = 128
ROT_DIMS = 32
ROPE_THETA = 500000.0
LRU_C = 8.0
CONV_WIDTH = 4
LN_EPS = 1e-5
DEEPNORM_ALPHA = 2.0 ** 0.25
ATTN_TILE = SPAN * DILATIONS[-1]

ADAM_LR = 0.001
ADAM_B1 = 0.9
ADAM_B2 = 0.999
ADAM_EPS = 1e-08
ADAM_WD = 0.01
ADAM_STEP = 10

MESH = pl.DeviceIdType.MESH
NT_DIMS = (((1,), (1,)), ((), ()))
EPILOGUE_ROWS = 64


def _cp(semantics, vmem_mib):
    return pltpu.CompilerParams(dimension_semantics=semantics, vmem_limit_bytes=vmem_mib * MIB)


def _pick(n, candidates):
    for c in candidates:
        if n % c == 0:
            return c
    return n


class _Comm:
    def __init__(self, arrays, out_shapes, scratch, start, end, mid=None, split=False):
        self.arrays, self.out_shapes, self.scratch = list(arrays), list(out_shapes), list(scratch)
        self.start, self.mid, self.end, self.split = start, mid, end, split


def _call(body, *, name, grid, in_specs, out_specs, out_shape, args, scratch_shapes=(), vmem_mib, comm=None):
    single = not isinstance(out_shape, (tuple, list))
    out_shape_t = (out_shape,) if single else tuple(out_shape)
    out_specs_t = (out_specs,) if single else tuple(out_specs)
    params = _cp(("arbitrary",) * len(grid), vmem_mib)
    if comm is None:
        res = pl.pallas_call(body, name=name, grid=grid, in_specs=list(in_specs), out_specs=out_specs_t,
                             out_shape=out_shape_t, scratch_shapes=list(scratch_shapes), compiler_params=params)(*args)
        return res[0] if single else res
    n_in, n_out, n_scr = len(args), len(out_shape_t), len(scratch_shapes)
    nci, nco = len(comm.arrays), len(comm.out_shapes)
    total = 1
    for g in grid:
        total *= g

    def wrapped(*refs):
        ins, refs = refs[:n_in], refs[n_in:]
        cin, refs = refs[:nci], refs[nci:]
        outs, refs = refs[:n_out], refs[n_out:]
        cout, refs = refs[:nco], refs[nco:]
        scr, csem = refs[:n_scr], refs[n_scr:]
        step = pl.program_id(0)
        for ax in range(1, len(grid)):
            step = step * grid[ax] + pl.program_id(ax)

        @pl.when(step == 0)
        def _():
            comm.start(cin, cout, csem)

        body(*ins, *outs, *scr)
        if comm.mid is not None:
            @pl.when(step == (3 * total) // 4)
            def _():
                comm.mid(cin, cout, csem)

        @pl.when(step == total - 1)
        def _():
            comm.end(cin, cout, csem)

    hbm = pl.BlockSpec(memory_space=pltpu.HBM)
    if comm.split:
        sem = pl.BlockSpec(memory_space=pltpu.SEMAPHORE)
        n_sems = nco - nci
        res = pl.pallas_call(
            wrapped, name=name, grid=grid,
            in_specs=list(in_specs) + [hbm] * nci,
            out_specs=out_specs_t + (sem,) * n_sems + (hbm,) * nci,
            out_shape=out_shape_t + tuple(comm.out_shapes),
            scratch_shapes=list(scratch_shapes) + comm.scratch,
            input_output_aliases={n_in + k: n_out + n_sems + k for k in range(nci)},
            compiler_params=pltpu.CompilerParams(
                dimension_semantics=("arbitrary",) * len(grid), vmem_limit_bytes=vmem_mib * MIB,
                has_side_effects=pltpu.SideEffectType.DATAFLOW_SIDE_EFFECTING),
        )(*args, *[pltpu.with_memory_space_constraint(a, pltpu.HBM) for a in comm.arrays])
    else:
        res = pl.pallas_call(
            wrapped, name=name, grid=grid,
            in_specs=list(in_specs) + [hbm] * nci,
            out_specs=out_specs_t + (hbm,) * nco,
            out_shape=out_shape_t + tuple(comm.out_shapes),
            scratch_shapes=list(scratch_shapes) + comm.scratch,
            compiler_params=params)(*args, *comm.arrays)
    own, extra = res[:n_out], res[n_out:]
    return (own[0] if single else own), extra


def _mm(a, b, *, ta=False, tb=False, out_dtype=F32, scale=1.0, bm, bn, bk, name, comm=None):
    m, k = (a.shape[1], a.shape[0]) if ta else a.shape
    n = b.shape[0] if tb else b.shape[1]
    bm, bn, bk = min(bm, m), min(bn, n), min(bk, k)
    assert m % bm == 0 and n % bn == 0 and k % bk == 0, (name, m, n, k, bm, bn, bk)
    nk = k // bk
    a_spec = pl.BlockSpec((bk, bm), lambda i, j, kk: (kk, i)) if ta else pl.BlockSpec((bm, bk), lambda i, j, kk: (i, kk))
    b_spec = pl.BlockSpec((bn, bk), lambda i, j, kk: (j, kk)) if tb else pl.BlockSpec((bk, bn), lambda i, j, kk: (kk, j))
    dn = (((0 if ta else 1,), (1 if tb else 0,)), ((), ()))

    def body(a_ref, b_ref, o_ref, *acc):
        part = lax.dot_general(a_ref[...].astype(BF16), b_ref[...].astype(BF16), dn, preferred_element_type=F32)
        if nk == 1:
            o_ref[...] = (part * scale).astype(out_dtype)
            return
        acc_ref, = acc
        kk = pl.program_id(2)

        @pl.when(kk == 0)
        def _():
            acc_ref[...] = part

        @pl.when(kk > 0)
        def _():
            acc_ref[...] += part

        @pl.when(kk == nk - 1)
        def _():
            o_ref[...] = (acc_ref[...] * scale).astype(out_dtype)

    return _call(
        body, name=name,
        out_shape=jax.ShapeDtypeStruct((m, n), out_dtype),
        grid=(m // bm, n // bn, nk),
        in_specs=[a_spec, b_spec],
        out_specs=pl.BlockSpec((bm, bn), lambda i, j, kk: (i, j)),
        scratch_shapes=[pltpu.VMEM((bm, bn), F32)] if nk > 1 else [],
        args=(a, b), vmem_mib=56, comm=comm)


def _ffn_up(xb, wg, wu, *, bm, bn, name, comm=None):
    s, d = xb.shape
    f = wg.shape[0]
    bm, bn = min(bm, s), min(bn, f)
    assert s % bm == 0 and f % bn == 0

    def body(x_ref, wg_ref, wu_ref, hg_ref, hu_ref, h_ref):
        x = x_ref[...]
        g = lax.dot_general(x, wg_ref[...], NT_DIMS, preferred_element_type=F32)
        u = lax.dot_general(x, wu_ref[...], NT_DIMS, preferred_element_type=F32)
        sig = jax.nn.sigmoid(g)
        silu = g * sig
        hg_ref[...] = (u * (sig * (1.0 + g * (1.0 - sig)))).astype(BF16)
        hu_ref[...] = silu.astype(BF16)
        h_ref[...] = (silu * u).astype(BF16)

    out = jax.ShapeDtypeStruct((s, f), BF16)
    blk = pl.BlockSpec((bm, bn), lambda i, j: (i, j))
    return _call(
        body, name=name, out_shape=(out, out, out),
        grid=(s // bm, f // bn),
        in_specs=[pl.BlockSpec((bm, d), lambda i, j: (i, 0)),
                  pl.BlockSpec((bn, d), lambda i, j: (j, 0)),
                  pl.BlockSpec((bn, d), lambda i, j: (j, 0))],
        out_specs=(blk, blk, blk),
        args=(xb, wg, wu), vmem_mib=56, comm=comm)


def _ffn_bwd_dh(dzb, wd, g, u, *, scale, bm, bn, name, chunks=2, comm=None):
    s, d = dzb.shape
    f = wd.shape[0]
    bm, bn = min(bm, s), min(bn, f)
    assert s % bm == 0 and f % bn == 0

    cr = bm // chunks

    def body(dz_ref, wd_ref, hg_ref, hu_ref, dg_ref, du_ref):
        for r in range(chunks):
            rows = slice(r * cr, (r + 1) * cr)
            dh = lax.dot_general(dz_ref[rows, :], wd_ref[...], NT_DIMS, preferred_element_type=F32) * scale
            dg_ref[rows, :] = (dh * hg_ref[rows, :].astype(F32)).astype(BF16)
            du_ref[rows, :] = (dh * hu_ref[rows, :].astype(F32)).astype(BF16)

    out = jax.ShapeDtypeStruct((s, f), BF16)
    blk = pl.BlockSpec((bm, bn), lambda i, j: (i, j))
    return _call(
        body, name=name, out_shape=(out, out),
        grid=(s // bm, f // bn),
        in_specs=[pl.BlockSpec((bm, d), lambda i, j: (i, 0)),
                  pl.BlockSpec((bn, d), lambda i, j: (j, 0)), blk, blk],
        out_specs=(blk, blk),
        args=(dzb, wd, g, u), vmem_mib=56, comm=comm)


def _full_rows(acc_ref, rows, nj):
    return jnp.concatenate([acc_ref[jj, rows, :] for jj in range(nj)], axis=1)


def _mm_ln(a, b, res, gamma, beta, *, res_scale, mm_scale, bm, bn, name, res_affine=None, comm=None):
    s, k = a.shape
    d = b.shape[1]
    bm, bn = min(bm, s), min(bn, d)
    assert s % bm == 0 and d % bn == 0
    nj = d // bn
    ch = min(EPILOGUE_ROWS, bm)
    affine = res_affine is not None

    def body(a_ref, b_ref, r_ref, g_ref, be_ref, *rest):
        if affine:
            rg_ref, rb_ref, yb_ref, xh_ref, rs_ref, acc_ref = rest
        else:
            yb_ref, xh_ref, rs_ref, acc_ref = rest
        j = pl.program_id(1)
        acc_ref[j] = jnp.dot(a_ref[...], b_ref[...], preferred_element_type=F32)

        @pl.when(j == nj - 1)
        def _():
            def chunk(ci, carry):
                rows = pl.ds(pl.multiple_of(ci * ch, ch), ch)
                r = r_ref[rows, :]
                if affine:
                    r = r * rg_ref[...] + rb_ref[...]
                z = res_scale * r + mm_scale * _full_rows(acc_ref, rows, nj)
                mu = jnp.mean(z, axis=-1, keepdims=True)
                zc = z - mu
                var = jnp.mean(zc * zc, axis=-1, keepdims=True)
                rstd = lax.rsqrt(var + LN_EPS)
                xh = zc * rstd
                y = xh * g_ref[...] + be_ref[...]
                yb_ref[rows, :] = y.astype(BF16)
                xh_ref[rows, :] = xh
                rs_ref[rows, :] = rstd
                return carry

            lax.fori_loop(0, bm // ch, chunk, 0)

    row = pl.BlockSpec((bm, d), lambda i, j: (i, 0))
    vec = pl.BlockSpec((1, d), lambda i, j: (0, 0))
    return _call(
        body, name=name,
        out_shape=(jax.ShapeDtypeStruct((s, d), BF16), jax.ShapeDtypeStruct((s, d), F32),
                   jax.ShapeDtypeStruct((s, 1), F32)),
        grid=(s // bm, nj),
        in_specs=[pl.BlockSpec((bm, k), lambda i, j: (i, 0)),
                  pl.BlockSpec((k, bn), lambda i, j: (0, j)), row, vec, vec] + ([vec, vec] if affine else []),
        out_specs=(row, row, pl.BlockSpec((bm, 1), lambda i, j: (i, 0))),
        scratch_shapes=[pltpu.VMEM((nj, bm, bn), F32)],
        args=(a, b, res, gamma, beta) + (tuple(res_affine) if affine else ()), vmem_mib=58, comm=comm)


def _mm_dx(a, wt, extra, xhat, rstd, gamma, *, extra_scale, bm, bn, name, tb=False, comm=None):
    s, k = a.shape
    d = wt.shape[0] if tb else wt.shape[1]
    bm, bn = min(bm, s), min(bn, d)
    assert s % bm == 0 and d % bn == 0
    nj = d // bn
    ch = min(EPILOGUE_ROWS, bm)
    dims = NT_DIMS if tb else (((1,), (0,)), ((), ()))

    def body(a_ref, w_ref, e_ref, xh_ref, rs_ref, g_ref, dz_ref, dzb_ref, dg_ref, db_ref, acc_ref):
        i = pl.program_id(0)
        j = pl.program_id(1)
        acc_ref[j] = lax.dot_general(a_ref[...], w_ref[...], dims, preferred_element_type=F32)

        @pl.when(j == nj - 1)
        def _():
            def chunk(ci, carry):
                dgp, dbp = carry
                rows = pl.ds(pl.multiple_of(ci * ch, ch), ch)
                dx = extra_scale * e_ref[rows, :] + _full_rows(acc_ref, rows, nj)
                xh = xh_ref[rows, :]
                dxh = dx * g_ref[...]
                m1 = jnp.mean(dxh, axis=-1, keepdims=True)
                m2 = jnp.mean(dxh * xh, axis=-1, keepdims=True)
                dz = rs_ref[rows, :] * (dxh - m1 - xh * m2)
                dz_ref[rows, :] = dz
                dzb_ref[rows, :] = dz.astype(BF16)
                return dgp + jnp.sum(dx * xh, axis=0, keepdims=True), dbp + jnp.sum(dx, axis=0, keepdims=True)

            zero = jnp.zeros((1, d), F32)
            dgp, dbp = lax.fori_loop(0, bm // ch, chunk, (zero, zero))

            @pl.when(i == 0)
            def _():
                dg_ref[...] = dgp
                db_ref[...] = dbp

            @pl.when(i > 0)
            def _():
                dg_ref[...] += dgp
                db_ref[...] += dbp

    row = pl.BlockSpec((bm, d), lambda i, j: (i, 0))
    vec = pl.BlockSpec((1, d), lambda i, j: (0, 0))
    return _call(
        body, name=name,
        out_shape=(jax.ShapeDtypeStruct((s, d), F32), jax.ShapeDtypeStruct((s, d), BF16),
                   jax.ShapeDtypeStruct((1, d), F32), jax.ShapeDtypeStruct((1, d), F32)),
        grid=(s // bm, nj),
        in_specs=[pl.BlockSpec((bm, k), lambda i, j: (i, 0)),
                  pl.BlockSpec((bn, k), lambda i, j: (j, 0)) if tb else pl.BlockSpec((k, bn), lambda i, j: (0, j)),
                  row, row, pl.BlockSpec((bm, 1), lambda i, j: (i, 0)), vec],
        out_specs=(row, row, vec, vec),
        scratch_shapes=[pltpu.VMEM((nj, bm, bn), F32)],
        args=(a, wt, extra, xhat, rstd, gamma), vmem_mib=58, comm=comm)


def _ffn_dx(dg, du, wgt, wut, extra, *, extra_scale, bm, bn, name, comm=None):
    s, f = dg.shape
    d = wgt.shape[1]
    bm, bn = min(bm, s), min(bn, d)
    assert s % bm == 0 and d % bn == 0 and f % (2 * LANES) == 0
    nj, half = d // bn, f // 2

    def body(dg_ref, du_ref, wg_ref, wu_ref, e_ref, o_ref, acc_ref):
        kh, j = pl.program_id(1), pl.program_id(2)
        part = jnp.dot(dg_ref[...], wg_ref[...], preferred_element_type=F32)
        part = part + jnp.dot(du_ref[...], wu_ref[...], preferred_element_type=F32)

        @pl.when(kh == 0)
        def _():
            acc_ref[j] = part

        @pl.when(kh == 1)
        def _():
            o_ref[...] = extra_scale * e_ref[...] + (acc_ref[j] + part)

    rows = pl.BlockSpec((bm, half), lambda i, kh, j: (i, kh))
    cols = pl.BlockSpec((half, bn), lambda i, kh, j: (kh, j))
    blk = pl.BlockSpec((bm, bn), lambda i, kh, j: (i, j * kh))
    return _call(
        body, name=name, out_shape=jax.ShapeDtypeStruct((s, d), F32),
        grid=(s // bm, 2, nj), in_specs=[rows, rows, cols, cols, blk], out_specs=blk,
        scratch_shapes=[pltpu.VMEM((nj, bm, bn), F32)],
        args=(dg, du, wgt, wut, extra), vmem_mib=58, comm=comm)


def _ln_bwd(dx, xhat, rstd, gamma, *, bm, name):
    s, d = dx.shape
    bm = min(bm, s)
    assert s % bm == 0
    ch = min(EPILOGUE_ROWS, bm)

    def body(dx_ref, xh_ref, rs_ref, g_ref, dz_ref, dzb_ref, dg_ref, db_ref):
        def chunk(ci, carry):
            dgp, dbp = carry
            rows = pl.ds(pl.multiple_of(ci * ch, ch), ch)
            dxv = dx_ref[rows, :]
            xh = xh_ref[rows, :]
            dxh = dxv * g_ref[...]
            m1 = jnp.mean(dxh, axis=-1, keepdims=True)
            m2 = jnp.mean(dxh * xh, axis=-1, keepdims=True)
            dz = rs_ref[rows, :] * (dxh - m1 - xh * m2)
            dz_ref[rows, :] = dz
            dzb_ref[rows, :] = dz.astype(BF16)
            return dgp + jnp.sum(dxv * xh, axis=0, keepdims=True), dbp + jnp.sum(dxv, axis=0, keepdims=True)

        zero = jnp.zeros((1, d), F32)
        dgp, dbp = lax.fori_loop(0, bm // ch, chunk, (zero, zero))
        i = pl.program_id(0)

        @pl.when(i == 0)
        def _():
            dg_ref[...] = dgp
            db_ref[...] = dbp

        @pl.when(i > 0)
        def _():
            dg_ref[...] += dgp
            db_ref[...] += dbp

    row = pl.BlockSpec((bm, d), lambda i: (i, 0))
    vec = pl.BlockSpec((1, d), lambda i: (0, 0))
    return _call(
        body, name=name,
        out_shape=(jax.ShapeDtypeStruct((s, d), F32), jax.ShapeDtypeStruct((s, d), BF16),
                   jax.ShapeDtypeStruct((1, d), F32), jax.ShapeDtypeStruct((1, d), F32)),
        grid=(s // bm,), in_specs=[row, row, pl.BlockSpec((bm, 1), lambda i: (i, 0)), vec],
        out_specs=(row, row, vec, vec), args=(dx, xhat, rstd, gamma), vmem_mib=48)


def _to_bf16(x, *, bm, name, comm=None):
    s, d = x.shape
    bm = min(bm, s)
    assert s % bm == 0

    def body(x_ref, o_ref):
        o_ref[...] = x_ref[...].astype(BF16)

    row = pl.BlockSpec((bm, d), lambda i: (i, 0))
    return _call(body, name=name, out_shape=jax.ShapeDtypeStruct((s, d), BF16), grid=(s // bm,),
                 in_specs=[row], out_specs=row, args=(x,), vmem_mib=48, comm=comm)


def _ple_loss(xh3, gamma, beta, x3b, p, wpg, wpp, target, *, bm, bn, name):
    s, d = xh3.shape
    dp = p.shape[1]
    bm, bn = min(bm, s), min(bn, d)
    assert s % bm == 0 and d % bn == 0
    inv_d = 1.0 / d
    chunks = 4 if bm % 64 == 0 else 1
    cr = bm // chunks

    def body(x_ref, g_ref, b_ref, xb_ref, p_ref, wg_ref, wp_ref, t_ref, l_ref, dy_ref, dg_ref, dp_ref):
        first = (pl.program_id(0) == 0) & (pl.program_id(1) == 0)

        @pl.when(first)
        def _():
            l_ref[...] = jnp.zeros_like(l_ref)

        part = 0.0
        for r in range(chunks):
            rows = slice(r * cr, (r + 1) * cr)
            gp = jnp.dot(xb_ref[rows, :], wg_ref[...], preferred_element_type=F32)
            pp = lax.dot_general(p_ref[rows, :].astype(BF16), wp_ref[...], NT_DIMS, preferred_element_type=F32)
            sig = jax.nn.sigmoid(gp)
            err = (x_ref[rows, :] * g_ref[...] + b_ref[...]) + sig * pp - t_ref[rows, :]
            part = part + jnp.sum(err * err)
            dy = err * inv_d
            dy_ref[rows, :] = dy
            dg_ref[rows, :] = (dy * pp * sig * (1.0 - sig)).astype(BF16)
            dp_ref[rows, :] = (dy * sig).astype(BF16)
        l_ref[...] += part

    blk = pl.BlockSpec((bm, bn), lambda i, j: (i, j))
    return pl.pallas_call(
        body, name=name,
        out_shape=(jax.ShapeDtypeStruct((8, LANES), F32), jax.ShapeDtypeStruct((s, d), F32),
                   jax.ShapeDtypeStruct((s, d), BF16), jax.ShapeDtypeStruct((s, d), BF16)),
        grid=(s // bm, d // bn),
        in_specs=[blk, pl.BlockSpec((1, bn), lambda i, j: (0, j)), pl.BlockSpec((1, bn), lambda i, j: (0, j)),
                  pl.BlockSpec((bm, d), lambda i, j: (i, 0)), pl.BlockSpec((bm, dp), lambda i, j: (i, 0)),
                  pl.BlockSpec((d, bn), lambda i, j: (0, j)), pl.BlockSpec((bn, dp), lambda i, j: (j, 0)), blk],
        out_specs=(pl.BlockSpec((8, LANES), lambda i, j: (0, 0)), blk, blk, blk),
        compiler_params=_cp(("arbitrary", "arbitrary"), 56),
    )(xh3, gamma, beta, x3b, p, wpg, wpp, target)


def _rope_tables(positions):
    half = ROT_DIMS // 2
    lane = jnp.arange(HEAD_DIM)
    inv_freq = jnp.power(jnp.float32(ROPE_THETA), -(lane % half).astype(F32) * (2.0 / ROT_DIMS))
    ang = positions.astype(F32)[:, None] * inv_freq
    cos, sin = jnp.cos(ang), jnp.sin(ang)
    cf = jnp.where(lane < ROT_DIMS, cos, 1.0)
    sa = jnp.where(lane < half, -sin, 0.0)
    sb = jnp.where((lane >= half) & (lane < ROT_DIMS), sin, 0.0)
    return cf, sa, sb


def _rotary(t, tabs, *, n_cols, inverse, out_dtype, bs, name):
    s = t.shape[0]
    bs = min(bs, s)
    half = ROT_DIMS // 2
    heads = N_KV_HEADS
    assert n_cols % heads == 0

    def body(t_ref, cf_ref, sa_ref, sb_ref, o_ref):
        cf, sa, sb = cf_ref[...], sa_ref[...], sb_ref[...]
        for hd in range(heads):
            lanes = slice(hd * HEAD_DIM, (hd + 1) * HEAD_DIM)
            v = t_ref[:, lanes]
            if inverse:
                o = v * cf + pltpu.roll(v * sa, half, 1) + pltpu.roll(v * sb, HEAD_DIM - half, 1)
            else:
                o = v * cf + pltpu.roll(v, HEAD_DIM - half, 1) * sa + pltpu.roll(v, half, 1) * sb
            o_ref[:, lanes] = o.astype(out_dtype)

    blk = pl.BlockSpec((bs, heads * HEAD_DIM), lambda i, j: (i, j))
    tab = pl.BlockSpec((bs, HEAD_DIM), lambda i, j: (i, 0))
    return pl.pallas_call(
        body, name=name, out_shape=jax.ShapeDtypeStruct((s, n_cols * HEAD_DIM), out_dtype),
        grid=(s // bs, n_cols // heads), in_specs=[blk, tab, tab, tab], out_specs=blk,
        compiler_params=_cp(("parallel", "arbitrary"), 32),
    )(t, *tabs)


def _attn_blocks():
    out = []
    for g, dil in enumerate(DILATIONS):
        sup = SPAN * dil
        for j in range(ATTN_TILE // sup):
            for r in range(dil):
                out.append((g, j * sup + r, dil, (j - 1) * sup + r if j > 0 else None, ATTN_TILE - sup + r))
    return out


def _rows(ref, start, dil, lead=None):
    idx = pl.ds(start, SPAN, stride=dil) if dil > 1 else pl.ds(start, SPAN)
    return ref[idx, :] if lead is None else ref[lead, idx, :]


def _band_masks(n):
    qi = lax.broadcasted_iota(jnp.int32, (SPAN, 2 * SPAN), 0)
    ki = lax.broadcasted_iota(jnp.int32, (SPAN, 2 * SPAN), 1)
    band = (ki >= qi) & (ki <= qi + SPAN)
    return band, band & ((ki >= SPAN) | (n > 0))


def _attn_fwd(qkr, proj, *, name):
    s = qkr.shape[0]
    t = ATTN_TILE
    assert s % t == 0
    nt = s // t
    scale = HEAD_DIM ** -0.5
    kcol, vcol = N_PATTERNS * N_KV_HEADS, (N_PATTERNS + 1) * N_KV_HEADS
    blocks = _attn_blocks()

    def body(q0, q1, q2, kc_ref, kp_ref, vc_ref, vp_ref, o_ref, l_ref, og, lg):
        n = pl.program_id(1)
        band, band_first = _band_masks(n)
        q_refs = (q0, q1, q2)
        for g, start, dil, prev_in_tile, prev_start in blocks:
            q = _rows(q_refs[g], start, dil).astype(BF16)
            if prev_in_tile is not None:
                kp, vp, mask = _rows(kc_ref, prev_in_tile, dil), _rows(vc_ref, prev_in_tile, dil), band
            else:
                kp, vp, mask = _rows(kp_ref, prev_start, dil), _rows(vp_ref, prev_start, dil), band_first
            kk = jnp.concatenate([kp, _rows(kc_ref, start, dil)], axis=0).astype(BF16)
            vv = jnp.concatenate([vp, _rows(vc_ref, start, dil)], axis=0).astype(BF16)
            sc = lax.dot_general(q, kk, (((1,), (1,)), ((), ())), preferred_element_type=F32) * scale
            sc = jnp.where(mask, sc, -1e30)
            m = jnp.max(sc, axis=-1, keepdims=True)
            e = jnp.exp(sc - m)
            den = jnp.sum(e, axis=-1, keepdims=True)
            o = jnp.dot(e.astype(BF16), vv, preferred_element_type=F32) / den
            idx = pl.ds(start, SPAN, stride=dil) if dil > 1 else pl.ds(start, SPAN)
            og[g, idx, :] = o
            lg[g, idx, :] = jnp.broadcast_to(m + jnp.log(den), (SPAN, HEAD_DIM))
        l0, l1, l2 = lg[0], lg[1], lg[2]
        m = jnp.maximum(jnp.maximum(l0, l1), l2)
        w0, w1, w2 = jnp.exp(l0 - m), jnp.exp(l1 - m), jnp.exp(l2 - m)
        den = w0 + w1 + w2
        o_ref[...] = (w0 * og[0] + w1 * og[1] + w2 * og[2]) / den
        l_ref[...] = m + jnp.log(den)

    def col(c, prev=False):
        if prev:
            return pl.BlockSpec((t, HEAD_DIM), lambda h, n: (jnp.maximum(n - 1, 0), c + h))
        return pl.BlockSpec((t, HEAD_DIM), lambda h, n: (n, c + h))

    out = jax.ShapeDtypeStruct((s, N_KV_HEADS * HEAD_DIM), F32)
    return pl.pallas_call(
        body, name=name, out_shape=(out, out),
        grid=(N_KV_HEADS, nt),
        in_specs=[col(0), col(N_KV_HEADS), col(2 * N_KV_HEADS), col(kcol), col(kcol, True), col(vcol), col(vcol, True)],
        out_specs=(col(0), col(0)),
        scratch_shapes=[pltpu.VMEM((N_PATTERNS, t, HEAD_DIM), F32), pltpu.VMEM((N_PATTERNS, t, HEAD_DIM), F32)],
        compiler_params=_cp(("parallel", "arbitrary"), 48),
    )(qkr, qkr, qkr, qkr, qkr, proj, proj)


def _attn_bwd(qkr, proj, attn, lse, dcat, *, name, comm=None):
    s = qkr.shape[0]
    t = ATTN_TILE
    nt = s // t
    scale = HEAD_DIM ** -0.5
    kcol, vcol = N_PATTERNS * N_KV_HEADS, (N_PATTERNS + 1) * N_KV_HEADS
    blocks = _attn_blocks()

    def body(q0, q1, q2, kc_ref, kp_ref, vc_ref, vp_ref, o_ref, l_ref, do_ref,
             dq0, dq1, dq2, dk_ref, dv_ref, ck, cv, tkc, tvc, tkp, tvp):
        n = pl.program_id(1)
        for ref in (tkc, tvc, tkp, tvp):
            ref[...] = jnp.zeros_like(ref)

        @pl.when(n < nt)
        def _():
            band, band_first = _band_masks(n)
            q_refs, dq_refs = (q0, q1, q2), (dq0, dq1, dq2)
            for g, start, dil, prev_in_tile, prev_start in blocks:
                idx = pl.ds(start, SPAN, stride=dil) if dil > 1 else pl.ds(start, SPAN)
                q = q_refs[g][idx, :].astype(BF16)
                if prev_in_tile is not None:
                    kp, vp, mask = _rows(kc_ref, prev_in_tile, dil), _rows(vc_ref, prev_in_tile, dil), band
                else:
                    kp, vp, mask = _rows(kp_ref, prev_start, dil), _rows(vp_ref, prev_start, dil), band_first
                kk = jnp.concatenate([kp, kc_ref[idx, :]], axis=0).astype(BF16)
                vv = jnp.concatenate([vp, vc_ref[idx, :]], axis=0).astype(BF16)
                do = do_ref[idx, :]
                dsum = jnp.sum(do * o_ref[idx, :], axis=-1, keepdims=True)
                lrow = l_ref[idx, :][:, :1]
                dob = do.astype(BF16)
                sc = lax.dot_general(q, kk, (((1,), (1,)), ((), ())), preferred_element_type=F32) * scale
                p = jnp.where(mask, jnp.exp(sc - lrow), 0.0)
                dp = lax.dot_general(dob, vv, (((1,), (1,)), ((), ())), preferred_element_type=F32)
                ds = (p * (dp - dsum) * scale).astype(BF16)
                pb = p.astype(BF16)
                dq_refs[g][idx, :] = jnp.dot(ds, kk, preferred_element_type=F32)
                dkk = lax.dot_general(ds, q, (((0,), (0,)), ((), ())), preferred_element_type=F32)
                dvv = lax.dot_general(pb, dob, (((0,), (0,)), ((), ())), preferred_element_type=F32)
                tkc[idx, :] += dkk[SPAN:]
                tvc[idx, :] += dvv[SPAN:]
                if prev_in_tile is not None:
                    pidx = pl.ds(prev_in_tile, SPAN, stride=dil) if dil > 1 else pl.ds(prev_in_tile, SPAN)
                    tkc[pidx, :] += dkk[:SPAN]
                    tvc[pidx, :] += dvv[:SPAN]
                else:
                    pidx = pl.ds(prev_start, SPAN, stride=dil) if dil > 1 else pl.ds(prev_start, SPAN)
                    tkp[pidx, :] += dkk[:SPAN]
                    tvp[pidx, :] += dvv[:SPAN]

        @pl.when(n > 0)
        def _():
            dk_ref[...] = ck[...] + tkp[...]
            dv_ref[...] = (cv[...] + tvp[...]).astype(BF16)

        ck[...] = tkc[...]
        cv[...] = tvc[...]

    def col(c, prev=False):
        if prev:
            return pl.BlockSpec((t, HEAD_DIM), lambda h, n: (jnp.maximum(jnp.minimum(n, nt - 1) - 1, 0), c + h))
        return pl.BlockSpec((t, HEAD_DIM), lambda h, n: (jnp.minimum(n, nt - 1), c + h))

    kv_out = pl.BlockSpec((t, HEAD_DIM), lambda h, n: (jnp.maximum(n - 1, 0), h))
    tile = pltpu.VMEM((t, HEAD_DIM), F32)
    per_head = jax.ShapeDtypeStruct((s, N_KV_HEADS * HEAD_DIM), F32)
    return _call(
        body, name=name,
        out_shape=(per_head, per_head, per_head, per_head, jax.ShapeDtypeStruct((s, N_KV_HEADS * HEAD_DIM), BF16)),
        grid=(N_KV_HEADS, nt + 1),
        in_specs=[col(0), col(N_KV_HEADS), col(2 * N_KV_HEADS), col(kcol), col(kcol, True), col(vcol), col(vcol, True),
                  col(0), col(0), col(0)],
        out_specs=(col(0), col(0), col(0), kv_out, kv_out),
        scratch_shapes=[tile] * 6,
        args=(qkr, qkr, qkr, qkr, qkr, proj, proj, attn, lse, dcat), vmem_mib=48, comm=comm)


GELU_C0 = 0.7978845608028654
GELU_C1 = 0.044715


def _softplus_neg(lam):
    y = jnp.exp(-jnp.abs(lam))
    w = 1.0 + y
    log1p = jnp.where(w == 1.0, y, jnp.log(w) * (y / jnp.where(w == 1.0, 1.0, w - 1.0)))
    return jnp.maximum(-lam, 0.0) + log1p


def _down(cur, prev, k, row):
    if k == 0:
        return cur
    return jnp.where(row < k, pltpu.roll(prev, k, 0), pltpu.roll(cur, k, 0))


def _up(cur, nxt, k, row, tt):
    if k == 0:
        return cur
    return jnp.where(row >= tt - k, pltpu.roll(nxt, tt - k, 0), pltpu.roll(cur, tt - k, 0))


def _lru_gates(x, xp, cw, cb, wr, br, wi, bi, lam, row):
    shifts = [_down(x, xp, k, row) for k in range(CONV_WIDTH)]
    xc = cb
    for j in range(CONV_WIDTH):
        xc = xc + cw[j:j + 1, :] * shifts[CONV_WIDTH - 1 - j]
    xcb = xc.astype(BF16)
    r = jax.nn.sigmoid(jnp.dot(xcb, wr, preferred_element_type=F32) + br)
    i = jax.nn.sigmoid(jnp.dot(xcb, wi, preferred_element_type=F32) + bi)
    c = -LRU_C * _softplus_neg(lam)
    la = c * r
    a = jnp.exp(la)
    mult = jnp.sqrt(jnp.tanh(-la) * (a * a + 1.0))
    return shifts, xc, xcb, r, i, c, a, mult


def _lru_fwd(proj, cw, cb, wr, br, wi, bi, lam, *, tt, name):
    s = proj.shape[0]
    nblk = wr.shape[0]
    c = nblk * LANES
    tt = min(tt, s)
    xcol0 = (N_PATTERNS + 2) * N_KV_HEADS
    ycol0 = xcol0 + nblk

    def body(x_ref, y_ref, cw_ref, cb_ref, wr_ref, br_ref, wi_ref, bi_ref, lam_ref, rec_ref, h_ref, xprev, hc):
        n = pl.program_id(1)

        @pl.when(n == 0)
        def _():
            xprev[...] = jnp.zeros_like(xprev)
            hc[...] = jnp.zeros_like(hc)

        row = lax.broadcasted_iota(jnp.int32, (tt, LANES), 0)
        x = x_ref[...]
        _, xc, _, _, i, _, a, mult = _lru_gates(
            x, xprev[...], cw_ref[...], cb_ref[...], wr_ref[0].astype(BF16), br_ref[...],
            wi_ref[0].astype(BF16), bi_ref[...], lam_ref[...], row)
        av, bv = a, mult * (i * xc)
        k = 1
        while k < tt:
            bs = jnp.where(row < k, 0.0, pltpu.roll(bv, k, 0))
            as_ = jnp.where(row < k, 1.0, pltpu.roll(av, k, 0))
            bv = bv + av * bs
            av = av * as_
            k *= 2
        h = bv + av * hc[0:1, :]
        hc[...] = jnp.broadcast_to(h[tt - 1:tt, :], hc.shape)
        h_ref[...] = h
        y = y_ref[...]
        gel = 0.5 * y * (1.0 + jnp.tanh(GELU_C0 * (y + GELU_C1 * y * y * y)))
        rec_ref[...] = (h * gel).astype(BF16)
        xprev[...] = x

    vec = pl.BlockSpec((1, LANES), lambda b, n: (0, b))
    wblk = pl.BlockSpec((1, LANES, LANES), lambda b, n: (b, 0, 0))
    out = pl.BlockSpec((tt, LANES), lambda b, n: (n, b))
    return pl.pallas_call(
        body, name=name,
        out_shape=(jax.ShapeDtypeStruct((s, c), BF16), jax.ShapeDtypeStruct((s, c), F32)),
        grid=(nblk, s // tt),
        in_specs=[pl.BlockSpec((tt, LANES), lambda b, n: (n, xcol0 + b)),
                  pl.BlockSpec((tt, LANES), lambda b, n: (n, ycol0 + b)),
                  pl.BlockSpec((CONV_WIDTH, LANES), lambda b, n: (0, b)), vec, wblk, vec, wblk, vec, vec],
        out_specs=(out, out),
        scratch_shapes=[pltpu.VMEM((tt, LANES), F32), pltpu.VMEM((8, LANES), F32)],
        compiler_params=_cp(("parallel", "arbitrary"), 32),
    )(proj, proj, cw, cb, wr, br, wi, bi, lam)


def _lru_bwd(proj, hseq, dcat, cw, cb, wr, br, wi, bi, lam, *, tt, name, comm=None):
    s = proj.shape[0]
    nblk = wr.shape[0]
    c = nblk * LANES
    tt = min(tt, s)
    nt = s // tt
    xcol0 = (N_PATTERNS + 2) * N_KV_HEADS
    ycol0 = xcol0 + nblk
    rcol0 = N_KV_HEADS

    def body(x_ref, xp_ref, y_ref, h_ref, hp_ref, dr_ref, cw_ref, cb_ref, wr_ref, br_ref, wi_ref, bi_ref, lam_ref,
             dx_ref, dy_ref, dcw_ref, dcb_ref, dwr_ref, dbr_ref, dwi_ref, dbi_ref, dlam_ref, dxc_next, gcar, acar):
        n = pl.program_id(1)
        rt = nt - 1 - n

        @pl.when(n == 0)
        def _():
            for ref in (dxc_next, gcar, acar, dcw_ref, dcb_ref, dwr_ref, dbr_ref, dwi_ref, dbi_ref, dlam_ref):
                ref[...] = jnp.zeros_like(ref)

        row = lax.broadcasted_iota(jnp.int32, (tt, LANES), 0)
        x = x_ref[...]
        xp = jnp.where(rt > 0, xp_ref[...], 0.0)
        cwv = cw_ref[...]
        wrb, wib = wr_ref[0].astype(BF16), wi_ref[0].astype(BF16)
        lam_v = lam_ref[...]
        shifts, xc, xcb, r, i, cc, a, mult = _lru_gates(x, xp, cwv, cb_ref[...], wrb, br_ref[...], wib, bi_ref[...],
                                                        lam_v, row)
        h = h_ref[...]
        hp_last = jnp.where(rt > 0, hp_ref[7:8, :], 0.0)
        hprev = jnp.where(row < 1, hp_last, pltpu.roll(h, 1, 0))
        y = y_ref[...]
        y2 = y * y
        th = jnp.tanh(GELU_C0 * (y + GELU_C1 * y2 * y))
        gel = 0.5 * y * (1.0 + th)
        dgel = 0.5 * (1.0 + th) + 0.5 * y * (1.0 - th * th) * GELU_C0 * (1.0 + 3.0 * GELU_C1 * y2)
        drec = dr_ref[...]
        dy_ref[...] = (drec * h * dgel).astype(BF16)
        av = jnp.where(row >= tt - 1, acar[0:1, :], pltpu.roll(a, tt - 1, 0))
        bv = drec * gel
        k = 1
        while k < tt:
            bs = jnp.where(row >= tt - k, 0.0, pltpu.roll(bv, tt - k, 0))
            as_ = jnp.where(row >= tt - k, 1.0, pltpu.roll(av, tt - k, 0))
            bv = bv + av * bs
            av = av * as_
            k *= 2
        g = bv + av * gcar[0:1, :]
        gcar[...] = jnp.broadcast_to(g[0:1, :], gcar.shape)
        acar[...] = jnp.broadcast_to(a[0:1, :], acar.shape)
        da = g * hprev
        d_ixc = g * mult
        dmult = g * (i * xc)
        di = d_ixc * xc
        dxc = d_ixc * i
        a2 = a * a
        dla = da * a - dmult * (a2 / mult)
        dr = dla * cc
        dsp = jnp.sum(dla * r, axis=0, keepdims=True) * (-LRU_C)
        dlam_ref[...] += dsp * (-jax.nn.sigmoid(-lam_v))
        dzr = dr * r * (1.0 - r)
        dzi = di * i * (1.0 - i)
        dbr_ref[...] += jnp.sum(dzr, axis=0, keepdims=True)
        dbi_ref[...] += jnp.sum(dzi, axis=0, keepdims=True)
        dzrb, dzib = dzr.astype(BF16), dzi.astype(BF16)
        tn = (((0,), (0,)), ((), ()))
        ntd = (((1,), (1,)), ((), ()))
        dwr_ref[0] += lax.dot_general(xcb, dzrb, tn, preferred_element_type=F32)
        dwi_ref[0] += lax.dot_general(xcb, dzib, tn, preferred_element_type=F32)
        dxc = (dxc + lax.dot_general(dzrb, wrb, ntd, preferred_element_type=F32)
               + lax.dot_general(dzib, wib, ntd, preferred_element_type=F32))
        dcb_ref[...] += jnp.sum(dxc, axis=0, keepdims=True)
        dcw_ref[...] += jnp.concatenate(
            [jnp.sum(dxc * shifts[CONV_WIDTH - 1 - j], axis=0, keepdims=True) for j in range(CONV_WIDTH)], axis=0)
        nxt = dxc_next[...]
        dx = cwv[0:1, :] * _up(dxc, nxt, CONV_WIDTH - 1, row, tt)
        for j in range(1, CONV_WIDTH):
            dx = dx + cwv[j:j + 1, :] * _up(dxc, nxt, CONV_WIDTH - 1 - j, row, tt)
        dx_ref[...] = dx.astype(BF16)
        dxc_next[...] = dxc

    def tile(col0, prev=False):
        if prev:
            return pl.BlockSpec((tt, LANES), lambda b, n: (jnp.maximum(nt - 2 - n, 0), col0 + b))
        return pl.BlockSpec((tt, LANES), lambda b, n: (nt - 1 - n, col0 + b))

    vec = pl.BlockSpec((1, LANES), lambda b, n: (0, b))
    wblk = pl.BlockSpec((1, LANES, LANES), lambda b, n: (b, 0, 0))
    cwblk = pl.BlockSpec((CONV_WIDTH, LANES), lambda b, n: (0, b))
    hp8 = pl.BlockSpec((8, LANES), lambda b, n: (jnp.maximum((nt - 1 - n) * (tt // 8) - 1, 0), b))
    vshape = jax.ShapeDtypeStruct((1, c), F32)
    wshape = jax.ShapeDtypeStruct((nblk, LANES, LANES), F32)
    return _call(
        body, name=name,
        out_shape=(jax.ShapeDtypeStruct((s, c), BF16), jax.ShapeDtypeStruct((s, c), BF16),
                   jax.ShapeDtypeStruct((CONV_WIDTH, c), F32), vshape, wshape, vshape, wshape, vshape, vshape),
        grid=(nblk, nt),
        in_specs=[tile(xcol0), tile(xcol0, True), tile(ycol0), tile(0), hp8, tile(rcol0),
                  cwblk, vec, wblk, vec, wblk, vec, vec],
        out_specs=(tile(0), tile(0), cwblk, vec, wblk, vec, wblk, vec, vec),
        scratch_shapes=[pltpu.VMEM((tt, LANES), F32), pltpu.VMEM((8, LANES), F32), pltpu.VMEM((8, LANES), F32)],
        args=(proj, proj, proj, hseq, hseq, dcat, cw, cb, wr, br, wi, bi, lam), vmem_mib=32, comm=comm)


ROW_BLOCKS = (512, 256, 176, 128, 64, 32, 16, 8)


def _adamw(w, m, v, gparts, *, name):
    r, c = w.shape
    npart = gparts.shape[0]
    br = _pick(r, ROW_BLOCKS)
    c1 = 1.0 - ADAM_B1 ** ADAM_STEP
    c2 = 1.0 - ADAM_B2 ** ADAM_STEP

    def body(w_ref, m_ref, v_ref, g_ref, go_ref, d_ref, mo_ref, vo_ref):
        g = g_ref[0].astype(F32)
        for q in range(1, npart):
            g = g + g_ref[q].astype(F32)
        mn = ADAM_B1 * m_ref[...] + (1.0 - ADAM_B1) * g
        vn = ADAM_B2 * v_ref[...] + (1.0 - ADAM_B2) * (g * g)
        go_ref[...] = g
        mo_ref[...] = mn
        vo_ref[...] = vn
        d_ref[...] = -ADAM_LR * ((mn / c1) / (jnp.sqrt(vn / c2) + ADAM_EPS) + ADAM_WD * w_ref[...])

    blk = pl.BlockSpec((br, c), lambda i: (i, 0))
    out = jax.ShapeDtypeStruct((r, c), F32)
    return pl.pallas_call(
        body, name=name, out_shape=(out, out, out, out), grid=(r // br,),
        in_specs=[blk, blk, blk, pl.BlockSpec((npart, br, c), lambda i: (0, i, 0))],
        out_specs=(blk, blk, blk, blk),
        compiler_params=_cp(("parallel",), 48),
    )(w, m, v, gparts)


def _sum_parts(parts, *, name):
    npart, r, c = parts.shape
    br = next((b for b in range(min(r, 2048) // 8 * 8, 0, -8) if r % b == 0), r)

    def body(p_ref, o_ref):
        acc = p_ref[0]
        for q in range(1, npart):
            acc = acc + p_ref[q]
        o_ref[...] = acc

    return pl.pallas_call(
        body, name=name, out_shape=jax.ShapeDtypeStruct((r, c), F32), grid=(r // br,),
        in_specs=[pl.BlockSpec((npart, br, c), lambda i: (0, i, 0))],
        out_specs=pl.BlockSpec((br, c), lambda i: (i, 0)),
        compiler_params=_cp(("parallel",), 48),
    )(parts)


HBM = pl.BlockSpec(memory_space=pltpu.HBM)


def _mesh_pos():
    return lax.axis_index("x"), lax.axis_index("y"), lax.axis_index("c")


def _gather_comm(shards):
    na = len(shards)

    def parts(x_refs, out_refs, sems):
        send_sems, recv_sems, local_sems = sems
        x, y, c = _mesh_pos()
        me, sibling = (x, y, c), (x, y, 1 - c)
        chips = [(1 - x, y), (x, 1 - y), (1 - x, 1 - y)]

        def copy(a, k, block, to, src=None):
            px, py, pc = block
            dst = out_refs[a].at[4 * px + 2 * py + pc]
            return pltpu.make_async_remote_copy(
                src_ref=dst if src is None else src, dst_ref=dst,
                send_sem=send_sems.at[a, k], recv_sem=recv_sems.at[a, k],
                device_id=to, device_id_type=MESH)

        def mine(a):
            return pltpu.make_async_copy(x_refs[a], out_refs[a].at[4 * x + 2 * y + c], local_sems.at[a])

        def first(a):
            return [copy(a, 0, me, sibling, src=x_refs[a])] + [
                copy(a, 1 + j, me, (*chip, c), src=x_refs[a]) for j, chip in enumerate(chips)]

        def passed(a, j):
            return copy(a, 4 + j, (*chips[j], c), sibling)

        return me, sibling, chips, c, copy, mine, first, passed

    def start(x_refs, out_refs, sems):
        *_, mine, first, _ = parts(x_refs, out_refs, sems)
        for a in range(na):
            mine(a).start()
            for cp in first(a):
                cp.start()

    def mid(x_refs, out_refs, sems):
        me, _, chips, c, copy, _, _, passed = parts(x_refs, out_refs, sems)
        for j, chip in enumerate(chips):
            for a in range(na):
                copy(a, 1 + j, (*chip, c), me).wait_recv()
                passed(a, j).start()

    def end(x_refs, out_refs, sems):
        me, sibling, chips, c, copy, mine, first, passed = parts(x_refs, out_refs, sems)
        for a in range(na):
            copy(a, 0, sibling, me).wait_recv()
            for j, chip in enumerate(chips):
                copy(a, 4 + j, (*chip, 1 - c), me).wait_recv()
        for a in range(na):
            for cp in first(a) + [passed(a, j) for j in range(3)]:
                cp.wait_send()
            mine(a).wait()

    return _Comm(
        shards, [jax.ShapeDtypeStruct((N_DEV,) + a.shape, a.dtype) for a in shards],
        [pltpu.SemaphoreType.DMA((na, 7)), pltpu.SemaphoreType.DMA((na, 7)), pltpu.SemaphoreType.DMA((na,))],
        start, end, mid)


def _scatter_comm(g8s):
    na = len(g8s)

    def parts(g_refs, buf_refs, sems):
        send_sems, recv_sems, local_sems = sems
        x, y, c = _mesh_pos()
        me_idx = 4 * x + 2 * y + c

        def copy(a, k, slot):
            peer, peer_idx = _scatter_peer(k, x, y, c)
            return pltpu.make_async_remote_copy(
                src_ref=g_refs[a].at[peer_idx], dst_ref=buf_refs[a].at[me_idx if slot is None else slot],
                send_sem=send_sems.at[a, k - 1], recv_sem=recv_sems.at[a, k - 1],
                device_id=peer, device_id_type=MESH)

        def mine(a):
            return pltpu.make_async_copy(g_refs[a].at[me_idx], buf_refs[a].at[me_idx], local_sems.at[a])

        return x, y, c, copy, mine

    def start(g_refs, buf_refs, sems):
        *_, copy, mine = parts(g_refs, buf_refs, sems)
        for a in range(na):
            mine(a).start()
            for k in range(1, N_DEV):
                copy(a, k, None).start()

    def end(g_refs, buf_refs, sems):
        x, y, c, copy, mine = parts(g_refs, buf_refs, sems)
        for a in range(na):
            for k in range(1, N_DEV):
                copy(a, k, _scatter_peer(k, x, y, c)[1]).wait_recv()
        for a in range(na):
            for k in range(1, N_DEV):
                copy(a, k, None).wait_send()
            mine(a).wait()

    return _Comm(
        g8s, [jax.ShapeDtypeStruct(g.shape, g.dtype) for g in g8s],
        [pltpu.SemaphoreType.DMA((na, N_DEV - 1)), pltpu.SemaphoreType.DMA((na, N_DEV - 1)),
         pltpu.SemaphoreType.DMA((na,))],
        start, end)


def _scatter_peer(k, x, y, c):
    px, py, pc = (1 - x if k & 4 else x, 1 - y if k & 2 else y, 1 - c if k & 1 else c)
    return (px, py, pc), 4 * px + 2 * py + pc


def _scatter_start_comm(g8s):
    na = len(g8s)
    arrays = []
    for g8 in g8s:
        arrays += [g8, lax.empty(g8.shape, g8.dtype)]

    def local(refs, sems, a, me_idx):
        return pltpu.make_async_copy(refs[2 * a].at[me_idx], refs[2 * a + 1].at[me_idx], sems[0].at[a])

    def start(refs, outs, sems):
        send_sems, recv_sems = outs[:2]
        x, y, c = _mesh_pos()
        me_idx = 4 * x + 2 * y + c
        for a in range(na):
            local(refs, sems, a, me_idx).start()
            for k in range(1, N_DEV):
                peer, peer_idx = _scatter_peer(k, x, y, c)
                pltpu.make_async_remote_copy(
                    src_ref=refs[2 * a].at[peer_idx], dst_ref=refs[2 * a + 1].at[me_idx],
                    send_sem=send_sems.at[a * (N_DEV - 1) + k - 1], recv_sem=recv_sems.at[a * (N_DEV - 1) + k - 1],
                    device_id=peer, device_id_type=MESH).start()

    def end(refs, outs, sems):
        x, y, c = _mesh_pos()
        for a in range(na):
            local(refs, sems, a, 4 * x + 2 * y + c).wait()

    sem_shape = pltpu.SemaphoreType.DMA((na * (N_DEV - 1),))
    return _Comm(arrays, [sem_shape, sem_shape] + [pltpu.HBM(a.shape, a.dtype) for a in arrays],
                 [pltpu.SemaphoreType.DMA((na,))], start, end, split=True)


def _scatter_wait(started, after, *, name):
    send_sems, recv_sems, *arrays = started
    na = len(arrays) // 2

    def body(*refs):
        send_ref, recv_ref = refs[2 * na], refs[2 * na + 1]
        x, y, c = _mesh_pos()
        for a in range(na):
            for k in range(1, N_DEV):
                peer, peer_idx = _scatter_peer(k, x, y, c)
                pltpu.make_async_remote_copy(
                    src_ref=refs[2 * a].at[peer_idx], dst_ref=refs[2 * a + 1].at[peer_idx],
                    send_sem=send_ref.at[a * (N_DEV - 1) + k - 1], recv_sem=recv_ref.at[a * (N_DEV - 1) + k - 1],
                    device_id=peer, device_id_type=MESH).wait()

    sem = pl.BlockSpec(memory_space=pltpu.SEMAPHORE)
    outs = pl.pallas_call(
        body, name=name, out_shape=tuple(pltpu.HBM(a.shape, a.dtype) for a in arrays),
        in_specs=[HBM] * (2 * na) + [sem, sem, pl.BlockSpec(memory_space=pl.ANY)], out_specs=(HBM,) * (2 * na),
        input_output_aliases={k: k for k in range(2 * na)},
        compiler_params=pltpu.CompilerParams(has_side_effects=pltpu.SideEffectType.DATAFLOW_SIDE_EFFECTING),
    )(*arrays, send_sems, recv_sems, after)
    return outs[1::2]


BIG_WEIGHTS = ("ffn1_w_gate", "ffn1_w_up", "ffn1_w_down", "w_in", "w_out",
               "ffn2_w_gate", "ffn2_w_up", "ffn2_w_down", "w_ple_proj", "w_ple_gate")
COLUMN_SHARDED = ("ffn1_w_gate", "ffn1_w_up", "w_in", "ffn2_w_gate", "ffn2_w_up", "w_ple_proj", "conv_w")
SMALL_WEIGHTS = ("ln1_g", "ln1_b", "conv_b", "w_rgate", "b_rgate", "w_igate", "b_igate", "lru_lambda",
                 "ln2_g", "ln2_b", "ln3_g", "ln3_b")
SMALL_GRADS = SMALL_WEIGHTS + ("conv_w",)


class _Exchange:
    def __init__(self, full):
        self.full = dict(full)
        self.grads = {}

    def __getitem__(self, name):
        return self.full[name]

    def first_gather(self, x):
        return _to_bf16(x, bm=1024, name="x_bf16")

    def gather(self, names):
        return None, None

    def scatter(self, names):
        return None, None

    def scatter_start(self, names):
        return None, None

    def gather_small(self):
        return None, None


class _MeshExchange(_Exchange):
    def __init__(self, full, shards, conv_w):
        super().__init__(full)
        self.shards = shards
        self.conv_w = conv_w
        self.reduced = {}
        self.started = {}
        self.small_parts = None

    def first_gather(self, x):
        first = ("ffn1_w_gate", "ffn1_w_up")
        xb, (gate, up, conv_all) = _to_bf16(
            x, bm=1024, name="x_bf16", comm=_gather_comm([self.shards[n] for n in first] + [self.conv_w]))
        self.take(first[0], gate)
        self.take(first[1], up)
        self.full["conv_w"] = _to_full("conv_w", conv_all)
        return xb

    def gather(self, names):
        def done(outs):
            for n, o in zip(names, outs):
                self.take(n, o)
        return _gather_comm([self.shards[n] for n in names]), done

    def take(self, name, gathered):
        self.full[name] = gathered.reshape((N_DEV * gathered.shape[1],) + gathered.shape[2:])

    def scatter(self, names):
        def done(outs):
            self.reduced.update(zip(names, outs))
        return _scatter_comm([_to_owner_blocks(n, self.grads[n]) for n in names]), done

    def scatter_start(self, names):
        def done(outs):
            self.started[names] = outs
        return _scatter_start_comm([_to_owner_blocks(n, self.grads[n]) for n in names]), done

    def finish(self, after):
        for names, started in self.started.items():
            self.reduced.update(zip(names, _scatter_wait(started, after, name=f"scatter_wait_{names[0]}")))

    def gather_small(self):
        def done(outs):
            self.small_parts, = outs
        packed = jnp.concatenate([_rows128(self.grads[n]) for n in SMALL_GRADS], axis=0)
        return _gather_comm([packed]), done


def _carried(comm_done, call):
    comm, done = comm_done
    res = call(comm)
    if comm is None:
        return res
    res, outs = res
    done(outs)
    return res


def _dw(a, b, *, scale=1.0, name, comm=None):
    k, m = a.shape
    n = b.shape[1]
    return _mm(a, b, ta=True, scale=scale, out_dtype=BF16, bm=_pick(m, (1024, 512, 256, 128)),
               bn=_pick(n, (512, 256, 128)), bk=k, name=name, comm=comm)


def _ffn_bwd(ex, names, saved, xb_in, dz, dzb, ln_in, tag, on_dwd=None, on_dh=None, on_dwu=None, on_dx=None):
    gate, up, down = names
    g, u, h, _, _ = saved
    f = ex[gate].shape[0]

    def request(fn):
        return (None, None) if fn is None else fn(ex)

    ex.grads[down] = _carried(request(on_dwd), lambda c: _dw(h, dzb, scale=0.5, name=f"{tag}_dwd", comm=c))
    dg, du = _carried(request(on_dh), lambda c: _ffn_bwd_dh(
        dzb, ex[down], g, u, scale=0.5, bm=2048, bn=_pick(f, (512, 256, 128)), name=f"{tag}_dh", chunks=8, comm=c))
    ex.grads[gate] = _dw(xb_in, dg, name=f"{tag}_dwg")
    ex.grads[up] = _carried(request(on_dwu), lambda c: _dw(xb_in, du, name=f"{tag}_dwu", comm=c))
    d = dz.shape[1]
    dx = _carried(request(on_dx), lambda c: _ffn_dx(
        dg, du, ex[gate], ex[up], dz, extra_scale=DEEPNORM_ALPHA,
        bm=1024, bn=_pick(d, (512, 256, 128)), name=f"{tag}_dx", comm=c))
    return dx if ln_in is None else _ln_bwd(dx, *ln_in, bm=256, name=f"{tag}_ln_bwd")


def _local_step(x, p, target, positions, w):
    s, d = x.shape
    tabs = _rope_tables(positions)
    xb = w.first_gather(x)
    f = w["ffn1_w_gate"].shape[0]
    ffn_bn, ln_bn, ln_bn_short_k = _pick(f, (512, 256, 128)), _pick(d, (512, 256, 128)), _pick(d, (1024, 512, 256, 128))
    g1, u1, h1 = _carried(w.gather(("ffn1_w_down", "w_in", "w_out")), lambda c: _ffn_up(
        xb, w["ffn1_w_gate"], w["ffn1_w_up"], bm=1024, bn=ffn_bn, name="ffn1_up", comm=c))
    x1b, xh1, rs1 = _carried(w.gather(("ffn2_w_gate", "ffn2_w_up")), lambda c: _mm_ln(
        h1, w["ffn1_w_down"], x, w["ln1_g"], w["ln1_b"], res_scale=DEEPNORM_ALPHA, mm_scale=0.5,
        bm=512, bn=ln_bn, name="ffn1_down_ln", comm=c))
    sv1 = (g1, u1, h1, xh1, rs1)
    pw = w["w_in"].shape[0]
    proj = _carried(w.gather(("ffn2_w_down", "w_ple_gate", "w_ple_proj")), lambda c: _mm(
        x1b, w["w_in"], tb=True, bm=1024, bn=_pick(pw, (512, 256, 128)), bk=d, name="in_proj", comm=c))
    nqk = (N_PATTERNS + 1) * N_KV_HEADS
    qkr = _rotary(proj, tabs, n_cols=nqk, inverse=False, out_dtype=F32, bs=1024, name="rotary")
    attn, lse = _attn_fwd(qkr, proj, name="attn_fwd")
    lru_w = (w["conv_w"], w["conv_b"], w["w_rgate"], w["b_rgate"], w["w_igate"], w["b_igate"], w["lru_lambda"])
    rec, hseq = _lru_fwd(proj, *lru_w, tt=512, name="lru_fwd")
    cat = jnp.concatenate([attn.astype(BF16), rec], axis=1)
    x2b, xh2, rs2 = _mm_ln(cat, w["w_out"], xh1, w["ln2_g"], w["ln2_b"], res_scale=DEEPNORM_ALPHA, mm_scale=1.0,
                           res_affine=(w["ln1_g"], w["ln1_b"]), bm=512, bn=ln_bn_short_k, name="out_proj_ln")
    g2, u2, h2 = _ffn_up(x2b, w["ffn2_w_gate"], w["ffn2_w_up"], bm=1024, bn=ffn_bn, name="ffn2_up")
    x3b, xh3, rs3 = _mm_ln(h2, w["ffn2_w_down"], xh2, w["ln3_g"], w["ln3_b"], res_scale=DEEPNORM_ALPHA, mm_scale=0.5,
                           res_affine=(w["ln2_g"], w["ln2_b"]), bm=512, bn=ln_bn, name="ffn2_down_ln")
    sv3 = (g2, u2, h2, xh3, rs3)
    lsum, dy, dgate, dple = _ple_loss(xh3, w["ln3_g"], w["ln3_b"], x3b, p, w["w_ple_gate"], w["w_ple_proj"], target,
                                      bm=1024, bn=_pick(d, (512, 256, 128)), name="ple_loss")
    grads = w.grads
    grads["w_ple_gate"] = _dw(x3b, dgate, name="dw_ple_gate")
    grads["w_ple_proj"] = _dw(p, dple, name="dw_ple_proj")
    dz3, dz3b, grads["ln3_g"], grads["ln3_b"] = _carried(w.scatter(("w_ple_gate", "w_ple_proj")), lambda c: _mm_dx(
        dgate, w["w_ple_gate"], dy, xh3, rs3, w["ln3_g"], extra_scale=1.0, bm=512, bn=ln_bn_short_k,
        name="ple_dx", tb=True, comm=c))
    dz2, dz2b, grads["ln2_g"], grads["ln2_b"] = _ffn_bwd(
        w, ("ffn2_w_gate", "ffn2_w_up", "ffn2_w_down"), sv3, x2b, dz3, dz3b, (xh2, rs2, w["ln2_g"]), "ffn2",
        on_dx=lambda ex: ex.scatter_start(("ffn2_w_down", "ffn2_w_gate", "ffn2_w_up")))
    grads["w_out"] = _dw(cat, dz2b, name="dw_out")
    dcat = _mm(dz2b, w["w_out"], tb=True, bm=1024, bn=_pick(d, (512, 256, 128)), bk=d, name="out_proj_dx")
    dq0, dq1, dq2, dk, dvb = _attn_bwd(qkr, proj, attn, lse, dcat, name="attn_bwd")
    nh = N_KV_HEADS
    dqkv = [_rotary(t, tabs, n_cols=nh, inverse=True, out_dtype=BF16, bs=1024, name=f"rotary_bwd{i}")
            for i, t in enumerate((dq0, dq1, dq2, dk))]
    (dxb, dyb, grads["conv_w"], grads["conv_b"], grads["w_rgate"], grads["b_rgate"], grads["w_igate"],
     grads["b_igate"], grads["lru_lambda"]) = _lru_bwd(proj, hseq, dcat, *lru_w, tt=512, name="lru_bwd")
    dproj = jnp.concatenate(dqkv + [dvb, dxb, dyb], axis=1)
    grads["w_in"] = _dw(x1b, dproj, name="dw_in")
    dz1, dz1b, grads["ln1_g"], grads["ln1_b"] = _carried(w.scatter_start(("w_out", "w_in")), lambda c: _mm_dx(
        dproj, w["w_in"], dz2, xh1, rs1, w["ln1_g"], extra_scale=DEEPNORM_ALPHA,
        bm=512, bn=ln_bn, name="in_proj_dx", comm=c))
    grad_x = _ffn_bwd(w, ("ffn1_w_gate", "ffn1_w_up", "ffn1_w_down"), sv1, xb, dz1, dz1b, None, "ffn1",
                      on_dwd=lambda ex: ex.gather_small(),
                      on_dh=lambda ex: ex.scatter_start(("ffn1_w_down",)),
                      on_dwu=lambda ex: ex.scatter_start(("ffn1_w_gate",)),
                      on_dx=lambda ex: ex.scatter_start(("ffn1_w_up",)))
    return lsum, grad_x


def _to_full(name, gathered):
    if name in COLUMN_SHARDED:
        _, r, c = gathered.shape
        return jnp.transpose(gathered, (1, 0, 2)).reshape(r, N_DEV * c)
    return gathered.reshape((N_DEV * gathered.shape[1],) + gathered.shape[2:])


def _to_owner_blocks(name, full):
    if name in COLUMN_SHARDED:
        r, c = full.shape
        return jnp.transpose(full.reshape(r, N_DEV, c // N_DEV), (1, 0, 2))
    return full.reshape((N_DEV, full.shape[0] // N_DEV) + full.shape[1:])


def _rows128(a):
    flat = a.reshape(-1, LANES)
    pad = (-flat.shape[0]) % 8
    return jnp.pad(flat, ((0, pad), (0, 0))) if pad else flat


def kernel(x, p, positions, ffn1_w_gate, ffn1_w_up, ffn1_w_down, ln1_g, ln1_b, w_in, conv_w, conv_b, w_rgate, b_rgate, w_igate, b_igate, lru_lambda, w_out, ln2_g, ln2_b, ffn2_w_gate, ffn2_w_up, ffn2_w_down, ln3_g, ln3_b, w_ple_proj, w_ple_gate, loss_target, m_ffn1_w_gate, m_ffn1_w_up, m_ffn1_w_down, m_ln1_g, m_ln1_b, m_w_in, m_conv_w, m_conv_b, m_w_rgate, m_b_rgate, m_w_igate, m_b_igate, m_lru_lambda, m_w_out, m_ln2_g, m_ln2_b, m_ffn2_w_gate, m_ffn2_w_up, m_ffn2_w_down, m_ln3_g, m_ln3_b, m_w_ple_proj, m_w_ple_gate, v_ffn1_w_gate, v_ffn1_w_up, v_ffn1_w_down, v_ln1_g, v_ln1_b, v_w_in, v_conv_w, v_conv_b, v_w_rgate, v_b_rgate, v_w_igate, v_b_igate, v_lru_lambda, v_w_out, v_ln2_g, v_ln2_b, v_ffn2_w_gate, v_ffn2_w_up, v_ffn2_w_down, v_ln3_g, v_ln3_b, v_w_ple_proj, v_w_ple_gate):
    names = ("ffn1_w_gate", "ffn1_w_up", "ffn1_w_down", "ln1_g", "ln1_b", "w_in", "conv_w", "conv_b", "w_rgate",
             "b_rgate", "w_igate", "b_igate", "lru_lambda", "w_out", "ln2_g", "ln2_b", "ffn2_w_gate", "ffn2_w_up",
             "ffn2_w_down", "ln3_g", "ln3_b", "w_ple_proj", "w_ple_gate")
    ws = (ffn1_w_gate, ffn1_w_up, ffn1_w_down, ln1_g, ln1_b, w_in, conv_w, conv_b, w_rgate, b_rgate, w_igate, b_igate,
          lru_lambda, w_out, ln2_g, ln2_b, ffn2_w_gate, ffn2_w_up, ffn2_w_down, ln3_g, ln3_b, w_ple_proj, w_ple_gate)
    ms = (m_ffn1_w_gate, m_ffn1_w_up, m_ffn1_w_down, m_ln1_g, m_ln1_b, m_w_in, m_conv_w, m_conv_b, m_w_rgate, m_b_rgate,
          m_w_igate, m_b_igate, m_lru_lambda, m_w_out, m_ln2_g, m_ln2_b, m_ffn2_w_gate, m_ffn2_w_up, m_ffn2_w_down,
          m_ln3_g, m_ln3_b, m_w_ple_proj, m_w_ple_gate)
    vs = (v_ffn1_w_gate, v_ffn1_w_up, v_ffn1_w_down, v_ln1_g, v_ln1_b, v_w_in, v_conv_w, v_conv_b, v_w_rgate, v_b_rgate,
          v_w_igate, v_b_igate, v_lru_lambda, v_w_out, v_ln2_g, v_ln2_b, v_ffn2_w_gate, v_ffn2_w_up, v_ffn2_w_down,
          v_ln3_g, v_ln3_b, v_w_ple_proj, v_w_ple_gate)
    def local(a):
        return a[0] if a.ndim >= 3 else a

    w_loc = {n: local(a) for n, a in zip(names, ws)}
    m_loc = {n: local(a) for n, a in zip(names, ms)}
    v_loc = {n: local(a) for n, a in zip(names, vs)}
    out_shapes = {n: a.shape for n, a in zip(names, ws)}

    shards = {n: (w_loc[n].T if n in COLUMN_SHARDED else w_loc[n]).astype(BF16) for n in BIG_WEIGHTS}
    ex = _MeshExchange({n: w_loc[n] for n in SMALL_WEIGHTS}, shards, w_loc["conv_w"])

    lsum, grad_x = _local_step(x[0], p[0, 0], loss_target[0], positions[0], ex)
    ex.finish(grad_x)
    grads, reduced = ex.grads, ex.reduced
    d_model = x.shape[-1]
    loss = lax.psum(lsum[0, 0] * (0.5 / d_model), ("x", "y", "c"))

    small = SMALL_GRADS
    summed = _sum_parts(ex.small_parts, name="sum_small_grads")
    small_grads, row = {}, 0
    for n in small:
        rows = grads[n].size // LANES
        small_grads[n] = summed[row:row + rows].reshape(grads[n].shape)
        row += rows + (-rows) % 8
    me = 4 * lax.axis_index("x") + 2 * lax.axis_index("y") + lax.axis_index("c")
    cw_cols = w_loc["conv_w"].shape[1]
    small_grads["conv_w"] = lax.dynamic_slice_in_dim(small_grads["conv_w"], me * cw_cols, cw_cols, axis=1)

    out_g, out_d, out_m, out_v = {}, {}, {}, {}
    for n in names:
        wl, ml, vl = w_loc[n], m_loc[n], v_loc[n]
        shape = wl.shape
        if n in BIG_WEIGHTS:
            gparts = reduced[n]
        else:
            gparts = small_grads[n].reshape((1,) + shape)
        if wl.ndim == 3:
            wl, ml, vl = (t.reshape(-1, shape[-1]) for t in (wl, ml, vl))
            gparts = gparts.reshape(gparts.shape[0], -1, shape[-1])
        res = _adamw(wl, ml, vl, gparts, name=f"adamw_{n}")
        out_g[n], out_d[n], out_m[n], out_v[n] = (t.reshape(out_shapes[n]) for t in res)

    return (loss, grad_x[None], *[out_g[n] for n in names], *[out_d[n] for n in names],
            *[out_m[n] for n in names], *[out_v[n] for n in names])
```

```python
import jax
import jax.numpy as jnp
from jax import lax
from jax.experimental import pallas as pl
from jax.experimental.pallas import tpu as pltpu

F32 = jnp.float32
BF16 = jnp.bfloat16

N_DEV = 8
LANES = 128
MIB = 1 << 20

HEAD_DIM = 128
N_KV_HEADS = 4
DILATIONS = (1, 4, 16)
N_PATTERNS = 3
SPAN = 128
ROT_DIMS = 32
ROPE_THETA = 500000.0
LRU_C = 8.0
CONV_WIDTH = 4
LN_EPS = 1e-5
DEEPNORM_ALPHA = 2.0 ** 0.25
ATTN_TILE = SPAN * DILATIONS[-1]

ADAM_LR = 0.001
ADAM_B1 = 0.9
ADAM_B2 = 0.999
ADAM_EPS = 1e-08
ADAM_WD = 0.01
ADAM_STEP = 10

MESH = pl.DeviceIdType.MESH
NT_DIMS = (((1,), (1,)), ((), ()))
EPILOGUE_ROWS = 64


def _cp(semantics, vmem_mib):
    return pltpu.CompilerParams(dimension_semantics=semantics, vmem_limit_bytes=vmem_mib * MIB)


def _pick(n, candidates):
    for c in candidates:
        if n % c == 0:
            return c
    return n


class _Comm:
    def __init__(self, arrays, out_shapes, scratch, start, end, mid=None, split=False):
        self.arrays, self.out_shapes, self.scratch = list(arrays), list(out_shapes), list(scratch)
        self.start, self.mid, self.end, self.split = start, mid, end, split


def _call(body, *, name, grid, in_specs, out_specs, out_shape, args, scratch_shapes=(), vmem_mib, comm=None):
    single = not isinstance(out_shape, (tuple, list))
    out_shape_t = (out_shape,) if single else tuple(out_shape)
    out_specs_t = (out_specs,) if single else tuple(out_specs)
    params = _cp(("arbitrary",) * len(grid), vmem_mib)
    if comm is None:
        res = pl.pallas_call(body, name=name, grid=grid, in_specs=list(in_specs), out_specs=out_specs_t,
                             out_shape=out_shape_t, scratch_shapes=list(scratch_shapes), compiler_params=params)(*args)
        return res[0] if single else res
    n_in, n_out, n_scr = len(args), len(out_shape_t), len(scratch_shapes)
    nci, nco = len(comm.arrays), len(comm.out_shapes)
    total = 1
    for g in grid:
        total *= g

    def wrapped(*refs):
        ins, refs = refs[:n_in], refs[n_in:]
        cin, refs = refs[:nci], refs[nci:]
        outs, refs = refs[:n_out], refs[n_out:]
        cout, refs = refs[:nco], refs[nco:]
        scr, csem = refs[:n_scr], refs[n_scr:]
        step = pl.program_id(0)
        for ax in range(1, len(grid)):
            step = step * grid[ax] + pl.program_id(ax)

        @pl.when(step == 0)
        def _():
            comm.start(cin, cout, csem)

        body(*ins, *outs, *scr)
        if comm.mid is not None:
            @pl.when(step == (3 * total) // 4)
            def _():
                comm.mid(cin, cout, csem)

        @pl.when(step == total - 1)
        def _():
            comm.end(cin, cout, csem)

    hbm = pl.BlockSpec(memory_space=pltpu.HBM)
    if comm.split:
        sem = pl.BlockSpec(memory_space=pltpu.SEMAPHORE)
        n_sems = nco - nci
        res = pl.pallas_call(
            wrapped, name=name, grid=grid,
            in_specs=list(in_specs) + [hbm] * nci,
            out_specs=out_specs_t + (sem,) * n_sems + (hbm,) * nci,
            out_shape=out_shape_t + tuple(comm.out_shapes),
            scratch_shapes=list(scratch_shapes) + comm.scratch,
            input_output_aliases={n_in + k: n_out + n_sems + k for k in range(nci)},
            compiler_params=pltpu.CompilerParams(
                dimension_semantics=("arbitrary",) * len(grid), vmem_limit_bytes=vmem_mib * MIB,
                has_side_effects=pltpu.SideEffectType.DATAFLOW_SIDE_EFFECTING),
        )(*args, *[pltpu.with_memory_space_constraint(a, pltpu.HBM) for a in comm.arrays])
    else:
        res = pl.pallas_call(
            wrapped, name=name, grid=grid,
            in_specs=list(in_specs) + [hbm] * nci,
            out_specs=out_specs_t + (hbm,) * nco,
            out_shape=out_shape_t + tuple(comm.out_shapes),
            scratch_shapes=list(scratch_shapes) + comm.scratch,
            compiler_params=params)(*args, *comm.arrays)
    own, extra = res[:n_out], res[n_out:]
    return (own[0] if single else own), extra


def _mm(a, b, *, ta=False, tb=False, out_dtype=F32, scale=1.0, bm, bn, bk, name, comm=None):
    m, k = (a.shape[1], a.shape[0]) if ta else a.shape
    n = b.shape[0] if tb else b.shape[1]
    bm, bn, bk = min(bm, m), min(bn, n), min(bk, k)
    assert m % bm == 0 and n % bn == 0 and k % bk == 0, (name, m, n, k, bm, bn, bk)
    nk = k // bk
    a_spec = pl.BlockSpec((bk, bm), lambda i, j, kk: (kk, i)) if ta else pl.BlockSpec((bm, bk), lambda i, j, kk: (i, kk))
    b_spec = pl.BlockSpec((bn, bk), lambda i, j, kk: (j, kk)) if tb else pl.BlockSpec((bk, bn), lambda i, j, kk: (kk, j))
    dn = (((0 if ta else 1,), (1 if tb else 0,)), ((), ()))

    def body(a_ref, b_ref, o_ref, *acc):
        part = lax.dot_general(a_ref[...].astype(BF16), b_ref[...].astype(BF16), dn, preferred_element_type=F32)
        if nk == 1:
            o_ref[...] = (part * scale).astype(out_dtype)
            return
        acc_ref, = acc
        kk = pl.program_id(2)

        @pl.when(kk == 0)
        def _():
            acc_ref[...] = part

        @pl.when(kk > 0)
        def _():
            acc_ref[...] += part

        @pl.when(kk == nk - 1)
        def _():
            o_ref[...] = (acc_ref[...] * scale).astype(out_dtype)

    return _call(
        body, name=name,
        out_shape=jax.ShapeDtypeStruct((m, n), out_dtype),
        grid=(m // bm, n // bn, nk),
        in_specs=[a_spec, b_spec],
        out_specs=pl.BlockSpec((bm, bn), lambda i, j, kk: (i, j)),
        scratch_shapes=[pltpu.VMEM((bm, bn), F32)] if nk > 1 else [],
        args=(a, b), vmem_mib=56, comm=comm)


def _ffn_up(xb, wg, wu, *, bm, bn, name, gate_done=False, comm=None):
    s, d = xb.shape
    f = wu.shape[0]
    bm, bn = min(bm, s), min(bn, f)
    assert s % bm == 0 and f % bn == 0

    def body(x_ref, wg_ref, wu_ref, hg_ref, hu_ref, h_ref):
        x = x_ref[...]
        if gate_done:
            g = wg_ref[...].astype(F32)
        else:
            g = lax.dot_general(x, wg_ref[...], NT_DIMS, preferred_element_type=F32)
        u = lax.dot_general(x, wu_ref[...], NT_DIMS, preferred_element_type=F32)
        sig = jax.nn.sigmoid(g)
        silu = g * sig
        hg_ref[...] = (u * (sig * (1.0 + g * (1.0 - sig)))).astype(BF16)
        hu_ref[...] = silu.astype(BF16)
        h_ref[...] = (silu * u).astype(BF16)

    out = jax.ShapeDtypeStruct((s, f), BF16)
    blk = pl.BlockSpec((bm, bn), lambda i, j: (i, j))
    return _call(
        body, name=name, out_shape=(out, out, out),
        grid=(s // bm, f // bn),
        in_specs=[pl.BlockSpec((bm, d), lambda i, j: (i, 0)),
                  blk if gate_done else pl.BlockSpec((bn, d), lambda i, j: (j, 0)),
                  pl.BlockSpec((bn, d), lambda i, j: (j, 0))],
        out_specs=(blk, blk, blk),
        args=(xb, wg, wu), vmem_mib=56, comm=comm)


def _ffn_bwd_dh(dzb, wd, g, u, *, scale, bm, bn, name, chunks=2, comm=None):
    s, d = dzb.shape
    f = wd.shape[0]
    bm, bn = min(bm, s), min(bn, f)
    assert s % bm == 0 and f % bn == 0

    cr = bm // chunks

    def body(dz_ref, wd_ref, hg_ref, hu_ref, dg_ref, du_ref):
        for r in range(chunks):
            rows = slice(r * cr, (r + 1) * cr)
            dh = lax.dot_general(dz_ref[rows, :], wd_ref[...], NT_DIMS, preferred_element_type=F32) * scale
            dg_ref[rows, :] = (dh * hg_ref[rows, :].astype(F32)).astype(BF16)
            du_ref[rows, :] = (dh * hu_ref[rows, :].astype(F32)).astype(BF16)

    out = jax.ShapeDtypeStruct((s, f), BF16)
    blk = pl.BlockSpec((bm, bn), lambda i, j: (i, j))
    return _call(
        body, name=name, out_shape=(out, out),
        grid=(s // bm, f // bn),
        in_specs=[pl.BlockSpec((bm, d), lambda i, j: (i, 0)),
                  pl.BlockSpec((bn, d), lambda i, j: (j, 0)), blk, blk],
        out_specs=(blk, blk),
        args=(dzb, wd, g, u), vmem_mib=56, comm=comm)


def _full_rows(acc_ref, rows, nj):
    return jnp.concatenate([acc_ref[jj, rows, :] for jj in range(nj)], axis=1)


def _mm_ln(a, b, res, gamma, beta, *, res_scale, mm_scale, bm, bn, name, comm=None):
    s, k = a.shape
    d = b.shape[1]
    bm, bn = min(bm, s), min(bn, d)
    assert s % bm == 0 and d % bn == 0
    nj = d // bn
    ch = min(EPILOGUE_ROWS, bm)

    def body(a_ref, b_ref, r_ref, g_ref, be_ref, y_ref, yb_ref, xh_ref, rs_ref, acc_ref):
        j = pl.program_id(1)
        acc_ref[j] = jnp.dot(a_ref[...], b_ref[...], preferred_element_type=F32)

        @pl.when(j == nj - 1)
        def _():
            def chunk(ci, carry):
                rows = pl.ds(pl.multiple_of(ci * ch, ch), ch)
                z = res_scale * r_ref[rows, :] + mm_scale * _full_rows(acc_ref, rows, nj)
                mu = jnp.mean(z, axis=-1, keepdims=True)
                zc = z - mu
                var = jnp.mean(zc * zc, axis=-1, keepdims=True)
                rstd = lax.rsqrt(var + LN_EPS)
                xh = zc * rstd
                y = xh * g_ref[...] + be_ref[...]
                y_ref[rows, :] = y
                yb_ref[rows, :] = y.astype(BF16)
                xh_ref[rows, :] = xh
                rs_ref[rows, :] = rstd
                return carry

            lax.fori_loop(0, bm // ch, chunk, 0)

    row = pl.BlockSpec((bm, d), lambda i, j: (i, 0))
    vec = pl.BlockSpec((1, d), lambda i, j: (0, 0))
    return _call(
        body, name=name,
        out_shape=(jax.ShapeDtypeStruct((s, d), F32), jax.ShapeDtypeStruct((s, d), BF16),
                   jax.ShapeDtypeStruct((s, d), F32), jax.ShapeDtypeStruct((s, 1), F32)),
        grid=(s // bm, nj),
        in_specs=[pl.BlockSpec((bm, k), lambda i, j: (i, 0)),
                  pl.BlockSpec((k, bn), lambda i, j: (0, j)), row, vec, vec],
        out_specs=(row, row, row, pl.BlockSpec((bm, 1), lambda i, j: (i, 0))),
        scratch_shapes=[pltpu.VMEM((nj, bm, bn), F32)],
        args=(a, b, res, gamma, beta), vmem_mib=58, comm=comm)


def _mm_dx(a, wt, extra, xhat, rstd, gamma, *, extra_scale, bm, bn, name, tb=False, comm=None):
    s, k = a.shape
    d = wt.shape[0] if tb else wt.shape[1]
    bm, bn = min(bm, s), min(bn, d)
    assert s % bm == 0 and d % bn == 0
    nj = d // bn
    ch = min(EPILOGUE_ROWS, bm)
    dims = NT_DIMS if tb else (((1,), (0,)), ((), ()))

    def body(a_ref, w_ref, e_ref, xh_ref, rs_ref, g_ref, dz_ref, dzb_ref, dg_ref, db_ref, acc_ref):
        i = pl.program_id(0)
        j = pl.program_id(1)
        acc_ref[j] = lax.dot_general(a_ref[...], w_ref[...], dims, preferred_element_type=F32)

        @pl.when(j == nj - 1)
        def _():
            def chunk(ci, carry):
                dgp, dbp = carry
                rows = pl.ds(pl.multiple_of(ci * ch, ch), ch)
                dx = extra_scale * e_ref[rows, :] + _full_rows(acc_ref, rows, nj)
                xh = xh_ref[rows, :]
                dxh = dx * g_ref[...]
                m1 = jnp.mean(dxh, axis=-1, keepdims=True)
                m2 = jnp.mean(dxh * xh, axis=-1, keepdims=True)
                dz = rs_ref[rows, :] * (dxh - m1 - xh * m2)
                dz_ref[rows, :] = dz
                dzb_ref[rows, :] = dz.astype(BF16)
                return dgp + jnp.sum(dx * xh, axis=0, keepdims=True), dbp + jnp.sum(dx, axis=0, keepdims=True)

            zero = jnp.zeros((1, d), F32)
            dgp, dbp = lax.fori_loop(0, bm // ch, chunk, (zero, zero))

            @pl.when(i == 0)
            def _():
                dg_ref[...] = dgp
                db_ref[...] = dbp

            @pl.when(i > 0)
            def _():
                dg_ref[...] += dgp
                db_ref[...] += dbp

    row = pl.BlockSpec((bm, d), lambda i, j: (i, 0))
    vec = pl.BlockSpec((1, d), lambda i, j: (0, 0))
    return _call(
        body, name=name,
        out_shape=(jax.ShapeDtypeStruct((s, d), F32), jax.ShapeDtypeStruct((s, d), BF16),
                   jax.ShapeDtypeStruct((1, d), F32), jax.ShapeDtypeStruct((1, d), F32)),
        grid=(s // bm, nj),
        in_specs=[pl.BlockSpec((bm, k), lambda i, j: (i, 0)),
                  pl.BlockSpec((bn, k), lambda i, j: (j, 0)) if tb else pl.BlockSpec((k, bn), lambda i, j: (0, j)),
                  row, row, pl.BlockSpec((bm, 1), lambda i, j: (i, 0)), vec],
        out_specs=(row, row, vec, vec),
        scratch_shapes=[pltpu.VMEM((nj, bm, bn), F32)],
        args=(a, wt, extra, xhat, rstd, gamma), vmem_mib=58, comm=comm)


def _ffn_dx(dg, du, wgt, wut, extra, *, extra_scale, bm, bn, name, comm=None):
    s, f = dg.shape
    d = wgt.shape[1]
    bm, bn = min(bm, s), min(bn, d)
    assert s % bm == 0 and d % bn == 0 and f % (2 * LANES) == 0
    nj, half = d // bn, f // 2

    def body(dg_ref, du_ref, wg_ref, wu_ref, e_ref, o_ref, acc_ref):
        kh, j = pl.program_id(1), pl.program_id(2)
        part = jnp.dot(dg_ref[...], wg_ref[...], preferred_element_type=F32)
        part = part + jnp.dot(du_ref[...], wu_ref[...], preferred_element_type=F32)

        @pl.when(kh == 0)
        def _():
            acc_ref[j] = part

        @pl.when(kh == 1)
        def _():
            o_ref[...] = extra_scale * e_ref[...] + (acc_ref[j] + part)

    rows = pl.BlockSpec((bm, half), lambda i, kh, j: (i, kh))
    cols = pl.BlockSpec((half, bn), lambda i, kh, j: (kh, j))
    blk = pl.BlockSpec((bm, bn), lambda i, kh, j: (i, j * kh))
    return _call(
        body, name=name, out_shape=jax.ShapeDtypeStruct((s, d), F32),
        grid=(s // bm, 2, nj), in_specs=[rows, rows, cols, cols, blk], out_specs=blk,
        scratch_shapes=[pltpu.VMEM((nj, bm, bn), F32)],
        args=(dg, du, wgt, wut, extra), vmem_mib=58, comm=comm)


def _ln_bwd(dx, xhat, rstd, gamma, *, bm, name):
    s, d = dx.shape
    bm = min(bm, s)
    assert s % bm == 0
    ch = min(EPILOGUE_ROWS, bm)

    def body(dx_ref, xh_ref, rs_ref, g_ref, dz_ref, dzb_ref, dg_ref, db_ref):
        def chunk(ci, carry):
            dgp, dbp = carry
            rows = pl.ds(pl.multiple_of(ci * ch, ch), ch)
            dxv = dx_ref[rows, :]
            xh = xh_ref[rows, :]
            dxh = dxv * g_ref[...]
            m1 = jnp.mean(dxh, axis=-1, keepdims=True)
            m2 = jnp.mean(dxh * xh, axis=-1, keepdims=True)
            dz = rs_ref[rows, :] * (dxh - m1 - xh * m2)
            dz_ref[rows, :] = dz
            dzb_ref[rows, :] = dz.astype(BF16)
            return dgp + jnp.sum(dxv * xh, axis=0, keepdims=True), dbp + jnp.sum(dxv, axis=0, keepdims=True)

        zero = jnp.zeros((1, d), F32)
        dgp, dbp = lax.fori_loop(0, bm // ch, chunk, (zero, zero))
        i = pl.program_id(0)

        @pl.when(i == 0)
        def _():
            dg_ref[...] = dgp
            db_ref[...] = dbp

        @pl.when(i > 0)
        def _():
            dg_ref[...] += dgp
            db_ref[...] += dbp

    row = pl.BlockSpec((bm, d), lambda i: (i, 0))
    vec = pl.BlockSpec((1, d), lambda i: (0, 0))
    return _call(
        body, name=name,
        out_shape=(jax.ShapeDtypeStruct((s, d), F32), jax.ShapeDtypeStruct((s, d), BF16),
                   jax.ShapeDtypeStruct((1, d), F32), jax.ShapeDtypeStruct((1, d), F32)),
        grid=(s // bm,), in_specs=[row, row, pl.BlockSpec((bm, 1), lambda i: (i, 0)), vec],
        out_specs=(row, row, vec, vec), args=(dx, xhat, rstd, gamma), vmem_mib=48)


def _to_bf16(x, *, bm, name, comm=None):
    s, d = x.shape
    bm = min(bm, s)
    assert s % bm == 0

    def body(x_ref, o_ref):
        o_ref[...] = x_ref[...].astype(BF16)

    row = pl.BlockSpec((bm, d), lambda i: (i, 0))
    return _call(body, name=name, out_shape=jax.ShapeDtypeStruct((s, d), BF16), grid=(s // bm,),
                 in_specs=[row], out_specs=row, args=(x,), vmem_mib=48, comm=comm)


def _ple_loss(x3, x3b, p, wpg, wpp, target, *, bm, bn, name):
    s, d = x3.shape
    dp = p.shape[1]
    bm, bn = min(bm, s), min(bn, d)
    assert s % bm == 0 and d % bn == 0
    inv_d = 1.0 / d
    chunks = 4 if bm % 64 == 0 else 1
    cr = bm // chunks

    def body(x_ref, xb_ref, p_ref, wg_ref, wp_ref, t_ref, l_ref, dy_ref, dg_ref, dp_ref):
        first = (pl.program_id(0) == 0) & (pl.program_id(1) == 0)

        @pl.when(first)
        def _():
            l_ref[...] = jnp.zeros_like(l_ref)

        part = 0.0
        for r in range(chunks):
            rows = slice(r * cr, (r + 1) * cr)
            gp = jnp.dot(xb_ref[rows, :], wg_ref[...], preferred_element_type=F32)
            pp = lax.dot_general(p_ref[rows, :].astype(BF16), wp_ref[...], NT_DIMS, preferred_element_type=F32)
            sig = jax.nn.sigmoid(gp)
            err = x_ref[rows, :] + sig * pp - t_ref[rows, :]
            part = part + jnp.sum(err * err)
            dy = err * inv_d
            dy_ref[rows, :] = dy
            dg_ref[rows, :] = (dy * pp * sig * (1.0 - sig)).astype(BF16)
            dp_ref[rows, :] = (dy * sig).astype(BF16)
        l_ref[...] += part

    blk = pl.BlockSpec((bm, bn), lambda i, j: (i, j))
    return pl.pallas_call(
        body, name=name,
        out_shape=(jax.ShapeDtypeStruct((8, LANES), F32), jax.ShapeDtypeStruct((s, d), F32),
                   jax.ShapeDtypeStruct((s, d), BF16), jax.ShapeDtypeStruct((s, d), BF16)),
        grid=(s // bm, d // bn),
        in_specs=[blk, pl.BlockSpec((bm, d), lambda i, j: (i, 0)), pl.BlockSpec((bm, dp), lambda i, j: (i, 0)),
                  pl.BlockSpec((d, bn), lambda i, j: (0, j)), pl.BlockSpec((bn, dp), lambda i, j: (j, 0)), blk],
        out_specs=(pl.BlockSpec((8, LANES), lambda i, j: (0, 0)), blk, blk, blk),
        compiler_params=_cp(("arbitrary", "arbitrary"), 56),
    )(x3, x3b, p, wpg, wpp, target)


def _rope_tables(positions):
    half = ROT_DIMS // 2
    lane = jnp.arange(HEAD_DIM)
    inv_freq = jnp.power(jnp.float32(ROPE_THETA), -(lane % half).astype(F32) * (2.0 / ROT_DIMS))
    ang = positions.astype(F32)[:, None] * inv_freq
    cos, sin = jnp.cos(ang), jnp.sin(ang)
    cf = jnp.where(lane < ROT_DIMS, cos, 1.0)
    sa = jnp.where(lane < half, -sin, 0.0)
    sb = jnp.where((lane >= half) & (lane < ROT_DIMS), sin, 0.0)
    return cf, sa, sb


def _rotary(t, tabs, *, n_cols, inverse, out_dtype, bs, name):
    s = t.shape[0]
    bs = min(bs, s)
    half = ROT_DIMS // 2
    heads = N_KV_HEADS
    assert n_cols % heads == 0

    def body(t_ref, cf_ref, sa_ref, sb_ref, o_ref):
        cf, sa, sb = cf_ref[...], sa_ref[...], sb_ref[...]
        for hd in range(heads):
            lanes = slice(hd * HEAD_DIM, (hd + 1) * HEAD_DIM)
            v = t_ref[:, lanes]
            if inverse:
                o = v * cf + pltpu.roll(v * sa, half, 1) + pltpu.roll(v * sb, HEAD_DIM - half, 1)
            else:
                o = v * cf + pltpu.roll(v, HEAD_DIM - half, 1) * sa + pltpu.roll(v, half, 1) * sb
            o_ref[:, lanes] = o.astype(out_dtype)

    blk = pl.BlockSpec((bs, heads * HEAD_DIM), lambda i, j: (i, j))
    tab = pl.BlockSpec((bs, HEAD_DIM), lambda i, j: (i, 0))
    return pl.pallas_call(
        body, name=name, out_shape=jax.ShapeDtypeStruct((s, n_cols * HEAD_DIM), out_dtype),
        grid=(s // bs, n_cols // heads), in_specs=[blk, tab, tab, tab], out_specs=blk,
        compiler_params=_cp(("parallel", "arbitrary"), 32),
    )(t, *tabs)


def _attn_blocks():
    out = []
    for g, dil in enumerate(DILATIONS):
        sup = SPAN * dil
        for j in range(ATTN_TILE // sup):
            for r in range(dil):
                out.append((g, j * sup + r, dil, (j - 1) * sup + r if j > 0 else None, ATTN_TILE - sup + r))
    return out


def _rows(ref, start, dil, lead=None):
    idx = pl.ds(start, SPAN, stride=dil) if dil > 1 else pl.ds(start, SPAN)
    return ref[idx, :] if lead is None else ref[lead, idx, :]


def _band_masks(n):
    qi = lax.broadcasted_iota(jnp.int32, (SPAN, 2 * SPAN), 0)
    ki = lax.broadcasted_iota(jnp.int32, (SPAN, 2 * SPAN), 1)
    band = (ki >= qi) & (ki <= qi + SPAN)
    return band, band & ((ki >= SPAN) | (n > 0))


def _attn_fwd(qkr, proj, *, name):
    s = qkr.shape[0]
    t = ATTN_TILE
    assert s % t == 0
    nt = s // t
    scale = HEAD_DIM ** -0.5
    kcol, vcol = N_PATTERNS * N_KV_HEADS, (N_PATTERNS + 1) * N_KV_HEADS
    blocks = _attn_blocks()

    def body(q0, q1, q2, kc_ref, kp_ref, vc_ref, vp_ref, o_ref, l_ref, og, lg):
        n = pl.program_id(1)
        band, band_first = _band_masks(n)
        q_refs = (q0, q1, q2)
        for g, start, dil, prev_in_tile, prev_start in blocks:
            q = _rows(q_refs[g], start, dil).astype(BF16)
            if prev_in_tile is not None:
                kp, vp, mask = _rows(kc_ref, prev_in_tile, dil), _rows(vc_ref, prev_in_tile, dil), band
            else:
                kp, vp, mask = _rows(kp_ref, prev_start, dil), _rows(vp_ref, prev_start, dil), band_first
            kk = jnp.concatenate([kp, _rows(kc_ref, start, dil)], axis=0).astype(BF16)
            vv = jnp.concatenate([vp, _rows(vc_ref, start, dil)], axis=0).astype(BF16)
            sc = lax.dot_general(q, kk, (((1,), (1,)), ((), ())), preferred_element_type=F32) * scale
            sc = jnp.where(mask, sc, -1e30)
            m = jnp.max(sc, axis=-1, keepdims=True)
            e = jnp.exp(sc - m)
            den = jnp.sum(e, axis=-1, keepdims=True)
            o = jnp.dot(e.astype(BF16), vv, preferred_element_type=F32) / den
            idx = pl.ds(start, SPAN, stride=dil) if dil > 1 else pl.ds(start, SPAN)
            og[g, idx, :] = o
            lg[g, idx, :] = jnp.broadcast_to(m + jnp.log(den), (SPAN, HEAD_DIM))
        l0, l1, l2 = lg[0], lg[1], lg[2]
        m = jnp.maximum(jnp.maximum(l0, l1), l2)
        w0, w1, w2 = jnp.exp(l0 - m), jnp.exp(l1 - m), jnp.exp(l2 - m)
        den = w0 + w1 + w2
        o_ref[...] = (w0 * og[0] + w1 * og[1] + w2 * og[2]) / den
        l_ref[...] = m + jnp.log(den)

    def col(c, prev=False):
        if prev:
            return pl.BlockSpec((t, HEAD_DIM), lambda h, n: (jnp.maximum(n - 1, 0), c + h))
        return pl.BlockSpec((t, HEAD_DIM), lambda h, n: (n, c + h))

    out = jax.ShapeDtypeStruct((s, N_KV_HEADS * HEAD_DIM), F32)
    return pl.pallas_call(
        body, name=name, out_shape=(out, out),
        grid=(N_KV_HEADS, nt),
        in_specs=[col(0), col(N_KV_HEADS), col(2 * N_KV_HEADS), col(kcol), col(kcol, True), col(vcol), col(vcol, True)],
        out_specs=(col(0), col(0)),
        scratch_shapes=[pltpu.VMEM((N_PATTERNS, t, HEAD_DIM), F32), pltpu.VMEM((N_PATTERNS, t, HEAD_DIM), F32)],
        compiler_params=_cp(("parallel", "arbitrary"), 48),
    )(qkr, qkr, qkr, qkr, qkr, proj, proj)


def _attn_bwd(qkr, proj, attn, lse, dcat, *, name, comm=None):
    s = qkr.shape[0]
    t = ATTN_TILE
    nt = s // t
    scale = HEAD_DIM ** -0.5
    kcol, vcol = N_PATTERNS * N_KV_HEADS, (N_PATTERNS + 1) * N_KV_HEADS
    blocks = _attn_blocks()

    def body(q0, q1, q2, kc_ref, kp_ref, vc_ref, vp_ref, o_ref, l_ref, do_ref,
             dq0, dq1, dq2, dk_ref, dv_ref, ck, cv, tkc, tvc, tkp, tvp):
        n = pl.program_id(1)
        for ref in (tkc, tvc, tkp, tvp):
            ref[...] = jnp.zeros_like(ref)

        @pl.when(n < nt)
        def _():
            band, band_first = _band_masks(n)
            q_refs, dq_refs = (q0, q1, q2), (dq0, dq1, dq2)
            for g, start, dil, prev_in_tile, prev_start in blocks:
                idx = pl.ds(start, SPAN, stride=dil) if dil > 1 else pl.ds(start, SPAN)
                q = q_refs[g][idx, :].astype(BF16)
                if prev_in_tile is not None:
                    kp, vp, mask = _rows(kc_ref, prev_in_tile, dil), _rows(vc_ref, prev_in_tile, dil), band
                else:
                    kp, vp, mask = _rows(kp_ref, prev_start, dil), _rows(vp_ref, prev_start, dil), band_first
                kk = jnp.concatenate([kp, kc_ref[idx, :]], axis=0).astype(BF16)
                vv = jnp.concatenate([vp, vc_ref[idx, :]], axis=0).astype(BF16)
                do = do_ref[idx, :]
                dsum = jnp.sum(do * o_ref[idx, :], axis=-1, keepdims=True)
                lrow = l_ref[idx, :][:, :1]
                dob = do.astype(BF16)
                sc = lax.dot_general(q, kk, (((1,), (1,)), ((), ())), preferred_element_type=F32) * scale
                p = jnp.where(mask, jnp.exp(sc - lrow), 0.0)
                dp = lax.dot_general(dob, vv, (((1,), (1,)), ((), ())), preferred_element_type=F32)
                ds = (p * (dp - dsum) * scale).astype(BF16)
                pb = p.astype(BF16)
                dq_refs[g][idx, :] = jnp.dot(ds, kk, preferred_element_type=F32)
                dkk = lax.dot_general(ds, q, (((0,), (0,)), ((), ())), preferred_element_type=F32)
                dvv = lax.dot_general(pb, dob, (((0,), (0,)), ((), ())), preferred_element_type=F32)
                tkc[idx, :] += dkk[SPAN:]
                tvc[idx, :] += dvv[SPAN:]
                if prev_in_tile is not None:
                    pidx = pl.ds(prev_in_tile, SPAN, stride=dil) if dil > 1 else pl.ds(prev_in_tile, SPAN)
                    tkc[pidx, :] += dkk[:SPAN]
                    tvc[pidx, :] += dvv[:SPAN]
                else:
                    pidx = pl.ds(prev_start, SPAN, stride=dil) if dil > 1 else pl.ds(prev_start, SPAN)
                    tkp[pidx, :] += dkk[:SPAN]
                    tvp[pidx, :] += dvv[:SPAN]

        @pl.when(n > 0)
        def _():
            dk_ref[...] = ck[...] + tkp[...]
            dv_ref[...] = (cv[...] + tvp[...]).astype(BF16)

        ck[...] = tkc[...]
        cv[...] = tvc[...]

    def col(c, prev=False):
        if prev:
            return pl.BlockSpec((t, HEAD_DIM), lambda h, n: (jnp.maximum(jnp.minimum(n, nt - 1) - 1, 0), c + h))
        return pl.BlockSpec((t, HEAD_DIM), lambda h, n: (jnp.minimum(n, nt - 1), c + h))

    kv_out = pl.BlockSpec((t, HEAD_DIM), lambda h, n: (jnp.maximum(n - 1, 0), h))
    tile = pltpu.VMEM((t, HEAD_DIM), F32)
    per_head = jax.ShapeDtypeStruct((s, N_KV_HEADS * HEAD_DIM), F32)
    return _call(
        body, name=name,
        out_shape=(per_head, per_head, per_head, per_head, jax.ShapeDtypeStruct((s, N_KV_HEADS * HEAD_DIM), BF16)),
        grid=(N_KV_HEADS, nt + 1),
        in_specs=[col(0), col(N_KV_HEADS), col(2 * N_KV_HEADS), col(kcol), col(kcol, True), col(vcol), col(vcol, True),
                  col(0), col(0), col(0)],
        out_specs=(col(0), col(0), col(0), kv_out, kv_out),
        scratch_shapes=[tile] * 6,
        args=(qkr, qkr, qkr, qkr, qkr, proj, proj, attn, lse, dcat), vmem_mib=48, comm=comm)


GELU_C0 = 0.7978845608028654
GELU_C1 = 0.044715


def _softplus_neg(lam):
    y = jnp.exp(-jnp.abs(lam))
    w = 1.0 + y
    log1p = jnp.where(w == 1.0, y, jnp.log(w) * (y / jnp.where(w == 1.0, 1.0, w - 1.0)))
    return jnp.maximum(-lam, 0.0) + log1p


def _down(cur, prev, k, row):
    if k == 0:
        return cur
    return jnp.where(row < k, pltpu.roll(prev, k, 0), pltpu.roll(cur, k, 0))


def _up(cur, nxt, k, row, tt):
    if k == 0:
        return cur
    return jnp.where(row >= tt - k, pltpu.roll(nxt, tt - k, 0), pltpu.roll(cur, tt - k, 0))


def _lru_gates(x, xp, cw, cb, wr, br, wi, bi, lam, row):
    shifts = [_down(x, xp, k, row) for k in range(CONV_WIDTH)]
    xc = cb
    for j in range(CONV_WIDTH):
        xc = xc + cw[j:j + 1, :] * shifts[CONV_WIDTH - 1 - j]
    xcb = xc.astype(BF16)
    r = jax.nn.sigmoid(jnp.dot(xcb, wr, preferred_element_type=F32) + br)
    i = jax.nn.sigmoid(jnp.dot(xcb, wi, preferred_element_type=F32) + bi)
    c = -LRU_C * _softplus_neg(lam)
    la = c * r
    a = jnp.exp(la)
    mult = jnp.sqrt(jnp.tanh(-la) * (a * a + 1.0))
    return shifts, xc, xcb, r, i, c, a, mult


def _lru_fwd(proj, cw, cb, wr, br, wi, bi, lam, *, tt, name):
    s = proj.shape[0]
    nblk = wr.shape[0]
    c = nblk * LANES
    tt = min(tt, s)
    xcol0 = (N_PATTERNS + 2) * N_KV_HEADS
    ycol0 = xcol0 + nblk

    def body(x_ref, y_ref, cw_ref, cb_ref, wr_ref, br_ref, wi_ref, bi_ref, lam_ref, rec_ref, h_ref, xprev, hc):
        n = pl.program_id(1)

        @pl.when(n == 0)
        def _():
            xprev[...] = jnp.zeros_like(xprev)
            hc[...] = jnp.zeros_like(hc)

        row = lax.broadcasted_iota(jnp.int32, (tt, LANES), 0)
        x = x_ref[...]
        _, xc, _, _, i, _, a, mult = _lru_gates(
            x, xprev[...], cw_ref[...], cb_ref[...], wr_ref[0].astype(BF16), br_ref[...],
            wi_ref[0].astype(BF16), bi_ref[...], lam_ref[...], row)
        av, bv = a, mult * (i * xc)
        k = 1
        while k < tt:
            bs = jnp.where(row < k, 0.0, pltpu.roll(bv, k, 0))
            as_ = jnp.where(row < k, 1.0, pltpu.roll(av, k, 0))
            bv = bv + av * bs
            av = av * as_
            k *= 2
        h = bv + av * hc[0:1, :]
        hc[...] = jnp.broadcast_to(h[tt - 1:tt, :], hc.shape)
        h_ref[...] = h
        y = y_ref[...]
        gel = 0.5 * y * (1.0 + jnp.tanh(GELU_C0 * (y + GELU_C1 * y * y * y)))
        rec_ref[...] = (h * gel).astype(BF16)
        xprev[...] = x

    vec = pl.BlockSpec((1, LANES), lambda b, n: (0, b))
    wblk = pl.BlockSpec((1, LANES, LANES), lambda b, n: (b, 0, 0))
    out = pl.BlockSpec((tt, LANES), lambda b, n: (n, b))
    return pl.pallas_call(
        body, name=name,
        out_shape=(jax.ShapeDtypeStruct((s, c), BF16), jax.ShapeDtypeStruct((s, c), F32)),
        grid=(nblk, s // tt),
        in_specs=[pl.BlockSpec((tt, LANES), lambda b, n: (n, xcol0 + b)),
                  pl.BlockSpec((tt, LANES), lambda b, n: (n, ycol0 + b)),
                  pl.BlockSpec((CONV_WIDTH, LANES), lambda b, n: (0, b)), vec, wblk, vec, wblk, vec, vec],
        out_specs=(out, out),
        scratch_shapes=[pltpu.VMEM((tt, LANES), F32), pltpu.VMEM((8, LANES), F32)],
        compiler_params=_cp(("parallel", "arbitrary"), 32),
    )(proj, proj, cw, cb, wr, br, wi, bi, lam)


def _lru_bwd(proj, hseq, dcat, cw, cb, wr, br, wi, bi, lam, *, tt, name, comm=None):
    s = proj.shape[0]
    nblk = wr.shape[0]
    c = nblk * LANES
    tt = min(tt, s)
    nt = s // tt
    xcol0 = (N_PATTERNS + 2) * N_KV_HEADS
    ycol0 = xcol0 + nblk
    rcol0 = N_KV_HEADS

    def body(x_ref, xp_ref, y_ref, h_ref, hp_ref, dr_ref, cw_ref, cb_ref, wr_ref, br_ref, wi_ref, bi_ref, lam_ref,
             dx_ref, dy_ref, dcw_ref, dcb_ref, dwr_ref, dbr_ref, dwi_ref, dbi_ref, dlam_ref, dxc_next, gcar, acar):
        n = pl.program_id(1)
        rt = nt - 1 - n

        @pl.when(n == 0)
        def _():
            for ref in (dxc_next, gcar, acar, dcw_ref, dcb_ref, dwr_ref, dbr_ref, dwi_ref, dbi_ref, dlam_ref):
                ref[...] = jnp.zeros_like(ref)

        row = lax.broadcasted_iota(jnp.int32, (tt, LANES), 0)
        x = x_ref[...]
        xp = jnp.where(rt > 0, xp_ref[...], 0.0)
        cwv = cw_ref[...]
        wrb, wib = wr_ref[0].astype(BF16), wi_ref[0].astype(BF16)
        lam_v = lam_ref[...]
        shifts, xc, xcb, r, i, cc, a, mult = _lru_gates(x, xp, cwv, cb_ref[...], wrb, br_ref[...], wib, bi_ref[...],
                                                        lam_v, row)
        h = h_ref[...]
        hp_last = jnp.where(rt > 0, hp_ref[7:8, :], 0.0)
        hprev = jnp.where(row < 1, hp_last, pltpu.roll(h, 1, 0))
        y = y_ref[...]
        y2 = y * y
        th = jnp.tanh(GELU_C0 * (y + GELU_C1 * y2 * y))
        gel = 0.5 * y * (1.0 + th)
        dgel = 0.5 * (1.0 + th) + 0.5 * y * (1.0 - th * th) * GELU_C0 * (1.0 + 3.0 * GELU_C1 * y2)
        drec = dr_ref[...]
        dy_ref[...] = (drec * h * dgel).astype(BF16)
        av = jnp.where(row >= tt - 1, acar[0:1, :], pltpu.roll(a, tt - 1, 0))
        bv = drec * gel
        k = 1
        while k < tt:
            bs = jnp.where(row >= tt - k, 0.0, pltpu.roll(bv, tt - k, 0))
            as_ = jnp.where(row >= tt - k, 1.0, pltpu.roll(av, tt - k, 0))
            bv = bv + av * bs
            av = av * as_
            k *= 2
        g = bv + av * gcar[0:1, :]
        gcar[...] = jnp.broadcast_to(g[0:1, :], gcar.shape)
        acar[...] = jnp.broadcast_to(a[0:1, :], acar.shape)
        da = g * hprev
        d_ixc = g * mult
        dmult = g * (i * xc)
        di = d_ixc * xc
        dxc = d_ixc * i
        a2 = a * a
        dla = da * a - dmult * (a2 / mult)
        dr = dla * cc
        dsp = jnp.sum(dla * r, axis=0, keepdims=True) * (-LRU_C)
        dlam_ref[...] += dsp * (-jax.nn.sigmoid(-lam_v))
        dzr = dr * r * (1.0 - r)
        dzi = di * i * (1.0 - i)
        dbr_ref[...] += jnp.sum(dzr, axis=0, keepdims=True)
        dbi_ref[...] += jnp.sum(dzi, axis=0, keepdims=True)
        dzrb, dzib = dzr.astype(BF16), dzi.astype(BF16)
        tn = (((0,), (0,)), ((), ()))
        ntd = (((1,), (1,)), ((), ()))
        dwr_ref[0] += lax.dot_general(xcb, dzrb, tn, preferred_element_type=F32)
        dwi_ref[0] += lax.dot_general(xcb, dzib, tn, preferred_element_type=F32)
        dxc = (dxc + lax.dot_general(dzrb, wrb, ntd, preferred_element_type=F32)
               + lax.dot_general(dzib, wib, ntd, preferred_element_type=F32))
        dcb_ref[...] += jnp.sum(dxc, axis=0, keepdims=True)
        dcw_ref[...] += jnp.concatenate(
            [jnp.sum(dxc * shifts[CONV_WIDTH - 1 - j], axis=0, keepdims=True) for j in range(CONV_WIDTH)], axis=0)
        nxt = dxc_next[...]
        dx = cwv[0:1, :] * _up(dxc, nxt, CONV_WIDTH - 1, row, tt)
        for j in range(1, CONV_WIDTH):
            dx = dx + cwv[j:j + 1, :] * _up(dxc, nxt, CONV_WIDTH - 1 - j, row, tt)
        dx_ref[...] = dx.astype(BF16)
        dxc_next[...] = dxc

    def tile(col0, prev=False):
        if prev:
            return pl.BlockSpec((tt, LANES), lambda b, n: (jnp.maximum(nt - 2 - n, 0), col0 + b))
        return pl.BlockSpec((tt, LANES), lambda b, n: (nt - 1 - n, col0 + b))

    vec = pl.BlockSpec((1, LANES), lambda b, n: (0, b))
    wblk = pl.BlockSpec((1, LANES, LANES), lambda b, n: (b, 0, 0))
    cwblk = pl.BlockSpec((CONV_WIDTH, LANES), lambda b, n: (0, b))
    hp8 = pl.BlockSpec((8, LANES), lambda b, n: (jnp.maximum((nt - 1 - n) * (tt // 8) - 1, 0), b))
    vshape = jax.ShapeDtypeStruct((1, c), F32)
    wshape = jax.ShapeDtypeStruct((nblk, LANES, LANES), F32)
    return _call(
        body, name=name,
        out_shape=(jax.ShapeDtypeStruct((s, c), BF16), jax.ShapeDtypeStruct((s, c), BF16),
                   jax.ShapeDtypeStruct((CONV_WIDTH, c), F32), vshape, wshape, vshape, wshape, vshape, vshape),
        grid=(nblk, nt),
        in_specs=[tile(xcol0), tile(xcol0, True), tile(ycol0), tile(0), hp8, tile(rcol0),
                  cwblk, vec, wblk, vec, wblk, vec, vec],
        out_specs=(tile(0), tile(0), cwblk, vec, wblk, vec, wblk, vec, vec),
        scratch_shapes=[pltpu.VMEM((tt, LANES), F32), pltpu.VMEM((8, LANES), F32), pltpu.VMEM((8, LANES), F32)],
        args=(proj, proj, proj, hseq, hseq, dcat, cw, cb, wr, br, wi, bi, lam), vmem_mib=32, comm=comm)


ROW_BLOCKS = (512, 256, 176, 128, 64, 32, 16, 8)


def _adamw(w, m, v, gparts, *, name):
    r, c = w.shape
    npart = gparts.shape[0]
    br = _pick(r, ROW_BLOCKS)
    c1 = 1.0 - ADAM_B1 ** ADAM_STEP
    c2 = 1.0 - ADAM_B2 ** ADAM_STEP

    def body(w_ref, m_ref, v_ref, g_ref, go_ref, d_ref, mo_ref, vo_ref):
        g = g_ref[0].astype(F32)
        for q in range(1, npart):
            g = g + g_ref[q].astype(F32)
        mn = ADAM_B1 * m_ref[...] + (1.0 - ADAM_B1) * g
        vn = ADAM_B2 * v_ref[...] + (1.0 - ADAM_B2) * (g * g)
        go_ref[...] = g
        mo_ref[...] = mn
        vo_ref[...] = vn
        d_ref[...] = -ADAM_LR * ((mn / c1) / (jnp.sqrt(vn / c2) + ADAM_EPS) + ADAM_WD * w_ref[...])

    blk = pl.BlockSpec((br, c), lambda i: (i, 0))
    out = jax.ShapeDtypeStruct((r, c), F32)
    return pl.pallas_call(
        body, name=name, out_shape=(out, out, out, out), grid=(r // br,),
        in_specs=[blk, blk, blk, pl.BlockSpec((npart, br, c), lambda i: (0, i, 0))],
        out_specs=(blk, blk, blk, blk),
        compiler_params=_cp(("parallel",), 48),
    )(w, m, v, gparts)


def _sum_parts(parts, *, name):
    npart, r, c = parts.shape
    br = next((b for b in range(min(r, 2048) // 8 * 8, 0, -8) if r % b == 0), r)

    def body(p_ref, o_ref):
        acc = p_ref[0]
        for q in range(1, npart):
            acc = acc + p_ref[q]
        o_ref[...] = acc

    return pl.pallas_call(
        body, name=name, out_shape=jax.ShapeDtypeStruct((r, c), F32), grid=(r // br,),
        in_specs=[pl.BlockSpec((npart, br, c), lambda i: (0, i, 0))],
        out_specs=pl.BlockSpec((br, c), lambda i: (i, 0)),
        compiler_params=_cp(("parallel",), 48),
    )(parts)


HBM = pl.BlockSpec(memory_space=pltpu.HBM)


def _mesh_pos():
    return lax.axis_index("x"), lax.axis_index("y"), lax.axis_index("c")


def _gather_comm(shards):
    na = len(shards)

    def parts(x_refs, out_refs, sems):
        send_sems, recv_sems, local_sems = sems
        x, y, c = _mesh_pos()
        me, sibling = (x, y, c), (x, y, 1 - c)
        chips = [(1 - x, y), (x, 1 - y), (1 - x, 1 - y)]

        def copy(a, k, block, to, src=None):
            px, py, pc = block
            dst = out_refs[a].at[4 * px + 2 * py + pc]
            return pltpu.make_async_remote_copy(
                src_ref=dst if src is None else src, dst_ref=dst,
                send_sem=send_sems.at[a, k], recv_sem=recv_sems.at[a, k],
                device_id=to, device_id_type=MESH)

        def mine(a):
            return pltpu.make_async_copy(x_refs[a], out_refs[a].at[4 * x + 2 * y + c], local_sems.at[a])

        def first(a):
            return [copy(a, 0, me, sibling, src=x_refs[a])] + [
                copy(a, 1 + j, me, (*chip, c), src=x_refs[a]) for j, chip in enumerate(chips)]

        def passed(a, j):
            return copy(a, 4 + j, (*chips[j], c), sibling)

        return me, sibling, chips, c, copy, mine, first, passed

    def start(x_refs, out_refs, sems):
        *_, mine, first, _ = parts(x_refs, out_refs, sems)
        for a in range(na):
            mine(a).start()
            for cp in first(a):
                cp.start()

    def mid(x_refs, out_refs, sems):
        me, _, chips, c, copy, _, _, passed = parts(x_refs, out_refs, sems)
        for j, chip in enumerate(chips):
            for a in range(na):
                copy(a, 1 + j, (*chip, c), me).wait_recv()
                passed(a, j).start()

    def end(x_refs, out_refs, sems):
        me, sibling, chips, c, copy, mine, first, passed = parts(x_refs, out_refs, sems)
        for a in range(na):
            copy(a, 0, sibling, me).wait_recv()
            for j, chip in enumerate(chips):
                copy(a, 4 + j, (*chip, 1 - c), me).wait_recv()
        for a in range(na):
            for cp in first(a) + [passed(a, j) for j in range(3)]:
                cp.wait_send()
            mine(a).wait()

    return _Comm(
        shards, [jax.ShapeDtypeStruct((N_DEV,) + a.shape, a.dtype) for a in shards],
        [pltpu.SemaphoreType.DMA((na, 7)), pltpu.SemaphoreType.DMA((na, 7)), pltpu.SemaphoreType.DMA((na,))],
        start, end, mid)


def _scatter_comm(g8s):
    na = len(g8s)

    def parts(g_refs, buf_refs, sems):
        send_sems, recv_sems, local_sems = sems
        x, y, c = _mesh_pos()
        me_idx = 4 * x + 2 * y + c

        def copy(a, k, slot):
            peer, peer_idx = _scatter_peer(k, x, y, c)
            return pltpu.make_async_remote_copy(
                src_ref=g_refs[a].at[peer_idx], dst_ref=buf_refs[a].at[me_idx if slot is None else slot],
                send_sem=send_sems.at[a, k - 1], recv_sem=recv_sems.at[a, k - 1],
                device_id=peer, device_id_type=MESH)

        def mine(a):
            return pltpu.make_async_copy(g_refs[a].at[me_idx], buf_refs[a].at[me_idx], local_sems.at[a])

        return x, y, c, copy, mine

    def start(g_refs, buf_refs, sems):
        *_, copy, mine = parts(g_refs, buf_refs, sems)
        for a in range(na):
            mine(a).start()
            for k in range(1, N_DEV):
                copy(a, k, None).start()

    def end(g_refs, buf_refs, sems):
        x, y, c, copy, mine = parts(g_refs, buf_refs, sems)
        for a in range(na):
            for k in range(1, N_DEV):
                copy(a, k, _scatter_peer(k, x, y, c)[1]).wait_recv()
        for a in range(na):
            for k in range(1, N_DEV):
                copy(a, k, None).wait_send()
            mine(a).wait()

    return _Comm(
        g8s, [jax.ShapeDtypeStruct(g.shape, g.dtype) for g in g8s],
        [pltpu.SemaphoreType.DMA((na, N_DEV - 1)), pltpu.SemaphoreType.DMA((na, N_DEV - 1)),
         pltpu.SemaphoreType.DMA((na,))],
        start, end)


def _scatter_peer(k, x, y, c):
    px, py, pc = (1 - x if k & 4 else x, 1 - y if k & 2 else y, 1 - c if k & 1 else c)
    return (px, py, pc), 4 * px + 2 * py + pc


def _scatter_start_comm(g8s):
    na = len(g8s)
    arrays = []
    for g8 in g8s:
        arrays += [g8, lax.empty(g8.shape, g8.dtype)]

    def local(refs, sems, a, me_idx):
        return pltpu.make_async_copy(refs[2 * a].at[me_idx], refs[2 * a + 1].at[me_idx], sems[0].at[a])

    def start(refs, outs, sems):
        send_sems, recv_sems = outs[:2]
        x, y, c = _mesh_pos()
        me_idx = 4 * x + 2 * y + c
        for a in range(na):
            local(refs, sems, a, me_idx).start()
            for k in range(1, N_DEV):
                peer, peer_idx = _scatter_peer(k, x, y, c)
                pltpu.make_async_remote_copy(
                    src_ref=refs[2 * a].at[peer_idx], dst_ref=refs[2 * a + 1].at[me_idx],
                    send_sem=send_sems.at[a * (N_DEV - 1) + k - 1], recv_sem=recv_sems.at[a * (N_DEV - 1) + k - 1],
                    device_id=peer, device_id_type=MESH).start()

    def end(refs, outs, sems):
        x, y, c = _mesh_pos()
        for a in range(na):
            local(refs, sems, a, 4 * x + 2 * y + c).wait()

    sem_shape = pltpu.SemaphoreType.DMA((na * (N_DEV - 1),))
    return _Comm(arrays, [sem_shape, sem_shape] + [pltpu.HBM(a.shape, a.dtype) for a in arrays],
                 [pltpu.SemaphoreType.DMA((na,))], start, end, split=True)


def _scatter_wait(started, after, *, name):
    send_sems, recv_sems, *arrays = started
    na = len(arrays) // 2

    def body(*refs):
        send_ref, recv_ref = refs[2 * na], refs[2 * na + 1]
        x, y, c = _mesh_pos()
        for a in range(na):
            for k in range(1, N_DEV):
                peer, peer_idx = _scatter_peer(k, x, y, c)
                pltpu.make_async_remote_copy(
                    src_ref=refs[2 * a].at[peer_idx], dst_ref=refs[2 * a + 1].at[peer_idx],
                    send_sem=send_ref.at[a * (N_DEV - 1) + k - 1], recv_sem=recv_ref.at[a * (N_DEV - 1) + k - 1],
                    device_id=peer, device_id_type=MESH).wait()

    sem = pl.BlockSpec(memory_space=pltpu.SEMAPHORE)
    outs = pl.pallas_call(
        body, name=name, out_shape=tuple(pltpu.HBM(a.shape, a.dtype) for a in arrays),
        in_specs=[HBM] * (2 * na) + [sem, sem, pl.BlockSpec(memory_space=pl.ANY)], out_specs=(HBM,) * (2 * na),
        input_output_aliases={k: k for k in range(2 * na)},
        compiler_params=pltpu.CompilerParams(has_side_effects=pltpu.SideEffectType.DATAFLOW_SIDE_EFFECTING),
    )(*arrays, send_sems, recv_sems, after)
    return outs[1::2]


BIG_WEIGHTS = ("ffn1_w_gate", "ffn1_w_up", "ffn1_w_down", "w_in", "w_out",
               "ffn2_w_gate", "ffn2_w_up", "ffn2_w_down", "w_ple_proj", "w_ple_gate")
COLUMN_SHARDED = ("ffn1_w_gate", "ffn1_w_up", "w_in", "ffn2_w_gate", "ffn2_w_up", "w_ple_proj", "conv_w")
SMALL_WEIGHTS = ("ln1_g", "ln1_b", "conv_b", "w_rgate", "b_rgate", "w_igate", "b_igate", "lru_lambda",
                 "ln2_g", "ln2_b", "ln3_g", "ln3_b")
SMALL_GRADS = SMALL_WEIGHTS + ("conv_w",)


class _Exchange:
    def __init__(self, full):
        self.full = dict(full)
        self.grads = {}

    def __getitem__(self, name):
        return self.full[name]

    def first_gather(self, x):
        return _to_bf16(x, bm=1024, name="x_bf16")

    def gather(self, names):
        return None, None

    def scatter(self, names):
        return None, None

    def scatter_start(self, names):
        return None, None

    def gather_small(self):
        return None, None


class _MeshExchange(_Exchange):
    def __init__(self, full, shards, conv_w):
        super().__init__(full)
        self.shards = shards
        self.conv_w = conv_w
        self.reduced = {}
        self.started = {}
        self.small_parts = None

    def first_gather(self, x):
        xb, (gate, conv_all) = _to_bf16(
            x, bm=1024, name="x_bf16", comm=_gather_comm([self.shards["ffn1_w_gate"], self.conv_w]))
        self.take("ffn1_w_gate", gate)
        self.full["conv_w"] = _to_full("conv_w", conv_all)
        return xb

    def gather(self, names):
        def done(outs):
            for n, o in zip(names, outs):
                self.take(n, o)
        return _gather_comm([self.shards[n] for n in names]), done

    def take(self, name, gathered):
        self.full[name] = gathered.reshape((N_DEV * gathered.shape[1],) + gathered.shape[2:])

    def scatter(self, names):
        def done(outs):
            self.reduced.update(zip(names, outs))
        return _scatter_comm([_to_owner_blocks(n, self.grads[n]) for n in names]), done

    def scatter_start(self, names):
        def done(outs):
            self.started[names] = outs
        return _scatter_start_comm([_to_owner_blocks(n, self.grads[n]) for n in names]), done

    def finish(self, after):
        for names, started in self.started.items():
            self.reduced.update(zip(names, _scatter_wait(started, after, name=f"scatter_wait_{names[0]}")))

    def gather_small(self):
        def done(outs):
            self.small_parts, = outs
        packed = jnp.concatenate([_rows128(self.grads[n]) for n in SMALL_GRADS], axis=0)
        return _gather_comm([packed]), done


def _carried(comm_done, call):
    comm, done = comm_done
    res = call(comm)
    if comm is None:
        return res
    res, outs = res
    done(outs)
    return res


def _dw(a, b, *, scale=1.0, name, comm=None):
    k, m = a.shape
    n = b.shape[1]
    return _mm(a, b, ta=True, scale=scale, out_dtype=BF16, bm=_pick(m, (1024, 512, 256, 128)),
               bn=_pick(n, (512, 256, 128)), bk=k, name=name, comm=comm)


def _ffn_bwd(ex, names, saved, xb_in, dz, dzb, ln_in, tag, on_dwd=None, on_dh=None, on_dwu=None, on_dx=None):
    gate, up, down = names
    g, u, h, _, _ = saved
    f = ex[gate].shape[0]

    def request(fn):
        return (None, None) if fn is None else fn(ex)

    ex.grads[down] = _carried(request(on_dwd), lambda c: _dw(h, dzb, scale=0.5, name=f"{tag}_dwd", comm=c))
    dg, du = _carried(request(on_dh), lambda c: _ffn_bwd_dh(
        dzb, ex[down], g, u, scale=0.5, bm=2048, bn=_pick(f, (512, 256, 128)), name=f"{tag}_dh", chunks=8, comm=c))
    ex.grads[gate] = _dw(xb_in, dg, name=f"{tag}_dwg")
    ex.grads[up] = _carried(request(on_dwu), lambda c: _dw(xb_in, du, name=f"{tag}_dwu", comm=c))
    d = dz.shape[1]
    dx = _carried(request(on_dx), lambda c: _ffn_dx(
        dg, du, ex[gate], ex[up], dz, extra_scale=DEEPNORM_ALPHA,
        bm=1024, bn=_pick(d, (512, 256, 128)), name=f"{tag}_dx", comm=c))
    return dx if ln_in is None else _ln_bwd(dx, *ln_in, bm=256, name=f"{tag}_ln_bwd")


def _local_step(x, p, target, positions, w):
    s, d = x.shape
    tabs = _rope_tables(positions)
    xb = w.first_gather(x)
    f = w["ffn1_w_gate"].shape[0]
    ffn_bn, ln_bn, ln_bn_short_k = _pick(f, (512, 256, 128)), _pick(d, (512, 256, 128)), _pick(d, (1024, 512, 256, 128))
    gate1 = _carried(w.gather(("ffn1_w_up",)), lambda c: _mm(
        xb, w["ffn1_w_gate"], tb=True, out_dtype=BF16, bm=1024, bn=ffn_bn, bk=d, name="ffn1_gate", comm=c))
    g1, u1, h1 = _carried(w.gather(("ffn1_w_down", "w_in", "w_out")), lambda c: _ffn_up(
        xb, gate1, w["ffn1_w_up"], bm=1024, bn=ffn_bn, name="ffn1_up", gate_done=True, comm=c))
    x1, x1b, xh1, rs1 = _carried(w.gather(("ffn2_w_gate", "ffn2_w_up")), lambda c: _mm_ln(
        h1, w["ffn1_w_down"], x, w["ln1_g"], w["ln1_b"], res_scale=DEEPNORM_ALPHA, mm_scale=0.5,
        bm=512, bn=ln_bn, name="ffn1_down_ln", comm=c))
    sv1 = (g1, u1, h1, xh1, rs1)
    pw = w["w_in"].shape[0]
    proj = _carried(w.gather(("ffn2_w_down", "w_ple_gate", "w_ple_proj")), lambda c: _mm(
        x1b, w["w_in"], tb=True, bm=1024, bn=_pick(pw, (512, 256, 128)), bk=d, name="in_proj", comm=c))
    nqk = (N_PATTERNS + 1) * N_KV_HEADS
    qkr = _rotary(proj, tabs, n_cols=nqk, inverse=False, out_dtype=F32, bs=1024, name="rotary")
    attn, lse = _attn_fwd(qkr, proj, name="attn_fwd")
    lru_w = (w["conv_w"], w["conv_b"], w["w_rgate"], w["b_rgate"], w["w_igate"], w["b_igate"], w["lru_lambda"])
    rec, hseq = _lru_fwd(proj, *lru_w, tt=512, name="lru_fwd")
    cat = jnp.concatenate([attn.astype(BF16), rec], axis=1)
    x2, x2b, xh2, rs2 = _mm_ln(cat, w["w_out"], x1, w["ln2_g"], w["ln2_b"], res_scale=DEEPNORM_ALPHA, mm_scale=1.0,
                               bm=512, bn=ln_bn_short_k, name="out_proj_ln")
    g2, u2, h2 = _ffn_up(x2b, w["ffn2_w_gate"], w["ffn2_w_up"], bm=1024, bn=ffn_bn, name="ffn2_up")
    x3, x3b, xh3, rs3 = _mm_ln(h2, w["ffn2_w_down"], x2, w["ln3_g"], w["ln3_b"], res_scale=DEEPNORM_ALPHA, mm_scale=0.5,
                               bm=512, bn=ln_bn, name="ffn2_down_ln")
    sv3 = (g2, u2, h2, xh3, rs3)
    lsum, dy, dgate, dple = _ple_loss(x3, x3b, p, w["w_ple_gate"], w["w_ple_proj"], target,
                                      bm=1024, bn=_pick(d, (512, 256, 128)), name="ple_loss")
    grads = w.grads
    grads["w_ple_gate"] = _dw(x3b, dgate, name="dw_ple_gate")
    grads["w_ple_proj"] = _dw(p, dple, name="dw_ple_proj")
    dz3, dz3b, grads["ln3_g"], grads["ln3_b"] = _carried(w.scatter(("w_ple_gate", "w_ple_proj")), lambda c: _mm_dx(
        dgate, w["w_ple_gate"], dy, xh3, rs3, w["ln3_g"], extra_scale=1.0, bm=512, bn=ln_bn_short_k,
        name="ple_dx", tb=True, comm=c))
    dz2, dz2b, grads["ln2_g"], grads["ln2_b"] = _ffn_bwd(
        w, ("ffn2_w_gate", "ffn2_w_up", "ffn2_w_down"), sv3, x2b, dz3, dz3b, (xh2, rs2, w["ln2_g"]), "ffn2",
        on_dx=lambda ex: ex.scatter_start(("ffn2_w_down", "ffn2_w_gate", "ffn2_w_up")))
    grads["w_out"] = _dw(cat, dz2b, name="dw_out")
    dcat = _mm(dz2b, w["w_out"], tb=True, bm=1024, bn=_pick(d, (512, 256, 128)), bk=d, name="out_proj_dx")
    dq0, dq1, dq2, dk, dvb = _attn_bwd(qkr, proj, attn, lse, dcat, name="attn_bwd")
    nh = N_KV_HEADS
    dqkv = [_rotary(t, tabs, n_cols=nh, inverse=True, out_dtype=BF16, bs=1024, name=f"rotary_bwd{i}")
            for i, t in enumerate((dq0, dq1, dq2, dk))]
    (dxb, dyb, grads["conv_w"], grads["conv_b"], grads["w_rgate"], grads["b_rgate"], grads["w_igate"],
     grads["b_igate"], grads["lru_lambda"]) = _lru_bwd(proj, hseq, dcat, *lru_w, tt=512, name="lru_bwd")
    dproj = jnp.concatenate(dqkv + [dvb, dxb, dyb], axis=1)
    grads["w_in"] = _dw(x1b, dproj, name="dw_in")
    dz1, dz1b, grads["ln1_g"], grads["ln1_b"] = _carried(w.scatter_start(("w_out", "w_in")), lambda c: _mm_dx(
        dproj, w["w_in"], dz2, xh1, rs1, w["ln1_g"], extra_scale=DEEPNORM_ALPHA,
        bm=512, bn=ln_bn, name="in_proj_dx", comm=c))
    grad_x = _ffn_bwd(w, ("ffn1_w_gate", "ffn1_w_up", "ffn1_w_down"), sv1, xb, dz1, dz1b, None, "ffn1",
                      on_dwd=lambda ex: ex.gather_small(),
                      on_dh=lambda ex: ex.scatter_start(("ffn1_w_down",)),
                      on_dwu=lambda ex: ex.scatter_start(("ffn1_w_gate",)),
                      on_dx=lambda ex: ex.scatter_start(("ffn1_w_up",)))
    return lsum, grad_x


def _to_full(name, gathered):
    if name in COLUMN_SHARDED:
        _, r, c = gathered.shape
        return jnp.transpose(gathered, (1, 0, 2)).reshape(r, N_DEV * c)
    return gathered.reshape((N_DEV * gathered.shape[1],) + gathered.shape[2:])


def _to_owner_blocks(name, full):
    if name in COLUMN_SHARDED:
        r, c = full.shape
        return jnp.transpose(full.reshape(r, N_DEV, c // N_DEV), (1, 0, 2))
    return full.reshape((N_DEV, full.shape[0] // N_DEV) + full.shape[1:])


def _rows128(a):
    flat = a.reshape(-1, LANES)
    pad = (-flat.shape[0]) % 8
    return jnp.pad(flat, ((0, pad), (0, 0))) if pad else flat


def kernel(x, p, positions, ffn1_w_gate, ffn1_w_up, ffn1_w_down, ln1_g, ln1_b, w_in, conv_w, conv_b, w_rgate, b_rgate, w_igate, b_igate, lru_lambda, w_out, ln2_g, ln2_b, ffn2_w_gate, ffn2_w_up, ffn2_w_down, ln3_g, ln3_b, w_ple_proj, w_ple_gate, loss_target, m_ffn1_w_gate, m_ffn1_w_up, m_ffn1_w_down, m_ln1_g, m_ln1_b, m_w_in, m_conv_w, m_conv_b, m_w_rgate, m_b_rgate, m_w_igate, m_b_igate, m_lru_lambda, m_w_out, m_ln2_g, m_ln2_b, m_ffn2_w_gate, m_ffn2_w_up, m_ffn2_w_down, m_ln3_g, m_ln3_b, m_w_ple_proj, m_w_ple_gate, v_ffn1_w_gate, v_ffn1_w_up, v_ffn1_w_down, v_ln1_g, v_ln1_b, v_w_in, v_conv_w, v_conv_b, v_w_rgate, v_b_rgate, v_w_igate, v_b_igate, v_lru_lambda, v_w_out, v_ln2_g, v_ln2_b, v_ffn2_w_gate, v_ffn2_w_up, v_ffn2_w_down, v_ln3_g, v_ln3_b, v_w_ple_proj, v_w_ple_gate):
    names = ("ffn1_w_gate", "ffn1_w_up", "ffn1_w_down", "ln1_g", "ln1_b", "w_in", "conv_w", "conv_b", "w_rgate",
             "b_rgate", "w_igate", "b_igate", "lru_lambda", "w_out", "ln2_g", "ln2_b", "ffn2_w_gate", "ffn2_w_up",
             "ffn2_w_down", "ln3_g", "ln3_b", "w_ple_proj", "w_ple_gate")
    ws = (ffn1_w_gate, ffn1_w_up, ffn1_w_down, ln1_g, ln1_b, w_in, conv_w, conv_b, w_rgate, b_rgate, w_igate, b_igate,
          lru_lambda, w_out, ln2_g, ln2_b, ffn2_w_gate, ffn2_w_up, ffn2_w_down, ln3_g, ln3_b, w_ple_proj, w_ple_gate)
    ms = (m_ffn1_w_gate, m_ffn1_w_up, m_ffn1_w_down, m_ln1_g, m_ln1_b, m_w_in, m_conv_w, m_conv_b, m_w_rgate, m_b_rgate,
          m_w_igate, m_b_igate, m_lru_lambda, m_w_out, m_ln2_g, m_ln2_b, m_ffn2_w_gate, m_ffn2_w_up, m_ffn2_w_down,
          m_ln3_g, m_ln3_b, m_w_ple_proj, m_w_ple_gate)
    vs = (v_ffn1_w_gate, v_ffn1_w_up, v_ffn1_w_down, v_ln1_g, v_ln1_b, v_w_in, v_conv_w, v_conv_b, v_w_rgate, v_b_rgate,
          v_w_igate, v_b_igate, v_lru_lambda, v_w_out, v_ln2_g, v_ln2_b, v_ffn2_w_gate, v_ffn2_w_up, v_ffn2_w_down,
          v_ln3_g, v_ln3_b, v_w_ple_proj, v_w_ple_gate)
    def local(a):
        return a[0] if a.ndim >= 3 else a

    w_loc = {n: local(a) for n, a in zip(names, ws)}
    m_loc = {n: local(a) for n, a in zip(names, ms)}
    v_loc = {n: local(a) for n, a in zip(names, vs)}
    out_shapes = {n: a.shape for n, a in zip(names, ws)}

    shards = {n: (w_loc[n].T if n in COLUMN_SHARDED else w_loc[n]).astype(BF16) for n in BIG_WEIGHTS}
    ex = _MeshExchange({n: w_loc[n] for n in SMALL_WEIGHTS}, shards, w_loc["conv_w"])

    lsum, grad_x = _local_step(x[0], p[0, 0], loss_target[0], positions[0], ex)
    ex.finish(grad_x)
    grads, reduced = ex.grads, ex.reduced
    d_model = x.shape[-1]
    loss = lax.psum(lsum[0, 0] * (0.5 / d_model), ("x", "y", "c"))

    small = SMALL_GRADS
    summed = _sum_parts(ex.small_parts, name="sum_small_grads")
    small_grads, row = {}, 0
    for n in small:
        rows = grads[n].size // LANES
        small_grads[n] = summed[row:row + rows].reshape(grads[n].shape)
        row += rows + (-rows) % 8
    me = 4 * lax.axis_index("x") + 2 * lax.axis_index("y") + lax.axis_index("c")
    cw_cols = w_loc["conv_w"].shape[1]
    small_grads["conv_w"] = lax.dynamic_slice_in_dim(small_grads["conv_w"], me * cw_cols, cw_cols, axis=1)

    out_g, out_d, out_m, out_v = {}, {}, {}, {}
    for n in names:
        wl, ml, vl = w_loc[n], m_loc[n], v_loc[n]
        shape = wl.shape
        if n in BIG_WEIGHTS:
            gparts = reduced[n]
        else:
            gparts = small_grads[n].reshape((1,) + shape)
        if wl.ndim == 3:
            wl, ml, vl = (t.reshape(-1, shape[-1]) for t in (wl, ml, vl))
            gparts = gparts.reshape(gparts.shape[0], -1, shape[-1])
        res = _adamw(wl, ml, vl, gparts, name=f"adamw_{n}")
        out_g[n], out_d[n], out_m[n], out_v[n] = (t.reshape(out_shapes[n]) for t in res)

    return (loss, grad_x[None], *[out_g[n] for n in names], *[out_d[n] for n in names],
            *[out_m[n] for n in names], *[out_v[n] for n in names])
```

```python
import jax
import jax.numpy as jnp
from jax import lax
from jax.experimental import pallas as pl
from jax.experimental.pallas import tpu as pltpu

F32 = jnp.float32
BF16 = jnp.bfloat16

N_DEV = 8
LANES = 128
MIB = 1 << 20

HEAD_DIM = 128
N_KV_HEADS = 4
DILATIONS = (1, 4, 16)
N_PATTERNS = 3
SPAN = 128
ROT_DIMS = 32
ROPE_THETA = 500000.0
LRU_C = 8.0
CONV_WIDTH = 4
LN_EPS = 1e-5
DEEPNORM_ALPHA = 2.0 ** 0.25
ATTN_TILE = SPAN * DILATIONS[-1]

ADAM_LR = 0.001
ADAM_B1 = 0.9
ADAM_B2 = 0.999
ADAM_EPS = 1e-08
ADAM_WD = 0.01
ADAM_STEP = 10

MESH = pl.DeviceIdType.MESH
NT_DIMS = (((1,), (1,)), ((), ()))
EPILOGUE_ROWS = 64


def _cp(semantics, vmem_mib):
    return pltpu.CompilerParams(dimension_semantics=semantics, vmem_limit_bytes=vmem_mib * MIB)


def _pick(n, candidates):
    for c in candidates:
        if n % c == 0:
            return c
    return n


class _Comm:
    def __init__(self, arrays, out_shapes, scratch, start, end, mid=None, split=False):
        self.arrays, self.out_shapes, self.scratch = list(arrays), list(out_shapes), list(scratch)
        self.start, self.mid, self.end, self.split = start, mid, end, split


def _call(body, *, name, grid, in_specs, out_specs, out_shape, args, scratch_shapes=(), vmem_mib, comm=None):
    single = not isinstance(out_shape, (tuple, list))
    out_shape_t = (out_shape,) if single else tuple(out_shape)
    out_specs_t = (out_specs,) if single else tuple(out_specs)
    params = _cp(("arbitrary",) * len(grid), vmem_mib)
    if comm is None:
        res = pl.pallas_call(body, name=name, grid=grid, in_specs=list(in_specs), out_specs=out_specs_t,
                             out_shape=out_shape_t, scratch_shapes=list(scratch_shapes), compiler_params=params)(*args)
        return res[0] if single else res
    n_in, n_out, n_scr = len(args), len(out_shape_t), len(scratch_shapes)
    nci, nco = len(comm.arrays), len(comm.out_shapes)
    total = 1
    for g in grid:
        total *= g

    def wrapped(*refs):
        ins, refs = refs[:n_in], refs[n_in:]
        cin, refs = refs[:nci], refs[nci:]
        outs, refs = refs[:n_out], refs[n_out:]
        cout, refs = refs[:nco], refs[nco:]
        scr, csem = refs[:n_scr], refs[n_scr:]
        step = pl.program_id(0)
        for ax in range(1, len(grid)):
            step = step * grid[ax] + pl.program_id(ax)

        @pl.when(step == 0)
        def _():
            comm.start(cin, cout, csem)

        body(*ins, *outs, *scr)
        if comm.mid is not None:
            @pl.when(step == (3 * total) // 4)
            def _():
                comm.mid(cin, cout, csem)

        @pl.when(step == total - 1)
        def _():
            comm.end(cin, cout, csem)

    hbm = pl.BlockSpec(memory_space=pltpu.HBM)
    if comm.split:
        sem = pl.BlockSpec(memory_space=pltpu.SEMAPHORE)
        n_sems = nco - nci
        res = pl.pallas_call(
            wrapped, name=name, grid=grid,
            in_specs=list(in_specs) + [hbm] * nci,
            out_specs=out_specs_t + (sem,) * n_sems + (hbm,) * nci,
            out_shape=out_shape_t + tuple(comm.out_shapes),
            scratch_shapes=list(scratch_shapes) + comm.scratch,
            input_output_aliases={n_in + k: n_out + n_sems + k for k in range(nci)},
            compiler_params=pltpu.CompilerParams(
                dimension_semantics=("arbitrary",) * len(grid), vmem_limit_bytes=vmem_mib * MIB,
                has_side_effects=pltpu.SideEffectType.DATAFLOW_SIDE_EFFECTING),
        )(*args, *[pltpu.with_memory_space_constraint(a, pltpu.HBM) for a in comm.arrays])
    else:
        res = pl.pallas_call(
            wrapped, name=name, grid=grid,
            in_specs=list(in_specs) + [hbm] * nci,
            out_specs=out_specs_t + (hbm,) * nco,
            out_shape=out_shape_t + tuple(comm.out_shapes),
            scratch_shapes=list(scratch_shapes) + comm.scratch,
            compiler_params=params)(*args, *comm.arrays)
    own, extra = res[:n_out], res[n_out:]
    return (own[0] if single else own), extra


def _mm(a, b, *, ta=False, tb=False, out_dtype=F32, scale=1.0, bm, bn, bk, name, comm=None):
    m, k = (a.shape[1], a.shape[0]) if ta else a.shape
    n = b.shape[0] if tb else b.shape[1]
    bm, bn, bk = min(bm, m), min(bn, n), min(bk, k)
    assert m % bm == 0 and n % bn == 0 and k % bk == 0, (name, m, n, k, bm, bn, bk)
    nk = k // bk
    a_spec = pl.BlockSpec((bk, bm), lambda i, j, kk: (kk, i)) if ta else pl.BlockSpec((bm, bk), lambda i, j, kk: (i, kk))
    b_spec = pl.BlockSpec((bn, bk), lambda i, j, kk: (j, kk)) if tb else pl.BlockSpec((bk, bn), lambda i, j, kk: (kk, j))
    dn = (((0 if ta else 1,), (1 if tb else 0,)), ((), ()))

    def body(a_ref, b_ref, o_ref, *acc):
        part = lax.dot_general(a_ref[...].astype(BF16), b_ref[...].astype(BF16), dn, preferred_element_type=F32)
        if nk == 1:
            o_ref[...] = (part * scale).astype(out_dtype)
            return
        acc_ref, = acc
        kk = pl.program_id(2)

        @pl.when(kk == 0)
        def _():
            acc_ref[...] = part

        @pl.when(kk > 0)
        def _():
            acc_ref[...] += part

        @pl.when(kk == nk - 1)
        def _():
            o_ref[...] = (acc_ref[...] * scale).astype(out_dtype)

    return _call(
        body, name=name,
        out_shape=jax.ShapeDtypeStruct((m, n), out_dtype),
        grid=(m // bm, n // bn, nk),
        in_specs=[a_spec, b_spec],
        out_specs=pl.BlockSpec((bm, bn), lambda i, j, kk: (i, j)),
        scratch_shapes=[pltpu.VMEM((bm, bn), F32)] if nk > 1 else [],
        args=(a, b), vmem_mib=56, comm=comm)


def _ffn_up(xb, wg, wu, *, bm, bn, name, gate_done=False, comm=None):
    s, d = xb.shape
    f = wu.shape[0]
    bm, bn = min(bm, s), min(bn, f)
    assert s % bm == 0 and f % bn == 0

    def body(x_ref, wg_ref, wu_ref, hg_ref, hu_ref, h_ref):
        x = x_ref[...]
        if gate_done:
            g = wg_ref[...].astype(F32)
        else:
            g = lax.dot_general(x, wg_ref[...], NT_DIMS, preferred_element_type=F32)
        u = lax.dot_general(x, wu_ref[...], NT_DIMS, preferred_element_type=F32)
        sig = jax.nn.sigmoid(g)
        silu = g * sig
        hg_ref[...] = (u * (sig * (1.0 + g * (1.0 - sig)))).astype(BF16)
        hu_ref[...] = silu.astype(BF16)
        h_ref[...] = (silu * u).astype(BF16)

    out = jax.ShapeDtypeStruct((s, f), BF16)
    blk = pl.BlockSpec((bm, bn), lambda i, j: (i, j))
    return _call(
        body, name=name, out_shape=(out, out, out),
        grid=(s // bm, f // bn),
        in_specs=[pl.BlockSpec((bm, d), lambda i, j: (i, 0)),
                  blk if gate_done else pl.BlockSpec((bn, d), lambda i, j: (j, 0)),
                  pl.BlockSpec((bn, d), lambda i, j: (j, 0))],
        out_specs=(blk, blk, blk),
        args=(xb, wg, wu), vmem_mib=56, comm=comm)


def _ffn_bwd_dh(dzb, wd, g, u, *, scale, bm, bn, name, chunks=2, comm=None):
    s, d = dzb.shape
    f = wd.shape[0]
    bm, bn = min(bm, s), min(bn, f)
    assert s % bm == 0 and f % bn == 0

    cr = bm // chunks

    def body(dz_ref, wd_ref, hg_ref, hu_ref, dg_ref, du_ref):
        for r in range(chunks):
            rows = slice(r * cr, (r + 1) * cr)
            dh = lax.dot_general(dz_ref[rows, :], wd_ref[...], NT_DIMS, preferred_element_type=F32) * scale
            dg_ref[rows, :] = (dh * hg_ref[rows, :].astype(F32)).astype(BF16)
            du_ref[rows, :] = (dh * hu_ref[rows, :].astype(F32)).astype(BF16)

    out = jax.ShapeDtypeStruct((s, f), BF16)
    blk = pl.BlockSpec((bm, bn), lambda i, j: (i, j))
    return _call(
        body, name=name, out_shape=(out, out),
        grid=(s // bm, f // bn),
        in_specs=[pl.BlockSpec((bm, d), lambda i, j: (i, 0)),
                  pl.BlockSpec((bn, d), lambda i, j: (j, 0)), blk, blk],
        out_specs=(blk, blk),
        args=(dzb, wd, g, u), vmem_mib=56, comm=comm)


def _full_rows(acc_ref, rows, nj):
    return jnp.concatenate([acc_ref[jj, rows, :] for jj in range(nj)], axis=1)


def _mm_ln(a, b, res, gamma, beta, *, res_scale, mm_scale, bm, bn, name, comm=None):
    s, k = a.shape
    d = b.shape[1]
    bm, bn = min(bm, s), min(bn, d)
    assert s % bm == 0 and d % bn == 0
    nj = d // bn
    ch = min(EPILOGUE_ROWS, bm)

    def body(a_ref, b_ref, r_ref, g_ref, be_ref, y_ref, yb_ref, xh_ref, rs_ref, acc_ref):
        j = pl.program_id(1)
        acc_ref[j] = jnp.dot(a_ref[...], b_ref[...], preferred_element_type=F32)

        @pl.when(j == nj - 1)
        def _():
            def chunk(ci, carry):
                rows = pl.ds(pl.multiple_of(ci * ch, ch), ch)
                z = res_scale * r_ref[rows, :] + mm_scale * _full_rows(acc_ref, rows, nj)
                mu = jnp.mean(z, axis=-1, keepdims=True)
                zc = z - mu
                var = jnp.mean(zc * zc, axis=-1, keepdims=True)
                rstd = lax.rsqrt(var + LN_EPS)
                xh = zc * rstd
                y = xh * g_ref[...] + be_ref[...]
                y_ref[rows, :] = y
                yb_ref[rows, :] = y.astype(BF16)
                xh_ref[rows, :] = xh
                rs_ref[rows, :] = rstd
                return carry

            lax.fori_loop(0, bm // ch, chunk, 0)

    row = pl.BlockSpec((bm, d), lambda i, j: (i, 0))
    vec = pl.BlockSpec((1, d), lambda i, j: (0, 0))
    return _call(
        body, name=name,
        out_shape=(jax.ShapeDtypeStruct((s, d), F32), jax.ShapeDtypeStruct((s, d), BF16),
                   jax.ShapeDtypeStruct((s, d), F32), jax.ShapeDtypeStruct((s, 1), F32)),
        grid=(s // bm, nj),
        in_specs=[pl.BlockSpec((bm, k), lambda i, j: (i, 0)),
                  pl.BlockSpec((k, bn), lambda i, j: (0, j)), row, vec, vec],
        out_specs=(row, row, row, pl.BlockSpec((bm, 1), lambda i, j: (i, 0))),
        scratch_shapes=[pltpu.VMEM((nj, bm, bn), F32)],
        args=(a, b, res, gamma, beta), vmem_mib=58, comm=comm)


def _mm_dx(a, wt, extra, xhat, rstd, gamma, *, extra_scale, bm, bn, name, tb=False, comm=None):
    s, k = a.shape
    d = wt.shape[0] if tb else wt.shape[1]
    bm, bn = min(bm, s), min(bn, d)
    assert s % bm == 0 and d % bn == 0
    nj = d // bn
    ch = min(EPILOGUE_ROWS, bm)
    dims = NT_DIMS if tb else (((1,), (0,)), ((), ()))

    def body(a_ref, w_ref, e_ref, xh_ref, rs_ref, g_ref, dz_ref, dzb_ref, dg_ref, db_ref, acc_ref):
        i = pl.program_id(0)
        j = pl.program_id(1)
        acc_ref[j] = lax.dot_general(a_ref[...], w_ref[...], dims, preferred_element_type=F32)

        @pl.when(j == nj - 1)
        def _():
            def chunk(ci, carry):
                dgp, dbp = carry
                rows = pl.ds(pl.multiple_of(ci * ch, ch), ch)
                dx = extra_scale * e_ref[rows, :] + _full_rows(acc_ref, rows, nj)
                xh = xh_ref[rows, :]
                dxh = dx * g_ref[...]
                m1 = jnp.mean(dxh, axis=-1, keepdims=True)
                m2 = jnp.mean(dxh * xh, axis=-1, keepdims=True)
                dz = rs_ref[rows, :] * (dxh - m1 - xh * m2)
                dz_ref[rows, :] = dz
                dzb_ref[rows, :] = dz.astype(BF16)
                return dgp + jnp.sum(dx * xh, axis=0, keepdims=True), dbp + jnp.sum(dx, axis=0, keepdims=True)

            zero = jnp.zeros((1, d), F32)
            dgp, dbp = lax.fori_loop(0, bm // ch, chunk, (zero, zero))

            @pl.when(i == 0)
            def _():
                dg_ref[...] = dgp
                db_ref[...] = dbp

            @pl.when(i > 0)
            def _():
                dg_ref[...] += dgp
                db_ref[...] += dbp

    row = pl.BlockSpec((bm, d), lambda i, j: (i, 0))
    vec = pl.BlockSpec((1, d), lambda i, j: (0, 0))
    return _call(
        body, name=name,
        out_shape=(jax.ShapeDtypeStruct((s, d), F32), jax.ShapeDtypeStruct((s, d), BF16),
                   jax.ShapeDtypeStruct((1, d), F32), jax.ShapeDtypeStruct((1, d), F32)),
        grid=(s // bm, nj),
        in_specs=[pl.BlockSpec((bm, k), lambda i, j: (i, 0)),
                  pl.BlockSpec((bn, k), lambda i, j: (j, 0)) if tb else pl.BlockSpec((k, bn), lambda i, j: (0, j)),
                  row, row, pl.BlockSpec((bm, 1), lambda i, j: (i, 0)), vec],
        out_specs=(row, row, vec, vec),
        scratch_shapes=[pltpu.VMEM((nj, bm, bn), F32)],
        args=(a, wt, extra, xhat, rstd, gamma), vmem_mib=58, comm=comm)


def _ffn_dx(dg, du, wgt, wut, extra, *, extra_scale, bm, bn, name, comm=None):
    s, f = dg.shape
    d = wgt.shape[1]
    bm, bn = min(bm, s), min(bn, d)
    assert s % bm == 0 and d % bn == 0 and f % (2 * LANES) == 0
    nj, half = d // bn, f // 2

    def body(dg_ref, du_ref, wg_ref, wu_ref, e_ref, o_ref, acc_ref):
        kh, j = pl.program_id(1), pl.program_id(2)
        part = jnp.dot(dg_ref[...], wg_ref[...], preferred_element_type=F32)
        part = part + jnp.dot(du_ref[...], wu_ref[...], preferred_element_type=F32)

        @pl.when(kh == 0)
        def _():
            acc_ref[j] = part

        @pl.when(kh == 1)
        def _():
            o_ref[...] = extra_scale * e_ref[...] + (acc_ref[j] + part)

    rows = pl.BlockSpec((bm, half), lambda i, kh, j: (i, kh))
    cols = pl.BlockSpec((half, bn), lambda i, kh, j: (kh, j))
    blk = pl.BlockSpec((bm, bn), lambda i, kh, j: (i, j * kh))
    return _call(
        body, name=name, out_shape=jax.ShapeDtypeStruct((s, d), F32),
        grid=(s // bm, 2, nj), in_specs=[rows, rows, cols, cols, blk], out_specs=blk,
        scratch_shapes=[pltpu.VMEM((nj, bm, bn), F32)],
        args=(dg, du, wgt, wut, extra), vmem_mib=58, comm=comm)


def _ln_bwd(dx, xhat, rstd, gamma, *, bm, name):
    s, d = dx.shape
    bm = min(bm, s)
    assert s % bm == 0
    ch = min(EPILOGUE_ROWS, bm)

    def body(dx_ref, xh_ref, rs_ref, g_ref, dz_ref, dzb_ref, dg_ref, db_ref):
        def chunk(ci, carry):
            dgp, dbp = carry
            rows = pl.ds(pl.multiple_of(ci * ch, ch), ch)
            dxv = dx_ref[rows, :]
            xh = xh_ref[rows, :]
            dxh = dxv * g_ref[...]
            m1 = jnp.mean(dxh, axis=-1, keepdims=True)
            m2 = jnp.mean(dxh * xh, axis=-1, keepdims=True)
            dz = rs_ref[rows, :] * (dxh - m1 - xh * m2)
            dz_ref[rows, :] = dz
            dzb_ref[rows, :] = dz.astype(BF16)
            return dgp + jnp.sum(dxv * xh, axis=0, keepdims=True), dbp + jnp.sum(dxv, axis=0, keepdims=True)

        zero = jnp.zeros((1, d), F32)
        dgp, dbp = lax.fori_loop(0, bm // ch, chunk, (zero, zero))
        i = pl.program_id(0)

        @pl.when(i == 0)
        def _():
            dg_ref[...] = dgp
            db_ref[...] = dbp

        @pl.when(i > 0)
        def _():
            dg_ref[...] += dgp
            db_ref[...] += dbp

    row = pl.BlockSpec((bm, d), lambda i: (i, 0))
    vec = pl.BlockSpec((1, d), lambda i: (0, 0))
    return _call(
        body, name=name,
        out_shape=(jax.ShapeDtypeStruct((s, d), F32), jax.ShapeDtypeStruct((s, d), BF16),
                   jax.ShapeDtypeStruct((1, d), F32), jax.ShapeDtypeStruct((1, d), F32)),
        grid=(s // bm,), in_specs=[row, row, pl.BlockSpec((bm, 1), lambda i: (i, 0)), vec],
        out_specs=(row, row, vec, vec), args=(dx, xhat, rstd, gamma), vmem_mib=48)


def _to_bf16(x, *, bm, name, comm=None):
    s, d = x.shape
    bm = min(bm, s)
    assert s % bm == 0

    def body(x_ref, o_ref):
        o_ref[...] = x_ref[...].astype(BF16)

    row = pl.BlockSpec((bm, d), lambda i: (i, 0))
    return _call(body, name=name, out_shape=jax.ShapeDtypeStruct((s, d), BF16), grid=(s // bm,),
                 in_specs=[row], out_specs=row, args=(x,), vmem_mib=48, comm=comm)


def _ple_loss(x3, x3b, p, wpg, wpp, target, *, bm, bn, name):
    s, d = x3.shape
    dp = p.shape[1]
    bm, bn = min(bm, s), min(bn, d)
    assert s % bm == 0 and d % bn == 0
    inv_d = 1.0 / d
    chunks = 4 if bm % 64 == 0 else 1
    cr = bm // chunks

    def body(x_ref, xb_ref, p_ref, wg_ref, wp_ref, t_ref, l_ref, dy_ref, dg_ref, dp_ref):
        first = (pl.program_id(0) == 0) & (pl.program_id(1) == 0)

        @pl.when(first)
        def _():
            l_ref[...] = jnp.zeros_like(l_ref)

        part = 0.0
        for r in range(chunks):
            rows = slice(r * cr, (r + 1) * cr)
            gp = jnp.dot(xb_ref[rows, :], wg_ref[...], preferred_element_type=F32)
            pp = lax.dot_general(p_ref[rows, :].astype(BF16), wp_ref[...], NT_DIMS, preferred_element_type=F32)
            sig = jax.nn.sigmoid(gp)
            err = x_ref[rows, :] + sig * pp - t_ref[rows, :]
            part = part + jnp.sum(err * err)
            dy = err * inv_d
            dy_ref[rows, :] = dy
            dg_ref[rows, :] = (dy * pp * sig * (1.0 - sig)).astype(BF16)
            dp_ref[rows, :] = (dy * sig).astype(BF16)
        l_ref[...] += part

    blk = pl.BlockSpec((bm, bn), lambda i, j: (i, j))
    return pl.pallas_call(
        body, name=name,
        out_shape=(jax.ShapeDtypeStruct((8, LANES), F32), jax.ShapeDtypeStruct((s, d), F32),
                   jax.ShapeDtypeStruct((s, d), BF16), jax.ShapeDtypeStruct((s, d), BF16)),
        grid=(s // bm, d // bn),
        in_specs=[blk, pl.BlockSpec((bm, d), lambda i, j: (i, 0)), pl.BlockSpec((bm, dp), lambda i, j: (i, 0)),
                  pl.BlockSpec((d, bn), lambda i, j: (0, j)), pl.BlockSpec((bn, dp), lambda i, j: (j, 0)), blk],
        out_specs=(pl.BlockSpec((8, LANES), lambda i, j: (0, 0)), blk, blk, blk),
        compiler_params=_cp(("arbitrary", "arbitrary"), 56),
    )(x3, x3b, p, wpg, wpp, target)


def _rope_tables(positions):
    half = ROT_DIMS // 2
    lane = jnp.arange(HEAD_DIM)
    inv_freq = jnp.power(jnp.float32(ROPE_THETA), -(lane % half).astype(F32) * (2.0 / ROT_DIMS))
    ang = positions.astype(F32)[:, None] * inv_freq
    cos, sin = jnp.cos(ang), jnp.sin(ang)
    cf = jnp.where(lane < ROT_DIMS, cos, 1.0)
    sa = jnp.where(lane < half, -sin, 0.0)
    sb = jnp.where((lane >= half) & (lane < ROT_DIMS), sin, 0.0)
    return cf, sa, sb


def _rotary(t, tabs, *, n_cols, inverse, out_dtype, bs, name):
    s = t.shape[0]
    bs = min(bs, s)
    half = ROT_DIMS // 2
    heads = N_KV_HEADS
    assert n_cols % heads == 0

    def body(t_ref, cf_ref, sa_ref, sb_ref, o_ref):
        cf, sa, sb = cf_ref[...], sa_ref[...], sb_ref[...]
        for hd in range(heads):
            lanes = slice(hd * HEAD_DIM, (hd + 1) * HEAD_DIM)
            v = t_ref[:, lanes]
            if inverse:
                o = v * cf + pltpu.roll(v * sa, half, 1) + pltpu.roll(v * sb, HEAD_DIM - half, 1)
            else:
                o = v * cf + pltpu.roll(v, HEAD_DIM - half, 1) * sa + pltpu.roll(v, half, 1) * sb
            o_ref[:, lanes] = o.astype(out_dtype)

    blk = pl.BlockSpec((bs, heads * HEAD_DIM), lambda i, j: (i, j))
    tab = pl.BlockSpec((bs, HEAD_DIM), lambda i, j: (i, 0))
    return pl.pallas_call(
        body, name=name, out_shape=jax.ShapeDtypeStruct((s, n_cols * HEAD_DIM), out_dtype),
        grid=(s // bs, n_cols // heads), in_specs=[blk, tab, tab, tab], out_specs=blk,
        compiler_params=_cp(("parallel", "arbitrary"), 32),
    )(t, *tabs)


def _attn_blocks():
    out = []
    for g, dil in enumerate(DILATIONS):
        sup = SPAN * dil
        for j in range(ATTN_TILE // sup):
            for r in range(dil):
                out.append((g, j * sup + r, dil, (j - 1) * sup + r if j > 0 else None, ATTN_TILE - sup + r))
    return out


def _rows(ref, start, dil, lead=None):
    idx = pl.ds(start, SPAN, stride=dil) if dil > 1 else pl.ds(start, SPAN)
    return ref[idx, :] if lead is None else ref[lead, idx, :]


def _band_masks(n):
    qi = lax.broadcasted_iota(jnp.int32, (SPAN, 2 * SPAN), 0)
    ki = lax.broadcasted_iota(jnp.int32, (SPAN, 2 * SPAN), 1)
    band = (ki >= qi) & (ki <= qi + SPAN)
    return band, band & ((ki >= SPAN) | (n > 0))


def _attn_fwd(qkr, proj, *, name):
    s = qkr.shape[0]
    t = ATTN_TILE
    assert s % t == 0
    nt = s // t
    scale = HEAD_DIM ** -0.5
    kcol, vcol = N_PATTERNS * N_KV_HEADS, (N_PATTERNS + 1) * N_KV_HEADS
    blocks = _attn_blocks()

    def body(q0, q1, q2, kc_ref, kp_ref, vc_ref, vp_ref, o_ref, l_ref, og, lg):
        n = pl.program_id(1)
        band, band_first = _band_masks(n)
        q_refs = (q0, q1, q2)
        for g, start, dil, prev_in_tile, prev_start in blocks:
            q = _rows(q_refs[g], start, dil).astype(BF16)
            if prev_in_tile is not None:
                kp, vp, mask = _rows(kc_ref, prev_in_tile, dil), _rows(vc_ref, prev_in_tile, dil), band
            else:
                kp, vp, mask = _rows(kp_ref, prev_start, dil), _rows(vp_ref, prev_start, dil), band_first
            kk = jnp.concatenate([kp, _rows(kc_ref, start, dil)], axis=0).astype(BF16)
            vv = jnp.concatenate([vp, _rows(vc_ref, start, dil)], axis=0).astype(BF16)
            sc = lax.dot_general(q, kk, (((1,), (1,)), ((), ())), preferred_element_type=F32) * scale
            sc = jnp.where(mask, sc, -1e30)
            m = jnp.max(sc, axis=-1, keepdims=True)
            e = jnp.exp(sc - m)
            den = jnp.sum(e, axis=-1, keepdims=True)
            o = jnp.dot(e.astype(BF16), vv, preferred_element_type=F32) / den
            idx = pl.ds(start, SPAN, stride=dil) if dil > 1 else pl.ds(start, SPAN)
            og[g, idx, :] = o
            lg[g, idx, :] = jnp.broadcast_to(m + jnp.log(den), (SPAN, HEAD_DIM))
        l0, l1, l2 = lg[0], lg[1], lg[2]
        m = jnp.maximum(jnp.maximum(l0, l1), l2)
        w0, w1, w2 = jnp.exp(l0 - m), jnp.exp(l1 - m), jnp.exp(l2 - m)
        den = w0 + w1 + w2
        o_ref[...] = (w0 * og[0] + w1 * og[1] + w2 * og[2]) / den
        l_ref[...] = m + jnp.log(den)

    def col(c, prev=False):
        if prev:
            return pl.BlockSpec((t, HEAD_DIM), lambda h, n: (jnp.maximum(n - 1, 0), c + h))
        return pl.BlockSpec((t, HEAD_DIM), lambda h, n: (n, c + h))

    out = jax.ShapeDtypeStruct((s, N_KV_HEADS * HEAD_DIM), F32)
    return pl.pallas_call(
        body, name=name, out_shape=(out, out),
        grid=(N_KV_HEADS, nt),
        in_specs=[col(0), col(N_KV_HEADS), col(2 * N_KV_HEADS), col(kcol), col(kcol, True), col(vcol), col(vcol, True)],
        out_specs=(col(0), col(0)),
        scratch_shapes=[pltpu.VMEM((N_PATTERNS, t, HEAD_DIM), F32), pltpu.VMEM((N_PATTERNS, t, HEAD_DIM), F32)],
        compiler_params=_cp(("parallel", "arbitrary"), 48),
    )(qkr, qkr, qkr, qkr, qkr, proj, proj)


def _attn_bwd(qkr, proj, attn, lse, dcat, *, name, comm=None):
    s = qkr.shape[0]
    t = ATTN_TILE
    nt = s // t
    scale = HEAD_DIM ** -0.5
    kcol, vcol = N_PATTERNS * N_KV_HEADS, (N_PATTERNS + 1) * N_KV_HEADS
    blocks = _attn_blocks()

    def body(q0, q1, q2, kc_ref, kp_ref, vc_ref, vp_ref, o_ref, l_ref, do_ref,
             dq0, dq1, dq2, dk_ref, dv_ref, ck, cv, tkc, tvc, tkp, tvp):
        n = pl.program_id(1)
        for ref in (tkc, tvc, tkp, tvp):
            ref[...] = jnp.zeros_like(ref)

        @pl.when(n < nt)
        def _():
            band, band_first = _band_masks(n)
            q_refs, dq_refs = (q0, q1, q2), (dq0, dq1, dq2)
            for g, start, dil, prev_in_tile, prev_start in blocks:
                idx = pl.ds(start, SPAN, stride=dil) if dil > 1 else pl.ds(start, SPAN)
                q = q_refs[g][idx, :].astype(BF16)
                if prev_in_tile is not None:
                    kp, vp, mask = _rows(kc_ref, prev_in_tile, dil), _rows(vc_ref, prev_in_tile, dil), band
                else:
                    kp, vp, mask = _rows(kp_ref, prev_start, dil), _rows(vp_ref, prev_start, dil), band_first
                kk = jnp.concatenate([kp, kc_ref[idx, :]], axis=0).astype(BF16)
                vv = jnp.concatenate([vp, vc_ref[idx, :]], axis=0).astype(BF16)
                do = do_ref[idx, :]
                dsum = jnp.sum(do * o_ref[idx, :], axis=-1, keepdims=True)
                lrow = l_ref[idx, :][:, :1]
                dob = do.astype(BF16)
                sc = lax.dot_general(q, kk, (((1,), (1,)), ((), ())), preferred_element_type=F32) * scale
                p = jnp.where(mask, jnp.exp(sc - lrow), 0.0)
                dp = lax.dot_general(dob, vv, (((1,), (1,)), ((), ())), preferred_element_type=F32)
                ds = (p * (dp - dsum) * scale).astype(BF16)
                pb = p.astype(BF16)
                dq_refs[g][idx, :] = jnp.dot(ds, kk, preferred_element_type=F32)
                dkk = lax.dot_general(ds, q, (((0,), (0,)), ((), ())), preferred_element_type=F32)
                dvv = lax.dot_general(pb, dob, (((0,), (0,)), ((), ())), preferred_element_type=F32)
                tkc[idx, :] += dkk[SPAN:]
                tvc[idx, :] += dvv[SPAN:]
                if prev_in_tile is not None:
                    pidx = pl.ds(prev_in_tile, SPAN, stride=dil) if dil > 1 else pl.ds(prev_in_tile, SPAN)
                    tkc[pidx, :] += dkk[:SPAN]
                    tvc[pidx, :] += dvv[:SPAN]
                else:
                    pidx = pl.ds(prev_start, SPAN, stride=dil) if dil > 1 else pl.ds(prev_start, SPAN)
                    tkp[pidx, :] += dkk[:SPAN]
                    tvp[pidx, :] += dvv[:SPAN]

        @pl.when(n > 0)
        def _():
            dk_ref[...] = ck[...] + tkp[...]
            dv_ref[...] = (cv[...] + tvp[...]).astype(BF16)

        ck[...] = tkc[...]
        cv[...] = tvc[...]

    def col(c, prev=False):
        if prev:
            return pl.BlockSpec((t, HEAD_DIM), lambda h, n: (jnp.maximum(jnp.minimum(n, nt - 1) - 1, 0), c + h))
        return pl.BlockSpec((t, HEAD_DIM), lambda h, n: (jnp.minimum(n, nt - 1), c + h))

    kv_out = pl.BlockSpec((t, HEAD_DIM), lambda h, n: (jnp.maximum(n - 1, 0), h))
    tile = pltpu.VMEM((t, HEAD_DIM), F32)
    per_head = jax.ShapeDtypeStruct((s, N_KV_HEADS * HEAD_DIM), F32)
    return _call(
        body, name=name,
        out_shape=(per_head, per_head, per_head, per_head, jax.ShapeDtypeStruct((s, N_KV_HEADS * HEAD_DIM), BF16)),
        grid=(N_KV_HEADS, nt + 1),
        in_specs=[col(0), col(N_KV_HEADS), col(2 * N_KV_HEADS), col(kcol), col(kcol, True), col(vcol), col(vcol, True),
                  col(0), col(0), col(0)],
        out_specs=(col(0), col(0), col(0), kv_out, kv_out),
        scratch_shapes=[tile] * 6,
        args=(qkr, qkr, qkr, qkr, qkr, proj, proj, attn, lse, dcat), vmem_mib=48, comm=comm)


GELU_C0 = 0.7978845608028654
GELU_C1 = 0.044715


def _softplus_neg(lam):
    y = jnp.exp(-jnp.abs(lam))
    w = 1.0 + y
    log1p = jnp.where(w == 1.0, y, jnp.log(w) * (y / jnp.where(w == 1.0, 1.0, w - 1.0)))
    return jnp.maximum(-lam, 0.0) + log1p


def _down(cur, prev, k, row):
    if k == 0:
        return cur
    return jnp.where(row < k, pltpu.roll(prev, k, 0), pltpu.roll(cur, k, 0))


def _up(cur, nxt, k, row, tt):
    if k == 0:
        return cur
    return jnp.where(row >= tt - k, pltpu.roll(nxt, tt - k, 0), pltpu.roll(cur, tt - k, 0))


def _lru_gates(x, xp, cw, cb, wr, br, wi, bi, lam, row):
    shifts = [_down(x, xp, k, row) for k in range(CONV_WIDTH)]
    xc = cb
    for j in range(CONV_WIDTH):
        xc = xc + cw[j:j + 1, :] * shifts[CONV_WIDTH - 1 - j]
    xcb = xc.astype(BF16)
    r = jax.nn.sigmoid(jnp.dot(xcb, wr, preferred_element_type=F32) + br)
    i = jax.nn.sigmoid(jnp.dot(xcb, wi, preferred_element_type=F32) + bi)
    c = -LRU_C * _softplus_neg(lam)
    la = c * r
    a = jnp.exp(la)
    mult = jnp.sqrt(jnp.tanh(-la) * (a * a + 1.0))
    return shifts, xc, xcb, r, i, c, a, mult


def _lru_fwd(proj, cw, cb, wr, br, wi, bi, lam, *, tt, name):
    s = proj.shape[0]
    nblk = wr.shape[0]
    c = nblk * LANES
    tt = min(tt, s)
    xcol0 = (N_PATTERNS + 2) * N_KV_HEADS
    ycol0 = xcol0 + nblk

    def body(x_ref, y_ref, cw_ref, cb_ref, wr_ref, br_ref, wi_ref, bi_ref, lam_ref, rec_ref, h_ref, xprev, hc):
        n = pl.program_id(1)

        @pl.when(n == 0)
        def _():
            xprev[...] = jnp.zeros_like(xprev)
            hc[...] = jnp.zeros_like(hc)

        row = lax.broadcasted_iota(jnp.int32, (tt, LANES), 0)
        x = x_ref[...]
        _, xc, _, _, i, _, a, mult = _lru_gates(
            x, xprev[...], cw_ref[...], cb_ref[...], wr_ref[0].astype(BF16), br_ref[...],
            wi_ref[0].astype(BF16), bi_ref[...], lam_ref[...], row)
        av, bv = a, mult * (i * xc)
        k = 1
        while k < tt:
            bs = jnp.where(row < k, 0.0, pltpu.roll(bv, k, 0))
            as_ = jnp.where(row < k, 1.0, pltpu.roll(av, k, 0))
            bv = bv + av * bs
            av = av * as_
            k *= 2
        h = bv + av * hc[0:1, :]
        hc[...] = jnp.broadcast_to(h[tt - 1:tt, :], hc.shape)
        h_ref[...] = h
        y = y_ref[...]
        gel = 0.5 * y * (1.0 + jnp.tanh(GELU_C0 * (y + GELU_C1 * y * y * y)))
        rec_ref[...] = (h * gel).astype(BF16)
        xprev[...] = x

    vec = pl.BlockSpec((1, LANES), lambda b, n: (0, b))
    wblk = pl.BlockSpec((1, LANES, LANES), lambda b, n: (b, 0, 0))
    out = pl.BlockSpec((tt, LANES), lambda b, n: (n, b))
    return pl.pallas_call(
        body, name=name,
        out_shape=(jax.ShapeDtypeStruct((s, c), BF16), jax.ShapeDtypeStruct((s, c), F32)),
        grid=(nblk, s // tt),
        in_specs=[pl.BlockSpec((tt, LANES), lambda b, n: (n, xcol0 + b)),
                  pl.BlockSpec((tt, LANES), lambda b, n: (n, ycol0 + b)),
                  pl.BlockSpec((CONV_WIDTH, LANES), lambda b, n: (0, b)), vec, wblk, vec, wblk, vec, vec],
        out_specs=(out, out),
        scratch_shapes=[pltpu.VMEM((tt, LANES), F32), pltpu.VMEM((8, LANES), F32)],
        compiler_params=_cp(("parallel", "arbitrary"), 32),
    )(proj, proj, cw, cb, wr, br, wi, bi, lam)


def _lru_bwd(proj, hseq, dcat, cw, cb, wr, br, wi, bi, lam, *, tt, name, comm=None):
    s = proj.shape[0]
    nblk = wr.shape[0]
    c = nblk * LANES
    tt = min(tt, s)
    nt = s // tt
    xcol0 = (N_PATTERNS + 2) * N_KV_HEADS
    ycol0 = xcol0 + nblk
    rcol0 = N_KV_HEADS

    def body(x_ref, xp_ref, y_ref, h_ref, hp_ref, dr_ref, cw_ref, cb_ref, wr_ref, br_ref, wi_ref, bi_ref, lam_ref,
             dx_ref, dy_ref, dcw_ref, dcb_ref, dwr_ref, dbr_ref, dwi_ref, dbi_ref, dlam_ref, dxc_next, gcar, acar):
        n = pl.program_id(1)
        rt = nt - 1 - n

        @pl.when(n == 0)
        def _():
            for ref in (dxc_next, gcar, acar, dcw_ref, dcb_ref, dwr_ref, dbr_ref, dwi_ref, dbi_ref, dlam_ref):
                ref[...] = jnp.zeros_like(ref)

        row = lax.broadcasted_iota(jnp.int32, (tt, LANES), 0)
        x = x_ref[...]
        xp = jnp.where(rt > 0, xp_ref[...], 0.0)
        cwv = cw_ref[...]
        wrb, wib = wr_ref[0].astype(BF16), wi_ref[0].astype(BF16)
        lam_v = lam_ref[...]
        shifts, xc, xcb, r, i, cc, a, mult = _lru_gates(x, xp, cwv, cb_ref[...], wrb, br_ref[...], wib, bi_ref[...],
                                                        lam_v, row)
        h = h_ref[...]
        hp_last = jnp.where(rt > 0, hp_ref[7:8, :], 0.0)
        hprev = jnp.where(row < 1, hp_last, pltpu.roll(h, 1, 0))
        y = y_ref[...]
        y2 = y * y
        th = jnp.tanh(GELU_C0 * (y + GELU_C1 * y2 * y))
        gel = 0.5 * y * (1.0 + th)
        dgel = 0.5 * (1.0 + th) + 0.5 * y * (1.0 - th * th) * GELU_C0 * (1.0 + 3.0 * GELU_C1 * y2)
        drec = dr_ref[...]
        dy_ref[...] = (drec * h * dgel).astype(BF16)
        av = jnp.where(row >= tt - 1, acar[0:1, :], pltpu.roll(a, tt - 1, 0))
        bv = drec * gel
        k = 1
        while k < tt:
            bs = jnp.where(row >= tt - k, 0.0, pltpu.roll(bv, tt - k, 0))
            as_ = jnp.where(row >= tt - k, 1.0, pltpu.roll(av, tt - k, 0))
            bv = bv + av * bs
            av = av * as_
            k *= 2
        g = bv + av * gcar[0:1, :]
        gcar[...] = jnp.broadcast_to(g[0:1, :], gcar.shape)
        acar[...] = jnp.broadcast_to(a[0:1, :], acar.shape)
        da = g * hprev
        d_ixc = g * mult
        dmult = g * (i * xc)
        di = d_ixc * xc
        dxc = d_ixc * i
        a2 = a * a
        dla = da * a - dmult * (a2 / mult)
        dr = dla * cc
        dsp = jnp.sum(dla * r, axis=0, keepdims=True) * (-LRU_C)
        dlam_ref[...] += dsp * (-jax.nn.sigmoid(-lam_v))
        dzr = dr * r * (1.0 - r)
        dzi = di * i * (1.0 - i)
        dbr_ref[...] += jnp.sum(dzr, axis=0, keepdims=True)
        dbi_ref[...] += jnp.sum(dzi, axis=0, keepdims=True)
        dzrb, dzib = dzr.astype(BF16), dzi.astype(BF16)
        tn = (((0,), (0,)), ((), ()))
        ntd = (((1,), (1,)), ((), ()))
        dwr_ref[0] += lax.dot_general(xcb, dzrb, tn, preferred_element_type=F32)
        dwi_ref[0] += lax.dot_general(xcb, dzib, tn, preferred_element_type=F32)
        dxc = (dxc + lax.dot_general(dzrb, wrb, ntd, preferred_element_type=F32)
               + lax.dot_general(dzib, wib, ntd, preferred_element_type=F32))
        dcb_ref[...] += jnp.sum(dxc, axis=0, keepdims=True)
        dcw_ref[...] += jnp.concatenate(
            [jnp.sum(dxc * shifts[CONV_WIDTH - 1 - j], axis=0, keepdims=True) for j in range(CONV_WIDTH)], axis=0)
        nxt = dxc_next[...]
        dx = cwv[0:1, :] * _up(dxc, nxt, CONV_WIDTH - 1, row, tt)
        for j in range(1, CONV_WIDTH):
            dx = dx + cwv[j:j + 1, :] * _up(dxc, nxt, CONV_WIDTH - 1 - j, row, tt)
        dx_ref[...] = dx.astype(BF16)
        dxc_next[...] = dxc

    def tile(col0, prev=False):
        if prev:
            return pl.BlockSpec((tt, LANES), lambda b, n: (jnp.maximum(nt - 2 - n, 0), col0 + b))
        return pl.BlockSpec((tt, LANES), lambda b, n: (nt - 1 - n, col0 + b))

    vec = pl.BlockSpec((1, LANES), lambda b, n: (0, b))
    wblk = pl.BlockSpec((1, LANES, LANES), lambda b, n: (b, 0, 0))
    cwblk = pl.BlockSpec((CONV_WIDTH, LANES), lambda b, n: (0, b))
    hp8 = pl.BlockSpec((8, LANES), lambda b, n: (jnp.maximum((nt - 1 - n) * (tt // 8) - 1, 0), b))
    vshape = jax.ShapeDtypeStruct((1, c), F32)
    wshape = jax.ShapeDtypeStruct((nblk, LANES, LANES), F32)
    return _call(
        body, name=name,
        out_shape=(jax.ShapeDtypeStruct((s, c), BF16), jax.ShapeDtypeStruct((s, c), BF16),
                   jax.ShapeDtypeStruct((CONV_WIDTH, c), F32), vshape, wshape, vshape, wshape, vshape, vshape),
        grid=(nblk, nt),
        in_specs=[tile(xcol0), tile(xcol0, True), tile(ycol0), tile(0), hp8, tile(rcol0),
                  cwblk, vec, wblk, vec, wblk, vec, vec],
        out_specs=(tile(0), tile(0), cwblk, vec, wblk, vec, wblk, vec, vec),
        scratch_shapes=[pltpu.VMEM((tt, LANES), F32), pltpu.VMEM((8, LANES), F32), pltpu.VMEM((8, LANES), F32)],
        args=(proj, proj, proj, hseq, hseq, dcat, cw, cb, wr, br, wi, bi, lam), vmem_mib=32, comm=comm)


ROW_BLOCKS = (512, 256, 176, 128, 64, 32, 16, 8)


def _adamw(w, m, v, gparts, *, name):
    r, c = w.shape
    npart = gparts.shape[0]
    br = _pick(r, ROW_BLOCKS)
    c1 = 1.0 - ADAM_B1 ** ADAM_STEP
    c2 = 1.0 - ADAM_B2 ** ADAM_STEP

    def body(w_ref, m_ref, v_ref, g_ref, go_ref, d_ref, mo_ref, vo_ref):
        g = g_ref[0].astype(F32)
        for q in range(1, npart):
            g = g + g_ref[q].astype(F32)
        mn = ADAM_B1 * m_ref[...] + (1.0 - ADAM_B1) * g
        vn = ADAM_B2 * v_ref[...] + (1.0 - ADAM_B2) * (g * g)
        go_ref[...] = g
        mo_ref[...] = mn
        vo_ref[...] = vn
        d_ref[...] = -ADAM_LR * ((mn / c1) / (jnp.sqrt(vn / c2) + ADAM_EPS) + ADAM_WD * w_ref[...])

    blk = pl.BlockSpec((br, c), lambda i: (i, 0))
    out = jax.ShapeDtypeStruct((r, c), F32)
    return pl.pallas_call(
        body, name=name, out_shape=(out, out, out, out), grid=(r // br,),
        in_specs=[blk, blk, blk, pl.BlockSpec((npart, br, c), lambda i: (0, i, 0))],
        out_specs=(blk, blk, blk, blk),
        compiler_params=_cp(("parallel",), 48),
    )(w, m, v, gparts)


def _sum_parts(parts, *, name):
    npart, r, c = parts.shape
    br = next((b for b in range(min(r, 2048) // 8 * 8, 0, -8) if r % b == 0), r)

    def body(p_ref, o_ref):
        acc = p_ref[0]
        for q in range(1, npart):
            acc = acc + p_ref[q]
        o_ref[...] = acc

    return pl.pallas_call(
        body, name=name, out_shape=jax.ShapeDtypeStruct((r, c), F32), grid=(r // br,),
        in_specs=[pl.BlockSpec((npart, br, c), lambda i: (0, i, 0))],
        out_specs=pl.BlockSpec((br, c), lambda i: (i, 0)),
        compiler_params=_cp(("parallel",), 48),
    )(parts)


HBM = pl.BlockSpec(memory_space=pltpu.HBM)


def _mesh_pos():
    return lax.axis_index("x"), lax.axis_index("y"), lax.axis_index("c")


def _gather_comm(shards):
    na = len(shards)

    def parts(x_refs, out_refs, sems):
        send_sems, recv_sems, local_sems = sems
        x, y, c = _mesh_pos()
        me, sibling = (x, y, c), (x, y, 1 - c)
        chips = [(1 - x, y), (x, 1 - y), (1 - x, 1 - y)]

        def copy(a, k, block, to, src=None):
            px, py, pc = block
            dst = out_refs[a].at[4 * px + 2 * py + pc]
            return pltpu.make_async_remote_copy(
                src_ref=dst if src is None else src, dst_ref=dst,
                send_sem=send_sems.at[a, k], recv_sem=recv_sems.at[a, k],
                device_id=to, device_id_type=MESH)

        def mine(a):
            return pltpu.make_async_copy(x_refs[a], out_refs[a].at[4 * x + 2 * y + c], local_sems.at[a])

        def first(a):
            return [copy(a, 0, me, sibling, src=x_refs[a])] + [
                copy(a, 1 + j, me, (*chip, c), src=x_refs[a]) for j, chip in enumerate(chips)]

        def passed(a, j):
            return copy(a, 4 + j, (*chips[j], c), sibling)

        return me, sibling, chips, c, copy, mine, first, passed

    def start(x_refs, out_refs, sems):
        *_, mine, first, _ = parts(x_refs, out_refs, sems)
        for a in range(na):
            mine(a).start()
            for cp in first(a):
                cp.start()

    def mid(x_refs, out_refs, sems):
        me, _, chips, c, copy, _, _, passed = parts(x_refs, out_refs, sems)
        for j, chip in enumerate(chips):
            for a in range(na):
                copy(a, 1 + j, (*chip, c), me).wait_recv()
                passed(a, j).start()

    def end(x_refs, out_refs, sems):
        me, sibling, chips, c, copy, mine, first, passed = parts(x_refs, out_refs, sems)
        for a in range(na):
            copy(a, 0, sibling, me).wait_recv()
            for j, chip in enumerate(chips):
                copy(a, 4 + j, (*chip, 1 - c), me).wait_recv()
        for a in range(na):
            for cp in first(a) + [passed(a, j) for j in range(3)]:
                cp.wait_send()
            mine(a).wait()

    return _Comm(
        shards, [jax.ShapeDtypeStruct((N_DEV,) + a.shape, a.dtype) for a in shards],
        [pltpu.SemaphoreType.DMA((na, 7)), pltpu.SemaphoreType.DMA((na, 7)), pltpu.SemaphoreType.DMA((na,))],
        start, end, mid)


def _scatter_comm(g8s):
    na = len(g8s)

    def parts(g_refs, buf_refs, sems):
        send_sems, recv_sems, local_sems = sems
        x, y, c = _mesh_pos()
        me_idx = 4 * x + 2 * y + c

        def copy(a, k, slot):
            peer, peer_idx = _scatter_peer(k, x, y, c)
            return pltpu.make_async_remote_copy(
                src_ref=g_refs[a].at[peer_idx], dst_ref=buf_refs[a].at[me_idx if slot is None else slot],
                send_sem=send_sems.at[a, k - 1], recv_sem=recv_sems.at[a, k - 1],
                device_id=peer, device_id_type=MESH)

        def mine(a):
            return pltpu.make_async_copy(g_refs[a].at[me_idx], buf_refs[a].at[me_idx], local_sems.at[a])

        return x, y, c, copy, mine

    def start(g_refs, buf_refs, sems):
        *_, copy, mine = parts(g_refs, buf_refs, sems)
        for a in range(na):
            mine(a).start()
            for k in range(1, N_DEV):
                copy(a, k, None).start()

    def end(g_refs, buf_refs, sems):
        x, y, c, copy, mine = parts(g_refs, buf_refs, sems)
        for a in range(na):
            for k in range(1, N_DEV):
                copy(a, k, _scatter_peer(k, x, y, c)[1]).wait_recv()
        for a in range(na):
            for k in range(1, N_DEV):
                copy(a, k, None).wait_send()
            mine(a).wait()

    return _Comm(
        g8s, [jax.ShapeDtypeStruct(g.shape, g.dtype) for g in g8s],
        [pltpu.SemaphoreType.DMA((na, N_DEV - 1)), pltpu.SemaphoreType.DMA((na, N_DEV - 1)),
         pltpu.SemaphoreType.DMA((na,))],
        start, end)


def _scatter_peer(k, x, y, c):
    px, py, pc = (1 - x if k & 4 else x, 1 - y if k & 2 else y, 1 - c if k & 1 else c)
    return (px, py, pc), 4 * px + 2 * py + pc


def _scatter_start_comm(g8s):
    na = len(g8s)
    arrays = []
    for g8 in g8s:
        arrays += [g8, lax.empty(g8.shape, g8.dtype)]

    def local(refs, sems, a, me_idx):
        return pltpu.make_async_copy(refs[2 * a].at[me_idx], refs[2 * a + 1].at[me_idx], sems[0].at[a])

    def start(refs, outs, sems):
        send_sems, recv_sems = outs[:2]
        x, y, c = _mesh_pos()
        me_idx = 4 * x + 2 * y + c
        for a in range(na):
            local(refs, sems, a, me_idx).start()
            for k in range(1, N_DEV):
                peer, peer_idx = _scatter_peer(k, x, y, c)
                pltpu.make_async_remote_copy(
                    src_ref=refs[2 * a].at[peer_idx], dst_ref=refs[2 * a + 1].at[me_idx],
                    send_sem=send_sems.at[a * (N_DEV - 1) + k - 1], recv_sem=recv_sems.at[a * (N_DEV - 1) + k - 1],
                    device_id=peer, device_id_type=MESH).start()

    def end(refs, outs, sems):
        x, y, c = _mesh_pos()
        for a in range(na):
            local(refs, sems, a, 4 * x + 2 * y + c).wait()

    sem_shape = pltpu.SemaphoreType.DMA((na * (N_DEV - 1),))
    return _Comm(arrays, [sem_shape, sem_shape] + [pltpu.HBM(a.shape, a.dtype) for a in arrays],
                 [pltpu.SemaphoreType.DMA((na,))], start, end, split=True)


def _scatter_wait(started, after, *, name):
    send_sems, recv_sems, *arrays = started
    na = len(arrays) // 2

    def body(*refs):
        send_ref, recv_ref = refs[2 * na], refs[2 * na + 1]
        x, y, c = _mesh_pos()
        for a in range(na):
            for k in range(1, N_DEV):
                peer, peer_idx = _scatter_peer(k, x, y, c)
                pltpu.make_async_remote_copy(
                    src_ref=refs[2 * a].at[peer_idx], dst_ref=refs[2 * a + 1].at[peer_idx],
                    send_sem=send_ref.at[a * (N_DEV - 1) + k - 1], recv_sem=recv_ref.at[a * (N_DEV - 1) + k - 1],
                    device_id=peer, device_id_type=MESH).wait()

    sem = pl.BlockSpec(memory_space=pltpu.SEMAPHORE)
    outs = pl.pallas_call(
        body, name=name, out_shape=tuple(pltpu.HBM(a.shape, a.dtype) for a in arrays),
        in_specs=[HBM] * (2 * na) + [sem, sem, pl.BlockSpec(memory_space=pl.ANY)], out_specs=(HBM,) * (2 * na),
        input_output_aliases={k: k for k in range(2 * na)},
        compiler_params=pltpu.CompilerParams(has_side_effects=pltpu.SideEffectType.DATAFLOW_SIDE_EFFECTING),
    )(*arrays, send_sems, recv_sems, after)
    return outs[1::2]


BIG_WEIGHTS = ("ffn1_w_gate", "ffn1_w_up", "ffn1_w_down", "w_in", "w_out",
               "ffn2_w_gate", "ffn2_w_up", "ffn2_w_down", "w_ple_proj", "w_ple_gate")
COLUMN_SHARDED = ("ffn1_w_gate", "ffn1_w_up", "w_in", "ffn2_w_gate", "ffn2_w_up", "w_ple_proj", "conv_w")
SMALL_WEIGHTS = ("ln1_g", "ln1_b", "conv_b", "w_rgate", "b_rgate", "w_igate", "b_igate", "lru_lambda",
                 "ln2_g", "ln2_b", "ln3_g", "ln3_b")
SMALL_GRADS = SMALL_WEIGHTS + ("conv_w",)


class _Exchange:
    def __init__(self, full):
        self.full = dict(full)
        self.grads = {}

    def __getitem__(self, name):
        return self.full[name]

    def first_gather(self, x):
        return _to_bf16(x, bm=1024, name="x_bf16")

    def gather(self, names):
        return None, None

    def scatter(self, names):
        return None, None

    def scatter_start(self, names):
        return None, None

    def gather_small(self):
        return None, None


class _MeshExchange(_Exchange):
    def __init__(self, full, shards, conv_w):
        super().__init__(full)
        self.shards = shards
        self.conv_w = conv_w
        self.reduced = {}
        self.started = {}
        self.small_parts = None

    def first_gather(self, x):
        xb, (gate, conv_all) = _to_bf16(
            x, bm=1024, name="x_bf16", comm=_gather_comm([self.shards["ffn1_w_gate"], self.conv_w]))
        self.take("ffn1_w_gate", gate)
        self.full["conv_w"] = _to_full("conv_w", conv_all)
        return xb

    def gather(self, names):
        def done(outs):
            for n, o in zip(names, outs):
                self.take(n, o)
        return _gather_comm([self.shards[n] for n in names]), done

    def take(self, name, gathered):
        self.full[name] = gathered.reshape((N_DEV * gathered.shape[1],) + gathered.shape[2:])

    def scatter(self, names):
        def done(outs):
            self.reduced.update(zip(names, outs))
        return _scatter_comm([_to_owner_blocks(n, self.grads[n]) for n in names]), done

    def scatter_start(self, names):
        def done(outs):
            self.started[names] = outs
        return _scatter_start_comm([_to_owner_blocks(n, self.grads[n]) for n in names]), done

    def finish(self, after):
        for names, started in self.started.items():
            self.reduced.update(zip(names, _scatter_wait(started, after, name=f"scatter_wait_{names[0]}")))

    def gather_small(self):
        def done(outs):
            self.small_parts, = outs
        packed = jnp.concatenate([_rows128(self.grads[n]) for n in SMALL_GRADS], axis=0)
        return _gather_comm([packed]), done


def _carried(comm_done, call):
    comm, done = comm_done
    res = call(comm)
    if comm is None:
        return res
    res, outs = res
    done(outs)
    return res


def _dw(a, b, *, scale=1.0, name, comm=None):
    k, m = a.shape
    n = b.shape[1]
    return _mm(a, b, ta=True, scale=scale, out_dtype=BF16, bm=_pick(m, (1024, 512, 256, 128)),
               bn=_pick(n, (512, 256, 128)), bk=k, name=name, comm=comm)


def _ffn_bwd(ex, names, saved, xb_in, dz, dzb, ln_in, tag, on_dwd=None, on_dh=None, on_dwu=None, on_dx=None):
    gate, up, down = names
    g, u, h, _, _ = saved
    f = ex[gate].shape[0]

    def request(fn):
        return (None, None) if fn is None else fn(ex)

    ex.grads[down] = _carried(request(on_dwd), lambda c: _dw(h, dzb, scale=0.5, name=f"{tag}_dwd", comm=c))
    dg, du = _carried(request(on_dh), lambda c: _ffn_bwd_dh(
        dzb, ex[down], g, u, scale=0.5, bm=2048, bn=_pick(f, (512, 256, 128)), name=f"{tag}_dh", chunks=8, comm=c))
    ex.grads[gate] = _dw(xb_in, dg, name=f"{tag}_dwg")
    ex.grads[up] = _carried(request(on_dwu), lambda c: _dw(xb_in, du, name=f"{tag}_dwu", comm=c))
    d = dz.shape[1]
    dx = _carried(request(on_dx), lambda c: _ffn_dx(
        dg, du, ex[gate], ex[up], dz, extra_scale=DEEPNORM_ALPHA,
        bm=1024, bn=_pick(d, (512, 256, 128)), name=f"{tag}_dx", comm=c))
    return dx if ln_in is None else _ln_bwd(dx, *ln_in, bm=256, name=f"{tag}_ln_bwd")


def _local_step(x, p, target, positions, w):
    s, d = x.shape
    tabs = _rope_tables(positions)
    xb = w.first_gather(x)
    f = w["ffn1_w_gate"].shape[0]
    ffn_bn, ln_bn, ln_bn_short_k = _pick(f, (512, 256, 128)), _pick(d, (512, 256, 128)), _pick(d, (1024, 512, 256, 128))
    gate1 = _carried(w.gather(("ffn1_w_up", "ffn1_w_down")), lambda c: _mm(
        xb, w["ffn1_w_gate"], tb=True, out_dtype=BF16, bm=1024, bn=ffn_bn, bk=d, name="ffn1_gate", comm=c))
    g1, u1, h1 = _carried(w.gather(("w_in", "w_out")), lambda c: _ffn_up(
        xb, gate1, w["ffn1_w_up"], bm=1024, bn=ffn_bn, name="ffn1_up", gate_done=True, comm=c))
    x1, x1b, xh1, rs1 = _carried(w.gather(("ffn2_w_gate", "ffn2_w_up")), lambda c: _mm_ln(
        h1, w["ffn1_w_down"], x, w["ln1_g"], w["ln1_b"], res_scale=DEEPNORM_ALPHA, mm_scale=0.5,
        bm=512, bn=ln_bn, name="ffn1_down_ln", comm=c))
    sv1 = (g1, u1, h1, xh1, rs1)
    pw = w["w_in"].shape[0]
    proj = _carried(w.gather(("ffn2_w_down", "w_ple_gate", "w_ple_proj")), lambda c: _mm(
        x1b, w["w_in"], tb=True, bm=1024, bn=_pick(pw, (512, 256, 128)), bk=d, name="in_proj", comm=c))
    nqk = (N_PATTERNS + 1) * N_KV_HEADS
    qkr = _rotary(proj, tabs, n_cols=nqk, inverse=False, out_dtype=F32, bs=1024, name="rotary")
    attn, lse = _attn_fwd(qkr, proj, name="attn_fwd")
    lru_w = (w["conv_w"], w["conv_b"], w["w_rgate"], w["b_rgate"], w["w_igate"], w["b_igate"], w["lru_lambda"])
    rec, hseq = _lru_fwd(proj, *lru_w, tt=512, name="lru_fwd")
    cat = jnp.concatenate([attn.astype(BF16), rec], axis=1)
    x2, x2b, xh2, rs2 = _mm_ln(cat, w["w_out"], x1, w["ln2_g"], w["ln2_b"], res_scale=DEEPNORM_ALPHA, mm_scale=1.0,
                               bm=512, bn=ln_bn_short_k, name="out_proj_ln")
    g2, u2, h2 = _ffn_up(x2b, w["ffn2_w_gate"], w["ffn2_w_up"], bm=1024, bn=ffn_bn, name="ffn2_up")
    x3, x3b, xh3, rs3 = _mm_ln(h2, w["ffn2_w_down"], x2, w["ln3_g"], w["ln3_b"], res_scale=DEEPNORM_ALPHA, mm_scale=0.5,
                               bm=512, bn=ln_bn, name="ffn2_down_ln")
    sv3 = (g2, u2, h2, xh3, rs3)
    lsum, dy, dgate, dple = _ple_loss(x3, x3b, p, w["w_ple_gate"], w["w_ple_proj"], target,
                                      bm=1024, bn=_pick(d, (512, 256, 128)), name="ple_loss")
    grads = w.grads
    grads["w_ple_gate"] = _dw(x3b, dgate, name="dw_ple_gate")
    grads["w_ple_proj"] = _dw(p, dple, name="dw_ple_proj")
    dz3, dz3b, grads["ln3_g"], grads["ln3_b"] = _carried(w.scatter(("w_ple_gate", "w_ple_proj")), lambda c: _mm_dx(
        dgate, w["w_ple_gate"], dy, xh3, rs3, w["ln3_g"], extra_scale=1.0, bm=512, bn=ln_bn_short_k,
        name="ple_dx", tb=True, comm=c))
    dz2, dz2b, grads["ln2_g"], grads["ln2_b"] = _ffn_bwd(
        w, ("ffn2_w_gate", "ffn2_w_up", "ffn2_w_down"), sv3, x2b, dz3, dz3b, (xh2, rs2, w["ln2_g"]), "ffn2",
        on_dx=lambda ex: ex.scatter_start(("ffn2_w_down", "ffn2_w_gate", "ffn2_w_up")))
    grads["w_out"] = _dw(cat, dz2b, name="dw_out")
    dcat = _mm(dz2b, w["w_out"], tb=True, bm=1024, bn=_pick(d, (512, 256, 128)), bk=d, name="out_proj_dx")
    dq0, dq1, dq2, dk, dvb = _attn_bwd(qkr, proj, attn, lse, dcat, name="attn_bwd")
    nh = N_KV_HEADS
    dqkv = [_rotary(t, tabs, n_cols=nh, inverse=True, out_dtype=BF16, bs=1024, name=f"rotary_bwd{i}")
            for i, t in enumerate((dq0, dq1, dq2, dk))]
    (dxb, dyb, grads["conv_w"], grads["conv_b"], grads["w_rgate"], grads["b_rgate"], grads["w_igate"],
     grads["b_igate"], grads["lru_lambda"]) = _lru_bwd(proj, hseq, dcat, *lru_w, tt=512, name="lru_bwd")
    dproj = jnp.concatenate(dqkv + [dvb, dxb, dyb], axis=1)
    grads["w_in"] = _dw(x1b, dproj, name="dw_in")
    dz1, dz1b, grads["ln1_g"], grads["ln1_b"] = _carried(w.scatter_start(("w_out", "w_in")), lambda c: _mm_dx(
        dproj, w["w_in"], dz2, xh1, rs1, w["ln1_g"], extra_scale=DEEPNORM_ALPHA,
        bm=512, bn=ln_bn, name="in_proj_dx", comm=c))
    grad_x = _ffn_bwd(w, ("ffn1_w_gate", "ffn1_w_up", "ffn1_w_down"), sv1, xb, dz1, dz1b, None, "ffn1",
                      on_dwd=lambda ex: ex.gather_small(),
                      on_dh=lambda ex: ex.scatter_start(("ffn1_w_down",)),
                      on_dwu=lambda ex: ex.scatter_start(("ffn1_w_gate",)),
                      on_dx=lambda ex: ex.scatter_start(("ffn1_w_up",)))
    return lsum, grad_x


def _to_full(name, gathered):
    if name in COLUMN_SHARDED:
        _, r, c = gathered.shape
        return jnp.transpose(gathered, (1, 0, 2)).reshape(r, N_DEV * c)
    return gathered.reshape((N_DEV * gathered.shape[1],) + gathered.shape[2:])


def _to_owner_blocks(name, full):
    if name in COLUMN_SHARDED:
        r, c = full.shape
        return jnp.transpose(full.reshape(r, N_DEV, c // N_DEV), (1, 0, 2))
    return full.reshape((N_DEV, full.shape[0] // N_DEV) + full.shape[1:])


def _rows128(a):
    flat = a.reshape(-1, LANES)
    pad = (-flat.shape[0]) % 8
    return jnp.pad(flat, ((0, pad), (0, 0))) if pad else flat


def kernel(x, p, positions, ffn1_w_gate, ffn1_w_up, ffn1_w_down, ln1_g, ln1_b, w_in, conv_w, conv_b, w_rgate, b_rgate, w_igate, b_igate, lru_lambda, w_out, ln2_g, ln2_b, ffn2_w_gate, ffn2_w_up, ffn2_w_down, ln3_g, ln3_b, w_ple_proj, w_ple_gate, loss_target, m_ffn1_w_gate, m_ffn1_w_up, m_ffn1_w_down, m_ln1_g, m_ln1_b, m_w_in, m_conv_w, m_conv_b, m_w_rgate, m_b_rgate, m_w_igate, m_b_igate, m_lru_lambda, m_w_out, m_ln2_g, m_ln2_b, m_ffn2_w_gate, m_ffn2_w_up, m_ffn2_w_down, m_ln3_g, m_ln3_b, m_w_ple_proj, m_w_ple_gate, v_ffn1_w_gate, v_ffn1_w_up, v_ffn1_w_down, v_ln1_g, v_ln1_b, v_w_in, v_conv_w, v_conv_b, v_w_rgate, v_b_rgate, v_w_igate, v_b_igate, v_lru_lambda, v_w_out, v_ln2_g, v_ln2_b, v_ffn2_w_gate, v_ffn2_w_up, v_ffn2_w_down, v_ln3_g, v_ln3_b, v_w_ple_proj, v_w_ple_gate):
    names = ("ffn1_w_gate", "ffn1_w_up", "ffn1_w_down", "ln1_g", "ln1_b", "w_in", "conv_w", "conv_b", "w_rgate",
             "b_rgate", "w_igate", "b_igate", "lru_lambda", "w_out", "ln2_g", "ln2_b", "ffn2_w_gate", "ffn2_w_up",
             "ffn2_w_down", "ln3_g", "ln3_b", "w_ple_proj", "w_ple_gate")
    ws = (ffn1_w_gate, ffn1_w_up, ffn1_w_down, ln1_g, ln1_b, w_in, conv_w, conv_b, w_rgate, b_rgate, w_igate, b_igate,
          lru_lambda, w_out, ln2_g, ln2_b, ffn2_w_gate, ffn2_w_up, ffn2_w_down, ln3_g, ln3_b, w_ple_proj, w_ple_gate)
    ms = (m_ffn1_w_gate, m_ffn1_w_up, m_ffn1_w_down, m_ln1_g, m_ln1_b, m_w_in, m_conv_w, m_conv_b, m_w_rgate, m_b_rgate,
          m_w_igate, m_b_igate, m_lru_lambda, m_w_out, m_ln2_g, m_ln2_b, m_ffn2_w_gate, m_ffn2_w_up, m_ffn2_w_down,
          m_ln3_g, m_ln3_b, m_w_ple_proj, m_w_ple_gate)
    vs = (v_ffn1_w_gate, v_ffn1_w_up, v_ffn1_w_down, v_ln1_g, v_ln1_b, v_w_in, v_conv_w, v_conv_b, v_w_rgate, v_b_rgate,
          v_w_igate, v_b_igate, v_lru_lambda, v_w_out, v_ln2_g, v_ln2_b, v_ffn2_w_gate, v_ffn2_w_up, v_ffn2_w_down,
          v_ln3_g, v_ln3_b, v_w_ple_proj, v_w_ple_gate)
    def local(a):
        return a[0] if a.ndim >= 3 else a

    w_loc = {n: local(a) for n, a in zip(names, ws)}
    m_loc = {n: local(a) for n, a in zip(names, ms)}
    v_loc = {n: local(a) for n, a in zip(names, vs)}
    out_shapes = {n: a.shape for n, a in zip(names, ws)}

    shards = {n: (w_loc[n].T if n in COLUMN_SHARDED else w_loc[n]).astype(BF16) for n in BIG_WEIGHTS}
    ex = _MeshExchange({n: w_loc[n] for n in SMALL_WEIGHTS}, shards, w_loc["conv_w"])

    lsum, grad_x = _local_step(x[0], p[0, 0], loss_target[0], positions[0], ex)
    ex.finish(grad_x)
    grads, reduced = ex.grads, ex.reduced
    d_model = x.shape[-1]
    loss = lax.psum(lsum[0, 0] * (0.5 / d_model), ("x", "y", "c"))

    small = SMALL_GRADS
    summed = _sum_parts(ex.small_parts, name="sum_small_grads")
    small_grads, row = {}, 0
    for n in small:
        rows = grads[n].size // LANES
        small_grads[n] = summed[row:row + rows].reshape(grads[n].shape)
        row += rows + (-rows) % 8
    me = 4 * lax.axis_index("x") + 2 * lax.axis_index("y") + lax.axis_index("c")
    cw_cols = w_loc["conv_w"].shape[1]
    small_grads["conv_w"] = lax.dynamic_slice_in_dim(small_grads["conv_w"], me * cw_cols, cw_cols, axis=1)

    out_g, out_d, out_m, out_v = {}, {}, {}, {}
    for n in names:
        wl, ml, vl = w_loc[n], m_loc[n], v_loc[n]
        shape = wl.shape
        if n in BIG_WEIGHTS:
            gparts = reduced[n]
        else:
            gparts = small_grads[n].reshape((1,) + shape)
        if wl.ndim == 3:
            wl, ml, vl = (t.reshape(-1, shape[-1]) for t in (wl, ml, vl))
            gparts = gparts.reshape(gparts.shape[0], -1, shape[-1])
        res = _adamw(wl, ml, vl, gparts, name=f"adamw_{n}")
        out_g[n], out_d[n], out_m[n], out_v[n] = (t.reshape(out_shapes[n]) for t in res)

    return (loss, grad_x[None], *[out_g[n] for n in names], *[out_d[n] for n in names],
            *[out_m[n] for n in names], *[out_v[n] for n in names])
```

```python
import jax
import jax.numpy as jnp
from jax import lax
from jax.experimental import pallas as pl
from jax.experimental.pallas import tpu as pltpu

F32 = jnp.float32
BF16 = jnp.bfloat16

N_DEV = 8
LANES = 128
MIB = 1 << 20

HEAD_DIM = 128
N_KV_HEADS = 4
DILATIONS = (1, 4, 16)
N_PATTERNS = 3
SPAN = 128
ROT_DIMS = 32
ROPE_THETA = 500000.0
LRU_C = 8.0
CONV_WIDTH = 4
LN_EPS = 1e-5
DEEPNORM_ALPHA = 2.0 ** 0.25
ATTN_TILE = SPAN * DILATIONS[-1]

ADAM_LR = 0.001
ADAM_B1 = 0.9
ADAM_B2 = 0.999
ADAM_EPS = 1e-08
ADAM_WD = 0.01
ADAM_STEP = 10

MESH = pl.DeviceIdType.MESH
NT_DIMS = (((1,), (1,)), ((), ()))
EPILOGUE_ROWS = 64


def _cp(semantics, vmem_mib):
    return pltpu.CompilerParams(dimension_semantics=semantics, vmem_limit_bytes=vmem_mib * MIB)


def _pick(n, candidates):
    for c in candidates:
        if n % c == 0:
            return c
    return n


class _Comm:
    def __init__(self, arrays, out_shapes, scratch, start, end, mid=None, split=False):
        self.arrays, self.out_shapes, self.scratch = list(arrays), list(out_shapes), list(scratch)
        self.start, self.mid, self.end, self.split = start, mid, end, split


def _call(body, *, name, grid, in_specs, out_specs, out_shape, args, scratch_shapes=(), vmem_mib, comm=None):
    single = not isinstance(out_shape, (tuple, list))
    out_shape_t = (out_shape,) if single else tuple(out_shape)
    out_specs_t = (out_specs,) if single else tuple(out_specs)
    params = _cp(("arbitrary",) * len(grid), vmem_mib)
    if comm is None:
        res = pl.pallas_call(body, name=name, grid=grid, in_specs=list(in_specs), out_specs=out_specs_t,
                             out_shape=out_shape_t, scratch_shapes=list(scratch_shapes), compiler_params=params)(*args)
        return res[0] if single else res
    n_in, n_out, n_scr = len(args), len(out_shape_t), len(scratch_shapes)
    nci, nco = len(comm.arrays), len(comm.out_shapes)
    total = 1
    for g in grid:
        total *= g

    def wrapped(*refs):
        ins, refs = refs[:n_in], refs[n_in:]
        cin, refs = refs[:nci], refs[nci:]
        outs, refs = refs[:n_out], refs[n_out:]
        cout, refs = refs[:nco], refs[nco:]
        scr, csem = refs[:n_scr], refs[n_scr:]
        step = pl.program_id(0)
        for ax in range(1, len(grid)):
            step = step * grid[ax] + pl.program_id(ax)

        @pl.when(step == 0)
        def _():
            comm.start(cin, cout, csem)

        body(*ins, *outs, *scr)
        if comm.mid is not None:
            @pl.when(step == (3 * total) // 4)
            def _():
                comm.mid(cin, cout, csem)

        @pl.when(step == total - 1)
        def _():
            comm.end(cin, cout, csem)

    hbm = pl.BlockSpec(memory_space=pltpu.HBM)
    if comm.split:
        sem = pl.BlockSpec(memory_space=pltpu.SEMAPHORE)
        n_sems = nco - nci
        res = pl.pallas_call(
            wrapped, name=name, grid=grid,
            in_specs=list(in_specs) + [hbm] * nci,
            out_specs=out_specs_t + (sem,) * n_sems + (hbm,) * nci,
            out_shape=out_shape_t + tuple(comm.out_shapes),
            scratch_shapes=list(scratch_shapes) + comm.scratch,
            input_output_aliases={n_in + k: n_out + n_sems + k for k in range(nci)},
            compiler_params=pltpu.CompilerParams(
                dimension_semantics=("arbitrary",) * len(grid), vmem_limit_bytes=vmem_mib * MIB,
                has_side_effects=pltpu.SideEffectType.DATAFLOW_SIDE_EFFECTING),
        )(*args, *[pltpu.with_memory_space_constraint(a, pltpu.HBM) for a in comm.arrays])
    else:
        res = pl.pallas_call(
            wrapped, name=name, grid=grid,
            in_specs=list(in_specs) + [hbm] * nci,
            out_specs=out_specs_t + (hbm,) * nco,
            out_shape=out_shape_t + tuple(comm.out_shapes),
            scratch_shapes=list(scratch_shapes) + comm.scratch,
            compiler_params=params)(*args, *comm.arrays)
    own, extra = res[:n_out], res[n_out:]
    return (own[0] if single else own), extra


def _mm(a, b, *, ta=False, tb=False, out_dtype=F32, scale=1.0, bm, bn, bk, name, comm=None):
    m, k = (a.shape[1], a.shape[0]) if ta else a.shape
    n = b.shape[0] if tb else b.shape[1]
    bm, bn, bk = min(bm, m), min(bn, n), min(bk, k)
    assert m % bm == 0 and n % bn == 0 and k % bk == 0, (name, m, n, k, bm, bn, bk)
    nk = k // bk
    a_spec = pl.BlockSpec((bk, bm), lambda i, j, kk: (kk, i)) if ta else pl.BlockSpec((bm, bk), lambda i, j, kk: (i, kk))
    b_spec = pl.BlockSpec((bn, bk), lambda i, j, kk: (j, kk)) if tb else pl.BlockSpec((bk, bn), lambda i, j, kk: (kk, j))
    dn = (((0 if ta else 1,), (1 if tb else 0,)), ((), ()))

    def body(a_ref, b_ref, o_ref, *acc):
        part = lax.dot_general(a_ref[...].astype(BF16), b_ref[...].astype(BF16), dn, preferred_element_type=F32)
        if nk == 1:
            o_ref[...] = (part * scale).astype(out_dtype)
            return
        acc_ref, = acc
        kk = pl.program_id(2)

        @pl.when(kk == 0)
        def _():
            acc_ref[...] = part

        @pl.when(kk > 0)
        def _():
            acc_ref[...] += part

        @pl.when(kk == nk - 1)
        def _():
            o_ref[...] = (acc_ref[...] * scale).astype(out_dtype)

    return _call(
        body, name=name,
        out_shape=jax.ShapeDtypeStruct((m, n), out_dtype),
        grid=(m // bm, n // bn, nk),
        in_specs=[a_spec, b_spec],
        out_specs=pl.BlockSpec((bm, bn), lambda i, j, kk: (i, j)),
        scratch_shapes=[pltpu.VMEM((bm, bn), F32)] if nk > 1 else [],
        args=(a, b), vmem_mib=56, comm=comm)


def _ffn_up(xb, wg, wu, *, bm, bn, name, comm=None):
    s, d = xb.shape
    f = wg.shape[0]
    bm, bn = min(bm, s), min(bn, f)
    assert s % bm == 0 and f % bn == 0

    def body(x_ref, wg_ref, wu_ref, hg_ref, hu_ref, h_ref):
        x = x_ref[...]
        g = lax.dot_general(x, wg_ref[...], NT_DIMS, preferred_element_type=F32)
        u = lax.dot_general(x, wu_ref[...], NT_DIMS, preferred_element_type=F32)
        sig = jax.nn.sigmoid(g)
        silu = g * sig
        hg_ref[...] = (u * (sig * (1.0 + g * (1.0 - sig)))).astype(BF16)
        hu_ref[...] = silu.astype(BF16)
        h_ref[...] = (silu * u).astype(BF16)

    out = jax.ShapeDtypeStruct((s, f), BF16)
    blk = pl.BlockSpec((bm, bn), lambda i, j: (i, j))
    return _call(
        body, name=name, out_shape=(out, out, out),
        grid=(s // bm, f // bn),
        in_specs=[pl.BlockSpec((bm, d), lambda i, j: (i, 0)),
                  pl.BlockSpec((bn, d), lambda i, j: (j, 0)),
                  pl.BlockSpec((bn, d), lambda i, j: (j, 0))],
        out_specs=(blk, blk, blk),
        args=(xb, wg, wu), vmem_mib=56, comm=comm)


def _ffn_bwd_dh(dzb, wd, g, u, *, scale, bm, bn, name, chunks=2, comm=None):
    s, d = dzb.shape
    f = wd.shape[0]
    bm, bn = min(bm, s), min(bn, f)
    assert s % bm == 0 and f % bn == 0

    cr = bm // chunks

    def body(dz_ref, wd_ref, hg_ref, hu_ref, dg_ref, du_ref):
        for r in range(chunks):
            rows = slice(r * cr, (r + 1) * cr)
            dh = lax.dot_general(dz_ref[rows, :], wd_ref[...], NT_DIMS, preferred_element_type=F32) * scale
            dg_ref[rows, :] = (dh * hg_ref[rows, :].astype(F32)).astype(BF16)
            du_ref[rows, :] = (dh * hu_ref[rows, :].astype(F32)).astype(BF16)

    out = jax.ShapeDtypeStruct((s, f), BF16)
    blk = pl.BlockSpec((bm, bn), lambda i, j: (i, j))
    return _call(
        body, name=name, out_shape=(out, out),
        grid=(s // bm, f // bn),
        in_specs=[pl.BlockSpec((bm, d), lambda i, j: (i, 0)),
                  pl.BlockSpec((bn, d), lambda i, j: (j, 0)), blk, blk],
        out_specs=(blk, blk),
        args=(dzb, wd, g, u), vmem_mib=56, comm=comm)


def _full_rows(acc_ref, rows, nj):
    return jnp.concatenate([acc_ref[jj, rows, :] for jj in range(nj)], axis=1)


def _mm_ln(a, b, res, gamma, beta, *, res_scale, mm_scale, bm, bn, name, comm=None):
    s, k = a.shape
    d = b.shape[1]
    bm, bn = min(bm, s), min(bn, d)
    assert s % bm == 0 and d % bn == 0
    nj = d // bn
    ch = min(EPILOGUE_ROWS, bm)

    def body(a_ref, b_ref, r_ref, g_ref, be_ref, y_ref, yb_ref, xh_ref, rs_ref, acc_ref):
        j = pl.program_id(1)
        acc_ref[j] = jnp.dot(a_ref[...], b_ref[...], preferred_element_type=F32)

        @pl.when(j == nj - 1)
        def _():
            def chunk(ci, carry):
                rows = pl.ds(pl.multiple_of(ci * ch, ch), ch)
                z = res_scale * r_ref[rows, :] + mm_scale * _full_rows(acc_ref, rows, nj)
                mu = jnp.mean(z, axis=-1, keepdims=True)
                zc = z - mu
                var = jnp.mean(zc * zc, axis=-1, keepdims=True)
                rstd = lax.rsqrt(var + LN_EPS)
                xh = zc * rstd
                y = xh * g_ref[...] + be_ref[...]
                y_ref[rows, :] = y
                yb_ref[rows, :] = y.astype(BF16)
                xh_ref[rows, :] = xh
                rs_ref[rows, :] = rstd
                return carry

            lax.fori_loop(0, bm // ch, chunk, 0)

    row = pl.BlockSpec((bm, d), lambda i, j: (i, 0))
    vec = pl.BlockSpec((1, d), lambda i, j: (0, 0))
    return _call(
        body, name=name,
        out_shape=(jax.ShapeDtypeStruct((s, d), F32), jax.ShapeDtypeStruct((s, d), BF16),
                   jax.ShapeDtypeStruct((s, d), F32), jax.ShapeDtypeStruct((s, 1), F32)),
        grid=(s // bm, nj),
        in_specs=[pl.BlockSpec((bm, k), lambda i, j: (i, 0)),
                  pl.BlockSpec((k, bn), lambda i, j: (0, j)), row, vec, vec],
        out_specs=(row, row, row, pl.BlockSpec((bm, 1), lambda i, j: (i, 0))),
        scratch_shapes=[pltpu.VMEM((nj, bm, bn), F32)],
        args=(a, b, res, gamma, beta), vmem_mib=58, comm=comm)


def _mm_dx(a, wt, extra, xhat, rstd, gamma, *, extra_scale, bm, bn, name, tb=False, comm=None):
    s, k = a.shape
    d = wt.shape[0] if tb else wt.shape[1]
    bm, bn = min(bm, s), min(bn, d)
    assert s % bm == 0 and d % bn == 0
    nj = d // bn
    ch = min(EPILOGUE_ROWS, bm)
    dims = NT_DIMS if tb else (((1,), (0,)), ((), ()))

    def body(a_ref, w_ref, e_ref, xh_ref, rs_ref, g_ref, dz_ref, dzb_ref, dg_ref, db_ref, acc_ref):
        i = pl.program_id(0)
        j = pl.program_id(1)
        acc_ref[j] = lax.dot_general(a_ref[...], w_ref[...], dims, preferred_element_type=F32)

        @pl.when(j == nj - 1)
        def _():
            def chunk(ci, carry):
                dgp, dbp = carry
                rows = pl.ds(pl.multiple_of(ci * ch, ch), ch)
                dx = extra_scale * e_ref[rows, :] + _full_rows(acc_ref, rows, nj)
                xh = xh_ref[rows, :]
                dxh = dx * g_ref[...]
                m1 = jnp.mean(dxh, axis=-1, keepdims=True)
                m2 = jnp.mean(dxh * xh, axis=-1, keepdims=True)
                dz = rs_ref[rows, :] * (dxh - m1 - xh * m2)
                dz_ref[rows, :] = dz
                dzb_ref[rows, :] = dz.astype(BF16)
                return dgp + jnp.sum(dx * xh, axis=0, keepdims=True), dbp + jnp.sum(dx, axis=0, keepdims=True)

            zero = jnp.zeros((1, d), F32)
            dgp, dbp = lax.fori_loop(0, bm // ch, chunk, (zero, zero))

            @pl.when(i == 0)
            def _():
                dg_ref[...] = dgp
                db_ref[...] = dbp

            @pl.when(i > 0)
            def _():
                dg_ref[...] += dgp
                db_ref[...] += dbp

    row = pl.BlockSpec((bm, d), lambda i, j: (i, 0))
    vec = pl.BlockSpec((1, d), lambda i, j: (0, 0))
    return _call(
        body, name=name,
        out_shape=(jax.ShapeDtypeStruct((s, d), F32), jax.ShapeDtypeStruct((s, d), BF16),
                   jax.ShapeDtypeStruct((1, d), F32), jax.ShapeDtypeStruct((1, d), F32)),
        grid=(s // bm, nj),
        in_specs=[pl.BlockSpec((bm, k), lambda i, j: (i, 0)),
                  pl.BlockSpec((bn, k), lambda i, j: (j, 0)) if tb else pl.BlockSpec((k, bn), lambda i, j: (0, j)),
                  row, row, pl.BlockSpec((bm, 1), lambda i, j: (i, 0)), vec],
        out_specs=(row, row, vec, vec),
        scratch_shapes=[pltpu.VMEM((nj, bm, bn), F32)],
        args=(a, wt, extra, xhat, rstd, gamma), vmem_mib=58, comm=comm)


def _ffn_dx(dg, du, wgt, wut, extra, *, extra_scale, bm, bn, name, comm=None):
    s, f = dg.shape
    d = wgt.shape[1]
    bm, bn = min(bm, s), min(bn, d)
    assert s % bm == 0 and d % bn == 0 and f % (2 * LANES) == 0
    nj, half = d // bn, f // 2

    def body(dg_ref, du_ref, wg_ref, wu_ref, e_ref, o_ref, acc_ref):
        kh, j = pl.program_id(1), pl.program_id(2)
        part = jnp.dot(dg_ref[...], wg_ref[...], preferred_element_type=F32)
        part = part + jnp.dot(du_ref[...], wu_ref[...], preferred_element_type=F32)

        @pl.when(kh == 0)
        def _():
            acc_ref[j] = part

        @pl.when(kh == 1)
        def _():
            o_ref[...] = extra_scale * e_ref[...] + (acc_ref[j] + part)

    rows = pl.BlockSpec((bm, half), lambda i, kh, j: (i, kh))
    cols = pl.BlockSpec((half, bn), lambda i, kh, j: (kh, j))
    blk = pl.BlockSpec((bm, bn), lambda i, kh, j: (i, j * kh))
    return _call(
        body, name=name, out_shape=jax.ShapeDtypeStruct((s, d), F32),
        grid=(s // bm, 2, nj), in_specs=[rows, rows, cols, cols, blk], out_specs=blk,
        scratch_shapes=[pltpu.VMEM((nj, bm, bn), F32)],
        args=(dg, du, wgt, wut, extra), vmem_mib=58, comm=comm)


def _ln_bwd(dx, xhat, rstd, gamma, *, bm, name):
    s, d = dx.shape
    bm = min(bm, s)
    assert s % bm == 0
    ch = min(EPILOGUE_ROWS, bm)

    def body(dx_ref, xh_ref, rs_ref, g_ref, dz_ref, dzb_ref, dg_ref, db_ref):
        def chunk(ci, carry):
            dgp, dbp = carry
            rows = pl.ds(pl.multiple_of(ci * ch, ch), ch)
            dxv = dx_ref[rows, :]
            xh = xh_ref[rows, :]
            dxh = dxv * g_ref[...]
            m1 = jnp.mean(dxh, axis=-1, keepdims=True)
            m2 = jnp.mean(dxh * xh, axis=-1, keepdims=True)
            dz = rs_ref[rows, :] * (dxh - m1 - xh * m2)
            dz_ref[rows, :] = dz
            dzb_ref[rows, :] = dz.astype(BF16)
            return dgp + jnp.sum(dxv * xh, axis=0, keepdims=True), dbp + jnp.sum(dxv, axis=0, keepdims=True)

        zero = jnp.zeros((1, d), F32)
        dgp, dbp = lax.fori_loop(0, bm // ch, chunk, (zero, zero))
        i = pl.program_id(0)

        @pl.when(i == 0)
        def _():
            dg_ref[...] = dgp
            db_ref[...] = dbp

        @pl.when(i > 0)
        def _():
            dg_ref[...] += dgp
            db_ref[...] += dbp

    row = pl.BlockSpec((bm, d), lambda i: (i, 0))
    vec = pl.BlockSpec((1, d), lambda i: (0, 0))
    return _call(
        body, name=name,
        out_shape=(jax.ShapeDtypeStruct((s, d), F32), jax.ShapeDtypeStruct((s, d), BF16),
                   jax.ShapeDtypeStruct((1, d), F32), jax.ShapeDtypeStruct((1, d), F32)),
        grid=(s // bm,), in_specs=[row, row, pl.BlockSpec((bm, 1), lambda i: (i, 0)), vec],
        out_specs=(row, row, vec, vec), args=(dx, xhat, rstd, gamma), vmem_mib=48)


def _to_bf16(x, *, bm, name, comm=None):
    s, d = x.shape
    bm = min(bm, s)
    assert s % bm == 0

    def body(x_ref, o_ref):
        o_ref[...] = x_ref[...].astype(BF16)

    row = pl.BlockSpec((bm, d), lambda i: (i, 0))
    return _call(body, name=name, out_shape=jax.ShapeDtypeStruct((s, d), BF16), grid=(s // bm,),
                 in_specs=[row], out_specs=row, args=(x,), vmem_mib=48, comm=comm)


def _ple_loss(x3, x3b, p, wpg, wpp, target, *, bm, bn, name):
    s, d = x3.shape
    dp = p.shape[1]
    bm, bn = min(bm, s), min(bn, d)
    assert s % bm == 0 and d % bn == 0
    inv_d = 1.0 / d
    chunks = 4 if bm % 64 == 0 else 1
    cr = bm // chunks

    def body(x_ref, xb_ref, p_ref, wg_ref, wp_ref, t_ref, l_ref, dy_ref, dg_ref, dp_ref):
        first = (pl.program_id(0) == 0) & (pl.program_id(1) == 0)

        @pl.when(first)
        def _():
            l_ref[...] = jnp.zeros_like(l_ref)

        part = 0.0
        for r in range(chunks):
            rows = slice(r * cr, (r + 1) * cr)
            gp = jnp.dot(xb_ref[rows, :], wg_ref[...], preferred_element_type=F32)
            pp = lax.dot_general(p_ref[rows, :].astype(BF16), wp_ref[...], NT_DIMS, preferred_element_type=F32)
            sig = jax.nn.sigmoid(gp)
            err = x_ref[rows, :] + sig * pp - t_ref[rows, :]
            part = part + jnp.sum(err * err)
            dy = err * inv_d
            dy_ref[rows, :] = dy
            dg_ref[rows, :] = (dy * pp * sig * (1.0 - sig)).astype(BF16)
            dp_ref[rows, :] = (dy * sig).astype(BF16)
        l_ref[...] += part

    blk = pl.BlockSpec((bm, bn), lambda i, j: (i, j))
    return pl.pallas_call(
        body, name=name,
        out_shape=(jax.ShapeDtypeStruct((8, LANES), F32), jax.ShapeDtypeStruct((s, d), F32),
                   jax.ShapeDtypeStruct((s, d), BF16), jax.ShapeDtypeStruct((s, d), BF16)),
        grid=(s // bm, d // bn),
        in_specs=[blk, pl.BlockSpec((bm, d), lambda i, j: (i, 0)), pl.BlockSpec((bm, dp), lambda i, j: (i, 0)),
                  pl.BlockSpec((d, bn), lambda i, j: (0, j)), pl.BlockSpec((bn, dp), lambda i, j: (j, 0)), blk],
        out_specs=(pl.BlockSpec((8, LANES), lambda i, j: (0, 0)), blk, blk, blk),
        compiler_params=_cp(("arbitrary", "arbitrary"), 56),
    )(x3, x3b, p, wpg, wpp, target)


def _rope_tables(positions):
    half = ROT_DIMS // 2
    lane = jnp.arange(HEAD_DIM)
    inv_freq = jnp.power(jnp.float32(ROPE_THETA), -(lane % half).astype(F32) * (2.0 / ROT_DIMS))
    ang = positions.astype(F32)[:, None] * inv_freq
    cos, sin = jnp.cos(ang), jnp.sin(ang)
    cf = jnp.where(lane < ROT_DIMS, cos, 1.0)
    sa = jnp.where(lane < half, -sin, 0.0)
    sb = jnp.where((lane >= half) & (lane < ROT_DIMS), sin, 0.0)
    return cf, sa, sb


def _rotary(t, tabs, *, n_cols, inverse, out_dtype, bs, name):
    s = t.shape[0]
    bs = min(bs, s)
    half = ROT_DIMS // 2
    heads = N_KV_HEADS
    assert n_cols % heads == 0

    def body(t_ref, cf_ref, sa_ref, sb_ref, o_ref):
        cf, sa, sb = cf_ref[...], sa_ref[...], sb_ref[...]
        for hd in range(heads):
            lanes = slice(hd * HEAD_DIM, (hd + 1) * HEAD_DIM)
            v = t_ref[:, lanes]
            if inverse:
                o = v * cf + pltpu.roll(v * sa, half, 1) + pltpu.roll(v * sb, HEAD_DIM - half, 1)
            else:
                o = v * cf + pltpu.roll(v, HEAD_DIM - half, 1) * sa + pltpu.roll(v, half, 1) * sb
            o_ref[:, lanes] = o.astype(out_dtype)

    blk = pl.BlockSpec((bs, heads * HEAD_DIM), lambda i, j: (i, j))
    tab = pl.BlockSpec((bs, HEAD_DIM), lambda i, j: (i, 0))
    return pl.pallas_call(
        body, name=name, out_shape=jax.ShapeDtypeStruct((s, n_cols * HEAD_DIM), out_dtype),
        grid=(s // bs, n_cols // heads), in_specs=[blk, tab, tab, tab], out_specs=blk,
        compiler_params=_cp(("parallel", "arbitrary"), 32),
    )(t, *tabs)


def _attn_blocks():
    out = []
    for g, dil in enumerate(DILATIONS):
        sup = SPAN * dil
        for j in range(ATTN_TILE // sup):
            for r in range(dil):
                out.append((g, j * sup + r, dil, (j - 1) * sup + r if j > 0 else None, ATTN_TILE - sup + r))
    return out


def _rows(ref, start, dil, lead=None):
    idx = pl.ds(start, SPAN, stride=dil) if dil > 1 else pl.ds(start, SPAN)
    return ref[idx, :] if lead is None else ref[lead, idx, :]


def _band_masks(n):
    qi = lax.broadcasted_iota(jnp.int32, (SPAN, 2 * SPAN), 0)
    ki = lax.broadcasted_iota(jnp.int32, (SPAN, 2 * SPAN), 1)
    band = (ki >= qi) & (ki <= qi + SPAN)
    return band, band & ((ki >= SPAN) | (n > 0))


def _attn_fwd(qkr, proj, *, name):
    s = qkr.shape[0]
    t = ATTN_TILE
    assert s % t == 0
    nt = s // t
    scale = HEAD_DIM ** -0.5
    kcol, vcol = N_PATTERNS * N_KV_HEADS, (N_PATTERNS + 1) * N_KV_HEADS
    blocks = _attn_blocks()

    def body(q0, q1, q2, kc_ref, kp_ref, vc_ref, vp_ref, o_ref, l_ref, og, lg):
        n = pl.program_id(1)
        band, band_first = _band_masks(n)
        q_refs = (q0, q1, q2)
        for g, start, dil, prev_in_tile, prev_start in blocks:
            q = _rows(q_refs[g], start, dil).astype(BF16)
            if prev_in_tile is not None:
                kp, vp, mask = _rows(kc_ref, prev_in_tile, dil), _rows(vc_ref, prev_in_tile, dil), band
            else:
                kp, vp, mask = _rows(kp_ref, prev_start, dil), _rows(vp_ref, prev_start, dil), band_first
            kk = jnp.concatenate([kp, _rows(kc_ref, start, dil)], axis=0).astype(BF16)
            vv = jnp.concatenate([vp, _rows(vc_ref, start, dil)], axis=0).astype(BF16)
            sc = lax.dot_general(q, kk, (((1,), (1,)), ((), ())), preferred_element_type=F32) * scale
            sc = jnp.where(mask, sc, -1e30)
            m = jnp.max(sc, axis=-1, keepdims=True)
            e = jnp.exp(sc - m)
            den = jnp.sum(e, axis=-1, keepdims=True)
            o = jnp.dot(e.astype(BF16), vv, preferred_element_type=F32) / den
            idx = pl.ds(start, SPAN, stride=dil) if dil > 1 else pl.ds(start, SPAN)
            og[g, idx, :] = o
            lg[g, idx, :] = jnp.broadcast_to(m + jnp.log(den), (SPAN, HEAD_DIM))
        l0, l1, l2 = lg[0], lg[1], lg[2]
        m = jnp.maximum(jnp.maximum(l0, l1), l2)
        w0, w1, w2 = jnp.exp(l0 - m), jnp.exp(l1 - m), jnp.exp(l2 - m)
        den = w0 + w1 + w2
        o_ref[...] = (w0 * og[0] + w1 * og[1] + w2 * og[2]) / den
        l_ref[...] = m + jnp.log(den)

    def col(c, prev=False):
        if prev:
            return pl.BlockSpec((t, HEAD_DIM), lambda h, n: (jnp.maximum(n - 1, 0), c + h))
        return pl.BlockSpec((t, HEAD_DIM), lambda h, n: (n, c + h))

    out = jax.ShapeDtypeStruct((s, N_KV_HEADS * HEAD_DIM), F32)
    return pl.pallas_call(
        body, name=name, out_shape=(out, out),
        grid=(N_KV_HEADS, nt),
        in_specs=[col(0), col(N_KV_HEADS), col(2 * N_KV_HEADS), col(kcol), col(kcol, True), col(vcol), col(vcol, True)],
        out_specs=(col(0), col(0)),
        scratch_shapes=[pltpu.VMEM((N_PATTERNS, t, HEAD_DIM), F32), pltpu.VMEM((N_PATTERNS, t, HEAD_DIM), F32)],
        compiler_params=_cp(("parallel", "arbitrary"), 48),
    )(qkr, qkr, qkr, qkr, qkr, proj, proj)


def _attn_bwd(qkr, proj, attn, lse, dcat, *, name, comm=None):
    s = qkr.shape[0]
    t = ATTN_TILE
    nt = s // t
    scale = HEAD_DIM ** -0.5
    kcol, vcol = N_PATTERNS * N_KV_HEADS, (N_PATTERNS + 1) * N_KV_HEADS
    blocks = _attn_blocks()

    def body(q0, q1, q2, kc_ref, kp_ref, vc_ref, vp_ref, o_ref, l_ref, do_ref,
             dq0, dq1, dq2, dk_ref, dv_ref, ck, cv, tkc, tvc, tkp, tvp):
        n = pl.program_id(1)
        for ref in (tkc, tvc, tkp, tvp):
            ref[...] = jnp.zeros_like(ref)

        @pl.when(n < nt)
        def _():
            band, band_first = _band_masks(n)
            q_refs, dq_refs = (q0, q1, q2), (dq0, dq1, dq2)
            for g, start, dil, prev_in_tile, prev_start in blocks:
                idx = pl.ds(start, SPAN, stride=dil) if dil > 1 else pl.ds(start, SPAN)
                q = q_refs[g][idx, :].astype(BF16)
                if prev_in_tile is not None:
                    kp, vp, mask = _rows(kc_ref, prev_in_tile, dil), _rows(vc_ref, prev_in_tile, dil), band
                else:
                    kp, vp, mask = _rows(kp_ref, prev_start, dil), _rows(vp_ref, prev_start, dil), band_first
                kk = jnp.concatenate([kp, kc_ref[idx, :]], axis=0).astype(BF16)
                vv = jnp.concatenate([vp, vc_ref[idx, :]], axis=0).astype(BF16)
                do = do_ref[idx, :]
                dsum = jnp.sum(do * o_ref[idx, :], axis=-1, keepdims=True)
                lrow = l_ref[idx, :][:, :1]
                dob = do.astype(BF16)
                sc = lax.dot_general(q, kk, (((1,), (1,)), ((), ())), preferred_element_type=F32) * scale
                p = jnp.where(mask, jnp.exp(sc - lrow), 0.0)
                dp = lax.dot_general(dob, vv, (((1,), (1,)), ((), ())), preferred_element_type=F32)
                ds = (p * (dp - dsum) * scale).astype(BF16)
                pb = p.astype(BF16)
                dq_refs[g][idx, :] = jnp.dot(ds, kk, preferred_element_type=F32)
                dkk = lax.dot_general(ds, q, (((0,), (0,)), ((), ())), preferred_element_type=F32)
                dvv = lax.dot_general(pb, dob, (((0,), (0,)), ((), ())), preferred_element_type=F32)
                tkc[idx, :] += dkk[SPAN:]
                tvc[idx, :] += dvv[SPAN:]
                if prev_in_tile is not None:
                    pidx = pl.ds(prev_in_tile, SPAN, stride=dil) if dil > 1 else pl.ds(prev_in_tile, SPAN)
                    tkc[pidx, :] += dkk[:SPAN]
                    tvc[pidx, :] += dvv[:SPAN]
                else:
                    pidx = pl.ds(prev_start, SPAN, stride=dil) if dil > 1 else pl.ds(prev_start, SPAN)
                    tkp[pidx, :] += dkk[:SPAN]
                    tvp[pidx, :] += dvv[:SPAN]

        @pl.when(n > 0)
        def _():
            dk_ref[...] = ck[...] + tkp[...]
            dv_ref[...] = (cv[...] + tvp[...]).astype(BF16)

        ck[...] = tkc[...]
        cv[...] = tvc[...]

    def col(c, prev=False):
        if prev:
            return pl.BlockSpec((t, HEAD_DIM), lambda h, n: (jnp.maximum(jnp.minimum(n, nt - 1) - 1, 0), c + h))
        return pl.BlockSpec((t, HEAD_DIM), lambda h, n: (jnp.minimum(n, nt - 1), c + h))

    kv_out = pl.BlockSpec((t, HEAD_DIM), lambda h, n: (jnp.maximum(n - 1, 0), h))
    tile = pltpu.VMEM((t, HEAD_DIM), F32)
    per_head = jax.ShapeDtypeStruct((s, N_KV_HEADS * HEAD_DIM), F32)
    return _call(
        body, name=name,
        out_shape=(per_head, per_head, per_head, per_head, jax.ShapeDtypeStruct((s, N_KV_HEADS * HEAD_DIM), BF16)),
        grid=(N_KV_HEADS, nt + 1),
        in_specs=[col(0), col(N_KV_HEADS), col(2 * N_KV_HEADS), col(kcol), col(kcol, True), col(vcol), col(vcol, True),
                  col(0), col(0), col(0)],
        out_specs=(col(0), col(0), col(0), kv_out, kv_out),
        scratch_shapes=[tile] * 6,
        args=(qkr, qkr, qkr, qkr, qkr, proj, proj, attn, lse, dcat), vmem_mib=48, comm=comm)


GELU_C0 = 0.7978845608028654
GELU_C1 = 0.044715


def _softplus_neg(lam):
    y = jnp.exp(-jnp.abs(lam))
    w = 1.0 + y
    log1p = jnp.where(w == 1.0, y, jnp.log(w) * (y / jnp.where(w == 1.0, 1.0, w - 1.0)))
    return jnp.maximum(-lam, 0.0) + log1p


def _down(cur, prev, k, row):
    if k == 0:
        return cur
    return jnp.where(row < k, pltpu.roll(prev, k, 0), pltpu.roll(cur, k, 0))


def _up(cur, nxt, k, row, tt):
    if k == 0:
        return cur
    return jnp.where(row >= tt - k, pltpu.roll(nxt, tt - k, 0), pltpu.roll(cur, tt - k, 0))


def _lru_gates(x, xp, cw, cb, wr, br, wi, bi, lam, row):
    shifts = [_down(x, xp, k, row) for k in range(CONV_WIDTH)]
    xc = cb
    for j in range(CONV_WIDTH):
        xc = xc + cw[j:j + 1, :] * shifts[CONV_WIDTH - 1 - j]
    xcb = xc.astype(BF16)
    r = jax.nn.sigmoid(jnp.dot(xcb, wr, preferred_element_type=F32) + br)
    i = jax.nn.sigmoid(jnp.dot(xcb, wi, preferred_element_type=F32) + bi)
    c = -LRU_C * _softplus_neg(lam)
    la = c * r
    a = jnp.exp(la)
    mult = jnp.sqrt(jnp.tanh(-la) * (a * a + 1.0))
    return shifts, xc, xcb, r, i, c, a, mult


def _lru_fwd(proj, cw, cb, wr, br, wi, bi, lam, *, tt, name):
    s = proj.shape[0]
    nblk = wr.shape[0]
    c = nblk * LANES
    tt = min(tt, s)
    xcol0 = (N_PATTERNS + 2) * N_KV_HEADS
    ycol0 = xcol0 + nblk

    def body(x_ref, y_ref, cw_ref, cb_ref, wr_ref, br_ref, wi_ref, bi_ref, lam_ref, rec_ref, h_ref, xprev, hc):
        n = pl.program_id(1)

        @pl.when(n == 0)
        def _():
            xprev[...] = jnp.zeros_like(xprev)
            hc[...] = jnp.zeros_like(hc)

        row = lax.broadcasted_iota(jnp.int32, (tt, LANES), 0)
        x = x_ref[...]
        _, xc, _, _, i, _, a, mult = _lru_gates(
            x, xprev[...], cw_ref[...], cb_ref[...], wr_ref[0].astype(BF16), br_ref[...],
            wi_ref[0].astype(BF16), bi_ref[...], lam_ref[...], row)
        av, bv = a, mult * (i * xc)
        k = 1
        while k < tt:
            bs = jnp.where(row < k, 0.0, pltpu.roll(bv, k, 0))
            as_ = jnp.where(row < k, 1.0, pltpu.roll(av, k, 0))
            bv = bv + av * bs
            av = av * as_
            k *= 2
        h = bv + av * hc[0:1, :]
        hc[...] = jnp.broadcast_to(h[tt - 1:tt, :], hc.shape)
        h_ref[...] = h
        y = y_ref[...]
        gel = 0.5 * y * (1.0 + jnp.tanh(GELU_C0 * (y + GELU_C1 * y * y * y)))
        rec_ref[...] = (h * gel).astype(BF16)
        xprev[...] = x

    vec = pl.BlockSpec((1, LANES), lambda b, n: (0, b))
    wblk = pl.BlockSpec((1, LANES, LANES), lambda b, n: (b, 0, 0))
    out = pl.BlockSpec((tt, LANES), lambda b, n: (n, b))
    return pl.pallas_call(
        body, name=name,
        out_shape=(jax.ShapeDtypeStruct((s, c), BF16), jax.ShapeDtypeStruct((s, c), F32)),
        grid=(nblk, s // tt),
        in_specs=[pl.BlockSpec((tt, LANES), lambda b, n: (n, xcol0 + b)),
                  pl.BlockSpec((tt, LANES), lambda b, n: (n, ycol0 + b)),
                  pl.BlockSpec((CONV_WIDTH, LANES), lambda b, n: (0, b)), vec, wblk, vec, wblk, vec, vec],
        out_specs=(out, out),
        scratch_shapes=[pltpu.VMEM((tt, LANES), F32), pltpu.VMEM((8, LANES), F32)],
        compiler_params=_cp(("parallel", "arbitrary"), 32),
    )(proj, proj, cw, cb, wr, br, wi, bi, lam)


def _lru_bwd(proj, hseq, dcat, cw, cb, wr, br, wi, bi, lam, *, tt, name, comm=None):
    s = proj.shape[0]
    nblk = wr.shape[0]
    c = nblk * LANES
    tt = min(tt, s)
    nt = s // tt
    xcol0 = (N_PATTERNS + 2) * N_KV_HEADS
    ycol0 = xcol0 + nblk
    rcol0 = N_KV_HEADS

    def body(x_ref, xp_ref, y_ref, h_ref, hp_ref, dr_ref, cw_ref, cb_ref, wr_ref, br_ref, wi_ref, bi_ref, lam_ref,
             dx_ref, dy_ref, dcw_ref, dcb_ref, dwr_ref, dbr_ref, dwi_ref, dbi_ref, dlam_ref, dxc_next, gcar, acar):
        n = pl.program_id(1)
        rt = nt - 1 - n

        @pl.when(n == 0)
        def _():
            for ref in (dxc_next, gcar, acar, dcw_ref, dcb_ref, dwr_ref, dbr_ref, dwi_ref, dbi_ref, dlam_ref):
                ref[...] = jnp.zeros_like(ref)

        row = lax.broadcasted_iota(jnp.int32, (tt, LANES), 0)
        x = x_ref[...]
        xp = jnp.where(rt > 0, xp_ref[...], 0.0)
        cwv = cw_ref[...]
        wrb, wib = wr_ref[0].astype(BF16), wi_ref[0].astype(BF16)
        lam_v = lam_ref[...]
        shifts, xc, xcb, r, i, cc, a, mult = _lru_gates(x, xp, cwv, cb_ref[...], wrb, br_ref[...], wib, bi_ref[...],
                                                        lam_v, row)
        h = h_ref[...]
        hp_last = jnp.where(rt > 0, hp_ref[7:8, :], 0.0)
        hprev = jnp.where(row < 1, hp_last, pltpu.roll(h, 1, 0))
        y = y_ref[...]
        y2 = y * y
        th = jnp.tanh(GELU_C0 * (y + GELU_C1 * y2 * y))
        gel = 0.5 * y * (1.0 + th)
        dgel = 0.5 * (1.0 + th) + 0.5 * y * (1.0 - th * th) * GELU_C0 * (1.0 + 3.0 * GELU_C1 * y2)
        drec = dr_ref[...]
        dy_ref[...] = (drec * h * dgel).astype(BF16)
        av = jnp.where(row >= tt - 1, acar[0:1, :], pltpu.roll(a, tt - 1, 0))
        bv = drec * gel
        k = 1
        while k < tt:
            bs = jnp.where(row >= tt - k, 0.0, pltpu.roll(bv, tt - k, 0))
            as_ = jnp.where(row >= tt - k, 1.0, pltpu.roll(av, tt - k, 0))
            bv = bv + av * bs
            av = av * as_
            k *= 2
        g = bv + av * gcar[0:1, :]
        gcar[...] = jnp.broadcast_to(g[0:1, :], gcar.shape)
        acar[...] = jnp.broadcast_to(a[0:1, :], acar.shape)
        da = g * hprev
        d_ixc = g * mult
        dmult = g * (i * xc)
        di = d_ixc * xc
        dxc = d_ixc * i
        a2 = a * a
        dla = da * a - dmult * (a2 / mult)
        dr = dla * cc
        dsp = jnp.sum(dla * r, axis=0, keepdims=True) * (-LRU_C)
        dlam_ref[...] += dsp * (-jax.nn.sigmoid(-lam_v))
        dzr = dr * r * (1.0 - r)
        dzi = di * i * (1.0 - i)
        dbr_ref[...] += jnp.sum(dzr, axis=0, keepdims=True)
        dbi_ref[...] += jnp.sum(dzi, axis=0, keepdims=True)
        dzrb, dzib = dzr.astype(BF16), dzi.astype(BF16)
        tn = (((0,), (0,)), ((), ()))
        ntd = (((1,), (1,)), ((), ()))
        dwr_ref[0] += lax.dot_general(xcb, dzrb, tn, preferred_element_type=F32)
        dwi_ref[0] += lax.dot_general(xcb, dzib, tn, preferred_element_type=F32)
        dxc = (dxc + lax.dot_general(dzrb, wrb, ntd, preferred_element_type=F32)
               + lax.dot_general(dzib, wib, ntd, preferred_element_type=F32))
        dcb_ref[...] += jnp.sum(dxc, axis=0, keepdims=True)
        dcw_ref[...] += jnp.concatenate(
            [jnp.sum(dxc * shifts[CONV_WIDTH - 1 - j], axis=0, keepdims=True) for j in range(CONV_WIDTH)], axis=0)
        nxt = dxc_next[...]
        dx = cwv[0:1, :] * _up(dxc, nxt, CONV_WIDTH - 1, row, tt)
        for j in range(1, CONV_WIDTH):
            dx = dx + cwv[j:j + 1, :] * _up(dxc, nxt, CONV_WIDTH - 1 - j, row, tt)
        dx_ref[...] = dx.astype(BF16)
        dxc_next[...] = dxc

    def tile(col0, prev=False):
        if prev:
            return pl.BlockSpec((tt, LANES), lambda b, n: (jnp.maximum(nt - 2 - n, 0), col0 + b))
        return pl.BlockSpec((tt, LANES), lambda b, n: (nt - 1 - n, col0 + b))

    vec = pl.BlockSpec((1, LANES), lambda b, n: (0, b))
    wblk = pl.BlockSpec((1, LANES, LANES), lambda b, n: (b, 0, 0))
    cwblk = pl.BlockSpec((CONV_WIDTH, LANES), lambda b, n: (0, b))
    hp8 = pl.BlockSpec((8, LANES), lambda b, n: (jnp.maximum((nt - 1 - n) * (tt // 8) - 1, 0), b))
    vshape = jax.ShapeDtypeStruct((1, c), F32)
    wshape = jax.ShapeDtypeStruct((nblk, LANES, LANES), F32)
    return _call(
        body, name=name,
        out_shape=(jax.ShapeDtypeStruct((s, c), BF16), jax.ShapeDtypeStruct((s, c), BF16),
                   jax.ShapeDtypeStruct((CONV_WIDTH, c), F32), vshape, wshape, vshape, wshape, vshape, vshape),
        grid=(nblk, nt),
        in_specs=[tile(xcol0), tile(xcol0, True), tile(ycol0), tile(0), hp8, tile(rcol0),
                  cwblk, vec, wblk, vec, wblk, vec, vec],
        out_specs=(tile(0), tile(0), cwblk, vec, wblk, vec, wblk, vec, vec),
        scratch_shapes=[pltpu.VMEM((tt, LANES), F32), pltpu.VMEM((8, LANES), F32), pltpu.VMEM((8, LANES), F32)],
        args=(proj, proj, proj, hseq, hseq, dcat, cw, cb, wr, br, wi, bi, lam), vmem_mib=32, comm=comm)


ROW_BLOCKS = (512, 256, 176, 128, 64, 32, 16, 8)


def _adamw(w, m, v, gparts, *, name):
    r, c = w.shape
    npart = gparts.shape[0]
    br = _pick(r, ROW_BLOCKS)
    c1 = 1.0 - ADAM_B1 ** ADAM_STEP
    c2 = 1.0 - ADAM_B2 ** ADAM_STEP

    def body(w_ref, m_ref, v_ref, g_ref, go_ref, d_ref, mo_ref, vo_ref):
        g = g_ref[0].astype(F32)
        for q in range(1, npart):
            g = g + g_ref[q].astype(F32)
        mn = ADAM_B1 * m_ref[...] + (1.0 - ADAM_B1) * g
        vn = ADAM_B2 * v_ref[...] + (1.0 - ADAM_B2) * (g * g)
        go_ref[...] = g
        mo_ref[...] = mn
        vo_ref[...] = vn
        d_ref[...] = -ADAM_LR * ((mn / c1) / (jnp.sqrt(vn / c2) + ADAM_EPS) + ADAM_WD * w_ref[...])

    blk = pl.BlockSpec((br, c), lambda i: (i, 0))
    out = jax.ShapeDtypeStruct((r, c), F32)
    return pl.pallas_call(
        body, name=name, out_shape=(out, out, out, out), grid=(r // br,),
        in_specs=[blk, blk, blk, pl.BlockSpec((npart, br, c), lambda i: (0, i, 0))],
        out_specs=(blk, blk, blk, blk),
        compiler_params=_cp(("parallel",), 48),
    )(w, m, v, gparts)


def _sum_parts(parts, *, name):
    npart, r, c = parts.shape
    br = next((b for b in range(min(r, 2048) // 8 * 8, 0, -8) if r % b == 0), r)

    def body(p_ref, o_ref):
        acc = p_ref[0]
        for q in range(1, npart):
            acc = acc + p_ref[q]
        o_ref[...] = acc

    return pl.pallas_call(
        body, name=name, out_shape=jax.ShapeDtypeStruct((r, c), F32), grid=(r // br,),
        in_specs=[pl.BlockSpec((npart, br, c), lambda i: (0, i, 0))],
        out_specs=pl.BlockSpec((br, c), lambda i: (i, 0)),
        compiler_params=_cp(("parallel",), 48),
    )(parts)


HBM = pl.BlockSpec(memory_space=pltpu.HBM)


def _mesh_pos():
    return lax.axis_index("x"), lax.axis_index("y"), lax.axis_index("c")


def _gather_comm(shards):
    na = len(shards)

    def parts(x_refs, out_refs, sems):
        send_sems, recv_sems, local_sems = sems
        x, y, c = _mesh_pos()
        me, sibling = (x, y, c), (x, y, 1 - c)
        chips = [(1 - x, y), (x, 1 - y), (1 - x, 1 - y)]

        def copy(a, k, block, to, src=None):
            px, py, pc = block
            dst = out_refs[a].at[4 * px + 2 * py + pc]
            return pltpu.make_async_remote_copy(
                src_ref=dst if src is None else src, dst_ref=dst,
                send_sem=send_sems.at[a, k], recv_sem=recv_sems.at[a, k],
                device_id=to, device_id_type=MESH)

        def mine(a):
            return pltpu.make_async_copy(x_refs[a], out_refs[a].at[4 * x + 2 * y + c], local_sems.at[a])

        def first(a):
            return [copy(a, 0, me, sibling, src=x_refs[a])] + [
                copy(a, 1 + j, me, (*chip, c), src=x_refs[a]) for j, chip in enumerate(chips)]

        def passed(a, j):
            return copy(a, 4 + j, (*chips[j], c), sibling)

        return me, sibling, chips, c, copy, mine, first, passed

    def start(x_refs, out_refs, sems):
        *_, mine, first, _ = parts(x_refs, out_refs, sems)
        for a in range(na):
            mine(a).start()
            for cp in first(a):
                cp.start()

    def mid(x_refs, out_refs, sems):
        me, _, chips, c, copy, _, _, passed = parts(x_refs, out_refs, sems)
        for j, chip in enumerate(chips):
            for a in range(na):
                copy(a, 1 + j, (*chip, c), me).wait_recv()
                passed(a, j).start()

    def end(x_refs, out_refs, sems):
        me, sibling, chips, c, copy, mine, first, passed = parts(x_refs, out_refs, sems)
        for a in range(na):
            copy(a, 0, sibling, me).wait_recv()
            for j, chip in enumerate(chips):
                copy(a, 4 + j, (*chip, 1 - c), me).wait_recv()
        for a in range(na):
            for cp in first(a) + [passed(a, j) for j in range(3)]:
                cp.wait_send()
            mine(a).wait()

    return _Comm(
        shards, [jax.ShapeDtypeStruct((N_DEV,) + a.shape, a.dtype) for a in shards],
        [pltpu.SemaphoreType.DMA((na, 7)), pltpu.SemaphoreType.DMA((na, 7)), pltpu.SemaphoreType.DMA((na,))],
        start, end, mid)


def _scatter_comm(g8s):
    na = len(g8s)

    def parts(g_refs, buf_refs, sems):
        send_sems, recv_sems, local_sems = sems
        x, y, c = _mesh_pos()
        me_idx = 4 * x + 2 * y + c

        def copy(a, k, slot):
            peer, peer_idx = _scatter_peer(k, x, y, c)
            return pltpu.make_async_remote_copy(
                src_ref=g_refs[a].at[peer_idx], dst_ref=buf_refs[a].at[me_idx if slot is None else slot],
                send_sem=send_sems.at[a, k - 1], recv_sem=recv_sems.at[a, k - 1],
                device_id=peer, device_id_type=MESH)

        def mine(a):
            return pltpu.make_async_copy(g_refs[a].at[me_idx], buf_refs[a].at[me_idx], local_sems.at[a])

        return x, y, c, copy, mine

    def start(g_refs, buf_refs, sems):
        *_, copy, mine = parts(g_refs, buf_refs, sems)
        for a in range(na):
            mine(a).start()
            for k in range(1, N_DEV):
                copy(a, k, None).start()

    def end(g_refs, buf_refs, sems):
        x, y, c, copy, mine = parts(g_refs, buf_refs, sems)
        for a in range(na):
            for k in range(1, N_DEV):
                copy(a, k, _scatter_peer(k, x, y, c)[1]).wait_recv()
        for a in range(na):
            for k in range(1, N_DEV):
                copy(a, k, None).wait_send()
            mine(a).wait()

    return _Comm(
        g8s, [jax.ShapeDtypeStruct(g.shape, g.dtype) for g in g8s],
        [pltpu.SemaphoreType.DMA((na, N_DEV - 1)), pltpu.SemaphoreType.DMA((na, N_DEV - 1)),
         pltpu.SemaphoreType.DMA((na,))],
        start, end)


def _scatter_peer(k, x, y, c):
    px, py, pc = (1 - x if k & 4 else x, 1 - y if k & 2 else y, 1 - c if k & 1 else c)
    return (px, py, pc), 4 * px + 2 * py + pc


def _scatter_start_comm(g8s):
    na = len(g8s)
    arrays = []
    for g8 in g8s:
        arrays += [g8, lax.empty(g8.shape, g8.dtype)]

    def local(refs, sems, a, me_idx):
        return pltpu.make_async_copy(refs[2 * a].at[me_idx], refs[2 * a + 1].at[me_idx], sems[0].at[a])

    def start(refs, outs, sems):
        send_sems, recv_sems = outs[:2]
        x, y, c = _mesh_pos()
        me_idx = 4 * x + 2 * y + c
        for a in range(na):
            local(refs, sems, a, me_idx).start()
            for k in range(1, N_DEV):
                peer, peer_idx = _scatter_peer(k, x, y, c)
                pltpu.make_async_remote_copy(
                    src_ref=refs[2 * a].at[peer_idx], dst_ref=refs[2 * a + 1].at[me_idx],
                    send_sem=send_sems.at[a * (N_DEV - 1) + k - 1], recv_sem=recv_sems.at[a * (N_DEV - 1) + k - 1],
                    device_id=peer, device_id_type=MESH).start()

    def end(refs, outs, sems):
        x, y, c = _mesh_pos()
        for a in range(na):
            local(refs, sems, a, 4 * x + 2 * y + c).wait()

    sem_shape = pltpu.SemaphoreType.DMA((na * (N_DEV - 1),))
    return _Comm(arrays, [sem_shape, sem_shape] + [pltpu.HBM(a.shape, a.dtype) for a in arrays],
                 [pltpu.SemaphoreType.DMA((na,))], start, end, split=True)


def _scatter_wait(started, after, *, name):
    send_sems, recv_sems, *arrays = started
    na = len(arrays) // 2

    def body(*refs):
        send_ref, recv_ref = refs[2 * na], refs[2 * na + 1]
        x, y, c = _mesh_pos()
        for a in range(na):
            for k in range(1, N_DEV):
                peer, peer_idx = _scatter_peer(k, x, y, c)
                pltpu.make_async_remote_copy(
                    src_ref=refs[2 * a].at[peer_idx], dst_ref=refs[2 * a + 1].at[peer_idx],
                    send_sem=send_ref.at[a * (N_DEV - 1) + k - 1], recv_sem=recv_ref.at[a * (N_DEV - 1) + k - 1],
                    device_id=peer, device_id_type=MESH).wait()

    sem = pl.BlockSpec(memory_space=pltpu.SEMAPHORE)
    outs = pl.pallas_call(
        body, name=name, out_shape=tuple(pltpu.HBM(a.shape, a.dtype) for a in arrays),
        in_specs=[HBM] * (2 * na) + [sem, sem, pl.BlockSpec(memory_space=pl.ANY)], out_specs=(HBM,) * (2 * na),
        input_output_aliases={k: k for k in range(2 * na)},
        compiler_params=pltpu.CompilerParams(has_side_effects=pltpu.SideEffectType.DATAFLOW_SIDE_EFFECTING),
    )(*arrays, send_sems, recv_sems, after)
    return outs[1::2]


BIG_WEIGHTS = ("ffn1_w_gate", "ffn1_w_up", "ffn1_w_down", "w_in", "w_out",
               "ffn2_w_gate", "ffn2_w_up", "ffn2_w_down", "w_ple_proj", "w_ple_gate")
COLUMN_SHARDED = ("ffn1_w_gate", "ffn1_w_up", "w_in", "ffn2_w_gate", "ffn2_w_up", "w_ple_proj", "conv_w")
SMALL_WEIGHTS = ("ln1_g", "ln1_b", "conv_b", "w_rgate", "b_rgate", "w_igate", "b_igate", "lru_lambda",
                 "ln2_g", "ln2_b", "ln3_g", "ln3_b")
SMALL_GRADS = SMALL_WEIGHTS + ("conv_w",)


class _Exchange:
    def __init__(self, full):
        self.full = dict(full)
        self.grads = {}

    def __getitem__(self, name):
        return self.full[name]

    def first_gather(self, x):
        return _to_bf16(x, bm=1024, name="x_bf16")

    def gather(self, names):
        return None, None

    def scatter(self, names):
        return None, None

    def scatter_start(self, names):
        return None, None

    def gather_small(self):
        return None, None


class _MeshExchange(_Exchange):
    def __init__(self, full, shards, conv_w):
        super().__init__(full)
        self.shards = shards
        self.conv_w = conv_w
        self.reduced = {}
        self.started = {}
        self.small_parts = None

    def first_gather(self, x):
        first = ("ffn1_w_gate", "ffn1_w_up")
        xb, (gate, up, conv_all) = _to_bf16(
            x, bm=1024, name="x_bf16", comm=_gather_comm([self.shards[n] for n in first] + [self.conv_w]))
        self.take(first[0], gate)
        self.take(first[1], up)
        self.full["conv_w"] = _to_full("conv_w", conv_all)
        return xb

    def gather(self, names):
        def done(outs):
            for n, o in zip(names, outs):
                self.take(n, o)
        return _gather_comm([self.shards[n] for n in names]), done

    def take(self, name, gathered):
        self.full[name] = gathered.reshape((N_DEV * gathered.shape[1],) + gathered.shape[2:])

    def scatter(self, names):
        def done(outs):
            self.reduced.update(zip(names, outs))
        return _scatter_comm([_to_owner_blocks(n, self.grads[n]) for n in names]), done

    def scatter_start(self, names):
        def done(outs):
            self.started[names] = outs
        return _scatter_start_comm([_to_owner_blocks(n, self.grads[n]) for n in names]), done

    def finish(self, after):
        for names, started in self.started.items():
            self.reduced.update(zip(names, _scatter_wait(started, after, name=f"scatter_wait_{names[0]}")))

    def gather_small(self):
        def done(outs):
            self.small_parts, = outs
        packed = jnp.concatenate([_rows128(self.grads[n]) for n in SMALL_GRADS], axis=0)
        return _gather_comm([packed]), done


def _carried(comm_done, call):
    comm, done = comm_done
    res = call(comm)
    if comm is None:
        return res
    res, outs = res
    done(outs)
    return res


def _dw(a, b, *, scale=1.0, name, comm=None):
    k, m = a.shape
    n = b.shape[1]
    return _mm(a, b, ta=True, scale=scale, out_dtype=BF16, bm=_pick(m, (1024, 512, 256, 128)),
               bn=_pick(n, (512, 256, 128)), bk=k, name=name, comm=comm)


def _ffn_bwd(ex, names, saved, xb_in, dz, dzb, ln_in, tag, on_dwd=None, on_dh=None, on_dwu=None, on_dx=None):
    gate, up, down = names
    g, u, h, _, _ = saved
    f = ex[gate].shape[0]

    def request(fn):
        return (None, None) if fn is None else fn(ex)

    ex.grads[down] = _carried(request(on_dwd), lambda c: _dw(h, dzb, scale=0.5, name=f"{tag}_dwd", comm=c))
    dg, du = _carried(request(on_dh), lambda c: _ffn_bwd_dh(
        dzb, ex[down], g, u, scale=0.5, bm=2048, bn=_pick(f, (512, 256, 128)), name=f"{tag}_dh", chunks=8, comm=c))
    ex.grads[gate] = _dw(xb_in, dg, name=f"{tag}_dwg")
    ex.grads[up] = _carried(request(on_dwu), lambda c: _dw(xb_in, du, name=f"{tag}_dwu", comm=c))
    d = dz.shape[1]
    dx = _carried(request(on_dx), lambda c: _ffn_dx(
        dg, du, ex[gate], ex[up], dz, extra_scale=DEEPNORM_ALPHA,
        bm=1024, bn=_pick(d, (512, 256, 128)), name=f"{tag}_dx", comm=c))
    return dx if ln_in is None else _ln_bwd(dx, *ln_in, bm=256, name=f"{tag}_ln_bwd")


def _local_step(x, p, target, positions, w):
    s, d = x.shape
    tabs = _rope_tables(positions)
    xb = w.first_gather(x)
    f = w["ffn1_w_gate"].shape[0]
    ffn_bn, ln_bn, ln_bn_short_k = _pick(f, (512, 256, 128)), _pick(d, (512, 256, 128)), _pick(d, (1024, 512, 256, 128))
    g1, u1, h1 = _carried(w.gather(("ffn1_w_down", "w_in")), lambda c: _ffn_up(
        xb, w["ffn1_w_gate"], w["ffn1_w_up"], bm=1024, bn=ffn_bn, name="ffn1_up", comm=c))
    x1, x1b, xh1, rs1 = _carried(w.gather(("w_out", "ffn2_w_gate")), lambda c: _mm_ln(
        h1, w["ffn1_w_down"], x, w["ln1_g"], w["ln1_b"], res_scale=DEEPNORM_ALPHA, mm_scale=0.5,
        bm=512, bn=ln_bn, name="ffn1_down_ln", comm=c))
    sv1 = (g1, u1, h1, xh1, rs1)
    pw = w["w_in"].shape[0]
    proj = _carried(w.gather(("ffn2_w_up",)), lambda c: _mm(
        x1b, w["w_in"], tb=True, bm=1024, bn=_pick(pw, (512, 256, 128)), bk=d, name="in_proj", comm=c))
    nqk = (N_PATTERNS + 1) * N_KV_HEADS
    qkr = _rotary(proj, tabs, n_cols=nqk, inverse=False, out_dtype=F32, bs=1024, name="rotary")
    attn, lse = _attn_fwd(qkr, proj, name="attn_fwd")
    lru_w = (w["conv_w"], w["conv_b"], w["w_rgate"], w["b_rgate"], w["w_igate"], w["b_igate"], w["lru_lambda"])
    rec, hseq = _lru_fwd(proj, *lru_w, tt=512, name="lru_fwd")
    cat = jnp.concatenate([attn.astype(BF16), rec], axis=1)
    x2, x2b, xh2, rs2 = _carried(w.gather(("w_ple_gate", "w_ple_proj")), lambda c: _mm_ln(
        cat, w["w_out"], x1, w["ln2_g"], w["ln2_b"], res_scale=DEEPNORM_ALPHA, mm_scale=1.0,
        bm=512, bn=ln_bn_short_k, name="out_proj_ln", comm=c))
    g2, u2, h2 = _carried(w.gather(("ffn2_w_down",)), lambda c: _ffn_up(
        x2b, w["ffn2_w_gate"], w["ffn2_w_up"], bm=1024, bn=ffn_bn, name="ffn2_up", comm=c))
    x3, x3b, xh3, rs3 = _mm_ln(h2, w["ffn2_w_down"], x2, w["ln3_g"], w["ln3_b"], res_scale=DEEPNORM_ALPHA, mm_scale=0.5,
                               bm=512, bn=ln_bn, name="ffn2_down_ln")
    sv3 = (g2, u2, h2, xh3, rs3)
    lsum, dy, dgate, dple = _ple_loss(x3, x3b, p, w["w_ple_gate"], w["w_ple_proj"], target,
                                      bm=1024, bn=_pick(d, (512, 256, 128)), name="ple_loss")
    grads = w.grads
    grads["w_ple_gate"] = _dw(x3b, dgate, name="dw_ple_gate")
    grads["w_ple_proj"] = _dw(p, dple, name="dw_ple_proj")
    dz3, dz3b, grads["ln3_g"], grads["ln3_b"] = _carried(w.scatter(("w_ple_gate", "w_ple_proj")), lambda c: _mm_dx(
        dgate, w["w_ple_gate"], dy, xh3, rs3, w["ln3_g"], extra_scale=1.0, bm=512, bn=ln_bn_short_k,
        name="ple_dx", tb=True, comm=c))
    dz2, dz2b, grads["ln2_g"], grads["ln2_b"] = _ffn_bwd(
        w, ("ffn2_w_gate", "ffn2_w_up", "ffn2_w_down"), sv3, x2b, dz3, dz3b, (xh2, rs2, w["ln2_g"]), "ffn2",
        on_dx=lambda ex: ex.scatter_start(("ffn2_w_down", "ffn2_w_gate", "ffn2_w_up")))
    grads["w_out"] = _dw(cat, dz2b, name="dw_out")
    dcat = _mm(dz2b, w["w_out"], tb=True, bm=1024, bn=_pick(d, (512, 256, 128)), bk=d, name="out_proj_dx")
    dq0, dq1, dq2, dk, dvb = _attn_bwd(qkr, proj, attn, lse, dcat, name="attn_bwd")
    nh = N_KV_HEADS
    dqkv = [_rotary(t, tabs, n_cols=nh, inverse=True, out_dtype=BF16, bs=1024, name=f"rotary_bwd{i}")
            for i, t in enumerate((dq0, dq1, dq2, dk))]
    (dxb, dyb, grads["conv_w"], grads["conv_b"], grads["w_rgate"], grads["b_rgate"], grads["w_igate"],
     grads["b_igate"], grads["lru_lambda"]) = _lru_bwd(proj, hseq, dcat, *lru_w, tt=512, name="lru_bwd")
    dproj = jnp.concatenate(dqkv + [dvb, dxb, dyb], axis=1)
    grads["w_in"] = _dw(x1b, dproj, name="dw_in")
    dz1, dz1b, grads["ln1_g"], grads["ln1_b"] = _carried(w.scatter_start(("w_out", "w_in")), lambda c: _mm_dx(
        dproj, w["w_in"], dz2, xh1, rs1, w["ln1_g"], extra_scale=DEEPNORM_ALPHA,
        bm=512, bn=ln_bn, name="in_proj_dx", comm=c))
    grad_x = _ffn_bwd(w, ("ffn1_w_gate", "ffn1_w_up", "ffn1_w_down"), sv1, xb, dz1, dz1b, None, "ffn1",
                      on_dwd=lambda ex: ex.gather_small(),
                      on_dh=lambda ex: ex.scatter_start(("ffn1_w_down",)),
                      on_dwu=lambda ex: ex.scatter_start(("ffn1_w_gate",)),
                      on_dx=lambda ex: ex.scatter_start(("ffn1_w_up",)))
    return lsum, grad_x


def _to_full(name, gathered):
    if name in COLUMN_SHARDED:
        _, r, c = gathered.shape
        return jnp.transpose(gathered, (1, 0, 2)).reshape(r, N_DEV * c)
    return gathered.reshape((N_DEV * gathered.shape[1],) + gathered.shape[2:])


def _to_owner_blocks(name, full):
    if name in COLUMN_SHARDED:
        r, c = full.shape
        return jnp.transpose(full.reshape(r, N_DEV, c // N_DEV), (1, 0, 2))
    return full.reshape((N_DEV, full.shape[0] // N_DEV) + full.shape[1:])


def _rows128(a):
    flat = a.reshape(-1, LANES)
    pad = (-flat.shape[0]) % 8
    return jnp.pad(flat, ((0, pad), (0, 0))) if pad else flat


def kernel(x, p, positions, ffn1_w_gate, ffn1_w_up, ffn1_w_down, ln1_g, ln1_b, w_in, conv_w, conv_b, w_rgate, b_rgate, w_igate, b_igate, lru_lambda, w_out, ln2_g, ln2_b, ffn2_w_gate, ffn2_w_up, ffn2_w_down, ln3_g, ln3_b, w_ple_proj, w_ple_gate, loss_target, m_ffn1_w_gate, m_ffn1_w_up, m_ffn1_w_down, m_ln1_g, m_ln1_b, m_w_in, m_conv_w, m_conv_b, m_w_rgate, m_b_rgate, m_w_igate, m_b_igate, m_lru_lambda, m_w_out, m_ln2_g, m_ln2_b, m_ffn2_w_gate, m_ffn2_w_up, m_ffn2_w_down, m_ln3_g, m_ln3_b, m_w_ple_proj, m_w_ple_gate, v_ffn1_w_gate, v_ffn1_w_up, v_ffn1_w_down, v_ln1_g, v_ln1_b, v_w_in, v_conv_w, v_conv_b, v_w_rgate, v_b_rgate, v_w_igate, v_b_igate, v_lru_lambda, v_w_out, v_ln2_g, v_ln2_b, v_ffn2_w_gate, v_ffn2_w_up, v_ffn2_w_down, v_ln3_g, v_ln3_b, v_w_ple_proj, v_w_ple_gate):
    names = ("ffn1_w_gate", "ffn1_w_up", "ffn1_w_down", "ln1_g", "ln1_b", "w_in", "conv_w", "conv_b", "w_rgate",
             "b_rgate", "w_igate", "b_igate", "lru_lambda", "w_out", "ln2_g", "ln2_b", "ffn2_w_gate", "ffn2_w_up",
             "ffn2_w_down", "ln3_g", "ln3_b", "w_ple_proj", "w_ple_gate")
    ws = (ffn1_w_gate, ffn1_w_up, ffn1_w_down, ln1_g, ln1_b, w_in, conv_w, conv_b, w_rgate, b_rgate, w_igate, b_igate,
          lru_lambda, w_out, ln2_g, ln2_b, ffn2_w_gate, ffn2_w_up, ffn2_w_down, ln3_g, ln3_b, w_ple_proj, w_ple_gate)
    ms = (m_ffn1_w_gate, m_ffn1_w_up, m_ffn1_w_down, m_ln1_g, m_ln1_b, m_w_in, m_conv_w, m_conv_b, m_w_rgate, m_b_rgate,
          m_w_igate, m_b_igate, m_lru_lambda, m_w_out, m_ln2_g, m_ln2_b, m_ffn2_w_gate, m_ffn2_w_up, m_ffn2_w_down,
          m_ln3_g, m_ln3_b, m_w_ple_proj, m_w_ple_gate)
    vs = (v_ffn1_w_gate, v_ffn1_w_up, v_ffn1_w_down, v_ln1_g, v_ln1_b, v_w_in, v_conv_w, v_conv_b, v_w_rgate, v_b_rgate,
          v_w_igate, v_b_igate, v_lru_lambda, v_w_out, v_ln2_g, v_ln2_b, v_ffn2_w_gate, v_ffn2_w_up, v_ffn2_w_down,
          v_ln3_g, v_ln3_b, v_w_ple_proj, v_w_ple_gate)
    def local(a):
        return a[0] if a.ndim >= 3 else a

    w_loc = {n: local(a) for n, a in zip(names, ws)}
    m_loc = {n: local(a) for n, a in zip(names, ms)}
    v_loc = {n: local(a) for n, a in zip(names, vs)}
    out_shapes = {n: a.shape for n, a in zip(names, ws)}

    shards = {n: (w_loc[n].T if n in COLUMN_SHARDED else w_loc[n]).astype(BF16) for n in BIG_WEIGHTS}
    ex = _MeshExchange({n: w_loc[n] for n in SMALL_WEIGHTS}, shards, w_loc["conv_w"])

    lsum, grad_x = _local_step(x[0], p[0, 0], loss_target[0], positions[0], ex)
    ex.finish(grad_x)
    grads, reduced = ex.grads, ex.reduced
    d_model = x.shape[-1]
    loss = lax.psum(lsum[0, 0] * (0.5 / d_model), ("x", "y", "c"))

    small = SMALL_GRADS
    summed = _sum_parts(ex.small_parts, name="sum_small_grads")
    small_grads, row = {}, 0
    for n in small:
        rows = grads[n].size // LANES
        small_grads[n] = summed[row:row + rows].reshape(grads[n].shape)
        row += rows + (-rows) % 8
    me = 4 * lax.axis_index("x") + 2 * lax.axis_index("y") + lax.axis_index("c")
    cw_cols = w_loc["conv_w"].shape[1]
    small_grads["conv_w"] = lax.dynamic_slice_in_dim(small_grads["conv_w"], me * cw_cols, cw_cols, axis=1)

    out_g, out_d, out_m, out_v = {}, {}, {}, {}
    for n in names:
        wl, ml, vl = w_loc[n], m_loc[n], v_loc[n]
        shape = wl.shape
        if n in BIG_WEIGHTS:
            gparts = reduced[n]
        else:
            gparts = small_grads[n].reshape((1,) + shape)
        if wl.ndim == 3:
            wl, ml, vl = (t.reshape(-1, shape[-1]) for t in (wl, ml, vl))
            gparts = gparts.reshape(gparts.shape[0], -1, shape[-1])
        res = _adamw(wl, ml, vl, gparts, name=f"adamw_{n}")
        out_g[n], out_d[n], out_m[n], out_v[n] = (t.reshape(out_shapes[n]) for t in res)

    return (loss, grad_x[None], *[out_g[n] for n in names], *[out_d[n] for n in names],
            *[out_m[n] for n in names], *[out_v[n] for n in names])
```

```python
import jax
import jax.numpy as jnp
from jax import lax
from jax.experimental import pallas as pl
from jax.experimental.pallas import tpu as pltpu

F32 = jnp.float32
BF16 = jnp.bfloat16

N_DEV = 8
LANES = 128
MIB = 1 << 20

HEAD_DIM = 128
N_KV_HEADS = 4
DILATIONS = (1, 4, 16)
N_PATTERNS = 3
SPAN = 128
ROT_DIMS = 32
ROPE_THETA = 500000.0
LRU_C = 8.0
CONV_WIDTH = 4
LN_EPS = 1e-5
DEEPNORM_ALPHA = 2.0 ** 0.25
ATTN_TILE = SPAN * DILATIONS[-1]

ADAM_LR = 0.001
ADAM_B1 = 0.9
ADAM_B2 = 0.999
ADAM_EPS = 1e-08
ADAM_WD = 0.01
ADAM_STEP = 10

MESH = pl.DeviceIdType.MESH
NT_DIMS = (((1,), (1,)), ((), ()))
EPILOGUE_ROWS = 64


def _cp(semantics, vmem_mib):
    return pltpu.CompilerParams(dimension_semantics=semantics, vmem_limit_bytes=vmem_mib * MIB)


def _pick(n, candidates):
    for c in candidates:
        if n % c == 0:
            return c
    return n


class _Comm:
    def __init__(self, arrays, out_shapes, scratch, start, end, mid=None, split=False):
        self.arrays, self.out_shapes, self.scratch = list(arrays), list(out_shapes), list(scratch)
        self.start, self.mid, self.end, self.split = start, mid, end, split


def _call(body, *, name, grid, in_specs, out_specs, out_shape, args, scratch_shapes=(), vmem_mib, comm=None):
    single = not isinstance(out_shape, (tuple, list))
    out_shape_t = (out_shape,) if single else tuple(out_shape)
    out_specs_t = (out_specs,) if single else tuple(out_specs)
    params = _cp(("arbitrary",) * len(grid), vmem_mib)
    if comm is None:
        res = pl.pallas_call(body, name=name, grid=grid, in_specs=list(in_specs), out_specs=out_specs_t,
                             out_shape=out_shape_t, scratch_shapes=list(scratch_shapes), compiler_params=params)(*args)
        return res[0] if single else res
    n_in, n_out, n_scr = len(args), len(out_shape_t), len(scratch_shapes)
    nci, nco = len(comm.arrays), len(comm.out_shapes)
    total = 1
    for g in grid:
        total *= g

    def wrapped(*refs):
        ins, refs = refs[:n_in], refs[n_in:]
        cin, refs = refs[:nci], refs[nci:]
        outs, refs = refs[:n_out], refs[n_out:]
        cout, refs = refs[:nco], refs[nco:]
        scr, csem = refs[:n_scr], refs[n_scr:]
        step = pl.program_id(0)
        for ax in range(1, len(grid)):
            step = step * grid[ax] + pl.program_id(ax)

        @pl.when(step == 0)
        def _():
            comm.start(cin, cout, csem)

        body(*ins, *outs, *scr)
        if comm.mid is not None:
            @pl.when(step == (3 * total) // 4)
            def _():
                comm.mid(cin, cout, csem)

        @pl.when(step == total - 1)
        def _():
            comm.end(cin, cout, csem)

    hbm = pl.BlockSpec(memory_space=pltpu.HBM)
    if comm.split:
        sem = pl.BlockSpec(memory_space=pltpu.SEMAPHORE)
        n_sems = nco - nci
        res = pl.pallas_call(
            wrapped, name=name, grid=grid,
            in_specs=list(in_specs) + [hbm] * nci,
            out_specs=out_specs_t + (sem,) * n_sems + (hbm,) * nci,
            out_shape=out_shape_t + tuple(comm.out_shapes),
            scratch_shapes=list(scratch_shapes) + comm.scratch,
            input_output_aliases={n_in + k: n_out + n_sems + k for k in range(nci)},
            compiler_params=pltpu.CompilerParams(
                dimension_semantics=("arbitrary",) * len(grid), vmem_limit_bytes=vmem_mib * MIB,
                has_side_effects=pltpu.SideEffectType.DATAFLOW_SIDE_EFFECTING),
        )(*args, *[pltpu.with_memory_space_constraint(a, pltpu.HBM) for a in comm.arrays])
    else:
        res = pl.pallas_call(
            wrapped, name=name, grid=grid,
            in_specs=list(in_specs) + [hbm] * nci,
            out_specs=out_specs_t + (hbm,) * nco,
            out_shape=out_shape_t + tuple(comm.out_shapes),
            scratch_shapes=list(scratch_shapes) + comm.scratch,
            compiler_params=params)(*args, *comm.arrays)
    own, extra = res[:n_out], res[n_out:]
    return (own[0] if single else own), extra


def _mm(a, b, *, ta=False, tb=False, out_dtype=F32, scale=1.0, bm, bn, bk, name, comm=None):
    m, k = (a.shape[1], a.shape[0]) if ta else a.shape
    n = b.shape[0] if tb else b.shape[1]
    bm, bn, bk = min(bm, m), min(bn, n), min(bk, k)
    assert m % bm == 0 and n % bn == 0 and k % bk == 0, (name, m, n, k, bm, bn, bk)
    nk = k // bk
    a_spec = pl.BlockSpec((bk, bm), lambda i, j, kk: (kk, i)) if ta else pl.BlockSpec((bm, bk), lambda i, j, kk: (i, kk))
    b_spec = pl.BlockSpec((bn, bk), lambda i, j, kk: (j, kk)) if tb else pl.BlockSpec((bk, bn), lambda i, j, kk: (kk, j))
    dn = (((0 if ta else 1,), (1 if tb else 0,)), ((), ()))

    def body(a_ref, b_ref, o_ref, *acc):
        part = lax.dot_general(a_ref[...].astype(BF16), b_ref[...].astype(BF16), dn, preferred_element_type=F32)
        if nk == 1:
            o_ref[...] = (part * scale).astype(out_dtype)
            return
        acc_ref, = acc
        kk = pl.program_id(2)

        @pl.when(kk == 0)
        def _():
            acc_ref[...] = part

        @pl.when(kk > 0)
        def _():
            acc_ref[...] += part

        @pl.when(kk == nk - 1)
        def _():
            o_ref[...] = (acc_ref[...] * scale).astype(out_dtype)

    return _call(
        body, name=name,
        out_shape=jax.ShapeDtypeStruct((m, n), out_dtype),
        grid=(m // bm, n // bn, nk),
        in_specs=[a_spec, b_spec],
        out_specs=pl.BlockSpec((bm, bn), lambda i, j, kk: (i, j)),
        scratch_shapes=[pltpu.VMEM((bm, bn), F32)] if nk > 1 else [],
        args=(a, b), vmem_mib=56, comm=comm)


def _ffn_up(xb, wg, wu, *, bm, bn, name, comm=None):
    s, d = xb.shape
    f = wg.shape[0]
    bm, bn = min(bm, s), min(bn, f)
    assert s % bm == 0 and f % bn == 0

    def body(x_ref, wg_ref, wu_ref, hg_ref, hu_ref, h_ref):
        x = x_ref[...]
        g = lax.dot_general(x, wg_ref[...], NT_DIMS, preferred_element_type=F32)
        u = lax.dot_general(x, wu_ref[...], NT_DIMS, preferred_element_type=F32)
        sig = jax.nn.sigmoid(g)
        silu = g * sig
        hg_ref[...] = (u * (sig * (1.0 + g * (1.0 - sig)))).astype(BF16)
        hu_ref[...] = silu.astype(BF16)
        h_ref[...] = (silu * u).astype(BF16)

    out = jax.ShapeDtypeStruct((s, f), BF16)
    blk = pl.BlockSpec((bm, bn), lambda i, j: (i, j))
    return _call(
        body, name=name, out_shape=(out, out, out),
        grid=(s // bm, f // bn),
        in_specs=[pl.BlockSpec((bm, d), lambda i, j: (i, 0)),
                  pl.BlockSpec((bn, d), lambda i, j: (j, 0)),
                  pl.BlockSpec((bn, d), lambda i, j: (j, 0))],
        out_specs=(blk, blk, blk),
        args=(xb, wg, wu), vmem_mib=56, comm=comm)


def _ffn_bwd_dh(dzb, wd, g, u, *, scale, bm, bn, name, chunks=2, comm=None):
    s, d = dzb.shape
    f = wd.shape[0]
    bm, bn = min(bm, s), min(bn, f)
    assert s % bm == 0 and f % bn == 0

    cr = bm // chunks

    def body(dz_ref, wd_ref, hg_ref, hu_ref, dg_ref, du_ref):
        for r in range(chunks):
            rows = slice(r * cr, (r + 1) * cr)
            dh = lax.dot_general(dz_ref[rows, :], wd_ref[...], NT_DIMS, preferred_element_type=F32) * scale
            dg_ref[rows, :] = (dh * hg_ref[rows, :].astype(F32)).astype(BF16)
            du_ref[rows, :] = (dh * hu_ref[rows, :].astype(F32)).astype(BF16)

    out = jax.ShapeDtypeStruct((s, f), BF16)
    blk = pl.BlockSpec((bm, bn), lambda i, j: (i, j))
    return _call(
        body, name=name, out_shape=(out, out),
        grid=(s // bm, f // bn),
        in_specs=[pl.BlockSpec((bm, d), lambda i, j: (i, 0)),
                  pl.BlockSpec((bn, d), lambda i, j: (j, 0)), blk, blk],
        out_specs=(blk, blk),
        args=(dzb, wd, g, u), vmem_mib=56, comm=comm)


def _full_rows(acc_ref, rows, nj):
    return jnp.concatenate([acc_ref[jj, rows, :] for jj in range(nj)], axis=1)


def _mm_ln(a, b, res, gamma, beta, *, res_scale, mm_scale, bm, bn, name, comm=None):
    s, k = a.shape
    d = b.shape[1]
    bm, bn = min(bm, s), min(bn, d)
    assert s % bm == 0 and d % bn == 0
    nj = d // bn
    ch = min(EPILOGUE_ROWS, bm)

    def body(a_ref, b_ref, r_ref, g_ref, be_ref, y_ref, yb_ref, xh_ref, rs_ref, acc_ref):
        j = pl.program_id(1)
        acc_ref[j] = jnp.dot(a_ref[...], b_ref[...], preferred_element_type=F32)

        @pl.when(j == nj - 1)
        def _():
            def chunk(ci, carry):
                rows = pl.ds(pl.multiple_of(ci * ch, ch), ch)
                z = res_scale * r_ref[rows, :] + mm_scale * _full_rows(acc_ref, rows, nj)
                mu = jnp.mean(z, axis=-1, keepdims=True)
                zc = z - mu
                var = jnp.mean(zc * zc, axis=-1, keepdims=True)
                rstd = lax.rsqrt(var + LN_EPS)
                xh = zc * rstd
                y = xh * g_ref[...] + be_ref[...]
                y_ref[rows, :] = y
                yb_ref[rows, :] = y.astype(BF16)
                xh_ref[rows, :] = xh
                rs_ref[rows, :] = rstd
                return carry

            lax.fori_loop(0, bm // ch, chunk, 0)

    row = pl.BlockSpec((bm, d), lambda i, j: (i, 0))
    vec = pl.BlockSpec((1, d), lambda i, j: (0, 0))
    return _call(
        body, name=name,
        out_shape=(jax.ShapeDtypeStruct((s, d), F32), jax.ShapeDtypeStruct((s, d), BF16),
                   jax.ShapeDtypeStruct((s, d), F32), jax.ShapeDtypeStruct((s, 1), F32)),
        grid=(s // bm, nj),
        in_specs=[pl.BlockSpec((bm, k), lambda i, j: (i, 0)),
                  pl.BlockSpec((k, bn), lambda i, j: (0, j)), row, vec, vec],
        out_specs=(row, row, row, pl.BlockSpec((bm, 1), lambda i, j: (i, 0))),
        scratch_shapes=[pltpu.VMEM((nj, bm, bn), F32)],
        args=(a, b, res, gamma, beta), vmem_mib=58, comm=comm)


def _mm_dx(a, wt, extra, xhat, rstd, gamma, *, extra_scale, bm, bn, name, tb=False, comm=None):
    s, k = a.shape
    d = wt.shape[0] if tb else wt.shape[1]
    bm, bn = min(bm, s), min(bn, d)
    assert s % bm == 0 and d % bn == 0
    nj = d // bn
    ch = min(EPILOGUE_ROWS, bm)
    dims = NT_DIMS if tb else (((1,), (0,)), ((), ()))

    def body(a_ref, w_ref, e_ref, xh_ref, rs_ref, g_ref, dz_ref, dzb_ref, dg_ref, db_ref, acc_ref):
        i = pl.program_id(0)
        j = pl.program_id(1)
        acc_ref[j] = lax.dot_general(a_ref[...], w_ref[...], dims, preferred_element_type=F32)

        @pl.when(j == nj - 1)
        def _():
            def chunk(ci, carry):
                dgp, dbp = carry
                rows = pl.ds(pl.multiple_of(ci * ch, ch), ch)
                dx = extra_scale * e_ref[rows, :] + _full_rows(acc_ref, rows, nj)
                xh = xh_ref[rows, :]
                dxh = dx * g_ref[...]
                m1 = jnp.mean(dxh, axis=-1, keepdims=True)
                m2 = jnp.mean(dxh * xh, axis=-1, keepdims=True)
                dz = rs_ref[rows, :] * (dxh - m1 - xh * m2)
                dz_ref[rows, :] = dz
                dzb_ref[rows, :] = dz.astype(BF16)
                return dgp + jnp.sum(dx * xh, axis=0, keepdims=True), dbp + jnp.sum(dx, axis=0, keepdims=True)

            zero = jnp.zeros((1, d), F32)
            dgp, dbp = lax.fori_loop(0, bm // ch, chunk, (zero, zero))

            @pl.when(i == 0)
            def _():
                dg_ref[...] = dgp
                db_ref[...] = dbp

            @pl.when(i > 0)
            def _():
                dg_ref[...] += dgp
                db_ref[...] += dbp

    row = pl.BlockSpec((bm, d), lambda i, j: (i, 0))
    vec = pl.BlockSpec((1, d), lambda i, j: (0, 0))
    return _call(
        body, name=name,
        out_shape=(jax.ShapeDtypeStruct((s, d), F32), jax.ShapeDtypeStruct((s, d), BF16),
                   jax.ShapeDtypeStruct((1, d), F32), jax.ShapeDtypeStruct((1, d), F32)),
        grid=(s // bm, nj),
        in_specs=[pl.BlockSpec((bm, k), lambda i, j: (i, 0)),
                  pl.BlockSpec((bn, k), lambda i, j: (j, 0)) if tb else pl.BlockSpec((k, bn), lambda i, j: (0, j)),
                  row, row, pl.BlockSpec((bm, 1), lambda i, j: (i, 0)), vec],
        out_specs=(row, row, vec, vec),
        scratch_shapes=[pltpu.VMEM((nj, bm, bn), F32)],
        args=(a, wt, extra, xhat, rstd, gamma), vmem_mib=58, comm=comm)


def _ffn_dx(dg, du, wgt, wut, extra, *, extra_scale, bm, bn, name, comm=None):
    s, f = dg.shape
    d = wgt.shape[1]
    bm, bn = min(bm, s), min(bn, d)
    assert s % bm == 0 and d % bn == 0 and f % (2 * LANES) == 0
    nj, half = d // bn, f // 2

    def body(dg_ref, du_ref, wg_ref, wu_ref, e_ref, o_ref, acc_ref):
        kh, j = pl.program_id(1), pl.program_id(2)
        part = jnp.dot(dg_ref[...], wg_ref[...], preferred_element_type=F32)
        part = part + jnp.dot(du_ref[...], wu_ref[...], preferred_element_type=F32)

        @pl.when(kh == 0)
        def _():
            acc_ref[j] = part

        @pl.when(kh == 1)
        def _():
            o_ref[...] = extra_scale * e_ref[...] + (acc_ref[j] + part)

    rows = pl.BlockSpec((bm, half), lambda i, kh, j: (i, kh))
    cols = pl.BlockSpec((half, bn), lambda i, kh, j: (kh, j))
    blk = pl.BlockSpec((bm, bn), lambda i, kh, j: (i, j * kh))
    return _call(
        body, name=name, out_shape=jax.ShapeDtypeStruct((s, d), F32),
        grid=(s // bm, 2, nj), in_specs=[rows, rows, cols, cols, blk], out_specs=blk,
        scratch_shapes=[pltpu.VMEM((nj, bm, bn), F32)],
        args=(dg, du, wgt, wut, extra), vmem_mib=58, comm=comm)


def _ln_bwd(dx, xhat, rstd, gamma, *, bm, name):
    s, d = dx.shape
    bm = min(bm, s)
    assert s % bm == 0
    ch = min(EPILOGUE_ROWS, bm)

    def body(dx_ref, xh_ref, rs_ref, g_ref, dz_ref, dzb_ref, dg_ref, db_ref):
        def chunk(ci, carry):
            dgp, dbp = carry
            rows = pl.ds(pl.multiple_of(ci * ch, ch), ch)
            dxv = dx_ref[rows, :]
            xh = xh_ref[rows, :]
            dxh = dxv * g_ref[...]
            m1 = jnp.mean(dxh, axis=-1, keepdims=True)
            m2 = jnp.mean(dxh * xh, axis=-1, keepdims=True)
            dz = rs_ref[rows, :] * (dxh - m1 - xh * m2)
            dz_ref[rows, :] = dz
            dzb_ref[rows, :] = dz.astype(BF16)
            return dgp + jnp.sum(dxv * xh, axis=0, keepdims=True), dbp + jnp.sum(dxv, axis=0, keepdims=True)

        zero = jnp.zeros((1, d), F32)
        dgp, dbp = lax.fori_loop(0, bm // ch, chunk, (zero, zero))
        i = pl.program_id(0)

        @pl.when(i == 0)
        def _():
            dg_ref[...] = dgp
            db_ref[...] = dbp

        @pl.when(i > 0)
        def _():
            dg_ref[...] += dgp
            db_ref[...] += dbp

    row = pl.BlockSpec((bm, d), lambda i: (i, 0))
    vec = pl.BlockSpec((1, d), lambda i: (0, 0))
    return _call(
        body, name=name,
        out_shape=(jax.ShapeDtypeStruct((s, d), F32), jax.ShapeDtypeStruct((s, d), BF16),
                   jax.ShapeDtypeStruct((1, d), F32), jax.ShapeDtypeStruct((1, d), F32)),
        grid=(s // bm,), in_specs=[row, row, pl.BlockSpec((bm, 1), lambda i: (i, 0)), vec],
        out_specs=(row, row, vec, vec), args=(dx, xhat, rstd, gamma), vmem_mib=48)


def _to_bf16(x, *, bm, name, comm=None):
    s, d = x.shape
    bm = min(bm, s)
    assert s % bm == 0

    def body(x_ref, o_ref):
        o_ref[...] = x_ref[...].astype(BF16)

    row = pl.BlockSpec((bm, d), lambda i: (i, 0))
    return _call(body, name=name, out_shape=jax.ShapeDtypeStruct((s, d), BF16), grid=(s // bm,),
                 in_specs=[row], out_specs=row, args=(x,), vmem_mib=48, comm=comm)


def _ple_loss(x3, x3b, p, wpg, wpp, target, *, bm, bn, name):
    s, d = x3.shape
    dp = p.shape[1]
    bm, bn = min(bm, s), min(bn, d)
    assert s % bm == 0 and d % bn == 0
    inv_d = 1.0 / d
    chunks = 4 if bm % 64 == 0 else 1
    cr = bm // chunks

    def body(x_ref, xb_ref, p_ref, wg_ref, wp_ref, t_ref, l_ref, dy_ref, dg_ref, dp_ref):
        first = (pl.program_id(0) == 0) & (pl.program_id(1) == 0)

        @pl.when(first)
        def _():
            l_ref[...] = jnp.zeros_like(l_ref)

        part = 0.0
        for r in range(chunks):
            rows = slice(r * cr, (r + 1) * cr)
            gp = jnp.dot(xb_ref[rows, :], wg_ref[...], preferred_element_type=F32)
            pp = lax.dot_general(p_ref[rows, :].astype(BF16), wp_ref[...], NT_DIMS, preferred_element_type=F32)
            sig = jax.nn.sigmoid(gp)
            err = x_ref[rows, :] + sig * pp - t_ref[rows, :]
            part = part + jnp.sum(err * err)
            dy = err * inv_d
            dy_ref[rows, :] = dy
            dg_ref[rows, :] = (dy * pp * sig * (1.0 - sig)).astype(BF16)
            dp_ref[rows, :] = (dy * sig).astype(BF16)
        l_ref[...] += part

    blk = pl.BlockSpec((bm, bn), lambda i, j: (i, j))
    return pl.pallas_call(
        body, name=name,
        out_shape=(jax.ShapeDtypeStruct((8, LANES), F32), jax.ShapeDtypeStruct((s, d), F32),
                   jax.ShapeDtypeStruct((s, d), BF16), jax.ShapeDtypeStruct((s, d), BF16)),
        grid=(s // bm, d // bn),
        in_specs=[blk, pl.BlockSpec((bm, d), lambda i, j: (i, 0)), pl.BlockSpec((bm, dp), lambda i, j: (i, 0)),
                  pl.BlockSpec((d, bn), lambda i, j: (0, j)), pl.BlockSpec((bn, dp), lambda i, j: (j, 0)), blk],
        out_specs=(pl.BlockSpec((8, LANES), lambda i, j: (0, 0)), blk, blk, blk),
        compiler_params=_cp(("arbitrary", "arbitrary"), 56),
    )(x3, x3b, p, wpg, wpp, target)


def _rope_tables(positions):
    half = ROT_DIMS // 2
    lane = jnp.arange(HEAD_DIM)
    inv_freq = jnp.power(jnp.float32(ROPE_THETA), -(lane % half).astype(F32) * (2.0 / ROT_DIMS))
    ang = positions.astype(F32)[:, None] * inv_freq
    cos, sin = jnp.cos(ang), jnp.sin(ang)
    cf = jnp.where(lane < ROT_DIMS, cos, 1.0)
    sa = jnp.where(lane < half, -sin, 0.0)
    sb = jnp.where((lane >= half) & (lane < ROT_DIMS), sin, 0.0)
    return cf, sa, sb


def _rotary(t, tabs, *, n_cols, inverse, out_dtype, bs, name):
    s = t.shape[0]
    bs = min(bs, s)
    half = ROT_DIMS // 2
    heads = N_KV_HEADS
    assert n_cols % heads == 0

    def body(t_ref, cf_ref, sa_ref, sb_ref, o_ref):
        cf, sa, sb = cf_ref[...], sa_ref[...], sb_ref[...]
        for hd in range(heads):
            lanes = slice(hd * HEAD_DIM, (hd + 1) * HEAD_DIM)
            v = t_ref[:, lanes]
            if inverse:
                o = v * cf + pltpu.roll(v * sa, half, 1) + pltpu.roll(v * sb, HEAD_DIM - half, 1)
            else:
                o = v * cf + pltpu.roll(v, HEAD_DIM - half, 1) * sa + pltpu.roll(v, half, 1) * sb
            o_ref[:, lanes] = o.astype(out_dtype)

    blk = pl.BlockSpec((bs, heads * HEAD_DIM), lambda i, j: (i, j))
    tab = pl.BlockSpec((bs, HEAD_DIM), lambda i, j: (i, 0))
    return pl.pallas_call(
        body, name=name, out_shape=jax.ShapeDtypeStruct((s, n_cols * HEAD_DIM), out_dtype),
        grid=(s // bs, n_cols // heads), in_specs=[blk, tab, tab, tab], out_specs=blk,
        compiler_params=_cp(("parallel", "arbitrary"), 32),
    )(t, *tabs)


def _attn_blocks():
    out = []
    for g, dil in enumerate(DILATIONS):
        sup = SPAN * dil
        for j in range(ATTN_TILE // sup):
            for r in range(dil):
                out.append((g, j * sup + r, dil, (j - 1) * sup + r if j > 0 else None, ATTN_TILE - sup + r))
    return out


def _rows(ref, start, dil, lead=None):
    idx = pl.ds(start, SPAN, stride=dil) if dil > 1 else pl.ds(start, SPAN)
    return ref[idx, :] if lead is None else ref[lead, idx, :]


def _band_masks(n):
    qi = lax.broadcasted_iota(jnp.int32, (SPAN, 2 * SPAN), 0)
    ki = lax.broadcasted_iota(jnp.int32, (SPAN, 2 * SPAN), 1)
    band = (ki >= qi) & (ki <= qi + SPAN)
    return band, band & ((ki >= SPAN) | (n > 0))


def _attn_fwd(qkr, proj, *, name):
    s = qkr.shape[0]
    t = ATTN_TILE
    assert s % t == 0
    nt = s // t
    scale = HEAD_DIM ** -0.5
    kcol, vcol = N_PATTERNS * N_KV_HEADS, (N_PATTERNS + 1) * N_KV_HEADS
    blocks = _attn_blocks()

    def body(q0, q1, q2, kc_ref, kp_ref, vc_ref, vp_ref, o_ref, l_ref, og, lg):
        n = pl.program_id(1)
        band, band_first = _band_masks(n)
        q_refs = (q0, q1, q2)
        for g, start, dil, prev_in_tile, prev_start in blocks:
            q = _rows(q_refs[g], start, dil).astype(BF16)
            if prev_in_tile is not None:
                kp, vp, mask = _rows(kc_ref, prev_in_tile, dil), _rows(vc_ref, prev_in_tile, dil), band
            else:
                kp, vp, mask = _rows(kp_ref, prev_start, dil), _rows(vp_ref, prev_start, dil), band_first
            kk = jnp.concatenate([kp, _rows(kc_ref, start, dil)], axis=0).astype(BF16)
            vv = jnp.concatenate([vp, _rows(vc_ref, start, dil)], axis=0).astype(BF16)
            sc = lax.dot_general(q, kk, (((1,), (1,)), ((), ())), preferred_element_type=F32) * scale
            sc = jnp.where(mask, sc, -1e30)
            m = jnp.max(sc, axis=-1, keepdims=True)
            e = jnp.exp(sc - m)
            den = jnp.sum(e, axis=-1, keepdims=True)
            o = jnp.dot(e.astype(BF16), vv, preferred_element_type=F32) / den
            idx = pl.ds(start, SPAN, stride=dil) if dil > 1 else pl.ds(start, SPAN)
            og[g, idx, :] = o
            lg[g, idx, :] = jnp.broadcast_to(m + jnp.log(den), (SPAN, HEAD_DIM))
        l0, l1, l2 = lg[0], lg[1], lg[2]
        m = jnp.maximum(jnp.maximum(l0, l1), l2)
        w0, w1, w2 = jnp.exp(l0 - m), jnp.exp(l1 - m), jnp.exp(l2 - m)
        den = w0 + w1 + w2
        o_ref[...] = (w0 * og[0] + w1 * og[1] + w2 * og[2]) / den
        l_ref[...] = m + jnp.log(den)

    def col(c, prev=False):
        if prev:
            return pl.BlockSpec((t, HEAD_DIM), lambda h, n: (jnp.maximum(n - 1, 0), c + h))
        return pl.BlockSpec((t, HEAD_DIM), lambda h, n: (n, c + h))

    out = jax.ShapeDtypeStruct((s, N_KV_HEADS * HEAD_DIM), F32)
    return pl.pallas_call(
        body, name=name, out_shape=(out, out),
        grid=(N_KV_HEADS, nt),
        in_specs=[col(0), col(N_KV_HEADS), col(2 * N_KV_HEADS), col(kcol), col(kcol, True), col(vcol), col(vcol, True)],
        out_specs=(col(0), col(0)),
        scratch_shapes=[pltpu.VMEM((N_PATTERNS, t, HEAD_DIM), F32), pltpu.VMEM((N_PATTERNS, t, HEAD_DIM), F32)],
        compiler_params=_cp(("parallel", "arbitrary"), 48),
    )(qkr, qkr, qkr, qkr, qkr, proj, proj)


def _attn_bwd(qkr, proj, attn, lse, dcat, *, name, comm=None):
    s = qkr.shape[0]
    t = ATTN_TILE
    nt = s // t
    scale = HEAD_DIM ** -0.5
    kcol, vcol = N_PATTERNS * N_KV_HEADS, (N_PATTERNS + 1) * N_KV_HEADS
    blocks = _attn_blocks()

    def body(q0, q1, q2, kc_ref, kp_ref, vc_ref, vp_ref, o_ref, l_ref, do_ref,
             dq0, dq1, dq2, dk_ref, dv_ref, ck, cv, tkc, tvc, tkp, tvp):
        n = pl.program_id(1)
        for ref in (tkc, tvc, tkp, tvp):
            ref[...] = jnp.zeros_like(ref)

        @pl.when(n < nt)
        def _():
            band, band_first = _band_masks(n)
            q_refs, dq_refs = (q0, q1, q2), (dq0, dq1, dq2)
            for g, start, dil, prev_in_tile, prev_start in blocks:
                idx = pl.ds(start, SPAN, stride=dil) if dil > 1 else pl.ds(start, SPAN)
                q = q_refs[g][idx, :].astype(BF16)
                if prev_in_tile is not None:
                    kp, vp, mask = _rows(kc_ref, prev_in_tile, dil), _rows(vc_ref, prev_in_tile, dil), band
                else:
                    kp, vp, mask = _rows(kp_ref, prev_start, dil), _rows(vp_ref, prev_start, dil), band_first
                kk = jnp.concatenate([kp, kc_ref[idx, :]], axis=0).astype(BF16)
                vv = jnp.concatenate([vp, vc_ref[idx, :]], axis=0).astype(BF16)
                do = do_ref[idx, :]
                dsum = jnp.sum(do * o_ref[idx, :], axis=-1, keepdims=True)
                lrow = l_ref[idx, :][:, :1]
                dob = do.astype(BF16)
                sc = lax.dot_general(q, kk, (((1,), (1,)), ((), ())), preferred_element_type=F32) * scale
                p = jnp.where(mask, jnp.exp(sc - lrow), 0.0)
                dp = lax.dot_general(dob, vv, (((1,), (1,)), ((), ())), preferred_element_type=F32)
                ds = (p * (dp - dsum) * scale).astype(BF16)
                pb = p.astype(BF16)
                dq_refs[g][idx, :] = jnp.dot(ds, kk, preferred_element_type=F32)
                dkk = lax.dot_general(ds, q, (((0,), (0,)), ((), ())), preferred_element_type=F32)
                dvv = lax.dot_general(pb, dob, (((0,), (0,)), ((), ())), preferred_element_type=F32)
                tkc[idx, :] += dkk[SPAN:]
                tvc[idx, :] += dvv[SPAN:]
                if prev_in_tile is not None:
                    pidx = pl.ds(prev_in_tile, SPAN, stride=dil) if dil > 1 else pl.ds(prev_in_tile, SPAN)
                    tkc[pidx, :] += dkk[:SPAN]
                    tvc[pidx, :] += dvv[:SPAN]
                else:
                    pidx = pl.ds(prev_start, SPAN, stride=dil) if dil > 1 else pl.ds(prev_start, SPAN)
                    tkp[pidx, :] += dkk[:SPAN]
                    tvp[pidx, :] += dvv[:SPAN]

        @pl.when(n > 0)
        def _():
            dk_ref[...] = ck[...] + tkp[...]
            dv_ref[...] = (cv[...] + tvp[...]).astype(BF16)

        ck[...] = tkc[...]
        cv[...] = tvc[...]

    def col(c, prev=False):
        if prev:
            return pl.BlockSpec((t, HEAD_DIM), lambda h, n: (jnp.maximum(jnp.minimum(n, nt - 1) - 1, 0), c + h))
        return pl.BlockSpec((t, HEAD_DIM), lambda h, n: (jnp.minimum(n, nt - 1), c + h))

    kv_out = pl.BlockSpec((t, HEAD_DIM), lambda h, n: (jnp.maximum(n - 1, 0), h))
    tile = pltpu.VMEM((t, HEAD_DIM), F32)
    per_head = jax.ShapeDtypeStruct((s, N_KV_HEADS * HEAD_DIM), F32)
    return _call(
        body, name=name,
        out_shape=(per_head, per_head, per_head, per_head, jax.ShapeDtypeStruct((s, N_KV_HEADS * HEAD_DIM), BF16)),
        grid=(N_KV_HEADS, nt + 1),
        in_specs=[col(0), col(N_KV_HEADS), col(2 * N_KV_HEADS), col(kcol), col(kcol, True), col(vcol), col(vcol, True),
                  col(0), col(0), col(0)],
        out_specs=(col(0), col(0), col(0), kv_out, kv_out),
        scratch_shapes=[tile] * 6,
        args=(qkr, qkr, qkr, qkr, qkr, proj, proj, attn, lse, dcat), vmem_mib=48, comm=comm)


GELU_C0 = 0.7978845608028654
GELU_C1 = 0.044715


def _softplus_neg(lam):
    y = jnp.exp(-jnp.abs(lam))
    w = 1.0 + y
    log1p = jnp.where(w == 1.0, y, jnp.log(w) * (y / jnp.where(w == 1.0, 1.0, w - 1.0)))
    return jnp.maximum(-lam, 0.0) + log1p


def _down(cur, prev, k, row):
    if k == 0:
        return cur
    return jnp.where(row < k, pltpu.roll(prev, k, 0), pltpu.roll(cur, k, 0))


def _up(cur, nxt, k, row, tt):
    if k == 0:
        return cur
    return jnp.where(row >= tt - k, pltpu.roll(nxt, tt - k, 0), pltpu.roll(cur, tt - k, 0))


def _lru_gates(x, xp, cw, cb, wr, br, wi, bi, lam, row):
    shifts = [_down(x, xp, k, row) for k in range(CONV_WIDTH)]
    xc = cb
    for j in range(CONV_WIDTH):
        xc = xc + cw[j:j + 1, :] * shifts[CONV_WIDTH - 1 - j]
    xcb = xc.astype(BF16)
    r = jax.nn.sigmoid(jnp.dot(xcb, wr, preferred_element_type=F32) + br)
    i = jax.nn.sigmoid(jnp.dot(xcb, wi, preferred_element_type=F32) + bi)
    c = -LRU_C * _softplus_neg(lam)
    la = c * r
    a = jnp.exp(la)
    mult = jnp.sqrt(jnp.tanh(-la) * (a * a + 1.0))
    return shifts, xc, xcb, r, i, c, a, mult


def _lru_fwd(proj, cw, cb, wr, br, wi, bi, lam, *, tt, name):
    s = proj.shape[0]
    nblk = wr.shape[0]
    c = nblk * LANES
    tt = min(tt, s)
    xcol0 = (N_PATTERNS + 2) * N_KV_HEADS
    ycol0 = xcol0 + nblk

    def body(x_ref, y_ref, cw_ref, cb_ref, wr_ref, br_ref, wi_ref, bi_ref, lam_ref, rec_ref, h_ref, xprev, hc):
        n = pl.program_id(1)

        @pl.when(n == 0)
        def _():
            xprev[...] = jnp.zeros_like(xprev)
            hc[...] = jnp.zeros_like(hc)

        row = lax.broadcasted_iota(jnp.int32, (tt, LANES), 0)
        x = x_ref[...]
        _, xc, _, _, i, _, a, mult = _lru_gates(
            x, xprev[...], cw_ref[...], cb_ref[...], wr_ref[0].astype(BF16), br_ref[...],
            wi_ref[0].astype(BF16), bi_ref[...], lam_ref[...], row)
        av, bv = a, mult * (i * xc)
        k = 1
        while k < tt:
            bs = jnp.where(row < k, 0.0, pltpu.roll(bv, k, 0))
            as_ = jnp.where(row < k, 1.0, pltpu.roll(av, k, 0))
            bv = bv + av * bs
            av = av * as_
            k *= 2
        h = bv + av * hc[0:1, :]
        hc[...] = jnp.broadcast_to(h[tt - 1:tt, :], hc.shape)
        h_ref[...] = h
        y = y_ref[...]
        gel = 0.5 * y * (1.0 + jnp.tanh(GELU_C0 * (y + GELU_C1 * y * y * y)))
        rec_ref[...] = (h * gel).astype(BF16)
        xprev[...] = x

    vec = pl.BlockSpec((1, LANES), lambda b, n: (0, b))
    wblk = pl.BlockSpec((1, LANES, LANES), lambda b, n: (b, 0, 0))
    out = pl.BlockSpec((tt, LANES), lambda b, n: (n, b))
    return pl.pallas_call(
        body, name=name,
        out_shape=(jax.ShapeDtypeStruct((s, c), BF16), jax.ShapeDtypeStruct((s, c), F32)),
        grid=(nblk, s // tt),
        in_specs=[pl.BlockSpec((tt, LANES), lambda b, n: (n, xcol0 + b)),
                  pl.BlockSpec((tt, LANES), lambda b, n: (n, ycol0 + b)),
                  pl.BlockSpec((CONV_WIDTH, LANES), lambda b, n: (0, b)), vec, wblk, vec, wblk, vec, vec],
        out_specs=(out, out),
        scratch_shapes=[pltpu.VMEM((tt, LANES), F32), pltpu.VMEM((8, LANES), F32)],
        compiler_params=_cp(("parallel", "arbitrary"), 32),
    )(proj, proj, cw, cb, wr, br, wi, bi, lam)


def _lru_bwd(proj, hseq, dcat, cw, cb, wr, br, wi, bi, lam, *, tt, name, comm=None):
    s = proj.shape[0]
    nblk = wr.shape[0]
    c = nblk * LANES
    tt = min(tt, s)
    nt = s // tt
    xcol0 = (N_PATTERNS + 2) * N_KV_HEADS
    ycol0 = xcol0 + nblk
    rcol0 = N_KV_HEADS

    def body(x_ref, xp_ref, y_ref, h_ref, hp_ref, dr_ref, cw_ref, cb_ref, wr_ref, br_ref, wi_ref, bi_ref, lam_ref,
             dx_ref, dy_ref, dcw_ref, dcb_ref, dwr_ref, dbr_ref, dwi_ref, dbi_ref, dlam_ref, dxc_next, gcar, acar):
        n = pl.program_id(1)
        rt = nt - 1 - n

        @pl.when(n == 0)
        def _():
            for ref in (dxc_next, gcar, acar, dcw_ref, dcb_ref, dwr_ref, dbr_ref, dwi_ref, dbi_ref, dlam_ref):
                ref[...] = jnp.zeros_like(ref)

        row = lax.broadcasted_iota(jnp.int32, (tt, LANES), 0)
        x = x_ref[...]
        xp = jnp.where(rt > 0, xp_ref[...], 0.0)
        cwv = cw_ref[...]
        wrb, wib = wr_ref[0].astype(BF16), wi_ref[0].astype(BF16)
        lam_v = lam_ref[...]
        shifts, xc, xcb, r, i, cc, a, mult = _lru_gates(x, xp, cwv, cb_ref[...], wrb, br_ref[...], wib, bi_ref[...],
                                                        lam_v, row)
        h = h_ref[...]
        hp_last = jnp.where(rt > 0, hp_ref[7:8, :], 0.0)
        hprev = jnp.where(row < 1, hp_last, pltpu.roll(h, 1, 0))
        y = y_ref[...]
        y2 = y * y
        th = jnp.tanh(GELU_C0 * (y + GELU_C1 * y2 * y))
        gel = 0.5 * y * (1.0 + th)
        dgel = 0.5 * (1.0 + th) + 0.5 * y * (1.0 - th * th) * GELU_C0 * (1.0 + 3.0 * GELU_C1 * y2)
        drec = dr_ref[...]
        dy_ref[...] = (drec * h * dgel).astype(BF16)
        av = jnp.where(row >= tt - 1, acar[0:1, :], pltpu.roll(a, tt - 1, 0))
        bv = drec * gel
        k = 1
        while k < tt:
            bs = jnp.where(row >= tt - k, 0.0, pltpu.roll(bv, tt - k, 0))
            as_ = jnp.where(row >= tt - k, 1.0, pltpu.roll(av, tt - k, 0))
            bv = bv + av * bs
            av = av * as_
            k *= 2
        g = bv + av * gcar[0:1, :]
        gcar[...] = jnp.broadcast_to(g[0:1, :], gcar.shape)
        acar[...] = jnp.broadcast_to(a[0:1, :], acar.shape)
        da = g * hprev
        d_ixc = g * mult
        dmult = g * (i * xc)
        di = d_ixc * xc
        dxc = d_ixc * i
        a2 = a * a
        dla = da * a - dmult * (a2 / mult)
        dr = dla * cc
        dsp = jnp.sum(dla * r, axis=0, keepdims=True) * (-LRU_C)
        dlam_ref[...] += dsp * (-jax.nn.sigmoid(-lam_v))
        dzr = dr * r * (1.0 - r)
        dzi = di * i * (1.0 - i)
        dbr_ref[...] += jnp.sum(dzr, axis=0, keepdims=True)
        dbi_ref[...] += jnp.sum(dzi, axis=0, keepdims=True)
        dzrb, dzib = dzr.astype(BF16), dzi.astype(BF16)
        tn = (((0,), (0,)), ((), ()))
        ntd = (((1,), (1,)), ((), ()))
        dwr_ref[0] += lax.dot_general(xcb, dzrb, tn, preferred_element_type=F32)
        dwi_ref[0] += lax.dot_general(xcb, dzib, tn, preferred_element_type=F32)
        dxc = (dxc + lax.dot_general(dzrb, wrb, ntd, preferred_element_type=F32)
               + lax.dot_general(dzib, wib, ntd, preferred_element_type=F32))
        dcb_ref[...] += jnp.sum(dxc, axis=0, keepdims=True)
        dcw_ref[...] += jnp.concatenate(
            [jnp.sum(dxc * shifts[CONV_WIDTH - 1 - j], axis=0, keepdims=True) for j in range(CONV_WIDTH)], axis=0)
        nxt = dxc_next[...]
        dx = cwv[0:1, :] * _up(dxc, nxt, CONV_WIDTH - 1, row, tt)
        for j in range(1, CONV_WIDTH):
            dx = dx + cwv[j:j + 1, :] * _up(dxc, nxt, CONV_WIDTH - 1 - j, row, tt)
        dx_ref[...] = dx.astype(BF16)
        dxc_next[...] = dxc

    def tile(col0, prev=False):
        if prev:
            return pl.BlockSpec((tt, LANES), lambda b, n: (jnp.maximum(nt - 2 - n, 0), col0 + b))
        return pl.BlockSpec((tt, LANES), lambda b, n: (nt - 1 - n, col0 + b))

    vec = pl.BlockSpec((1, LANES), lambda b, n: (0, b))
    wblk = pl.BlockSpec((1, LANES, LANES), lambda b, n: (b, 0, 0))
    cwblk = pl.BlockSpec((CONV_WIDTH, LANES), lambda b, n: (0, b))
    hp8 = pl.BlockSpec((8, LANES), lambda b, n: (jnp.maximum((nt - 1 - n) * (tt // 8) - 1, 0), b))
    vshape = jax.ShapeDtypeStruct((1, c), F32)
    wshape = jax.ShapeDtypeStruct((nblk, LANES, LANES), F32)
    return _call(
        body, name=name,
        out_shape=(jax.ShapeDtypeStruct((s, c), BF16), jax.ShapeDtypeStruct((s, c), BF16),
                   jax.ShapeDtypeStruct((CONV_WIDTH, c), F32), vshape, wshape, vshape, wshape, vshape, vshape),
        grid=(nblk, nt),
        in_specs=[tile(xcol0), tile(xcol0, True), tile(ycol0), tile(0), hp8, tile(rcol0),
                  cwblk, vec, wblk, vec, wblk, vec, vec],
        out_specs=(tile(0), tile(0), cwblk, vec, wblk, vec, wblk, vec, vec),
        scratch_shapes=[pltpu.VMEM((tt, LANES), F32), pltpu.VMEM((8, LANES), F32), pltpu.VMEM((8, LANES), F32)],
        args=(proj, proj, proj, hseq, hseq, dcat, cw, cb, wr, br, wi, bi, lam), vmem_mib=32, comm=comm)


ROW_BLOCKS = (512, 256, 176, 128, 64, 32, 16, 8)


def _adamw(w, m, v, gparts, *, name):
    r, c = w.shape
    npart = gparts.shape[0]
    br = _pick(r, ROW_BLOCKS)
    c1 = 1.0 - ADAM_B1 ** ADAM_STEP
    c2 = 1.0 - ADAM_B2 ** ADAM_STEP

    def body(w_ref, m_ref, v_ref, g_ref, go_ref, d_ref, mo_ref, vo_ref):
        g = g_ref[0].astype(F32)
        for q in range(1, npart):
            g = g + g_ref[q].astype(F32)
        mn = ADAM_B1 * m_ref[...] + (1.0 - ADAM_B1) * g
        vn = ADAM_B2 * v_ref[...] + (1.0 - ADAM_B2) * (g * g)
        go_ref[...] = g
        mo_ref[...] = mn
        vo_ref[...] = vn
        d_ref[...] = -ADAM_LR * ((mn / c1) / (jnp.sqrt(vn / c2) + ADAM_EPS) + ADAM_WD * w_ref[...])

    blk = pl.BlockSpec((br, c), lambda i: (i, 0))
    out = jax.ShapeDtypeStruct((r, c), F32)
    return pl.pallas_call(
        body, name=name, out_shape=(out, out, out, out), grid=(r // br,),
        in_specs=[blk, blk, blk, pl.BlockSpec((npart, br, c), lambda i: (0, i, 0))],
        out_specs=(blk, blk, blk, blk),
        compiler_params=_cp(("parallel",), 48),
    )(w, m, v, gparts)


def _sum_parts(parts, *, name):
    npart, r, c = parts.shape
    br = next((b for b in range(min(r, 2048) // 8 * 8, 0, -8) if r % b == 0), r)

    def body(p_ref, o_ref):
        acc = p_ref[0]
        for q in range(1, npart):
            acc = acc + p_ref[q]
        o_ref[...] = acc

    return pl.pallas_call(
        body, name=name, out_shape=jax.ShapeDtypeStruct((r, c), F32), grid=(r // br,),
        in_specs=[pl.BlockSpec((npart, br, c), lambda i: (0, i, 0))],
        out_specs=pl.BlockSpec((br, c), lambda i: (i, 0)),
        compiler_params=_cp(("parallel",), 48),
    )(parts)


HBM = pl.BlockSpec(memory_space=pltpu.HBM)


def _mesh_pos():
    return lax.axis_index("x"), lax.axis_index("y"), lax.axis_index("c")


def _gather_comm(shards):
    na = len(shards)

    def parts(x_refs, out_refs, sems):
        send_sems, recv_sems, local_sems = sems
        x, y, c = _mesh_pos()
        me, sibling = (x, y, c), (x, y, 1 - c)
        chips = [(1 - x, y), (x, 1 - y), (1 - x, 1 - y)]

        def copy(a, k, block, to, src=None):
            px, py, pc = block
            dst = out_refs[a].at[4 * px + 2 * py + pc]
            return pltpu.make_async_remote_copy(
                src_ref=dst if src is None else src, dst_ref=dst,
                send_sem=send_sems.at[a, k], recv_sem=recv_sems.at[a, k],
                device_id=to, device_id_type=MESH)

        def mine(a):
            return pltpu.make_async_copy(x_refs[a], out_refs[a].at[4 * x + 2 * y + c], local_sems.at[a])

        def first(a):
            return [copy(a, 0, me, sibling, src=x_refs[a])] + [
                copy(a, 1 + j, me, (*chip, c), src=x_refs[a]) for j, chip in enumerate(chips)]

        def passed(a, j):
            return copy(a, 4 + j, (*chips[j], c), sibling)

        return me, sibling, chips, c, copy, mine, first, passed

    def start(x_refs, out_refs, sems):
        *_, mine, first, _ = parts(x_refs, out_refs, sems)
        for a in range(na):
            mine(a).start()
            for cp in first(a):
                cp.start()

    def mid(x_refs, out_refs, sems):
        me, _, chips, c, copy, _, _, passed = parts(x_refs, out_refs, sems)
        for j, chip in enumerate(chips):
            for a in range(na):
                copy(a, 1 + j, (*chip, c), me).wait_recv()
                passed(a, j).start()

    def end(x_refs, out_refs, sems):
        me, sibling, chips, c, copy, mine, first, passed = parts(x_refs, out_refs, sems)
        for a in range(na):
            copy(a, 0, sibling, me).wait_recv()
            for j, chip in enumerate(chips):
                copy(a, 4 + j, (*chip, 1 - c), me).wait_recv()
        for a in range(na):
            for cp in first(a) + [passed(a, j) for j in range(3)]:
                cp.wait_send()
            mine(a).wait()

    return _Comm(
        shards, [jax.ShapeDtypeStruct((N_DEV,) + a.shape, a.dtype) for a in shards],
        [pltpu.SemaphoreType.DMA((na, 7)), pltpu.SemaphoreType.DMA((na, 7)), pltpu.SemaphoreType.DMA((na,))],
        start, end, mid)


def _scatter_comm(g8s):
    na = len(g8s)

    def parts(g_refs, buf_refs, sems):
        send_sems, recv_sems, local_sems = sems
        x, y, c = _mesh_pos()
        me_idx = 4 * x + 2 * y + c

        def copy(a, k, slot):
            peer, peer_idx = _scatter_peer(k, x, y, c)
            return pltpu.make_async_remote_copy(
                src_ref=g_refs[a].at[peer_idx], dst_ref=buf_refs[a].at[me_idx if slot is None else slot],
                send_sem=send_sems.at[a, k - 1], recv_sem=recv_sems.at[a, k - 1],
                device_id=peer, device_id_type=MESH)

        def mine(a):
            return pltpu.make_async_copy(g_refs[a].at[me_idx], buf_refs[a].at[me_idx], local_sems.at[a])

        return x, y, c, copy, mine

    def start(g_refs, buf_refs, sems):
        *_, copy, mine = parts(g_refs, buf_refs, sems)
        for a in range(na):
            mine(a).start()
            for k in range(1, N_DEV):
                copy(a, k, None).start()

    def end(g_refs, buf_refs, sems):
        x, y, c, copy, mine = parts(g_refs, buf_refs, sems)
        for a in range(na):
            for k in range(1, N_DEV):
                copy(a, k, _scatter_peer(k, x, y, c)[1]).wait_recv()
        for a in range(na):
            for k in range(1, N_DEV):
                copy(a, k, None).wait_send()
            mine(a).wait()

    return _Comm(
        g8s, [jax.ShapeDtypeStruct(g.shape, g.dtype) for g in g8s],
        [pltpu.SemaphoreType.DMA((na, N_DEV - 1)), pltpu.SemaphoreType.DMA((na, N_DEV - 1)),
         pltpu.SemaphoreType.DMA((na,))],
        start, end)


def _scatter_peer(k, x, y, c):
    px, py, pc = (1 - x if k & 4 else x, 1 - y if k & 2 else y, 1 - c if k & 1 else c)
    return (px, py, pc), 4 * px + 2 * py + pc


def _scatter_start_comm(g8s):
    na = len(g8s)
    arrays = []
    for g8 in g8s:
        arrays += [g8, lax.empty(g8.shape, g8.dtype)]

    def local(refs, sems, a, me_idx):
        return pltpu.make_async_copy(refs[2 * a].at[me_idx], refs[2 * a + 1].at[me_idx], sems[0].at[a])

    def start(refs, outs, sems):
        send_sems, recv_sems = outs[:2]
        x, y, c = _mesh_pos()
        me_idx = 4 * x + 2 * y + c
        for a in range(na):
            local(refs, sems, a, me_idx).start()
            for k in range(1, N_DEV):
                peer, peer_idx = _scatter_peer(k, x, y, c)
                pltpu.make_async_remote_copy(
                    src_ref=refs[2 * a].at[peer_idx], dst_ref=refs[2 * a + 1].at[me_idx],
                    send_sem=send_sems.at[a * (N_DEV - 1) + k - 1], recv_sem=recv_sems.at[a * (N_DEV - 1) + k - 1],
                    device_id=peer, device_id_type=MESH).start()

    def end(refs, outs, sems):
        x, y, c = _mesh_pos()
        for a in range(na):
            local(refs, sems, a, 4 * x + 2 * y + c).wait()

    sem_shape = pltpu.SemaphoreType.DMA((na * (N_DEV - 1),))
    return _Comm(arrays, [sem_shape, sem_shape] + [pltpu.HBM(a.shape, a.dtype) for a in arrays],
                 [pltpu.SemaphoreType.DMA((na,))], start, end, split=True)


def _scatter_wait(started, after, *, name):
    send_sems, recv_sems, *arrays = started
    na = len(arrays) // 2

    def body(*refs):
        send_ref, recv_ref = refs[2 * na], refs[2 * na + 1]
        x, y, c = _mesh_pos()
        for a in range(na):
            for k in range(1, N_DEV):
                peer, peer_idx = _scatter_peer(k, x, y, c)
                pltpu.make_async_remote_copy(
                    src_ref=refs[2 * a].at[peer_idx], dst_ref=refs[2 * a + 1].at[peer_idx],
                    send_sem=send_ref.at[a * (N_DEV - 1) + k - 1], recv_sem=recv_ref.at[a * (N_DEV - 1) + k - 1],
                    device_id=peer, device_id_type=MESH).wait()

    sem = pl.BlockSpec(memory_space=pltpu.SEMAPHORE)
    outs = pl.pallas_call(
        body, name=name, out_shape=tuple(pltpu.HBM(a.shape, a.dtype) for a in arrays),
        in_specs=[HBM] * (2 * na) + [sem, sem, pl.BlockSpec(memory_space=pl.ANY)], out_specs=(HBM,) * (2 * na),
        input_output_aliases={k: k for k in range(2 * na)},
        compiler_params=pltpu.CompilerParams(has_side_effects=pltpu.SideEffectType.DATAFLOW_SIDE_EFFECTING),
    )(*arrays, send_sems, recv_sems, after)
    return outs[1::2]


BIG_WEIGHTS = ("ffn1_w_gate", "ffn1_w_up", "ffn1_w_down", "w_in", "w_out",
               "ffn2_w_gate", "ffn2_w_up", "ffn2_w_down", "w_ple_proj", "w_ple_gate")
COLUMN_SHARDED = ("ffn1_w_gate", "ffn1_w_up", "w_in", "ffn2_w_gate", "ffn2_w_up", "w_ple_proj", "conv_w")
SMALL_WEIGHTS = ("ln1_g", "ln1_b", "conv_b", "w_rgate", "b_rgate", "w_igate", "b_igate", "lru_lambda",
                 "ln2_g", "ln2_b", "ln3_g", "ln3_b")
SMALL_GRADS = SMALL_WEIGHTS + ("conv_w",)


class _Exchange:
    def __init__(self, full):
        self.full = dict(full)
        self.grads = {}

    def __getitem__(self, name):
        return self.full[name]

    def first_gather(self, x):
        return _to_bf16(x, bm=1024, name="x_bf16")

    def gather(self, names):
        return None, None

    def scatter(self, names):
        return None, None

    def scatter_start(self, names):
        return None, None

    def gather_small(self):
        return None, None


class _MeshExchange(_Exchange):
    def __init__(self, full, shards, conv_w):
        super().__init__(full)
        self.shards = shards
        self.conv_w = conv_w
        self.reduced = {}
        self.started = {}
        self.small_parts = None

    def first_gather(self, x):
        first = ("ffn1_w_gate", "ffn1_w_up")
        xb, (gate, up, conv_all) = _to_bf16(
            x, bm=1024, name="x_bf16", comm=_gather_comm([self.shards[n] for n in first] + [self.conv_w]))
        self.take(first[0], gate)
        self.take(first[1], up)
        self.full["conv_w"] = _to_full("conv_w", conv_all)
        return xb

    def gather(self, names):
        def done(outs):
            for n, o in zip(names, outs):
                self.take(n, o)
        return _gather_comm([self.shards[n] for n in names]), done

    def take(self, name, gathered):
        self.full[name] = gathered.reshape((N_DEV * gathered.shape[1],) + gathered.shape[2:])

    def scatter(self, names):
        def done(outs):
            self.reduced.update(zip(names, outs))
        return _scatter_comm([_to_owner_blocks(n, self.grads[n]) for n in names]), done

    def scatter_start(self, names):
        def done(outs):
            self.started[names] = outs
        return _scatter_start_comm([_to_owner_blocks(n, self.grads[n]) for n in names]), done

    def finish(self, after):
        for names, started in self.started.items():
            self.reduced.update(zip(names, _scatter_wait(started, after, name=f"scatter_wait_{names[0]}")))

    def gather_small(self):
        def done(outs):
            self.small_parts, = outs
        packed = jnp.concatenate([_rows128(self.grads[n]) for n in SMALL_GRADS], axis=0)
        return _gather_comm([packed]), done


def _carried(comm_done, call):
    comm, done = comm_done
    res = call(comm)
    if comm is None:
        return res
    res, outs = res
    done(outs)
    return res


def _dw(a, b, *, scale=1.0, name, comm=None):
    k, m = a.shape
    n = b.shape[1]
    return _mm(a, b, ta=True, scale=scale, out_dtype=BF16, bm=_pick(m, (1024, 512, 256, 128)),
               bn=_pick(n, (512, 256, 128)), bk=k, name=name, comm=comm)


def _ffn_bwd(ex, names, saved, xb_in, dz, dzb, ln_in, tag, on_dwd=None, on_dh=None, on_dwu=None, on_dx=None):
    gate, up, down = names
    g, u, h, _, _ = saved
    f = ex[gate].shape[0]

    def request(fn):
        return (None, None) if fn is None else fn(ex)

    ex.grads[down] = _carried(request(on_dwd), lambda c: _dw(h, dzb, scale=0.5, name=f"{tag}_dwd", comm=c))
    dg, du = _carried(request(on_dh), lambda c: _ffn_bwd_dh(
        dzb, ex[down], g, u, scale=0.5, bm=2048, bn=_pick(f, (512, 256, 128)), name=f"{tag}_dh", chunks=8, comm=c))
    ex.grads[gate] = _dw(xb_in, dg, name=f"{tag}_dwg")
    ex.grads[up] = _carried(request(on_dwu), lambda c: _dw(xb_in, du, name=f"{tag}_dwu", comm=c))
    d = dz.shape[1]
    dx = _carried(request(on_dx), lambda c: _ffn_dx(
        dg, du, ex[gate], ex[up], dz, extra_scale=DEEPNORM_ALPHA,
        bm=1024, bn=_pick(d, (512, 256, 128)), name=f"{tag}_dx", comm=c))
    return dx if ln_in is None else _ln_bwd(dx, *ln_in, bm=256, name=f"{tag}_ln_bwd")


def _local_step(x, p, target, positions, w):
    s, d = x.shape
    tabs = _rope_tables(positions)
    xb = w.first_gather(x)
    f = w["ffn1_w_gate"].shape[0]
    ffn_bn, ln_bn, ln_bn_short_k = _pick(f, (512, 256, 128)), _pick(d, (512, 256, 128)), _pick(d, (1024, 512, 256, 128))
    g1, u1, h1 = _carried(w.gather(("ffn1_w_down", "w_in")), lambda c: _ffn_up(
        xb, w["ffn1_w_gate"], w["ffn1_w_up"], bm=1024, bn=ffn_bn, name="ffn1_up", comm=c))
    x1, x1b, xh1, rs1 = _carried(w.gather(("w_out", "ffn2_w_gate")), lambda c: _mm_ln(
        h1, w["ffn1_w_down"], x, w["ln1_g"], w["ln1_b"], res_scale=DEEPNORM_ALPHA, mm_scale=0.5,
        bm=512, bn=ln_bn, name="ffn1_down_ln", comm=c))
    sv1 = (g1, u1, h1, xh1, rs1)
    pw = w["w_in"].shape[0]
    proj = _carried(w.gather(("ffn2_w_up",)), lambda c: _mm(
        x1b, w["w_in"], tb=True, bm=1024, bn=_pick(pw, (512, 256, 128)), bk=d, name="in_proj", comm=c))
    nqk = (N_PATTERNS + 1) * N_KV_HEADS
    qkr = _rotary(proj, tabs, n_cols=nqk, inverse=False, out_dtype=F32, bs=1024, name="rotary")
    attn, lse = _attn_fwd(qkr, proj, name="attn_fwd")
    lru_w = (w["conv_w"], w["conv_b"], w["w_rgate"], w["b_rgate"], w["w_igate"], w["b_igate"], w["lru_lambda"])
    rec, hseq = _lru_fwd(proj, *lru_w, tt=512, name="lru_fwd")
    cat = jnp.concatenate([attn.astype(BF16), rec], axis=1)
    x2, x2b, xh2, rs2 = _mm_ln(cat, w["w_out"], x1, w["ln2_g"], w["ln2_b"], res_scale=DEEPNORM_ALPHA, mm_scale=1.0,
                               bm=512, bn=ln_bn_short_k, name="out_proj_ln")
    g2, u2, h2 = _carried(w.gather(("ffn2_w_down", "w_ple_gate", "w_ple_proj")), lambda c: _ffn_up(
        x2b, w["ffn2_w_gate"], w["ffn2_w_up"], bm=1024, bn=ffn_bn, name="ffn2_up", comm=c))
    x3, x3b, xh3, rs3 = _mm_ln(h2, w["ffn2_w_down"], x2, w["ln3_g"], w["ln3_b"], res_scale=DEEPNORM_ALPHA, mm_scale=0.5,
                               bm=512, bn=ln_bn, name="ffn2_down_ln")
    sv3 = (g2, u2, h2, xh3, rs3)
    lsum, dy, dgate, dple = _ple_loss(x3, x3b, p, w["w_ple_gate"], w["w_ple_proj"], target,
                                      bm=1024, bn=_pick(d, (512, 256, 128)), name="ple_loss")
    grads = w.grads
    grads["w_ple_gate"] = _dw(x3b, dgate, name="dw_ple_gate")
    grads["w_ple_proj"] = _dw(p, dple, name="dw_ple_proj")
    dz3, dz3b, grads["ln3_g"], grads["ln3_b"] = _carried(w.scatter(("w_ple_gate", "w_ple_proj")), lambda c: _mm_dx(
        dgate, w["w_ple_gate"], dy, xh3, rs3, w["ln3_g"], extra_scale=1.0, bm=512, bn=ln_bn_short_k,
        name="ple_dx", tb=True, comm=c))
    dz2, dz2b, grads["ln2_g"], grads["ln2_b"] = _ffn_bwd(
        w, ("ffn2_w_gate", "ffn2_w_up", "ffn2_w_down"), sv3, x2b, dz3, dz3b, (xh2, rs2, w["ln2_g"]), "ffn2",
        on_dx=lambda ex: ex.scatter_start(("ffn2_w_down", "ffn2_w_gate", "ffn2_w_up")))
    grads["w_out"] = _dw(cat, dz2b, name="dw_out")
    dcat = _mm(dz2b, w["w_out"], tb=True, bm=1024, bn=_pick(d, (512, 256, 128)), bk=d, name="out_proj_dx")
    dq0, dq1, dq2, dk, dvb = _attn_bwd(qkr, proj, attn, lse, dcat, name="attn_bwd")
    nh = N_KV_HEADS
    dqkv = [_rotary(t, tabs, n_cols=nh, inverse=True, out_dtype=BF16, bs=1024, name=f"rotary_bwd{i}")
            for i, t in enumerate((dq0, dq1, dq2, dk))]
    (dxb, dyb, grads["conv_w"], grads["conv_b"], grads["w_rgate"], grads["b_rgate"], grads["w_igate"],
     grads["b_igate"], grads["lru_lambda"]) = _lru_bwd(proj, hseq, dcat, *lru_w, tt=512, name="lru_bwd")
    dproj = jnp.concatenate(dqkv + [dvb, dxb, dyb], axis=1)
    grads["w_in"] = _dw(x1b, dproj, name="dw_in")
    dz1, dz1b, grads["ln1_g"], grads["ln1_b"] = _carried(w.scatter_start(("w_out", "w_in")), lambda c: _mm_dx(
        dproj, w["w_in"], dz2, xh1, rs1, w["ln1_g"], extra_scale=DEEPNORM_ALPHA,
        bm=512, bn=ln_bn, name="in_proj_dx", comm=c))
    grad_x = _ffn_bwd(w, ("ffn1_w_gate", "ffn1_w_up", "ffn1_w_down"), sv1, xb, dz1, dz1b, None, "ffn1",
                      on_dwd=lambda ex: ex.gather_small(),
                      on_dh=lambda ex: ex.scatter_start(("ffn1_w_down",)),
                      on_dwu=lambda ex: ex.scatter_start(("ffn1_w_gate",)),
                      on_dx=lambda ex: ex.scatter_start(("ffn1_w_up",)))
    return lsum, grad_x


def _to_full(name, gathered):
    if name in COLUMN_SHARDED:
        _, r, c = gathered.shape
        return jnp.transpose(gathered, (1, 0, 2)).reshape(r, N_DEV * c)
    return gathered.reshape((N_DEV * gathered.shape[1],) + gathered.shape[2:])


def _to_owner_blocks(name, full):
    if name in COLUMN_SHARDED:
        r, c = full.shape
        return jnp.transpose(full.reshape(r, N_DEV, c // N_DEV), (1, 0, 2))
    return full.reshape((N_DEV, full.shape[0] // N_DEV) + full.shape[1:])


def _rows128(a):
    flat = a.reshape(-1, LANES)
    pad = (-flat.shape[0]) % 8
    return jnp.pad(flat, ((0, pad), (0, 0))) if pad else flat


def kernel(x, p, positions, ffn1_w_gate, ffn1_w_up, ffn1_w_down, ln1_g, ln1_b, w_in, conv_w, conv_b, w_rgate, b_rgate, w_igate, b_igate, lru_lambda, w_out, ln2_g, ln2_b, ffn2_w_gate, ffn2_w_up, ffn2_w_down, ln3_g, ln3_b, w_ple_proj, w_ple_gate, loss_target, m_ffn1_w_gate, m_ffn1_w_up, m_ffn1_w_down, m_ln1_g, m_ln1_b, m_w_in, m_conv_w, m_conv_b, m_w_rgate, m_b_rgate, m_w_igate, m_b_igate, m_lru_lambda, m_w_out, m_ln2_g, m_ln2_b, m_ffn2_w_gate, m_ffn2_w_up, m_ffn2_w_down, m_ln3_g, m_ln3_b, m_w_ple_proj, m_w_ple_gate, v_ffn1_w_gate, v_ffn1_w_up, v_ffn1_w_down, v_ln1_g, v_ln1_b, v_w_in, v_conv_w, v_conv_b, v_w_rgate, v_b_rgate, v_w_igate, v_b_igate, v_lru_lambda, v_w_out, v_ln2_g, v_ln2_b, v_ffn2_w_gate, v_ffn2_w_up, v_ffn2_w_down, v_ln3_g, v_ln3_b, v_w_ple_proj, v_w_ple_gate):
    names = ("ffn1_w_gate", "ffn1_w_up", "ffn1_w_down", "ln1_g", "ln1_b", "w_in", "conv_w", "conv_b", "w_rgate",
             "b_rgate", "w_igate", "b_igate", "lru_lambda", "w_out", "ln2_g", "ln2_b", "ffn2_w_gate", "ffn2_w_up",
             "ffn2_w_down", "ln3_g", "ln3_b", "w_ple_proj", "w_ple_gate")
    ws = (ffn1_w_gate, ffn1_w_up, ffn1_w_down, ln1_g, ln1_b, w_in, conv_w, conv_b, w_rgate, b_rgate, w_igate, b_igate,
          lru_lambda, w_out, ln2_g, ln2_b, ffn2_w_gate, ffn2_w_up, ffn2_w_down, ln3_g, ln3_b, w_ple_proj, w_ple_gate)
    ms = (m_ffn1_w_gate, m_ffn1_w_up, m_ffn1_w_down, m_ln1_g, m_ln1_b, m_w_in, m_conv_w, m_conv_b, m_w_rgate, m_b_rgate,
          m_w_igate, m_b_igate, m_lru_lambda, m_w_out, m_ln2_g, m_ln2_b, m_ffn2_w_gate, m_ffn2_w_up, m_ffn2_w_down,
          m_ln3_g, m_ln3_b, m_w_ple_proj, m_w_ple_gate)
    vs = (v_ffn1_w_gate, v_ffn1_w_up, v_ffn1_w_down, v_ln1_g, v_ln1_b, v_w_in, v_conv_w, v_conv_b, v_w_rgate, v_b_rgate,
          v_w_igate, v_b_igate, v_lru_lambda, v_w_out, v_ln2_g, v_ln2_b, v_ffn2_w_gate, v_ffn2_w_up, v_ffn2_w_down,
          v_ln3_g, v_ln3_b, v_w_ple_proj, v_w_ple_gate)
    def local(a):
        return a[0] if a.ndim >= 3 else a

    w_loc = {n: local(a) for n, a in zip(names, ws)}
    m_loc = {n: local(a) for n, a in zip(names, ms)}
    v_loc = {n: local(a) for n, a in zip(names, vs)}
    out_shapes = {n: a.shape for n, a in zip(names, ws)}

    shards = {n: (w_loc[n].T if n in COLUMN_SHARDED else w_loc[n]).astype(BF16) for n in BIG_WEIGHTS}
    ex = _MeshExchange({n: w_loc[n] for n in SMALL_WEIGHTS}, shards, w_loc["conv_w"])

    lsum, grad_x = _local_step(x[0], p[0, 0], loss_target[0], positions[0], ex)
    ex.finish(grad_x)
    grads, reduced = ex.grads, ex.reduced
    d_model = x.shape[-1]
    loss = lax.psum(lsum[0, 0] * (0.5 / d_model), ("x", "y", "c"))

    small = SMALL_GRADS
    summed = _sum_parts(ex.small_parts, name="sum_small_grads")
    small_grads, row = {}, 0
    for n in small:
        rows = grads[n].size // LANES
        small_grads[n] = summed[row:row + rows].reshape(grads[n].shape)
        row += rows + (-rows) % 8
    me = 4 * lax.axis_index("x") + 2 * lax.axis_index("y") + lax.axis_index("c")
    cw_cols = w_loc["conv_w"].shape[1]
    small_grads["conv_w"] = lax.dynamic_slice_in_dim(small_grads["conv_w"], me * cw_cols, cw_cols, axis=1)

    out_g, out_d, out_m, out_v = {}, {}, {}, {}
    for n in names:
        wl, ml, vl = w_loc[n], m_loc[n], v_loc[n]
        shape = wl.shape
        if n in BIG_WEIGHTS:
            gparts = reduced[n]
        else:
            gparts = small_grads[n].reshape((1,) + shape)
        if wl.ndim == 3:
            wl, ml, vl = (t.reshape(-1, shape[-1]) for t in (wl, ml, vl))
            gparts = gparts.reshape(gparts.shape[0], -1, shape[-1])
        res = _adamw(wl, ml, vl, gparts, name=f"adamw_{n}")
        out_g[n], out_d[n], out_m[n], out_v[n] = (t.reshape(out_shapes[n]) for t in res)

    return (loss, grad_x[None], *[out_g[n] for n in names], *[out_d[n] for n in names],
            *[out_m[n] for n in names], *[out_v[n] for n in names])
```

```python
import jax
import jax.numpy as jnp
from jax import lax
from jax.experimental import pallas as pl
from jax.experimental.pallas import tpu as pltpu

F32 = jnp.float32
BF16 = jnp.bfloat16

N_DEV = 8
LANES = 128
MIB = 1 << 20

HEAD_DIM = 128
N_KV_HEADS = 4
DILATIONS = (1, 4, 16)
N_PATTERNS = 3
SPAN = 128
ROT_DIMS = 32
ROPE_THETA = 500000.0
LRU_C = 8.0
CONV_WIDTH = 4
LN_EPS = 1e-5
DEEPNORM_ALPHA = 2.0 ** 0.25
ATTN_TILE = SPAN * DILATIONS[-1]

ADAM_LR = 0.001
ADAM_B1 = 0.9
ADAM_B2 = 0.999
ADAM_EPS = 1e-08
ADAM_WD = 0.01
ADAM_STEP = 10

MESH = pl.DeviceIdType.MESH
NT_DIMS = (((1,), (1,)), ((), ()))
EPILOGUE_ROWS = 64


def _cp(semantics, vmem_mib):
    return pltpu.CompilerParams(dimension_semantics=semantics, vmem_limit_bytes=vmem_mib * MIB)


def _pick(n, candidates):
    for c in candidates:
        if n % c == 0:
            return c
    return n


class _Comm:
    def __init__(self, arrays, out_shapes, scratch, start, end, mid=None, split=False):
        self.arrays, self.out_shapes, self.scratch = list(arrays), list(out_shapes), list(scratch)
        self.start, self.mid, self.end, self.split = start, mid, end, split


def _call(body, *, name, grid, in_specs, out_specs, out_shape, args, scratch_shapes=(), vmem_mib, comm=None):
    single = not isinstance(out_shape, (tuple, list))
    out_shape_t = (out_shape,) if single else tuple(out_shape)
    out_specs_t = (out_specs,) if single else tuple(out_specs)
    params = _cp(("arbitrary",) * len(grid), vmem_mib)
    if comm is None:
        res = pl.pallas_call(body, name=name, grid=grid, in_specs=list(in_specs), out_specs=out_specs_t,
                             out_shape=out_shape_t, scratch_shapes=list(scratch_shapes), compiler_params=params)(*args)
        return res[0] if single else res
    n_in, n_out, n_scr = len(args), len(out_shape_t), len(scratch_shapes)
    nci, nco = len(comm.arrays), len(comm.out_shapes)
    total = 1
    for g in grid:
        total *= g

    def wrapped(*refs):
        ins, refs = refs[:n_in], refs[n_in:]
        cin, refs = refs[:nci], refs[nci:]
        outs, refs = refs[:n_out], refs[n_out:]
        cout, refs = refs[:nco], refs[nco:]
        scr, csem = refs[:n_scr], refs[n_scr:]
        step = pl.program_id(0)
        for ax in range(1, len(grid)):
            step = step * grid[ax] + pl.program_id(ax)

        @pl.when(step == 0)
        def _():
            comm.start(cin, cout, csem)

        body(*ins, *outs, *scr)
        if comm.mid is not None:
            @pl.when(step == (3 * total) // 4)
            def _():
                comm.mid(cin, cout, csem)

        @pl.when(step == total - 1)
        def _():
            comm.end(cin, cout, csem)

    hbm = pl.BlockSpec(memory_space=pltpu.HBM)
    if comm.split:
        sem = pl.BlockSpec(memory_space=pltpu.SEMAPHORE)
        n_sems = nco - nci
        res = pl.pallas_call(
            wrapped, name=name, grid=grid,
            in_specs=list(in_specs) + [hbm] * nci,
            out_specs=out_specs_t + (sem,) * n_sems + (hbm,) * nci,
            out_shape=out_shape_t + tuple(comm.out_shapes),
            scratch_shapes=list(scratch_shapes) + comm.scratch,
            input_output_aliases={n_in + k: n_out + n_sems + k for k in range(nci)},
            compiler_params=pltpu.CompilerParams(
                dimension_semantics=("arbitrary",) * len(grid), vmem_limit_bytes=vmem_mib * MIB,
                has_side_effects=pltpu.SideEffectType.DATAFLOW_SIDE_EFFECTING),
        )(*args, *[pltpu.with_memory_space_constraint(a, pltpu.HBM) for a in comm.arrays])
    else:
        res = pl.pallas_call(
            wrapped, name=name, grid=grid,
            in_specs=list(in_specs) + [hbm] * nci,
            out_specs=out_specs_t + (hbm,) * nco,
            out_shape=out_shape_t + tuple(comm.out_shapes),
            scratch_shapes=list(scratch_shapes) + comm.scratch,
            compiler_params=params)(*args, *comm.arrays)
    own, extra = res[:n_out], res[n_out:]
    return (own[0] if single else own), extra


def _mm(a, b, *, ta=False, tb=False, out_dtype=F32, scale=1.0, bm, bn, bk, name, comm=None):
    m, k = (a.shape[1], a.shape[0]) if ta else a.shape
    n = b.shape[0] if tb else b.shape[1]
    bm, bn, bk = min(bm, m), min(bn, n), min(bk, k)
    assert m % bm == 0 and n % bn == 0 and k % bk == 0, (name, m, n, k, bm, bn, bk)
    nk = k // bk
    a_spec = pl.BlockSpec((bk, bm), lambda i, j, kk: (kk, i)) if ta else pl.BlockSpec((bm, bk), lambda i, j, kk: (i, kk))
    b_spec = pl.BlockSpec((bn, bk), lambda i, j, kk: (j, kk)) if tb else pl.BlockSpec((bk, bn), lambda i, j, kk: (kk, j))
    dn = (((0 if ta else 1,), (1 if tb else 0,)), ((), ()))

    def body(a_ref, b_ref, o_ref, *acc):
        part = lax.dot_general(a_ref[...].astype(BF16), b_ref[...].astype(BF16), dn, preferred_element_type=F32)
        if nk == 1:
            o_ref[...] = (part * scale).astype(out_dtype)
            return
        acc_ref, = acc
        kk = pl.program_id(2)

        @pl.when(kk == 0)
        def _():
            acc_ref[...] = part

        @pl.when(kk > 0)
        def _():
            acc_ref[...] += part

        @pl.when(kk == nk - 1)
        def _():
            o_ref[...] = (acc_ref[...] * scale).astype(out_dtype)

    return _call(
        body, name=name,
        out_shape=jax.ShapeDtypeStruct((m, n), out_dtype),
        grid=(m // bm, n // bn, nk),
        in_specs=[a_spec, b_spec],
        out_specs=pl.BlockSpec((bm, bn), lambda i, j, kk: (i, j)),
        scratch_shapes=[pltpu.VMEM((bm, bn), F32)] if nk > 1 else [],
        args=(a, b), vmem_mib=56, comm=comm)


def _ffn_up(xb, wg, wu, *, bm, bn, name, comm=None):
    s, d = xb.shape
    f = wg.shape[0]
    bm, bn = min(bm, s), min(bn, f)
    assert s % bm == 0 and f % bn == 0

    def body(x_ref, wg_ref, wu_ref, hg_ref, hu_ref, h_ref):
        x = x_ref[...]
        g = lax.dot_general(x, wg_ref[...], NT_DIMS, preferred_element_type=F32)
        u = lax.dot_general(x, wu_ref[...], NT_DIMS, preferred_element_type=F32)
        sig = jax.nn.sigmoid(g)
        silu = g * sig
        hg_ref[...] = (u * (sig * (1.0 + g * (1.0 - sig)))).astype(BF16)
        hu_ref[...] = silu.astype(BF16)
        h_ref[...] = (silu * u).astype(BF16)

    out = jax.ShapeDtypeStruct((s, f), BF16)
    blk = pl.BlockSpec((bm, bn), lambda i, j: (i, j))
    return _call(
        body, name=name, out_shape=(out, out, out),
        grid=(s // bm, f // bn),
        in_specs=[pl.BlockSpec((bm, d), lambda i, j: (i, 0)),
                  pl.BlockSpec((bn, d), lambda i, j: (j, 0)),
                  pl.BlockSpec((bn, d), lambda i, j: (j, 0))],
        out_specs=(blk, blk, blk),
        args=(xb, wg, wu), vmem_mib=56, comm=comm)


def _ffn_bwd_dh(dzb, wd, g, u, *, scale, bm, bn, name, chunks=2, comm=None):
    s, d = dzb.shape
    f = wd.shape[0]
    bm, bn = min(bm, s), min(bn, f)
    assert s % bm == 0 and f % bn == 0

    cr = bm // chunks

    def body(dz_ref, wd_ref, hg_ref, hu_ref, dg_ref, du_ref):
        for r in range(chunks):
            rows = slice(r * cr, (r + 1) * cr)
            dh = lax.dot_general(dz_ref[rows, :], wd_ref[...], NT_DIMS, preferred_element_type=F32) * scale
            dg_ref[rows, :] = (dh * hg_ref[rows, :].astype(F32)).astype(BF16)
            du_ref[rows, :] = (dh * hu_ref[rows, :].astype(F32)).astype(BF16)

    out = jax.ShapeDtypeStruct((s, f), BF16)
    blk = pl.BlockSpec((bm, bn), lambda i, j: (i, j))
    return _call(
        body, name=name, out_shape=(out, out),
        grid=(s // bm, f // bn),
        in_specs=[pl.BlockSpec((bm, d), lambda i, j: (i, 0)),
                  pl.BlockSpec((bn, d), lambda i, j: (j, 0)), blk, blk],
        out_specs=(blk, blk),
        args=(dzb, wd, g, u), vmem_mib=56, comm=comm)


def _full_rows(acc_ref, rows, nj):
    return jnp.concatenate([acc_ref[jj, rows, :] for jj in range(nj)], axis=1)


def _mm_ln(a, b, res, gamma, beta, *, res_scale, mm_scale, bm, bn, name, comm=None):
    s, k = a.shape
    d = b.shape[1]
    bm, bn = min(bm, s), min(bn, d)
    assert s % bm == 0 and d % bn == 0
    nj = d // bn
    ch = min(EPILOGUE_ROWS, bm)

    def body(a_ref, b_ref, r_ref, g_ref, be_ref, y_ref, yb_ref, xh_ref, rs_ref, acc_ref):
        j = pl.program_id(1)
        acc_ref[j] = jnp.dot(a_ref[...], b_ref[...], preferred_element_type=F32)

        @pl.when(j == nj - 1)
        def _():
            def chunk(ci, carry):
                rows = pl.ds(pl.multiple_of(ci * ch, ch), ch)
                z = res_scale * r_ref[rows, :] + mm_scale * _full_rows(acc_ref, rows, nj)
                mu = jnp.mean(z, axis=-1, keepdims=True)
                zc = z - mu
                var = jnp.mean(zc * zc, axis=-1, keepdims=True)
                rstd = lax.rsqrt(var + LN_EPS)
                xh = zc * rstd
                y = xh * g_ref[...] + be_ref[...]
                y_ref[rows, :] = y
                yb_ref[rows, :] = y.astype(BF16)
                xh_ref[rows, :] = xh
                rs_ref[rows, :] = rstd
                return carry

            lax.fori_loop(0, bm // ch, chunk, 0)

    row = pl.BlockSpec((bm, d), lambda i, j: (i, 0))
    vec = pl.BlockSpec((1, d), lambda i, j: (0, 0))
    return _call(
        body, name=name,
        out_shape=(jax.ShapeDtypeStruct((s, d), F32), jax.ShapeDtypeStruct((s, d), BF16),
                   jax.ShapeDtypeStruct((s, d), F32), jax.ShapeDtypeStruct((s, 1), F32)),
        grid=(s // bm, nj),
        in_specs=[pl.BlockSpec((bm, k), lambda i, j: (i, 0)),
                  pl.BlockSpec((k, bn), lambda i, j: (0, j)), row, vec, vec],
        out_specs=(row, row, row, pl.BlockSpec((bm, 1), lambda i, j: (i, 0))),
        scratch_shapes=[pltpu.VMEM((nj, bm, bn), F32)],
        args=(a, b, res, gamma, beta), vmem_mib=58, comm=comm)


def _mm_dx(a, wt, extra, xhat, rstd, gamma, *, extra_scale, bm, bn, name, tb=False, comm=None):
    s, k = a.shape
    d = wt.shape[0] if tb else wt.shape[1]
    bm, bn = min(bm, s), min(bn, d)
    assert s % bm == 0 and d % bn == 0
    nj = d // bn
    ch = min(EPILOGUE_ROWS, bm)
    dims = NT_DIMS if tb else (((1,), (0,)), ((), ()))

    def body(a_ref, w_ref, e_ref, xh_ref, rs_ref, g_ref, dz_ref, dzb_ref, dg_ref, db_ref, acc_ref):
        i = pl.program_id(0)
        j = pl.program_id(1)
        acc_ref[j] = lax.dot_general(a_ref[...], w_ref[...], dims, preferred_element_type=F32)

        @pl.when(j == nj - 1)
        def _():
            def chunk(ci, carry):
                dgp, dbp = carry
                rows = pl.ds(pl.multiple_of(ci * ch, ch), ch)
                dx = extra_scale * e_ref[rows, :] + _full_rows(acc_ref, rows, nj)
                xh = xh_ref[rows, :]
                dxh = dx * g_ref[...]
                m1 = jnp.mean(dxh, axis=-1, keepdims=True)
                m2 = jnp.mean(dxh * xh, axis=-1, keepdims=True)
                dz = rs_ref[rows, :] * (dxh - m1 - xh * m2)
                dz_ref[rows, :] = dz
                dzb_ref[rows, :] = dz.astype(BF16)
                return dgp + jnp.sum(dx * xh, axis=0, keepdims=True), dbp + jnp.sum(dx, axis=0, keepdims=True)

            zero = jnp.zeros((1, d), F32)
            dgp, dbp = lax.fori_loop(0, bm // ch, chunk, (zero, zero))

            @pl.when(i == 0)
            def _():
                dg_ref[...] = dgp
                db_ref[...] = dbp

            @pl.when(i > 0)
            def _():
                dg_ref[...] += dgp
                db_ref[...] += dbp

    row = pl.BlockSpec((bm, d), lambda i, j: (i, 0))
    vec = pl.BlockSpec((1, d), lambda i, j: (0, 0))
    return _call(
        body, name=name,
        out_shape=(jax.ShapeDtypeStruct((s, d), F32), jax.ShapeDtypeStruct((s, d), BF16),
                   jax.ShapeDtypeStruct((1, d), F32), jax.ShapeDtypeStruct((1, d), F32)),
        grid=(s // bm, nj),
        in_specs=[pl.BlockSpec((bm, k), lambda i, j: (i, 0)),
                  pl.BlockSpec((bn, k), lambda i, j: (j, 0)) if tb else pl.BlockSpec((k, bn), lambda i, j: (0, j)),
                  row, row, pl.BlockSpec((bm, 1), lambda i, j: (i, 0)), vec],
        out_specs=(row, row, vec, vec),
        scratch_shapes=[pltpu.VMEM((nj, bm, bn), F32)],
        args=(a, wt, extra, xhat, rstd, gamma), vmem_mib=58, comm=comm)


def _ffn_dx(dg, du, wgt, wut, extra, *, extra_scale, bm, bn, name, comm=None):
    s, f = dg.shape
    d = wgt.shape[1]
    bm, bn = min(bm, s), min(bn, d)
    assert s % bm == 0 and d % bn == 0 and f % (2 * LANES) == 0
    nj, half = d // bn, f // 2

    def body(dg_ref, du_ref, wg_ref, wu_ref, e_ref, o_ref, acc_ref):
        kh, j = pl.program_id(1), pl.program_id(2)
        part = jnp.dot(dg_ref[...], wg_ref[...], preferred_element_type=F32)
        part = part + jnp.dot(du_ref[...], wu_ref[...], preferred_element_type=F32)

        @pl.when(kh == 0)
        def _():
            acc_ref[j] = part

        @pl.when(kh == 1)
        def _():
            o_ref[...] = extra_scale * e_ref[...] + (acc_ref[j] + part)

    rows = pl.BlockSpec((bm, half), lambda i, kh, j: (i, kh))
    cols = pl.BlockSpec((half, bn), lambda i, kh, j: (kh, j))
    blk = pl.BlockSpec((bm, bn), lambda i, kh, j: (i, j * kh))
    return _call(
        body, name=name, out_shape=jax.ShapeDtypeStruct((s, d), F32),
        grid=(s // bm, 2, nj), in_specs=[rows, rows, cols, cols, blk], out_specs=blk,
        scratch_shapes=[pltpu.VMEM((nj, bm, bn), F32)],
        args=(dg, du, wgt, wut, extra), vmem_mib=58, comm=comm)


def _ln_bwd(dx, xhat, rstd, gamma, *, bm, name):
    s, d = dx.shape
    bm = min(bm, s)
    assert s % bm == 0
    ch = min(EPILOGUE_ROWS, bm)

    def body(dx_ref, xh_ref, rs_ref, g_ref, dz_ref, dzb_ref, dg_ref, db_ref):
        def chunk(ci, carry):
            dgp, dbp = carry
            rows = pl.ds(pl.multiple_of(ci * ch, ch), ch)
            dxv = dx_ref[rows, :]
            xh = xh_ref[rows, :]
            dxh = dxv * g_ref[...]
            m1 = jnp.mean(dxh, axis=-1, keepdims=True)
            m2 = jnp.mean(dxh * xh, axis=-1, keepdims=True)
            dz = rs_ref[rows, :] * (dxh - m1 - xh * m2)
            dz_ref[rows, :] = dz
            dzb_ref[rows, :] = dz.astype(BF16)
            return dgp + jnp.sum(dxv * xh, axis=0, keepdims=True), dbp + jnp.sum(dxv, axis=0, keepdims=True)

        zero = jnp.zeros((1, d), F32)
        dgp, dbp = lax.fori_loop(0, bm // ch, chunk, (zero, zero))
        i = pl.program_id(0)

        @pl.when(i == 0)
        def _():
            dg_ref[...] = dgp
            db_ref[...] = dbp

        @pl.when(i > 0)
        def _():
            dg_ref[...] += dgp
            db_ref[...] += dbp

    row = pl.BlockSpec((bm, d), lambda i: (i, 0))
    vec = pl.BlockSpec((1, d), lambda i: (0, 0))
    return _call(
        body, name=name,
        out_shape=(jax.ShapeDtypeStruct((s, d), F32), jax.ShapeDtypeStruct((s, d), BF16),
                   jax.ShapeDtypeStruct((1, d), F32), jax.ShapeDtypeStruct((1, d), F32)),
        grid=(s // bm,), in_specs=[row, row, pl.BlockSpec((bm, 1), lambda i: (i, 0)), vec],
        out_specs=(row, row, vec, vec), args=(dx, xhat, rstd, gamma), vmem_mib=48)


def _to_bf16(x, *, bm, name, comm=None):
    s, d = x.shape
    bm = min(bm, s)
    assert s % bm == 0

    def body(x_ref, o_ref):
        o_ref[...] = x_ref[...].astype(BF16)

    row = pl.BlockSpec((bm, d), lambda i: (i, 0))
    return _call(body, name=name, out_shape=jax.ShapeDtypeStruct((s, d), BF16), grid=(s // bm,),
                 in_specs=[row], out_specs=row, args=(x,), vmem_mib=48, comm=comm)


def _ple_loss(x3, x3b, p, wpg, wpp, target, *, bm, bn, name):
    s, d = x3.shape
    dp = p.shape[1]
    bm, bn = min(bm, s), min(bn, d)
    assert s % bm == 0 and d % bn == 0
    inv_d = 1.0 / d
    chunks = 4 if bm % 64 == 0 else 1
    cr = bm // chunks

    def body(x_ref, xb_ref, p_ref, wg_ref, wp_ref, t_ref, l_ref, dy_ref, dg_ref, dp_ref):
        first = (pl.program_id(0) == 0) & (pl.program_id(1) == 0)

        @pl.when(first)
        def _():
            l_ref[...] = jnp.zeros_like(l_ref)

        part = 0.0
        for r in range(chunks):
            rows = slice(r * cr, (r + 1) * cr)
            gp = jnp.dot(xb_ref[rows, :], wg_ref[...], preferred_element_type=F32)
            pp = lax.dot_general(p_ref[rows, :].astype(BF16), wp_ref[...], NT_DIMS, preferred_element_type=F32)
            sig = jax.nn.sigmoid(gp)
            err = x_ref[rows, :] + sig * pp - t_ref[rows, :]
            part = part + jnp.sum(err * err)
            dy = err * inv_d
            dy_ref[rows, :] = dy
            dg_ref[rows, :] = (dy * pp * sig * (1.0 - sig)).astype(BF16)
            dp_ref[rows, :] = (dy * sig).astype(BF16)
        l_ref[...] += part

    blk = pl.BlockSpec((bm, bn), lambda i, j: (i, j))
    return pl.pallas_call(
        body, name=name,
        out_shape=(jax.ShapeDtypeStruct((8, LANES), F32), jax.ShapeDtypeStruct((s, d), F32),
                   jax.ShapeDtypeStruct((s, d), BF16), jax.ShapeDtypeStruct((s, d), BF16)),
        grid=(s // bm, d // bn),
        in_specs=[blk, pl.BlockSpec((bm, d), lambda i, j: (i, 0)), pl.BlockSpec((bm, dp), lambda i, j: (i, 0)),
                  pl.BlockSpec((d, bn), lambda i, j: (0, j)), pl.BlockSpec((bn, dp), lambda i, j: (j, 0)), blk],
        out_specs=(pl.BlockSpec((8, LANES), lambda i, j: (0, 0)), blk, blk, blk),
        compiler_params=_cp(("arbitrary", "arbitrary"), 56),
    )(x3, x3b, p, wpg, wpp, target)


def _rope_tables(positions):
    half = ROT_DIMS // 2
    lane = jnp.arange(HEAD_DIM)
    inv_freq = jnp.power(jnp.float32(ROPE_THETA), -(lane % half).astype(F32) * (2.0 / ROT_DIMS))
    ang = positions.astype(F32)[:, None] * inv_freq
    cos, sin = jnp.cos(ang), jnp.sin(ang)
    cf = jnp.where(lane < ROT_DIMS, cos, 1.0)
    sa = jnp.where(lane < half, -sin, 0.0)
    sb = jnp.where((lane >= half) & (lane < ROT_DIMS), sin, 0.0)
    return cf, sa, sb


def _rotary(t, tabs, *, n_cols, inverse, out_dtype, bs, name):
    s = t.shape[0]
    bs = min(bs, s)
    half = ROT_DIMS // 2
    heads = N_KV_HEADS
    assert n_cols % heads == 0

    def body(t_ref, cf_ref, sa_ref, sb_ref, o_ref):
        cf, sa, sb = cf_ref[...], sa_ref[...], sb_ref[...]
        for hd in range(heads):
            lanes = slice(hd * HEAD_DIM, (hd + 1) * HEAD_DIM)
            v = t_ref[:, lanes]
            if inverse:
                o = v * cf + pltpu.roll(v * sa, half, 1) + pltpu.roll(v * sb, HEAD_DIM - half, 1)
            else:
                o = v * cf + pltpu.roll(v, HEAD_DIM - half, 1) * sa + pltpu.roll(v, half, 1) * sb
            o_ref[:, lanes] = o.astype(out_dtype)

    blk = pl.BlockSpec((bs, heads * HEAD_DIM), lambda i, j: (i, j))
    tab = pl.BlockSpec((bs, HEAD_DIM), lambda i, j: (i, 0))
    return pl.pallas_call(
        body, name=name, out_shape=jax.ShapeDtypeStruct((s, n_cols * HEAD_DIM), out_dtype),
        grid=(s // bs, n_cols // heads), in_specs=[blk, tab, tab, tab], out_specs=blk,
        compiler_params=_cp(("parallel", "arbitrary"), 32),
    )(t, *tabs)


def _attn_blocks():
    out = []
    for g, dil in enumerate(DILATIONS):
        sup = SPAN * dil
        for j in range(ATTN_TILE // sup):
            for r in range(dil):
                out.append((g, j * sup + r, dil, (j - 1) * sup + r if j > 0 else None, ATTN_TILE - sup + r))
    return out


def _rows(ref, start, dil, lead=None):
    idx = pl.ds(start, SPAN, stride=dil) if dil > 1 else pl.ds(start, SPAN)
    return ref[idx, :] if lead is None else ref[lead, idx, :]


def _band_masks(n):
    qi = lax.broadcasted_iota(jnp.int32, (SPAN, 2 * SPAN), 0)
    ki = lax.broadcasted_iota(jnp.int32, (SPAN, 2 * SPAN), 1)
    band = (ki >= qi) & (ki <= qi + SPAN)
    return band, band & ((ki >= SPAN) | (n > 0))


def _attn_fwd(qkr, proj, *, name):
    s = qkr.shape[0]
    t = ATTN_TILE
    assert s % t == 0
    nt = s // t
    scale = HEAD_DIM ** -0.5
    kcol, vcol = N_PATTERNS * N_KV_HEADS, (N_PATTERNS + 1) * N_KV_HEADS
    blocks = _attn_blocks()

    def body(q0, q1, q2, kc_ref, kp_ref, vc_ref, vp_ref, o_ref, l_ref, og, lg):
        n = pl.program_id(1)
        band, band_first = _band_masks(n)
        q_refs = (q0, q1, q2)
        for g, start, dil, prev_in_tile, prev_start in blocks:
            q = _rows(q_refs[g], start, dil).astype(BF16)
            if prev_in_tile is not None:
                kp, vp, mask = _rows(kc_ref, prev_in_tile, dil), _rows(vc_ref, prev_in_tile, dil), band
            else:
                kp, vp, mask = _rows(kp_ref, prev_start, dil), _rows(vp_ref, prev_start, dil), band_first
            kk = jnp.concatenate([kp, _rows(kc_ref, start, dil)], axis=0).astype(BF16)
            vv = jnp.concatenate([vp, _rows(vc_ref, start, dil)], axis=0).astype(BF16)
            sc = lax.dot_general(q, kk, (((1,), (1,)), ((), ())), preferred_element_type=F32) * scale
            sc = jnp.where(mask, sc, -1e30)
            m = jnp.max(sc, axis=-1, keepdims=True)
            e = jnp.exp(sc - m)
            den = jnp.sum(e, axis=-1, keepdims=True)
            o = jnp.dot(e.astype(BF16), vv, preferred_element_type=F32) / den
            idx = pl.ds(start, SPAN, stride=dil) if dil > 1 else pl.ds(start, SPAN)
            og[g, idx, :] = o
            lg[g, idx, :] = jnp.broadcast_to(m + jnp.log(den), (SPAN, HEAD_DIM))
        l0, l1, l2 = lg[0], lg[1], lg[2]
        m = jnp.maximum(jnp.maximum(l0, l1), l2)
        w0, w1, w2 = jnp.exp(l0 - m), jnp.exp(l1 - m), jnp.exp(l2 - m)
        den = w0 + w1 + w2
        o_ref[...] = (w0 * og[0] + w1 * og[1] + w2 * og[2]) / den
        l_ref[...] = m + jnp.log(den)

    def col(c, prev=False):
        if prev:
            return pl.BlockSpec((t, HEAD_DIM), lambda h, n: (jnp.maximum(n - 1, 0), c + h))
        return pl.BlockSpec((t, HEAD_DIM), lambda h, n: (n, c + h))

    out = jax.ShapeDtypeStruct((s, N_KV_HEADS * HEAD_DIM), F32)
    return pl.pallas_call(
        body, name=name, out_shape=(out, out),
        grid=(N_KV_HEADS, nt),
        in_specs=[col(0), col(N_KV_HEADS), col(2 * N_KV_HEADS), col(kcol), col(kcol, True), col(vcol), col(vcol, True)],
        out_specs=(col(0), col(0)),
        scratch_shapes=[pltpu.VMEM((N_PATTERNS, t, HEAD_DIM), F32), pltpu.VMEM((N_PATTERNS, t, HEAD_DIM), F32)],
        compiler_params=_cp(("parallel", "arbitrary"), 48),
    )(qkr, qkr, qkr, qkr, qkr, proj, proj)


def _attn_bwd(qkr, proj, attn, lse, dcat, *, name, comm=None):
    s = qkr.shape[0]
    t = ATTN_TILE
    nt = s // t
    scale = HEAD_DIM ** -0.5
    kcol, vcol = N_PATTERNS * N_KV_HEADS, (N_PATTERNS + 1) * N_KV_HEADS
    blocks = _attn_blocks()

    def body(q0, q1, q2, kc_ref, kp_ref, vc_ref, vp_ref, o_ref, l_ref, do_ref,
             dq0, dq1, dq2, dk_ref, dv_ref, ck, cv, tkc, tvc, tkp, tvp):
        n = pl.program_id(1)
        for ref in (tkc, tvc, tkp, tvp):
            ref[...] = jnp.zeros_like(ref)

        @pl.when(n < nt)
        def _():
            band, band_first = _band_masks(n)
            q_refs, dq_refs = (q0, q1, q2), (dq0, dq1, dq2)
            for g, start, dil, prev_in_tile, prev_start in blocks:
                idx = pl.ds(start, SPAN, stride=dil) if dil > 1 else pl.ds(start, SPAN)
                q = q_refs[g][idx, :].astype(BF16)
                if prev_in_tile is not None:
                    kp, vp, mask = _rows(kc_ref, prev_in_tile, dil), _rows(vc_ref, prev_in_tile, dil), band
                else:
                    kp, vp, mask = _rows(kp_ref, prev_start, dil), _rows(vp_ref, prev_start, dil), band_first
                kk = jnp.concatenate([kp, kc_ref[idx, :]], axis=0).astype(BF16)
                vv = jnp.concatenate([vp, vc_ref[idx, :]], axis=0).astype(BF16)
                do = do_ref[idx, :]
                dsum = jnp.sum(do * o_ref[idx, :], axis=-1, keepdims=True)
                lrow = l_ref[idx, :][:, :1]
                dob = do.astype(BF16)
                sc = lax.dot_general(q, kk, (((1,), (1,)), ((), ())), preferred_element_type=F32) * scale
                p = jnp.where(mask, jnp.exp(sc - lrow), 0.0)
                dp = lax.dot_general(dob, vv, (((1,), (1,)), ((), ())), preferred_element_type=F32)
                ds = (p * (dp - dsum) * scale).astype(BF16)
                pb = p.astype(BF16)
                dq_refs[g][idx, :] = jnp.dot(ds, kk, preferred_element_type=F32)
                dkk = lax.dot_general(ds, q, (((0,), (0,)), ((), ())), preferred_element_type=F32)
                dvv = lax.dot_general(pb, dob, (((0,), (0,)), ((), ())), preferred_element_type=F32)
                tkc[idx, :] += dkk[SPAN:]
                tvc[idx, :] += dvv[SPAN:]
                if prev_in_tile is not None:
                    pidx = pl.ds(prev_in_tile, SPAN, stride=dil) if dil > 1 else pl.ds(prev_in_tile, SPAN)
                    tkc[pidx, :] += dkk[:SPAN]
                    tvc[pidx, :] += dvv[:SPAN]
                else:
                    pidx = pl.ds(prev_start, SPAN, stride=dil) if dil > 1 else pl.ds(prev_start, SPAN)
                    tkp[pidx, :] += dkk[:SPAN]
                    tvp[pidx, :] += dvv[:SPAN]

        @pl.when(n > 0)
        def _():
            dk_ref[...] = ck[...] + tkp[...]
            dv_ref[...] = (cv[...] + tvp[...]).astype(BF16)

        ck[...] = tkc[...]
        cv[...] = tvc[...]

    def col(c, prev=False):
        if prev:
            return pl.BlockSpec((t, HEAD_DIM), lambda h, n: (jnp.maximum(jnp.minimum(n, nt - 1) - 1, 0), c + h))
        return pl.BlockSpec((t, HEAD_DIM), lambda h, n: (jnp.minimum(n, nt - 1), c + h))

    kv_out = pl.BlockSpec((t, HEAD_DIM), lambda h, n: (jnp.maximum(n - 1, 0), h))
    tile = pltpu.VMEM((t, HEAD_DIM), F32)
    per_head = jax.ShapeDtypeStruct((s, N_KV_HEADS * HEAD_DIM), F32)
    return _call(
        body, name=name,
        out_shape=(per_head, per_head, per_head, per_head, jax.ShapeDtypeStruct((s, N_KV_HEADS * HEAD_DIM), BF16)),
        grid=(N_KV_HEADS, nt + 1),
        in_specs=[col(0), col(N_KV_HEADS), col(2 * N_KV_HEADS), col(kcol), col(kcol, True), col(vcol), col(vcol, True),
                  col(0), col(0), col(0)],
        out_specs=(col(0), col(0), col(0), kv_out, kv_out),
        scratch_shapes=[tile] * 6,
        args=(qkr, qkr, qkr, qkr, qkr, proj, proj, attn, lse, dcat), vmem_mib=48, comm=comm)


GELU_C0 = 0.7978845608028654
GELU_C1 = 0.044715


def _softplus_neg(lam):
    y = jnp.exp(-jnp.abs(lam))
    w = 1.0 + y
    log1p = jnp.where(w == 1.0, y, jnp.log(w) * (y / jnp.where(w == 1.0, 1.0, w - 1.0)))
    return jnp.maximum(-lam, 0.0) + log1p


def _down(cur, prev, k, row):
    if k == 0:
        return cur
    return jnp.where(row < k, pltpu.roll(prev, k, 0), pltpu.roll(cur, k, 0))


def _up(cur, nxt, k, row, tt):
    if k == 0:
        return cur
    return jnp.where(row >= tt - k, pltpu.roll(nxt, tt - k, 0), pltpu.roll(cur, tt - k, 0))


def _lru_gates(x, xp, cw, cb, wr, br, wi, bi, lam, row):
    shifts = [_down(x, xp, k, row) for k in range(CONV_WIDTH)]
    xc = cb
    for j in range(CONV_WIDTH):
        xc = xc + cw[j:j + 1, :] * shifts[CONV_WIDTH - 1 - j]
    xcb = xc.astype(BF16)
    r = jax.nn.sigmoid(jnp.dot(xcb, wr, preferred_element_type=F32) + br)
    i = jax.nn.sigmoid(jnp.dot(xcb, wi, preferred_element_type=F32) + bi)
    c = -LRU_C * _softplus_neg(lam)
    la = c * r
    a = jnp.exp(la)
    mult = jnp.sqrt(jnp.tanh(-la) * (a * a + 1.0))
    return shifts, xc, xcb, r, i, c, a, mult


def _lru_fwd(proj, cw, cb, wr, br, wi, bi, lam, *, tt, name):
    s = proj.shape[0]
    nblk = wr.shape[0]
    c = nblk * LANES
    tt = min(tt, s)
    xcol0 = (N_PATTERNS + 2) * N_KV_HEADS
    ycol0 = xcol0 + nblk

    def body(x_ref, y_ref, cw_ref, cb_ref, wr_ref, br_ref, wi_ref, bi_ref, lam_ref, rec_ref, h_ref, xprev, hc):
        n = pl.program_id(1)

        @pl.when(n == 0)
        def _():
            xprev[...] = jnp.zeros_like(xprev)
            hc[...] = jnp.zeros_like(hc)

        row = lax.broadcasted_iota(jnp.int32, (tt, LANES), 0)
        x = x_ref[...]
        _, xc, _, _, i, _, a, mult = _lru_gates(
            x, xprev[...], cw_ref[...], cb_ref[...], wr_ref[0].astype(BF16), br_ref[...],
            wi_ref[0].astype(BF16), bi_ref[...], lam_ref[...], row)
        av, bv = a, mult * (i * xc)
        k = 1
        while k < tt:
            bs = jnp.where(row < k, 0.0, pltpu.roll(bv, k, 0))
            as_ = jnp.where(row < k, 1.0, pltpu.roll(av, k, 0))
            bv = bv + av * bs
            av = av * as_
            k *= 2
        h = bv + av * hc[0:1, :]
        hc[...] = jnp.broadcast_to(h[tt - 1:tt, :], hc.shape)
        h_ref[...] = h
        y = y_ref[...]
        gel = 0.5 * y * (1.0 + jnp.tanh(GELU_C0 * (y + GELU_C1 * y * y * y)))
        rec_ref[...] = (h * gel).astype(BF16)
        xprev[...] = x

    vec = pl.BlockSpec((1, LANES), lambda b, n: (0, b))
    wblk = pl.BlockSpec((1, LANES, LANES), lambda b, n: (b, 0, 0))
    out = pl.BlockSpec((tt, LANES), lambda b, n: (n, b))
    return pl.pallas_call(
        body, name=name,
        out_shape=(jax.ShapeDtypeStruct((s, c), BF16), jax.ShapeDtypeStruct((s, c), F32)),
        grid=(nblk, s // tt),
        in_specs=[pl.BlockSpec((tt, LANES), lambda b, n: (n, xcol0 + b)),
                  pl.BlockSpec((tt, LANES), lambda b, n: (n, ycol0 + b)),
                  pl.BlockSpec((CONV_WIDTH, LANES), lambda b, n: (0, b)), vec, wblk, vec, wblk, vec, vec],
        out_specs=(out, out),
        scratch_shapes=[pltpu.VMEM((tt, LANES), F32), pltpu.VMEM((8, LANES), F32)],
        compiler_params=_cp(("parallel", "arbitrary"), 32),
    )(proj, proj, cw, cb, wr, br, wi, bi, lam)


def _lru_bwd(proj, hseq, dcat, cw, cb, wr, br, wi, bi, lam, *, tt, name, comm=None):
    s = proj.shape[0]
    nblk = wr.shape[0]
    c = nblk * LANES
    tt = min(tt, s)
    nt = s // tt
    xcol0 = (N_PATTERNS + 2) * N_KV_HEADS
    ycol0 = xcol0 + nblk
    rcol0 = N_KV_HEADS

    def body(x_ref, xp_ref, y_ref, h_ref, hp_ref, dr_ref, cw_ref, cb_ref, wr_ref, br_ref, wi_ref, bi_ref, lam_ref,
             dx_ref, dy_ref, dcw_ref, dcb_ref, dwr_ref, dbr_ref, dwi_ref, dbi_ref, dlam_ref, dxc_next, gcar, acar):
        n = pl.program_id(1)
        rt = nt - 1 - n

        @pl.when(n == 0)
        def _():
            for ref in (dxc_next, gcar, acar, dcw_ref, dcb_ref, dwr_ref, dbr_ref, dwi_ref, dbi_ref, dlam_ref):
                ref[...] = jnp.zeros_like(ref)

        row = lax.broadcasted_iota(jnp.int32, (tt, LANES), 0)
        x = x_ref[...]
        xp = jnp.where(rt > 0, xp_ref[...], 0.0)
        cwv = cw_ref[...]
        wrb, wib = wr_ref[0].astype(BF16), wi_ref[0].astype(BF16)
        lam_v = lam_ref[...]
        shifts, xc, xcb, r, i, cc, a, mult = _lru_gates(x, xp, cwv, cb_ref[...], wrb, br_ref[...], wib, bi_ref[...],
                                                        lam_v, row)
        h = h_ref[...]
        hp_last = jnp.where(rt > 0, hp_ref[7:8, :], 0.0)
        hprev = jnp.where(row < 1, hp_last, pltpu.roll(h, 1, 0))
        y = y_ref[...]
        y2 = y * y
        th = jnp.tanh(GELU_C0 * (y + GELU_C1 * y2 * y))
        gel = 0.5 * y * (1.0 + th)
        dgel = 0.5 * (1.0 + th) + 0.5 * y * (1.0 - th * th) * GELU_C0 * (1.0 + 3.0 * GELU_C1 * y2)
        drec = dr_ref[...]
        dy_ref[...] = (drec * h * dgel).astype(BF16)
        av = jnp.where(row >= tt - 1, acar[0:1, :], pltpu.roll(a, tt - 1, 0))
        bv = drec * gel
        k = 1
        while k < tt:
            bs = jnp.where(row >= tt - k, 0.0, pltpu.roll(bv, tt - k, 0))
            as_ = jnp.where(row >= tt - k, 1.0, pltpu.roll(av, tt - k, 0))
            bv = bv + av * bs
            av = av * as_
            k *= 2
        g = bv + av * gcar[0:1, :]
        gcar[...] = jnp.broadcast_to(g[0:1, :], gcar.shape)
        acar[...] = jnp.broadcast_to(a[0:1, :], acar.shape)
        da = g * hprev
        d_ixc = g * mult
        dmult = g * (i * xc)
        di = d_ixc * xc
        dxc = d_ixc * i
        a2 = a * a
        dla = da * a - dmult * (a2 / mult)
        dr = dla * cc
        dsp = jnp.sum(dla * r, axis=0, keepdims=True) * (-LRU_C)
        dlam_ref[...] += dsp * (-jax.nn.sigmoid(-lam_v))
        dzr = dr * r * (1.0 - r)
        dzi = di * i * (1.0 - i)
        dbr_ref[...] += jnp.sum(dzr, axis=0, keepdims=True)
        dbi_ref[...] += jnp.sum(dzi, axis=0, keepdims=True)
        dzrb, dzib = dzr.astype(BF16), dzi.astype(BF16)
        tn = (((0,), (0,)), ((), ()))
        ntd = (((1,), (1,)), ((), ()))
        dwr_ref[0] += lax.dot_general(xcb, dzrb, tn, preferred_element_type=F32)
        dwi_ref[0] += lax.dot_general(xcb, dzib, tn, preferred_element_type=F32)
        dxc = (dxc + lax.dot_general(dzrb, wrb, ntd, preferred_element_type=F32)
               + lax.dot_general(dzib, wib, ntd, preferred_element_type=F32))
        dcb_ref[...] += jnp.sum(dxc, axis=0, keepdims=True)
        dcw_ref[...] += jnp.concatenate(
            [jnp.sum(dxc * shifts[CONV_WIDTH - 1 - j], axis=0, keepdims=True) for j in range(CONV_WIDTH)], axis=0)
        nxt = dxc_next[...]
        dx = cwv[0:1, :] * _up(dxc, nxt, CONV_WIDTH - 1, row, tt)
        for j in range(1, CONV_WIDTH):
            dx = dx + cwv[j:j + 1, :] * _up(dxc, nxt, CONV_WIDTH - 1 - j, row, tt)
        dx_ref[...] = dx.astype(BF16)
        dxc_next[...] = dxc

    def tile(col0, prev=False):
        if prev:
            return pl.BlockSpec((tt, LANES), lambda b, n: (jnp.maximum(nt - 2 - n, 0), col0 + b))
        return pl.BlockSpec((tt, LANES), lambda b, n: (nt - 1 - n, col0 + b))

    vec = pl.BlockSpec((1, LANES), lambda b, n: (0, b))
    wblk = pl.BlockSpec((1, LANES, LANES), lambda b, n: (b, 0, 0))
    cwblk = pl.BlockSpec((CONV_WIDTH, LANES), lambda b, n: (0, b))
    hp8 = pl.BlockSpec((8, LANES), lambda b, n: (jnp.maximum((nt - 1 - n) * (tt // 8) - 1, 0), b))
    vshape = jax.ShapeDtypeStruct((1, c), F32)
    wshape = jax.ShapeDtypeStruct((nblk, LANES, LANES), F32)
    return _call(
        body, name=name,
        out_shape=(jax.ShapeDtypeStruct((s, c), BF16), jax.ShapeDtypeStruct((s, c), BF16),
                   jax.ShapeDtypeStruct((CONV_WIDTH, c), F32), vshape, wshape, vshape, wshape, vshape, vshape),
        grid=(nblk, nt),
        in_specs=[tile(xcol0), tile(xcol0, True), tile(ycol0), tile(0), hp8, tile(rcol0),
                  cwblk, vec, wblk, vec, wblk, vec, vec],
        out_specs=(tile(0), tile(0), cwblk, vec, wblk, vec, wblk, vec, vec),
        scratch_shapes=[pltpu.VMEM((tt, LANES), F32), pltpu.VMEM((8, LANES), F32), pltpu.VMEM((8, LANES), F32)],
        args=(proj, proj, proj, hseq, hseq, dcat, cw, cb, wr, br, wi, bi, lam), vmem_mib=32, comm=comm)


ROW_BLOCKS = (512, 256, 176, 128, 64, 32, 16, 8)


def _adamw(w, m, v, gparts, *, name):
    r, c = w.shape
    npart = gparts.shape[0]
    br = _pick(r, ROW_BLOCKS)
    c1 = 1.0 - ADAM_B1 ** ADAM_STEP
    c2 = 1.0 - ADAM_B2 ** ADAM_STEP

    def body(w_ref, m_ref, v_ref, g_ref, go_ref, d_ref, mo_ref, vo_ref):
        g = g_ref[0].astype(F32)
        for q in range(1, npart):
            g = g + g_ref[q].astype(F32)
        mn = ADAM_B1 * m_ref[...] + (1.0 - ADAM_B1) * g
        vn = ADAM_B2 * v_ref[...] + (1.0 - ADAM_B2) * (g * g)
        go_ref[...] = g
        mo_ref[...] = mn
        vo_ref[...] = vn
        d_ref[...] = -ADAM_LR * ((mn / c1) / (jnp.sqrt(vn / c2) + ADAM_EPS) + ADAM_WD * w_ref[...])

    blk = pl.BlockSpec((br, c), lambda i: (i, 0))
    out = jax.ShapeDtypeStruct((r, c), F32)
    return pl.pallas_call(
        body, name=name, out_shape=(out, out, out, out), grid=(r // br,),
        in_specs=[blk, blk, blk, pl.BlockSpec((npart, br, c), lambda i: (0, i, 0))],
        out_specs=(blk, blk, blk, blk),
        compiler_params=_cp(("parallel",), 48),
    )(w, m, v, gparts)


def _sum_parts(parts, *, name):
    npart, r, c = parts.shape
    br = next((b for b in range(min(r, 2048) // 8 * 8, 0, -8) if r % b == 0), r)

    def body(p_ref, o_ref):
        acc = p_ref[0]
        for q in range(1, npart):
            acc = acc + p_ref[q]
        o_ref[...] = acc

    return pl.pallas_call(
        body, name=name, out_shape=jax.ShapeDtypeStruct((r, c), F32), grid=(r // br,),
        in_specs=[pl.BlockSpec((npart, br, c), lambda i: (0, i, 0))],
        out_specs=pl.BlockSpec((br, c), lambda i: (i, 0)),
        compiler_params=_cp(("parallel",), 48),
    )(parts)


HBM = pl.BlockSpec(memory_space=pltpu.HBM)


def _mesh_pos():
    return lax.axis_index("x"), lax.axis_index("y"), lax.axis_index("c")


def _gather_comm(shards):
    na = len(shards)

    def parts(x_refs, out_refs, sems):
        send_sems, recv_sems, local_sems = sems
        x, y, c = _mesh_pos()
        me, sibling = (x, y, c), (x, y, 1 - c)
        chips = [(1 - x, y), (x, 1 - y), (1 - x, 1 - y)]

        def copy(a, k, block, to, src=None):
            px, py, pc = block
            dst = out_refs[a].at[4 * px + 2 * py + pc]
            return pltpu.make_async_remote_copy(
                src_ref=dst if src is None else src, dst_ref=dst,
                send_sem=send_sems.at[a, k], recv_sem=recv_sems.at[a, k],
                device_id=to, device_id_type=MESH)

        def mine(a):
            return pltpu.make_async_copy(x_refs[a], out_refs[a].at[4 * x + 2 * y + c], local_sems.at[a])

        def first(a):
            return [copy(a, 0, me, sibling, src=x_refs[a])] + [
                copy(a, 1 + j, me, (*chip, c), src=x_refs[a]) for j, chip in enumerate(chips)]

        def passed(a, j):
            return copy(a, 4 + j, (*chips[j], c), sibling)

        return me, sibling, chips, c, copy, mine, first, passed

    def start(x_refs, out_refs, sems):
        *_, mine, first, _ = parts(x_refs, out_refs, sems)
        for a in range(na):
            mine(a).start()
            for cp in first(a):
                cp.start()

    def mid(x_refs, out_refs, sems):
        me, _, chips, c, copy, _, _, passed = parts(x_refs, out_refs, sems)
        for j, chip in enumerate(chips):
            for a in range(na):
                copy(a, 1 + j, (*chip, c), me).wait_recv()
                passed(a, j).start()

    def end(x_refs, out_refs, sems):
        me, sibling, chips, c, copy, mine, first, passed = parts(x_refs, out_refs, sems)
        for a in range(na):
            copy(a, 0, sibling, me).wait_recv()
            for j, chip in enumerate(chips):
                copy(a, 4 + j, (*chip, 1 - c), me).wait_recv()
        for a in range(na):
            for cp in first(a) + [passed(a, j) for j in range(3)]:
                cp.wait_send()
            mine(a).wait()

    return _Comm(
        shards, [jax.ShapeDtypeStruct((N_DEV,) + a.shape, a.dtype) for a in shards],
        [pltpu.SemaphoreType.DMA((na, 7)), pltpu.SemaphoreType.DMA((na, 7)), pltpu.SemaphoreType.DMA((na,))],
        start, end, mid)


def _scatter_peer(k, x, y, c):
    px, py, pc = (1 - x if k & 4 else x, 1 - y if k & 2 else y, 1 - c if k & 1 else c)
    return (px, py, pc), 4 * px + 2 * py + pc


def _scatter_start_comm(g8s):
    na = len(g8s)
    arrays = []
    for g8 in g8s:
        arrays += [g8, lax.empty(g8.shape, g8.dtype)]

    def local(refs, sems, a, me_idx):
        return pltpu.make_async_copy(refs[2 * a].at[me_idx], refs[2 * a + 1].at[me_idx], sems[0].at[a])

    def start(refs, outs, sems):
        send_sems, recv_sems = outs[:2]
        x, y, c = _mesh_pos()
        me_idx = 4 * x + 2 * y + c
        for a in range(na):
            local(refs, sems, a, me_idx).start()
            for k in range(1, N_DEV):
                peer, peer_idx = _scatter_peer(k, x, y, c)
                pltpu.make_async_remote_copy(
                    src_ref=refs[2 * a].at[peer_idx], dst_ref=refs[2 * a + 1].at[me_idx],
                    send_sem=send_sems.at[a * (N_DEV - 1) + k - 1], recv_sem=recv_sems.at[a * (N_DEV - 1) + k - 1],
                    device_id=peer, device_id_type=MESH).start()

    def end(refs, outs, sems):
        x, y, c = _mesh_pos()
        for a in range(na):
            local(refs, sems, a, 4 * x + 2 * y + c).wait()

    sem_shape = pltpu.SemaphoreType.DMA((na * (N_DEV - 1),))
    return _Comm(arrays, [sem_shape, sem_shape] + [pltpu.HBM(a.shape, a.dtype) for a in arrays],
                 [pltpu.SemaphoreType.DMA((na,))], start, end, split=True)


def _scatter_wait(started, after, *, name):
    send_sems, recv_sems, *arrays = started
    na = len(arrays) // 2

    def body(*refs):
        send_ref, recv_ref = refs[2 * na], refs[2 * na + 1]
        x, y, c = _mesh_pos()
        for a in range(na):
            for k in range(1, N_DEV):
                peer, peer_idx = _scatter_peer(k, x, y, c)
                pltpu.make_async_remote_copy(
                    src_ref=refs[2 * a].at[peer_idx], dst_ref=refs[2 * a + 1].at[peer_idx],
                    send_sem=send_ref.at[a * (N_DEV - 1) + k - 1], recv_sem=recv_ref.at[a * (N_DEV - 1) + k - 1],
                    device_id=peer, device_id_type=MESH).wait()

    sem = pl.BlockSpec(memory_space=pltpu.SEMAPHORE)
    outs = pl.pallas_call(
        body, name=name, out_shape=tuple(pltpu.HBM(a.shape, a.dtype) for a in arrays),
        in_specs=[HBM] * (2 * na) + [sem, sem, pl.BlockSpec(memory_space=pl.ANY)], out_specs=(HBM,) * (2 * na),
        input_output_aliases={k: k for k in range(2 * na)},
        compiler_params=pltpu.CompilerParams(has_side_effects=pltpu.SideEffectType.DATAFLOW_SIDE_EFFECTING),
    )(*arrays, send_sems, recv_sems, after)
    return outs[1::2]


BIG_WEIGHTS = ("ffn1_w_gate", "ffn1_w_up", "ffn1_w_down", "w_in", "w_out",
               "ffn2_w_gate", "ffn2_w_up", "ffn2_w_down", "w_ple_proj", "w_ple_gate")
COLUMN_SHARDED = ("ffn1_w_gate", "ffn1_w_up", "w_in", "ffn2_w_gate", "ffn2_w_up", "w_ple_proj", "conv_w")
SMALL_WEIGHTS = ("ln1_g", "ln1_b", "conv_b", "w_rgate", "b_rgate", "w_igate", "b_igate", "lru_lambda",
                 "ln2_g", "ln2_b", "ln3_g", "ln3_b")
SMALL_GRADS = SMALL_WEIGHTS + ("conv_w",)


class _Exchange:
    def __init__(self, full):
        self.full = dict(full)
        self.grads = {}

    def __getitem__(self, name):
        return self.full[name]

    def first_gather(self, x):
        return _to_bf16(x, bm=1024, name="x_bf16")

    def gather(self, names):
        return None, None

    def scatter_start(self, names):
        return None, None

    def gather_small(self):
        return None, None


class _MeshExchange(_Exchange):
    def __init__(self, full, shards, conv_w):
        super().__init__(full)
        self.shards = shards
        self.conv_w = conv_w
        self.reduced = {}
        self.started = {}
        self.small_parts = None

    def first_gather(self, x):
        first = ("ffn1_w_gate", "ffn1_w_up")
        xb, (gate, up, conv_all) = _to_bf16(
            x, bm=1024, name="x_bf16", comm=_gather_comm([self.shards[n] for n in first] + [self.conv_w]))
        self.take(first[0], gate)
        self.take(first[1], up)
        self.full["conv_w"] = _to_full("conv_w", conv_all)
        return xb

    def gather(self, names):
        def done(outs):
            for n, o in zip(names, outs):
                self.take(n, o)
        return _gather_comm([self.shards[n] for n in names]), done

    def take(self, name, gathered):
        self.full[name] = gathered.reshape((N_DEV * gathered.shape[1],) + gathered.shape[2:])

    def scatter_start(self, names):
        def done(outs):
            self.started[names] = outs
        return _scatter_start_comm([_to_owner_blocks(n, self.grads[n]) for n in names]), done

    def finish(self, after):
        for names, started in self.started.items():
            self.reduced.update(zip(names, _scatter_wait(started, after, name=f"scatter_wait_{names[0]}")))

    def gather_small(self):
        def done(outs):
            self.small_parts, = outs
        packed = jnp.concatenate([_rows128(self.grads[n]) for n in SMALL_GRADS], axis=0)
        return _gather_comm([packed]), done


def _carried(comm_done, call):
    comm, done = comm_done
    res = call(comm)
    if comm is None:
        return res
    res, outs = res
    done(outs)
    return res


def _dw(a, b, *, scale=1.0, name, comm=None):
    k, m = a.shape
    n = b.shape[1]
    return _mm(a, b, ta=True, scale=scale, out_dtype=BF16, bm=_pick(m, (1024, 512, 256, 128)),
               bn=_pick(n, (512, 256, 128)), bk=k, name=name, comm=comm)


def _ffn_bwd(ex, names, saved, xb_in, dz, dzb, ln_in, tag, on_dwd=None, on_dh=None, on_dwu=None, on_dx=None):
    gate, up, down = names
    g, u, h, _, _ = saved
    f = ex[gate].shape[0]

    def request(fn):
        return (None, None) if fn is None else fn(ex)

    ex.grads[down] = _carried(request(on_dwd), lambda c: _dw(h, dzb, scale=0.5, name=f"{tag}_dwd", comm=c))
    dg, du = _carried(request(on_dh), lambda c: _ffn_bwd_dh(
        dzb, ex[down], g, u, scale=0.5, bm=2048, bn=_pick(f, (512, 256, 128)), name=f"{tag}_dh", chunks=8, comm=c))
    ex.grads[gate] = _dw(xb_in, dg, name=f"{tag}_dwg")
    ex.grads[up] = _carried(request(on_dwu), lambda c: _dw(xb_in, du, name=f"{tag}_dwu", comm=c))
    d = dz.shape[1]
    dx = _carried(request(on_dx), lambda c: _ffn_dx(
        dg, du, ex[gate], ex[up], dz, extra_scale=DEEPNORM_ALPHA,
        bm=1024, bn=_pick(d, (512, 256, 128)), name=f"{tag}_dx", comm=c))
    return dx if ln_in is None else _ln_bwd(dx, *ln_in, bm=256, name=f"{tag}_ln_bwd")


def _local_step(x, p, target, positions, w):
    s, d = x.shape
    tabs = _rope_tables(positions)
    xb = w.first_gather(x)
    f = w["ffn1_w_gate"].shape[0]
    ffn_bn, ln_bn, ln_bn_short_k = _pick(f, (512, 256, 128)), _pick(d, (512, 256, 128)), _pick(d, (1024, 512, 256, 128))
    g1, u1, h1 = _carried(w.gather(("ffn1_w_down", "w_in")), lambda c: _ffn_up(
        xb, w["ffn1_w_gate"], w["ffn1_w_up"], bm=1024, bn=ffn_bn, name="ffn1_up", comm=c))
    x1, x1b, xh1, rs1 = _carried(w.gather(("w_out", "ffn2_w_gate")), lambda c: _mm_ln(
        h1, w["ffn1_w_down"], x, w["ln1_g"], w["ln1_b"], res_scale=DEEPNORM_ALPHA, mm_scale=0.5,
        bm=512, bn=ln_bn, name="ffn1_down_ln", comm=c))
    sv1 = (g1, u1, h1, xh1, rs1)
    pw = w["w_in"].shape[0]
    proj = _carried(w.gather(("ffn2_w_up",)), lambda c: _mm(
        x1b, w["w_in"], tb=True, bm=1024, bn=_pick(pw, (512, 256, 128)), bk=d, name="in_proj", comm=c))
    nqk = (N_PATTERNS + 1) * N_KV_HEADS
    qkr = _rotary(proj, tabs, n_cols=nqk, inverse=False, out_dtype=F32, bs=1024, name="rotary")
    attn, lse = _attn_fwd(qkr, proj, name="attn_fwd")
    lru_w = (w["conv_w"], w["conv_b"], w["w_rgate"], w["b_rgate"], w["w_igate"], w["b_igate"], w["lru_lambda"])
    rec, hseq = _lru_fwd(proj, *lru_w, tt=512, name="lru_fwd")
    cat = jnp.concatenate([attn.astype(BF16), rec], axis=1)
    x2, x2b, xh2, rs2 = _mm_ln(cat, w["w_out"], x1, w["ln2_g"], w["ln2_b"], res_scale=DEEPNORM_ALPHA, mm_scale=1.0,
                               bm=512, bn=ln_bn_short_k, name="out_proj_ln")
    g2, u2, h2 = _carried(w.gather(("ffn2_w_down", "w_ple_gate", "w_ple_proj")), lambda c: _ffn_up(
        x2b, w["ffn2_w_gate"], w["ffn2_w_up"], bm=1024, bn=ffn_bn, name="ffn2_up", comm=c))
    x3, x3b, xh3, rs3 = _mm_ln(h2, w["ffn2_w_down"], x2, w["ln3_g"], w["ln3_b"], res_scale=DEEPNORM_ALPHA, mm_scale=0.5,
                               bm=512, bn=ln_bn, name="ffn2_down_ln")
    sv3 = (g2, u2, h2, xh3, rs3)
    lsum, dy, dgate, dple = _ple_loss(x3, x3b, p, w["w_ple_gate"], w["w_ple_proj"], target,
                                      bm=1024, bn=_pick(d, (512, 256, 128)), name="ple_loss")
    grads = w.grads
    grads["w_ple_gate"] = _dw(x3b, dgate, name="dw_ple_gate")
    grads["w_ple_proj"] = _dw(p, dple, name="dw_ple_proj")
    dz3, dz3b, grads["ln3_g"], grads["ln3_b"] = _mm_dx(
        dgate, w["w_ple_gate"], dy, xh3, rs3, w["ln3_g"], extra_scale=1.0, bm=512, bn=ln_bn_short_k,
        name="ple_dx", tb=True)
    dz2, dz2b, grads["ln2_g"], grads["ln2_b"] = _ffn_bwd(
        w, ("ffn2_w_gate", "ffn2_w_up", "ffn2_w_down"), sv3, x2b, dz3, dz3b, (xh2, rs2, w["ln2_g"]), "ffn2",
        on_dx=lambda ex: ex.scatter_start(("ffn2_w_down", "ffn2_w_gate", "ffn2_w_up", "w_ple_gate", "w_ple_proj")))
    grads["w_out"] = _dw(cat, dz2b, name="dw_out")
    dcat = _mm(dz2b, w["w_out"], tb=True, bm=1024, bn=_pick(d, (512, 256, 128)), bk=d, name="out_proj_dx")
    dq0, dq1, dq2, dk, dvb = _attn_bwd(qkr, proj, attn, lse, dcat, name="attn_bwd")
    nh = N_KV_HEADS
    dqkv = [_rotary(t, tabs, n_cols=nh, inverse=True, out_dtype=BF16, bs=1024, name=f"rotary_bwd{i}")
            for i, t in enumerate((dq0, dq1, dq2, dk))]
    (dxb, dyb, grads["conv_w"], grads["conv_b"], grads["w_rgate"], grads["b_rgate"], grads["w_igate"],
     grads["b_igate"], grads["lru_lambda"]) = _lru_bwd(proj, hseq, dcat, *lru_w, tt=512, name="lru_bwd")
    dproj = jnp.concatenate(dqkv + [dvb, dxb, dyb], axis=1)
    grads["w_in"] = _dw(x1b, dproj, name="dw_in")
    dz1, dz1b, grads["ln1_g"], grads["ln1_b"] = _carried(w.scatter_start(("w_out", "w_in")), lambda c: _mm_dx(
        dproj, w["w_in"], dz2, xh1, rs1, w["ln1_g"], extra_scale=DEEPNORM_ALPHA,
        bm=512, bn=ln_bn, name="in_proj_dx", comm=c))
    grad_x = _ffn_bwd(w, ("ffn1_w_gate", "ffn1_w_up", "ffn1_w_down"), sv1, xb, dz1, dz1b, None, "ffn1",
                      on_dwd=lambda ex: ex.gather_small(),
                      on_dh=lambda ex: ex.scatter_start(("ffn1_w_down",)),
                      on_dwu=lambda ex: ex.scatter_start(("ffn1_w_gate",)),
                      on_dx=lambda ex: ex.scatter_start(("ffn1_w_up",)))
    return lsum, grad_x


def _to_full(name, gathered):
    if name in COLUMN_SHARDED:
        _, r, c = gathered.shape
        return jnp.transpose(gathered, (1, 0, 2)).reshape(r, N_DEV * c)
    return gathered.reshape((N_DEV * gathered.shape[1],) + gathered.shape[2:])


def _to_owner_blocks(name, full):
    if name in COLUMN_SHARDED:
        r, c = full.shape
        return jnp.transpose(full.reshape(r, N_DEV, c // N_DEV), (1, 0, 2))
    return full.reshape((N_DEV, full.shape[0] // N_DEV) + full.shape[1:])


def _rows128(a):
    flat = a.reshape(-1, LANES)
    pad = (-flat.shape[0]) % 8
    return jnp.pad(flat, ((0, pad), (0, 0))) if pad else flat


def kernel(x, p, positions, ffn1_w_gate, ffn1_w_up, ffn1_w_down, ln1_g, ln1_b, w_in, conv_w, conv_b, w_rgate, b_rgate, w_igate, b_igate, lru_lambda, w_out, ln2_g, ln2_b, ffn2_w_gate, ffn2_w_up, ffn2_w_down, ln3_g, ln3_b, w_ple_proj, w_ple_gate, loss_target, m_ffn1_w_gate, m_ffn1_w_up, m_ffn1_w_down, m_ln1_g, m_ln1_b, m_w_in, m_conv_w, m_conv_b, m_w_rgate, m_b_rgate, m_w_igate, m_b_igate, m_lru_lambda, m_w_out, m_ln2_g, m_ln2_b, m_ffn2_w_gate, m_ffn2_w_up, m_ffn2_w_down, m_ln3_g, m_ln3_b, m_w_ple_proj, m_w_ple_gate, v_ffn1_w_gate, v_ffn1_w_up, v_ffn1_w_down, v_ln1_g, v_ln1_b, v_w_in, v_conv_w, v_conv_b, v_w_rgate, v_b_rgate, v_w_igate, v_b_igate, v_lru_lambda, v_w_out, v_ln2_g, v_ln2_b, v_ffn2_w_gate, v_ffn2_w_up, v_ffn2_w_down, v_ln3_g, v_ln3_b, v_w_ple_proj, v_w_ple_gate):
    names = ("ffn1_w_gate", "ffn1_w_up", "ffn1_w_down", "ln1_g", "ln1_b", "w_in", "conv_w", "conv_b", "w_rgate",
             "b_rgate", "w_igate", "b_igate", "lru_lambda", "w_out", "ln2_g", "ln2_b", "ffn2_w_gate", "ffn2_w_up",
             "ffn2_w_down", "ln3_g", "ln3_b", "w_ple_proj", "w_ple_gate")
    ws = (ffn1_w_gate, ffn1_w_up, ffn1_w_down, ln1_g, ln1_b, w_in, conv_w, conv_b, w_rgate, b_rgate, w_igate, b_igate,
          lru_lambda, w_out, ln2_g, ln2_b, ffn2_w_gate, ffn2_w_up, ffn2_w_down, ln3_g, ln3_b, w_ple_proj, w_ple_gate)
    ms = (m_ffn1_w_gate, m_ffn1_w_up, m_ffn1_w_down, m_ln1_g, m_ln1_b, m_w_in, m_conv_w, m_conv_b, m_w_rgate, m_b_rgate,
          m_w_igate, m_b_igate, m_lru_lambda, m_w_out, m_ln2_g, m_ln2_b, m_ffn2_w_gate, m_ffn2_w_up, m_ffn2_w_down,
          m_ln3_g, m_ln3_b, m_w_ple_proj, m_w_ple_gate)
    vs = (v_ffn1_w_gate, v_ffn1_w_up, v_ffn1_w_down, v_ln1_g, v_ln1_b, v_w_in, v_conv_w, v_conv_b, v_w_rgate, v_b_rgate,
          v_w_igate, v_b_igate, v_lru_lambda, v_w_out, v_ln2_g, v_ln2_b, v_ffn2_w_gate, v_ffn2_w_up, v_ffn2_w_down,
          v_ln3_g, v_ln3_b, v_w_ple_proj, v_w_ple_gate)
    def local(a):
        return a[0] if a.ndim >= 3 else a

    w_loc = {n: local(a) for n, a in zip(names, ws)}
    m_loc = {n: local(a) for n, a in zip(names, ms)}
    v_loc = {n: local(a) for n, a in zip(names, vs)}
    out_shapes = {n: a.shape for n, a in zip(names, ws)}

    shards = {n: (w_loc[n].T if n in COLUMN_SHARDED else w_loc[n]).astype(BF16) for n in BIG_WEIGHTS}
    ex = _MeshExchange({n: w_loc[n] for n in SMALL_WEIGHTS}, shards, w_loc["conv_w"])

    lsum, grad_x = _local_step(x[0], p[0, 0], loss_target[0], positions[0], ex)
    ex.finish(grad_x)
    grads, reduced = ex.grads, ex.reduced
    d_model = x.shape[-1]
    loss = lax.psum(lsum[0, 0] * (0.5 / d_model), ("x", "y", "c"))

    small = SMALL_GRADS
    summed = _sum_parts(ex.small_parts, name="sum_small_grads")
    small_grads, row = {}, 0
    for n in small:
        rows = grads[n].size // LANES
        small_grads[n] = summed[row:row + rows].reshape(grads[n].shape)
        row += rows + (-rows) % 8
    me = 4 * lax.axis_index("x") + 2 * lax.axis_index("y") + lax.axis_index("c")
    cw_cols = w_loc["conv_w"].shape[1]
    small_grads["conv_w"] = lax.dynamic_slice_in_dim(small_grads["conv_w"], me * cw_cols, cw_cols, axis=1)

    out_g, out_d, out_m, out_v = {}, {}, {}, {}
    for n in names:
        wl, ml, vl = w_loc[n], m_loc[n], v_loc[n]
        shape = wl.shape
        if n in BIG_WEIGHTS:
            gparts = reduced[n]
        else:
            gparts = small_grads[n].reshape((1,) + shape)
        if wl.ndim == 3:
            wl, ml, vl = (t.reshape(-1, shape[-1]) for t in (wl, ml, vl))
            gparts = gparts.reshape(gparts.shape[0], -1, shape[-1])
        res = _adamw(wl, ml, vl, gparts, name=f"adamw_{n}")
        out_g[n], out_d[n], out_m[n], out_v[n] = (t.reshape(out_shapes[n]) for t in res)

    return (loss, grad_x[None], *[out_g[n] for n in names], *[out_d[n] for n in names],
            *[out_m[n] for n in names], *[out_v[n] for n in names])
```
